```python
import math
import jax, jax.numpy as jnp
from jax import lax
import numpy as np

D_MODEL = 1024
BATCH = 8
SEQ = 4096
DEPTH = 1

PLE_DIM = 256
HEAD_DIM = 64
A_Q_HEADS = 8
A_KV_HEADS = 2
A_GROUP = A_Q_HEADS // A_KV_HEADS
A_WINDOW = 128
B_HEADS = 8
B_PATTERNS = ((128, 1), (512, 4), (2048, 16))
N_HEADS_TOTAL = A_Q_HEADS + B_HEADS
A_Q = A_Q_HEADS * HEAD_DIM
A_KV = A_KV_HEADS * HEAD_DIM
B_W = B_HEADS * HEAD_DIM
D_IN = A_Q + 2 * A_KV + 3 * B_W
D_MIX = A_Q + B_W
D_FF = 2816
NUM_BUCKETS = 32
MAX_DISTANCE = 2048
BLOCK = 128
EPS = 1e-6
NEG_INF = -1e30

kernel_name = "hymba_swa_sink_dilated_macaron_layer"


def rms_norm(x, g):
    xf = x.astype(jnp.float32)
    y = xf * lax.rsqrt(jnp.mean(xf * xf, axis=-1, keepdims=True) + EPS)
    return (y * g.astype(jnp.float32)).astype(x.dtype)


def swiglu(x, w_gu, w_down):
    g, u = jnp.split(x @ w_gu, 2, axis=-1)
    return (jax.nn.silu(g) * u) @ w_down


def t5_bucket(dist):
    max_exact = NUM_BUCKETS // 2
    n = jnp.maximum(dist, 0)
    nf = jnp.maximum(n, 1).astype(jnp.float32)
    large = max_exact + (jnp.log(nf / max_exact) / math.log(MAX_DISTANCE / max_exact)
                         * (NUM_BUCKETS - max_exact)).astype(jnp.int32)
    large = jnp.minimum(large, NUM_BUCKETS - 1)
    return jnp.where(n < max_exact, n, large)


def banded_attention(q, k, v, rel_bias, max_dist, stride):
    n, L, hkv, grp, dh = q.shape
    bq = math.gcd(L, BLOCK)
    nb = L // bq
    nk = bq + max_dist
    pad = ((0, 0), (max_dist, 0), (0, 0), (0, 0))
    k_pad = jnp.pad(k, pad)
    v_pad = jnp.pad(v, pad)
    key_idx = jnp.arange(nb)[:, None] * bq + jnp.arange(nk)[None, :]
    kb = k_pad[:, key_idx]
    vb = v_pad[:, key_idx]
    qb = q.reshape(n, nb, bq, hkv, grp, dh)
    logits = jnp.einsum('nbqhgd,nbkhd->nbhgqk', qb, kb,
                        preferred_element_type=jnp.float32) * (dh ** -0.5)
    rel = jnp.arange(bq)[:, None] + max_dist - jnp.arange(nk)[None, :]
    bias = rel_bias[t5_bucket(rel * stride)].astype(jnp.float32)
    bias = bias.reshape(bq, nk, hkv, grp).transpose(2, 3, 0, 1)
    in_band = (rel >= 0) & (rel <= max_dist)
    key_pos = key_idx - max_dist
    valid = in_band[None] & (key_pos >= 0)[:, None, :]
    logits = jnp.where(valid[None, :, None, None], logits + bias, NEG_INF)
    m = jnp.max(logits, axis=-1)
    pr = jnp.exp(logits - m[..., None])
    s = jnp.sum(pr, axis=-1)
    o = jnp.einsum('nbhgqk,nbkhd->nbqhgd', pr, vb.astype(jnp.float32))
    o = o.reshape(n, L, hkv, grp, dh)
    m = m.transpose(0, 1, 4, 2, 3).reshape(n, L, hkv, grp)
    s = s.transpose(0, 1, 4, 2, 3).reshape(n, L, hkv, grp)
    return o, m, s


def to_classes(t, r):
    b, s = t.shape[:2]
    rest = t.shape[2:]
    t = jnp.moveaxis(t.reshape((b, s // r, r) + rest), 2, 1)
    return t.reshape((b * r, s // r) + rest)


def from_classes(t, b, r):
    n, L = t.shape[:2]
    rest = t.shape[2:]
    t = jnp.moveaxis(t.reshape((b, r, L) + rest), 1, 2)
    return t.reshape((b, L * r) + rest)


def sink_swa_gqa(q_a, k_a, v_a, sinks, rel_bias_a):
    b, s, _ = q_a.shape
    q = q_a.reshape(b, s, A_KV_HEADS, A_GROUP, HEAD_DIM)
    k = k_a.reshape(b, s, A_KV_HEADS, HEAD_DIM)
    v = v_a.reshape(b, s, A_KV_HEADS, HEAD_DIM)
    o, m, den = banded_attention(q, k, v, rel_bias_a, A_WINDOW - 1, 1)
    sink = sinks.reshape(A_KV_HEADS, A_GROUP).astype(jnp.float32)
    m_all = jnp.maximum(m, sink)
    scale = jnp.exp(m - m_all)
    total = den * scale + jnp.exp(sink - m_all)
    o = o * (scale / total)[..., None]
    return o.reshape(b, s, A_Q)


def dilated_mixture(q_b, k_b, v_b, rel_bias_b):
    b, s, _ = q_b.shape
    q = q_b.reshape(b, s, B_HEADS, 1, HEAD_DIM)
    k = k_b.reshape(b, s, B_HEADS, HEAD_DIM)
    v = v_b.reshape(b, s, B_HEADS, HEAD_DIM)
    outs, maxes, dens = [], [], []
    for window, dil in B_PATTERNS:
        o, m, den = banded_attention(to_classes(q, dil), to_classes(k, dil), to_classes(v, dil),
                                     rel_bias_b, window // dil, dil)
        outs.append(from_classes(o, b, dil))
        maxes.append(from_classes(m, b, dil))
        dens.append(from_classes(den, b, dil))
    m_stack = jnp.stack(maxes)
    m_all = jnp.max(m_stack, axis=0)
    w = jnp.exp(m_stack - m_all)
    num = jnp.sum(w[..., None] * jnp.stack(outs), axis=0)
    den_all = jnp.sum(w * jnp.stack(dens), axis=0)
    return (num / den_all[..., None]).reshape(b, s, B_W)


def _fwd_setup_inputs(seed: int = 0) -> dict:
    key = jax.random.key(seed)
    ks = jax.random.split(key, 24)
    f32 = jnp.float32

    def w(k, shape, fan_in):
        return jax.random.normal(k, shape, f32) * (fan_in ** -0.5)

    def gain(k, shape):
        return 1.0 + 0.05 * jax.random.normal(k, shape, f32)

    def small(k, shape, scale=0.02):
        return scale * jax.random.normal(k, shape, f32)

    D = D_MODEL
    return {
        "x": jax.random.normal(ks[0], (BATCH, SEQ, D), f32),
        "p": jax.random.normal(ks[1], (DEPTH, BATCH, SEQ, PLE_DIM), f32),
        "rel_bias": small(ks[2], (NUM_BUCKETS, N_HEADS_TOTAL), 0.5),
        "ffn1_pre_g": gain(ks[3], (DEPTH, D)),
        "ffn1_w_gu": w(ks[4], (DEPTH, D, 2 * D_FF), D),
        "ffn1_w_down": w(ks[5], (DEPTH, D_FF, D), D_FF),
        "ffn1_post_g": gain(ks[6], (DEPTH, D)),
        "attn_pre_g": gain(ks[7], (DEPTH, D)),
        "w_in": w(ks[8], (DEPTH, D, D_IN), D),
        "b_in": small(ks[9], (DEPTH, D_IN)),
        "sinks": small(ks[10], (DEPTH, A_Q_HEADS), 0.5),
        "w_out": w(ks[11], (DEPTH, D_MIX, D), D_MIX),
        "b_out": small(ks[12], (DEPTH, D)),
        "attn_post_g": gain(ks[13], (DEPTH, D)),
        "ffn2_pre_g": gain(ks[14], (DEPTH, D)),
        "ffn2_w_gu": w(ks[15], (DEPTH, D, 2 * D_FF), D),
        "ffn2_w_down": w(ks[16], (DEPTH, D_FF, D), D_FF),
        "ffn2_post_g": gain(ks[17], (DEPTH, D)),
        "ple_pre_g": gain(ks[18], (DEPTH, D)),
        "w_ple_gate": w(ks[19], (DEPTH, D, D), D),
        "w_ple_proj": w(ks[20], (DEPTH, PLE_DIM, D), PLE_DIM),
        "ple_post_g": gain(ks[21], (DEPTH, D)),
    }


def _fwd_reference(x, p, rel_bias, ffn1_pre_g, ffn1_w_gu, ffn1_w_down, ffn1_post_g,
              attn_pre_g, w_in, b_in, sinks, w_out, b_out, attn_post_g,
              ffn2_pre_g, ffn2_w_gu, ffn2_w_down, ffn2_post_g,
              ple_pre_g, w_ple_gate, w_ple_proj, ple_post_g):
    h = x
    splits = [A_Q, A_Q + A_KV, A_Q + 2 * A_KV, A_Q + 2 * A_KV + B_W, A_Q + 2 * A_KV + 2 * B_W]
    for i in range(DEPTH):
        f = swiglu(rms_norm(h, ffn1_pre_g[i]), ffn1_w_gu[i], ffn1_w_down[i])
        h = h + 0.5 * rms_norm(f, ffn1_post_g[i])

        z = rms_norm(h, attn_pre_g[i]) @ w_in[i] + b_in[i]
        q_a, k_a, v_a, q_b, k_b, v_b = jnp.split(z, splits, axis=-1)
        out_a = sink_swa_gqa(q_a, k_a, v_a, sinks[i], rel_bias[:, :A_Q_HEADS])
        out_b = dilated_mixture(q_b, k_b, v_b, rel_bias[:, A_Q_HEADS:])
        mix = jnp.concatenate([out_a, out_b], axis=-1).astype(x.dtype)
        att = mix @ w_out[i] + b_out[i]
        h = h + rms_norm(att, attn_post_g[i])

        f = swiglu(rms_norm(h, ffn2_pre_g[i]), ffn2_w_gu[i], ffn2_w_down[i])
        h = h + 0.5 * rms_norm(f, ffn2_post_g[i])

        gate = jax.nn.sigmoid(rms_norm(h, ple_pre_g[i]) @ w_ple_gate[i])
        e = p[i] @ w_ple_proj[i]
        h = h + rms_norm(gate * e, ple_post_g[i])
    return h


import jax as _jax
import jax.numpy as _jnp

TWIN_FORMAT = 'train_step'
FWD_PARAMS = ['x', 'p', 'rel_bias', 'ffn1_pre_g', 'ffn1_w_gu', 'ffn1_w_down', 'ffn1_post_g', 'attn_pre_g', 'w_in', 'b_in', 'sinks', 'w_out', 'b_out', 'attn_post_g', 'ffn2_pre_g', 'ffn2_w_gu', 'ffn2_w_down', 'ffn2_post_g', 'ple_pre_g', 'w_ple_gate', 'w_ple_proj', 'ple_post_g']
TWIN_WEIGHTS = ['rel_bias', 'ffn1_pre_g', 'ffn1_w_gu', 'ffn1_w_down', 'ffn1_post_g', 'attn_pre_g', 'w_in', 'b_in', 'sinks', 'w_out', 'b_out', 'attn_post_g', 'ffn2_pre_g', 'ffn2_w_gu', 'ffn2_w_down', 'ffn2_post_g', 'ple_pre_g', 'w_ple_gate', 'w_ple_proj', 'ple_post_g']
TWIN_DIFF_INPUT = 'x'
TWIN_INPUTS = ['x', 'p', 'rel_bias', 'ffn1_pre_g', 'ffn1_w_gu', 'ffn1_w_down', 'ffn1_post_g', 'attn_pre_g', 'w_in', 'b_in', 'sinks', 'w_out', 'b_out', 'attn_post_g', 'ffn2_pre_g', 'ffn2_w_gu', 'ffn2_w_down', 'ffn2_post_g', 'ple_pre_g', 'w_ple_gate', 'w_ple_proj', 'ple_post_g', 'loss_target', 'm_rel_bias', 'm_ffn1_pre_g', 'm_ffn1_w_gu', 'm_ffn1_w_down', 'm_ffn1_post_g', 'm_attn_pre_g', 'm_w_in', 'm_b_in', 'm_sinks', 'm_w_out', 'm_b_out', 'm_attn_post_g', 'm_ffn2_pre_g', 'm_ffn2_w_gu', 'm_ffn2_w_down', 'm_ffn2_post_g', 'm_ple_pre_g', 'm_w_ple_gate', 'm_w_ple_proj', 'm_ple_post_g', 'v_rel_bias', 'v_ffn1_pre_g', 'v_ffn1_w_gu', 'v_ffn1_w_down', 'v_ffn1_post_g', 'v_attn_pre_g', 'v_w_in', 'v_b_in', 'v_sinks', 'v_w_out', 'v_b_out', 'v_attn_post_g', 'v_ffn2_pre_g', 'v_ffn2_w_gu', 'v_ffn2_w_down', 'v_ffn2_post_g', 'v_ple_pre_g', 'v_w_ple_gate', 'v_w_ple_proj', 'v_ple_post_g']
TWIN_OUTPUTS = ['loss', 'grad_x', 'grad_rel_bias', 'grad_ffn1_pre_g', 'grad_ffn1_w_gu', 'grad_ffn1_w_down', 'grad_ffn1_post_g', 'grad_attn_pre_g', 'grad_w_in', 'grad_b_in', 'grad_sinks', 'grad_w_out', 'grad_b_out', 'grad_attn_post_g', 'grad_ffn2_pre_g', 'grad_ffn2_w_gu', 'grad_ffn2_w_down', 'grad_ffn2_post_g', 'grad_ple_pre_g', 'grad_w_ple_gate', 'grad_w_ple_proj', 'grad_ple_post_g', 'delta_rel_bias', 'delta_ffn1_pre_g', 'delta_ffn1_w_gu', 'delta_ffn1_w_down', 'delta_ffn1_post_g', 'delta_attn_pre_g', 'delta_w_in', 'delta_b_in', 'delta_sinks', 'delta_w_out', 'delta_b_out', 'delta_attn_post_g', 'delta_ffn2_pre_g', 'delta_ffn2_w_gu', 'delta_ffn2_w_down', 'delta_ffn2_post_g', 'delta_ple_pre_g', 'delta_w_ple_gate', 'delta_w_ple_proj', 'delta_ple_post_g', 'new_m_rel_bias', 'new_m_ffn1_pre_g', 'new_m_ffn1_w_gu', 'new_m_ffn1_w_down', 'new_m_ffn1_post_g', 'new_m_attn_pre_g', 'new_m_w_in', 'new_m_b_in', 'new_m_sinks', 'new_m_w_out', 'new_m_b_out', 'new_m_attn_post_g', 'new_m_ffn2_pre_g', 'new_m_ffn2_w_gu', 'new_m_ffn2_w_down', 'new_m_ffn2_post_g', 'new_m_ple_pre_g', 'new_m_w_ple_gate', 'new_m_w_ple_proj', 'new_m_ple_post_g', 'new_v_rel_bias', 'new_v_ffn1_pre_g', 'new_v_ffn1_w_gu', 'new_v_ffn1_w_down', 'new_v_ffn1_post_g', 'new_v_attn_pre_g', 'new_v_w_in', 'new_v_b_in', 'new_v_sinks', 'new_v_w_out', 'new_v_b_out', 'new_v_attn_post_g', 'new_v_ffn2_pre_g', 'new_v_ffn2_w_gu', 'new_v_ffn2_w_down', 'new_v_ffn2_post_g', 'new_v_ple_pre_g', 'new_v_w_ple_gate', 'new_v_w_ple_proj', 'new_v_ple_post_g']
TWIN_LEAF_KINDS = {'loss': 'loss', 'grad_x': 'grad_x', 'grad_rel_bias': 'grad_w', 'grad_ffn1_pre_g': 'grad_w', 'grad_ffn1_w_gu': 'grad_w', 'grad_ffn1_w_down': 'grad_w', 'grad_ffn1_post_g': 'grad_w', 'grad_attn_pre_g': 'grad_w', 'grad_w_in': 'grad_w', 'grad_b_in': 'grad_w', 'grad_sinks': 'grad_w', 'grad_w_out': 'grad_w', 'grad_b_out': 'grad_w', 'grad_attn_post_g': 'grad_w', 'grad_ffn2_pre_g': 'grad_w', 'grad_ffn2_w_gu': 'grad_w', 'grad_ffn2_w_down': 'grad_w', 'grad_ffn2_post_g': 'grad_w', 'grad_ple_pre_g': 'grad_w', 'grad_w_ple_gate': 'grad_w', 'grad_w_ple_proj': 'grad_w', 'grad_ple_post_g': 'grad_w', 'delta_rel_bias': 'delta_w', 'delta_ffn1_pre_g': 'delta_w', 'delta_ffn1_w_gu': 'delta_w', 'delta_ffn1_w_down': 'delta_w', 'delta_ffn1_post_g': 'delta_w', 'delta_attn_pre_g': 'delta_w', 'delta_w_in': 'delta_w', 'delta_b_in': 'delta_w', 'delta_sinks': 'delta_w', 'delta_w_out': 'delta_w', 'delta_b_out': 'delta_w', 'delta_attn_post_g': 'delta_w', 'delta_ffn2_pre_g': 'delta_w', 'delta_ffn2_w_gu': 'delta_w', 'delta_ffn2_w_down': 'delta_w', 'delta_ffn2_post_g': 'delta_w', 'delta_ple_pre_g': 'delta_w', 'delta_w_ple_gate': 'delta_w', 'delta_w_ple_proj': 'delta_w', 'delta_ple_post_g': 'delta_w', 'new_m_rel_bias': 'new_m', 'new_m_ffn1_pre_g': 'new_m', 'new_m_ffn1_w_gu': 'new_m', 'new_m_ffn1_w_down': 'new_m', 'new_m_ffn1_post_g': 'new_m', 'new_m_attn_pre_g': 'new_m', 'new_m_w_in': 'new_m', 'new_m_b_in': 'new_m', 'new_m_sinks': 'new_m', 'new_m_w_out': 'new_m', 'new_m_b_out': 'new_m', 'new_m_attn_post_g': 'new_m', 'new_m_ffn2_pre_g': 'new_m', 'new_m_ffn2_w_gu': 'new_m', 'new_m_ffn2_w_down': 'new_m', 'new_m_ffn2_post_g': 'new_m', 'new_m_ple_pre_g': 'new_m', 'new_m_w_ple_gate': 'new_m', 'new_m_w_ple_proj': 'new_m', 'new_m_ple_post_g': 'new_m', 'new_v_rel_bias': 'new_v', 'new_v_ffn1_pre_g': 'new_v', 'new_v_ffn1_w_gu': 'new_v', 'new_v_ffn1_w_down': 'new_v', 'new_v_ffn1_post_g': 'new_v', 'new_v_attn_pre_g': 'new_v', 'new_v_w_in': 'new_v', 'new_v_b_in': 'new_v', 'new_v_sinks': 'new_v', 'new_v_w_out': 'new_v', 'new_v_b_out': 'new_v', 'new_v_attn_post_g': 'new_v', 'new_v_ffn2_pre_g': 'new_v', 'new_v_ffn2_w_gu': 'new_v', 'new_v_ffn2_w_down': 'new_v', 'new_v_ffn2_post_g': 'new_v', 'new_v_ple_pre_g': 'new_v', 'new_v_w_ple_gate': 'new_v', 'new_v_w_ple_proj': 'new_v', 'new_v_ple_post_g': 'new_v'}


def _forward(args):
    return _fwd_reference(*[args[k] for k in FWD_PARAMS])


def _output_shape():
    out = _jax.eval_shape(lambda: _forward(_fwd_setup_inputs(0)))
    return out.shape, out.dtype

N_MICROBATCH = 1
ADAM_LR = 0.001
ADAM_B1 = 0.9
ADAM_B2 = 0.999
ADAM_EPS = 1e-08
ADAM_WD = 0.01
ADAM_STEP = 10
PER_EXAMPLE_BATCH_AXIS = {'x': 0, 'p': 1, 'loss_target': 0}
SHARED_INPUTS = []
_WEIGHT_DTYPES = {'rel_bias': _jnp.float32, 'ffn1_pre_g': _jnp.float32, 'ffn1_w_gu': _jnp.float32, 'ffn1_w_down': _jnp.float32, 'ffn1_post_g': _jnp.float32, 'attn_pre_g': _jnp.float32, 'w_in': _jnp.float32, 'b_in': _jnp.float32, 'sinks': _jnp.float32, 'w_out': _jnp.float32, 'b_out': _jnp.float32, 'attn_post_g': _jnp.float32, 'ffn2_pre_g': _jnp.float32, 'ffn2_w_gu': _jnp.float32, 'ffn2_w_down': _jnp.float32, 'ffn2_post_g': _jnp.float32, 'ple_pre_g': _jnp.float32, 'w_ple_gate': _jnp.float32, 'w_ple_proj': _jnp.float32, 'ple_post_g': _jnp.float32}
MOMENT_SCALE = {'rel_bias': 4.720895e-01, 'ffn1_pre_g': 5.261369e-01, 'ffn1_w_gu': 2.319955e-01, 'ffn1_w_down': 3.870906e-01, 'ffn1_post_g': 7.786421e+00, 'attn_pre_g': 6.915131e-01, 'w_in': 4.733044e-01, 'b_in': 7.565430e+00, 'sinks': 9.171674e-02, 'w_out': 4.783326e-01, 'b_out': 1.150221e+01, 'attn_post_g': 3.208784e+01, 'ffn2_pre_g': 3.829443e-01, 'ffn2_w_gu': 1.613101e-01, 'ffn2_w_down': 3.032597e-01, 'ffn2_post_g': 8.043658e+00, 'ple_pre_g': 1.536284e-01, 'w_ple_gate': 1.555846e-01, 'w_ple_proj': 3.493765e-01, 'ple_post_g': 3.222236e+01}


def _to_microbatches(a, axis):
    t = _jnp.moveaxis(a, axis, 0)
    t = t.reshape((N_MICROBATCH, t.shape[0] // N_MICROBATCH) + t.shape[1:])
    return _jnp.moveaxis(t, 1, axis + 1)


def setup_inputs(seed: int = 0) -> dict:
    inp = _fwd_setup_inputs(seed)
    key = _jax.random.fold_in(_jax.random.key(seed), 7919)
    shape, _ = _output_shape()
    out = dict(inp)
    out["loss_target"] = _jax.random.normal(_jax.random.fold_in(key, 0), shape, _jnp.float32)
    for i, name in enumerate(TWIN_WEIGHTS):
        w = inp[name].astype(_jnp.float32)
        if MOMENT_SCALE is None:
            s = _jnp.sqrt(_jnp.mean(_jnp.square(w)) + 1e-30)
        else:
            s = MOMENT_SCALE[name]
        km, kv = _jax.random.split(_jax.random.fold_in(key, i + 1))
        out[name] = w
        out["m_" + name] = s * _jax.random.normal(km, w.shape, _jnp.float32)
        out["v_" + name] = (s * s) * _jax.random.uniform(kv, w.shape, _jnp.float32, 0.5, 1.5)
    if N_MICROBATCH > 1:
        for name, axis in PER_EXAMPLE_BATCH_AXIS.items():
            out[name] = _to_microbatches(out[name], axis)
    return {'x': out['x'], 'p': out['p'], 'rel_bias': out['rel_bias'], 'ffn1_pre_g': out['ffn1_pre_g'], 'ffn1_w_gu': out['ffn1_w_gu'], 'ffn1_w_down': out['ffn1_w_down'], 'ffn1_post_g': out['ffn1_post_g'], 'attn_pre_g': out['attn_pre_g'], 'w_in': out['w_in'], 'b_in': out['b_in'], 'sinks': out['sinks'], 'w_out': out['w_out'], 'b_out': out['b_out'], 'attn_post_g': out['attn_post_g'], 'ffn2_pre_g': out['ffn2_pre_g'], 'ffn2_w_gu': out['ffn2_w_gu'], 'ffn2_w_down': out['ffn2_w_down'], 'ffn2_post_g': out['ffn2_post_g'], 'ple_pre_g': out['ple_pre_g'], 'w_ple_gate': out['w_ple_gate'], 'w_ple_proj': out['w_ple_proj'], 'ple_post_g': out['ple_post_g'], 'loss_target': out['loss_target'], 'm_rel_bias': out['m_rel_bias'], 'm_ffn1_pre_g': out['m_ffn1_pre_g'], 'm_ffn1_w_gu': out['m_ffn1_w_gu'], 'm_ffn1_w_down': out['m_ffn1_w_down'], 'm_ffn1_post_g': out['m_ffn1_post_g'], 'm_attn_pre_g': out['m_attn_pre_g'], 'm_w_in': out['m_w_in'], 'm_b_in': out['m_b_in'], 'm_sinks': out['m_sinks'], 'm_w_out': out['m_w_out'], 'm_b_out': out['m_b_out'], 'm_attn_post_g': out['m_attn_post_g'], 'm_ffn2_pre_g': out['m_ffn2_pre_g'], 'm_ffn2_w_gu': out['m_ffn2_w_gu'], 'm_ffn2_w_down': out['m_ffn2_w_down'], 'm_ffn2_post_g': out['m_ffn2_post_g'], 'm_ple_pre_g': out['m_ple_pre_g'], 'm_w_ple_gate': out['m_w_ple_gate'], 'm_w_ple_proj': out['m_w_ple_proj'], 'm_ple_post_g': out['m_ple_post_g'], 'v_rel_bias': out['v_rel_bias'], 'v_ffn1_pre_g': out['v_ffn1_pre_g'], 'v_ffn1_w_gu': out['v_ffn1_w_gu'], 'v_ffn1_w_down': out['v_ffn1_w_down'], 'v_ffn1_post_g': out['v_ffn1_post_g'], 'v_attn_pre_g': out['v_attn_pre_g'], 'v_w_in': out['v_w_in'], 'v_b_in': out['v_b_in'], 'v_sinks': out['v_sinks'], 'v_w_out': out['v_w_out'], 'v_b_out': out['v_b_out'], 'v_attn_post_g': out['v_attn_post_g'], 'v_ffn2_pre_g': out['v_ffn2_pre_g'], 'v_ffn2_w_gu': out['v_ffn2_w_gu'], 'v_ffn2_w_down': out['v_ffn2_w_down'], 'v_ffn2_post_g': out['v_ffn2_post_g'], 'v_ple_pre_g': out['v_ple_pre_g'], 'v_w_ple_gate': out['v_w_ple_gate'], 'v_w_ple_proj': out['v_w_ple_proj'], 'v_ple_post_g': out['v_ple_post_g']}


def _loss(weights, diff, rest, loss_target):
    with _jax.named_scope("forward"):
        args = {**rest, TWIN_DIFF_INPUT: diff, **{k: w.astype(_WEIGHT_DTYPES[k]) for k, w in weights.items()}}
        y = _forward(args)
    with _jax.named_scope("loss_head"):
        err = _jnp.square(y.astype(_jnp.float32) - loss_target)
        return 0.5 * _jnp.sum(_jnp.mean(err, axis=-1)) if err.ndim else 0.5 * err


def _adamw(w, g, m, v):
    m = ADAM_B1 * m + (1.0 - ADAM_B1) * g
    v = ADAM_B2 * v + (1.0 - ADAM_B2) * _jnp.square(g)
    m_hat = m / (1.0 - ADAM_B1 ** ADAM_STEP)
    v_hat = v / (1.0 - ADAM_B2 ** ADAM_STEP)
    delta = -ADAM_LR * (m_hat / (_jnp.sqrt(v_hat) + ADAM_EPS) + ADAM_WD * w)
    return delta, m, v


def reference(x, p, rel_bias, ffn1_pre_g, ffn1_w_gu, ffn1_w_down, ffn1_post_g, attn_pre_g, w_in, b_in, sinks, w_out, b_out, attn_post_g, ffn2_pre_g, ffn2_w_gu, ffn2_w_down, ffn2_post_g, ple_pre_g, w_ple_gate, w_ple_proj, ple_post_g, loss_target, m_rel_bias, m_ffn1_pre_g, m_ffn1_w_gu, m_ffn1_w_down, m_ffn1_post_g, m_attn_pre_g, m_w_in, m_b_in, m_sinks, m_w_out, m_b_out, m_attn_post_g, m_ffn2_pre_g, m_ffn2_w_gu, m_ffn2_w_down, m_ffn2_post_g, m_ple_pre_g, m_w_ple_gate, m_w_ple_proj, m_ple_post_g, v_rel_bias, v_ffn1_pre_g, v_ffn1_w_gu, v_ffn1_w_down, v_ffn1_post_g, v_attn_pre_g, v_w_in, v_b_in, v_sinks, v_w_out, v_b_out, v_attn_post_g, v_ffn2_pre_g, v_ffn2_w_gu, v_ffn2_w_down, v_ffn2_post_g, v_ple_pre_g, v_w_ple_gate, v_w_ple_proj, v_ple_post_g):
    given = dict(x=x, p=p, rel_bias=rel_bias, ffn1_pre_g=ffn1_pre_g, ffn1_w_gu=ffn1_w_gu, ffn1_w_down=ffn1_w_down, ffn1_post_g=ffn1_post_g, attn_pre_g=attn_pre_g, w_in=w_in, b_in=b_in, sinks=sinks, w_out=w_out, b_out=b_out, attn_post_g=attn_post_g, ffn2_pre_g=ffn2_pre_g, ffn2_w_gu=ffn2_w_gu, ffn2_w_down=ffn2_w_down, ffn2_post_g=ffn2_post_g, ple_pre_g=ple_pre_g, w_ple_gate=w_ple_gate, w_ple_proj=w_ple_proj, ple_post_g=ple_post_g, loss_target=loss_target, m_rel_bias=m_rel_bias, m_ffn1_pre_g=m_ffn1_pre_g, m_ffn1_w_gu=m_ffn1_w_gu, m_ffn1_w_down=m_ffn1_w_down, m_ffn1_post_g=m_ffn1_post_g, m_attn_pre_g=m_attn_pre_g, m_w_in=m_w_in, m_b_in=m_b_in, m_sinks=m_sinks, m_w_out=m_w_out, m_b_out=m_b_out, m_attn_post_g=m_attn_post_g, m_ffn2_pre_g=m_ffn2_pre_g, m_ffn2_w_gu=m_ffn2_w_gu, m_ffn2_w_down=m_ffn2_w_down, m_ffn2_post_g=m_ffn2_post_g, m_ple_pre_g=m_ple_pre_g, m_w_ple_gate=m_w_ple_gate, m_w_ple_proj=m_w_ple_proj, m_ple_post_g=m_ple_post_g, v_rel_bias=v_rel_bias, v_ffn1_pre_g=v_ffn1_pre_g, v_ffn1_w_gu=v_ffn1_w_gu, v_ffn1_w_down=v_ffn1_w_down, v_ffn1_post_g=v_ffn1_post_g, v_attn_pre_g=v_attn_pre_g, v_w_in=v_w_in, v_b_in=v_b_in, v_sinks=v_sinks, v_w_out=v_w_out, v_b_out=v_b_out, v_attn_post_g=v_attn_post_g, v_ffn2_pre_g=v_ffn2_pre_g, v_ffn2_w_gu=v_ffn2_w_gu, v_ffn2_w_down=v_ffn2_w_down, v_ffn2_post_g=v_ffn2_post_g, v_ple_pre_g=v_ple_pre_g, v_w_ple_gate=v_w_ple_gate, v_w_ple_proj=v_w_ple_proj, v_ple_post_g=v_ple_post_g)
    weights = {n: given[n] for n in TWIN_WEIGHTS}
    shared = {n: given[n] for n in SHARED_INPUTS}
    per_example = {n: given[n] for n in ['x', 'p']}
    grad_fn = _jax.value_and_grad(_loss, argnums=(0, 1))

    def one_microbatch(ex, loss_target):
        ex = dict(ex)
        diff = ex.pop(TWIN_DIFF_INPUT)
        return grad_fn(weights, diff, {**shared, **ex}, loss_target)

    if N_MICROBATCH == 1:
        loss, (grad_w, grad_x) = one_microbatch(per_example, given["loss_target"])
    else:
        def body(carry, xs):
            loss_sum, grad_sum = carry
            l_k, (gw_k, gx_k) = one_microbatch(xs[0], xs[1])
            with _jax.named_scope("update"):
                return (loss_sum + l_k, _jax.tree.map(_jnp.add, grad_sum, gw_k)), gx_k

        init = (_jnp.zeros((), _jnp.float32), _jax.tree.map(_jnp.zeros_like, weights))
        (loss, grad_w), grad_x = _jax.lax.scan(body, init, (per_example, given["loss_target"]))
    with _jax.named_scope("update"):
        delta_w, new_m, new_v = {}, {}, {}
        for n in TWIN_WEIGHTS:
            delta_w[n], new_m[n], new_v[n] = _adamw(weights[n], grad_w[n], given["m_" + n], given["v_" + n])
    return (loss, grad_x, *[grad_w[n] for n in TWIN_WEIGHTS], *[delta_w[n] for n in TWIN_WEIGHTS],
            *[new_m[n] for n in TWIN_WEIGHTS], *[new_v[n] for n in TWIN_WEIGHTS])
```

```python
import functools
import math

import numpy as np
import jax
import jax.numpy as jnp
from jax import lax
from jax.experimental import pallas as pl
from jax.experimental.pallas import tpu as pltpu

F32 = jnp.float32
BF16 = jnp.bfloat16
MESH = pl.DeviceIdType.MESH

N_DEV = 8
EPS = 1e-6
NEG_INF = -1e30
HEAD_DIM = 64
LANES = 128
QBLK = 128
D_IN = 2304
A_Q, A_KV, B_W = 512, 128, 512
N_HEAD_GROUP = 8
NUM_BUCKETS = 32
MAX_DISTANCE = 2048
PATTERNS_A = ((1, 127),)
PATTERNS_B = ((1, 128), (4, 128), (16, 128))
Q_A_COL, K_A_COL, V_A_COL = 0, 4, 5
Q_B_COL, K_B_COL, V_B_COL = 6, 10, 14

ADAM_LR, ADAM_B1, ADAM_B2, ADAM_EPS, ADAM_WD, ADAM_STEP = 0.001, 0.9, 0.999, 1e-08, 0.01, 10

TOKEN_TILE = 512
VMEM_LIMIT = 56 * 1024 * 1024
ARB = "arbitrary"

BIG = ("ffn1_w_gu", "ffn1_w_down", "w_in", "w_out", "ffn2_w_gu", "ffn2_w_down", "w_ple_gate", "w_ple_proj")
GAINS = ("ffn1_pre_g", "ffn1_post_g", "attn_pre_g", "attn_post_g", "ffn2_pre_g", "ffn2_post_g",
         "ple_pre_g", "ple_post_g", "b_out")
SMALL = GAINS + ("b_in", "sinks", "rel_bias")
WEIGHTS = ("rel_bias", "ffn1_pre_g", "ffn1_w_gu", "ffn1_w_down", "ffn1_post_g", "attn_pre_g", "w_in", "b_in",
           "sinks", "w_out", "b_out", "attn_post_g", "ffn2_pre_g", "ffn2_w_gu", "ffn2_w_down", "ffn2_post_g",
           "ple_pre_g", "w_ple_gate", "w_ple_proj", "ple_post_g")


def _params(n_axes):
    return pltpu.CompilerParams(dimension_semantics=(ARB,) * n_axes, vmem_limit_bytes=VMEM_LIMIT)


def _mm(a, b):
    return jnp.dot(a, b, preferred_element_type=F32)


def _mm_nt(a, b):
    return lax.dot_general(a, b, (((1,), (1,)), ((), ())), preferred_element_type=F32)


def _mm_tn(a, b):
    return lax.dot_general(a, b, (((0,), (0,)), ((), ())), preferred_element_type=F32)


def _rstd(x):
    return lax.rsqrt(jnp.mean(x * x, axis=-1, keepdims=True) + EPS)


def _rms_bwd(x, r, gain, dy):
    n = x * r
    gdy = dy * gain
    return r * (gdy - n * jnp.mean(gdy * n, axis=-1, keepdims=True)), dy * n


def _colsum(v):
    return jnp.sum(v, axis=0, keepdims=True)


def _full(shape):
    return pl.BlockSpec(shape, lambda *_: (0,) * len(shape))


def _ffn_fwd(h, g_pre, g_post, w_gu, w_down, name):
    T, D = h.shape
    nj = w_gu.shape[0] // 2
    FB = w_gu.shape[2]
    tm = TOKEN_TILE

    def body(h_ref, gpre_ref, gpost_ref, wg_ref, wu_ref, wd_ref, hout_ref, f_ref, a_ref, gu_ref, a_scr, acc):
        j = pl.program_id(1)

        @pl.when(j == 0)
        def _():
            x = h_ref[...]
            a = (x * _rstd(x) * gpre_ref[...]).astype(BF16)
            a_scr[...] = a
            a_ref[...] = a
            acc[...] = jnp.zeros_like(acc)

        a = a_scr[...]
        g = _mm(a, wg_ref[...])
        u = _mm(a, wu_ref[...])
        gu_ref[0] = g.astype(BF16)
        gu_ref[1] = u.astype(BF16)
        hh = (g * jax.nn.sigmoid(g) * u).astype(BF16)
        acc[...] += _mm(hh, wd_ref[...])

        @pl.when(j == nj - 1)
        def _():
            f = acc[...]
            f_ref[...] = f
            hout_ref[...] = h_ref[...] + 0.5 * (f * _rstd(f) * gpost_ref[...])

    return pl.pallas_call(
        body, name=name, grid=(T // tm, nj),
        in_specs=[
            pl.BlockSpec((tm, D), lambda i, j: (i, 0)),
            _full((1, D)), _full((1, D)),
            pl.BlockSpec((None, D, FB), lambda i, j: (j, 0, 0)),
            pl.BlockSpec((None, D, FB), lambda i, j: (j + nj, 0, 0)),
            pl.BlockSpec((FB, D), lambda i, j: (j, 0)),
        ],
        out_specs=[
            pl.BlockSpec((tm, D), lambda i, j: (i, 0)),
            pl.BlockSpec((tm, D), lambda i, j: (i, 0)),
            pl.BlockSpec((tm, D), lambda i, j: (i, 0)),
            pl.BlockSpec((None, 2, tm, FB), lambda i, j: (j, 0, i, 0)),
        ],
        out_shape=[
            jax.ShapeDtypeStruct((T, D), F32),
            jax.ShapeDtypeStruct((T, D), F32),
            jax.ShapeDtypeStruct((T, D), BF16),
            jax.ShapeDtypeStruct((nj, 2, T, FB), BF16),
        ],
        scratch_shapes=[pltpu.VMEM((tm, D), BF16), pltpu.VMEM((tm, D), F32)],
        compiler_params=_params(2),
    )(h, g_pre, g_post, w_gu, w_gu, w_down)


def _ffn_bwd(dh_out, f, g_post, h, g_pre, gu, w_gu, w_down, name):
    T, D = h.shape
    nj = w_gu.shape[0] // 2
    FB = w_gu.shape[2]
    tm = TOKEN_TILE

    def body(dho_ref, f_ref, gpost_ref, h_ref, gpre_ref, gu_ref, wg_ref, wu_ref, wd_ref,
             dhin_ref, df_ref, hh_ref, dgu_ref, dgpost_ref, dgpre_ref, df_scr, da):
        i, j = pl.program_id(0), pl.program_id(1)

        @pl.when(jnp.logical_and(i == 0, j == 0))
        def _():
            dgpost_ref[...] = jnp.zeros_like(dgpost_ref)
            dgpre_ref[...] = jnp.zeros_like(dgpre_ref)

        @pl.when(j == 0)
        def _():
            fv = f_ref[...]
            df, dgain = _rms_bwd(fv, _rstd(fv), gpost_ref[...], 0.5 * dho_ref[...])
            dgpost_ref[...] += _colsum(dgain)
            dfb = df.astype(BF16)
            df_scr[...] = dfb
            df_ref[...] = dfb
            da[...] = jnp.zeros_like(da)

        dhh = _mm_nt(df_scr[...], wd_ref[...])
        g = gu_ref[0].astype(F32)
        u = gu_ref[1].astype(F32)
        sg = jax.nn.sigmoid(g)
        silu = g * sg
        hh_ref[...] = (silu * u).astype(BF16)
        dg = (dhh * u * (sg * (1.0 + g * (1.0 - sg)))).astype(BF16)
        du = (dhh * silu).astype(BF16)
        dgu_ref[0] = dg
        dgu_ref[1] = du
        da[...] += _mm_nt(dg, wg_ref[...]) + _mm_nt(du, wu_ref[...])

        @pl.when(j == nj - 1)
        def _():
            x = h_ref[...]
            dx, dgain = _rms_bwd(x, _rstd(x), gpre_ref[...], da[...])
            dgpre_ref[...] += _colsum(dgain)
            dhin_ref[...] = dho_ref[...] + dx

    tile = pl.BlockSpec((tm, D), lambda i, j: (i, 0))
    return pl.pallas_call(
        body, name=name, grid=(T // tm, nj),
        in_specs=[
            tile, tile, _full((1, D)), tile, _full((1, D)),
            pl.BlockSpec((None, 2, tm, FB), lambda i, j: (j, 0, i, 0)),
            pl.BlockSpec((None, D, FB), lambda i, j: (j, 0, 0)),
            pl.BlockSpec((None, D, FB), lambda i, j: (j + nj, 0, 0)),
            pl.BlockSpec((FB, D), lambda i, j: (j, 0)),
        ],
        out_specs=[
            tile, tile,
            pl.BlockSpec((None, tm, FB), lambda i, j: (j, i, 0)),
            pl.BlockSpec((None, 2, tm, FB), lambda i, j: (j, 0, i, 0)),
            _full((1, D)), _full((1, D)),
        ],
        out_shape=[
            jax.ShapeDtypeStruct((T, D), F32),
            jax.ShapeDtypeStruct((T, D), BF16),
            jax.ShapeDtypeStruct((nj, T, FB), BF16),
            jax.ShapeDtypeStruct((nj, 2, T, FB), BF16),
            jax.ShapeDtypeStruct((1, D), F32),
            jax.ShapeDtypeStruct((1, D), F32),
        ],
        scratch_shapes=[pltpu.VMEM((tm, D), BF16), pltpu.VMEM((tm, D), F32)],
        compiler_params=_params(2),
    )(dh_out, f, g_post, h, g_pre, gu, w_gu, w_gu, w_down)


def _tn_matmul(x, y, x_spec, y_spec, out_shape, out_spec, n_blocks, n_steps, acc_shape, name):
    def body(x_ref, y_ref, o_ref, acc):
        t = pl.program_id(1)

        @pl.when(t == 0)
        def _():
            acc[...] = jnp.zeros_like(acc)

        acc[...] += _mm_tn(x_ref[...].astype(BF16), y_ref[...].astype(BF16))

        @pl.when(t == n_steps - 1)
        def _():
            o_ref[...] = acc[...].astype(o_ref.dtype)

    return pl.pallas_call(
        body, name=name, grid=(n_blocks, n_steps),
        in_specs=[x_spec, y_spec], out_specs=out_spec, out_shape=out_shape,
        scratch_shapes=[pltpu.VMEM(acc_shape, F32)],
        compiler_params=_params(2),
    )(x, y)


def _inproj_fwd(h, g_pre, w_in, b_in):
    T, D = h.shape
    tm = TOKEN_TILE

    def body(h_ref, g_ref, w_ref, b_ref, z_ref, a_ref):
        x = h_ref[...]
        a = (x * _rstd(x) * g_ref[...]).astype(BF16)
        a_ref[...] = a
        z_ref[...] = _mm(a, w_ref[...]) + b_ref[...]

    return pl.pallas_call(
        body, name="inproj_fwd", grid=(T // tm,),
        in_specs=[pl.BlockSpec((tm, D), lambda i: (i, 0)), _full((1, D)), _full((D, D_IN)), _full((1, D_IN))],
        out_specs=[pl.BlockSpec((tm, D_IN), lambda i: (i, 0)), pl.BlockSpec((tm, D), lambda i: (i, 0))],
        out_shape=[jax.ShapeDtypeStruct((T, D_IN), F32), jax.ShapeDtypeStruct((T, D), BF16)],
        compiler_params=_params(1),
    )(h, g_pre, w_in, b_in)


def _inproj_bwd(dqa, dka, dva, dqb, dkb, dvb, w_in, h, g_pre, dres):
    T, D = h.shape
    tm = TOKEN_TILE

    def body(dqa_ref, dka_ref, dva_ref, dqb_ref, dkb_ref, dvb_ref, w_ref, h_ref, g_ref, dres_ref,
             dh_ref, dz_ref, dbin_ref, dg_ref):
        i = pl.program_id(0)

        @pl.when(i == 0)
        def _():
            dbin_ref[...] = jnp.zeros_like(dbin_ref)
            dg_ref[...] = jnp.zeros_like(dg_ref)

        dz = jnp.concatenate([dqa_ref[...], dka_ref[...], dva_ref[...], dqb_ref[...], dkb_ref[...], dvb_ref[...]],
                             axis=1)
        dbin_ref[...] += _colsum(dz)
        dzb = dz.astype(BF16)
        dz_ref[...] = dzb
        da = _mm_nt(dzb, w_ref[...])
        x = h_ref[...]
        dx, dgain = _rms_bwd(x, _rstd(x), g_ref[...], da)
        dg_ref[...] += _colsum(dgain)
        dh_ref[...] = dres_ref[...] + dx

    def tile(w):
        return pl.BlockSpec((tm, w), lambda i: (i, 0))

    return pl.pallas_call(
        body, name="inproj_bwd", grid=(T // tm,),
        in_specs=[tile(A_Q), tile(A_KV), tile(A_KV), tile(B_W), tile(B_W), tile(B_W),
                  _full((D, D_IN)), tile(D), _full((1, D)), tile(D)],
        out_specs=[tile(D), tile(D_IN), _full((1, D_IN)), _full((1, D))],
        out_shape=[jax.ShapeDtypeStruct((T, D), F32), jax.ShapeDtypeStruct((T, D_IN), BF16),
                   jax.ShapeDtypeStruct((1, D_IN), F32), jax.ShapeDtypeStruct((1, D), F32)],
        compiler_params=_params(1),
    )(dqa, dka, dva, dqb, dkb, dvb, w_in, h, g_pre, dres)


def _bucket_tiles(patterns):
    i = np.arange(QBLK)[:, None]
    j = np.arange(2 * QBLK)[None, :]
    dist = QBLK + i - j
    max_exact = NUM_BUCKETS // 2
    tiles = []
    for dilation, max_dist in patterns:
        n = np.maximum(dist * dilation, 0)
        nf = np.maximum(n, 1).astype(np.float32)
        large = max_exact + (np.log(nf / np.float32(max_exact)) / np.float32(math.log(MAX_DISTANCE / max_exact))
                             * np.float32(NUM_BUCKETS - max_exact)).astype(np.int32)
        bucket = np.where(n < max_exact, n, np.minimum(large, NUM_BUCKETS - 1))
        tiles.append(np.where((dist >= 0) & (dist <= max_dist), bucket, -1))
    return jnp.asarray(np.stack(tiles).astype(np.int32))


def _bias_build(rel_bias, buckets, head0, name):
    n = buckets.shape[0]

    def body(bk_ref, rb_ref, o_ref):
        bk = bk_ref[...]
        base = jnp.where(bk < 0, NEG_INF, 0.0).astype(F32)
        for hd in range(N_HEAD_GROUP):
            o_ref[hd] = lax.fori_loop(
                0, NUM_BUCKETS, lambda b, acc, hd=hd: jnp.where(bk == b, rb_ref[b, head0 + hd], acc), base)

    return pl.pallas_call(
        body, name=name, grid=(n,),
        in_specs=[pl.BlockSpec((None, QBLK, 2 * QBLK), lambda p: (p, 0, 0)), pl.BlockSpec(memory_space=pltpu.SMEM)],
        out_specs=pl.BlockSpec((None, N_HEAD_GROUP, QBLK, 2 * QBLK), lambda p: (p, 0, 0, 0)),
        out_shape=jax.ShapeDtypeStruct((n, N_HEAD_GROUP, QBLK, 2 * QBLK), F32),
        compiler_params=_params(1),
    )(buckets, rel_bias)


def _bias_grad(ds, buckets, name):
    n = buckets.shape[0]

    def body(ds_ref, bk_ref, o_ref):
        p = pl.program_id(0)
        bk = bk_ref[...]
        for hd in range(N_HEAD_GROUP):
            d = ds_ref[hd]

            def per_bucket(b, carry, d=d, hd=hd):
                o_ref[p * N_HEAD_GROUP + hd, b] = jnp.sum(jnp.where(bk == b, d, 0.0))
                return carry

            lax.fori_loop(0, NUM_BUCKETS, per_bucket, 0)

    return pl.pallas_call(
        body, name=name, grid=(n,),
        in_specs=[pl.BlockSpec((None, N_HEAD_GROUP, QBLK, 2 * QBLK), lambda p: (p, 0, 0, 0)),
                  pl.BlockSpec((None, QBLK, 2 * QBLK), lambda p: (p, 0, 0))],
        out_specs=pl.BlockSpec(memory_space=pltpu.SMEM),
        out_shape=jax.ShapeDtypeStruct((n * N_HEAD_GROUP, NUM_BUCKETS), F32),
        compiler_params=_params(1),
    )(ds, buckets)


def _class_rows(start, dilation):
    if dilation == 1:
        return pl.ds(pl.multiple_of(start, QBLK), QBLK)
    return pl.ds(start, QBLK, stride=dilation)


def _block_starts(idx, n_blocks, dilation):
    cls = idx // n_blocks
    n = idx % n_blocks
    cur = cls + dilation * QBLK * n
    prev = cls + dilation * QBLK * jnp.maximum(n - 1, 0)
    return n, cur, prev


def _attn_specs(T, qcol, kcol, vcol, shared_kv):
    kv = (lambda c: (lambda g: (0, c))) if shared_kv else (lambda c: (lambda g: (0, c + g)))
    return [pl.BlockSpec((T, LANES), lambda g: (0, qcol + g)),
            pl.BlockSpec((T, LANES), kv(kcol)),
            pl.BlockSpec((T, LANES), kv(vcol))]


def _attn_fwd(z, bias, sinks, patterns, qcol, kcol, vcol, shared_kv, name):
    T = z.shape[0]
    n_pat = len(patterns)
    has_sink = sinks is not None

    def body(*refs):
        if has_sink:
            sink_ref, refs = refs[0], refs[1:]
        q_ref, k_ref, v_ref, b_ref, o_ref, l_ref = refs[:6]
        po_scr, pl_scr = refs[6:6 + n_pat], refs[6 + n_pat:]
        g = pl.program_id(0)
        in_prev = lax.broadcasted_iota(jnp.int32, (QBLK, 2 * QBLK), 1) < QBLK
        upper = (g // 2) == 1

        def head(t, hd):
            if shared_kv:
                return jnp.where(upper, t[:, HEAD_DIM:], t[:, :HEAD_DIM])
            return t[:, HEAD_DIM * hd:HEAD_DIM * (hd + 1)]

        for pi, (dilation, _) in enumerate(patterns):
            n_blocks = T // (QBLK * dilation)

            def step(idx, carry, pi=pi, dilation=dilation, n_blocks=n_blocks):
                n, cur, prev = _block_starts(idx, n_blocks, dilation)
                rows_c, rows_p = _class_rows(cur, dilation), _class_rows(prev, dilation)
                q = q_ref[rows_c, :]
                k2 = jnp.concatenate([k_ref[rows_p, :], k_ref[rows_c, :]], axis=0)
                v2 = jnp.concatenate([v_ref[rows_p, :], v_ref[rows_c, :]], axis=0)
                no_prev = jnp.logical_and(in_prev, n == 0)
                outs, lses = [], []
                for hd in range(2):
                    qh = q[:, HEAD_DIM * hd:HEAD_DIM * (hd + 1)].astype(BF16)
                    kh = head(k2, hd).astype(BF16)
                    vh = head(v2, hd).astype(BF16)
                    s = _mm_nt(qh, kh) * (HEAD_DIM ** -0.5) + b_ref[pi, hd]
                    s = jnp.where(no_prev, NEG_INF, s)
                    m = jnp.max(s, axis=1, keepdims=True)
                    pr = jnp.exp(s - m)
                    den = jnp.sum(pr, axis=1, keepdims=True)
                    outs.append(_mm(pr.astype(BF16), vh) / den)
                    lses.append(jnp.broadcast_to(m + jnp.log(den), (QBLK, HEAD_DIM)))
                po_scr[pi][rows_c, :] = jnp.concatenate(outs, axis=1)
                pl_scr[pi][rows_c, :] = jnp.concatenate(lses, axis=1)
                return carry

            lax.fori_loop(0, dilation * n_blocks, step, 0)

        if has_sink:
            lane = lax.broadcasted_iota(jnp.int32, (1, LANES), 1)
            sink = jnp.where(lane < HEAD_DIM, sink_ref[0, 2 * g], sink_ref[0, 2 * g + 1])

        def merge(ci, carry):
            rows = pl.ds(pl.multiple_of(ci * QBLK, QBLK), QBLK)
            parts = [pl_scr[pi][rows, :] for pi in range(n_pat)]
            m = functools.reduce(jnp.maximum, parts)
            if has_sink:
                m = jnp.maximum(m, sink)
            den = functools.reduce(jnp.add, [jnp.exp(x - m) for x in parts])
            if has_sink:
                den = den + jnp.exp(sink - m)
            lse = m + jnp.log(den)
            o_ref[rows, :] = functools.reduce(
                jnp.add, [jnp.exp(parts[pi] - lse) * po_scr[pi][rows, :] for pi in range(n_pat)])
            l_ref[rows, :] = lse
            return carry

        lax.fori_loop(0, T // QBLK, merge, 0)

    in_specs = _attn_specs(T, qcol, kcol, vcol, shared_kv)
    in_specs.append(pl.BlockSpec((n_pat, 2, QBLK, 2 * QBLK), lambda g: (0, g, 0, 0)))
    args = [z, z, z, bias]
    if has_sink:
        in_specs.insert(0, pl.BlockSpec(memory_space=pltpu.SMEM))
        args.insert(0, sinks)
    out = pl.BlockSpec((T, LANES), lambda g: (0, g))
    return pl.pallas_call(
        body, name=name, grid=(N_HEAD_GROUP // 2,),
        in_specs=in_specs, out_specs=[out, out],
        out_shape=[jax.ShapeDtypeStruct((T, N_HEAD_GROUP * HEAD_DIM), F32)] * 2,
        scratch_shapes=[pltpu.VMEM((T, LANES), F32)] * (2 * n_pat),
        compiler_params=_params(1),
    )(*args)


def _attn_bwd(z, bias, sinks, d_out, out, lse, patterns, qcol, kcol, vcol, shared_kv, name):
    T = z.shape[0]
    n_pat = len(patterns)
    has_sink = sinks is not None
    kv_width = LANES if shared_kv else N_HEAD_GROUP * HEAD_DIM

    def body(*refs):
        if has_sink:
            sink_ref, refs = refs[0], refs[1:]
        q_ref, k_ref, v_ref, b_ref, do_ref, o_ref, l_ref = refs[:7]
        dq_ref, dk_ref, dv_ref, ds_ref = refs[7:11]
        dsink_ref = refs[11] if has_sink else None
        dk_acc, dv_acc = refs[-2:]
        g = pl.program_id(0)
        in_prev = lax.broadcasted_iota(jnp.int32, (QBLK, 2 * QBLK), 1) < QBLK
        lane = lax.broadcasted_iota(jnp.int32, (1, LANES), 1)
        upper = (g // 2) == 1
        own_half = (lane >= HEAD_DIM).astype(jnp.int32) == (g // 2)

        dq_ref[...] = jnp.zeros_like(dq_ref)
        ds_ref[...] = jnp.zeros_like(ds_ref)
        dk_acc[...] = jnp.zeros_like(dk_acc)
        dv_acc[...] = jnp.zeros_like(dv_acc)

        def head(t, hd):
            if shared_kv:
                return jnp.where(upper, t[:, HEAD_DIM:], t[:, :HEAD_DIM])
            return t[:, HEAD_DIM * hd:HEAD_DIM * (hd + 1)]

        def to_lanes(parts):
            if shared_kv:
                both = parts[0] + parts[1]
                return jnp.where(own_half, jnp.concatenate([both, both], axis=1), 0.0)
            return jnp.concatenate(parts, axis=1)

        dsink = jnp.zeros((1, LANES), F32)
        for pi, (dilation, _) in enumerate(patterns):
            n_blocks = T // (QBLK * dilation)

            def step(idx, dsink, pi=pi, dilation=dilation, n_blocks=n_blocks):
                n, cur, prev = _block_starts(idx, n_blocks, dilation)
                rows_c, rows_p = _class_rows(cur, dilation), _class_rows(prev, dilation)
                q = q_ref[rows_c, :]
                k2 = jnp.concatenate([k_ref[rows_p, :], k_ref[rows_c, :]], axis=0)
                v2 = jnp.concatenate([v_ref[rows_p, :], v_ref[rows_c, :]], axis=0)
                d_o = do_ref[rows_c, :]
                o = o_ref[rows_c, :]
                l = l_ref[rows_c, :]
                no_prev = jnp.logical_and(in_prev, n == 0)
                dqs, dks, dvs = [], [], []
                for hd in range(2):
                    cols = slice(HEAD_DIM * hd, HEAD_DIM * (hd + 1))
                    qh = q[:, cols].astype(BF16)
                    kh = head(k2, hd).astype(BF16)
                    vh = head(v2, hd).astype(BF16)
                    doh = d_o[:, cols]
                    delta = jnp.sum(doh * o[:, cols], axis=1, keepdims=True)
                    lh = l[:, HEAD_DIM * hd:HEAD_DIM * hd + 1]
                    s = _mm_nt(qh, kh) * (HEAD_DIM ** -0.5) + b_ref[pi, hd]
                    s = jnp.where(no_prev, NEG_INF, s)
                    pr = jnp.exp(s - lh)
                    dob = doh.astype(BF16)
                    ds = pr * (_mm_nt(dob, vh) - delta)
                    ds_ref[pi, hd] += ds
                    dsb = ds.astype(BF16)
                    dqs.append(_mm(dsb, kh) * (HEAD_DIM ** -0.5))
                    dks.append(_mm_tn(dsb, qh) * (HEAD_DIM ** -0.5))
                    dvs.append(_mm_tn(pr.astype(BF16), dob))
                    if has_sink:
                        p_sink = jnp.exp(sink_ref[0, 2 * g + hd] - lh)
                        dsink = dsink - jnp.where(lane == 2 * g + hd, jnp.sum(p_sink * delta), 0.0)
                dq_ref[rows_c, :] += jnp.concatenate(dqs, axis=1)
                dk2 = to_lanes(dks)
                dv2 = to_lanes(dvs)
                dk_acc[rows_p, :] += dk2[:QBLK]
                dk_acc[rows_c, :] += dk2[QBLK:]
                dv_acc[rows_p, :] += dv2[:QBLK]
                dv_acc[rows_c, :] += dv2[QBLK:]
                return dsink

            dsink = lax.fori_loop(0, dilation * n_blocks, step, dsink)

        if shared_kv:
            @pl.when(g == 0)
            def _():
                dk_ref[...] = dk_acc[...]
                dv_ref[...] = dv_acc[...]

            @pl.when(g != 0)
            def _():
                dk_ref[...] += dk_acc[...]
                dv_ref[...] += dv_acc[...]
        else:
            dk_ref[...] = dk_acc[...]
            dv_ref[...] = dv_acc[...]

        if has_sink:
            @pl.when(g == 0)
            def _():
                dsink_ref[...] = dsink

            @pl.when(g != 0)
            def _():
                dsink_ref[...] += dsink

    pair = pl.BlockSpec((T, LANES), lambda g: (0, g))
    in_specs = _attn_specs(T, qcol, kcol, vcol, shared_kv)
    in_specs += [pl.BlockSpec((n_pat, 2, QBLK, 2 * QBLK), lambda g: (0, g, 0, 0)), pair, pair, pair]
    args = [z, z, z, bias, d_out, out, lse]
    kv_out = _full((T, LANES)) if shared_kv else pair
    out_specs = [pair, kv_out, kv_out, pl.BlockSpec((n_pat, 2, QBLK, 2 * QBLK), lambda g: (0, g, 0, 0))]
    out_shape = [jax.ShapeDtypeStruct((T, N_HEAD_GROUP * HEAD_DIM), F32),
                 jax.ShapeDtypeStruct((T, kv_width), F32), jax.ShapeDtypeStruct((T, kv_width), F32),
                 jax.ShapeDtypeStruct((n_pat, N_HEAD_GROUP, QBLK, 2 * QBLK), F32)]
    if has_sink:
        in_specs.insert(0, pl.BlockSpec(memory_space=pltpu.SMEM))
        args.insert(0, sinks)
        out_specs.append(_full((1, LANES)))
        out_shape.append(jax.ShapeDtypeStruct((1, LANES), F32))
    return pl.pallas_call(
        body, name=name, grid=(N_HEAD_GROUP // 2,),
        in_specs=in_specs, out_specs=out_specs, out_shape=out_shape,
        scratch_shapes=[pltpu.VMEM((T, LANES), F32), pltpu.VMEM((T, LANES), F32)],
        compiler_params=_params(1),
    )(*args)


def _outproj_fwd(mix_a, mix_b, w_out, b_out, g_post, h):
    T, D = h.shape
    tm = TOKEN_TILE
    d_mix = w_out.shape[0]

    def body(ma_ref, mb_ref, w_ref, b_ref, g_ref, h_ref, att_ref, hout_ref, mix_ref):
        mix = jnp.concatenate([ma_ref[...], mb_ref[...]], axis=1).astype(BF16)
        mix_ref[...] = mix
        att = _mm(mix, w_ref[...]) + b_ref[...]
        att_ref[...] = att
        hout_ref[...] = h_ref[...] + att * _rstd(att) * g_ref[...]

    def tile(w):
        return pl.BlockSpec((tm, w), lambda i: (i, 0))

    return pl.pallas_call(
        body, name="outproj_fwd", grid=(T // tm,),
        in_specs=[tile(A_Q), tile(B_W), _full((d_mix, D)), _full((1, D)), _full((1, D)), tile(D)],
        out_specs=[tile(D), tile(D), tile(d_mix)],
        out_shape=[jax.ShapeDtypeStruct((T, D), F32), jax.ShapeDtypeStruct((T, D), F32),
                   jax.ShapeDtypeStruct((T, d_mix), BF16)],
        compiler_params=_params(1),
    )(mix_a, mix_b, w_out, b_out, g_post, h)


def _outproj_bwd(dh, att, g_post, w_out):
    T, D = dh.shape
    tm = TOKEN_TILE
    d_mix = w_out.shape[0]

    def body(dh_ref, att_ref, g_ref, w_ref, dma_ref, dmb_ref, datt_ref, dg_ref, db_ref):
        i = pl.program_id(0)

        @pl.when(i == 0)
        def _():
            dg_ref[...] = jnp.zeros_like(dg_ref)
            db_ref[...] = jnp.zeros_like(db_ref)

        att = att_ref[...]
        datt, dgain = _rms_bwd(att, _rstd(att), g_ref[...], dh_ref[...])
        dg_ref[...] += _colsum(dgain)
        db_ref[...] += _colsum(datt)
        dattb = datt.astype(BF16)
        datt_ref[...] = dattb
        dmix = _mm_nt(dattb, w_ref[...])
        dma_ref[...] = dmix[:, :A_Q]
        dmb_ref[...] = dmix[:, A_Q:]

    def tile(w):
        return pl.BlockSpec((tm, w), lambda i: (i, 0))

    return pl.pallas_call(
        body, name="outproj_bwd", grid=(T // tm,),
        in_specs=[tile(D), tile(D), _full((1, D)), _full((d_mix, D))],
        out_specs=[tile(A_Q), tile(B_W), tile(D), _full((1, D)), _full((1, D))],
        out_shape=[jax.ShapeDtypeStruct((T, A_Q), F32), jax.ShapeDtypeStruct((T, B_W), F32),
                   jax.ShapeDtypeStruct((T, D), BF16), jax.ShapeDtypeStruct((1, D), F32),
                   jax.ShapeDtypeStruct((1, D), F32)],
        compiler_params=_params(1),
    )(dh, att, g_post, w_out)


def _ple_fwd_loss(h, g_pre, w_gate, p, w_proj, g_post, target):
    T, D = h.shape
    tm = TOKEN_TILE
    n_proj, ple, db = w_proj.shape

    def body(h_ref, gpre_ref, wg_ref, p_ref, wp_ref, gpost_ref, t_ref,
             a_ref, dpre_ref, de_ref, dh_ref, loss_ref, dgpost_ref):
        i = pl.program_id(0)

        @pl.when(i == 0)
        def _():
            loss_ref[...] = jnp.zeros_like(loss_ref)
            dgpost_ref[...] = jnp.zeros_like(dgpost_ref)

        x = h_ref[...]
        a = (x * _rstd(x) * gpre_ref[...]).astype(BF16)
        a_ref[...] = a
        gate = jax.nn.sigmoid(_mm(a, wg_ref[...]))
        pb = p_ref[...].astype(BF16)
        e = jnp.concatenate([_mm(pb, wp_ref[k]) for k in range(n_proj)], axis=1)
        ge = gate * e
        rg = _rstd(ge)
        diff = x + ge * rg * gpost_ref[...] - t_ref[...]
        loss_ref[...] += 0.5 * jnp.sum(jnp.mean(diff * diff, axis=1, keepdims=True))
        dy = diff * (1.0 / D)
        dh_ref[...] = dy
        dge, dgain = _rms_bwd(ge, rg, gpost_ref[...], dy)
        dgpost_ref[...] += _colsum(dgain)
        de_ref[...] = (dge * gate).astype(BF16)
        dpre_ref[...] = (dge * e * gate * (1.0 - gate)).astype(BF16)

    def tile(w):
        return pl.BlockSpec((tm, w), lambda i: (i, 0))

    return pl.pallas_call(
        body, name="ple_fwd_loss", grid=(T // tm,),
        in_specs=[tile(D), _full((1, D)), _full((D, D)), tile(ple), _full((n_proj, ple, db)), _full((1, D)), tile(D)],
        out_specs=[tile(D), tile(D), tile(D), tile(D), _full((1, LANES)), _full((1, D))],
        out_shape=[jax.ShapeDtypeStruct((T, D), BF16),
                   jax.ShapeDtypeStruct((T, D), BF16),
                   jax.ShapeDtypeStruct((T, D), BF16),
                   jax.ShapeDtypeStruct((T, D), F32),
                   jax.ShapeDtypeStruct((1, LANES), F32),
                   jax.ShapeDtypeStruct((1, D), F32)],
        compiler_params=_params(1),
    )(h, g_pre, w_gate, p, w_proj, g_post, target)


def _ple_bwd(dpre, w_gate, h, g_pre, dres):
    T, D = h.shape
    tm = TOKEN_TILE

    def body(dpre_ref, w_ref, h_ref, g_ref, dres_ref, dh_ref, dg_ref):
        i = pl.program_id(0)

        @pl.when(i == 0)
        def _():
            dg_ref[...] = jnp.zeros_like(dg_ref)

        da = _mm_nt(dpre_ref[...], w_ref[...])
        x = h_ref[...]
        dx, dgain = _rms_bwd(x, _rstd(x), g_ref[...], da)
        dg_ref[...] += _colsum(dgain)
        dh_ref[...] = dres_ref[...] + dx

    tile = pl.BlockSpec((tm, D), lambda i: (i, 0))
    return pl.pallas_call(
        body, name="ple_bwd", grid=(T // tm,),
        in_specs=[tile, _full((D, D)), tile, _full((1, D)), tile],
        out_specs=[tile, _full((1, D))],
        out_shape=[jax.ShapeDtypeStruct((T, D), F32), jax.ShapeDtypeStruct((1, D), F32)],
        compiler_params=_params(1),
    )(dpre, w_gate, h, g_pre, dres)


def _local_step(x, p, target, small, w):
    T, D = x.shape
    tk = TOKEN_TILE
    nt = T // tk
    FB = w["ffn1_w_gu"].shape[2]
    nj = w["ffn1_w_gu"].shape[0] // 2
    ple = p.shape[1]
    db = w["w_ple_proj"].shape[2]

    buckets_a = _bucket_tiles(PATTERNS_A)
    buckets_b = _bucket_tiles(PATTERNS_B)
    bias_a = _bias_build(small["rel_bias"], buckets_a, 0, "bias_build_a")
    bias_b = _bias_build(small["rel_bias"], buckets_b, N_HEAD_GROUP, "bias_build_b")
    a_cfg = dict(patterns=PATTERNS_A, qcol=Q_A_COL, kcol=K_A_COL, vcol=V_A_COL, shared_kv=True)
    b_cfg = dict(patterns=PATTERNS_B, qcol=Q_B_COL, kcol=K_B_COL, vcol=V_B_COL, shared_kv=False)

    h1, f1, a1, gu1 = _ffn_fwd(x, small["ffn1_pre_g"], small["ffn1_post_g"], w["ffn1_w_gu"], w["ffn1_w_down"], "ffn1_fwd")
    z, a2 = _inproj_fwd(h1, small["attn_pre_g"], w["w_in"], small["b_in"])
    mix_a, lse_a = _attn_fwd(z, bias_a, small["sinks"], name="attn_a_fwd", **a_cfg)
    mix_b, lse_b = _attn_fwd(z, bias_b, None, name="attn_b_fwd", **b_cfg)
    att, h2, mix = _outproj_fwd(mix_a, mix_b, w["w_out"], small["b_out"], small["attn_post_g"], h1)
    h3, f2, a3, gu2 = _ffn_fwd(h2, small["ffn2_pre_g"], small["ffn2_post_g"], w["ffn2_w_gu"], w["ffn2_w_down"], "ffn2_fwd")
    a4, dpre, de, dh4, loss, dg_ple_post = _ple_fwd_loss(
        h3, small["ple_pre_g"], w["w_ple_gate"], p, w["w_ple_proj"], small["ple_post_g"], target)

    def tok(width):
        return pl.BlockSpec((tk, width), lambda b, t: (t, 0))

    def grads_ffn(a, hh, df, dgu, tag):
        d_gu = _tn_matmul(
            a, dgu, tok(D), pl.BlockSpec((None, None, tk, FB), lambda b, t: (b % nj, b // nj, t, 0)),
            jax.ShapeDtypeStruct((2 * nj, D, FB), BF16), pl.BlockSpec((None, D, FB), lambda b, t: (b, 0, 0)),
            2 * nj, nt, (D, FB), tag + "_dw_gu")
        d_down = _tn_matmul(
            hh, df, pl.BlockSpec((None, tk, FB), lambda b, t: (b, t, 0)), tok(D),
            jax.ShapeDtypeStruct((nj, FB, D), BF16), pl.BlockSpec((None, FB, D), lambda b, t: (b, 0, 0)),
            nj, nt, (FB, D), tag + "_dw_down")
        return d_gu, d_down.reshape(N_DEV, (nj * FB) // N_DEV, D)

    def grads_rows(xm, y, name, rows=256):
        k = xm.shape[1]
        out = _tn_matmul(
            xm, y, pl.BlockSpec((tk, rows), lambda b, t: (t, b)), tok(D),
            jax.ShapeDtypeStruct((k, D), BF16), pl.BlockSpec((rows, D), lambda b, t: (b, 0)),
            k // rows, nt, (rows, D), name)
        return out.reshape(N_DEV, k // N_DEV, D)

    dh3, dg_ple_pre = _ple_bwd(dpre, w["w_ple_gate"], h3, small["ple_pre_g"], dh4)
    d_gate = grads_rows(a4, dpre, "ple_dw_gate", rows=min(256, D))
    d_proj = _tn_matmul(
        p, de, tok(ple), pl.BlockSpec((tk, db), lambda b, t: (t, b)),
        jax.ShapeDtypeStruct((N_DEV, ple, db), BF16), pl.BlockSpec((None, ple, db), lambda b, t: (b, 0, 0)),
        N_DEV, nt, (ple, db), "ple_dw_proj")

    dh2, df2, hh2, dgu2, dg_f2_post, dg_f2_pre = _ffn_bwd(
        dh3, f2, small["ffn2_post_g"], h2, small["ffn2_pre_g"], gu2, w["ffn2_w_gu"], w["ffn2_w_down"], "ffn2_bwd")
    d_gu2, d_down2 = grads_ffn(a3, hh2, df2, dgu2, "ffn2")

    dmix_a, dmix_b, datt, dg_attn_post, db_out = _outproj_bwd(dh2, att, small["attn_post_g"], w["w_out"])
    d_out = grads_rows(mix, datt, "attn_dw_out")
    dqa, dka, dva, ds_a, dsinks = _attn_bwd(z, bias_a, small["sinks"], dmix_a, mix_a, lse_a, name="attn_a_bwd", **a_cfg)
    dqb, dkb, dvb, ds_b = _attn_bwd(z, bias_b, None, dmix_b, mix_b, lse_b, name="attn_b_bwd", **b_cfg)
    dh1, dz, db_in, dg_attn_pre = _inproj_bwd(dqa, dka, dva, dqb, dkb, dvb, w["w_in"], h1, small["attn_pre_g"], dh2)
    cols = D_IN // 3
    d_in = _tn_matmul(
        a2, dz, tok(D), pl.BlockSpec((tk, cols), lambda b, t: (t, b)),
        jax.ShapeDtypeStruct((D, D_IN), BF16), pl.BlockSpec((D, cols), lambda b, t: (0, b)),
        3, nt, (D, cols), "attn_dw_in")
    d_in = jnp.transpose(d_in.reshape(D, N_DEV, D_IN // N_DEV), (1, 0, 2))

    grad_x, df1, hh1, dgu1, dg_f1_post, dg_f1_pre = _ffn_bwd(
        dh1, f1, small["ffn1_post_g"], x, small["ffn1_pre_g"], gu1, w["ffn1_w_gu"], w["ffn1_w_down"], "ffn1_bwd")
    d_gu1, d_down1 = grads_ffn(a1, hh1, df1, dgu1, "ffn1")

    rb_a = _bias_grad(ds_a, buckets_a, "bias_grad_a")
    rb_b = _bias_grad(ds_b, buckets_b, "bias_grad_b").reshape(len(PATTERNS_B), N_HEAD_GROUP, NUM_BUCKETS)
    d_rel_bias = jnp.concatenate([rb_a.T, jnp.sum(rb_b, axis=0).T], axis=1)

    big = {"ffn1_w_gu": d_gu1, "ffn1_w_down": d_down1, "w_in": d_in, "w_out": d_out,
           "ffn2_w_gu": d_gu2, "ffn2_w_down": d_down2, "w_ple_gate": d_gate, "w_ple_proj": d_proj}
    small_grads = {"ffn1_pre_g": dg_f1_pre, "ffn1_post_g": dg_f1_post, "attn_pre_g": dg_attn_pre,
                   "attn_post_g": dg_attn_post, "ffn2_pre_g": dg_f2_pre, "ffn2_post_g": dg_f2_post,
                   "ple_pre_g": dg_ple_pre, "ple_post_g": dg_ple_post, "b_out": db_out, "b_in": db_in,
                   "sinks": dsinks, "rel_bias": d_rel_bias}
    return grad_x, big, small_grads, loss


def _mesh_place():
    x, y, c = lax.axis_index("x"), lax.axis_index("y"), lax.axis_index("c")
    return x, y, c


def _slot(dev):
    return 4 * dev[0] + 2 * dev[1] + dev[2]


def _all_gather_bf16(shards):
    n = len(shards)

    def body(*refs):
        ins, outs, scr = refs[:n], refs[n:2 * n], refs[2 * n:3 * n]
        send_sems, recv_sems, local_sems = refs[3 * n:]
        x, y, c = _mesh_place()
        me, sibling = (x, y, c), (x, y, 1 - c)
        chips = [(1 - x, y), (x, 1 - y), (1 - x, 1 - y)]
        for a in range(n):
            scr[a][...] = ins[a][...].astype(BF16)

        def copy(a, k, block, to, src=None):
            dst = outs[a].at[_slot(block)]
            return pltpu.make_async_remote_copy(
                src_ref=dst if src is None else src, dst_ref=dst,
                send_sem=send_sems.at[a, k], recv_sem=recv_sems.at[a, k], device_id=to, device_id_type=MESH)

        mine = [pltpu.make_async_copy(scr[a], outs[a].at[_slot(me)], local_sems.at[a]) for a in range(n)]
        first = [copy(a, 1 + j, me, (*chip, c), src=scr[a]) for j, chip in enumerate(chips) for a in range(n)]
        first += [copy(a, 0, me, sibling, src=scr[a]) for a in range(n)]
        for cp in first + mine:
            cp.start()
        passed = []
        for j, chip in enumerate(chips):
            for a in range(n):
                copy(a, 1 + j, (*chip, c), me).wait_recv()
                cp = copy(a, 4 + j, (*chip, c), sibling)
                cp.start()
                passed.append(cp)
        for a in range(n):
            copy(a, 0, sibling, me).wait_recv()
        for j, chip in enumerate(chips):
            for a in range(n):
                copy(a, 4 + j, (*chip, 1 - c), me).wait_recv()
        for cp in first + passed:
            cp.wait_send()
        for cp in mine:
            cp.wait()

    return pl.pallas_call(
        body, name="weights_all_gather",
        in_specs=[pl.BlockSpec(memory_space=pltpu.VMEM)] * n,
        out_specs=[pl.BlockSpec(memory_space=pl.ANY)] * n,
        out_shape=[jax.ShapeDtypeStruct((N_DEV,) + s.shape, BF16) for s in shards],
        scratch_shapes=[pltpu.VMEM(s.shape, BF16) for s in shards]
        + [pltpu.SemaphoreType.DMA((n, 7)), pltpu.SemaphoreType.DMA((n, 7)), pltpu.SemaphoreType.DMA((n,))],
        compiler_params=pltpu.CompilerParams(vmem_limit_bytes=VMEM_LIMIT),
    )(*shards)


def _peers(x, y, c):
    out = []
    for flip in range(1, N_DEV):
        dx, dy, dc = (flip >> 2) & 1, (flip >> 1) & 1, flip & 1
        out.append((1 - x if dx else x, 1 - y if dy else y, 1 - c if dc else c))
    return out


def _exchange_partials(grads):
    n = len(grads)

    def body(*refs):
        ins, outs = refs[:n], refs[n:2 * n]
        send_sems, recv_sems, local_sems = refs[2 * n:]
        x, y, c = _mesh_place()
        me = _slot((x, y, c))
        peers = _peers(x, y, c)
        mine = [pltpu.make_async_copy(ins[a].at[me], outs[a].at[me], local_sems.at[a]) for a in range(n)]

        def copy(a, k, peer):
            return pltpu.make_async_remote_copy(
                src_ref=ins[a].at[_slot(peer)], dst_ref=outs[a].at[me],
                send_sem=send_sems.at[a, k], recv_sem=recv_sems.at[a, k], device_id=peer, device_id_type=MESH)

        def arrival(a, k, peer):
            return pltpu.make_async_remote_copy(
                src_ref=ins[a].at[_slot(peer)], dst_ref=outs[a].at[_slot(peer)],
                send_sem=send_sems.at[a, k], recv_sem=recv_sems.at[a, k], device_id=peer, device_id_type=MESH)

        sends = [copy(a, k, peer) for a in range(n) for k, peer in enumerate(peers)]
        for cp in sends + mine:
            cp.start()
        for a in range(n):
            for k, peer in enumerate(peers):
                arrival(a, k, peer).wait_recv()
        for cp in sends:
            cp.wait_send()
        for cp in mine:
            cp.wait()

    return pl.pallas_call(
        body, name="grads_exchange",
        in_specs=[pl.BlockSpec(memory_space=pl.ANY)] * n,
        out_specs=[pl.BlockSpec(memory_space=pl.ANY)] * n,
        out_shape=[jax.ShapeDtypeStruct(g.shape, g.dtype) for g in grads],
        scratch_shapes=[pltpu.SemaphoreType.DMA((n, 7)), pltpu.SemaphoreType.DMA((n, 7)),
                        pltpu.SemaphoreType.DMA((n,))],
    )(*grads)


def _adamw(w, g, m, v):
    m = ADAM_B1 * m + (1.0 - ADAM_B1) * g
    v = ADAM_B2 * v + (1.0 - ADAM_B2) * (g * g)
    m_hat = m / (1.0 - ADAM_B1 ** ADAM_STEP)
    v_hat = v / (1.0 - ADAM_B2 ** ADAM_STEP)
    return -ADAM_LR * (m_hat / (jnp.sqrt(v_hat) + ADAM_EPS) + ADAM_WD * w), m, v


def _sum_adamw(partials, w, m, v, rows, name):
    R, C = w.shape

    def body(p_ref, w_ref, m_ref, v_ref, g_ref, d_ref, nm_ref, nv_ref):
        g = p_ref[0].astype(F32)
        for k in range(1, N_DEV):
            g = g + p_ref[k].astype(F32)
        g_ref[...] = g
        d_ref[...], nm_ref[...], nv_ref[...] = _adamw(w_ref[...], g, m_ref[...], v_ref[...])

    tile = pl.BlockSpec((rows, C), lambda i: (i, 0))
    return pl.pallas_call(
        body, name=name, grid=(R // rows,),
        in_specs=[pl.BlockSpec((N_DEV, rows, C), lambda i: (0, i, 0)), tile, tile, tile],
        out_specs=[tile] * 4, out_shape=[jax.ShapeDtypeStruct((R, C), F32)] * 4,
        compiler_params=_params(1),
    )(partials, w, m, v)


def _small_allreduce_adamw(partials, loss, ws, ms, vs):
    D = ws["ffn1_pre_g"].shape[1]
    n_bin = -(-D_IN // D)
    n_sink = ws["sinks"].shape[1]
    rb_shape = ws["rel_bias"].shape
    row_bin = len(GAINS)
    row_sink = row_bin + n_bin
    row_loss = row_sink + 1
    row_rb = -(-(row_loss + 1) // 8) * 8
    n_rows = row_rb + -(-rb_shape[0] // 8) * 8
    bin_parts = [(r, min(D, D_IN - r * D)) for r in range(n_bin)]
    n_small = len(SMALL)

    def body(*refs):
        part = dict(zip(SMALL, refs[:n_small]))
        loss_ref = refs[n_small]
        pos = n_small + 1
        w_ref = dict(zip(SMALL, refs[pos:pos + n_small]))
        m_ref = dict(zip(SMALL, refs[pos + n_small:pos + 2 * n_small]))
        v_ref = dict(zip(SMALL, refs[pos + 2 * n_small:pos + 3 * n_small]))
        pos += 3 * n_small
        outs = {name: refs[pos + 4 * i:pos + 4 * i + 4] for i, name in enumerate(SMALL)}
        loss_out = refs[pos + 4 * n_small]
        pack, gath, send_sems, recv_sems = refs[pos + 4 * n_small + 1:]

        x, y, c = _mesh_place()
        me = _slot((x, y, c))
        peers = _peers(x, y, c)

        pack[...] = jnp.zeros_like(pack)
        for i, name in enumerate(GAINS):
            pack[i:i + 1, :] = part[name][...]
        for r, width in bin_parts:
            pack[row_bin + r:row_bin + r + 1, 0:width] = part["b_in"][:, r * D:r * D + width]
        pack[row_sink:row_sink + 1, 0:LANES] = part["sinks"][...]
        pack[row_loss:row_loss + 1, 0:LANES] = loss_ref[...]
        pack[row_rb:row_rb + rb_shape[0], 0:rb_shape[1]] = part["rel_bias"][...]
        gath[me] = pack[...]

        def copy(k, peer, block):
            return pltpu.make_async_remote_copy(
                src_ref=pack, dst_ref=gath.at[block], send_sem=send_sems.at[k], recv_sem=recv_sems.at[k],
                device_id=peer, device_id_type=MESH)

        sends = [copy(k, peer, me) for k, peer in enumerate(peers)]
        for cp in sends:
            cp.start()
        for k, peer in enumerate(peers):
            copy(k, peer, _slot(peer)).wait_recv()
        for cp in sends:
            cp.wait_send()

        total = gath[0]
        for k in range(1, N_DEV):
            total = total + gath[k]
        pack[...] = total

        def update(name, g):
            g_out, d_out, m_out, v_out = outs[name]
            g_out[...] = g
            d_out[...], m_out[...], v_out[...] = _adamw(w_ref[name][...], g, m_ref[name][...], v_ref[name][...])

        for i, name in enumerate(GAINS):
            update(name, pack[i:i + 1, :])
        update("b_in", jnp.concatenate([pack[row_bin + r:row_bin + r + 1, 0:width] for r, width in bin_parts], axis=1))
        update("sinks", pack[row_sink:row_sink + 1, 0:n_sink])
        update("rel_bias", pack[row_rb:row_rb + rb_shape[0], 0:rb_shape[1]])
        loss_out[...] = pack[row_loss:row_loss + 1, 0:LANES]

    args = [partials[k] for k in SMALL] + [loss]
    for group in (ws, ms, vs):
        args += [group[k] for k in SMALL]
    out_shape = []
    for name in SMALL:
        out_shape += [jax.ShapeDtypeStruct(ws[name].shape, F32)] * 4
    out_shape.append(jax.ShapeDtypeStruct((1, LANES), F32))
    res = pl.pallas_call(
        body, name="small_allreduce_adamw",
        in_specs=[pl.BlockSpec(memory_space=pltpu.VMEM)] * len(args),
        out_specs=[pl.BlockSpec(memory_space=pltpu.VMEM)] * len(out_shape),
        out_shape=out_shape,
        scratch_shapes=[pltpu.VMEM((n_rows, D), F32), pltpu.VMEM((N_DEV, n_rows, D), F32),
                        pltpu.SemaphoreType.DMA((7,)), pltpu.SemaphoreType.DMA((7,))],
    )(*args)
    per_name = {name: res[4 * i:4 * i + 4] for i, name in enumerate(SMALL)}
    return per_name, res[-1]


def _assemble_weights(gathered):
    g = dict(zip(BIG, gathered))
    D = g["w_in"].shape[1]
    out = {"ffn1_w_gu": g["ffn1_w_gu"], "ffn2_w_gu": g["ffn2_w_gu"], "w_ple_proj": g["w_ple_proj"]}
    for name in ("ffn1_w_down", "ffn2_w_down", "w_out", "w_ple_gate"):
        out[name] = g[name].reshape(-1, D)
    out["w_in"] = jnp.transpose(g["w_in"], (1, 0, 2)).reshape(D, D_IN)
    return out


def _adamw_rows(name, rows_total):
    if name.endswith("w_down"):
        return rows_total // 2
    return min(rows_total, 256)


def kernel(x, p, rel_bias, ffn1_pre_g, ffn1_w_gu, ffn1_w_down, ffn1_post_g, attn_pre_g, w_in, b_in, sinks, w_out, b_out, attn_post_g, ffn2_pre_g, ffn2_w_gu, ffn2_w_down, ffn2_post_g, ple_pre_g, w_ple_gate, w_ple_proj, ple_post_g, loss_target, m_rel_bias, m_ffn1_pre_g, m_ffn1_w_gu, m_ffn1_w_down, m_ffn1_post_g, m_attn_pre_g, m_w_in, m_b_in, m_sinks, m_w_out, m_b_out, m_attn_post_g, m_ffn2_pre_g, m_ffn2_w_gu, m_ffn2_w_down, m_ffn2_post_g, m_ple_pre_g, m_w_ple_gate, m_w_ple_proj, m_ple_post_g, v_rel_bias, v_ffn1_pre_g, v_ffn1_w_gu, v_ffn1_w_down, v_ffn1_post_g, v_attn_pre_g, v_w_in, v_b_in, v_sinks, v_w_out, v_b_out, v_attn_post_g, v_ffn2_pre_g, v_ffn2_w_gu, v_ffn2_w_down, v_ffn2_post_g, v_ple_pre_g, v_w_ple_gate, v_w_ple_proj, v_ple_post_g):
    given = dict(locals())
    ws = {k: given[k] for k in WEIGHTS}
    ms = {k: given["m_" + k] for k in WEIGHTS}
    vs = {k: given["v_" + k] for k in WEIGHTS}

    def shard(t):
        return t.reshape(t.shape[1:])

    gathered = _all_gather_bf16([shard(ws[k]) for k in BIG])
    grad_x, big, small_grads, loss = _local_step(
        shard(x), shard(shard(p)), shard(loss_target), {k: ws[k] for k in SMALL}, _assemble_weights(gathered))

    landed = dict(zip(BIG, _exchange_partials([big[k] for k in BIG])))
    result = {}
    for k in BIG:
        rows_total = ws[k].shape[1]
        outs = _sum_adamw(landed[k], shard(ws[k]), shard(ms[k]), shard(vs[k]), _adamw_rows(k, rows_total), k + "_adamw")
        result[k] = [o.reshape(ws[k].shape) for o in outs]
    small_res, loss_all = _small_allreduce_adamw(
        small_grads, loss, {k: ws[k] for k in SMALL}, {k: ms[k] for k in SMALL}, {k: vs[k] for k in SMALL})
    result.update(small_res)

    out = [loss_all[0, 0], grad_x.reshape(x.shape)]
    for i in range(4):
        out += [result[k][i] for k in WEIGHTS]
    return tuple(out)
```

```python
import functools
import math

import numpy as np
import jax
import jax.numpy as jnp
from jax import lax
from jax.experimental import pallas as pl
from jax.experimental.pallas import tpu as pltpu

F32 = jnp.float32
BF16 = jnp.bfloat16
MESH = pl.DeviceIdType.MESH

N_DEV = 8
EPS = 1e-6
NEG_INF = -1e30
HEAD_DIM = 64
LANES = 128
QBLK = 128
D_IN = 2304
A_Q, A_KV, B_W = 512, 128, 512
N_HEAD_GROUP = 8
NUM_BUCKETS = 32
MAX_DISTANCE = 2048
PATTERNS_A = ((1, 127),)
PATTERNS_B = ((1, 128), (4, 128), (16, 128))
Q_A_COL, K_A_COL, V_A_COL = 0, 4, 5
Q_B_COL, K_B_COL, V_B_COL = 6, 10, 14

ADAM_LR, ADAM_B1, ADAM_B2, ADAM_EPS, ADAM_WD, ADAM_STEP = 0.001, 0.9, 0.999, 1e-08, 0.01, 10

TOKEN_TILE = 512
BLOCK_UNROLL = 2
VMEM_LIMIT = 56 * 1024 * 1024
ARB = "arbitrary"

BIG = ("ffn1_w_gu", "ffn1_w_down", "w_in", "w_out", "ffn2_w_gu", "ffn2_w_down", "w_ple_gate", "w_ple_proj")
GAINS = ("ffn1_pre_g", "ffn1_post_g", "attn_pre_g", "attn_post_g", "ffn2_pre_g", "ffn2_post_g",
         "ple_pre_g", "ple_post_g", "b_out")
SMALL = GAINS + ("b_in", "sinks", "rel_bias")
WEIGHTS = ("rel_bias", "ffn1_pre_g", "ffn1_w_gu", "ffn1_w_down", "ffn1_post_g", "attn_pre_g", "w_in", "b_in",
           "sinks", "w_out", "b_out", "attn_post_g", "ffn2_pre_g", "ffn2_w_gu", "ffn2_w_down", "ffn2_post_g",
           "ple_pre_g", "w_ple_gate", "w_ple_proj", "ple_post_g")


def _params(n_axes):
    return pltpu.CompilerParams(dimension_semantics=(ARB,) * n_axes, vmem_limit_bytes=VMEM_LIMIT)


def _mm(a, b):
    return jnp.dot(a, b, preferred_element_type=F32)


def _mm_nt(a, b):
    return lax.dot_general(a, b, (((1,), (1,)), ((), ())), preferred_element_type=F32)


def _mm_tn(a, b):
    return lax.dot_general(a, b, (((0,), (0,)), ((), ())), preferred_element_type=F32)


def _rstd(x):
    return lax.rsqrt(jnp.mean(x * x, axis=-1, keepdims=True) + EPS)


def _rms_bwd(x, r, gain, dy):
    n = x * r
    gdy = dy * gain
    return r * (gdy - n * jnp.mean(gdy * n, axis=-1, keepdims=True)), dy * n


def _colsum(v):
    return jnp.sum(v, axis=0, keepdims=True)


def _full(shape):
    return pl.BlockSpec(shape, lambda *_: (0,) * len(shape))


def _mesh_place():
    return lax.axis_index("x"), lax.axis_index("y"), lax.axis_index("c")


def _slot(dev):
    return 4 * dev[0] + 2 * dev[1] + dev[2]


def _peers(x, y, c):
    out = []
    for flip in range(1, N_DEV):
        dx, dy, dc = (flip >> 2) & 1, (flip >> 1) & 1, flip & 1
        out.append((1 - x if dx else x, 1 - y if dy else y, 1 - c if dc else c))
    return out


def _side_copies(kind, ins, outs, send_sems, recv_sems, local_sems, sem_row=0):
    n = len(ins)
    x, y, c = _mesh_place()
    me = _slot((x, y, c))
    peers = _peers(x, y, c)

    def src(a, block):
        return ins[a] if kind == "gather" else ins[a].at[block]

    def send(a, k, peer):
        return pltpu.make_async_remote_copy(
            src_ref=src(a, _slot(peer)), dst_ref=outs[a].at[me],
            send_sem=send_sems.at[sem_row + a, k], recv_sem=recv_sems.at[sem_row + a, k],
            device_id=peer, device_id_type=MESH)

    def arrival(a, k, peer):
        return pltpu.make_async_remote_copy(
            src_ref=src(a, _slot(peer)), dst_ref=outs[a].at[_slot(peer)],
            send_sem=send_sems.at[sem_row + a, k], recv_sem=recv_sems.at[sem_row + a, k],
            device_id=peer, device_id_type=MESH)

    def own(a):
        return pltpu.make_async_copy(src(a, me), outs[a].at[me], local_sems.at[sem_row + a])

    def start():
        for k, peer in enumerate(peers):
            for a in range(n):
                send(a, k, peer).start()
        for a in range(n):
            own(a).start()

    def wait():
        for k, peer in enumerate(peers):
            for a in range(n):
                arrival(a, k, peer).wait_recv()
        for k, peer in enumerate(peers):
            for a in range(n):
                send(a, k, peer).wait_send()
        for a in range(n):
            own(a).wait()

    return start, wait


def _side_out_shapes(kind, arrays):
    if kind == "gather":
        return [jax.ShapeDtypeStruct((N_DEV,) + a.shape, a.dtype) for a in arrays]
    return [jax.ShapeDtypeStruct(a.shape, a.dtype) for a in arrays]


def _hosted_call(body, name, grid, in_specs, out_specs, out_shape, scratch_shapes, args, side=None):
    if side is None:
        outs = pl.pallas_call(
            body, name=name, grid=grid, in_specs=in_specs, out_specs=out_specs, out_shape=out_shape,
            scratch_shapes=scratch_shapes, compiler_params=_params(len(grid)))(*args)
        return outs, []
    kind, arrays = side
    n_in, n_out, n_scr, n_side = len(in_specs), len(out_specs), len(scratch_shapes), len(arrays)

    def hosted(*refs):
        pos = 0
        groups = []
        for size in (n_in, n_side, n_out, n_side, n_scr):
            groups.append(refs[pos:pos + size])
            pos += size
        ins, side_in, outs, side_out, scr = groups
        send_sems, recv_sems, local_sems = refs[pos:]
        ids = [pl.program_id(d) for d in range(len(grid))]
        is_first = functools.reduce(jnp.logical_and, [i == 0 for i in ids])
        is_last = functools.reduce(jnp.logical_and, [i == g - 1 for i, g in zip(ids, grid)])
        start, wait = _side_copies(kind, side_in, side_out, send_sems, recv_sems, local_sems)
        pl.when(is_first)(start)
        body(*ins, *outs, *scr)
        pl.when(is_last)(wait)

    any_spec = pl.BlockSpec(memory_space=pl.ANY)
    outs = pl.pallas_call(
        hosted, name=name, grid=grid,
        in_specs=list(in_specs) + [any_spec] * n_side,
        out_specs=list(out_specs) + [any_spec] * n_side,
        out_shape=list(out_shape) + _side_out_shapes(kind, arrays),
        scratch_shapes=list(scratch_shapes) + [pltpu.SemaphoreType.DMA((n_side, 7)), pltpu.SemaphoreType.DMA((n_side, 7)),
                                               pltpu.SemaphoreType.DMA((n_side,))],
        compiler_params=_params(len(grid)))(*args, *arrays)
    return outs[:n_out], outs[n_out:]


def _ffn_fwd(h, g_pre, g_post, w_gu, w_down, name, side=None):
    T, D = h.shape
    nj = w_gu.shape[0] // 2
    FB = w_gu.shape[2]
    tm = TOKEN_TILE

    def body(h_ref, gpre_ref, gpost_ref, wg_ref, wu_ref, wd_ref, hout_ref, f_ref, a_ref, gu_ref, a_scr, acc):
        j = pl.program_id(1)

        @pl.when(j == 0)
        def _():
            x = h_ref[...]
            a = (x * _rstd(x) * gpre_ref[...]).astype(BF16)
            a_scr[...] = a
            a_ref[...] = a
            acc[...] = jnp.zeros_like(acc)

        a = a_scr[...]
        g = _mm(a, wg_ref[...])
        u = _mm(a, wu_ref[...])
        gu_ref[0] = g.astype(BF16)
        gu_ref[1] = u.astype(BF16)
        hh = (g * jax.nn.sigmoid(g) * u).astype(BF16)
        acc[...] += _mm(hh, wd_ref[...])

        @pl.when(j == nj - 1)
        def _():
            f = acc[...]
            f_ref[...] = f
            hout_ref[...] = h_ref[...] + 0.5 * (f * _rstd(f) * gpost_ref[...])

    return _hosted_call(
        body, name, (T // tm, nj),
        in_specs=[
            pl.BlockSpec((tm, D), lambda i, j: (i, 0)),
            _full((1, D)), _full((1, D)),
            pl.BlockSpec((None, D, FB), lambda i, j: (j, 0, 0)),
            pl.BlockSpec((None, D, FB), lambda i, j: (j + nj, 0, 0)),
            pl.BlockSpec((FB, D), lambda i, j: (j, 0)),
        ],
        out_specs=[
            pl.BlockSpec((tm, D), lambda i, j: (i, 0)),
            pl.BlockSpec((tm, D), lambda i, j: (i, 0)),
            pl.BlockSpec((tm, D), lambda i, j: (i, 0)),
            pl.BlockSpec((None, 2, tm, FB), lambda i, j: (j, 0, i, 0)),
        ],
        out_shape=[
            jax.ShapeDtypeStruct((T, D), F32),
            jax.ShapeDtypeStruct((T, D), F32),
            jax.ShapeDtypeStruct((T, D), BF16),
            jax.ShapeDtypeStruct((nj, 2, T, FB), BF16),
        ],
        scratch_shapes=[pltpu.VMEM((tm, D), BF16), pltpu.VMEM((tm, D), F32)],
        args=(h, g_pre, g_post, w_gu, w_gu, w_down), side=side)


def _ffn_bwd(dh_out, f, g_post, h, g_pre, gu, w_gu, w_down, name):
    T, D = h.shape
    nj = w_gu.shape[0] // 2
    FB = w_gu.shape[2]
    tm = TOKEN_TILE

    def body(dho_ref, f_ref, gpost_ref, h_ref, gpre_ref, gu_ref, wg_ref, wu_ref, wd_ref,
             dhin_ref, df_ref, hh_ref, dgu_ref, dgpost_ref, dgpre_ref, df_scr, da):
        i, j = pl.program_id(0), pl.program_id(1)

        @pl.when(jnp.logical_and(i == 0, j == 0))
        def _():
            dgpost_ref[...] = jnp.zeros_like(dgpost_ref)
            dgpre_ref[...] = jnp.zeros_like(dgpre_ref)

        @pl.when(j == 0)
        def _():
            fv = f_ref[...]
            df, dgain = _rms_bwd(fv, _rstd(fv), gpost_ref[...], 0.5 * dho_ref[...])
            dgpost_ref[...] += _colsum(dgain)
            dfb = df.astype(BF16)
            df_scr[...] = dfb
            df_ref[...] = dfb
            da[...] = jnp.zeros_like(da)

        dhh = _mm_nt(df_scr[...], wd_ref[...])
        g = gu_ref[0].astype(F32)
        u = gu_ref[1].astype(F32)
        sg = jax.nn.sigmoid(g)
        silu = g * sg
        hh_ref[...] = (silu * u).astype(BF16)
        dg = (dhh * u * (sg * (1.0 + g * (1.0 - sg)))).astype(BF16)
        du = (dhh * silu).astype(BF16)
        dgu_ref[0] = dg
        dgu_ref[1] = du
        da[...] += _mm_nt(dg, wg_ref[...]) + _mm_nt(du, wu_ref[...])

        @pl.when(j == nj - 1)
        def _():
            x = h_ref[...]
            dx, dgain = _rms_bwd(x, _rstd(x), gpre_ref[...], da[...])
            dgpre_ref[...] += _colsum(dgain)
            dhin_ref[...] = dho_ref[...] + dx

    tile = pl.BlockSpec((tm, D), lambda i, j: (i, 0))
    return pl.pallas_call(
        body, name=name, grid=(T // tm, nj),
        in_specs=[
            tile, tile, _full((1, D)), tile, _full((1, D)),
            pl.BlockSpec((None, 2, tm, FB), lambda i, j: (j, 0, i, 0)),
            pl.BlockSpec((None, D, FB), lambda i, j: (j, 0, 0)),
            pl.BlockSpec((None, D, FB), lambda i, j: (j + nj, 0, 0)),
            pl.BlockSpec((FB, D), lambda i, j: (j, 0)),
        ],
        out_specs=[
            tile, tile,
            pl.BlockSpec((None, tm, FB), lambda i, j: (j, i, 0)),
            pl.BlockSpec((None, 2, tm, FB), lambda i, j: (j, 0, i, 0)),
            _full((1, D)), _full((1, D)),
        ],
        out_shape=[
            jax.ShapeDtypeStruct((T, D), F32),
            jax.ShapeDtypeStruct((T, D), BF16),
            jax.ShapeDtypeStruct((nj, T, FB), BF16),
            jax.ShapeDtypeStruct((nj, 2, T, FB), BF16),
            jax.ShapeDtypeStruct((1, D), F32),
            jax.ShapeDtypeStruct((1, D), F32),
        ],
        scratch_shapes=[pltpu.VMEM((tm, D), BF16), pltpu.VMEM((tm, D), F32)],
        compiler_params=_params(2),
    )(dh_out, f, g_post, h, g_pre, gu, w_gu, w_gu, w_down)


def _tn_matmul(x, y, x_spec, y_spec, out_shape, out_spec, n_blocks, n_steps, acc_shape, name, side=None):
    def body(x_ref, y_ref, o_ref, acc):
        t = pl.program_id(1)

        @pl.when(t == 0)
        def _():
            acc[...] = jnp.zeros_like(acc)

        acc[...] += _mm_tn(x_ref[...].astype(BF16), y_ref[...].astype(BF16))

        @pl.when(t == n_steps - 1)
        def _():
            o_ref[...] = acc[...].astype(o_ref.dtype)

    outs, side_outs = _hosted_call(
        body, name, (n_blocks, n_steps), in_specs=[x_spec, y_spec], out_specs=[out_spec], out_shape=[out_shape],
        scratch_shapes=[pltpu.VMEM(acc_shape, F32)], args=(x, y), side=side)
    return (outs[0], side_outs) if side is not None else outs[0]


def _inproj_fwd(h, g_pre, w_in, b_in):
    T, D = h.shape
    tm = TOKEN_TILE

    def body(h_ref, g_ref, w_ref, b_ref, z_ref, a_ref):
        x = h_ref[...]
        a = (x * _rstd(x) * g_ref[...]).astype(BF16)
        a_ref[...] = a
        z_ref[...] = _mm(a, w_ref[...]) + b_ref[...]

    return pl.pallas_call(
        body, name="inproj_fwd", grid=(T // tm,),
        in_specs=[pl.BlockSpec((tm, D), lambda i: (i, 0)), _full((1, D)), _full((D, D_IN)), _full((1, D_IN))],
        out_specs=[pl.BlockSpec((tm, D_IN), lambda i: (i, 0)), pl.BlockSpec((tm, D), lambda i: (i, 0))],
        out_shape=[jax.ShapeDtypeStruct((T, D_IN), F32), jax.ShapeDtypeStruct((T, D), BF16)],
        compiler_params=_params(1),
    )(h, g_pre, w_in, b_in)


def _inproj_bwd(dqa, dka, dva, dqb, dkb, dvb, w_in, h, g_pre, dres):
    T, D = h.shape
    tm = TOKEN_TILE

    def body(dqa_ref, dka_ref, dva_ref, dqb_ref, dkb_ref, dvb_ref, w_ref, h_ref, g_ref, dres_ref,
             dh_ref, dz_ref, dbin_ref, dg_ref):
        i = pl.program_id(0)

        @pl.when(i == 0)
        def _():
            dbin_ref[...] = jnp.zeros_like(dbin_ref)
            dg_ref[...] = jnp.zeros_like(dg_ref)

        dz = jnp.concatenate([dqa_ref[...], dka_ref[...], dva_ref[...], dqb_ref[...], dkb_ref[...], dvb_ref[...]],
                             axis=1)
        dbin_ref[...] += _colsum(dz)
        dzb = dz.astype(BF16)
        dz_ref[...] = dzb
        da = _mm_nt(dzb, w_ref[...])
        x = h_ref[...]
        dx, dgain = _rms_bwd(x, _rstd(x), g_ref[...], da)
        dg_ref[...] += _colsum(dgain)
        dh_ref[...] = dres_ref[...] + dx

    def tile(w):
        return pl.BlockSpec((tm, w), lambda i: (i, 0))

    return pl.pallas_call(
        body, name="inproj_bwd", grid=(T // tm,),
        in_specs=[tile(A_Q), tile(A_KV), tile(A_KV), tile(B_W), tile(B_W), tile(B_W),
                  _full((D, D_IN)), tile(D), _full((1, D)), tile(D)],
        out_specs=[tile(D), tile(D_IN), _full((1, D_IN)), _full((1, D))],
        out_shape=[jax.ShapeDtypeStruct((T, D), F32), jax.ShapeDtypeStruct((T, D_IN), BF16),
                   jax.ShapeDtypeStruct((1, D_IN), F32), jax.ShapeDtypeStruct((1, D), F32)],
        compiler_params=_params(1),
    )(dqa, dka, dva, dqb, dkb, dvb, w_in, h, g_pre, dres)


def _bucket_tiles(patterns):
    i = np.arange(QBLK)[:, None]
    j = np.arange(2 * QBLK)[None, :]
    dist = QBLK + i - j
    max_exact = NUM_BUCKETS // 2
    tiles = []
    for dilation, max_dist in patterns:
        n = np.maximum(dist * dilation, 0)
        nf = np.maximum(n, 1).astype(np.float32)
        large = max_exact + (np.log(nf / np.float32(max_exact)) / np.float32(math.log(MAX_DISTANCE / max_exact))
                             * np.float32(NUM_BUCKETS - max_exact)).astype(np.int32)
        bucket = np.where(n < max_exact, n, np.minimum(large, NUM_BUCKETS - 1))
        tiles.append(np.where((dist >= 0) & (dist <= max_dist), bucket, -1))
    return jnp.asarray(np.stack(tiles).astype(np.int32))


def _bias_build(rel_bias, buckets, head0, name):
    n = buckets.shape[0]

    def body(bk_ref, rb_ref, o_ref):
        bk = bk_ref[...]
        base = jnp.where(bk < 0, NEG_INF, 0.0).astype(F32)
        for hd in range(N_HEAD_GROUP):
            o_ref[hd] = lax.fori_loop(
                0, NUM_BUCKETS, lambda b, acc, hd=hd: jnp.where(bk == b, rb_ref[b, head0 + hd], acc), base)

    return pl.pallas_call(
        body, name=name, grid=(n,),
        in_specs=[pl.BlockSpec((None, QBLK, 2 * QBLK), lambda p: (p, 0, 0)), pl.BlockSpec(memory_space=pltpu.SMEM)],
        out_specs=pl.BlockSpec((None, N_HEAD_GROUP, QBLK, 2 * QBLK), lambda p: (p, 0, 0, 0)),
        out_shape=jax.ShapeDtypeStruct((n, N_HEAD_GROUP, QBLK, 2 * QBLK), F32),
        compiler_params=_params(1),
    )(buckets, rel_bias)


def _bias_grad(ds, buckets, name):
    n = buckets.shape[0]

    def body(ds_ref, bk_ref, o_ref):
        bk = bk_ref[...]
        row = lax.broadcasted_iota(jnp.int32, (NUM_BUCKETS, 2 * QBLK), 0)
        for hd in range(N_HEAD_GROUP):
            d = ds_ref[hd]
            per_key = jnp.zeros((NUM_BUCKETS, 2 * QBLK), F32)
            for b in range(NUM_BUCKETS):
                per_key = jnp.where(row == b, jnp.sum(jnp.where(bk == b, d, 0.0), axis=0, keepdims=True), per_key)
            o_ref[hd] = jnp.broadcast_to(jnp.sum(per_key, axis=1, keepdims=True), (NUM_BUCKETS, LANES))

    out = pl.pallas_call(
        body, name=name, grid=(n,),
        in_specs=[pl.BlockSpec((None, N_HEAD_GROUP, QBLK, 2 * QBLK), lambda p: (p, 0, 0, 0)),
                  pl.BlockSpec((None, QBLK, 2 * QBLK), lambda p: (p, 0, 0))],
        out_specs=pl.BlockSpec((None, N_HEAD_GROUP, NUM_BUCKETS, LANES), lambda p: (p, 0, 0, 0)),
        out_shape=jax.ShapeDtypeStruct((n, N_HEAD_GROUP, NUM_BUCKETS, LANES), F32),
        compiler_params=_params(1),
    )(ds, buckets)
    return out[:, :, :, 0].reshape(n * N_HEAD_GROUP, NUM_BUCKETS)


def _class_rows(start, dilation):
    if dilation == 1:
        return pl.ds(pl.multiple_of(start, QBLK), QBLK)
    return pl.ds(start, QBLK, stride=dilation)


def _block_starts(idx, n_blocks, dilation):
    cls = idx // n_blocks
    n = idx % n_blocks
    cur = cls + dilation * QBLK * n
    prev = cls + dilation * QBLK * jnp.maximum(n - 1, 0)
    return n, cur, prev


def _attn_specs(T, qcol, kcol, vcol, shared_kv):
    kv = (lambda c: (lambda g: (0, c))) if shared_kv else (lambda c: (lambda g: (0, c + g)))
    return [pl.BlockSpec((T, LANES), lambda g: (0, qcol + g)),
            pl.BlockSpec((T, LANES), kv(kcol)),
            pl.BlockSpec((T, LANES), kv(vcol))]


def _attn_fwd(z, bias, sinks, patterns, qcol, kcol, vcol, shared_kv, name, side=None):
    T = z.shape[0]
    n_pat = len(patterns)
    has_sink = sinks is not None

    def body(*refs):
        if has_sink:
            sink_ref, refs = refs[0], refs[1:]
        q_ref, k_ref, v_ref, b_ref, o_ref, l_ref = refs[:6]
        po_scr, pl_scr = refs[6:6 + n_pat], refs[6 + n_pat:]
        g = pl.program_id(0)
        in_prev = lax.broadcasted_iota(jnp.int32, (QBLK, 2 * QBLK), 1) < QBLK
        upper = (g // 2) == 1

        def head(t, hd):
            if shared_kv:
                return jnp.where(upper, t[:, HEAD_DIM:], t[:, :HEAD_DIM])
            return t[:, HEAD_DIM * hd:HEAD_DIM * (hd + 1)]

        for pi, (dilation, _) in enumerate(patterns):
            n_blocks = T // (QBLK * dilation)

            def step(idx, carry, pi=pi, dilation=dilation, n_blocks=n_blocks):
                n, cur, prev = _block_starts(idx, n_blocks, dilation)
                rows_c, rows_p = _class_rows(cur, dilation), _class_rows(prev, dilation)
                q = q_ref[rows_c, :]
                k2 = jnp.concatenate([k_ref[rows_p, :], k_ref[rows_c, :]], axis=0)
                v2 = jnp.concatenate([v_ref[rows_p, :], v_ref[rows_c, :]], axis=0)
                no_prev = jnp.logical_and(in_prev, n == 0)
                outs, lses = [], []
                for hd in range(2):
                    qh = q[:, HEAD_DIM * hd:HEAD_DIM * (hd + 1)].astype(BF16)
                    kh = head(k2, hd).astype(BF16)
                    vh = head(v2, hd).astype(BF16)
                    s = _mm_nt(qh, kh) * (HEAD_DIM ** -0.5) + b_ref[pi, hd]
                    s = jnp.where(no_prev, NEG_INF, s)
                    m = jnp.max(s, axis=1, keepdims=True)
                    pr = jnp.exp(s - m)
                    den = jnp.sum(pr, axis=1, keepdims=True)
                    outs.append(_mm(pr.astype(BF16), vh) / den)
                    lses.append(jnp.broadcast_to(m + jnp.log(den), (QBLK, HEAD_DIM)))
                po_scr[pi][rows_c, :] = jnp.concatenate(outs, axis=1)
                pl_scr[pi][rows_c, :] = jnp.concatenate(lses, axis=1)
                return carry

            lax.fori_loop(0, dilation * n_blocks, step, 0, unroll=BLOCK_UNROLL)

        if has_sink:
            lane = lax.broadcasted_iota(jnp.int32, (1, LANES), 1)
            sink = jnp.where(lane < HEAD_DIM, sink_ref[0, 2 * g], sink_ref[0, 2 * g + 1])

        def merge(ci, carry):
            rows = pl.ds(pl.multiple_of(ci * QBLK, QBLK), QBLK)
            parts = [pl_scr[pi][rows, :] for pi in range(n_pat)]
            m = functools.reduce(jnp.maximum, parts)
            if has_sink:
                m = jnp.maximum(m, sink)
            den = functools.reduce(jnp.add, [jnp.exp(x - m) for x in parts])
            if has_sink:
                den = den + jnp.exp(sink - m)
            lse = m + jnp.log(den)
            o_ref[rows, :] = functools.reduce(
                jnp.add, [jnp.exp(parts[pi] - lse) * po_scr[pi][rows, :] for pi in range(n_pat)])
            l_ref[rows, :] = lse
            return carry

        lax.fori_loop(0, T // QBLK, merge, 0)

    in_specs = _attn_specs(T, qcol, kcol, vcol, shared_kv)
    in_specs.append(pl.BlockSpec((n_pat, 2, QBLK, 2 * QBLK), lambda g: (0, g, 0, 0)))
    args = [z, z, z, bias]
    if has_sink:
        in_specs.insert(0, pl.BlockSpec(memory_space=pltpu.SMEM))
        args.insert(0, sinks)
    out = pl.BlockSpec((T, LANES), lambda g: (0, g))
    return _hosted_call(
        body, name, (N_HEAD_GROUP // 2,),
        in_specs=in_specs, out_specs=[out, out],
        out_shape=[jax.ShapeDtypeStruct((T, N_HEAD_GROUP * HEAD_DIM), F32)] * 2,
        scratch_shapes=[pltpu.VMEM((T, LANES), F32)] * (2 * n_pat), args=args, side=side)


def _attn_bwd(z, bias, sinks, d_out, out, lse, patterns, qcol, kcol, vcol, shared_kv, name, side=None):
    T = z.shape[0]
    n_pat = len(patterns)
    has_sink = sinks is not None
    kv_width = LANES if shared_kv else N_HEAD_GROUP * HEAD_DIM

    def body(*refs):
        if has_sink:
            sink_ref, refs = refs[0], refs[1:]
        q_ref, k_ref, v_ref, b_ref, do_ref, o_ref, l_ref = refs[:7]
        dq_ref, dk_ref, dv_ref, ds_ref = refs[7:11]
        dsink_ref = refs[11] if has_sink else None
        dk_acc, dv_acc = refs[-2:]
        g = pl.program_id(0)
        in_prev = lax.broadcasted_iota(jnp.int32, (QBLK, 2 * QBLK), 1) < QBLK
        lane = lax.broadcasted_iota(jnp.int32, (1, LANES), 1)
        upper = (g // 2) == 1
        own_half = (lane >= HEAD_DIM).astype(jnp.int32) == (g // 2)

        dq_ref[...] = jnp.zeros_like(dq_ref)
        ds_ref[...] = jnp.zeros_like(ds_ref)
        dk_acc[...] = jnp.zeros_like(dk_acc)
        dv_acc[...] = jnp.zeros_like(dv_acc)

        def head(t, hd):
            if shared_kv:
                return jnp.where(upper, t[:, HEAD_DIM:], t[:, :HEAD_DIM])
            return t[:, HEAD_DIM * hd:HEAD_DIM * (hd + 1)]

        def to_lanes(parts):
            if shared_kv:
                both = parts[0] + parts[1]
                return jnp.where(own_half, jnp.concatenate([both, both], axis=1), 0.0)
            return jnp.concatenate(parts, axis=1)

        dsink = jnp.zeros((1, LANES), F32)
        for pi, (dilation, _) in enumerate(patterns):
            n_blocks = T // (QBLK * dilation)

            def step(idx, dsink, pi=pi, dilation=dilation, n_blocks=n_blocks):
                n, cur, prev = _block_starts(idx, n_blocks, dilation)
                rows_c, rows_p = _class_rows(cur, dilation), _class_rows(prev, dilation)
                q = q_ref[rows_c, :]
                k2 = jnp.concatenate([k_ref[rows_p, :], k_ref[rows_c, :]], axis=0)
                v2 = jnp.concatenate([v_ref[rows_p, :], v_ref[rows_c, :]], axis=0)
                d_o = do_ref[rows_c, :]
                o = o_ref[rows_c, :]
                l = l_ref[rows_c, :]
                no_prev = jnp.logical_and(in_prev, n == 0)
                dqs, dks, dvs = [], [], []
                for hd in range(2):
                    cols = slice(HEAD_DIM * hd, HEAD_DIM * (hd + 1))
                    qh = q[:, cols].astype(BF16)
                    kh = head(k2, hd).astype(BF16)
                    vh = head(v2, hd).astype(BF16)
                    doh = d_o[:, cols]
                    delta = jnp.sum(doh * o[:, cols], axis=1, keepdims=True)
                    lh = l[:, HEAD_DIM * hd:HEAD_DIM * hd + 1]
                    s = _mm_nt(qh, kh) * (HEAD_DIM ** -0.5) + b_ref[pi, hd]
                    s = jnp.where(no_prev, NEG_INF, s)
                    pr = jnp.exp(s - lh)
                    dob = doh.astype(BF16)
                    ds = pr * (_mm_nt(dob, vh) - delta)
                    ds_ref[pi, hd] += ds
                    dsb = ds.astype(BF16)
                    dqs.append(_mm(dsb, kh) * (HEAD_DIM ** -0.5))
                    dks.append(_mm_tn(dsb, qh) * (HEAD_DIM ** -0.5))
                    dvs.append(_mm_tn(pr.astype(BF16), dob))
                    if has_sink:
                        p_sink = jnp.exp(sink_ref[0, 2 * g + hd] - lh)
                        dsink = dsink - jnp.where(lane == 2 * g + hd, jnp.sum(p_sink * delta), 0.0)
                dq_ref[rows_c, :] += jnp.concatenate(dqs, axis=1)
                dk2 = to_lanes(dks)
                dv2 = to_lanes(dvs)
                dk_acc[rows_p, :] += dk2[:QBLK]
                dk_acc[rows_c, :] += dk2[QBLK:]
                dv_acc[rows_p, :] += dv2[:QBLK]
                dv_acc[rows_c, :] += dv2[QBLK:]
                return dsink

            dsink = lax.fori_loop(0, dilation * n_blocks, step, dsink, unroll=BLOCK_UNROLL)

        if shared_kv:
            @pl.when(g == 0)
            def _():
                dk_ref[...] = dk_acc[...]
                dv_ref[...] = dv_acc[...]

            @pl.when(g != 0)
            def _():
                dk_ref[...] += dk_acc[...]
                dv_ref[...] += dv_acc[...]
        else:
            dk_ref[...] = dk_acc[...]
            dv_ref[...] = dv_acc[...]

        if has_sink:
            @pl.when(g == 0)
            def _():
                dsink_ref[...] = dsink

            @pl.when(g != 0)
            def _():
                dsink_ref[...] += dsink

    pair = pl.BlockSpec((T, LANES), lambda g: (0, g))
    in_specs = _attn_specs(T, qcol, kcol, vcol, shared_kv)
    in_specs += [pl.BlockSpec((n_pat, 2, QBLK, 2 * QBLK), lambda g: (0, g, 0, 0)), pair, pair, pair]
    args = [z, z, z, bias, d_out, out, lse]
    kv_out = _full((T, LANES)) if shared_kv else pair
    out_specs = [pair, kv_out, kv_out, pl.BlockSpec((n_pat, 2, QBLK, 2 * QBLK), lambda g: (0, g, 0, 0))]
    out_shape = [jax.ShapeDtypeStruct((T, N_HEAD_GROUP * HEAD_DIM), F32),
                 jax.ShapeDtypeStruct((T, kv_width), F32), jax.ShapeDtypeStruct((T, kv_width), F32),
                 jax.ShapeDtypeStruct((n_pat, N_HEAD_GROUP, QBLK, 2 * QBLK), F32)]
    if has_sink:
        in_specs.insert(0, pl.BlockSpec(memory_space=pltpu.SMEM))
        args.insert(0, sinks)
        out_specs.append(_full((1, LANES)))
        out_shape.append(jax.ShapeDtypeStruct((1, LANES), F32))
    return _hosted_call(
        body, name, (N_HEAD_GROUP // 2,), in_specs=in_specs, out_specs=out_specs, out_shape=out_shape,
        scratch_shapes=[pltpu.VMEM((T, LANES), F32), pltpu.VMEM((T, LANES), F32)], args=args, side=side)


def _outproj_fwd(mix_a, mix_b, w_out, b_out, g_post, h):
    T, D = h.shape
    tm = TOKEN_TILE
    d_mix = w_out.shape[0]

    def body(ma_ref, mb_ref, w_ref, b_ref, g_ref, h_ref, att_ref, hout_ref, mix_ref):
        mix = jnp.concatenate([ma_ref[...], mb_ref[...]], axis=1).astype(BF16)
        mix_ref[...] = mix
        att = _mm(mix, w_ref[...]) + b_ref[...]
        att_ref[...] = att
        hout_ref[...] = h_ref[...] + att * _rstd(att) * g_ref[...]

    def tile(w):
        return pl.BlockSpec((tm, w), lambda i: (i, 0))

    return pl.pallas_call(
        body, name="outproj_fwd", grid=(T // tm,),
        in_specs=[tile(A_Q), tile(B_W), _full((d_mix, D)), _full((1, D)), _full((1, D)), tile(D)],
        out_specs=[tile(D), tile(D), tile(d_mix)],
        out_shape=[jax.ShapeDtypeStruct((T, D), F32), jax.ShapeDtypeStruct((T, D), F32),
                   jax.ShapeDtypeStruct((T, d_mix), BF16)],
        compiler_params=_params(1),
    )(mix_a, mix_b, w_out, b_out, g_post, h)


def _outproj_bwd(dh, att, g_post, w_out):
    T, D = dh.shape
    tm = TOKEN_TILE
    d_mix = w_out.shape[0]

    def body(dh_ref, att_ref, g_ref, w_ref, dma_ref, dmb_ref, datt_ref, dg_ref, db_ref):
        i = pl.program_id(0)

        @pl.when(i == 0)
        def _():
            dg_ref[...] = jnp.zeros_like(dg_ref)
            db_ref[...] = jnp.zeros_like(db_ref)

        att = att_ref[...]
        datt, dgain = _rms_bwd(att, _rstd(att), g_ref[...], dh_ref[...])
        dg_ref[...] += _colsum(dgain)
        db_ref[...] += _colsum(datt)
        dattb = datt.astype(BF16)
        datt_ref[...] = dattb
        dmix = _mm_nt(dattb, w_ref[...])
        dma_ref[...] = dmix[:, :A_Q]
        dmb_ref[...] = dmix[:, A_Q:]

    def tile(w):
        return pl.BlockSpec((tm, w), lambda i: (i, 0))

    return pl.pallas_call(
        body, name="outproj_bwd", grid=(T // tm,),
        in_specs=[tile(D), tile(D), _full((1, D)), _full((d_mix, D))],
        out_specs=[tile(A_Q), tile(B_W), tile(D), _full((1, D)), _full((1, D))],
        out_shape=[jax.ShapeDtypeStruct((T, A_Q), F32), jax.ShapeDtypeStruct((T, B_W), F32),
                   jax.ShapeDtypeStruct((T, D), BF16), jax.ShapeDtypeStruct((1, D), F32),
                   jax.ShapeDtypeStruct((1, D), F32)],
        compiler_params=_params(1),
    )(dh, att, g_post, w_out)


def _ple_fwd_loss(h, g_pre, w_gate, p, w_proj, g_post, target):
    T, D = h.shape
    tm = TOKEN_TILE
    n_proj, ple, db = w_proj.shape

    def body(h_ref, gpre_ref, wg_ref, p_ref, wp_ref, gpost_ref, t_ref,
             a_ref, dpre_ref, de_ref, dh_ref, loss_ref, dgpost_ref):
        i = pl.program_id(0)

        @pl.when(i == 0)
        def _():
            loss_ref[...] = jnp.zeros_like(loss_ref)
            dgpost_ref[...] = jnp.zeros_like(dgpost_ref)

        x = h_ref[...]
        a = (x * _rstd(x) * gpre_ref[...]).astype(BF16)
        a_ref[...] = a
        gate = jax.nn.sigmoid(_mm(a, wg_ref[...]))
        pb = p_ref[...].astype(BF16)
        e = jnp.concatenate([_mm(pb, wp_ref[k]) for k in range(n_proj)], axis=1)
        ge = gate * e
        rg = _rstd(ge)
        diff = x + ge * rg * gpost_ref[...] - t_ref[...]
        loss_ref[...] += 0.5 * jnp.sum(jnp.mean(diff * diff, axis=1, keepdims=True))
        dy = diff * (1.0 / D)
        dh_ref[...] = dy
        dge, dgain = _rms_bwd(ge, rg, gpost_ref[...], dy)
        dgpost_ref[...] += _colsum(dgain)
        de_ref[...] = (dge * gate).astype(BF16)
        dpre_ref[...] = (dge * e * gate * (1.0 - gate)).astype(BF16)

    def tile(w):
        return pl.BlockSpec((tm, w), lambda i: (i, 0))

    return pl.pallas_call(
        body, name="ple_fwd_loss", grid=(T // tm,),
        in_specs=[tile(D), _full((1, D)), _full((D, D)), tile(ple), _full((n_proj, ple, db)), _full((1, D)), tile(D)],
        out_specs=[tile(D), tile(D), tile(D), tile(D), _full((1, LANES)), _full((1, D))],
        out_shape=[jax.ShapeDtypeStruct((T, D), BF16),
                   jax.ShapeDtypeStruct((T, D), BF16),
                   jax.ShapeDtypeStruct((T, D), BF16),
                   jax.ShapeDtypeStruct((T, D), F32),
                   jax.ShapeDtypeStruct((1, LANES), F32),
                   jax.ShapeDtypeStruct((1, D), F32)],
        compiler_params=_params(1),
    )(h, g_pre, w_gate, p, w_proj, g_post, target)


def _ple_bwd(dpre, w_gate, h, g_pre, dres):
    T, D = h.shape
    tm = TOKEN_TILE

    def body(dpre_ref, w_ref, h_ref, g_ref, dres_ref, dh_ref, dg_ref):
        i = pl.program_id(0)

        @pl.when(i == 0)
        def _():
            dg_ref[...] = jnp.zeros_like(dg_ref)

        da = _mm_nt(dpre_ref[...], w_ref[...])
        x = h_ref[...]
        dx, dgain = _rms_bwd(x, _rstd(x), g_ref[...], da)
        dg_ref[...] += _colsum(dgain)
        dh_ref[...] = dres_ref[...] + dx

    tile = pl.BlockSpec((tm, D), lambda i: (i, 0))
    return pl.pallas_call(
        body, name="ple_bwd", grid=(T // tm,),
        in_specs=[tile, _full((D, D)), tile, _full((1, D)), tile],
        out_specs=[tile, _full((1, D))],
        out_shape=[jax.ShapeDtypeStruct((T, D), F32), jax.ShapeDtypeStruct((1, D), F32)],
        compiler_params=_params(1),
    )(dpre, w_gate, h, g_pre, dres)


def _ple_dw_proj(p, de, n_proj):
    T, ple = p.shape
    D = de.shape[1]
    db = D // n_proj
    tk = TOKEN_TILE
    nt = T // tk

    def body(p_ref, de_ref, o_ref, acc):
        t = pl.program_id(0)

        @pl.when(t == 0)
        def _():
            acc[...] = jnp.zeros_like(acc)

        acc[...] += _mm_tn(p_ref[...].astype(BF16), de_ref[...])

        @pl.when(t == nt - 1)
        def _():
            for k in range(n_proj):
                o_ref[k] = acc[:, k * db:(k + 1) * db].astype(BF16)

    return pl.pallas_call(
        body, name="ple_dw_proj", grid=(nt,),
        in_specs=[pl.BlockSpec((tk, ple), lambda t: (t, 0)), pl.BlockSpec((tk, D), lambda t: (t, 0))],
        out_specs=_full((n_proj, ple, db)), out_shape=jax.ShapeDtypeStruct((n_proj, ple, db), BF16),
        scratch_shapes=[pltpu.VMEM((ple, D), F32)], compiler_params=_params(1),
    )(p, de)


def _tok(width):
    return pl.BlockSpec((TOKEN_TILE, width), lambda b, t: (t, 0))


def _dw_gu(a, dgu, name, side=None):
    T, D = a.shape
    nj, _, _, FB = dgu.shape
    return _tn_matmul(
        a, dgu, _tok(D), pl.BlockSpec((None, None, TOKEN_TILE, FB), lambda b, t: (b % nj, b // nj, t, 0)),
        jax.ShapeDtypeStruct((2 * nj, D, FB), BF16), pl.BlockSpec((None, D, FB), lambda b, t: (b, 0, 0)),
        2 * nj, T // TOKEN_TILE, (D, FB), name, side=side)


def _dw_down(hh, df, name, side=None):
    nj, T, FB = hh.shape
    D = df.shape[1]
    return _tn_matmul(
        hh, df, pl.BlockSpec((None, TOKEN_TILE, FB), lambda b, t: (b, t, 0)), _tok(D),
        jax.ShapeDtypeStruct((nj, FB, D), BF16), pl.BlockSpec((None, FB, D), lambda b, t: (b, 0, 0)),
        nj, T // TOKEN_TILE, (FB, D), name, side=side)


def _dw_rows(xm, y, name, rows):
    T, k = xm.shape
    D = y.shape[1]
    out = _tn_matmul(
        xm, y, pl.BlockSpec((TOKEN_TILE, rows), lambda b, t: (t, b)), _tok(D),
        jax.ShapeDtypeStruct((k, D), BF16), pl.BlockSpec((rows, D), lambda b, t: (b, 0)),
        k // rows, T // TOKEN_TILE, (rows, D), name)
    return out.reshape(N_DEV, k // N_DEV, D)


def _cast_bf16(arrays):
    n = len(arrays)

    def body(*refs):
        for a in range(n):
            refs[n + a][...] = refs[a][...].astype(BF16)

    return pl.pallas_call(
        body, name="cast_shards",
        in_specs=[pl.BlockSpec(memory_space=pltpu.VMEM)] * n, out_specs=[pl.BlockSpec(memory_space=pltpu.VMEM)] * n,
        out_shape=[jax.ShapeDtypeStruct(a.shape, BF16) for a in arrays],
        compiler_params=pltpu.CompilerParams(vmem_limit_bytes=VMEM_LIMIT),
    )(*arrays)


def _all_gather_bf16(shards):
    n = len(shards)

    def body(*refs):
        ins, outs, scr = refs[:n], refs[n:2 * n], refs[2 * n:3 * n]
        send_sems, recv_sems, local_sems = refs[3 * n:]
        x, y, c = _mesh_place()
        me, sibling = (x, y, c), (x, y, 1 - c)
        chips = [(1 - x, y), (x, 1 - y), (1 - x, 1 - y)]
        for a in range(n):
            scr[a][...] = ins[a][...].astype(BF16)

        def copy(a, k, block, to, src=None):
            dst = outs[a].at[_slot(block)]
            return pltpu.make_async_remote_copy(
                src_ref=dst if src is None else src, dst_ref=dst,
                send_sem=send_sems.at[a, k], recv_sem=recv_sems.at[a, k], device_id=to, device_id_type=MESH)

        mine = [pltpu.make_async_copy(scr[a], outs[a].at[_slot(me)], local_sems.at[a]) for a in range(n)]
        first = [copy(a, 1 + j, me, (*chip, c), src=scr[a]) for j, chip in enumerate(chips) for a in range(n)]
        first += [copy(a, 0, me, sibling, src=scr[a]) for a in range(n)]
        for cp in first + mine:
            cp.start()
        passed = []
        for j, chip in enumerate(chips):
            for a in range(n):
                copy(a, 1 + j, (*chip, c), me).wait_recv()
                cp = copy(a, 4 + j, (*chip, c), sibling)
                cp.start()
                passed.append(cp)
        for a in range(n):
            copy(a, 0, sibling, me).wait_recv()
        for j, chip in enumerate(chips):
            for a in range(n):
                copy(a, 4 + j, (*chip, 1 - c), me).wait_recv()
        for cp in first + passed:
            cp.wait_send()
        for cp in mine:
            cp.wait()

    return pl.pallas_call(
        body, name="weights_all_gather",
        in_specs=[pl.BlockSpec(memory_space=pltpu.VMEM)] * n,
        out_specs=[pl.BlockSpec(memory_space=pl.ANY)] * n,
        out_shape=[jax.ShapeDtypeStruct((N_DEV,) + s.shape, BF16) for s in shards],
        scratch_shapes=[pltpu.VMEM(s.shape, BF16) for s in shards]
        + [pltpu.SemaphoreType.DMA((n, 7)), pltpu.SemaphoreType.DMA((n, 7)), pltpu.SemaphoreType.DMA((n,))],
        compiler_params=pltpu.CompilerParams(vmem_limit_bytes=VMEM_LIMIT),
    )(*shards)


def _pack_layout(D, n_rel_rows):
    n_bin = -(-D_IN // D)
    row_bin = len(GAINS)
    row_sink = row_bin + n_bin
    row_loss = row_sink + 1
    row_rb = -(-(row_loss + 1) // 8) * 8
    n_rows = row_rb + -(-n_rel_rows // 8) * 8
    bin_parts = [(r, min(D, D_IN - r * D)) for r in range(n_bin)]
    return row_bin, row_sink, row_loss, row_rb, n_rows, bin_parts


def _final_exchange(grad_blocks, partials, loss):
    D = partials["ffn1_pre_g"].shape[1]
    rb_shape = partials["rel_bias"].shape
    row_bin, row_sink, row_loss, row_rb, n_rows, bin_parts = _pack_layout(D, rb_shape[0])
    n_small = len(SMALL)

    def body(*refs):
        g_in = refs[0]
        part = dict(zip(SMALL, refs[1:1 + n_small]))
        loss_ref = refs[1 + n_small]
        landed, gath, pack, send_sems, recv_sems, local_sems = refs[2 + n_small:]

        pack[...] = jnp.zeros_like(pack)
        for i, name in enumerate(GAINS):
            pack[i:i + 1, :] = part[name][...]
        for r, width in bin_parts:
            pack[row_bin + r:row_bin + r + 1, 0:width] = part["b_in"][:, r * D:r * D + width]
        pack[row_sink:row_sink + 1, 0:LANES] = part["sinks"][...]
        pack[row_loss:row_loss + 1, 0:LANES] = loss_ref[...]
        pack[row_rb:row_rb + rb_shape[0], 0:rb_shape[1]] = part["rel_bias"][...]

        small_start, small_wait = _side_copies("gather", [pack], [gath], send_sems, recv_sems, local_sems, sem_row=0)
        big_start, big_wait = _side_copies("exchange", [g_in], [landed], send_sems, recv_sems, local_sems, sem_row=1)
        small_start()
        big_start()
        small_wait()
        big_wait()

    args = [grad_blocks] + [partials[k] for k in SMALL] + [loss]
    vmem = pl.BlockSpec(memory_space=pltpu.VMEM)
    any_spec = pl.BlockSpec(memory_space=pl.ANY)
    return pl.pallas_call(
        body, name="final_exchange",
        in_specs=[any_spec] + [vmem] * (n_small + 1),
        out_specs=[any_spec, any_spec],
        out_shape=[jax.ShapeDtypeStruct(grad_blocks.shape, grad_blocks.dtype),
                   jax.ShapeDtypeStruct((N_DEV, n_rows, D), F32)],
        scratch_shapes=[pltpu.VMEM((n_rows, D), F32), pltpu.SemaphoreType.DMA((2, 7)),
                        pltpu.SemaphoreType.DMA((2, 7)), pltpu.SemaphoreType.DMA((2,))],
    )(*args)


def _adamw(w, g, m, v):
    m = ADAM_B1 * m + (1.0 - ADAM_B1) * g
    v = ADAM_B2 * v + (1.0 - ADAM_B2) * (g * g)
    m_hat = m / (1.0 - ADAM_B1 ** ADAM_STEP)
    v_hat = v / (1.0 - ADAM_B2 ** ADAM_STEP)
    return -ADAM_LR * (m_hat / (jnp.sqrt(v_hat) + ADAM_EPS) + ADAM_WD * w), m, v


def _sum_adamw(partials, w, m, v, rows, name):
    R, C = w.shape

    def body(p_ref, w_ref, m_ref, v_ref, g_ref, d_ref, nm_ref, nv_ref):
        g = p_ref[0].astype(F32)
        for k in range(1, N_DEV):
            g = g + p_ref[k].astype(F32)
        g_ref[...] = g
        d_ref[...], nm_ref[...], nv_ref[...] = _adamw(w_ref[...], g, m_ref[...], v_ref[...])

    tile = pl.BlockSpec((rows, C), lambda i: (i, 0))
    return pl.pallas_call(
        body, name=name, grid=(R // rows,),
        in_specs=[pl.BlockSpec((N_DEV, rows, C), lambda i: (0, i, 0)), tile, tile, tile],
        out_specs=[tile] * 4, out_shape=[jax.ShapeDtypeStruct((R, C), F32)] * 4,
        compiler_params=_params(1),
    )(partials, w, m, v)


def _small_adamw(gathered, ws, ms, vs):
    D = ws["ffn1_pre_g"].shape[1]
    n_sink = ws["sinks"].shape[1]
    rb_shape = ws["rel_bias"].shape
    row_bin, row_sink, row_loss, row_rb, n_rows, bin_parts = _pack_layout(D, rb_shape[0])
    n_small = len(SMALL)

    def body(*refs):
        gath = refs[0]
        pos = 1
        w_ref = dict(zip(SMALL, refs[pos:pos + n_small]))
        m_ref = dict(zip(SMALL, refs[pos + n_small:pos + 2 * n_small]))
        v_ref = dict(zip(SMALL, refs[pos + 2 * n_small:pos + 3 * n_small]))
        pos += 3 * n_small
        outs = {name: refs[pos + 4 * i:pos + 4 * i + 4] for i, name in enumerate(SMALL)}
        loss_out = refs[pos + 4 * n_small]
        pack = refs[pos + 4 * n_small + 1]

        total = gath[0]
        for k in range(1, N_DEV):
            total = total + gath[k]
        pack[...] = total

        def update(name, g):
            g_out, d_out, m_out, v_out = outs[name]
            g_out[...] = g
            d_out[...], m_out[...], v_out[...] = _adamw(w_ref[name][...], g, m_ref[name][...], v_ref[name][...])

        for i, name in enumerate(GAINS):
            update(name, pack[i:i + 1, :])
        update("b_in", jnp.concatenate([pack[row_bin + r:row_bin + r + 1, 0:width] for r, width in bin_parts], axis=1))
        update("sinks", pack[row_sink:row_sink + 1, 0:n_sink])
        update("rel_bias", pack[row_rb:row_rb + rb_shape[0], 0:rb_shape[1]])
        loss_out[...] = pack[row_loss:row_loss + 1, 0:LANES]

    args = [gathered]
    for group in (ws, ms, vs):
        args += [group[k] for k in SMALL]
    out_shape = []
    for name in SMALL:
        out_shape += [jax.ShapeDtypeStruct(ws[name].shape, F32)] * 4
    out_shape.append(jax.ShapeDtypeStruct((1, LANES), F32))
    res = pl.pallas_call(
        body, name="small_adamw",
        in_specs=[pl.BlockSpec(memory_space=pltpu.VMEM)] * len(args),
        out_specs=[pl.BlockSpec(memory_space=pltpu.VMEM)] * len(out_shape),
        out_shape=out_shape,
        scratch_shapes=[pltpu.VMEM((n_rows, D), F32)],
    )(*args)
    per_name = {name: res[4 * i:4 * i + 4] for i, name in enumerate(SMALL)}
    return per_name, res[-1]


def _adamw_rows(name, rows_total):
    if name.endswith("w_down"):
        return rows_total // 2
    return min(rows_total, 256)


def kernel(x, p, rel_bias, ffn1_pre_g, ffn1_w_gu, ffn1_w_down, ffn1_post_g, attn_pre_g, w_in, b_in, sinks, w_out, b_out, attn_post_g, ffn2_pre_g, ffn2_w_gu, ffn2_w_down, ffn2_post_g, ple_pre_g, w_ple_gate, w_ple_proj, ple_post_g, loss_target, m_rel_bias, m_ffn1_pre_g, m_ffn1_w_gu, m_ffn1_w_down, m_ffn1_post_g, m_attn_pre_g, m_w_in, m_b_in, m_sinks, m_w_out, m_b_out, m_attn_post_g, m_ffn2_pre_g, m_ffn2_w_gu, m_ffn2_w_down, m_ffn2_post_g, m_ple_pre_g, m_w_ple_gate, m_w_ple_proj, m_ple_post_g, v_rel_bias, v_ffn1_pre_g, v_ffn1_w_gu, v_ffn1_w_down, v_ffn1_post_g, v_attn_pre_g, v_w_in, v_b_in, v_sinks, v_w_out, v_b_out, v_attn_post_g, v_ffn2_pre_g, v_ffn2_w_gu, v_ffn2_w_down, v_ffn2_post_g, v_ple_pre_g, v_w_ple_gate, v_w_ple_proj, v_ple_post_g):
    given = dict(locals())
    ws = {k: given[k] for k in WEIGHTS}
    ms = {k: given["m_" + k] for k in WEIGHTS}
    vs = {k: given["v_" + k] for k in WEIGHTS}

    def shard(t):
        return t.reshape(t.shape[1:])

    xs, ps, target = shard(x), shard(shard(p)), shard(loss_target)
    T, D = xs.shape
    small = {k: ws[k] for k in SMALL}
    shards = {k: shard(ws[k]) for k in BIG}

    w_gu1, w_down1 = _all_gather_bf16([shards["ffn1_w_gu"], shards["ffn1_w_down"]])
    w_down1 = w_down1.reshape(-1, D)
    later = ("w_in", "w_out", "ffn2_w_gu", "ffn2_w_down", "w_ple_gate", "w_ple_proj")
    cast = dict(zip(later, _cast_bf16([shards[k] for k in later])))

    buckets_a = _bucket_tiles(PATTERNS_A)
    buckets_b = _bucket_tiles(PATTERNS_B)
    bias_a = _bias_build(small["rel_bias"], buckets_a, 0, "bias_build_a")
    bias_b = _bias_build(small["rel_bias"], buckets_b, N_HEAD_GROUP, "bias_build_b")
    a_cfg = dict(patterns=PATTERNS_A, qcol=Q_A_COL, kcol=K_A_COL, vcol=V_A_COL, shared_kv=True)
    b_cfg = dict(patterns=PATTERNS_B, qcol=Q_B_COL, kcol=K_B_COL, vcol=V_B_COL, shared_kv=False)

    (h1, f1, a1, gu1), (w_in_g, w_out_g) = _ffn_fwd(
        xs, small["ffn1_pre_g"], small["ffn1_post_g"], w_gu1, w_down1, "ffn1_fwd",
        side=("gather", [cast["w_in"], cast["w_out"]]))
    w_in_full = jnp.transpose(w_in_g, (1, 0, 2)).reshape(D, D_IN)
    w_out_full = w_out_g.reshape(-1, D)
    z, a2 = _inproj_fwd(h1, small["attn_pre_g"], w_in_full, small["b_in"])
    (mix_a, lse_a), _ = _attn_fwd(z, bias_a, small["sinks"], name="attn_a_fwd", **a_cfg)
    (mix_b, lse_b), (w_gu2, w_down2, w_gate, w_proj) = _attn_fwd(
        z, bias_b, None, name="attn_b_fwd", **b_cfg,
        side=("gather", [cast["ffn2_w_gu"], cast["ffn2_w_down"], cast["w_ple_gate"], cast["w_ple_proj"]]))
    w_down2 = w_down2.reshape(-1, D)
    w_gate = w_gate.reshape(-1, D)
    att, h2, mix = _outproj_fwd(mix_a, mix_b, w_out_full, small["b_out"], small["attn_post_g"], h1)
    (h3, f2, a3, gu2), _ = _ffn_fwd(h2, small["ffn2_pre_g"], small["ffn2_post_g"], w_gu2, w_down2, "ffn2_fwd")
    a4, dpre, de, dh4, loss, dg_ple_post = _ple_fwd_loss(
        h3, small["ple_pre_g"], w_gate, ps, w_proj, small["ple_post_g"], target)

    dh3, dg_ple_pre = _ple_bwd(dpre, w_gate, h3, small["ple_pre_g"], dh4)
    d_gate = _dw_rows(a4, dpre, "ple_dw_gate", min(256, D))
    d_proj = _ple_dw_proj(ps, de, N_DEV)
    dh2, df2, hh2, dgu2, dg_f2_post, dg_f2_pre = _ffn_bwd(
        dh3, f2, small["ffn2_post_g"], h2, small["ffn2_pre_g"], gu2, w_gu2, w_down2, "ffn2_bwd")
    d_gu2 = _dw_gu(a3, dgu2, "ffn2_dw_gu")
    d_down2 = _dw_down(hh2, df2, "ffn2_dw_down").reshape(N_DEV, -1, D)
    dmix_a, dmix_b, datt, dg_attn_post, db_out = _outproj_bwd(dh2, att, small["attn_post_g"], w_out_full)
    d_out = _dw_rows(mix, datt, "attn_dw_out", 256)
    (dqa, dka, dva, ds_a, dsinks), _ = _attn_bwd(
        z, bias_a, small["sinks"], dmix_a, mix_a, lse_a, name="attn_a_bwd", **a_cfg)
    early = ("w_ple_gate", "w_ple_proj", "ffn2_w_gu", "ffn2_w_down", "w_out")
    (dqb, dkb, dvb, ds_b), landed_early = _attn_bwd(
        z, bias_b, None, dmix_b, mix_b, lse_b, name="attn_b_bwd", **b_cfg,
        side=("exchange", [d_gate, d_proj, d_gu2, d_down2, d_out]))
    landed = dict(zip(early, landed_early))
    dh1, dz, db_in, dg_attn_pre = _inproj_bwd(dqa, dka, dva, dqb, dkb, dvb, w_in_full, h1, small["attn_pre_g"], dh2)
    cols = D_IN // 3
    d_in = _tn_matmul(
        a2, dz, _tok(D), pl.BlockSpec((TOKEN_TILE, cols), lambda b, t: (t, b)),
        jax.ShapeDtypeStruct((D, D_IN), BF16), pl.BlockSpec((D, cols), lambda b, t: (0, b)),
        3, T // TOKEN_TILE, (D, cols), "attn_dw_in")
    d_in = jnp.transpose(d_in.reshape(D, N_DEV, D_IN // N_DEV), (1, 0, 2))
    grad_x, df1, hh1, dgu1, dg_f1_post, dg_f1_pre = _ffn_bwd(
        dh1, f1, small["ffn1_post_g"], xs, small["ffn1_pre_g"], gu1, w_gu1, w_down1, "ffn1_bwd")
    d_down1, (landed["w_in"],) = _dw_down(hh1, df1, "ffn1_dw_down", side=("exchange", [d_in]))
    d_down1 = d_down1.reshape(N_DEV, -1, D)
    d_gu1, (landed["ffn1_w_down"],) = _dw_gu(a1, dgu1, "ffn1_dw_gu", side=("exchange", [d_down1]))

    rb_a = _bias_grad(ds_a, buckets_a, "bias_grad_a")
    rb_b = _bias_grad(ds_b, buckets_b, "bias_grad_b").reshape(len(PATTERNS_B), N_HEAD_GROUP, NUM_BUCKETS)
    d_rel_bias = jnp.concatenate([rb_a.T, jnp.sum(rb_b, axis=0).T], axis=1)
    small_grads = {"ffn1_pre_g": dg_f1_pre, "ffn1_post_g": dg_f1_post, "attn_pre_g": dg_attn_pre,
                   "attn_post_g": dg_attn_post, "ffn2_pre_g": dg_f2_pre, "ffn2_post_g": dg_f2_post,
                   "ple_pre_g": dg_ple_pre, "ple_post_g": dg_ple_post, "b_out": db_out, "b_in": db_in,
                   "sinks": dsinks, "rel_bias": d_rel_bias}
    landed["ffn1_w_gu"], small_gathered = _final_exchange(d_gu1, small_grads, loss)

    result = {}
    for k in BIG:
        rows_total = ws[k].shape[1]
        outs = _sum_adamw(landed[k], shards[k], shard(ms[k]), shard(vs[k]), _adamw_rows(k, rows_total), k + "_adamw")
        result[k] = [o.reshape(ws[k].shape) for o in outs]
    small_res, loss_all = _small_adamw(
        small_gathered, small, {k: ms[k] for k in SMALL}, {k: vs[k] for k in SMALL})
    result.update(small_res)

    out = [loss_all[0, 0], grad_x.reshape(x.shape)]
    for i in range(4):
        out += [result[k][i] for k in WEIGHTS]
    return tuple(out)
```

```python
import functools
import math

import numpy as np
import jax
import jax.numpy as jnp
from jax import lax
from jax.experimental import pallas as pl
from jax.experimental.pallas import tpu as pltpu

F32 = jnp.float32
BF16 = jnp.bfloat16
MESH = pl.DeviceIdType.MESH

N_DEV = 8
EPS = 1e-6
NEG_INF = -1e30
HEAD_DIM = 64
LANES = 128
QBLK = 128
D_IN = 2304
A_Q, A_KV, B_W = 512, 128, 512
N_HEAD_GROUP = 8
NUM_BUCKETS = 32
MAX_DISTANCE = 2048
PATTERNS_A = ((1, 127),)
PATTERNS_B = ((1, 128), (4, 128), (16, 128))
Q_A_COL, K_A_COL, V_A_COL = 0, 4, 5
Q_B_COL, K_B_COL, V_B_COL = 6, 10, 14

ADAM_LR, ADAM_B1, ADAM_B2, ADAM_EPS, ADAM_WD, ADAM_STEP = 0.001, 0.9, 0.999, 1e-08, 0.01, 10

TOKEN_TILE = 512
BLOCK_UNROLL = 2
VMEM_LIMIT = 56 * 1024 * 1024
ARB = "arbitrary"

BIG = ("ffn1_w_gu", "ffn1_w_down", "w_in", "w_out", "ffn2_w_gu", "ffn2_w_down", "w_ple_gate", "w_ple_proj")
GAINS = ("ffn1_pre_g", "ffn1_post_g", "attn_pre_g", "attn_post_g", "ffn2_pre_g", "ffn2_post_g",
         "ple_pre_g", "ple_post_g", "b_out")
SMALL = GAINS + ("b_in", "sinks", "rel_bias")
WEIGHTS = ("rel_bias", "ffn1_pre_g", "ffn1_w_gu", "ffn1_w_down", "ffn1_post_g", "attn_pre_g", "w_in", "b_in",
           "sinks", "w_out", "b_out", "attn_post_g", "ffn2_pre_g", "ffn2_w_gu", "ffn2_w_down", "ffn2_post_g",
           "ple_pre_g", "w_ple_gate", "w_ple_proj", "ple_post_g")


def _params(n_axes):
    return pltpu.CompilerParams(dimension_semantics=(ARB,) * n_axes, vmem_limit_bytes=VMEM_LIMIT)


def _mm(a, b):
    return jnp.dot(a, b, preferred_element_type=F32)


def _mm_nt(a, b):
    return lax.dot_general(a, b, (((1,), (1,)), ((), ())), preferred_element_type=F32)


def _mm_tn(a, b):
    return lax.dot_general(a, b, (((0,), (0,)), ((), ())), preferred_element_type=F32)


def _rstd(x):
    return lax.rsqrt(jnp.mean(x * x, axis=-1, keepdims=True) + EPS)


def _rms_bwd(x, r, gain, dy):
    n = x * r
    gdy = dy * gain
    return r * (gdy - n * jnp.mean(gdy * n, axis=-1, keepdims=True)), dy * n


def _colsum(v):
    return jnp.sum(v, axis=0, keepdims=True)


def _full(shape):
    return pl.BlockSpec(shape, lambda *_: (0,) * len(shape))


def _mesh_place():
    return lax.axis_index("x"), lax.axis_index("y"), lax.axis_index("c")


def _slot(dev):
    return 4 * dev[0] + 2 * dev[1] + dev[2]


def _peers(x, y, c):
    out = []
    for flip in range(1, N_DEV):
        dx, dy, dc = (flip >> 2) & 1, (flip >> 1) & 1, flip & 1
        out.append((1 - x if dx else x, 1 - y if dy else y, 1 - c if dc else c))
    return out


def _side_copies(kind, ins, outs, send_sems, recv_sems, local_sems, sem_row=0):
    n = len(ins)
    x, y, c = _mesh_place()
    me = _slot((x, y, c))
    peers = _peers(x, y, c)

    def src(a, block):
        return ins[a] if kind == "gather" else ins[a].at[block]

    def send(a, k, peer):
        return pltpu.make_async_remote_copy(
            src_ref=src(a, _slot(peer)), dst_ref=outs[a].at[me],
            send_sem=send_sems.at[sem_row + a, k], recv_sem=recv_sems.at[sem_row + a, k],
            device_id=peer, device_id_type=MESH)

    def arrival(a, k, peer):
        return pltpu.make_async_remote_copy(
            src_ref=src(a, _slot(peer)), dst_ref=outs[a].at[_slot(peer)],
            send_sem=send_sems.at[sem_row + a, k], recv_sem=recv_sems.at[sem_row + a, k],
            device_id=peer, device_id_type=MESH)

    def own(a):
        return pltpu.make_async_copy(src(a, me), outs[a].at[me], local_sems.at[sem_row + a])

    def start():
        for k, peer in enumerate(peers):
            for a in range(n):
                send(a, k, peer).start()
        for a in range(n):
            own(a).start()

    def wait():
        for k, peer in enumerate(peers):
            for a in range(n):
                arrival(a, k, peer).wait_recv()
        for k, peer in enumerate(peers):
            for a in range(n):
                send(a, k, peer).wait_send()
        for a in range(n):
            own(a).wait()

    return start, wait


def _side_out_shapes(kind, arrays):
    if kind == "gather":
        return [jax.ShapeDtypeStruct((N_DEV,) + a.shape, a.dtype) for a in arrays]
    return [jax.ShapeDtypeStruct(a.shape, a.dtype) for a in arrays]


def _hosted_call(body, name, grid, in_specs, out_specs, out_shape, scratch_shapes, args, side=None):
    if side is None:
        outs = pl.pallas_call(
            body, name=name, grid=grid, in_specs=in_specs, out_specs=out_specs, out_shape=out_shape,
            scratch_shapes=scratch_shapes, compiler_params=_params(len(grid)))(*args)
        return outs, []
    kind, arrays = side
    n_in, n_out, n_scr, n_side = len(in_specs), len(out_specs), len(scratch_shapes), len(arrays)

    def hosted(*refs):
        pos = 0
        groups = []
        for size in (n_in, n_side, n_out, n_side, n_scr):
            groups.append(refs[pos:pos + size])
            pos += size
        ins, side_in, outs, side_out, scr = groups
        send_sems, recv_sems, local_sems = refs[pos:]
        ids = [pl.program_id(d) for d in range(len(grid))]
        is_first = functools.reduce(jnp.logical_and, [i == 0 for i in ids])
        is_last = functools.reduce(jnp.logical_and, [i == g - 1 for i, g in zip(ids, grid)])
        start, wait = _side_copies(kind, side_in, side_out, send_sems, recv_sems, local_sems)
        pl.when(is_first)(start)
        body(*ins, *outs, *scr)
        pl.when(is_last)(wait)

    any_spec = pl.BlockSpec(memory_space=pl.ANY)
    outs = pl.pallas_call(
        hosted, name=name, grid=grid,
        in_specs=list(in_specs) + [any_spec] * n_side,
        out_specs=list(out_specs) + [any_spec] * n_side,
        out_shape=list(out_shape) + _side_out_shapes(kind, arrays),
        scratch_shapes=list(scratch_shapes) + [pltpu.SemaphoreType.DMA((n_side, 7)), pltpu.SemaphoreType.DMA((n_side, 7)),
                                               pltpu.SemaphoreType.DMA((n_side,))],
        compiler_params=_params(len(grid)))(*args, *arrays)
    return outs[:n_out], outs[n_out:]


def _ffn_fwd(h, g_pre, g_post, w_gu, w_down, name, side=None):
    T, D = h.shape
    nj = w_gu.shape[0] // 2
    FB = w_gu.shape[2]
    tm = TOKEN_TILE

    def body(h_ref, gpre_ref, gpost_ref, wg_ref, wu_ref, wd_ref, hout_ref, f_ref, a_ref, gu_ref, a_scr, acc):
        j = pl.program_id(1)

        @pl.when(j == 0)
        def _():
            x = h_ref[...]
            a = (x * _rstd(x) * gpre_ref[...]).astype(BF16)
            a_scr[...] = a
            a_ref[...] = a
            acc[...] = jnp.zeros_like(acc)

        a = a_scr[...]
        g = _mm(a, wg_ref[...])
        u = _mm(a, wu_ref[...])
        gu_ref[0] = g.astype(BF16)
        gu_ref[1] = u.astype(BF16)
        hh = (g * jax.nn.sigmoid(g) * u).astype(BF16)
        acc[...] += _mm(hh, wd_ref[...])

        @pl.when(j == nj - 1)
        def _():
            f = acc[...]
            f_ref[...] = f
            hout_ref[...] = h_ref[...] + 0.5 * (f * _rstd(f) * gpost_ref[...])

    return _hosted_call(
        body, name, (T // tm, nj),
        in_specs=[
            pl.BlockSpec((tm, D), lambda i, j: (i, 0)),
            _full((1, D)), _full((1, D)),
            pl.BlockSpec((None, D, FB), lambda i, j: (j, 0, 0)),
            pl.BlockSpec((None, D, FB), lambda i, j: (j + nj, 0, 0)),
            pl.BlockSpec((FB, D), lambda i, j: (j, 0)),
        ],
        out_specs=[
            pl.BlockSpec((tm, D), lambda i, j: (i, 0)),
            pl.BlockSpec((tm, D), lambda i, j: (i, 0)),
            pl.BlockSpec((tm, D), lambda i, j: (i, 0)),
            pl.BlockSpec((None, 2, tm, FB), lambda i, j: (j, 0, i, 0)),
        ],
        out_shape=[
            jax.ShapeDtypeStruct((T, D), F32),
            jax.ShapeDtypeStruct((T, D), F32),
            jax.ShapeDtypeStruct((T, D), BF16),
            jax.ShapeDtypeStruct((nj, 2, T, FB), BF16),
        ],
        scratch_shapes=[pltpu.VMEM((tm, D), BF16), pltpu.VMEM((tm, D), F32)],
        args=(h, g_pre, g_post, w_gu, w_gu, w_down), side=side)


def _ffn_bwd(dh_out, f, g_post, h, g_pre, gu, w_gu, w_down, name):
    T, D = h.shape
    nj = w_gu.shape[0] // 2
    FB = w_gu.shape[2]
    tm = TOKEN_TILE

    def body(dho_ref, f_ref, gpost_ref, h_ref, gpre_ref, gu_ref, wg_ref, wu_ref, wd_ref,
             dhin_ref, df_ref, hh_ref, dgu_ref, dgpost_ref, dgpre_ref, df_scr, da):
        i, j = pl.program_id(0), pl.program_id(1)

        @pl.when(jnp.logical_and(i == 0, j == 0))
        def _():
            dgpost_ref[...] = jnp.zeros_like(dgpost_ref)
            dgpre_ref[...] = jnp.zeros_like(dgpre_ref)

        @pl.when(j == 0)
        def _():
            fv = f_ref[...]
            df, dgain = _rms_bwd(fv, _rstd(fv), gpost_ref[...], 0.5 * dho_ref[...])
            dgpost_ref[...] += _colsum(dgain)
            dfb = df.astype(BF16)
            df_scr[...] = dfb
            df_ref[...] = dfb
            da[...] = jnp.zeros_like(da)

        dhh = _mm_nt(df_scr[...], wd_ref[...])
        g = gu_ref[0].astype(F32)
        u = gu_ref[1].astype(F32)
        sg = jax.nn.sigmoid(g)
        silu = g * sg
        hh_ref[...] = (silu * u).astype(BF16)
        dg = (dhh * u * (sg * (1.0 + g * (1.0 - sg)))).astype(BF16)
        du = (dhh * silu).astype(BF16)
        dgu_ref[0] = dg
        dgu_ref[1] = du
        da[...] += _mm_nt(dg, wg_ref[...]) + _mm_nt(du, wu_ref[...])

        @pl.when(j == nj - 1)
        def _():
            x = h_ref[...]
            dx, dgain = _rms_bwd(x, _rstd(x), gpre_ref[...], da[...])
            dgpre_ref[...] += _colsum(dgain)
            dhin_ref[...] = dho_ref[...] + dx

    tile = pl.BlockSpec((tm, D), lambda i, j: (i, 0))
    return pl.pallas_call(
        body, name=name, grid=(T // tm, nj),
        in_specs=[
            tile, tile, _full((1, D)), tile, _full((1, D)),
            pl.BlockSpec((None, 2, tm, FB), lambda i, j: (j, 0, i, 0)),
            pl.BlockSpec((None, D, FB), lambda i, j: (j, 0, 0)),
            pl.BlockSpec((None, D, FB), lambda i, j: (j + nj, 0, 0)),
            pl.BlockSpec((FB, D), lambda i, j: (j, 0)),
        ],
        out_specs=[
            tile, tile,
            pl.BlockSpec((None, tm, FB), lambda i, j: (j, i, 0)),
            pl.BlockSpec((None, 2, tm, FB), lambda i, j: (j, 0, i, 0)),
            _full((1, D)), _full((1, D)),
        ],
        out_shape=[
            jax.ShapeDtypeStruct((T, D), F32),
            jax.ShapeDtypeStruct((T, D), BF16),
            jax.ShapeDtypeStruct((nj, T, FB), BF16),
            jax.ShapeDtypeStruct((nj, 2, T, FB), BF16),
            jax.ShapeDtypeStruct((1, D), F32),
            jax.ShapeDtypeStruct((1, D), F32),
        ],
        scratch_shapes=[pltpu.VMEM((tm, D), BF16), pltpu.VMEM((tm, D), F32)],
        compiler_params=_params(2),
    )(dh_out, f, g_post, h, g_pre, gu, w_gu, w_gu, w_down)


def _tn_matmul(x, y, x_spec, y_spec, out_shape, out_spec, n_blocks, n_steps, acc_shape, name, side=None):
    def body(x_ref, y_ref, o_ref, acc):
        t = pl.program_id(1)

        @pl.when(t == 0)
        def _():
            acc[...] = jnp.zeros_like(acc)

        acc[...] += _mm_tn(x_ref[...].astype(BF16), y_ref[...].astype(BF16))

        @pl.when(t == n_steps - 1)
        def _():
            o_ref[...] = acc[...].astype(o_ref.dtype)

    outs, side_outs = _hosted_call(
        body, name, (n_blocks, n_steps), in_specs=[x_spec, y_spec], out_specs=[out_spec], out_shape=[out_shape],
        scratch_shapes=[pltpu.VMEM(acc_shape, F32)], args=(x, y), side=side)
    return (outs[0], side_outs) if side is not None else outs[0]


def _inproj_fwd(h, g_pre, w_in, b_in):
    T, D = h.shape
    tm = TOKEN_TILE

    def body(h_ref, g_ref, w_ref, b_ref, z_ref, a_ref):
        x = h_ref[...]
        a = (x * _rstd(x) * g_ref[...]).astype(BF16)
        a_ref[...] = a
        z_ref[...] = _mm(a, w_ref[...]) + b_ref[...]

    return pl.pallas_call(
        body, name="inproj_fwd", grid=(T // tm,),
        in_specs=[pl.BlockSpec((tm, D), lambda i: (i, 0)), _full((1, D)), _full((D, D_IN)), _full((1, D_IN))],
        out_specs=[pl.BlockSpec((tm, D_IN), lambda i: (i, 0)), pl.BlockSpec((tm, D), lambda i: (i, 0))],
        out_shape=[jax.ShapeDtypeStruct((T, D_IN), F32), jax.ShapeDtypeStruct((T, D), BF16)],
        compiler_params=_params(1),
    )(h, g_pre, w_in, b_in)


def _inproj_bwd(dqa, dka, dva, dqb, dkb, dvb, w_in, h, g_pre, dres):
    T, D = h.shape
    tm = TOKEN_TILE

    def body(dqa_ref, dka_ref, dva_ref, dqb_ref, dkb_ref, dvb_ref, w_ref, h_ref, g_ref, dres_ref,
             dh_ref, dz_ref, dbin_ref, dg_ref):
        i = pl.program_id(0)

        @pl.when(i == 0)
        def _():
            dbin_ref[...] = jnp.zeros_like(dbin_ref)
            dg_ref[...] = jnp.zeros_like(dg_ref)

        dz = jnp.concatenate([dqa_ref[...], dka_ref[...], dva_ref[...], dqb_ref[...], dkb_ref[...], dvb_ref[...]],
                             axis=1)
        dbin_ref[...] += _colsum(dz)
        dzb = dz.astype(BF16)
        dz_ref[...] = dzb
        da = _mm_nt(dzb, w_ref[...])
        x = h_ref[...]
        dx, dgain = _rms_bwd(x, _rstd(x), g_ref[...], da)
        dg_ref[...] += _colsum(dgain)
        dh_ref[...] = dres_ref[...] + dx

    def tile(w):
        return pl.BlockSpec((tm, w), lambda i: (i, 0))

    return pl.pallas_call(
        body, name="inproj_bwd", grid=(T // tm,),
        in_specs=[tile(A_Q), tile(A_KV), tile(A_KV), tile(B_W), tile(B_W), tile(B_W),
                  _full((D, D_IN)), tile(D), _full((1, D)), tile(D)],
        out_specs=[tile(D), tile(D_IN), _full((1, D_IN)), _full((1, D))],
        out_shape=[jax.ShapeDtypeStruct((T, D), F32), jax.ShapeDtypeStruct((T, D_IN), BF16),
                   jax.ShapeDtypeStruct((1, D_IN), F32), jax.ShapeDtypeStruct((1, D), F32)],
        compiler_params=_params(1),
    )(dqa, dka, dva, dqb, dkb, dvb, w_in, h, g_pre, dres)


def _bucket_tiles(patterns):
    i = np.arange(QBLK)[:, None]
    j = np.arange(2 * QBLK)[None, :]
    dist = QBLK + i - j
    max_exact = NUM_BUCKETS // 2
    tiles = []
    for dilation, max_dist in patterns:
        n = np.maximum(dist * dilation, 0)
        nf = np.maximum(n, 1).astype(np.float32)
        large = max_exact + (np.log(nf / np.float32(max_exact)) / np.float32(math.log(MAX_DISTANCE / max_exact))
                             * np.float32(NUM_BUCKETS - max_exact)).astype(np.int32)
        bucket = np.where(n < max_exact, n, np.minimum(large, NUM_BUCKETS - 1))
        tiles.append(np.where((dist >= 0) & (dist <= max_dist), bucket, -1))
    return jnp.asarray(np.stack(tiles).astype(np.int32))


def _bias_build(rel_bias, buckets, head0, name):
    n = buckets.shape[0]

    def body(bk_ref, rb_ref, o_ref):
        bk = bk_ref[...]
        base = jnp.where(bk < 0, NEG_INF, 0.0).astype(F32)
        for hd in range(N_HEAD_GROUP):
            o_ref[hd] = lax.fori_loop(
                0, NUM_BUCKETS, lambda b, acc, hd=hd: jnp.where(bk == b, rb_ref[b, head0 + hd], acc), base)

    return pl.pallas_call(
        body, name=name, grid=(n,),
        in_specs=[pl.BlockSpec((None, QBLK, 2 * QBLK), lambda p: (p, 0, 0)), pl.BlockSpec(memory_space=pltpu.SMEM)],
        out_specs=pl.BlockSpec((None, N_HEAD_GROUP, QBLK, 2 * QBLK), lambda p: (p, 0, 0, 0)),
        out_shape=jax.ShapeDtypeStruct((n, N_HEAD_GROUP, QBLK, 2 * QBLK), F32),
        compiler_params=_params(1),
    )(buckets, rel_bias)


def _bias_grad(ds, buckets, name):
    n = buckets.shape[0]

    def body(ds_ref, bk_ref, o_ref):
        bk = bk_ref[...]
        row = lax.broadcasted_iota(jnp.int32, (NUM_BUCKETS, 2 * QBLK), 0)
        for hd in range(N_HEAD_GROUP):
            d = ds_ref[hd]
            per_key = jnp.zeros((NUM_BUCKETS, 2 * QBLK), F32)
            for b in range(NUM_BUCKETS):
                per_key = jnp.where(row == b, jnp.sum(jnp.where(bk == b, d, 0.0), axis=0, keepdims=True), per_key)
            o_ref[hd] = jnp.broadcast_to(jnp.sum(per_key, axis=1, keepdims=True), (NUM_BUCKETS, LANES))

    out = pl.pallas_call(
        body, name=name, grid=(n,),
        in_specs=[pl.BlockSpec((None, N_HEAD_GROUP, QBLK, 2 * QBLK), lambda p: (p, 0, 0, 0)),
                  pl.BlockSpec((None, QBLK, 2 * QBLK), lambda p: (p, 0, 0))],
        out_specs=pl.BlockSpec((None, N_HEAD_GROUP, NUM_BUCKETS, LANES), lambda p: (p, 0, 0, 0)),
        out_shape=jax.ShapeDtypeStruct((n, N_HEAD_GROUP, NUM_BUCKETS, LANES), F32),
        compiler_params=_params(1),
    )(ds, buckets)
    return out[:, :, :, 0].reshape(n * N_HEAD_GROUP, NUM_BUCKETS)


def _class_rows(start, dilation):
    if dilation == 1:
        return pl.ds(pl.multiple_of(start, QBLK), QBLK)
    return pl.ds(start, QBLK, stride=dilation)


def _block_starts(idx, n_blocks, dilation):
    cls = idx // n_blocks
    n = idx % n_blocks
    cur = cls + dilation * QBLK * n
    prev = cls + dilation * QBLK * jnp.maximum(n - 1, 0)
    return n, cur, prev


class _HeadPair:
    def __init__(self, g, shared_kv):
        self.lane = lax.broadcasted_iota(jnp.int32, (1, LANES), 1)
        self.lower = self.lane < HEAD_DIM
        self.shared_kv = shared_kv
        self.key_lanes = (self.lane >= HEAD_DIM).astype(jnp.int32) == (g // 2)

    def stack(self, t):
        return jnp.concatenate([jnp.where(self.lower, t, 0.0), jnp.where(self.lower, 0.0, t)], axis=0).astype(BF16)

    def unstack(self, t2):
        return jnp.where(self.lower, t2[:QBLK], t2[QBLK:])

    def keys(self, t):
        if self.shared_kv:
            return jnp.where(self.key_lanes, t, pltpu.roll(t, HEAD_DIM, 1))
        return t

    def key_grads(self, t):
        if self.shared_kv:
            return jnp.where(self.key_lanes, t + pltpu.roll(t, HEAD_DIM, 1), 0.0)
        return t


def _attn_specs(T, qcol, kcol, vcol, shared_kv):
    kv = (lambda c: (lambda g: (0, c))) if shared_kv else (lambda c: (lambda g: (0, c + g)))
    return [pl.BlockSpec((T, LANES), lambda g: (0, qcol + g)),
            pl.BlockSpec((T, LANES), kv(kcol)),
            pl.BlockSpec((T, LANES), kv(vcol))]


def _attn_fwd(z, bias, sinks, patterns, qcol, kcol, vcol, shared_kv, name, side=None):
    T = z.shape[0]
    n_pat = len(patterns)
    has_sink = sinks is not None

    def body(*refs):
        if has_sink:
            sink_ref, refs = refs[0], refs[1:]
        q_ref, k_ref, v_ref, b_ref, o_ref, l_ref = refs[:6]
        po_scr = refs[6:6 + n_pat]
        pl_scr = refs[6 + n_pat:]
        g = pl.program_id(0)
        heads = _HeadPair(g, shared_kv)
        in_prev = lax.broadcasted_iota(jnp.int32, (2 * QBLK, 2 * QBLK), 1) < QBLK

        for pi, (dilation, _) in enumerate(patterns):
            n_blocks = T // (QBLK * dilation)

            def step(idx, carry, pi=pi, dilation=dilation, n_blocks=n_blocks):
                n, cur, prev = _block_starts(idx, n_blocks, dilation)
                rows_c, rows_p = _class_rows(cur, dilation), _class_rows(prev, dilation)
                qm = heads.stack(q_ref[rows_c, :])
                k2 = heads.keys(jnp.concatenate([k_ref[rows_p, :], k_ref[rows_c, :]], axis=0)).astype(BF16)
                v2 = heads.keys(jnp.concatenate([v_ref[rows_p, :], v_ref[rows_c, :]], axis=0)).astype(BF16)
                s = _mm_nt(qm, k2) * (HEAD_DIM ** -0.5) + b_ref[pi]
                s = jnp.where(jnp.logical_and(in_prev, n == 0), NEG_INF, s)
                m = jnp.max(s, axis=1, keepdims=True)
                pr = jnp.exp(s - m)
                den = jnp.sum(pr, axis=1, keepdims=True)
                o2 = _mm(pr.astype(BF16), v2) / den
                lse = m + jnp.log(den)
                po_scr[pi][rows_c, :] = heads.unstack(o2)
                pl_scr[2 * pi][rows_c, :] = jnp.broadcast_to(lse[:QBLK], (QBLK, LANES))
                pl_scr[2 * pi + 1][rows_c, :] = jnp.broadcast_to(lse[QBLK:], (QBLK, LANES))
                return carry

            lax.fori_loop(0, dilation * n_blocks, step, 0, unroll=BLOCK_UNROLL)

        def merge(ci, carry):
            rows = pl.ds(pl.multiple_of(ci * QBLK, QBLK), QBLK)
            weights = []
            for hd in range(2):
                parts = [pl_scr[2 * pi + hd][rows, :] for pi in range(n_pat)]
                m = functools.reduce(jnp.maximum, parts)
                if has_sink:
                    sink = sink_ref[0, 2 * g + hd]
                    m = jnp.maximum(m, sink)
                den = functools.reduce(jnp.add, [jnp.exp(x - m) for x in parts])
                if has_sink:
                    den = den + jnp.exp(sink - m)
                lse = m + jnp.log(den)
                l_ref[hd, rows, :] = lse
                weights.append([jnp.exp(x - lse) for x in parts])
            o_ref[rows, :] = functools.reduce(
                jnp.add, [jnp.where(heads.lower, weights[0][pi], weights[1][pi]) * po_scr[pi][rows, :]
                          for pi in range(n_pat)])
            return carry

        lax.fori_loop(0, T // QBLK, merge, 0)

    in_specs = _attn_specs(T, qcol, kcol, vcol, shared_kv)
    in_specs.append(pl.BlockSpec((n_pat, None, 2 * QBLK, 2 * QBLK), lambda g: (0, g, 0, 0)))
    args = [z, z, z, bias.reshape(n_pat, N_HEAD_GROUP // 2, 2 * QBLK, 2 * QBLK)]
    if has_sink:
        in_specs.insert(0, pl.BlockSpec(memory_space=pltpu.SMEM))
        args.insert(0, sinks)
    return _hosted_call(
        body, name, (N_HEAD_GROUP // 2,),
        in_specs=in_specs,
        out_specs=[pl.BlockSpec((T, LANES), lambda g: (0, g)), pl.BlockSpec((2, T, LANES), lambda g: (g, 0, 0))],
        out_shape=[jax.ShapeDtypeStruct((T, N_HEAD_GROUP * HEAD_DIM), F32),
                   jax.ShapeDtypeStruct((N_HEAD_GROUP, T, LANES), F32)],
        scratch_shapes=[pltpu.VMEM((T, LANES), F32)] * (3 * n_pat), args=args, side=side)


def _attn_bwd(z, bias, sinks, d_out, out, lse, patterns, qcol, kcol, vcol, shared_kv, name, side=None):
    T = z.shape[0]
    n_pat = len(patterns)
    has_sink = sinks is not None
    kv_width = LANES if shared_kv else N_HEAD_GROUP * HEAD_DIM

    def body(*refs):
        if has_sink:
            sink_ref, refs = refs[0], refs[1:]
        q_ref, k_ref, v_ref, b_ref, do_ref, o_ref, l0_ref, l1_ref = refs[:8]
        dq_ref, dk_ref, dv_ref, ds_ref = refs[8:12]
        dsink_ref = refs[12] if has_sink else None
        dk_acc, dv_acc = refs[-2:]
        g = pl.program_id(0)
        heads = _HeadPair(g, shared_kv)
        in_prev = lax.broadcasted_iota(jnp.int32, (2 * QBLK, 2 * QBLK), 1) < QBLK

        dq_ref[...] = jnp.zeros_like(dq_ref)
        ds_ref[...] = jnp.zeros_like(ds_ref)
        dk_acc[...] = jnp.zeros_like(dk_acc)
        dv_acc[...] = jnp.zeros_like(dv_acc)

        dsink = jnp.zeros((1, LANES), F32)
        for pi, (dilation, _) in enumerate(patterns):
            n_blocks = T // (QBLK * dilation)

            def step(idx, dsink, pi=pi, dilation=dilation, n_blocks=n_blocks):
                n, cur, prev = _block_starts(idx, n_blocks, dilation)
                rows_c, rows_p = _class_rows(cur, dilation), _class_rows(prev, dilation)
                qm = heads.stack(q_ref[rows_c, :])
                k2 = heads.keys(jnp.concatenate([k_ref[rows_p, :], k_ref[rows_c, :]], axis=0)).astype(BF16)
                v2 = heads.keys(jnp.concatenate([v_ref[rows_p, :], v_ref[rows_c, :]], axis=0)).astype(BF16)
                d_o = do_ref[rows_c, :]
                dom = heads.stack(d_o)
                dd = d_o * o_ref[rows_c, :]
                delta = jnp.concatenate([jnp.sum(jnp.where(heads.lower, dd, 0.0), axis=1, keepdims=True),
                                         jnp.sum(jnp.where(heads.lower, 0.0, dd), axis=1, keepdims=True)], axis=0)
                lse = jnp.concatenate([l0_ref[rows_c, :], l1_ref[rows_c, :]], axis=0)
                s = _mm_nt(qm, k2) * (HEAD_DIM ** -0.5) + b_ref[pi]
                s = jnp.where(jnp.logical_and(in_prev, n == 0), NEG_INF, s)
                pr = jnp.exp(s - jnp.concatenate([lse, lse], axis=1))
                ds = pr * (_mm_nt(dom, v2) - delta)
                ds_ref[pi] += ds
                dsb = ds.astype(BF16)
                dq_ref[rows_c, :] += heads.unstack(_mm(dsb, k2)) * (HEAD_DIM ** -0.5)
                dk2 = heads.key_grads(_mm_tn(dsb, qm)) * (HEAD_DIM ** -0.5)
                dv2 = heads.key_grads(_mm_tn(pr.astype(BF16), dom))
                if has_sink:
                    for hd in range(2):
                        rows_h = slice(QBLK * hd, QBLK * (hd + 1))
                        p_sink = jnp.exp(sink_ref[0, 2 * g + hd] - lse[rows_h, 0:1])
                        dsink = dsink - jnp.where(heads.lane == 2 * g + hd, jnp.sum(p_sink * delta[rows_h]), 0.0)
                dk_acc[rows_p, :] += dk2[:QBLK]
                dk_acc[rows_c, :] += dk2[QBLK:]
                dv_acc[rows_p, :] += dv2[:QBLK]
                dv_acc[rows_c, :] += dv2[QBLK:]
                return dsink

            dsink = lax.fori_loop(0, dilation * n_blocks, step, dsink, unroll=BLOCK_UNROLL)

        if shared_kv:
            @pl.when(g == 0)
            def _():
                dk_ref[...] = dk_acc[...]
                dv_ref[...] = dv_acc[...]

            @pl.when(g != 0)
            def _():
                dk_ref[...] += dk_acc[...]
                dv_ref[...] += dv_acc[...]
        else:
            dk_ref[...] = dk_acc[...]
            dv_ref[...] = dv_acc[...]

        if has_sink:
            @pl.when(g == 0)
            def _():
                dsink_ref[...] = dsink

            @pl.when(g != 0)
            def _():
                dsink_ref[...] += dsink

    pair = pl.BlockSpec((T, LANES), lambda g: (0, g))
    stacked = pl.BlockSpec((n_pat, None, 2 * QBLK, 2 * QBLK), lambda g: (0, g, 0, 0))
    stacked_shape = (n_pat, N_HEAD_GROUP // 2, 2 * QBLK, 2 * QBLK)
    in_specs = _attn_specs(T, qcol, kcol, vcol, shared_kv)
    in_specs += [stacked, pair, pair,
                 pl.BlockSpec((None, T, LANES), lambda g: (2 * g, 0, 0)),
                 pl.BlockSpec((None, T, LANES), lambda g: (2 * g + 1, 0, 0))]
    args = [z, z, z, bias.reshape(stacked_shape), d_out, out, lse, lse]
    kv_out = _full((T, LANES)) if shared_kv else pair
    out_specs = [pair, kv_out, kv_out, stacked]
    out_shape = [jax.ShapeDtypeStruct((T, N_HEAD_GROUP * HEAD_DIM), F32),
                 jax.ShapeDtypeStruct((T, kv_width), F32), jax.ShapeDtypeStruct((T, kv_width), F32),
                 jax.ShapeDtypeStruct(stacked_shape, F32)]
    if has_sink:
        in_specs.insert(0, pl.BlockSpec(memory_space=pltpu.SMEM))
        args.insert(0, sinks)
        out_specs.append(_full((1, LANES)))
        out_shape.append(jax.ShapeDtypeStruct((1, LANES), F32))
    outs, side_outs = _hosted_call(
        body, name, (N_HEAD_GROUP // 2,), in_specs=in_specs, out_specs=out_specs, out_shape=out_shape,
        scratch_shapes=[pltpu.VMEM((T, LANES), F32), pltpu.VMEM((T, LANES), F32)], args=args, side=side)
    outs = list(outs)
    outs[3] = outs[3].reshape(n_pat, N_HEAD_GROUP, QBLK, 2 * QBLK)
    return outs, side_outs


def _outproj_fwd(mix_a, mix_b, w_out, b_out, g_post, h):
    T, D = h.shape
    tm = TOKEN_TILE
    d_mix = w_out.shape[0]

    def body(ma_ref, mb_ref, w_ref, b_ref, g_ref, h_ref, att_ref, hout_ref, mix_ref):
        mix = jnp.concatenate([ma_ref[...], mb_ref[...]], axis=1).astype(BF16)
        mix_ref[...] = mix
        att = _mm(mix, w_ref[...]) + b_ref[...]
        att_ref[...] = att
        hout_ref[...] = h_ref[...] + att * _rstd(att) * g_ref[...]

    def tile(w):
        return pl.BlockSpec((tm, w), lambda i: (i, 0))

    return pl.pallas_call(
        body, name="outproj_fwd", grid=(T // tm,),
        in_specs=[tile(A_Q), tile(B_W), _full((d_mix, D)), _full((1, D)), _full((1, D)), tile(D)],
        out_specs=[tile(D), tile(D), tile(d_mix)],
        out_shape=[jax.ShapeDtypeStruct((T, D), F32), jax.ShapeDtypeStruct((T, D), F32),
                   jax.ShapeDtypeStruct((T, d_mix), BF16)],
        compiler_params=_params(1),
    )(mix_a, mix_b, w_out, b_out, g_post, h)


def _outproj_bwd(dh, att, g_post, w_out):
    T, D = dh.shape
    tm = TOKEN_TILE
    d_mix = w_out.shape[0]

    def body(dh_ref, att_ref, g_ref, w_ref, dma_ref, dmb_ref, datt_ref, dg_ref, db_ref):
        i = pl.program_id(0)

        @pl.when(i == 0)
        def _():
            dg_ref[...] = jnp.zeros_like(dg_ref)
            db_ref[...] = jnp.zeros_like(db_ref)

        att = att_ref[...]
        datt, dgain = _rms_bwd(att, _rstd(att), g_ref[...], dh_ref[...])
        dg_ref[...] += _colsum(dgain)
        db_ref[...] += _colsum(datt)
        dattb = datt.astype(BF16)
        datt_ref[...] = dattb
        dmix = _mm_nt(dattb, w_ref[...])
        dma_ref[...] = dmix[:, :A_Q]
        dmb_ref[...] = dmix[:, A_Q:]

    def tile(w):
        return pl.BlockSpec((tm, w), lambda i: (i, 0))

    return pl.pallas_call(
        body, name="outproj_bwd", grid=(T // tm,),
        in_specs=[tile(D), tile(D), _full((1, D)), _full((d_mix, D))],
        out_specs=[tile(A_Q), tile(B_W), tile(D), _full((1, D)), _full((1, D))],
        out_shape=[jax.ShapeDtypeStruct((T, A_Q), F32), jax.ShapeDtypeStruct((T, B_W), F32),
                   jax.ShapeDtypeStruct((T, D), BF16), jax.ShapeDtypeStruct((1, D), F32),
                   jax.ShapeDtypeStruct((1, D), F32)],
        compiler_params=_params(1),
    )(dh, att, g_post, w_out)


def _ple_fwd_loss(h, g_pre, w_gate, p, w_proj, g_post, target):
    T, D = h.shape
    tm = TOKEN_TILE
    n_proj, ple, db = w_proj.shape

    def body(h_ref, gpre_ref, wg_ref, p_ref, wp_ref, gpost_ref, t_ref,
             a_ref, dpre_ref, de_ref, dh_ref, loss_ref, dgpost_ref):
        i = pl.program_id(0)

        @pl.when(i == 0)
        def _():
            loss_ref[...] = jnp.zeros_like(loss_ref)
            dgpost_ref[...] = jnp.zeros_like(dgpost_ref)

        x = h_ref[...]
        a = (x * _rstd(x) * gpre_ref[...]).astype(BF16)
        a_ref[...] = a
        gate = jax.nn.sigmoid(_mm(a, wg_ref[...]))
        pb = p_ref[...].astype(BF16)
        e = jnp.concatenate([_mm(pb, wp_ref[k]) for k in range(n_proj)], axis=1)
        ge = gate * e
        rg = _rstd(ge)
        diff = x + ge * rg * gpost_ref[...] - t_ref[...]
        loss_ref[...] += 0.5 * jnp.sum(jnp.mean(diff * diff, axis=1, keepdims=True))
        dy = diff * (1.0 / D)
        dh_ref[...] = dy
        dge, dgain = _rms_bwd(ge, rg, gpost_ref[...], dy)
        dgpost_ref[...] += _colsum(dgain)
        de_ref[...] = (dge * gate).astype(BF16)
        dpre_ref[...] = (dge * e * gate * (1.0 - gate)).astype(BF16)

    def tile(w):
        return pl.BlockSpec((tm, w), lambda i: (i, 0))

    return pl.pallas_call(
        body, name="ple_fwd_loss", grid=(T // tm,),
        in_specs=[tile(D), _full((1, D)), _full((D, D)), tile(ple), _full((n_proj, ple, db)), _full((1, D)), tile(D)],
        out_specs=[tile(D), tile(D), tile(D), tile(D), _full((1, LANES)), _full((1, D))],
        out_shape=[jax.ShapeDtypeStruct((T, D), BF16),
                   jax.ShapeDtypeStruct((T, D), BF16),
                   jax.ShapeDtypeStruct((T, D), BF16),
                   jax.ShapeDtypeStruct((T, D), F32),
                   jax.ShapeDtypeStruct((1, LANES), F32),
                   jax.ShapeDtypeStruct((1, D), F32)],
        compiler_params=_params(1),
    )(h, g_pre, w_gate, p, w_proj, g_post, target)


def _ple_bwd(dpre, w_gate, h, g_pre, dres):
    T, D = h.shape
    tm = TOKEN_TILE

    def body(dpre_ref, w_ref, h_ref, g_ref, dres_ref, dh_ref, dg_ref):
        i = pl.program_id(0)

        @pl.when(i == 0)
        def _():
            dg_ref[...] = jnp.zeros_like(dg_ref)

        da = _mm_nt(dpre_ref[...], w_ref[...])
        x = h_ref[...]
        dx, dgain = _rms_bwd(x, _rstd(x), g_ref[...], da)
        dg_ref[...] += _colsum(dgain)
        dh_ref[...] = dres_ref[...] + dx

    tile = pl.BlockSpec((tm, D), lambda i: (i, 0))
    return pl.pallas_call(
        body, name="ple_bwd", grid=(T // tm,),
        in_specs=[tile, _full((D, D)), tile, _full((1, D)), tile],
        out_specs=[tile, _full((1, D))],
        out_shape=[jax.ShapeDtypeStruct((T, D), F32), jax.ShapeDtypeStruct((1, D), F32)],
        compiler_params=_params(1),
    )(dpre, w_gate, h, g_pre, dres)


def _ple_dw_proj(p, de, n_proj):
    T, ple = p.shape
    D = de.shape[1]
    db = D // n_proj
    tk = TOKEN_TILE
    nt = T // tk

    def body(p_ref, de_ref, o_ref, acc):
        t = pl.program_id(0)

        @pl.when(t == 0)
        def _():
            acc[...] = jnp.zeros_like(acc)

        acc[...] += _mm_tn(p_ref[...].astype(BF16), de_ref[...])

        @pl.when(t == nt - 1)
        def _():
            for k in range(n_proj):
                o_ref[k] = acc[:, k * db:(k + 1) * db].astype(BF16)

    return pl.pallas_call(
        body, name="ple_dw_proj", grid=(nt,),
        in_specs=[pl.BlockSpec((tk, ple), lambda t: (t, 0)), pl.BlockSpec((tk, D), lambda t: (t, 0))],
        out_specs=_full((n_proj, ple, db)), out_shape=jax.ShapeDtypeStruct((n_proj, ple, db), BF16),
        scratch_shapes=[pltpu.VMEM((ple, D), F32)], compiler_params=_params(1),
    )(p, de)


def _tok(width):
    return pl.BlockSpec((TOKEN_TILE, width), lambda b, t: (t, 0))


def _dw_gu(a, dgu, name, side=None):
    T, D = a.shape
    nj, _, _, FB = dgu.shape
    return _tn_matmul(
        a, dgu, _tok(D), pl.BlockSpec((None, None, TOKEN_TILE, FB), lambda b, t: (b % nj, b // nj, t, 0)),
        jax.ShapeDtypeStruct((2 * nj, D, FB), BF16), pl.BlockSpec((None, D, FB), lambda b, t: (b, 0, 0)),
        2 * nj, T // TOKEN_TILE, (D, FB), name, side=side)


def _dw_down(hh, df, name, side=None):
    nj, T, FB = hh.shape
    D = df.shape[1]
    return _tn_matmul(
        hh, df, pl.BlockSpec((None, TOKEN_TILE, FB), lambda b, t: (b, t, 0)), _tok(D),
        jax.ShapeDtypeStruct((nj, FB, D), BF16), pl.BlockSpec((None, FB, D), lambda b, t: (b, 0, 0)),
        nj, T // TOKEN_TILE, (FB, D), name, side=side)


def _dw_rows(xm, y, name, rows):
    T, k = xm.shape
    D = y.shape[1]
    out = _tn_matmul(
        xm, y, pl.BlockSpec((TOKEN_TILE, rows), lambda b, t: (t, b)), _tok(D),
        jax.ShapeDtypeStruct((k, D), BF16), pl.BlockSpec((rows, D), lambda b, t: (b, 0)),
        k // rows, T // TOKEN_TILE, (rows, D), name)
    return out.reshape(N_DEV, k // N_DEV, D)


def _cast_bf16(arrays):
    n = len(arrays)

    def body(*refs):
        for a in range(n):
            refs[n + a][...] = refs[a][...].astype(BF16)

    return pl.pallas_call(
        body, name="cast_shards",
        in_specs=[pl.BlockSpec(memory_space=pltpu.VMEM)] * n, out_specs=[pl.BlockSpec(memory_space=pltpu.VMEM)] * n,
        out_shape=[jax.ShapeDtypeStruct(a.shape, BF16) for a in arrays],
        compiler_params=pltpu.CompilerParams(vmem_limit_bytes=VMEM_LIMIT),
    )(*arrays)


def _all_gather_bf16(shards):
    n = len(shards)

    def body(*refs):
        ins, outs, scr = refs[:n], refs[n:2 * n], refs[2 * n:3 * n]
        send_sems, recv_sems, local_sems = refs[3 * n:]
        x, y, c = _mesh_place()
        me, sibling = (x, y, c), (x, y, 1 - c)
        chips = [(1 - x, y), (x, 1 - y), (1 - x, 1 - y)]
        for a in range(n):
            scr[a][...] = ins[a][...].astype(BF16)

        def copy(a, k, block, to, src=None):
            dst = outs[a].at[_slot(block)]
            return pltpu.make_async_remote_copy(
                src_ref=dst if src is None else src, dst_ref=dst,
                send_sem=send_sems.at[a, k], recv_sem=recv_sems.at[a, k], device_id=to, device_id_type=MESH)

        mine = [pltpu.make_async_copy(scr[a], outs[a].at[_slot(me)], local_sems.at[a]) for a in range(n)]
        first = [copy(a, 1 + j, me, (*chip, c), src=scr[a]) for j, chip in enumerate(chips) for a in range(n)]
        first += [copy(a, 0, me, sibling, src=scr[a]) for a in range(n)]
        for cp in first + mine:
            cp.start()
        passed = []
        for j, chip in enumerate(chips):
            for a in range(n):
                copy(a, 1 + j, (*chip, c), me).wait_recv()
                cp = copy(a, 4 + j, (*chip, c), sibling)
                cp.start()
                passed.append(cp)
        for a in range(n):
            copy(a, 0, sibling, me).wait_recv()
        for j, chip in enumerate(chips):
            for a in range(n):
                copy(a, 4 + j, (*chip, 1 - c), me).wait_recv()
        for cp in first + passed:
            cp.wait_send()
        for cp in mine:
            cp.wait()

    return pl.pallas_call(
        body, name="weights_all_gather",
        in_specs=[pl.BlockSpec(memory_space=pltpu.VMEM)] * n,
        out_specs=[pl.BlockSpec(memory_space=pl.ANY)] * n,
        out_shape=[jax.ShapeDtypeStruct((N_DEV,) + s.shape, BF16) for s in shards],
        scratch_shapes=[pltpu.VMEM(s.shape, BF16) for s in shards]
        + [pltpu.SemaphoreType.DMA((n, 7)), pltpu.SemaphoreType.DMA((n, 7)), pltpu.SemaphoreType.DMA((n,))],
        compiler_params=pltpu.CompilerParams(vmem_limit_bytes=VMEM_LIMIT),
    )(*shards)


def _pack_layout(D, n_rel_rows):
    n_bin = -(-D_IN // D)
    row_bin = len(GAINS)
    row_sink = row_bin + n_bin
    row_loss = row_sink + 1
    row_rb = -(-(row_loss + 1) // 8) * 8
    n_rows = row_rb + -(-n_rel_rows // 8) * 8
    bin_parts = [(r, min(D, D_IN - r * D)) for r in range(n_bin)]
    return row_bin, row_sink, row_loss, row_rb, n_rows, bin_parts


def _final_exchange(grad_blocks, partials, loss):
    D = partials["ffn1_pre_g"].shape[1]
    rb_shape = partials["rel_bias"].shape
    row_bin, row_sink, row_loss, row_rb, n_rows, bin_parts = _pack_layout(D, rb_shape[0])
    n_small = len(SMALL)

    def body(*refs):
        g_in = refs[0]
        part = dict(zip(SMALL, refs[1:1 + n_small]))
        loss_ref = refs[1 + n_small]
        landed, gath, pack, send_sems, recv_sems, local_sems = refs[2 + n_small:]

        pack[...] = jnp.zeros_like(pack)
        for i, name in enumerate(GAINS):
            pack[i:i + 1, :] = part[name][...]
        for r, width in bin_parts:
            pack[row_bin + r:row_bin + r + 1, 0:width] = part["b_in"][:, r * D:r * D + width]
        pack[row_sink:row_sink + 1, 0:LANES] = part["sinks"][...]
        pack[row_loss:row_loss + 1, 0:LANES] = loss_ref[...]
        pack[row_rb:row_rb + rb_shape[0], 0:rb_shape[1]] = part["rel_bias"][...]

        small_start, small_wait = _side_copies("gather", [pack], [gath], send_sems, recv_sems, local_sems, sem_row=0)
        big_start, big_wait = _side_copies("exchange", [g_in], [landed], send_sems, recv_sems, local_sems, sem_row=1)
        small_start()
        big_start()
        small_wait()
        big_wait()

    args = [grad_blocks] + [partials[k] for k in SMALL] + [loss]
    vmem = pl.BlockSpec(memory_space=pltpu.VMEM)
    any_spec = pl.BlockSpec(memory_space=pl.ANY)
    return pl.pallas_call(
        body, name="final_exchange",
        in_specs=[any_spec] + [vmem] * (n_small + 1),
        out_specs=[any_spec, any_spec],
        out_shape=[jax.ShapeDtypeStruct(grad_blocks.shape, grad_blocks.dtype),
                   jax.ShapeDtypeStruct((N_DEV, n_rows, D), F32)],
        scratch_shapes=[pltpu.VMEM((n_rows, D), F32), pltpu.SemaphoreType.DMA((2, 7)),
                        pltpu.SemaphoreType.DMA((2, 7)), pltpu.SemaphoreType.DMA((2,))],
    )(*args)


def _adamw(w, g, m, v):
    m = ADAM_B1 * m + (1.0 - ADAM_B1) * g
    v = ADAM_B2 * v + (1.0 - ADAM_B2) * (g * g)
    m_hat = m / (1.0 - ADAM_B1 ** ADAM_STEP)
    v_hat = v / (1.0 - ADAM_B2 ** ADAM_STEP)
    return -ADAM_LR * (m_hat / (jnp.sqrt(v_hat) + ADAM_EPS) + ADAM_WD * w), m, v


def _sum_adamw(partials, w, m, v, rows, name):
    R, C = w.shape

    def body(p_ref, w_ref, m_ref, v_ref, g_ref, d_ref, nm_ref, nv_ref):
        g = p_ref[0].astype(F32)
        for k in range(1, N_DEV):
            g = g + p_ref[k].astype(F32)
        g_ref[...] = g
        d_ref[...], nm_ref[...], nv_ref[...] = _adamw(w_ref[...], g, m_ref[...], v_ref[...])

    tile = pl.BlockSpec((rows, C), lambda i: (i, 0))
    return pl.pallas_call(
        body, name=name, grid=(R // rows,),
        in_specs=[pl.BlockSpec((N_DEV, rows, C), lambda i: (0, i, 0)), tile, tile, tile],
        out_specs=[tile] * 4, out_shape=[jax.ShapeDtypeStruct((R, C), F32)] * 4,
        compiler_params=_params(1),
    )(partials, w, m, v)


def _small_adamw(gathered, ws, ms, vs):
    D = ws["ffn1_pre_g"].shape[1]
    n_sink = ws["sinks"].shape[1]
    rb_shape = ws["rel_bias"].shape
    row_bin, row_sink, row_loss, row_rb, n_rows, bin_parts = _pack_layout(D, rb_shape[0])
    n_small = len(SMALL)

    def body(*refs):
        gath = refs[0]
        pos = 1
        w_ref = dict(zip(SMALL, refs[pos:pos + n_small]))
        m_ref = dict(zip(SMALL, refs[pos + n_small:pos + 2 * n_small]))
        v_ref = dict(zip(SMALL, refs[pos + 2 * n_small:pos + 3 * n_small]))
        pos += 3 * n_small
        outs = {name: refs[pos + 4 * i:pos + 4 * i + 4] for i, name in enumerate(SMALL)}
        loss_out = refs[pos + 4 * n_small]
        pack = refs[pos + 4 * n_small + 1]

        total = gath[0]
        for k in range(1, N_DEV):
            total = total + gath[k]
        pack[...] = total

        def update(name, g):
            g_out, d_out, m_out, v_out = outs[name]
            g_out[...] = g
            d_out[...], m_out[...], v_out[...] = _adamw(w_ref[name][...], g, m_ref[name][...], v_ref[name][...])

        for i, name in enumerate(GAINS):
            update(name, pack[i:i + 1, :])
        update("b_in", jnp.concatenate([pack[row_bin + r:row_bin + r + 1, 0:width] for r, width in bin_parts], axis=1))
        update("sinks", pack[row_sink:row_sink + 1, 0:n_sink])
        update("rel_bias", pack[row_rb:row_rb + rb_shape[0], 0:rb_shape[1]])
        loss_out[...] = pack[row_loss:row_loss + 1, 0:LANES]

    args = [gathered]
    for group in (ws, ms, vs):
        args += [group[k] for k in SMALL]
    out_shape = []
    for name in SMALL:
        out_shape += [jax.ShapeDtypeStruct(ws[name].shape, F32)] * 4
    out_shape.append(jax.ShapeDtypeStruct((1, LANES), F32))
    res = pl.pallas_call(
        body, name="small_adamw",
        in_specs=[pl.BlockSpec(memory_space=pltpu.VMEM)] * len(args),
        out_specs=[pl.BlockSpec(memory_space=pltpu.VMEM)] * len(out_shape),
        out_shape=out_shape,
        scratch_shapes=[pltpu.VMEM((n_rows, D), F32)],
    )(*args)
    per_name = {name: res[4 * i:4 * i + 4] for i, name in enumerate(SMALL)}
    return per_name, res[-1]


def _adamw_rows(name, rows_total):
    if name.endswith("w_down"):
        return rows_total // 2
    return min(rows_total, 256)


def kernel(x, p, rel_bias, ffn1_pre_g, ffn1_w_gu, ffn1_w_down, ffn1_post_g, attn_pre_g, w_in, b_in, sinks, w_out, b_out, attn_post_g, ffn2_pre_g, ffn2_w_gu, ffn2_w_down, ffn2_post_g, ple_pre_g, w_ple_gate, w_ple_proj, ple_post_g, loss_target, m_rel_bias, m_ffn1_pre_g, m_ffn1_w_gu, m_ffn1_w_down, m_ffn1_post_g, m_attn_pre_g, m_w_in, m_b_in, m_sinks, m_w_out, m_b_out, m_attn_post_g, m_ffn2_pre_g, m_ffn2_w_gu, m_ffn2_w_down, m_ffn2_post_g, m_ple_pre_g, m_w_ple_gate, m_w_ple_proj, m_ple_post_g, v_rel_bias, v_ffn1_pre_g, v_ffn1_w_gu, v_ffn1_w_down, v_ffn1_post_g, v_attn_pre_g, v_w_in, v_b_in, v_sinks, v_w_out, v_b_out, v_attn_post_g, v_ffn2_pre_g, v_ffn2_w_gu, v_ffn2_w_down, v_ffn2_post_g, v_ple_pre_g, v_w_ple_gate, v_w_ple_proj, v_ple_post_g):
    given = dict(locals())
    ws = {k: given[k] for k in WEIGHTS}
    ms = {k: given["m_" + k] for k in WEIGHTS}
    vs = {k: given["v_" + k] for k in WEIGHTS}

    def shard(t):
        return t.reshape(t.shape[1:])

    xs, ps, target = shard(x), shard(shard(p)), shard(loss_target)
    T, D = xs.shape
    small = {k: ws[k] for k in SMALL}
    shards = {k: shard(ws[k]) for k in BIG}

    w_gu1, w_down1 = _all_gather_bf16([shards["ffn1_w_gu"], shards["ffn1_w_down"]])
    w_down1 = w_down1.reshape(-1, D)
    later = ("w_in", "w_out", "ffn2_w_gu", "ffn2_w_down", "w_ple_gate", "w_ple_proj")
    cast = dict(zip(later, _cast_bf16([shards[k] for k in later])))

    buckets_a = _bucket_tiles(PATTERNS_A)
    buckets_b = _bucket_tiles(PATTERNS_B)
    bias_a = _bias_build(small["rel_bias"], buckets_a, 0, "bias_build_a")
    bias_b = _bias_build(small["rel_bias"], buckets_b, N_HEAD_GROUP, "bias_build_b")
    a_cfg = dict(patterns=PATTERNS_A, qcol=Q_A_COL, kcol=K_A_COL, vcol=V_A_COL, shared_kv=True)
    b_cfg = dict(patterns=PATTERNS_B, qcol=Q_B_COL, kcol=K_B_COL, vcol=V_B_COL, shared_kv=False)

    (h1, f1, a1, gu1), (w_in_g, w_out_g) = _ffn_fwd(
        xs, small["ffn1_pre_g"], small["ffn1_post_g"], w_gu1, w_down1, "ffn1_fwd",
        side=("gather", [cast["w_in"], cast["w_out"]]))
    w_in_full = jnp.transpose(w_in_g, (1, 0, 2)).reshape(D, D_IN)
    w_out_full = w_out_g.reshape(-1, D)
    z, a2 = _inproj_fwd(h1, small["attn_pre_g"], w_in_full, small["b_in"])
    (mix_a, lse_a), _ = _attn_fwd(z, bias_a, small["sinks"], name="attn_a_fwd", **a_cfg)
    (mix_b, lse_b), (w_gu2, w_down2, w_gate, w_proj) = _attn_fwd(
        z, bias_b, None, name="attn_b_fwd", **b_cfg,
        side=("gather", [cast["ffn2_w_gu"], cast["ffn2_w_down"], cast["w_ple_gate"], cast["w_ple_proj"]]))
    w_down2 = w_down2.reshape(-1, D)
    w_gate = w_gate.reshape(-1, D)
    att, h2, mix = _outproj_fwd(mix_a, mix_b, w_out_full, small["b_out"], small["attn_post_g"], h1)
    (h3, f2, a3, gu2), _ = _ffn_fwd(h2, small["ffn2_pre_g"], small["ffn2_post_g"], w_gu2, w_down2, "ffn2_fwd")
    a4, dpre, de, dh4, loss, dg_ple_post = _ple_fwd_loss(
        h3, small["ple_pre_g"], w_gate, ps, w_proj, small["ple_post_g"], target)

    dh3, dg_ple_pre = _ple_bwd(dpre, w_gate, h3, small["ple_pre_g"], dh4)
    d_gate = _dw_rows(a4, dpre, "ple_dw_gate", min(256, D))
    d_proj = _ple_dw_proj(ps, de, N_DEV)
    dh2, df2, hh2, dgu2, dg_f2_post, dg_f2_pre = _ffn_bwd(
        dh3, f2, small["ffn2_post_g"], h2, small["ffn2_pre_g"], gu2, w_gu2, w_down2, "ffn2_bwd")
    d_gu2 = _dw_gu(a3, dgu2, "ffn2_dw_gu")
    d_down2 = _dw_down(hh2, df2, "ffn2_dw_down").reshape(N_DEV, -1, D)
    dmix_a, dmix_b, datt, dg_attn_post, db_out = _outproj_bwd(dh2, att, small["attn_post_g"], w_out_full)
    d_out = _dw_rows(mix, datt, "attn_dw_out", 256)
    (dqa, dka, dva, ds_a, dsinks), _ = _attn_bwd(
        z, bias_a, small["sinks"], dmix_a, mix_a, lse_a, name="attn_a_bwd", **a_cfg)
    early = ("w_ple_gate", "w_ple_proj", "ffn2_w_gu", "ffn2_w_down", "w_out")
    (dqb, dkb, dvb, ds_b), landed_early = _attn_bwd(
        z, bias_b, None, dmix_b, mix_b, lse_b, name="attn_b_bwd", **b_cfg,
        side=("exchange", [d_gate, d_proj, d_gu2, d_down2, d_out]))
    landed = dict(zip(early, landed_early))
    dh1, dz, db_in, dg_attn_pre = _inproj_bwd(dqa, dka, dva, dqb, dkb, dvb, w_in_full, h1, small["attn_pre_g"], dh2)
    cols = D_IN // 3
    d_in = _tn_matmul(
        a2, dz, _tok(D), pl.BlockSpec((TOKEN_TILE, cols), lambda b, t: (t, b)),
        jax.ShapeDtypeStruct((D, D_IN), BF16), pl.BlockSpec((D, cols), lambda b, t: (0, b)),
        3, T // TOKEN_TILE, (D, cols), "attn_dw_in")
    d_in = jnp.transpose(d_in.reshape(D, N_DEV, D_IN // N_DEV), (1, 0, 2))
    grad_x, df1, hh1, dgu1, dg_f1_post, dg_f1_pre = _ffn_bwd(
        dh1, f1, small["ffn1_post_g"], xs, small["ffn1_pre_g"], gu1, w_gu1, w_down1, "ffn1_bwd")
    d_down1, (landed["w_in"],) = _dw_down(hh1, df1, "ffn1_dw_down", side=("exchange", [d_in]))
    d_down1 = d_down1.reshape(N_DEV, -1, D)
    d_gu1, (landed["ffn1_w_down"],) = _dw_gu(a1, dgu1, "ffn1_dw_gu", side=("exchange", [d_down1]))

    rb_a = _bias_grad(ds_a, buckets_a, "bias_grad_a")
    rb_b = _bias_grad(ds_b, buckets_b, "bias_grad_b").reshape(len(PATTERNS_B), N_HEAD_GROUP, NUM_BUCKETS)
    d_rel_bias = jnp.concatenate([rb_a.T, jnp.sum(rb_b, axis=0).T], axis=1)
    small_grads = {"ffn1_pre_g": dg_f1_pre, "ffn1_post_g": dg_f1_post, "attn_pre_g": dg_attn_pre,
                   "attn_post_g": dg_attn_post, "ffn2_pre_g": dg_f2_pre, "ffn2_post_g": dg_f2_post,
                   "ple_pre_g": dg_ple_pre, "ple_post_g": dg_ple_post, "b_out": db_out, "b_in": db_in,
                   "sinks": dsinks, "rel_bias": d_rel_bias}
    landed["ffn1_w_gu"], small_gathered = _final_exchange(d_gu1, small_grads, loss)

    result = {}
    for k in BIG:
        rows_total = ws[k].shape[1]
        outs = _sum_adamw(landed[k], shards[k], shard(ms[k]), shard(vs[k]), _adamw_rows(k, rows_total), k + "_adamw")
        result[k] = [o.reshape(ws[k].shape) for o in outs]
    small_res, loss_all = _small_adamw(
        small_gathered, small, {k: ms[k] for k in SMALL}, {k: vs[k] for k in SMALL})
    result.update(small_res)

    out = [loss_all[0, 0], grad_x.reshape(x.shape)]
    for i in range(4):
        out += [result[k][i] for k in WEIGHTS]
    return tuple(out)
```

```python
import functools
import math

import numpy as np
import jax
import jax.numpy as jnp
from jax import lax
from jax.experimental import pallas as pl
from jax.experimental.pallas import tpu as pltpu

F32 = jnp.float32
BF16 = jnp.bfloat16
MESH = pl.DeviceIdType.MESH

N_DEV = 8
EPS = 1e-6
NEG_INF = -1e30
HEAD_DIM = 64
LANES = 128
QBLK = 128
D_IN = 2304
A_Q, A_KV, B_W = 512, 128, 512
N_HEAD_GROUP = 8
NUM_BUCKETS = 32
MAX_DISTANCE = 2048
PATTERNS_A = ((1, 127),)
PATTERNS_B = ((1, 128), (4, 128), (16, 128))
Q_A_COL, K_A_COL, V_A_COL = 0, 4, 5
Q_B_COL, K_B_COL, V_B_COL = 6, 10, 14

ADAM_LR, ADAM_B1, ADAM_B2, ADAM_EPS, ADAM_WD, ADAM_STEP = 0.001, 0.9, 0.999, 1e-08, 0.01, 10

TOKEN_TILE = 512
FWD_BLOCKS = 4
BWD_BLOCKS = 2
VMEM_LIMIT = 56 * 1024 * 1024
ARB = "arbitrary"

BIG = ("ffn1_w_gu", "ffn1_w_down", "w_in", "w_out", "ffn2_w_gu", "ffn2_w_down", "w_ple_gate", "w_ple_proj")
GAINS = ("ffn1_pre_g", "ffn1_post_g", "attn_pre_g", "attn_post_g", "ffn2_pre_g", "ffn2_post_g",
         "ple_pre_g", "ple_post_g", "b_out")
SMALL = GAINS + ("b_in", "sinks", "rel_bias")
WEIGHTS = ("rel_bias", "ffn1_pre_g", "ffn1_w_gu", "ffn1_w_down", "ffn1_post_g", "attn_pre_g", "w_in", "b_in",
           "sinks", "w_out", "b_out", "attn_post_g", "ffn2_pre_g", "ffn2_w_gu", "ffn2_w_down", "ffn2_post_g",
           "ple_pre_g", "w_ple_gate", "w_ple_proj", "ple_post_g")


def _params(n_axes):
    return pltpu.CompilerParams(dimension_semantics=(ARB,) * n_axes, vmem_limit_bytes=VMEM_LIMIT)


def _mm(a, b):
    return jnp.dot(a, b, preferred_element_type=F32)


def _mm_nt(a, b):
    return lax.dot_general(a, b, (((1,), (1,)), ((), ())), preferred_element_type=F32)


def _mm_tn(a, b):
    return lax.dot_general(a, b, (((0,), (0,)), ((), ())), preferred_element_type=F32)


def _rstd(x):
    return lax.rsqrt(jnp.mean(x * x, axis=-1, keepdims=True) + EPS)


def _rms_bwd(x, r, gain, dy):
    n = x * r
    gdy = dy * gain
    return r * (gdy - n * jnp.mean(gdy * n, axis=-1, keepdims=True)), dy * n


def _colsum(v):
    return jnp.sum(v, axis=0, keepdims=True)


def _full(shape):
    return pl.BlockSpec(shape, lambda *_: (0,) * len(shape))


def _mesh_place():
    return lax.axis_index("x"), lax.axis_index("y"), lax.axis_index("c")


def _slot(dev):
    return 4 * dev[0] + 2 * dev[1] + dev[2]


def _peers(x, y, c):
    out = []
    for flip in range(1, N_DEV):
        dx, dy, dc = (flip >> 2) & 1, (flip >> 1) & 1, flip & 1
        out.append((1 - x if dx else x, 1 - y if dy else y, 1 - c if dc else c))
    return out


def _side_copies(kind, ins, outs, send_sems, recv_sems, local_sems, sem_row=0):
    n = len(ins)
    x, y, c = _mesh_place()
    me = _slot((x, y, c))
    peers = _peers(x, y, c)

    def src(a, block):
        return ins[a] if kind == "gather" else ins[a].at[block]

    def send(a, k, peer):
        return pltpu.make_async_remote_copy(
            src_ref=src(a, _slot(peer)), dst_ref=outs[a].at[me],
            send_sem=send_sems.at[sem_row + a, k], recv_sem=recv_sems.at[sem_row + a, k],
            device_id=peer, device_id_type=MESH)

    def arrival(a, k, peer):
        return pltpu.make_async_remote_copy(
            src_ref=src(a, _slot(peer)), dst_ref=outs[a].at[_slot(peer)],
            send_sem=send_sems.at[sem_row + a, k], recv_sem=recv_sems.at[sem_row + a, k],
            device_id=peer, device_id_type=MESH)

    def own(a):
        return pltpu.make_async_copy(src(a, me), outs[a].at[me], local_sems.at[sem_row + a])

    def start():
        for k, peer in enumerate(peers):
            for a in range(n):
                send(a, k, peer).start()
        for a in range(n):
            own(a).start()

    def wait():
        for k, peer in enumerate(peers):
            for a in range(n):
                arrival(a, k, peer).wait_recv()
        for k, peer in enumerate(peers):
            for a in range(n):
                send(a, k, peer).wait_send()
        for a in range(n):
            own(a).wait()

    return start, wait


def _side_out_shapes(kind, arrays):
    if kind == "gather":
        return [jax.ShapeDtypeStruct((N_DEV,) + a.shape, a.dtype) for a in arrays]
    return [jax.ShapeDtypeStruct(a.shape, a.dtype) for a in arrays]


def _hosted_call(body, name, grid, in_specs, out_specs, out_shape, scratch_shapes, args, side=None):
    if side is None:
        outs = pl.pallas_call(
            body, name=name, grid=grid, in_specs=in_specs, out_specs=out_specs, out_shape=out_shape,
            scratch_shapes=scratch_shapes, compiler_params=_params(len(grid)))(*args)
        return outs, []
    kind, arrays = side
    n_in, n_out, n_scr, n_side = len(in_specs), len(out_specs), len(scratch_shapes), len(arrays)

    def hosted(*refs):
        pos = 0
        groups = []
        for size in (n_in, n_side, n_out, n_side, n_scr):
            groups.append(refs[pos:pos + size])
            pos += size
        ins, side_in, outs, side_out, scr = groups
        send_sems, recv_sems, local_sems = refs[pos:]
        ids = [pl.program_id(d) for d in range(len(grid))]
        is_first = functools.reduce(jnp.logical_and, [i == 0 for i in ids])
        is_last = functools.reduce(jnp.logical_and, [i == g - 1 for i, g in zip(ids, grid)])
        start, wait = _side_copies(kind, side_in, side_out, send_sems, recv_sems, local_sems)
        pl.when(is_first)(start)
        body(*ins, *outs, *scr)
        pl.when(is_last)(wait)

    any_spec = pl.BlockSpec(memory_space=pl.ANY)
    outs = pl.pallas_call(
        hosted, name=name, grid=grid,
        in_specs=list(in_specs) + [any_spec] * n_side,
        out_specs=list(out_specs) + [any_spec] * n_side,
        out_shape=list(out_shape) + _side_out_shapes(kind, arrays),
        scratch_shapes=list(scratch_shapes) + [pltpu.SemaphoreType.DMA((n_side, 7)), pltpu.SemaphoreType.DMA((n_side, 7)),
                                               pltpu.SemaphoreType.DMA((n_side,))],
        compiler_params=_params(len(grid)))(*args, *arrays)
    return outs[:n_out], outs[n_out:]


def _ffn_fwd(h, g_pre, g_post, w_gu, w_down, name, side=None):
    T, D = h.shape
    nj = w_gu.shape[0] // 2
    FB = w_gu.shape[2]
    tm = TOKEN_TILE

    def body(h_ref, gpre_ref, gpost_ref, wg_ref, wu_ref, wd_ref, hout_ref, f_ref, a_ref, gu_ref, a_scr, acc):
        j = pl.program_id(1)

        @pl.when(j == 0)
        def _():
            x = h_ref[...]
            a = (x * _rstd(x) * gpre_ref[...]).astype(BF16)
            a_scr[...] = a
            a_ref[...] = a
            acc[...] = jnp.zeros_like(acc)

        a = a_scr[...]
        g = _mm(a, wg_ref[...])
        u = _mm(a, wu_ref[...])
        gu_ref[0] = g.astype(BF16)
        gu_ref[1] = u.astype(BF16)
        hh = (g * jax.nn.sigmoid(g) * u).astype(BF16)
        acc[...] += _mm(hh, wd_ref[...])

        @pl.when(j == nj - 1)
        def _():
            f = acc[...]
            f_ref[...] = f
            hout_ref[...] = h_ref[...] + 0.5 * (f * _rstd(f) * gpost_ref[...])

    return _hosted_call(
        body, name, (T // tm, nj),
        in_specs=[
            pl.BlockSpec((tm, D), lambda i, j: (i, 0)),
            _full((1, D)), _full((1, D)),
            pl.BlockSpec((None, D, FB), lambda i, j: (j, 0, 0)),
            pl.BlockSpec((None, D, FB), lambda i, j: (j + nj, 0, 0)),
            pl.BlockSpec((FB, D), lambda i, j: (j, 0)),
        ],
        out_specs=[
            pl.BlockSpec((tm, D), lambda i, j: (i, 0)),
            pl.BlockSpec((tm, D), lambda i, j: (i, 0)),
            pl.BlockSpec((tm, D), lambda i, j: (i, 0)),
            pl.BlockSpec((None, 2, tm, FB), lambda i, j: (j, 0, i, 0)),
        ],
        out_shape=[
            jax.ShapeDtypeStruct((T, D), F32),
            jax.ShapeDtypeStruct((T, D), F32),
            jax.ShapeDtypeStruct((T, D), BF16),
            jax.ShapeDtypeStruct((nj, 2, T, FB), BF16),
        ],
        scratch_shapes=[pltpu.VMEM((tm, D), BF16), pltpu.VMEM((tm, D), F32)],
        args=(h, g_pre, g_post, w_gu, w_gu, w_down), side=side)


def _ffn_bwd(dh_out, f, g_post, h, g_pre, gu, w_gu, w_down, name):
    T, D = h.shape
    nj = w_gu.shape[0] // 2
    FB = w_gu.shape[2]
    tm = TOKEN_TILE

    def body(dho_ref, f_ref, gpost_ref, h_ref, gpre_ref, gu_ref, wg_ref, wu_ref, wd_ref,
             dhin_ref, df_ref, hh_ref, dgu_ref, dgpost_ref, dgpre_ref, df_scr, da):
        i, j = pl.program_id(0), pl.program_id(1)

        @pl.when(jnp.logical_and(i == 0, j == 0))
        def _():
            dgpost_ref[...] = jnp.zeros_like(dgpost_ref)
            dgpre_ref[...] = jnp.zeros_like(dgpre_ref)

        @pl.when(j == 0)
        def _():
            fv = f_ref[...]
            df, dgain = _rms_bwd(fv, _rstd(fv), gpost_ref[...], 0.5 * dho_ref[...])
            dgpost_ref[...] += _colsum(dgain)
            dfb = df.astype(BF16)
            df_scr[...] = dfb
            df_ref[...] = dfb
            da[...] = jnp.zeros_like(da)

        dhh = _mm_nt(df_scr[...], wd_ref[...])
        g = gu_ref[0].astype(F32)
        u = gu_ref[1].astype(F32)
        sg = jax.nn.sigmoid(g)
        silu = g * sg
        hh_ref[...] = (silu * u).astype(BF16)
        dg = (dhh * u * (sg * (1.0 + g * (1.0 - sg)))).astype(BF16)
        du = (dhh * silu).astype(BF16)
        dgu_ref[0] = dg
        dgu_ref[1] = du
        da[...] += _mm_nt(dg, wg_ref[...]) + _mm_nt(du, wu_ref[...])

        @pl.when(j == nj - 1)
        def _():
            x = h_ref[...]
            dx, dgain = _rms_bwd(x, _rstd(x), gpre_ref[...], da[...])
            dgpre_ref[...] += _colsum(dgain)
            dhin_ref[...] = dho_ref[...] + dx

    tile = pl.BlockSpec((tm, D), lambda i, j: (i, 0))
    return pl.pallas_call(
        body, name=name, grid=(T // tm, nj),
        in_specs=[
            tile, tile, _full((1, D)), tile, _full((1, D)),
            pl.BlockSpec((None, 2, tm, FB), lambda i, j: (j, 0, i, 0)),
            pl.BlockSpec((None, D, FB), lambda i, j: (j, 0, 0)),
            pl.BlockSpec((None, D, FB), lambda i, j: (j + nj, 0, 0)),
            pl.BlockSpec((FB, D), lambda i, j: (j, 0)),
        ],
        out_specs=[
            tile, tile,
            pl.BlockSpec((None, tm, FB), lambda i, j: (j, i, 0)),
            pl.BlockSpec((None, 2, tm, FB), lambda i, j: (j, 0, i, 0)),
            _full((1, D)), _full((1, D)),
        ],
        out_shape=[
            jax.ShapeDtypeStruct((T, D), F32),
            jax.ShapeDtypeStruct((T, D), BF16),
            jax.ShapeDtypeStruct((nj, T, FB), BF16),
            jax.ShapeDtypeStruct((nj, 2, T, FB), BF16),
            jax.ShapeDtypeStruct((1, D), F32),
            jax.ShapeDtypeStruct((1, D), F32),
        ],
        scratch_shapes=[pltpu.VMEM((tm, D), BF16), pltpu.VMEM((tm, D), F32)],
        compiler_params=_params(2),
    )(dh_out, f, g_post, h, g_pre, gu, w_gu, w_gu, w_down)


def _tn_matmul(x, y, x_spec, y_spec, out_shape, out_spec, n_blocks, n_steps, acc_shape, name, side=None):
    def body(x_ref, y_ref, o_ref, acc):
        t = pl.program_id(1)

        @pl.when(t == 0)
        def _():
            acc[...] = jnp.zeros_like(acc)

        acc[...] += _mm_tn(x_ref[...].astype(BF16), y_ref[...].astype(BF16))

        @pl.when(t == n_steps - 1)
        def _():
            o_ref[...] = acc[...].astype(o_ref.dtype)

    outs, side_outs = _hosted_call(
        body, name, (n_blocks, n_steps), in_specs=[x_spec, y_spec], out_specs=[out_spec], out_shape=[out_shape],
        scratch_shapes=[pltpu.VMEM(acc_shape, F32)], args=(x, y), side=side)
    return (outs[0], side_outs) if side is not None else outs[0]


def _inproj_fwd(h, g_pre, w_in, b_in):
    T, D = h.shape
    tm = TOKEN_TILE

    def body(h_ref, g_ref, w_ref, b_ref, z_ref, a_ref):
        x = h_ref[...]
        a = (x * _rstd(x) * g_ref[...]).astype(BF16)
        a_ref[...] = a
        z_ref[...] = _mm(a, w_ref[...]) + b_ref[...]

    return pl.pallas_call(
        body, name="inproj_fwd", grid=(T // tm,),
        in_specs=[pl.BlockSpec((tm, D), lambda i: (i, 0)), _full((1, D)), _full((D, D_IN)), _full((1, D_IN))],
        out_specs=[pl.BlockSpec((tm, D_IN), lambda i: (i, 0)), pl.BlockSpec((tm, D), lambda i: (i, 0))],
        out_shape=[jax.ShapeDtypeStruct((T, D_IN), F32), jax.ShapeDtypeStruct((T, D), BF16)],
        compiler_params=_params(1),
    )(h, g_pre, w_in, b_in)


def _inproj_bwd(dqa, dka, dva, dqb, dkb, dvb, w_in, h, g_pre, dres):
    T, D = h.shape
    tm = TOKEN_TILE

    def body(dqa_ref, dka_ref, dva_ref, dqb_ref, dkb_ref, dvb_ref, w_ref, h_ref, g_ref, dres_ref,
             dh_ref, dz_ref, dbin_ref, dg_ref):
        i = pl.program_id(0)

        @pl.when(i == 0)
        def _():
            dbin_ref[...] = jnp.zeros_like(dbin_ref)
            dg_ref[...] = jnp.zeros_like(dg_ref)

        dz = jnp.concatenate([dqa_ref[...], dka_ref[...], dva_ref[...], dqb_ref[...], dkb_ref[...], dvb_ref[...]],
                             axis=1)
        dbin_ref[...] += _colsum(dz)
        dzb = dz.astype(BF16)
        dz_ref[...] = dzb
        da = _mm_nt(dzb, w_ref[...])
        x = h_ref[...]
        dx, dgain = _rms_bwd(x, _rstd(x), g_ref[...], da)
        dg_ref[...] += _colsum(dgain)
        dh_ref[...] = dres_ref[...] + dx

    def tile(w):
        return pl.BlockSpec((tm, w), lambda i: (i, 0))

    return pl.pallas_call(
        body, name="inproj_bwd", grid=(T // tm,),
        in_specs=[tile(A_Q), tile(A_KV), tile(A_KV), tile(B_W), tile(B_W), tile(B_W),
                  _full((D, D_IN)), tile(D), _full((1, D)), tile(D)],
        out_specs=[tile(D), tile(D_IN), _full((1, D_IN)), _full((1, D))],
        out_shape=[jax.ShapeDtypeStruct((T, D), F32), jax.ShapeDtypeStruct((T, D_IN), BF16),
                   jax.ShapeDtypeStruct((1, D_IN), F32), jax.ShapeDtypeStruct((1, D), F32)],
        compiler_params=_params(1),
    )(dqa, dka, dva, dqb, dkb, dvb, w_in, h, g_pre, dres)


def _bucket_tiles(patterns):
    i = np.arange(QBLK)[:, None]
    j = np.arange(2 * QBLK)[None, :]
    dist = QBLK + i - j
    max_exact = NUM_BUCKETS // 2
    tiles = []
    for dilation, max_dist in patterns:
        n = np.maximum(dist * dilation, 0)
        nf = np.maximum(n, 1).astype(np.float32)
        large = max_exact + (np.log(nf / np.float32(max_exact)) / np.float32(math.log(MAX_DISTANCE / max_exact))
                             * np.float32(NUM_BUCKETS - max_exact)).astype(np.int32)
        bucket = np.where(n < max_exact, n, np.minimum(large, NUM_BUCKETS - 1))
        tiles.append(np.where((dist >= 0) & (dist <= max_dist), bucket, -1))
    return jnp.asarray(np.stack(tiles).astype(np.int32))


def _bias_build(rel_bias, buckets, head0, name):
    n = buckets.shape[0]

    def body(bk_ref, rb_ref, o_ref):
        bk = bk_ref[...]
        base = jnp.where(bk < 0, NEG_INF, 0.0).astype(F32)
        for hd in range(N_HEAD_GROUP):
            o_ref[hd] = lax.fori_loop(
                0, NUM_BUCKETS, lambda b, acc, hd=hd: jnp.where(bk == b, rb_ref[b, head0 + hd], acc), base)

    return pl.pallas_call(
        body, name=name, grid=(n,),
        in_specs=[pl.BlockSpec((None, QBLK, 2 * QBLK), lambda p: (p, 0, 0)), pl.BlockSpec(memory_space=pltpu.SMEM)],
        out_specs=pl.BlockSpec((None, N_HEAD_GROUP, QBLK, 2 * QBLK), lambda p: (p, 0, 0, 0)),
        out_shape=jax.ShapeDtypeStruct((n, N_HEAD_GROUP, QBLK, 2 * QBLK), F32),
        compiler_params=_params(1),
    )(buckets, rel_bias)


def _bias_grad(ds, buckets, name):
    n = buckets.shape[0]

    def body(ds_ref, bk_ref, o_ref):
        bk = bk_ref[...]
        row = lax.broadcasted_iota(jnp.int32, (NUM_BUCKETS, 2 * QBLK), 0)
        for hd in range(N_HEAD_GROUP):
            d = ds_ref[hd]
            per_key = jnp.zeros((NUM_BUCKETS, 2 * QBLK), F32)
            for b in range(NUM_BUCKETS):
                per_key = jnp.where(row == b, jnp.sum(jnp.where(bk == b, d, 0.0), axis=0, keepdims=True), per_key)
            o_ref[hd] = jnp.broadcast_to(jnp.sum(per_key, axis=1, keepdims=True), (NUM_BUCKETS, LANES))

    out = pl.pallas_call(
        body, name=name, grid=(n,),
        in_specs=[pl.BlockSpec((None, N_HEAD_GROUP, QBLK, 2 * QBLK), lambda p: (p, 0, 0, 0)),
                  pl.BlockSpec((None, QBLK, 2 * QBLK), lambda p: (p, 0, 0))],
        out_specs=pl.BlockSpec((None, N_HEAD_GROUP, NUM_BUCKETS, LANES), lambda p: (p, 0, 0, 0)),
        out_shape=jax.ShapeDtypeStruct((n, N_HEAD_GROUP, NUM_BUCKETS, LANES), F32),
        compiler_params=_params(1),
    )(ds, buckets)
    return out[:, :, :, 0].reshape(n * N_HEAD_GROUP, NUM_BUCKETS)


def _class_rows(start, dilation):
    if dilation == 1:
        return pl.ds(pl.multiple_of(start, QBLK), QBLK)
    return pl.ds(start, QBLK, stride=dilation)


def _block_starts(idx, n_blocks, dilation):
    cls = idx // n_blocks
    n = idx % n_blocks
    cur = cls + dilation * QBLK * n
    prev = cls + dilation * QBLK * jnp.maximum(n - 1, 0)
    return n, cur, prev


class _HeadPair:
    def __init__(self, g, shared_kv):
        self.lane = lax.broadcasted_iota(jnp.int32, (1, LANES), 1)
        self.lower = self.lane < HEAD_DIM
        self.shared_kv = shared_kv
        self.key_lanes = (self.lane >= HEAD_DIM).astype(jnp.int32) == (g // 2)

    def stack(self, t):
        return jnp.concatenate([jnp.where(self.lower, t, 0.0), jnp.where(self.lower, 0.0, t)], axis=0).astype(BF16)

    def unstack(self, t2):
        return jnp.where(self.lower, t2[:QBLK], t2[QBLK:])

    def keys(self, t):
        if self.shared_kv:
            return jnp.where(self.key_lanes, t, pltpu.roll(t, HEAD_DIM, 1))
        return t

    def key_grads(self, t):
        if self.shared_kv:
            return jnp.where(self.key_lanes, t + pltpu.roll(t, HEAD_DIM, 1), 0.0)
        return t


def _attn_specs(T, qcol, kcol, vcol, shared_kv):
    kv = (lambda c: (lambda g: (0, c))) if shared_kv else (lambda c: (lambda g: (0, c + g)))
    return [pl.BlockSpec((T, LANES), lambda g: (0, qcol + g)),
            pl.BlockSpec((T, LANES), kv(kcol)),
            pl.BlockSpec((T, LANES), kv(vcol))]


def _attn_fwd(z, bias, sinks, patterns, qcol, kcol, vcol, shared_kv, name, side=None):
    T = z.shape[0]
    n_pat = len(patterns)
    has_sink = sinks is not None

    def body(*refs):
        if has_sink:
            sink_ref, refs = refs[0], refs[1:]
        q_ref, k_ref, v_ref, b_ref, o_ref, l_ref = refs[:6]
        po_scr = refs[6:6 + n_pat]
        pl_scr = refs[6 + n_pat:]
        g = pl.program_id(0)
        heads = _HeadPair(g, shared_kv)
        in_prev = lax.broadcasted_iota(jnp.int32, (2 * QBLK, 2 * QBLK), 1) < QBLK

        for pi, (dilation, _) in enumerate(patterns):
            n_blocks = T // (QBLK * dilation)

            def step(it, carry, pi=pi, dilation=dilation, n_blocks=n_blocks):
                blocks = []
                for u in range(FWD_BLOCKS):
                    n, cur, prev = _block_starts(it * FWD_BLOCKS + u, n_blocks, dilation)
                    rows_c, rows_p = _class_rows(cur, dilation), _class_rows(prev, dilation)
                    qm = heads.stack(q_ref[rows_c, :])
                    k2 = heads.keys(jnp.concatenate([k_ref[rows_p, :], k_ref[rows_c, :]], axis=0)).astype(BF16)
                    v2 = heads.keys(jnp.concatenate([v_ref[rows_p, :], v_ref[rows_c, :]], axis=0)).astype(BF16)
                    blocks.append(dict(n=n, rows=rows_c, v2=v2, s=_mm_nt(qm, k2)))
                for b in blocks:
                    s = b["s"] * (HEAD_DIM ** -0.5) + b_ref[pi]
                    b["s"] = jnp.where(jnp.logical_and(in_prev, b["n"] == 0), NEG_INF, s)
                    b["m"] = jnp.max(b["s"], axis=1, keepdims=True)
                for b in blocks:
                    b["pr"] = jnp.exp(b["s"] - b["m"])
                    b["den"] = jnp.sum(b["pr"], axis=1, keepdims=True)
                for b in blocks:
                    b["o2"] = _mm(b["pr"].astype(BF16), b["v2"])
                for b in blocks:
                    lse = b["m"] + jnp.log(b["den"])
                    po_scr[pi][b["rows"], :] = heads.unstack(b["o2"] / b["den"])
                    pl_scr[2 * pi][b["rows"], :] = jnp.broadcast_to(lse[:QBLK], (QBLK, LANES))
                    pl_scr[2 * pi + 1][b["rows"], :] = jnp.broadcast_to(lse[QBLK:], (QBLK, LANES))
                return carry

            lax.fori_loop(0, (dilation * n_blocks) // FWD_BLOCKS, step, 0)

        def merge(ci, carry):
            rows = pl.ds(pl.multiple_of(ci * QBLK, QBLK), QBLK)
            weights = []
            for hd in range(2):
                parts = [pl_scr[2 * pi + hd][rows, :] for pi in range(n_pat)]
                m = functools.reduce(jnp.maximum, parts)
                if has_sink:
                    sink = sink_ref[0, 2 * g + hd]
                    m = jnp.maximum(m, sink)
                den = functools.reduce(jnp.add, [jnp.exp(x - m) for x in parts])
                if has_sink:
                    den = den + jnp.exp(sink - m)
                lse = m + jnp.log(den)
                l_ref[hd, rows, :] = lse
                weights.append([jnp.exp(x - lse) for x in parts])
            o_ref[rows, :] = functools.reduce(
                jnp.add, [jnp.where(heads.lower, weights[0][pi], weights[1][pi]) * po_scr[pi][rows, :]
                          for pi in range(n_pat)])
            return carry

        lax.fori_loop(0, T // QBLK, merge, 0)

    in_specs = _attn_specs(T, qcol, kcol, vcol, shared_kv)
    in_specs.append(pl.BlockSpec((n_pat, None, 2 * QBLK, 2 * QBLK), lambda g: (0, g, 0, 0)))
    args = [z, z, z, bias.reshape(n_pat, N_HEAD_GROUP // 2, 2 * QBLK, 2 * QBLK)]
    if has_sink:
        in_specs.insert(0, pl.BlockSpec(memory_space=pltpu.SMEM))
        args.insert(0, sinks)
    return _hosted_call(
        body, name, (N_HEAD_GROUP // 2,),
        in_specs=in_specs,
        out_specs=[pl.BlockSpec((T, LANES), lambda g: (0, g)), pl.BlockSpec((2, T, LANES), lambda g: (g, 0, 0))],
        out_shape=[jax.ShapeDtypeStruct((T, N_HEAD_GROUP * HEAD_DIM), F32),
                   jax.ShapeDtypeStruct((N_HEAD_GROUP, T, LANES), F32)],
        scratch_shapes=[pltpu.VMEM((T, LANES), F32)] * (3 * n_pat), args=args, side=side)


def _attn_bwd(z, bias, sinks, d_out, out, lse, patterns, qcol, kcol, vcol, shared_kv, name, side=None):
    T = z.shape[0]
    n_pat = len(patterns)
    has_sink = sinks is not None
    kv_width = LANES if shared_kv else N_HEAD_GROUP * HEAD_DIM

    def body(*refs):
        if has_sink:
            sink_ref, refs = refs[0], refs[1:]
        q_ref, k_ref, v_ref, b_ref, do_ref, o_ref, l0_ref, l1_ref = refs[:8]
        dq_ref, dk_ref, dv_ref, ds_ref = refs[8:12]
        dsink_ref = refs[12] if has_sink else None
        dk_acc, dv_acc = refs[-2:]
        g = pl.program_id(0)
        heads = _HeadPair(g, shared_kv)
        in_prev = lax.broadcasted_iota(jnp.int32, (2 * QBLK, 2 * QBLK), 1) < QBLK

        dq_ref[...] = jnp.zeros_like(dq_ref)
        ds_ref[...] = jnp.zeros_like(ds_ref)
        dk_acc[...] = jnp.zeros_like(dk_acc)
        dv_acc[...] = jnp.zeros_like(dv_acc)

        dsink = jnp.zeros((1, LANES), F32)
        for pi, (dilation, _) in enumerate(patterns):
            n_blocks = T // (QBLK * dilation)

            def step(idx, dsink, pi=pi, dilation=dilation, n_blocks=n_blocks):
                blocks = []
                for u in range(BWD_BLOCKS):
                    n, cur, prev = _block_starts(idx * BWD_BLOCKS + u, n_blocks, dilation)
                    rows_c, rows_p = _class_rows(cur, dilation), _class_rows(prev, dilation)
                    qm = heads.stack(q_ref[rows_c, :])
                    k2 = heads.keys(jnp.concatenate([k_ref[rows_p, :], k_ref[rows_c, :]], axis=0)).astype(BF16)
                    v2 = heads.keys(jnp.concatenate([v_ref[rows_p, :], v_ref[rows_c, :]], axis=0)).astype(BF16)
                    d_o = do_ref[rows_c, :]
                    dom = heads.stack(d_o)
                    dd = d_o * o_ref[rows_c, :]
                    delta = jnp.concatenate([jnp.sum(jnp.where(heads.lower, dd, 0.0), axis=1, keepdims=True),
                                             jnp.sum(jnp.where(heads.lower, 0.0, dd), axis=1, keepdims=True)], axis=0)
                    lse = jnp.concatenate([l0_ref[rows_c, :], l1_ref[rows_c, :]], axis=0)
                    blocks.append(dict(n=n, rows_c=rows_c, rows_p=rows_p, qm=qm, k2=k2, dom=dom, delta=delta, lse=lse,
                                       s=_mm_nt(qm, k2), dp=_mm_nt(dom, v2)))
                for b in blocks:
                    s = b["s"] * (HEAD_DIM ** -0.5) + b_ref[pi]
                    s = jnp.where(jnp.logical_and(in_prev, b["n"] == 0), NEG_INF, s)
                    b["pr"] = jnp.exp(s - jnp.concatenate([b["lse"], b["lse"]], axis=1))
                    b["ds"] = b["pr"] * (b["dp"] - b["delta"])
                for b in blocks:
                    dsb = b["ds"].astype(BF16)
                    b["dq2"] = _mm(dsb, b["k2"])
                    b["dk2"] = _mm_tn(dsb, b["qm"])
                    b["dv2"] = _mm_tn(b["pr"].astype(BF16), b["dom"])
                for b in blocks:
                    ds_ref[pi] += b["ds"]
                    dq_ref[b["rows_c"], :] += heads.unstack(b["dq2"]) * (HEAD_DIM ** -0.5)
                    dk2 = heads.key_grads(b["dk2"]) * (HEAD_DIM ** -0.5)
                    dv2 = heads.key_grads(b["dv2"])
                    dk_acc[b["rows_p"], :] += dk2[:QBLK]
                    dk_acc[b["rows_c"], :] += dk2[QBLK:]
                    dv_acc[b["rows_p"], :] += dv2[:QBLK]
                    dv_acc[b["rows_c"], :] += dv2[QBLK:]
                    if has_sink:
                        for hd in range(2):
                            rows_h = slice(QBLK * hd, QBLK * (hd + 1))
                            p_sink = jnp.exp(sink_ref[0, 2 * g + hd] - b["lse"][rows_h, 0:1])
                            dsink = dsink - jnp.where(heads.lane == 2 * g + hd,
                                                      jnp.sum(p_sink * b["delta"][rows_h]), 0.0)
                return dsink

            dsink = lax.fori_loop(0, (dilation * n_blocks) // BWD_BLOCKS, step, dsink)

        if shared_kv:
            @pl.when(g == 0)
            def _():
                dk_ref[...] = dk_acc[...]
                dv_ref[...] = dv_acc[...]

            @pl.when(g != 0)
            def _():
                dk_ref[...] += dk_acc[...]
                dv_ref[...] += dv_acc[...]
        else:
            dk_ref[...] = dk_acc[...]
            dv_ref[...] = dv_acc[...]

        if has_sink:
            @pl.when(g == 0)
            def _():
                dsink_ref[...] = dsink

            @pl.when(g != 0)
            def _():
                dsink_ref[...] += dsink

    pair = pl.BlockSpec((T, LANES), lambda g: (0, g))
    stacked = pl.BlockSpec((n_pat, None, 2 * QBLK, 2 * QBLK), lambda g: (0, g, 0, 0))
    stacked_shape = (n_pat, N_HEAD_GROUP // 2, 2 * QBLK, 2 * QBLK)
    in_specs = _attn_specs(T, qcol, kcol, vcol, shared_kv)
    in_specs += [stacked, pair, pair,
                 pl.BlockSpec((None, T, LANES), lambda g: (2 * g, 0, 0)),
                 pl.BlockSpec((None, T, LANES), lambda g: (2 * g + 1, 0, 0))]
    args = [z, z, z, bias.reshape(stacked_shape), d_out, out, lse, lse]
    kv_out = _full((T, LANES)) if shared_kv else pair
    out_specs = [pair, kv_out, kv_out, stacked]
    out_shape = [jax.ShapeDtypeStruct((T, N_HEAD_GROUP * HEAD_DIM), F32),
                 jax.ShapeDtypeStruct((T, kv_width), F32), jax.ShapeDtypeStruct((T, kv_width), F32),
                 jax.ShapeDtypeStruct(stacked_shape, F32)]
    if has_sink:
        in_specs.insert(0, pl.BlockSpec(memory_space=pltpu.SMEM))
        args.insert(0, sinks)
        out_specs.append(_full((1, LANES)))
        out_shape.append(jax.ShapeDtypeStruct((1, LANES), F32))
    outs, side_outs = _hosted_call(
        body, name, (N_HEAD_GROUP // 2,), in_specs=in_specs, out_specs=out_specs, out_shape=out_shape,
        scratch_shapes=[pltpu.VMEM((T, LANES), F32), pltpu.VMEM((T, LANES), F32)], args=args, side=side)
    outs = list(outs)
    outs[3] = outs[3].reshape(n_pat, N_HEAD_GROUP, QBLK, 2 * QBLK)
    return outs, side_outs


def _outproj_fwd(mix_a, mix_b, w_out, b_out, g_post, h):
    T, D = h.shape
    tm = TOKEN_TILE
    d_mix = w_out.shape[0]

    def body(ma_ref, mb_ref, w_ref, b_ref, g_ref, h_ref, att_ref, hout_ref, mix_ref):
        mix = jnp.concatenate([ma_ref[...], mb_ref[...]], axis=1).astype(BF16)
        mix_ref[...] = mix
        att = _mm(mix, w_ref[...]) + b_ref[...]
        att_ref[...] = att
        hout_ref[...] = h_ref[...] + att * _rstd(att) * g_ref[...]

    def tile(w):
        return pl.BlockSpec((tm, w), lambda i: (i, 0))

    return pl.pallas_call(
        body, name="outproj_fwd", grid=(T // tm,),
        in_specs=[tile(A_Q), tile(B_W), _full((d_mix, D)), _full((1, D)), _full((1, D)), tile(D)],
        out_specs=[tile(D), tile(D), tile(d_mix)],
        out_shape=[jax.ShapeDtypeStruct((T, D), F32), jax.ShapeDtypeStruct((T, D), F32),
                   jax.ShapeDtypeStruct((T, d_mix), BF16)],
        compiler_params=_params(1),
    )(mix_a, mix_b, w_out, b_out, g_post, h)


def _outproj_bwd(dh, att, g_post, w_out):
    T, D = dh.shape
    tm = TOKEN_TILE
    d_mix = w_out.shape[0]

    def body(dh_ref, att_ref, g_ref, w_ref, dma_ref, dmb_ref, datt_ref, dg_ref, db_ref):
        i = pl.program_id(0)

        @pl.when(i == 0)
        def _():
            dg_ref[...] = jnp.zeros_like(dg_ref)
            db_ref[...] = jnp.zeros_like(db_ref)

        att = att_ref[...]
        datt, dgain = _rms_bwd(att, _rstd(att), g_ref[...], dh_ref[...])
        dg_ref[...] += _colsum(dgain)
        db_ref[...] += _colsum(datt)
        dattb = datt.astype(BF16)
        datt_ref[...] = dattb
        dmix = _mm_nt(dattb, w_ref[...])
        dma_ref[...] = dmix[:, :A_Q]
        dmb_ref[...] = dmix[:, A_Q:]

    def tile(w):
        return pl.BlockSpec((tm, w), lambda i: (i, 0))

    return pl.pallas_call(
        body, name="outproj_bwd", grid=(T // tm,),
        in_specs=[tile(D), tile(D), _full((1, D)), _full((d_mix, D))],
        out_specs=[tile(A_Q), tile(B_W), tile(D), _full((1, D)), _full((1, D))],
        out_shape=[jax.ShapeDtypeStruct((T, A_Q), F32), jax.ShapeDtypeStruct((T, B_W), F32),
                   jax.ShapeDtypeStruct((T, D), BF16), jax.ShapeDtypeStruct((1, D), F32),
                   jax.ShapeDtypeStruct((1, D), F32)],
        compiler_params=_params(1),
    )(dh, att, g_post, w_out)


def _ple_fwd_loss(h, g_pre, w_gate, p, w_proj, g_post, target):
    T, D = h.shape
    tm = TOKEN_TILE
    n_proj, ple, db = w_proj.shape

    def body(h_ref, gpre_ref, wg_ref, p_ref, wp_ref, gpost_ref, t_ref,
             a_ref, dpre_ref, de_ref, dh_ref, loss_ref, dgpost_ref):
        i = pl.program_id(0)

        @pl.when(i == 0)
        def _():
            loss_ref[...] = jnp.zeros_like(loss_ref)
            dgpost_ref[...] = jnp.zeros_like(dgpost_ref)

        x = h_ref[...]
        a = (x * _rstd(x) * gpre_ref[...]).astype(BF16)
        a_ref[...] = a
        gate = jax.nn.sigmoid(_mm(a, wg_ref[...]))
        pb = p_ref[...].astype(BF16)
        e = jnp.concatenate([_mm(pb, wp_ref[k]) for k in range(n_proj)], axis=1)
        ge = gate * e
        rg = _rstd(ge)
        diff = x + ge * rg * gpost_ref[...] - t_ref[...]
        loss_ref[...] += 0.5 * jnp.sum(jnp.mean(diff * diff, axis=1, keepdims=True))
        dy = diff * (1.0 / D)
        dh_ref[...] = dy
        dge, dgain = _rms_bwd(ge, rg, gpost_ref[...], dy)
        dgpost_ref[...] += _colsum(dgain)
        de_ref[...] = (dge * gate).astype(BF16)
        dpre_ref[...] = (dge * e * gate * (1.0 - gate)).astype(BF16)

    def tile(w):
        return pl.BlockSpec((tm, w), lambda i: (i, 0))

    return pl.pallas_call(
        body, name="ple_fwd_loss", grid=(T // tm,),
        in_specs=[tile(D), _full((1, D)), _full((D, D)), tile(ple), _full((n_proj, ple, db)), _full((1, D)), tile(D)],
        out_specs=[tile(D), tile(D), tile(D), tile(D), _full((1, LANES)), _full((1, D))],
        out_shape=[jax.ShapeDtypeStruct((T, D), BF16),
                   jax.ShapeDtypeStruct((T, D), BF16),
                   jax.ShapeDtypeStruct((T, D), BF16),
                   jax.ShapeDtypeStruct((T, D), F32),
                   jax.ShapeDtypeStruct((1, LANES), F32),
                   jax.ShapeDtypeStruct((1, D), F32)],
        compiler_params=_params(1),
    )(h, g_pre, w_gate, p, w_proj, g_post, target)


def _ple_bwd(dpre, w_gate, h, g_pre, dres):
    T, D = h.shape
    tm = TOKEN_TILE

    def body(dpre_ref, w_ref, h_ref, g_ref, dres_ref, dh_ref, dg_ref):
        i = pl.program_id(0)

        @pl.when(i == 0)
        def _():
            dg_ref[...] = jnp.zeros_like(dg_ref)

        da = _mm_nt(dpre_ref[...], w_ref[...])
        x = h_ref[...]
        dx, dgain = _rms_bwd(x, _rstd(x), g_ref[...], da)
        dg_ref[...] += _colsum(dgain)
        dh_ref[...] = dres_ref[...] + dx

    tile = pl.BlockSpec((tm, D), lambda i: (i, 0))
    return pl.pallas_call(
        body, name="ple_bwd", grid=(T // tm,),
        in_specs=[tile, _full((D, D)), tile, _full((1, D)), tile],
        out_specs=[tile, _full((1, D))],
        out_shape=[jax.ShapeDtypeStruct((T, D), F32), jax.ShapeDtypeStruct((1, D), F32)],
        compiler_params=_params(1),
    )(dpre, w_gate, h, g_pre, dres)


def _ple_dw_proj(p, de, n_proj):
    T, ple = p.shape
    D = de.shape[1]
    db = D // n_proj
    tk = TOKEN_TILE
    nt = T // tk

    def body(p_ref, de_ref, o_ref, acc):
        t = pl.program_id(0)

        @pl.when(t == 0)
        def _():
            acc[...] = jnp.zeros_like(acc)

        acc[...] += _mm_tn(p_ref[...].astype(BF16), de_ref[...])

        @pl.when(t == nt - 1)
        def _():
            for k in range(n_proj):
                o_ref[k] = acc[:, k * db:(k + 1) * db].astype(BF16)

    return pl.pallas_call(
        body, name="ple_dw_proj", grid=(nt,),
        in_specs=[pl.BlockSpec((tk, ple), lambda t: (t, 0)), pl.BlockSpec((tk, D), lambda t: (t, 0))],
        out_specs=_full((n_proj, ple, db)), out_shape=jax.ShapeDtypeStruct((n_proj, ple, db), BF16),
        scratch_shapes=[pltpu.VMEM((ple, D), F32)], compiler_params=_params(1),
    )(p, de)


def _tok(width):
    return pl.BlockSpec((TOKEN_TILE, width), lambda b, t: (t, 0))


def _dw_gu(a, dgu, name, side=None):
    T, D = a.shape
    nj, _, _, FB = dgu.shape
    return _tn_matmul(
        a, dgu, _tok(D), pl.BlockSpec((None, None, TOKEN_TILE, FB), lambda b, t: (b % nj, b // nj, t, 0)),
        jax.ShapeDtypeStruct((2 * nj, D, FB), BF16), pl.BlockSpec((None, D, FB), lambda b, t: (b, 0, 0)),
        2 * nj, T // TOKEN_TILE, (D, FB), name, side=side)


def _dw_down(hh, df, name, side=None):
    nj, T, FB = hh.shape
    D = df.shape[1]
    return _tn_matmul(
        hh, df, pl.BlockSpec((None, TOKEN_TILE, FB), lambda b, t: (b, t, 0)), _tok(D),
        jax.ShapeDtypeStruct((nj, FB, D), BF16), pl.BlockSpec((None, FB, D), lambda b, t: (b, 0, 0)),
        nj, T // TOKEN_TILE, (FB, D), name, side=side)


def _dw_rows(xm, y, name, rows):
    T, k = xm.shape
    D = y.shape[1]
    out = _tn_matmul(
        xm, y, pl.BlockSpec((TOKEN_TILE, rows), lambda b, t: (t, b)), _tok(D),
        jax.ShapeDtypeStruct((k, D), BF16), pl.BlockSpec((rows, D), lambda b, t: (b, 0)),
        k // rows, T // TOKEN_TILE, (rows, D), name)
    return out.reshape(N_DEV, k // N_DEV, D)


def _cast_bf16(arrays):
    n = len(arrays)

    def body(*refs):
        for a in range(n):
            refs[n + a][...] = refs[a][...].astype(BF16)

    return pl.pallas_call(
        body, name="cast_shards",
        in_specs=[pl.BlockSpec(memory_space=pltpu.VMEM)] * n, out_specs=[pl.BlockSpec(memory_space=pltpu.VMEM)] * n,
        out_shape=[jax.ShapeDtypeStruct(a.shape, BF16) for a in arrays],
        compiler_params=pltpu.CompilerParams(vmem_limit_bytes=VMEM_LIMIT),
    )(*arrays)


def _all_gather_bf16(shards):
    n = len(shards)

    def body(*refs):
        ins, outs, scr = refs[:n], refs[n:2 * n], refs[2 * n:3 * n]
        send_sems, recv_sems, local_sems = refs[3 * n:]
        x, y, c = _mesh_place()
        me, sibling = (x, y, c), (x, y, 1 - c)
        chips = [(1 - x, y), (x, 1 - y), (1 - x, 1 - y)]
        for a in range(n):
            scr[a][...] = ins[a][...].astype(BF16)

        def copy(a, k, block, to, src=None):
            dst = outs[a].at[_slot(block)]
            return pltpu.make_async_remote_copy(
                src_ref=dst if src is None else src, dst_ref=dst,
                send_sem=send_sems.at[a, k], recv_sem=recv_sems.at[a, k], device_id=to, device_id_type=MESH)

        mine = [pltpu.make_async_copy(scr[a], outs[a].at[_slot(me)], local_sems.at[a]) for a in range(n)]
        first = [copy(a, 1 + j, me, (*chip, c), src=scr[a]) for j, chip in enumerate(chips) for a in range(n)]
        first += [copy(a, 0, me, sibling, src=scr[a]) for a in range(n)]
        for cp in first + mine:
            cp.start()
        passed = []
        for j, chip in enumerate(chips):
            for a in range(n):
                copy(a, 1 + j, (*chip, c), me).wait_recv()
                cp = copy(a, 4 + j, (*chip, c), sibling)
                cp.start()
                passed.append(cp)
        for a in range(n):
            copy(a, 0, sibling, me).wait_recv()
        for j, chip in enumerate(chips):
            for a in range(n):
                copy(a, 4 + j, (*chip, 1 - c), me).wait_recv()
        for cp in first + passed:
            cp.wait_send()
        for cp in mine:
            cp.wait()

    return pl.pallas_call(
        body, name="weights_all_gather",
        in_specs=[pl.BlockSpec(memory_space=pltpu.VMEM)] * n,
        out_specs=[pl.BlockSpec(memory_space=pl.ANY)] * n,
        out_shape=[jax.ShapeDtypeStruct((N_DEV,) + s.shape, BF16) for s in shards],
        scratch_shapes=[pltpu.VMEM(s.shape, BF16) for s in shards]
        + [pltpu.SemaphoreType.DMA((n, 7)), pltpu.SemaphoreType.DMA((n, 7)), pltpu.SemaphoreType.DMA((n,))],
        compiler_params=pltpu.CompilerParams(vmem_limit_bytes=VMEM_LIMIT),
    )(*shards)


def _pack_layout(D, n_rel_rows):
    n_bin = -(-D_IN // D)
    row_bin = len(GAINS)
    row_sink = row_bin + n_bin
    row_loss = row_sink + 1
    row_rb = -(-(row_loss + 1) // 8) * 8
    n_rows = row_rb + -(-n_rel_rows // 8) * 8
    bin_parts = [(r, min(D, D_IN - r * D)) for r in range(n_bin)]
    return row_bin, row_sink, row_loss, row_rb, n_rows, bin_parts


def _final_exchange(grad_blocks, partials, loss):
    D = partials["ffn1_pre_g"].shape[1]
    rb_shape = partials["rel_bias"].shape
    row_bin, row_sink, row_loss, row_rb, n_rows, bin_parts = _pack_layout(D, rb_shape[0])
    n_small = len(SMALL)

    def body(*refs):
        g_in = refs[0]
        part = dict(zip(SMALL, refs[1:1 + n_small]))
        loss_ref = refs[1 + n_small]
        landed, gath, pack, send_sems, recv_sems, local_sems = refs[2 + n_small:]

        pack[...] = jnp.zeros_like(pack)
        for i, name in enumerate(GAINS):
            pack[i:i + 1, :] = part[name][...]
        for r, width in bin_parts:
            pack[row_bin + r:row_bin + r + 1, 0:width] = part["b_in"][:, r * D:r * D + width]
        pack[row_sink:row_sink + 1, 0:LANES] = part["sinks"][...]
        pack[row_loss:row_loss + 1, 0:LANES] = loss_ref[...]
        pack[row_rb:row_rb + rb_shape[0], 0:rb_shape[1]] = part["rel_bias"][...]

        small_start, small_wait = _side_copies("gather", [pack], [gath], send_sems, recv_sems, local_sems, sem_row=0)
        big_start, big_wait = _side_copies("exchange", [g_in], [landed], send_sems, recv_sems, local_sems, sem_row=1)
        small_start()
        big_start()
        small_wait()
        big_wait()

    args = [grad_blocks] + [partials[k] for k in SMALL] + [loss]
    vmem = pl.BlockSpec(memory_space=pltpu.VMEM)
    any_spec = pl.BlockSpec(memory_space=pl.ANY)
    return pl.pallas_call(
        body, name="final_exchange",
        in_specs=[any_spec] + [vmem] * (n_small + 1),
        out_specs=[any_spec, any_spec],
        out_shape=[jax.ShapeDtypeStruct(grad_blocks.shape, grad_blocks.dtype),
                   jax.ShapeDtypeStruct((N_DEV, n_rows, D), F32)],
        scratch_shapes=[pltpu.VMEM((n_rows, D), F32), pltpu.SemaphoreType.DMA((2, 7)),
                        pltpu.SemaphoreType.DMA((2, 7)), pltpu.SemaphoreType.DMA((2,))],
    )(*args)


def _adamw(w, g, m, v):
    m = ADAM_B1 * m + (1.0 - ADAM_B1) * g
    v = ADAM_B2 * v + (1.0 - ADAM_B2) * (g * g)
    m_hat = m / (1.0 - ADAM_B1 ** ADAM_STEP)
    v_hat = v / (1.0 - ADAM_B2 ** ADAM_STEP)
    return -ADAM_LR * (m_hat / (jnp.sqrt(v_hat) + ADAM_EPS) + ADAM_WD * w), m, v


def _sum_adamw(partials, w, m, v, rows, name):
    R, C = w.shape

    def body(p_ref, w_ref, m_ref, v_ref, g_ref, d_ref, nm_ref, nv_ref):
        g = p_ref[0].astype(F32)
        for k in range(1, N_DEV):
            g = g + p_ref[k].astype(F32)
        g_ref[...] = g
        d_ref[...], nm_ref[...], nv_ref[...] = _adamw(w_ref[...], g, m_ref[...], v_ref[...])

    tile = pl.BlockSpec((rows, C), lambda i: (i, 0))
    return pl.pallas_call(
        body, name=name, grid=(R // rows,),
        in_specs=[pl.BlockSpec((N_DEV, rows, C), lambda i: (0, i, 0)), tile, tile, tile],
        out_specs=[tile] * 4, out_shape=[jax.ShapeDtypeStruct((R, C), F32)] * 4,
        compiler_params=_params(1),
    )(partials, w, m, v)


def _small_adamw(gathered, ws, ms, vs):
    D = ws["ffn1_pre_g"].shape[1]
    n_sink = ws["sinks"].shape[1]
    rb_shape = ws["rel_bias"].shape
    row_bin, row_sink, row_loss, row_rb, n_rows, bin_parts = _pack_layout(D, rb_shape[0])
    n_small = len(SMALL)

    def body(*refs):
        gath = refs[0]
        pos = 1
        w_ref = dict(zip(SMALL, refs[pos:pos + n_small]))
        m_ref = dict(zip(SMALL, refs[pos + n_small:pos + 2 * n_small]))
        v_ref = dict(zip(SMALL, refs[pos + 2 * n_small:pos + 3 * n_small]))
        pos += 3 * n_small
        outs = {name: refs[pos + 4 * i:pos + 4 * i + 4] for i, name in enumerate(SMALL)}
        loss_out = refs[pos + 4 * n_small]
        pack = refs[pos + 4 * n_small + 1]

        total = gath[0]
        for k in range(1, N_DEV):
            total = total + gath[k]
        pack[...] = total

        def update(name, g):
            g_out, d_out, m_out, v_out = outs[name]
            g_out[...] = g
            d_out[...], m_out[...], v_out[...] = _adamw(w_ref[name][...], g, m_ref[name][...], v_ref[name][...])

        for i, name in enumerate(GAINS):
            update(name, pack[i:i + 1, :])
        update("b_in", jnp.concatenate([pack[row_bin + r:row_bin + r + 1, 0:width] for r, width in bin_parts], axis=1))
        update("sinks", pack[row_sink:row_sink + 1, 0:n_sink])
        update("rel_bias", pack[row_rb:row_rb + rb_shape[0], 0:rb_shape[1]])
        loss_out[...] = pack[row_loss:row_loss + 1, 0:LANES]

    args = [gathered]
    for group in (ws, ms, vs):
        args += [group[k] for k in SMALL]
    out_shape = []
    for name in SMALL:
        out_shape += [jax.ShapeDtypeStruct(ws[name].shape, F32)] * 4
    out_shape.append(jax.ShapeDtypeStruct((1, LANES), F32))
    res = pl.pallas_call(
        body, name="small_adamw",
        in_specs=[pl.BlockSpec(memory_space=pltpu.VMEM)] * len(args),
        out_specs=[pl.BlockSpec(memory_space=pltpu.VMEM)] * len(out_shape),
        out_shape=out_shape,
        scratch_shapes=[pltpu.VMEM((n_rows, D), F32)],
    )(*args)
    per_name = {name: res[4 * i:4 * i + 4] for i, name in enumerate(SMALL)}
    return per_name, res[-1]


def _adamw_rows(name, rows_total):
    if name.endswith("w_down"):
        return rows_total // 2
    return min(rows_total, 256)


def kernel(x, p, rel_bias, ffn1_pre_g, ffn1_w_gu, ffn1_w_down, ffn1_post_g, attn_pre_g, w_in, b_in, sinks, w_out, b_out, attn_post_g, ffn2_pre_g, ffn2_w_gu, ffn2_w_down, ffn2_post_g, ple_pre_g, w_ple_gate, w_ple_proj, ple_post_g, loss_target, m_rel_bias, m_ffn1_pre_g, m_ffn1_w_gu, m_ffn1_w_down, m_ffn1_post_g, m_attn_pre_g, m_w_in, m_b_in, m_sinks, m_w_out, m_b_out, m_attn_post_g, m_ffn2_pre_g, m_ffn2_w_gu, m_ffn2_w_down, m_ffn2_post_g, m_ple_pre_g, m_w_ple_gate, m_w_ple_proj, m_ple_post_g, v_rel_bias, v_ffn1_pre_g, v_ffn1_w_gu, v_ffn1_w_down, v_ffn1_post_g, v_attn_pre_g, v_w_in, v_b_in, v_sinks, v_w_out, v_b_out, v_attn_post_g, v_ffn2_pre_g, v_ffn2_w_gu, v_ffn2_w_down, v_ffn2_post_g, v_ple_pre_g, v_w_ple_gate, v_w_ple_proj, v_ple_post_g):
    given = dict(locals())
    ws = {k: given[k] for k in WEIGHTS}
    ms = {k: given["m_" + k] for k in WEIGHTS}
    vs = {k: given["v_" + k] for k in WEIGHTS}

    def shard(t):
        return t.reshape(t.shape[1:])

    xs, ps, target = shard(x), shard(shard(p)), shard(loss_target)
    T, D = xs.shape
    small = {k: ws[k] for k in SMALL}
    shards = {k: shard(ws[k]) for k in BIG}

    w_gu1, w_down1 = _all_gather_bf16([shards["ffn1_w_gu"], shards["ffn1_w_down"]])
    w_down1 = w_down1.reshape(-1, D)
    later = ("w_in", "w_out", "ffn2_w_gu", "ffn2_w_down", "w_ple_gate", "w_ple_proj")
    cast = dict(zip(later, _cast_bf16([shards[k] for k in later])))

    buckets_a = _bucket_tiles(PATTERNS_A)
    buckets_b = _bucket_tiles(PATTERNS_B)
    bias_a = _bias_build(small["rel_bias"], buckets_a, 0, "bias_build_a")
    bias_b = _bias_build(small["rel_bias"], buckets_b, N_HEAD_GROUP, "bias_build_b")
    a_cfg = dict(patterns=PATTERNS_A, qcol=Q_A_COL, kcol=K_A_COL, vcol=V_A_COL, shared_kv=True)
    b_cfg = dict(patterns=PATTERNS_B, qcol=Q_B_COL, kcol=K_B_COL, vcol=V_B_COL, shared_kv=False)

    (h1, f1, a1, gu1), (w_in_g, w_out_g) = _ffn_fwd(
        xs, small["ffn1_pre_g"], small["ffn1_post_g"], w_gu1, w_down1, "ffn1_fwd",
        side=("gather", [cast["w_in"], cast["w_out"]]))
    w_in_full = jnp.transpose(w_in_g, (1, 0, 2)).reshape(D, D_IN)
    w_out_full = w_out_g.reshape(-1, D)
    z, a2 = _inproj_fwd(h1, small["attn_pre_g"], w_in_full, small["b_in"])
    (mix_a, lse_a), _ = _attn_fwd(z, bias_a, small["sinks"], name="attn_a_fwd", **a_cfg)
    (mix_b, lse_b), (w_gu2, w_down2, w_gate, w_proj) = _attn_fwd(
        z, bias_b, None, name="attn_b_fwd", **b_cfg,
        side=("gather", [cast["ffn2_w_gu"], cast["ffn2_w_down"], cast["w_ple_gate"], cast["w_ple_proj"]]))
    w_down2 = w_down2.reshape(-1, D)
    w_gate = w_gate.reshape(-1, D)
    att, h2, mix = _outproj_fwd(mix_a, mix_b, w_out_full, small["b_out"], small["attn_post_g"], h1)
    (h3, f2, a3, gu2), _ = _ffn_fwd(h2, small["ffn2_pre_g"], small["ffn2_post_g"], w_gu2, w_down2, "ffn2_fwd")
    a4, dpre, de, dh4, loss, dg_ple_post = _ple_fwd_loss(
        h3, small["ple_pre_g"], w_gate, ps, w_proj, small["ple_post_g"], target)

    dh3, dg_ple_pre = _ple_bwd(dpre, w_gate, h3, small["ple_pre_g"], dh4)
    d_gate = _dw_rows(a4, dpre, "ple_dw_gate", min(256, D))
    d_proj = _ple_dw_proj(ps, de, N_DEV)
    dh2, df2, hh2, dgu2, dg_f2_post, dg_f2_pre = _ffn_bwd(
        dh3, f2, small["ffn2_post_g"], h2, small["ffn2_pre_g"], gu2, w_gu2, w_down2, "ffn2_bwd")
    d_gu2 = _dw_gu(a3, dgu2, "ffn2_dw_gu")
    d_down2 = _dw_down(hh2, df2, "ffn2_dw_down").reshape(N_DEV, -1, D)
    dmix_a, dmix_b, datt, dg_attn_post, db_out = _outproj_bwd(dh2, att, small["attn_post_g"], w_out_full)
    d_out = _dw_rows(mix, datt, "attn_dw_out", 256)
    (dqa, dka, dva, ds_a, dsinks), _ = _attn_bwd(
        z, bias_a, small["sinks"], dmix_a, mix_a, lse_a, name="attn_a_bwd", **a_cfg)
    early = ("w_ple_gate", "w_ple_proj", "ffn2_w_gu", "ffn2_w_down", "w_out")
    (dqb, dkb, dvb, ds_b), landed_early = _attn_bwd(
        z, bias_b, None, dmix_b, mix_b, lse_b, name="attn_b_bwd", **b_cfg,
        side=("exchange", [d_gate, d_proj, d_gu2, d_down2, d_out]))
    landed = dict(zip(early, landed_early))
    dh1, dz, db_in, dg_attn_pre = _inproj_bwd(dqa, dka, dva, dqb, dkb, dvb, w_in_full, h1, small["attn_pre_g"], dh2)
    cols = D_IN // 3
    d_in = _tn_matmul(
        a2, dz, _tok(D), pl.BlockSpec((TOKEN_TILE, cols), lambda b, t: (t, b)),
        jax.ShapeDtypeStruct((D, D_IN), BF16), pl.BlockSpec((D, cols), lambda b, t: (0, b)),
        3, T // TOKEN_TILE, (D, cols), "attn_dw_in")
    d_in = jnp.transpose(d_in.reshape(D, N_DEV, D_IN // N_DEV), (1, 0, 2))
    grad_x, df1, hh1, dgu1, dg_f1_post, dg_f1_pre = _ffn_bwd(
        dh1, f1, small["ffn1_post_g"], xs, small["ffn1_pre_g"], gu1, w_gu1, w_down1, "ffn1_bwd")
    d_down1, (landed["w_in"],) = _dw_down(hh1, df1, "ffn1_dw_down", side=("exchange", [d_in]))
    d_down1 = d_down1.reshape(N_DEV, -1, D)
    d_gu1, (landed["ffn1_w_down"],) = _dw_gu(a1, dgu1, "ffn1_dw_gu", side=("exchange", [d_down1]))

    rb_a = _bias_grad(ds_a, buckets_a, "bias_grad_a")
    rb_b = _bias_grad(ds_b, buckets_b, "bias_grad_b").reshape(len(PATTERNS_B), N_HEAD_GROUP, NUM_BUCKETS)
    d_rel_bias = jnp.concatenate([rb_a.T, jnp.sum(rb_b, axis=0).T], axis=1)
    small_grads = {"ffn1_pre_g": dg_f1_pre, "ffn1_post_g": dg_f1_post, "attn_pre_g": dg_attn_pre,
                   "attn_post_g": dg_attn_post, "ffn2_pre_g": dg_f2_pre, "ffn2_post_g": dg_f2_post,
                   "ple_pre_g": dg_ple_pre, "ple_post_g": dg_ple_post, "b_out": db_out, "b_in": db_in,
                   "sinks": dsinks, "rel_bias": d_rel_bias}
    landed["ffn1_w_gu"], small_gathered = _final_exchange(d_gu1, small_grads, loss)

    result = {}
    for k in BIG:
        rows_total = ws[k].shape[1]
        outs = _sum_adamw(landed[k], shards[k], shard(ms[k]), shard(vs[k]), _adamw_rows(k, rows_total), k + "_adamw")
        result[k] = [o.reshape(ws[k].shape) for o in outs]
    small_res, loss_all = _small_adamw(
        small_gathered, small, {k: ms[k] for k in SMALL}, {k: vs[k] for k in SMALL})
    result.update(small_res)

    out = [loss_all[0, 0], grad_x.reshape(x.shape)]
    for i in range(4):
        out += [result[k][i] for k in WEIGHTS]
    return tuple(out)
```

```python
import functools
import math

import numpy as np
import jax
import jax.numpy as jnp
from jax import lax
from jax.experimental import pallas as pl
from jax.experimental.pallas import tpu as pltpu

F32 = jnp.float32
BF16 = jnp.bfloat16
MESH = pl.DeviceIdType.MESH

N_DEV = 8
EPS = 1e-6
NEG_INF = -1e30
HEAD_DIM = 64
LANES = 128
QBLK = 128
D_IN = 2304
A_Q, A_KV, B_W = 512, 128, 512
N_HEAD_GROUP = 8
NUM_BUCKETS = 32
MAX_DISTANCE = 2048
PATTERNS_A = ((1, 127),)
PATTERNS_B = ((1, 128), (4, 128), (16, 128))
Q_A_COL, K_A_COL, V_A_COL = 0, 4, 5
Q_B_COL, K_B_COL, V_B_COL = 6, 10, 14

ADAM_LR, ADAM_B1, ADAM_B2, ADAM_EPS, ADAM_WD, ADAM_STEP = 0.001, 0.9, 0.999, 1e-08, 0.01, 10

TOKEN_TILE = 512
FWD_BLOCKS = 4
BWD_BLOCKS = 2
VMEM_LIMIT = 56 * 1024 * 1024
ARB = "arbitrary"

BIG = ("ffn1_w_gu", "ffn1_w_down", "w_in", "w_out", "ffn2_w_gu", "ffn2_w_down", "w_ple_gate", "w_ple_proj")
GAINS = ("ffn1_pre_g", "ffn1_post_g", "attn_pre_g", "attn_post_g", "ffn2_pre_g", "ffn2_post_g",
         "ple_pre_g", "ple_post_g", "b_out")
SMALL = GAINS + ("b_in", "sinks", "rel_bias")
WEIGHTS = ("rel_bias", "ffn1_pre_g", "ffn1_w_gu", "ffn1_w_down", "ffn1_post_g", "attn_pre_g", "w_in", "b_in",
           "sinks", "w_out", "b_out", "attn_post_g", "ffn2_pre_g", "ffn2_w_gu", "ffn2_w_down", "ffn2_post_g",
           "ple_pre_g", "w_ple_gate", "w_ple_proj", "ple_post_g")


def _params(n_axes):
    return pltpu.CompilerParams(dimension_semantics=(ARB,) * n_axes, vmem_limit_bytes=VMEM_LIMIT)


def _mm(a, b):
    return jnp.dot(a, b, preferred_element_type=F32)


def _mm_nt(a, b):
    return lax.dot_general(a, b, (((1,), (1,)), ((), ())), preferred_element_type=F32)


def _mm_tn(a, b):
    return lax.dot_general(a, b, (((0,), (0,)), ((), ())), preferred_element_type=F32)


def _rstd(x):
    return lax.rsqrt(jnp.mean(x * x, axis=-1, keepdims=True) + EPS)


def _rms_bwd(x, r, gain, dy):
    n = x * r
    gdy = dy * gain
    return r * (gdy - n * jnp.mean(gdy * n, axis=-1, keepdims=True)), dy * n


def _colsum(v):
    return jnp.sum(v, axis=0, keepdims=True)


def _full(shape):
    return pl.BlockSpec(shape, lambda *_: (0,) * len(shape))


def _mesh_place():
    return lax.axis_index("x"), lax.axis_index("y"), lax.axis_index("c")


def _slot(dev):
    return 4 * dev[0] + 2 * dev[1] + dev[2]


def _peers(x, y, c):
    out = []
    for flip in range(1, N_DEV):
        dx, dy, dc = (flip >> 2) & 1, (flip >> 1) & 1, flip & 1
        out.append((1 - x if dx else x, 1 - y if dy else y, 1 - c if dc else c))
    return out


def _side_copies(kind, ins, outs, send_sems, recv_sems, local_sems, sem_row=0):
    n = len(ins)
    x, y, c = _mesh_place()
    me = _slot((x, y, c))
    peers = _peers(x, y, c)

    def src(a, block):
        return ins[a] if kind == "gather" else ins[a].at[block]

    def send(a, k, peer):
        return pltpu.make_async_remote_copy(
            src_ref=src(a, _slot(peer)), dst_ref=outs[a].at[me],
            send_sem=send_sems.at[sem_row + a, k], recv_sem=recv_sems.at[sem_row + a, k],
            device_id=peer, device_id_type=MESH)

    def arrival(a, k, peer):
        return pltpu.make_async_remote_copy(
            src_ref=src(a, _slot(peer)), dst_ref=outs[a].at[_slot(peer)],
            send_sem=send_sems.at[sem_row + a, k], recv_sem=recv_sems.at[sem_row + a, k],
            device_id=peer, device_id_type=MESH)

    def own(a):
        return pltpu.make_async_copy(src(a, me), outs[a].at[me], local_sems.at[sem_row + a])

    def start():
        for k, peer in enumerate(peers):
            for a in range(n):
                send(a, k, peer).start()
        for a in range(n):
            own(a).start()

    def wait():
        for k, peer in enumerate(peers):
            for a in range(n):
                arrival(a, k, peer).wait_recv()
        for k, peer in enumerate(peers):
            for a in range(n):
                send(a, k, peer).wait_send()
        for a in range(n):
            own(a).wait()

    return start, wait


def _side_out_shapes(kind, arrays):
    if kind == "gather":
        return [jax.ShapeDtypeStruct((N_DEV,) + a.shape, a.dtype) for a in arrays]
    return [jax.ShapeDtypeStruct(a.shape, a.dtype) for a in arrays]


def _hosted_call(body, name, grid, in_specs, out_specs, out_shape, scratch_shapes, args, side=None):
    if side is None:
        outs = pl.pallas_call(
            body, name=name, grid=grid, in_specs=in_specs, out_specs=out_specs, out_shape=out_shape,
            scratch_shapes=scratch_shapes, compiler_params=_params(len(grid)))(*args)
        return outs, []
    kind, arrays = side
    n_in, n_out, n_scr, n_side = len(in_specs), len(out_specs), len(scratch_shapes), len(arrays)

    def hosted(*refs):
        pos = 0
        groups = []
        for size in (n_in, n_side, n_out, n_side, n_scr):
            groups.append(refs[pos:pos + size])
            pos += size
        ins, side_in, outs, side_out, scr = groups
        send_sems, recv_sems, local_sems = refs[pos:]
        ids = [pl.program_id(d) for d in range(len(grid))]
        is_first = functools.reduce(jnp.logical_and, [i == 0 for i in ids])
        is_last = functools.reduce(jnp.logical_and, [i == g - 1 for i, g in zip(ids, grid)])
        start, wait = _side_copies(kind, side_in, side_out, send_sems, recv_sems, local_sems)
        pl.when(is_first)(start)
        body(*ins, *outs, *scr)
        pl.when(is_last)(wait)

    any_spec = pl.BlockSpec(memory_space=pl.ANY)
    outs = pl.pallas_call(
        hosted, name=name, grid=grid,
        in_specs=list(in_specs) + [any_spec] * n_side,
        out_specs=list(out_specs) + [any_spec] * n_side,
        out_shape=list(out_shape) + _side_out_shapes(kind, arrays),
        scratch_shapes=list(scratch_shapes) + [pltpu.SemaphoreType.DMA((n_side, 7)), pltpu.SemaphoreType.DMA((n_side, 7)),
                                               pltpu.SemaphoreType.DMA((n_side,))],
        compiler_params=_params(len(grid)))(*args, *arrays)
    return outs[:n_out], outs[n_out:]


def _ffn_fwd(h, g_pre, g_post, w_gu, w_down, name, side=None):
    T, D = h.shape
    nj = w_gu.shape[0] // 2
    FB = w_gu.shape[2]
    tm = TOKEN_TILE

    def body(h_ref, gpre_ref, gpost_ref, wg_ref, wu_ref, wd_ref, hout_ref, f_ref, a_ref, gu_ref, a_scr, acc):
        j = pl.program_id(1)

        @pl.when(j == 0)
        def _():
            x = h_ref[...]
            a = (x * _rstd(x) * gpre_ref[...]).astype(BF16)
            a_scr[...] = a
            a_ref[...] = a
            acc[...] = jnp.zeros_like(acc)

        a = a_scr[...]
        g = _mm(a, wg_ref[...])
        u = _mm(a, wu_ref[...])
        gu_ref[0] = g.astype(BF16)
        gu_ref[1] = u.astype(BF16)
        hh = (g * jax.nn.sigmoid(g) * u).astype(BF16)
        acc[...] += _mm(hh, wd_ref[...])

        @pl.when(j == nj - 1)
        def _():
            f = acc[...]
            f_ref[...] = f
            hout_ref[...] = h_ref[...] + 0.5 * (f * _rstd(f) * gpost_ref[...])

    return _hosted_call(
        body, name, (T // tm, nj),
        in_specs=[
            pl.BlockSpec((tm, D), lambda i, j: (i, 0)),
            _full((1, D)), _full((1, D)),
            pl.BlockSpec((None, D, FB), lambda i, j: (j, 0, 0)),
            pl.BlockSpec((None, D, FB), lambda i, j: (j + nj, 0, 0)),
            pl.BlockSpec((FB, D), lambda i, j: (j, 0)),
        ],
        out_specs=[
            pl.BlockSpec((tm, D), lambda i, j: (i, 0)),
            pl.BlockSpec((tm, D), lambda i, j: (i, 0)),
            pl.BlockSpec((tm, D), lambda i, j: (i, 0)),
            pl.BlockSpec((None, 2, tm, FB), lambda i, j: (j, 0, i, 0)),
        ],
        out_shape=[
            jax.ShapeDtypeStruct((T, D), F32),
            jax.ShapeDtypeStruct((T, D), F32),
            jax.ShapeDtypeStruct((T, D), BF16),
            jax.ShapeDtypeStruct((nj, 2, T, FB), BF16),
        ],
        scratch_shapes=[pltpu.VMEM((tm, D), BF16), pltpu.VMEM((tm, D), F32)],
        args=(h, g_pre, g_post, w_gu, w_gu, w_down), side=side)


def _ffn_bwd(dh_out, f, g_post, h, g_pre, gu, w_gu, w_down, name, side=None):
    T, D = h.shape
    nj = w_gu.shape[0] // 2
    FB = w_gu.shape[2]
    tm = TOKEN_TILE

    def body(dho_ref, f_ref, gpost_ref, h_ref, gpre_ref, gu_ref, wg_ref, wu_ref, wd_ref,
             dhin_ref, df_ref, hh_ref, dgu_ref, dgpost_ref, dgpre_ref, df_scr, da):
        i, j = pl.program_id(0), pl.program_id(1)

        @pl.when(jnp.logical_and(i == 0, j == 0))
        def _():
            dgpost_ref[...] = jnp.zeros_like(dgpost_ref)
            dgpre_ref[...] = jnp.zeros_like(dgpre_ref)

        @pl.when(j == 0)
        def _():
            fv = f_ref[...]
            df, dgain = _rms_bwd(fv, _rstd(fv), gpost_ref[...], 0.5 * dho_ref[...])
            dgpost_ref[...] += _colsum(dgain)
            dfb = df.astype(BF16)
            df_scr[...] = dfb
            df_ref[...] = dfb
            da[...] = jnp.zeros_like(da)

        dhh = _mm_nt(df_scr[...], wd_ref[...])
        g = gu_ref[0].astype(F32)
        u = gu_ref[1].astype(F32)
        sg = jax.nn.sigmoid(g)
        silu = g * sg
        hh_ref[...] = (silu * u).astype(BF16)
        dg = (dhh * u * (sg * (1.0 + g * (1.0 - sg)))).astype(BF16)
        du = (dhh * silu).astype(BF16)
        dgu_ref[0] = dg
        dgu_ref[1] = du
        da[...] += _mm_nt(dg, wg_ref[...]) + _mm_nt(du, wu_ref[...])

        @pl.when(j == nj - 1)
        def _():
            x = h_ref[...]
            dx, dgain = _rms_bwd(x, _rstd(x), gpre_ref[...], da[...])
            dgpre_ref[...] += _colsum(dgain)
            dhin_ref[...] = dho_ref[...] + dx

    tile = pl.BlockSpec((tm, D), lambda i, j: (i, 0))
    return _hosted_call(
        body, name, (T // tm, nj),
        in_specs=[
            tile, tile, _full((1, D)), tile, _full((1, D)),
            pl.BlockSpec((None, 2, tm, FB), lambda i, j: (j, 0, i, 0)),
            pl.BlockSpec((None, D, FB), lambda i, j: (j, 0, 0)),
            pl.BlockSpec((None, D, FB), lambda i, j: (j + nj, 0, 0)),
            pl.BlockSpec((FB, D), lambda i, j: (j, 0)),
        ],
        out_specs=[
            tile, tile,
            pl.BlockSpec((None, tm, FB), lambda i, j: (j, i, 0)),
            pl.BlockSpec((None, 2, tm, FB), lambda i, j: (j, 0, i, 0)),
            _full((1, D)), _full((1, D)),
        ],
        out_shape=[
            jax.ShapeDtypeStruct((T, D), F32),
            jax.ShapeDtypeStruct((T, D), BF16),
            jax.ShapeDtypeStruct((nj, T, FB), BF16),
            jax.ShapeDtypeStruct((nj, 2, T, FB), BF16),
            jax.ShapeDtypeStruct((1, D), F32),
            jax.ShapeDtypeStruct((1, D), F32),
        ],
        scratch_shapes=[pltpu.VMEM((tm, D), BF16), pltpu.VMEM((tm, D), F32)],
        args=(dh_out, f, g_post, h, g_pre, gu, w_gu, w_gu, w_down), side=side)


def _tn_matmul(x, y, x_spec, y_spec, out_shape, out_spec, n_blocks, n_steps, acc_shape, name, side=None):
    def body(x_ref, y_ref, o_ref, acc):
        t = pl.program_id(1)

        @pl.when(t == 0)
        def _():
            acc[...] = jnp.zeros_like(acc)

        acc[...] += _mm_tn(x_ref[...].astype(BF16), y_ref[...].astype(BF16))

        @pl.when(t == n_steps - 1)
        def _():
            o_ref[...] = acc[...].astype(o_ref.dtype)

    outs, side_outs = _hosted_call(
        body, name, (n_blocks, n_steps), in_specs=[x_spec, y_spec], out_specs=[out_spec], out_shape=[out_shape],
        scratch_shapes=[pltpu.VMEM(acc_shape, F32)], args=(x, y), side=side)
    return (outs[0], side_outs) if side is not None else outs[0]


def _inproj_fwd(h, g_pre, w_in, b_in, side=None):
    T, D = h.shape
    tm = TOKEN_TILE

    def body(h_ref, g_ref, w_ref, b_ref, z_ref, a_ref):
        x = h_ref[...]
        a = (x * _rstd(x) * g_ref[...]).astype(BF16)
        a_ref[...] = a
        z_ref[...] = _mm(a, w_ref[...]) + b_ref[...]

    return _hosted_call(
        body, "inproj_fwd", (T // tm,),
        in_specs=[pl.BlockSpec((tm, D), lambda i: (i, 0)), _full((1, D)), _full((D, D_IN)), _full((1, D_IN))],
        out_specs=[pl.BlockSpec((tm, D_IN), lambda i: (i, 0)), pl.BlockSpec((tm, D), lambda i: (i, 0))],
        out_shape=[jax.ShapeDtypeStruct((T, D_IN), F32), jax.ShapeDtypeStruct((T, D), BF16)],
        scratch_shapes=[], args=(h, g_pre, w_in, b_in), side=side)


def _inproj_bwd(dqa, dka, dva, dqb, dkb, dvb, w_in, h, g_pre, dres, side=None):
    T, D = h.shape
    tm = TOKEN_TILE

    def body(dqa_ref, dka_ref, dva_ref, dqb_ref, dkb_ref, dvb_ref, w_ref, h_ref, g_ref, dres_ref,
             dh_ref, dz_ref, dbin_ref, dg_ref):
        i = pl.program_id(0)

        @pl.when(i == 0)
        def _():
            dbin_ref[...] = jnp.zeros_like(dbin_ref)
            dg_ref[...] = jnp.zeros_like(dg_ref)

        dz = jnp.concatenate([dqa_ref[...], dka_ref[...], dva_ref[...], dqb_ref[...], dkb_ref[...], dvb_ref[...]],
                             axis=1)
        dbin_ref[...] += _colsum(dz)
        dzb = dz.astype(BF16)
        dz_ref[...] = dzb
        da = _mm_nt(dzb, w_ref[...])
        x = h_ref[...]
        dx, dgain = _rms_bwd(x, _rstd(x), g_ref[...], da)
        dg_ref[...] += _colsum(dgain)
        dh_ref[...] = dres_ref[...] + dx

    def tile(w):
        return pl.BlockSpec((tm, w), lambda i: (i, 0))

    return _hosted_call(
        body, "inproj_bwd", (T // tm,),
        in_specs=[tile(A_Q), tile(A_KV), tile(A_KV), tile(B_W), tile(B_W), tile(B_W),
                  _full((D, D_IN)), tile(D), _full((1, D)), tile(D)],
        out_specs=[tile(D), tile(D_IN), _full((1, D_IN)), _full((1, D))],
        out_shape=[jax.ShapeDtypeStruct((T, D), F32), jax.ShapeDtypeStruct((T, D_IN), BF16),
                   jax.ShapeDtypeStruct((1, D_IN), F32), jax.ShapeDtypeStruct((1, D), F32)],
        scratch_shapes=[], args=(dqa, dka, dva, dqb, dkb, dvb, w_in, h, g_pre, dres), side=side)


def _bucket_tiles(patterns):
    i = np.arange(QBLK)[:, None]
    j = np.arange(2 * QBLK)[None, :]
    dist = QBLK + i - j
    max_exact = NUM_BUCKETS // 2
    tiles = []
    for dilation, max_dist in patterns:
        n = np.maximum(dist * dilation, 0)
        nf = np.maximum(n, 1).astype(np.float32)
        large = max_exact + (np.log(nf / np.float32(max_exact)) / np.float32(math.log(MAX_DISTANCE / max_exact))
                             * np.float32(NUM_BUCKETS - max_exact)).astype(np.int32)
        bucket = np.where(n < max_exact, n, np.minimum(large, NUM_BUCKETS - 1))
        tiles.append(np.where((dist >= 0) & (dist <= max_dist), bucket, -1))
    return jnp.asarray(np.stack(tiles).astype(np.int32))


def _bias_build(rel_bias, buckets, head0, name):
    n = buckets.shape[0]

    def body(bk_ref, rb_ref, o_ref):
        bk = bk_ref[...]
        base = jnp.where(bk < 0, NEG_INF, 0.0).astype(F32)
        for hd in range(N_HEAD_GROUP):
            o_ref[hd] = lax.fori_loop(
                0, NUM_BUCKETS, lambda b, acc, hd=hd: jnp.where(bk == b, rb_ref[b, head0 + hd], acc), base)

    return pl.pallas_call(
        body, name=name, grid=(n,),
        in_specs=[pl.BlockSpec((None, QBLK, 2 * QBLK), lambda p: (p, 0, 0)), pl.BlockSpec(memory_space=pltpu.SMEM)],
        out_specs=pl.BlockSpec((None, N_HEAD_GROUP, QBLK, 2 * QBLK), lambda p: (p, 0, 0, 0)),
        out_shape=jax.ShapeDtypeStruct((n, N_HEAD_GROUP, QBLK, 2 * QBLK), F32),
        compiler_params=_params(1),
    )(buckets, rel_bias)


def _bias_grad(ds, buckets, name):
    n = buckets.shape[0]

    def body(ds_ref, bk_ref, o_ref):
        bk = bk_ref[...]
        row = lax.broadcasted_iota(jnp.int32, (NUM_BUCKETS, 2 * QBLK), 0)
        for hd in range(N_HEAD_GROUP):
            d = ds_ref[hd]
            per_key = jnp.zeros((NUM_BUCKETS, 2 * QBLK), F32)
            for b in range(NUM_BUCKETS):
                per_key = jnp.where(row == b, jnp.sum(jnp.where(bk == b, d, 0.0), axis=0, keepdims=True), per_key)
            o_ref[hd] = jnp.broadcast_to(jnp.sum(per_key, axis=1, keepdims=True), (NUM_BUCKETS, LANES))

    out = pl.pallas_call(
        body, name=name, grid=(n,),
        in_specs=[pl.BlockSpec((None, N_HEAD_GROUP, QBLK, 2 * QBLK), lambda p: (p, 0, 0, 0)),
                  pl.BlockSpec((None, QBLK, 2 * QBLK), lambda p: (p, 0, 0))],
        out_specs=pl.BlockSpec((None, N_HEAD_GROUP, NUM_BUCKETS, LANES), lambda p: (p, 0, 0, 0)),
        out_shape=jax.ShapeDtypeStruct((n, N_HEAD_GROUP, NUM_BUCKETS, LANES), F32),
        compiler_params=_params(1),
    )(ds, buckets)
    return out[:, :, :, 0].reshape(n * N_HEAD_GROUP, NUM_BUCKETS)


def _class_rows(start, dilation):
    if dilation == 1:
        return pl.ds(pl.multiple_of(start, QBLK), QBLK)
    return pl.ds(start, QBLK, stride=dilation)


def _block_starts(idx, n_blocks, dilation):
    cls = idx // n_blocks
    n = idx % n_blocks
    cur = cls + dilation * QBLK * n
    prev = cls + dilation * QBLK * jnp.maximum(n - 1, 0)
    return n, cur, prev


class _HeadPair:
    def __init__(self, g, shared_kv):
        self.lane = lax.broadcasted_iota(jnp.int32, (1, LANES), 1)
        self.lower = self.lane < HEAD_DIM
        self.shared_kv = shared_kv
        self.key_lanes = (self.lane >= HEAD_DIM).astype(jnp.int32) == (g // 2)

    def stack(self, t):
        return jnp.concatenate([jnp.where(self.lower, t, 0.0), jnp.where(self.lower, 0.0, t)], axis=0).astype(BF16)

    def unstack(self, t2):
        return jnp.where(self.lower, t2[:QBLK], t2[QBLK:])

    def keys(self, t):
        if self.shared_kv:
            return jnp.where(self.key_lanes, t, pltpu.roll(t, HEAD_DIM, 1))
        return t

    def key_grads(self, t):
        if self.shared_kv:
            return jnp.where(self.key_lanes, t + pltpu.roll(t, HEAD_DIM, 1), 0.0)
        return t


def _attn_specs(T, qcol, kcol, vcol, shared_kv):
    kv = (lambda c: (lambda g: (0, c))) if shared_kv else (lambda c: (lambda g: (0, c + g)))
    return [pl.BlockSpec((T, LANES), lambda g: (0, qcol + g)),
            pl.BlockSpec((T, LANES), kv(kcol)),
            pl.BlockSpec((T, LANES), kv(vcol))]


def _attn_fwd(z, bias, sinks, patterns, qcol, kcol, vcol, shared_kv, name, side=None):
    T = z.shape[0]
    n_pat = len(patterns)
    has_sink = sinks is not None

    def body(*refs):
        if has_sink:
            sink_ref, refs = refs[0], refs[1:]
        q_ref, k_ref, v_ref, b_ref, o_ref, l_ref = refs[:6]
        po_scr = refs[6:6 + n_pat]
        pl_scr = refs[6 + n_pat:]
        g = pl.program_id(0)
        heads = _HeadPair(g, shared_kv)
        in_prev = lax.broadcasted_iota(jnp.int32, (2 * QBLK, 2 * QBLK), 1) < QBLK

        for pi, (dilation, _) in enumerate(patterns):
            n_blocks = T // (QBLK * dilation)

            def step(it, carry, pi=pi, dilation=dilation, n_blocks=n_blocks):
                blocks = []
                for u in range(FWD_BLOCKS):
                    n, cur, prev = _block_starts(it * FWD_BLOCKS + u, n_blocks, dilation)
                    rows_c, rows_p = _class_rows(cur, dilation), _class_rows(prev, dilation)
                    qm = heads.stack(q_ref[rows_c, :])
                    k2 = heads.keys(jnp.concatenate([k_ref[rows_p, :], k_ref[rows_c, :]], axis=0)).astype(BF16)
                    v2 = heads.keys(jnp.concatenate([v_ref[rows_p, :], v_ref[rows_c, :]], axis=0)).astype(BF16)
                    blocks.append(dict(n=n, rows=rows_c, v2=v2, s=_mm_nt(qm, k2)))
                for b in blocks:
                    s = b["s"] * (HEAD_DIM ** -0.5) + b_ref[pi]
                    b["s"] = jnp.where(jnp.logical_and(in_prev, b["n"] == 0), NEG_INF, s)
                    b["m"] = jnp.max(b["s"], axis=1, keepdims=True)
                for b in blocks:
                    b["pr"] = jnp.exp(b["s"] - b["m"])
                    b["den"] = jnp.sum(b["pr"], axis=1, keepdims=True)
                for b in blocks:
                    b["o2"] = _mm(b["pr"].astype(BF16), b["v2"])
                for b in blocks:
                    lse = b["m"] + jnp.log(b["den"])
                    po_scr[pi][b["rows"], :] = heads.unstack(b["o2"] / b["den"])
                    pl_scr[2 * pi][b["rows"], :] = jnp.broadcast_to(lse[:QBLK], (QBLK, LANES))
                    pl_scr[2 * pi + 1][b["rows"], :] = jnp.broadcast_to(lse[QBLK:], (QBLK, LANES))
                return carry

            lax.fori_loop(0, (dilation * n_blocks) // FWD_BLOCKS, step, 0)

        def merge(ci, carry):
            rows = pl.ds(pl.multiple_of(ci * QBLK, QBLK), QBLK)
            weights = []
            for hd in range(2):
                parts = [pl_scr[2 * pi + hd][rows, :] for pi in range(n_pat)]
                m = functools.reduce(jnp.maximum, parts)
                if has_sink:
                    sink = sink_ref[0, 2 * g + hd]
                    m = jnp.maximum(m, sink)
                den = functools.reduce(jnp.add, [jnp.exp(x - m) for x in parts])
                if has_sink:
                    den = den + jnp.exp(sink - m)
                lse = m + jnp.log(den)
                l_ref[hd, rows, :] = lse
                weights.append([jnp.exp(x - lse) for x in parts])
            o_ref[rows, :] = functools.reduce(
                jnp.add, [jnp.where(heads.lower, weights[0][pi], weights[1][pi]) * po_scr[pi][rows, :]
                          for pi in range(n_pat)])
            return carry

        lax.fori_loop(0, T // QBLK, merge, 0)

    in_specs = _attn_specs(T, qcol, kcol, vcol, shared_kv)
    in_specs.append(pl.BlockSpec((n_pat, None, 2 * QBLK, 2 * QBLK), lambda g: (0, g, 0, 0)))
    args = [z, z, z, bias.reshape(n_pat, N_HEAD_GROUP // 2, 2 * QBLK, 2 * QBLK)]
    if has_sink:
        in_specs.insert(0, pl.BlockSpec(memory_space=pltpu.SMEM))
        args.insert(0, sinks)
    return _hosted_call(
        body, name, (N_HEAD_GROUP // 2,),
        in_specs=in_specs,
        out_specs=[pl.BlockSpec((T, LANES), lambda g: (0, g)), pl.BlockSpec((2, T, LANES), lambda g: (g, 0, 0))],
        out_shape=[jax.ShapeDtypeStruct((T, N_HEAD_GROUP * HEAD_DIM), F32),
                   jax.ShapeDtypeStruct((N_HEAD_GROUP, T, LANES), F32)],
        scratch_shapes=[pltpu.VMEM((T, LANES), F32)] * (3 * n_pat), args=args, side=side)


def _attn_bwd(z, bias, sinks, d_out, out, lse, patterns, qcol, kcol, vcol, shared_kv, name, side=None):
    T = z.shape[0]
    n_pat = len(patterns)
    has_sink = sinks is not None
    kv_width = LANES if shared_kv else N_HEAD_GROUP * HEAD_DIM

    def body(*refs):
        if has_sink:
            sink_ref, refs = refs[0], refs[1:]
        q_ref, k_ref, v_ref, b_ref, do_ref, o_ref, l0_ref, l1_ref = refs[:8]
        dq_ref, dk_ref, dv_ref, ds_ref = refs[8:12]
        dsink_ref = refs[12] if has_sink else None
        dk_acc, dv_acc = refs[-2:]
        g = pl.program_id(0)
        heads = _HeadPair(g, shared_kv)
        in_prev = lax.broadcasted_iota(jnp.int32, (2 * QBLK, 2 * QBLK), 1) < QBLK

        dq_ref[...] = jnp.zeros_like(dq_ref)
        ds_ref[...] = jnp.zeros_like(ds_ref)
        dk_acc[...] = jnp.zeros_like(dk_acc)
        dv_acc[...] = jnp.zeros_like(dv_acc)

        dsink = jnp.zeros((1, LANES), F32)
        for pi, (dilation, _) in enumerate(patterns):
            n_blocks = T // (QBLK * dilation)

            def step(idx, dsink, pi=pi, dilation=dilation, n_blocks=n_blocks):
                blocks = []
                for u in range(BWD_BLOCKS):
                    n, cur, prev = _block_starts(idx * BWD_BLOCKS + u, n_blocks, dilation)
                    rows_c, rows_p = _class_rows(cur, dilation), _class_rows(prev, dilation)
                    qm = heads.stack(q_ref[rows_c, :])
                    k2 = heads.keys(jnp.concatenate([k_ref[rows_p, :], k_ref[rows_c, :]], axis=0)).astype(BF16)
                    v2 = heads.keys(jnp.concatenate([v_ref[rows_p, :], v_ref[rows_c, :]], axis=0)).astype(BF16)
                    d_o = do_ref[rows_c, :]
                    dom = heads.stack(d_o)
                    dd = d_o * o_ref[rows_c, :]
                    delta = jnp.concatenate([jnp.sum(jnp.where(heads.lower, dd, 0.0), axis=1, keepdims=True),
                                             jnp.sum(jnp.where(heads.lower, 0.0, dd), axis=1, keepdims=True)], axis=0)
                    lse = jnp.concatenate([l0_ref[rows_c, :], l1_ref[rows_c, :]], axis=0)
                    blocks.append(dict(n=n, rows_c=rows_c, rows_p=rows_p, qm=qm, k2=k2, dom=dom, delta=delta, lse=lse,
                                       s=_mm_nt(qm, k2), dp=_mm_nt(dom, v2)))
                for b in blocks:
                    s = b["s"] * (HEAD_DIM ** -0.5) + b_ref[pi]
                    s = jnp.where(jnp.logical_and(in_prev, b["n"] == 0), NEG_INF, s)
                    b["pr"] = jnp.exp(s - jnp.concatenate([b["lse"], b["lse"]], axis=1))
                    b["ds"] = b["pr"] * (b["dp"] - b["delta"])
                for b in blocks:
                    dsb = b["ds"].astype(BF16)
                    b["dq2"] = _mm(dsb, b["k2"])
                    b["dk2"] = _mm_tn(dsb, b["qm"])
                    b["dv2"] = _mm_tn(b["pr"].astype(BF16), b["dom"])
                for b in blocks:
                    ds_ref[pi] += b["ds"]
                    dq_ref[b["rows_c"], :] += heads.unstack(b["dq2"]) * (HEAD_DIM ** -0.5)
                    dk2 = heads.key_grads(b["dk2"]) * (HEAD_DIM ** -0.5)
                    dv2 = heads.key_grads(b["dv2"])
                    dk_acc[b["rows_p"], :] += dk2[:QBLK]
                    dk_acc[b["rows_c"], :] += dk2[QBLK:]
                    dv_acc[b["rows_p"], :] += dv2[:QBLK]
                    dv_acc[b["rows_c"], :] += dv2[QBLK:]
                    if has_sink:
                        for hd in range(2):
                            rows_h = slice(QBLK * hd, QBLK * (hd + 1))
                            p_sink = jnp.exp(sink_ref[0, 2 * g + hd] - b["lse"][rows_h, 0:1])
                            dsink = dsink - jnp.where(heads.lane == 2 * g + hd,
                                                      jnp.sum(p_sink * b["delta"][rows_h]), 0.0)
                return dsink

            dsink = lax.fori_loop(0, (dilation * n_blocks) // BWD_BLOCKS, step, dsink)

        if shared_kv:
            @pl.when(g == 0)
            def _():
                dk_ref[...] = dk_acc[...]
                dv_ref[...] = dv_acc[...]

            @pl.when(g != 0)
            def _():
                dk_ref[...] += dk_acc[...]
                dv_ref[...] += dv_acc[...]
        else:
            dk_ref[...] = dk_acc[...]
            dv_ref[...] = dv_acc[...]

        if has_sink:
            @pl.when(g == 0)
            def _():
                dsink_ref[...] = dsink

            @pl.when(g != 0)
            def _():
                dsink_ref[...] += dsink

    pair = pl.BlockSpec((T, LANES), lambda g: (0, g))
    stacked = pl.BlockSpec((n_pat, None, 2 * QBLK, 2 * QBLK), lambda g: (0, g, 0, 0))
    stacked_shape = (n_pat, N_HEAD_GROUP // 2, 2 * QBLK, 2 * QBLK)
    in_specs = _attn_specs(T, qcol, kcol, vcol, shared_kv)
    in_specs += [stacked, pair, pair,
                 pl.BlockSpec((None, T, LANES), lambda g: (2 * g, 0, 0)),
                 pl.BlockSpec((None, T, LANES), lambda g: (2 * g + 1, 0, 0))]
    args = [z, z, z, bias.reshape(stacked_shape), d_out, out, lse, lse]
    kv_out = _full((T, LANES)) if shared_kv else pair
    out_specs = [pair, kv_out, kv_out, stacked]
    out_shape = [jax.ShapeDtypeStruct((T, N_HEAD_GROUP * HEAD_DIM), F32),
                 jax.ShapeDtypeStruct((T, kv_width), F32), jax.ShapeDtypeStruct((T, kv_width), F32),
                 jax.ShapeDtypeStruct(stacked_shape, F32)]
    if has_sink:
        in_specs.insert(0, pl.BlockSpec(memory_space=pltpu.SMEM))
        args.insert(0, sinks)
        out_specs.append(_full((1, LANES)))
        out_shape.append(jax.ShapeDtypeStruct((1, LANES), F32))
    outs, side_outs = _hosted_call(
        body, name, (N_HEAD_GROUP // 2,), in_specs=in_specs, out_specs=out_specs, out_shape=out_shape,
        scratch_shapes=[pltpu.VMEM((T, LANES), F32), pltpu.VMEM((T, LANES), F32)], args=args, side=side)
    outs = list(outs)
    outs[3] = outs[3].reshape(n_pat, N_HEAD_GROUP, QBLK, 2 * QBLK)
    return outs, side_outs


def _outproj_fwd(mix_a, mix_b, w_out, b_out, g_post, h):
    T, D = h.shape
    tm = TOKEN_TILE
    d_mix = w_out.shape[0]

    def body(ma_ref, mb_ref, w_ref, b_ref, g_ref, h_ref, att_ref, hout_ref, mix_ref):
        mix = jnp.concatenate([ma_ref[...], mb_ref[...]], axis=1).astype(BF16)
        mix_ref[...] = mix
        att = _mm(mix, w_ref[...]) + b_ref[...]
        att_ref[...] = att
        hout_ref[...] = h_ref[...] + att * _rstd(att) * g_ref[...]

    def tile(w):
        return pl.BlockSpec((tm, w), lambda i: (i, 0))

    return pl.pallas_call(
        body, name="outproj_fwd", grid=(T // tm,),
        in_specs=[tile(A_Q), tile(B_W), _full((d_mix, D)), _full((1, D)), _full((1, D)), tile(D)],
        out_specs=[tile(D), tile(D), tile(d_mix)],
        out_shape=[jax.ShapeDtypeStruct((T, D), F32), jax.ShapeDtypeStruct((T, D), F32),
                   jax.ShapeDtypeStruct((T, d_mix), BF16)],
        compiler_params=_params(1),
    )(mix_a, mix_b, w_out, b_out, g_post, h)


def _outproj_bwd(dh, att, g_post, w_out):
    T, D = dh.shape
    tm = TOKEN_TILE
    d_mix = w_out.shape[0]

    def body(dh_ref, att_ref, g_ref, w_ref, dma_ref, dmb_ref, datt_ref, dg_ref, db_ref):
        i = pl.program_id(0)

        @pl.when(i == 0)
        def _():
            dg_ref[...] = jnp.zeros_like(dg_ref)
            db_ref[...] = jnp.zeros_like(db_ref)

        att = att_ref[...]
        datt, dgain = _rms_bwd(att, _rstd(att), g_ref[...], dh_ref[...])
        dg_ref[...] += _colsum(dgain)
        db_ref[...] += _colsum(datt)
        dattb = datt.astype(BF16)
        datt_ref[...] = dattb
        dmix = _mm_nt(dattb, w_ref[...])
        dma_ref[...] = dmix[:, :A_Q]
        dmb_ref[...] = dmix[:, A_Q:]

    def tile(w):
        return pl.BlockSpec((tm, w), lambda i: (i, 0))

    return pl.pallas_call(
        body, name="outproj_bwd", grid=(T // tm,),
        in_specs=[tile(D), tile(D), _full((1, D)), _full((d_mix, D))],
        out_specs=[tile(A_Q), tile(B_W), tile(D), _full((1, D)), _full((1, D))],
        out_shape=[jax.ShapeDtypeStruct((T, A_Q), F32), jax.ShapeDtypeStruct((T, B_W), F32),
                   jax.ShapeDtypeStruct((T, D), BF16), jax.ShapeDtypeStruct((1, D), F32),
                   jax.ShapeDtypeStruct((1, D), F32)],
        compiler_params=_params(1),
    )(dh, att, g_post, w_out)


def _ple_fwd_loss(h, g_pre, w_gate, p, w_proj, g_post, target):
    T, D = h.shape
    tm = TOKEN_TILE
    n_proj, ple, db = w_proj.shape

    def body(h_ref, gpre_ref, wg_ref, p_ref, wp_ref, gpost_ref, t_ref,
             a_ref, dpre_ref, de_ref, dh_ref, loss_ref, dgpost_ref):
        i = pl.program_id(0)

        @pl.when(i == 0)
        def _():
            loss_ref[...] = jnp.zeros_like(loss_ref)
            dgpost_ref[...] = jnp.zeros_like(dgpost_ref)

        x = h_ref[...]
        a = (x * _rstd(x) * gpre_ref[...]).astype(BF16)
        a_ref[...] = a
        gate = jax.nn.sigmoid(_mm(a, wg_ref[...]))
        pb = p_ref[...].astype(BF16)
        e = jnp.concatenate([_mm(pb, wp_ref[k]) for k in range(n_proj)], axis=1)
        ge = gate * e
        rg = _rstd(ge)
        diff = x + ge * rg * gpost_ref[...] - t_ref[...]
        loss_ref[...] += 0.5 * jnp.sum(jnp.mean(diff * diff, axis=1, keepdims=True))
        dy = diff * (1.0 / D)
        dh_ref[...] = dy
        dge, dgain = _rms_bwd(ge, rg, gpost_ref[...], dy)
        dgpost_ref[...] += _colsum(dgain)
        de_ref[...] = (dge * gate).astype(BF16)
        dpre_ref[...] = (dge * e * gate * (1.0 - gate)).astype(BF16)

    def tile(w):
        return pl.BlockSpec((tm, w), lambda i: (i, 0))

    return pl.pallas_call(
        body, name="ple_fwd_loss", grid=(T // tm,),
        in_specs=[tile(D), _full((1, D)), _full((D, D)), tile(ple), _full((n_proj, ple, db)), _full((1, D)), tile(D)],
        out_specs=[tile(D), tile(D), tile(D), tile(D), _full((1, LANES)), _full((1, D))],
        out_shape=[jax.ShapeDtypeStruct((T, D), BF16),
                   jax.ShapeDtypeStruct((T, D), BF16),
                   jax.ShapeDtypeStruct((T, D), BF16),
                   jax.ShapeDtypeStruct((T, D), F32),
                   jax.ShapeDtypeStruct((1, LANES), F32),
                   jax.ShapeDtypeStruct((1, D), F32)],
        compiler_params=_params(1),
    )(h, g_pre, w_gate, p, w_proj, g_post, target)


def _ple_bwd(dpre, w_gate, h, g_pre, dres):
    T, D = h.shape
    tm = TOKEN_TILE

    def body(dpre_ref, w_ref, h_ref, g_ref, dres_ref, dh_ref, dg_ref):
        i = pl.program_id(0)

        @pl.when(i == 0)
        def _():
            dg_ref[...] = jnp.zeros_like(dg_ref)

        da = _mm_nt(dpre_ref[...], w_ref[...])
        x = h_ref[...]
        dx, dgain = _rms_bwd(x, _rstd(x), g_ref[...], da)
        dg_ref[...] += _colsum(dgain)
        dh_ref[...] = dres_ref[...] + dx

    tile = pl.BlockSpec((tm, D), lambda i: (i, 0))
    return pl.pallas_call(
        body, name="ple_bwd", grid=(T // tm,),
        in_specs=[tile, _full((D, D)), tile, _full((1, D)), tile],
        out_specs=[tile, _full((1, D))],
        out_shape=[jax.ShapeDtypeStruct((T, D), F32), jax.ShapeDtypeStruct((1, D), F32)],
        compiler_params=_params(1),
    )(dpre, w_gate, h, g_pre, dres)


def _ple_dw_proj(p, de, n_proj):
    T, ple = p.shape
    D = de.shape[1]
    db = D // n_proj
    tk = TOKEN_TILE
    nt = T // tk

    def body(p_ref, de_ref, o_ref, acc):
        t = pl.program_id(0)

        @pl.when(t == 0)
        def _():
            acc[...] = jnp.zeros_like(acc)

        acc[...] += _mm_tn(p_ref[...].astype(BF16), de_ref[...])

        @pl.when(t == nt - 1)
        def _():
            for k in range(n_proj):
                o_ref[k] = acc[:, k * db:(k + 1) * db].astype(BF16)

    return pl.pallas_call(
        body, name="ple_dw_proj", grid=(nt,),
        in_specs=[pl.BlockSpec((tk, ple), lambda t: (t, 0)), pl.BlockSpec((tk, D), lambda t: (t, 0))],
        out_specs=_full((n_proj, ple, db)), out_shape=jax.ShapeDtypeStruct((n_proj, ple, db), BF16),
        scratch_shapes=[pltpu.VMEM((ple, D), F32)], compiler_params=_params(1),
    )(p, de)


def _tok(width):
    return pl.BlockSpec((TOKEN_TILE, width), lambda b, t: (t, 0))


def _dw_gu(a, dgu, name, side=None):
    T, D = a.shape
    nj, _, _, FB = dgu.shape
    return _tn_matmul(
        a, dgu, _tok(D), pl.BlockSpec((None, None, TOKEN_TILE, FB), lambda b, t: (b % nj, b // nj, t, 0)),
        jax.ShapeDtypeStruct((2 * nj, D, FB), BF16), pl.BlockSpec((None, D, FB), lambda b, t: (b, 0, 0)),
        2 * nj, T // TOKEN_TILE, (D, FB), name, side=side)


def _dw_down(hh, df, name, side=None):
    nj, T, FB = hh.shape
    D = df.shape[1]
    return _tn_matmul(
        hh, df, pl.BlockSpec((None, TOKEN_TILE, FB), lambda b, t: (b, t, 0)), _tok(D),
        jax.ShapeDtypeStruct((nj, FB, D), BF16), pl.BlockSpec((None, FB, D), lambda b, t: (b, 0, 0)),
        nj, T // TOKEN_TILE, (FB, D), name, side=side)


def _dw_rows(xm, y, name, rows):
    T, k = xm.shape
    D = y.shape[1]
    out = _tn_matmul(
        xm, y, pl.BlockSpec((TOKEN_TILE, rows), lambda b, t: (t, b)), _tok(D),
        jax.ShapeDtypeStruct((k, D), BF16), pl.BlockSpec((rows, D), lambda b, t: (b, 0)),
        k // rows, T // TOKEN_TILE, (rows, D), name)
    return out.reshape(N_DEV, k // N_DEV, D)


def _cast_bf16(arrays):
    n = len(arrays)

    def body(*refs):
        for a in range(n):
            refs[n + a][...] = refs[a][...].astype(BF16)

    return pl.pallas_call(
        body, name="cast_shards",
        in_specs=[pl.BlockSpec(memory_space=pltpu.VMEM)] * n, out_specs=[pl.BlockSpec(memory_space=pltpu.VMEM)] * n,
        out_shape=[jax.ShapeDtypeStruct(a.shape, BF16) for a in arrays],
        compiler_params=pltpu.CompilerParams(vmem_limit_bytes=VMEM_LIMIT),
    )(*arrays)


def _all_gather_bf16(shards):
    n = len(shards)

    def body(*refs):
        ins, outs, scr = refs[:n], refs[n:2 * n], refs[2 * n:3 * n]
        send_sems, recv_sems, local_sems = refs[3 * n:]
        x, y, c = _mesh_place()
        me, sibling = (x, y, c), (x, y, 1 - c)
        chips = [(1 - x, y), (x, 1 - y), (1 - x, 1 - y)]
        for a in range(n):
            scr[a][...] = ins[a][...].astype(BF16)

        def copy(a, k, block, to, src=None):
            dst = outs[a].at[_slot(block)]
            return pltpu.make_async_remote_copy(
                src_ref=dst if src is None else src, dst_ref=dst,
                send_sem=send_sems.at[a, k], recv_sem=recv_sems.at[a, k], device_id=to, device_id_type=MESH)

        mine = [pltpu.make_async_copy(scr[a], outs[a].at[_slot(me)], local_sems.at[a]) for a in range(n)]
        first = [copy(a, 1 + j, me, (*chip, c), src=scr[a]) for j, chip in enumerate(chips) for a in range(n)]
        first += [copy(a, 0, me, sibling, src=scr[a]) for a in range(n)]
        for cp in first + mine:
            cp.start()
        passed = []
        for j, chip in enumerate(chips):
            for a in range(n):
                copy(a, 1 + j, (*chip, c), me).wait_recv()
                cp = copy(a, 4 + j, (*chip, c), sibling)
                cp.start()
                passed.append(cp)
        for a in range(n):
            copy(a, 0, sibling, me).wait_recv()
        for j, chip in enumerate(chips):
            for a in range(n):
                copy(a, 4 + j, (*chip, 1 - c), me).wait_recv()
        for cp in first + passed:
            cp.wait_send()
        for cp in mine:
            cp.wait()

    return pl.pallas_call(
        body, name="weights_all_gather",
        in_specs=[pl.BlockSpec(memory_space=pltpu.VMEM)] * n,
        out_specs=[pl.BlockSpec(memory_space=pl.ANY)] * n,
        out_shape=[jax.ShapeDtypeStruct((N_DEV,) + s.shape, BF16) for s in shards],
        scratch_shapes=[pltpu.VMEM(s.shape, BF16) for s in shards]
        + [pltpu.SemaphoreType.DMA((n, 7)), pltpu.SemaphoreType.DMA((n, 7)), pltpu.SemaphoreType.DMA((n,))],
        compiler_params=pltpu.CompilerParams(vmem_limit_bytes=VMEM_LIMIT),
    )(*shards)


def _pack_layout(D, n_rel_rows):
    n_bin = -(-D_IN // D)
    row_bin = len(GAINS)
    row_sink = row_bin + n_bin
    row_loss = row_sink + 1
    row_rb = -(-(row_loss + 1) // 8) * 8
    n_rows = row_rb + -(-n_rel_rows // 8) * 8
    bin_parts = [(r, min(D, D_IN - r * D)) for r in range(n_bin)]
    return row_bin, row_sink, row_loss, row_rb, n_rows, bin_parts


def _final_exchange(grad_blocks, partials, loss):
    D = partials["ffn1_pre_g"].shape[1]
    rb_shape = partials["rel_bias"].shape
    row_bin, row_sink, row_loss, row_rb, n_rows, bin_parts = _pack_layout(D, rb_shape[0])
    n_small = len(SMALL)

    def body(*refs):
        g_in = refs[0]
        part = dict(zip(SMALL, refs[1:1 + n_small]))
        loss_ref = refs[1 + n_small]
        landed, gath, pack, send_sems, recv_sems, local_sems = refs[2 + n_small:]

        pack[...] = jnp.zeros_like(pack)
        for i, name in enumerate(GAINS):
            pack[i:i + 1, :] = part[name][...]
        for r, width in bin_parts:
            pack[row_bin + r:row_bin + r + 1, 0:width] = part["b_in"][:, r * D:r * D + width]
        pack[row_sink:row_sink + 1, 0:LANES] = part["sinks"][...]
        pack[row_loss:row_loss + 1, 0:LANES] = loss_ref[...]
        pack[row_rb:row_rb + rb_shape[0], 0:rb_shape[1]] = part["rel_bias"][...]

        small_start, small_wait = _side_copies("gather", [pack], [gath], send_sems, recv_sems, local_sems, sem_row=0)
        big_start, big_wait = _side_copies("exchange", [g_in], [landed], send_sems, recv_sems, local_sems, sem_row=1)
        small_start()
        big_start()
        small_wait()
        big_wait()

    args = [grad_blocks] + [partials[k] for k in SMALL] + [loss]
    vmem = pl.BlockSpec(memory_space=pltpu.VMEM)
    any_spec = pl.BlockSpec(memory_space=pl.ANY)
    return pl.pallas_call(
        body, name="final_exchange",
        in_specs=[any_spec] + [vmem] * (n_small + 1),
        out_specs=[any_spec, any_spec],
        out_shape=[jax.ShapeDtypeStruct(grad_blocks.shape, grad_blocks.dtype),
                   jax.ShapeDtypeStruct((N_DEV, n_rows, D), F32)],
        scratch_shapes=[pltpu.VMEM((n_rows, D), F32), pltpu.SemaphoreType.DMA((2, 7)),
                        pltpu.SemaphoreType.DMA((2, 7)), pltpu.SemaphoreType.DMA((2,))],
    )(*args)


def _adamw(w, g, m, v):
    m = ADAM_B1 * m + (1.0 - ADAM_B1) * g
    v = ADAM_B2 * v + (1.0 - ADAM_B2) * (g * g)
    m_hat = m / (1.0 - ADAM_B1 ** ADAM_STEP)
    v_hat = v / (1.0 - ADAM_B2 ** ADAM_STEP)
    return -ADAM_LR * (m_hat / (jnp.sqrt(v_hat) + ADAM_EPS) + ADAM_WD * w), m, v


def _sum_adamw(partials, w, m, v, rows, name):
    R, C = w.shape

    def body(p_ref, w_ref, m_ref, v_ref, g_ref, d_ref, nm_ref, nv_ref):
        g = p_ref[0].astype(F32)
        for k in range(1, N_DEV):
            g = g + p_ref[k].astype(F32)
        g_ref[...] = g
        d_ref[...], nm_ref[...], nv_ref[...] = _adamw(w_ref[...], g, m_ref[...], v_ref[...])

    tile = pl.BlockSpec((rows, C), lambda i: (i, 0))
    return pl.pallas_call(
        body, name=name, grid=(R // rows,),
        in_specs=[pl.BlockSpec((N_DEV, rows, C), lambda i: (0, i, 0)), tile, tile, tile],
        out_specs=[tile] * 4, out_shape=[jax.ShapeDtypeStruct((R, C), F32)] * 4,
        compiler_params=_params(1),
    )(partials, w, m, v)


def _small_adamw(gathered, ws, ms, vs):
    D = ws["ffn1_pre_g"].shape[1]
    n_sink = ws["sinks"].shape[1]
    rb_shape = ws["rel_bias"].shape
    row_bin, row_sink, row_loss, row_rb, n_rows, bin_parts = _pack_layout(D, rb_shape[0])
    n_small = len(SMALL)

    def body(*refs):
        gath = refs[0]
        pos = 1
        w_ref = dict(zip(SMALL, refs[pos:pos + n_small]))
        m_ref = dict(zip(SMALL, refs[pos + n_small:pos + 2 * n_small]))
        v_ref = dict(zip(SMALL, refs[pos + 2 * n_small:pos + 3 * n_small]))
        pos += 3 * n_small
        outs = {name: refs[pos + 4 * i:pos + 4 * i + 4] for i, name in enumerate(SMALL)}
        loss_out = refs[pos + 4 * n_small]
        pack = refs[pos + 4 * n_small + 1]

        total = gath[0]
        for k in range(1, N_DEV):
            total = total + gath[k]
        pack[...] = total

        def update(name, g):
            g_out, d_out, m_out, v_out = outs[name]
            g_out[...] = g
            d_out[...], m_out[...], v_out[...] = _adamw(w_ref[name][...], g, m_ref[name][...], v_ref[name][...])

        for i, name in enumerate(GAINS):
            update(name, pack[i:i + 1, :])
        update("b_in", jnp.concatenate([pack[row_bin + r:row_bin + r + 1, 0:width] for r, width in bin_parts], axis=1))
        update("sinks", pack[row_sink:row_sink + 1, 0:n_sink])
        update("rel_bias", pack[row_rb:row_rb + rb_shape[0], 0:rb_shape[1]])
        loss_out[...] = pack[row_loss:row_loss + 1, 0:LANES]

    args = [gathered]
    for group in (ws, ms, vs):
        args += [group[k] for k in SMALL]
    out_shape = []
    for name in SMALL:
        out_shape += [jax.ShapeDtypeStruct(ws[name].shape, F32)] * 4
    out_shape.append(jax.ShapeDtypeStruct((1, LANES), F32))
    res = pl.pallas_call(
        body, name="small_adamw",
        in_specs=[pl.BlockSpec(memory_space=pltpu.VMEM)] * len(args),
        out_specs=[pl.BlockSpec(memory_space=pltpu.VMEM)] * len(out_shape),
        out_shape=out_shape,
        scratch_shapes=[pltpu.VMEM((n_rows, D), F32)],
    )(*args)
    per_name = {name: res[4 * i:4 * i + 4] for i, name in enumerate(SMALL)}
    return per_name, res[-1]


def _adamw_rows(name, rows_total):
    if name.endswith("w_down"):
        return rows_total // 2
    return min(rows_total, 256)


def kernel(x, p, rel_bias, ffn1_pre_g, ffn1_w_gu, ffn1_w_down, ffn1_post_g, attn_pre_g, w_in, b_in, sinks, w_out, b_out, attn_post_g, ffn2_pre_g, ffn2_w_gu, ffn2_w_down, ffn2_post_g, ple_pre_g, w_ple_gate, w_ple_proj, ple_post_g, loss_target, m_rel_bias, m_ffn1_pre_g, m_ffn1_w_gu, m_ffn1_w_down, m_ffn1_post_g, m_attn_pre_g, m_w_in, m_b_in, m_sinks, m_w_out, m_b_out, m_attn_post_g, m_ffn2_pre_g, m_ffn2_w_gu, m_ffn2_w_down, m_ffn2_post_g, m_ple_pre_g, m_w_ple_gate, m_w_ple_proj, m_ple_post_g, v_rel_bias, v_ffn1_pre_g, v_ffn1_w_gu, v_ffn1_w_down, v_ffn1_post_g, v_attn_pre_g, v_w_in, v_b_in, v_sinks, v_w_out, v_b_out, v_attn_post_g, v_ffn2_pre_g, v_ffn2_w_gu, v_ffn2_w_down, v_ffn2_post_g, v_ple_pre_g, v_w_ple_gate, v_w_ple_proj, v_ple_post_g):
    given = dict(locals())
    ws = {k: given[k] for k in WEIGHTS}
    ms = {k: given["m_" + k] for k in WEIGHTS}
    vs = {k: given["v_" + k] for k in WEIGHTS}

    def shard(t):
        return t.reshape(t.shape[1:])

    xs, ps, target = shard(x), shard(shard(p)), shard(loss_target)
    T, D = xs.shape
    small = {k: ws[k] for k in SMALL}
    shards = {k: shard(ws[k]) for k in BIG}

    w_gu1, w_down1 = _all_gather_bf16([shards["ffn1_w_gu"], shards["ffn1_w_down"]])
    w_down1 = w_down1.reshape(-1, D)
    later = ("w_in", "w_out", "ffn2_w_gu", "ffn2_w_down", "w_ple_gate", "w_ple_proj")
    cast = dict(zip(later, _cast_bf16([shards[k] for k in later])))

    buckets_a = _bucket_tiles(PATTERNS_A)
    buckets_b = _bucket_tiles(PATTERNS_B)
    bias_a = _bias_build(small["rel_bias"], buckets_a, 0, "bias_build_a")
    bias_b = _bias_build(small["rel_bias"], buckets_b, N_HEAD_GROUP, "bias_build_b")
    a_cfg = dict(patterns=PATTERNS_A, qcol=Q_A_COL, kcol=K_A_COL, vcol=V_A_COL, shared_kv=True)
    b_cfg = dict(patterns=PATTERNS_B, qcol=Q_B_COL, kcol=K_B_COL, vcol=V_B_COL, shared_kv=False)

    (h1, f1, a1, gu1), (w_in_g, w_down2) = _ffn_fwd(
        xs, small["ffn1_pre_g"], small["ffn1_post_g"], w_gu1, w_down1, "ffn1_fwd",
        side=("gather", [cast["w_in"], cast["ffn2_w_down"]]))
    w_in_full = jnp.transpose(w_in_g, (1, 0, 2)).reshape(D, D_IN)
    w_down2 = w_down2.reshape(-1, D)
    (z, a2), (w_out_g,) = _inproj_fwd(h1, small["attn_pre_g"], w_in_full, small["b_in"],
                                      side=("gather", [cast["w_out"]]))
    w_out_full = w_out_g.reshape(-1, D)
    (mix_a, lse_a), (w_gate, w_proj) = _attn_fwd(
        z, bias_a, small["sinks"], name="attn_a_fwd", **a_cfg,
        side=("gather", [cast["w_ple_gate"], cast["w_ple_proj"]]))
    w_gate = w_gate.reshape(-1, D)
    (mix_b, lse_b), (w_gu2,) = _attn_fwd(
        z, bias_b, None, name="attn_b_fwd", **b_cfg, side=("gather", [cast["ffn2_w_gu"]]))
    att, h2, mix = _outproj_fwd(mix_a, mix_b, w_out_full, small["b_out"], small["attn_post_g"], h1)
    (h3, f2, a3, gu2), _ = _ffn_fwd(h2, small["ffn2_pre_g"], small["ffn2_post_g"], w_gu2, w_down2, "ffn2_fwd")
    a4, dpre, de, dh4, loss, dg_ple_post = _ple_fwd_loss(
        h3, small["ple_pre_g"], w_gate, ps, w_proj, small["ple_post_g"], target)

    dh3, dg_ple_pre = _ple_bwd(dpre, w_gate, h3, small["ple_pre_g"], dh4)
    d_gate = _dw_rows(a4, dpre, "ple_dw_gate", min(256, D))
    d_proj = _ple_dw_proj(ps, de, N_DEV)
    landed = {}
    (dh2, df2, hh2, dgu2, dg_f2_post, dg_f2_pre), (landed["w_ple_gate"], landed["w_ple_proj"]) = _ffn_bwd(
        dh3, f2, small["ffn2_post_g"], h2, small["ffn2_pre_g"], gu2, w_gu2, w_down2, "ffn2_bwd",
        side=("exchange", [d_gate, d_proj]))
    d_gu2 = _dw_gu(a3, dgu2, "ffn2_dw_gu")
    d_down2 = _dw_down(hh2, df2, "ffn2_dw_down").reshape(N_DEV, -1, D)
    dmix_a, dmix_b, datt, dg_attn_post, db_out = _outproj_bwd(dh2, att, small["attn_post_g"], w_out_full)
    d_out = _dw_rows(mix, datt, "attn_dw_out", 256)
    (dqa, dka, dva, ds_a, dsinks), (landed["ffn2_w_down"],) = _attn_bwd(
        z, bias_a, small["sinks"], dmix_a, mix_a, lse_a, name="attn_a_bwd", **a_cfg,
        side=("exchange", [d_down2]))
    (dqb, dkb, dvb, ds_b), (landed["ffn2_w_gu"],) = _attn_bwd(
        z, bias_b, None, dmix_b, mix_b, lse_b, name="attn_b_bwd", **b_cfg, side=("exchange", [d_gu2]))
    (dh1, dz, db_in, dg_attn_pre), (landed["w_out"],) = _inproj_bwd(
        dqa, dka, dva, dqb, dkb, dvb, w_in_full, h1, small["attn_pre_g"], dh2, side=("exchange", [d_out]))
    cols = D_IN // 3
    d_in = _tn_matmul(
        a2, dz, _tok(D), pl.BlockSpec((TOKEN_TILE, cols), lambda b, t: (t, b)),
        jax.ShapeDtypeStruct((D, D_IN), BF16), pl.BlockSpec((D, cols), lambda b, t: (0, b)),
        3, T // TOKEN_TILE, (D, cols), "attn_dw_in")
    d_in = jnp.transpose(d_in.reshape(D, N_DEV, D_IN // N_DEV), (1, 0, 2))
    (grad_x, df1, hh1, dgu1, dg_f1_post, dg_f1_pre), _ = _ffn_bwd(
        dh1, f1, small["ffn1_post_g"], xs, small["ffn1_pre_g"], gu1, w_gu1, w_down1, "ffn1_bwd")
    d_down1, (landed["w_in"],) = _dw_down(hh1, df1, "ffn1_dw_down", side=("exchange", [d_in]))
    d_down1 = d_down1.reshape(N_DEV, -1, D)
    d_gu1, (landed["ffn1_w_down"],) = _dw_gu(a1, dgu1, "ffn1_dw_gu", side=("exchange", [d_down1]))

    rb_a = _bias_grad(ds_a, buckets_a, "bias_grad_a")
    rb_b = _bias_grad(ds_b, buckets_b, "bias_grad_b").reshape(len(PATTERNS_B), N_HEAD_GROUP, NUM_BUCKETS)
    d_rel_bias = jnp.concatenate([rb_a.T, jnp.sum(rb_b, axis=0).T], axis=1)
    small_grads = {"ffn1_pre_g": dg_f1_pre, "ffn1_post_g": dg_f1_post, "attn_pre_g": dg_attn_pre,
                   "attn_post_g": dg_attn_post, "ffn2_pre_g": dg_f2_pre, "ffn2_post_g": dg_f2_post,
                   "ple_pre_g": dg_ple_pre, "ple_post_g": dg_ple_post, "b_out": db_out, "b_in": db_in,
                   "sinks": dsinks, "rel_bias": d_rel_bias}
    landed["ffn1_w_gu"], small_gathered = _final_exchange(d_gu1, small_grads, loss)

    result = {}
    for k in BIG:
        rows_total = ws[k].shape[1]
        outs = _sum_adamw(landed[k], shards[k], shard(ms[k]), shard(vs[k]), _adamw_rows(k, rows_total), k + "_adamw")
        result[k] = [o.reshape(ws[k].shape) for o in outs]
    small_res, loss_all = _small_adamw(
        small_gathered, small, {k: ms[k] for k in SMALL}, {k: vs[k] for k in SMALL})
    result.update(small_res)

    out = [loss_all[0, 0], grad_x.reshape(x.shape)]
    for i in range(4):
        out += [result[k][i] for k in WEIGHTS]
    return tuple(out)
```

```python
import functools
import math

import numpy as np
import jax
import jax.numpy as jnp
from jax import lax
from jax.experimental import pallas as pl
from jax.experimental.pallas import tpu as pltpu

F32 = jnp.float32
BF16 = jnp.bfloat16
MESH = pl.DeviceIdType.MESH

N_DEV = 8
EPS = 1e-6
NEG_INF = -1e30
HEAD_DIM = 64
LANES = 128
QBLK = 128
D_IN = 2304
A_Q, A_KV, B_W = 512, 128, 512
N_HEAD_GROUP = 8
NUM_BUCKETS = 32
MAX_DISTANCE = 2048
PATTERNS_A = ((1, 127),)
PATTERNS_B = ((1, 128), (4, 128), (16, 128))
Q_A_COL, K_A_COL, V_A_COL = 0, 4, 5
Q_B_COL, K_B_COL, V_B_COL = 6, 10, 14

ADAM_LR, ADAM_B1, ADAM_B2, ADAM_EPS, ADAM_WD, ADAM_STEP = 0.001, 0.9, 0.999, 1e-08, 0.01, 10

TOKEN_TILE = 512
DW_TILE = 1024
FWD_BLOCKS = 4
BWD_BLOCKS = 2
VMEM_LIMIT = 56 * 1024 * 1024
ARB = "arbitrary"

BIG = ("ffn1_w_gu", "ffn1_w_down", "w_in", "w_out", "ffn2_w_gu", "ffn2_w_down", "w_ple_gate", "w_ple_proj")
GAINS = ("ffn1_pre_g", "ffn1_post_g", "attn_pre_g", "attn_post_g", "ffn2_pre_g", "ffn2_post_g",
         "ple_pre_g", "ple_post_g", "b_out")
SMALL = GAINS + ("b_in", "sinks", "rel_bias")
WEIGHTS = ("rel_bias", "ffn1_pre_g", "ffn1_w_gu", "ffn1_w_down", "ffn1_post_g", "attn_pre_g", "w_in", "b_in",
           "sinks", "w_out", "b_out", "attn_post_g", "ffn2_pre_g", "ffn2_w_gu", "ffn2_w_down", "ffn2_post_g",
           "ple_pre_g", "w_ple_gate", "w_ple_proj", "ple_post_g")


def _params(n_axes):
    return pltpu.CompilerParams(dimension_semantics=(ARB,) * n_axes, vmem_limit_bytes=VMEM_LIMIT)


def _mm(a, b):
    return jnp.dot(a, b, preferred_element_type=F32)


def _mm_nt(a, b):
    return lax.dot_general(a, b, (((1,), (1,)), ((), ())), preferred_element_type=F32)


def _mm_tn(a, b):
    return lax.dot_general(a, b, (((0,), (0,)), ((), ())), preferred_element_type=F32)


def _rstd(x):
    return lax.rsqrt(jnp.mean(x * x, axis=-1, keepdims=True) + EPS)


def _rms_bwd(x, r, gain, dy):
    n = x * r
    gdy = dy * gain
    return r * (gdy - n * jnp.mean(gdy * n, axis=-1, keepdims=True)), dy * n


def _colsum(v):
    return jnp.sum(v, axis=0, keepdims=True)


def _full(shape):
    return pl.BlockSpec(shape, lambda *_: (0,) * len(shape))


def _mesh_place():
    return lax.axis_index("x"), lax.axis_index("y"), lax.axis_index("c")


def _slot(dev):
    return 4 * dev[0] + 2 * dev[1] + dev[2]


def _peers(x, y, c):
    out = []
    for flip in range(1, N_DEV):
        dx, dy, dc = (flip >> 2) & 1, (flip >> 1) & 1, flip & 1
        out.append((1 - x if dx else x, 1 - y if dy else y, 1 - c if dc else c))
    return out


def _side_copies(kind, ins, outs, send_sems, recv_sems, local_sems, sem_row=0):
    n = len(ins)
    x, y, c = _mesh_place()
    me = _slot((x, y, c))
    peers = _peers(x, y, c)

    def src(a, block):
        return ins[a] if kind == "gather" else ins[a].at[block]

    def send(a, k, peer):
        return pltpu.make_async_remote_copy(
            src_ref=src(a, _slot(peer)), dst_ref=outs[a].at[me],
            send_sem=send_sems.at[sem_row + a, k], recv_sem=recv_sems.at[sem_row + a, k],
            device_id=peer, device_id_type=MESH)

    def arrival(a, k, peer):
        return pltpu.make_async_remote_copy(
            src_ref=src(a, _slot(peer)), dst_ref=outs[a].at[_slot(peer)],
            send_sem=send_sems.at[sem_row + a, k], recv_sem=recv_sems.at[sem_row + a, k],
            device_id=peer, device_id_type=MESH)

    def own(a):
        return pltpu.make_async_copy(src(a, me), outs[a].at[me], local_sems.at[sem_row + a])

    def start():
        for k, peer in enumerate(peers):
            for a in range(n):
                send(a, k, peer).start()
        for a in range(n):
            own(a).start()

    def wait():
        for k, peer in enumerate(peers):
            for a in range(n):
                arrival(a, k, peer).wait_recv()
        for k, peer in enumerate(peers):
            for a in range(n):
                send(a, k, peer).wait_send()
        for a in range(n):
            own(a).wait()

    return start, wait


def _side_out_shapes(kind, arrays):
    if kind == "gather":
        return [jax.ShapeDtypeStruct((N_DEV,) + a.shape, a.dtype) for a in arrays]
    return [jax.ShapeDtypeStruct(a.shape, a.dtype) for a in arrays]


def _hosted_call(body, name, grid, in_specs, out_specs, out_shape, scratch_shapes, args, side=None):
    if side is None:
        outs = pl.pallas_call(
            body, name=name, grid=grid, in_specs=in_specs, out_specs=out_specs, out_shape=out_shape,
            scratch_shapes=scratch_shapes, compiler_params=_params(len(grid)))(*args)
        return outs, []
    kind, arrays = side
    n_in, n_out, n_scr, n_side = len(in_specs), len(out_specs), len(scratch_shapes), len(arrays)

    def hosted(*refs):
        pos = 0
        groups = []
        for size in (n_in, n_side, n_out, n_side, n_scr):
            groups.append(refs[pos:pos + size])
            pos += size
        ins, side_in, outs, side_out, scr = groups
        send_sems, recv_sems, local_sems = refs[pos:]
        ids = [pl.program_id(d) for d in range(len(grid))]
        is_first = functools.reduce(jnp.logical_and, [i == 0 for i in ids])
        is_last = functools.reduce(jnp.logical_and, [i == g - 1 for i, g in zip(ids, grid)])
        start, wait = _side_copies(kind, side_in, side_out, send_sems, recv_sems, local_sems)
        pl.when(is_first)(start)
        body(*ins, *outs, *scr)
        pl.when(is_last)(wait)

    any_spec = pl.BlockSpec(memory_space=pl.ANY)
    outs = pl.pallas_call(
        hosted, name=name, grid=grid,
        in_specs=list(in_specs) + [any_spec] * n_side,
        out_specs=list(out_specs) + [any_spec] * n_side,
        out_shape=list(out_shape) + _side_out_shapes(kind, arrays),
        scratch_shapes=list(scratch_shapes) + [pltpu.SemaphoreType.DMA((n_side, 7)), pltpu.SemaphoreType.DMA((n_side, 7)),
                                               pltpu.SemaphoreType.DMA((n_side,))],
        compiler_params=_params(len(grid)))(*args, *arrays)
    return outs[:n_out], outs[n_out:]


def _ffn_fwd(h, g_pre, g_post, w_gu, w_down, name, side=None):
    T, D = h.shape
    nj = w_gu.shape[0] // 2
    FB = w_gu.shape[1]
    tm = TOKEN_TILE

    def body(h_ref, gpre_ref, gpost_ref, wg_ref, wu_ref, wd_ref, hout_ref, f_ref, a_ref, gu_ref, a_scr, acc):
        j = pl.program_id(1)

        @pl.when(j == 0)
        def _():
            x = h_ref[...]
            a = (x * _rstd(x) * gpre_ref[...]).astype(BF16)
            a_scr[...] = a
            a_ref[...] = a
            acc[...] = jnp.zeros_like(acc)

        a = a_scr[...]
        g = _mm_nt(a, wg_ref[...])
        u = _mm_nt(a, wu_ref[...])
        gu_ref[0] = g.astype(BF16)
        gu_ref[1] = u.astype(BF16)
        hh = (g * jax.nn.sigmoid(g) * u).astype(BF16)
        acc[...] += _mm(hh, wd_ref[...])

        @pl.when(j == nj - 1)
        def _():
            f = acc[...]
            f_ref[...] = f
            hout_ref[...] = h_ref[...] + 0.5 * (f * _rstd(f) * gpost_ref[...])

    return _hosted_call(
        body, name, (T // tm, nj),
        in_specs=[
            pl.BlockSpec((tm, D), lambda i, j: (i, 0)),
            _full((1, D)), _full((1, D)),
            pl.BlockSpec((None, FB, D), lambda i, j: (j, 0, 0)),
            pl.BlockSpec((None, FB, D), lambda i, j: (j + nj, 0, 0)),
            pl.BlockSpec((FB, D), lambda i, j: (j, 0)),
        ],
        out_specs=[
            pl.BlockSpec((tm, D), lambda i, j: (i, 0)),
            pl.BlockSpec((tm, D), lambda i, j: (i, 0)),
            pl.BlockSpec((tm, D), lambda i, j: (i, 0)),
            pl.BlockSpec((None, 2, tm, FB), lambda i, j: (j, 0, i, 0)),
        ],
        out_shape=[
            jax.ShapeDtypeStruct((T, D), F32),
            jax.ShapeDtypeStruct((T, D), F32),
            jax.ShapeDtypeStruct((T, D), BF16),
            jax.ShapeDtypeStruct((nj, 2, T, FB), BF16),
        ],
        scratch_shapes=[pltpu.VMEM((tm, D), BF16), pltpu.VMEM((tm, D), F32)],
        args=(h, g_pre, g_post, w_gu, w_gu, w_down), side=side)


def _ffn_bwd(dh_out, f, g_post, h, g_pre, gu, w_gu, w_down, name, side=None):
    T, D = h.shape
    nj = w_gu.shape[0] // 2
    FB = w_gu.shape[1]
    tm = TOKEN_TILE

    def body(dho_ref, f_ref, gpost_ref, h_ref, gpre_ref, gu_ref, wg_ref, wu_ref, wd_ref,
             dhin_ref, df_ref, hh_ref, dgu_ref, dgpost_ref, dgpre_ref, df_scr, da):
        i, j = pl.program_id(0), pl.program_id(1)

        @pl.when(jnp.logical_and(i == 0, j == 0))
        def _():
            dgpost_ref[...] = jnp.zeros_like(dgpost_ref)
            dgpre_ref[...] = jnp.zeros_like(dgpre_ref)

        @pl.when(j == 0)
        def _():
            fv = f_ref[...]
            df, dgain = _rms_bwd(fv, _rstd(fv), gpost_ref[...], 0.5 * dho_ref[...])
            dgpost_ref[...] += _colsum(dgain)
            dfb = df.astype(BF16)
            df_scr[...] = dfb
            df_ref[...] = dfb
            da[...] = jnp.zeros_like(da)

        dhh = _mm_nt(df_scr[...], wd_ref[...])
        g = gu_ref[0].astype(F32)
        u = gu_ref[1].astype(F32)
        sg = jax.nn.sigmoid(g)
        silu = g * sg
        hh_ref[...] = (silu * u).astype(BF16)
        dg = (dhh * u * (sg * (1.0 + g * (1.0 - sg)))).astype(BF16)
        du = (dhh * silu).astype(BF16)
        dgu_ref[0] = dg
        dgu_ref[1] = du
        da[...] += _mm(dg, wg_ref[...]) + _mm(du, wu_ref[...])

        @pl.when(j == nj - 1)
        def _():
            x = h_ref[...]
            dx, dgain = _rms_bwd(x, _rstd(x), gpre_ref[...], da[...])
            dgpre_ref[...] += _colsum(dgain)
            dhin_ref[...] = dho_ref[...] + dx

    tile = pl.BlockSpec((tm, D), lambda i, j: (i, 0))
    return _hosted_call(
        body, name, (T // tm, nj),
        in_specs=[
            tile, tile, _full((1, D)), tile, _full((1, D)),
            pl.BlockSpec((None, 2, tm, FB), lambda i, j: (j, 0, i, 0)),
            pl.BlockSpec((None, FB, D), lambda i, j: (j, 0, 0)),
            pl.BlockSpec((None, FB, D), lambda i, j: (j + nj, 0, 0)),
            pl.BlockSpec((FB, D), lambda i, j: (j, 0)),
        ],
        out_specs=[
            tile, tile,
            pl.BlockSpec((None, tm, FB), lambda i, j: (j, i, 0)),
            pl.BlockSpec((None, 2, tm, FB), lambda i, j: (j, 0, i, 0)),
            _full((1, D)), _full((1, D)),
        ],
        out_shape=[
            jax.ShapeDtypeStruct((T, D), F32),
            jax.ShapeDtypeStruct((T, D), BF16),
            jax.ShapeDtypeStruct((nj, T, FB), BF16),
            jax.ShapeDtypeStruct((nj, 2, T, FB), BF16),
            jax.ShapeDtypeStruct((1, D), F32),
            jax.ShapeDtypeStruct((1, D), F32),
        ],
        scratch_shapes=[pltpu.VMEM((tm, D), BF16), pltpu.VMEM((tm, D), F32)],
        args=(dh_out, f, g_post, h, g_pre, gu, w_gu, w_gu, w_down), side=side)


def _tn_matmul(x, y, x_spec, y_spec, out_shape, out_spec, n_blocks, n_steps, acc_shape, name, side=None):
    def body(x_ref, y_ref, o_ref, acc):
        t = pl.program_id(1)

        @pl.when(t == 0)
        def _():
            acc[...] = jnp.zeros_like(acc)

        acc[...] += _mm_tn(x_ref[...].astype(BF16), y_ref[...].astype(BF16))

        @pl.when(t == n_steps - 1)
        def _():
            o_ref[...] = acc[...].astype(o_ref.dtype)

    outs, side_outs = _hosted_call(
        body, name, (n_blocks, n_steps), in_specs=[x_spec, y_spec], out_specs=[out_spec], out_shape=[out_shape],
        scratch_shapes=[pltpu.VMEM(acc_shape, F32)], args=(x, y), side=side)
    return (outs[0], side_outs) if side is not None else outs[0]


def _inproj_fwd(h, g_pre, w_in, b_in, side=None):
    T, D = h.shape
    tm = TOKEN_TILE

    def body(h_ref, g_ref, w_ref, b_ref, z_ref, a_ref):
        x = h_ref[...]
        a = (x * _rstd(x) * g_ref[...]).astype(BF16)
        a_ref[...] = a
        z_ref[...] = _mm_nt(a, w_ref[...]) + b_ref[...]

    return _hosted_call(
        body, "inproj_fwd", (T // tm,),
        in_specs=[pl.BlockSpec((tm, D), lambda i: (i, 0)), _full((1, D)), _full((D_IN, D)), _full((1, D_IN))],
        out_specs=[pl.BlockSpec((tm, D_IN), lambda i: (i, 0)), pl.BlockSpec((tm, D), lambda i: (i, 0))],
        out_shape=[jax.ShapeDtypeStruct((T, D_IN), F32), jax.ShapeDtypeStruct((T, D), BF16)],
        scratch_shapes=[], args=(h, g_pre, w_in, b_in), side=side)


def _inproj_bwd(dqa, dka, dva, dqb, dkb, dvb, w_in, h, g_pre, dres, side=None):
    T, D = h.shape
    tm = TOKEN_TILE

    def body(dqa_ref, dka_ref, dva_ref, dqb_ref, dkb_ref, dvb_ref, w_ref, h_ref, g_ref, dres_ref,
             dh_ref, dz_ref, dbin_ref, dg_ref):
        i = pl.program_id(0)

        @pl.when(i == 0)
        def _():
            dbin_ref[...] = jnp.zeros_like(dbin_ref)
            dg_ref[...] = jnp.zeros_like(dg_ref)

        dz = jnp.concatenate([dqa_ref[...], dka_ref[...], dva_ref[...], dqb_ref[...], dkb_ref[...], dvb_ref[...]],
                             axis=1)
        dbin_ref[...] += _colsum(dz)
        dzb = dz.astype(BF16)
        dz_ref[...] = dzb
        da = _mm(dzb, w_ref[...])
        x = h_ref[...]
        dx, dgain = _rms_bwd(x, _rstd(x), g_ref[...], da)
        dg_ref[...] += _colsum(dgain)
        dh_ref[...] = dres_ref[...] + dx

    def tile(w):
        return pl.BlockSpec((tm, w), lambda i: (i, 0))

    return _hosted_call(
        body, "inproj_bwd", (T // tm,),
        in_specs=[tile(A_Q), tile(A_KV), tile(A_KV), tile(B_W), tile(B_W), tile(B_W),
                  _full((D_IN, D)), tile(D), _full((1, D)), tile(D)],
        out_specs=[tile(D), tile(D_IN), _full((1, D_IN)), _full((1, D))],
        out_shape=[jax.ShapeDtypeStruct((T, D), F32), jax.ShapeDtypeStruct((T, D_IN), BF16),
                   jax.ShapeDtypeStruct((1, D_IN), F32), jax.ShapeDtypeStruct((1, D), F32)],
        scratch_shapes=[], args=(dqa, dka, dva, dqb, dkb, dvb, w_in, h, g_pre, dres), side=side)


def _bucket_tiles(patterns):
    i = np.arange(QBLK)[:, None]
    j = np.arange(2 * QBLK)[None, :]
    dist = QBLK + i - j
    max_exact = NUM_BUCKETS // 2
    tiles = []
    for dilation, max_dist in patterns:
        n = np.maximum(dist * dilation, 0)
        nf = np.maximum(n, 1).astype(np.float32)
        large = max_exact + (np.log(nf / np.float32(max_exact)) / np.float32(math.log(MAX_DISTANCE / max_exact))
                             * np.float32(NUM_BUCKETS - max_exact)).astype(np.int32)
        bucket = np.where(n < max_exact, n, np.minimum(large, NUM_BUCKETS - 1))
        tiles.append(np.where((dist >= 0) & (dist <= max_dist), bucket, -1))
    return jnp.asarray(np.stack(tiles).astype(np.int32))


def _bias_build(rel_bias, buckets, head0, name):
    n = buckets.shape[0]

    def body(bk_ref, rb_ref, o_ref):
        bk = bk_ref[...]
        base = jnp.where(bk < 0, NEG_INF, 0.0).astype(F32)
        for hd in range(N_HEAD_GROUP):
            o_ref[hd] = lax.fori_loop(
                0, NUM_BUCKETS, lambda b, acc, hd=hd: jnp.where(bk == b, rb_ref[b, head0 + hd], acc), base)

    return pl.pallas_call(
        body, name=name, grid=(n,),
        in_specs=[pl.BlockSpec((None, QBLK, 2 * QBLK), lambda p: (p, 0, 0)), pl.BlockSpec(memory_space=pltpu.SMEM)],
        out_specs=pl.BlockSpec((None, N_HEAD_GROUP, QBLK, 2 * QBLK), lambda p: (p, 0, 0, 0)),
        out_shape=jax.ShapeDtypeStruct((n, N_HEAD_GROUP, QBLK, 2 * QBLK), F32),
        compiler_params=_params(1),
    )(buckets, rel_bias)


def _bias_grad(ds, buckets, name):
    n = buckets.shape[0]

    def body(ds_ref, bk_ref, o_ref):
        bk = bk_ref[...]
        row = lax.broadcasted_iota(jnp.int32, (NUM_BUCKETS, 2 * QBLK), 0)
        for hd in range(N_HEAD_GROUP):
            d = ds_ref[hd]
            per_key = jnp.zeros((NUM_BUCKETS, 2 * QBLK), F32)
            for b in range(NUM_BUCKETS):
                per_key = jnp.where(row == b, jnp.sum(jnp.where(bk == b, d, 0.0), axis=0, keepdims=True), per_key)
            o_ref[hd] = jnp.broadcast_to(jnp.sum(per_key, axis=1, keepdims=True), (NUM_BUCKETS, LANES))

    out = pl.pallas_call(
        body, name=name, grid=(n,),
        in_specs=[pl.BlockSpec((None, N_HEAD_GROUP, QBLK, 2 * QBLK), lambda p: (p, 0, 0, 0)),
                  pl.BlockSpec((None, QBLK, 2 * QBLK), lambda p: (p, 0, 0))],
        out_specs=pl.BlockSpec((None, N_HEAD_GROUP, NUM_BUCKETS, LANES), lambda p: (p, 0, 0, 0)),
        out_shape=jax.ShapeDtypeStruct((n, N_HEAD_GROUP, NUM_BUCKETS, LANES), F32),
        compiler_params=_params(1),
    )(ds, buckets)
    return out[:, :, :, 0].reshape(n * N_HEAD_GROUP, NUM_BUCKETS)


def _class_rows(start, dilation):
    if dilation == 1:
        return pl.ds(pl.multiple_of(start, QBLK), QBLK)
    return pl.ds(start, QBLK, stride=dilation)


def _block_starts(idx, n_blocks, dilation):
    cls = idx // n_blocks
    n = idx % n_blocks
    cur = cls + dilation * QBLK * n
    prev = cls + dilation * QBLK * jnp.maximum(n - 1, 0)
    return n, cur, prev


class _HeadPair:
    def __init__(self, g, shared_kv):
        self.lane = lax.broadcasted_iota(jnp.int32, (1, LANES), 1)
        self.lower = self.lane < HEAD_DIM
        self.shared_kv = shared_kv
        self.key_lanes = (self.lane >= HEAD_DIM).astype(jnp.int32) == (g // 2)

    def stack(self, t):
        return jnp.concatenate([jnp.where(self.lower, t, 0.0), jnp.where(self.lower, 0.0, t)], axis=0).astype(BF16)

    def unstack(self, t2):
        return jnp.where(self.lower, t2[:QBLK], t2[QBLK:])

    def keys(self, t):
        if self.shared_kv:
            return jnp.where(self.key_lanes, t, pltpu.roll(t, HEAD_DIM, 1))
        return t

    def key_grads(self, t):
        if self.shared_kv:
            return jnp.where(self.key_lanes, t + pltpu.roll(t, HEAD_DIM, 1), 0.0)
        return t


def _attn_specs(T, qcol, kcol, vcol, shared_kv):
    kv = (lambda c: (lambda g: (0, c))) if shared_kv else (lambda c: (lambda g: (0, c + g)))
    return [pl.BlockSpec((T, LANES), lambda g: (0, qcol + g)),
            pl.BlockSpec((T, LANES), kv(kcol)),
            pl.BlockSpec((T, LANES), kv(vcol))]


def _attn_fwd(z, bias, sinks, patterns, qcol, kcol, vcol, shared_kv, name, side=None):
    T = z.shape[0]
    n_pat = len(patterns)
    has_sink = sinks is not None

    def body(*refs):
        if has_sink:
            sink_ref, refs = refs[0], refs[1:]
        q_ref, k_ref, v_ref, b_ref, o_ref, l_ref = refs[:6]
        po_scr = refs[6:6 + n_pat]
        pl_scr = refs[6 + n_pat:]
        g = pl.program_id(0)
        heads = _HeadPair(g, shared_kv)
        in_prev = lax.broadcasted_iota(jnp.int32, (2 * QBLK, 2 * QBLK), 1) < QBLK

        for pi, (dilation, _) in enumerate(patterns):
            n_blocks = T // (QBLK * dilation)

            def step(it, carry, pi=pi, dilation=dilation, n_blocks=n_blocks):
                blocks = []
                for u in range(FWD_BLOCKS):
                    n, cur, prev = _block_starts(it * FWD_BLOCKS + u, n_blocks, dilation)
                    rows_c, rows_p = _class_rows(cur, dilation), _class_rows(prev, dilation)
                    qm = heads.stack(q_ref[rows_c, :])
                    k2 = heads.keys(jnp.concatenate([k_ref[rows_p, :], k_ref[rows_c, :]], axis=0)).astype(BF16)
                    v2 = heads.keys(jnp.concatenate([v_ref[rows_p, :], v_ref[rows_c, :]], axis=0)).astype(BF16)
                    blocks.append(dict(n=n, rows=rows_c, v2=v2, s=_mm_nt(qm, k2)))
                for b in blocks:
                    s = b["s"] * (HEAD_DIM ** -0.5) + b_ref[pi]
                    b["s"] = jnp.where(jnp.logical_and(in_prev, b["n"] == 0), NEG_INF, s)
                    b["m"] = jnp.max(b["s"], axis=1, keepdims=True)
                for b in blocks:
                    b["pr"] = jnp.exp(b["s"] - b["m"])
                    b["den"] = jnp.sum(b["pr"], axis=1, keepdims=True)
                for b in blocks:
                    b["o2"] = _mm(b["pr"].astype(BF16), b["v2"])
                for b in blocks:
                    lse = b["m"] + jnp.log(b["den"])
                    po_scr[pi][b["rows"], :] = heads.unstack(b["o2"] / b["den"])
                    pl_scr[2 * pi][b["rows"], :] = jnp.broadcast_to(lse[:QBLK], (QBLK, LANES))
                    pl_scr[2 * pi + 1][b["rows"], :] = jnp.broadcast_to(lse[QBLK:], (QBLK, LANES))
                return carry

            lax.fori_loop(0, (dilation * n_blocks) // FWD_BLOCKS, step, 0)

        def merge(ci, carry):
            rows = pl.ds(pl.multiple_of(ci * QBLK, QBLK), QBLK)
            weights = []
            for hd in range(2):
                parts = [pl_scr[2 * pi + hd][rows, :] for pi in range(n_pat)]
                m = functools.reduce(jnp.maximum, parts)
                if has_sink:
                    sink = sink_ref[0, 2 * g + hd]
                    m = jnp.maximum(m, sink)
                den = functools.reduce(jnp.add, [jnp.exp(x - m) for x in parts])
                if has_sink:
                    den = den + jnp.exp(sink - m)
                lse = m + jnp.log(den)
                l_ref[hd, rows, :] = lse
                weights.append([jnp.exp(x - lse) for x in parts])
            o_ref[rows, :] = functools.reduce(
                jnp.add, [jnp.where(heads.lower, weights[0][pi], weights[1][pi]) * po_scr[pi][rows, :]
                          for pi in range(n_pat)])
            return carry

        lax.fori_loop(0, T // QBLK, merge, 0)

    in_specs = _attn_specs(T, qcol, kcol, vcol, shared_kv)
    in_specs.append(pl.BlockSpec((n_pat, None, 2 * QBLK, 2 * QBLK), lambda g: (0, g, 0, 0)))
    args = [z, z, z, bias.reshape(n_pat, N_HEAD_GROUP // 2, 2 * QBLK, 2 * QBLK)]
    if has_sink:
        in_specs.insert(0, pl.BlockSpec(memory_space=pltpu.SMEM))
        args.insert(0, sinks)
    return _hosted_call(
        body, name, (N_HEAD_GROUP // 2,),
        in_specs=in_specs,
        out_specs=[pl.BlockSpec((T, LANES), lambda g: (0, g)), pl.BlockSpec((2, T, LANES), lambda g: (g, 0, 0))],
        out_shape=[jax.ShapeDtypeStruct((T, N_HEAD_GROUP * HEAD_DIM), F32),
                   jax.ShapeDtypeStruct((N_HEAD_GROUP, T, LANES), F32)],
        scratch_shapes=[pltpu.VMEM((T, LANES), F32)] * (3 * n_pat), args=args, side=side)


def _attn_bwd(z, bias, sinks, d_out, out, lse, patterns, qcol, kcol, vcol, shared_kv, name, side=None):
    T = z.shape[0]
    n_pat = len(patterns)
    has_sink = sinks is not None
    kv_width = LANES if shared_kv else N_HEAD_GROUP * HEAD_DIM

    def body(*refs):
        if has_sink:
            sink_ref, refs = refs[0], refs[1:]
        q_ref, k_ref, v_ref, b_ref, do_ref, o_ref, l0_ref, l1_ref = refs[:8]
        dq_ref, dk_ref, dv_ref, ds_ref = refs[8:12]
        dsink_ref = refs[12] if has_sink else None
        dk_acc, dv_acc = refs[-2:]
        g = pl.program_id(0)
        heads = _HeadPair(g, shared_kv)
        in_prev = lax.broadcasted_iota(jnp.int32, (2 * QBLK, 2 * QBLK), 1) < QBLK

        dq_ref[...] = jnp.zeros_like(dq_ref)
        ds_ref[...] = jnp.zeros_like(ds_ref)
        dk_acc[...] = jnp.zeros_like(dk_acc)
        dv_acc[...] = jnp.zeros_like(dv_acc)

        dsink = jnp.zeros((1, LANES), F32)
        for pi, (dilation, _) in enumerate(patterns):
            n_blocks = T // (QBLK * dilation)

            def step(idx, dsink, pi=pi, dilation=dilation, n_blocks=n_blocks):
                blocks = []
                for u in range(BWD_BLOCKS):
                    n, cur, prev = _block_starts(idx * BWD_BLOCKS + u, n_blocks, dilation)
                    rows_c, rows_p = _class_rows(cur, dilation), _class_rows(prev, dilation)
                    qm = heads.stack(q_ref[rows_c, :])
                    k2 = heads.keys(jnp.concatenate([k_ref[rows_p, :], k_ref[rows_c, :]], axis=0)).astype(BF16)
                    v2 = heads.keys(jnp.concatenate([v_ref[rows_p, :], v_ref[rows_c, :]], axis=0)).astype(BF16)
                    d_o = do_ref[rows_c, :]
                    dom = heads.stack(d_o)
                    dd = d_o * o_ref[rows_c, :]
                    delta = jnp.concatenate([jnp.sum(jnp.where(heads.lower, dd, 0.0), axis=1, keepdims=True),
                                             jnp.sum(jnp.where(heads.lower, 0.0, dd), axis=1, keepdims=True)], axis=0)
                    lse = jnp.concatenate([l0_ref[rows_c, :], l1_ref[rows_c, :]], axis=0)
                    blocks.append(dict(n=n, rows_c=rows_c, rows_p=rows_p, qm=qm, k2=k2, dom=dom, delta=delta, lse=lse,
                                       s=_mm_nt(qm, k2), dp=_mm_nt(dom, v2)))
                for b in blocks:
                    s = b["s"] * (HEAD_DIM ** -0.5) + b_ref[pi]
                    s = jnp.where(jnp.logical_and(in_prev, b["n"] == 0), NEG_INF, s)
                    b["pr"] = jnp.exp(s - jnp.concatenate([b["lse"], b["lse"]], axis=1))
                    b["ds"] = b["pr"] * (b["dp"] - b["delta"])
                for b in blocks:
                    dsb = b["ds"].astype(BF16)
                    b["dq2"] = _mm(dsb, b["k2"])
                    b["dk2"] = _mm_tn(dsb, b["qm"])
                    b["dv2"] = _mm_tn(b["pr"].astype(BF16), b["dom"])
                for b in blocks:
                    ds_ref[pi] += b["ds"]
                    dq_ref[b["rows_c"], :] += heads.unstack(b["dq2"]) * (HEAD_DIM ** -0.5)
                    dk2 = heads.key_grads(b["dk2"]) * (HEAD_DIM ** -0.5)
                    dv2 = heads.key_grads(b["dv2"])
                    dk_acc[b["rows_p"], :] += dk2[:QBLK]
                    dk_acc[b["rows_c"], :] += dk2[QBLK:]
                    dv_acc[b["rows_p"], :] += dv2[:QBLK]
                    dv_acc[b["rows_c"], :] += dv2[QBLK:]
                    if has_sink:
                        for hd in range(2):
                            rows_h = slice(QBLK * hd, QBLK * (hd + 1))
                            p_sink = jnp.exp(sink_ref[0, 2 * g + hd] - b["lse"][rows_h, 0:1])
                            dsink = dsink - jnp.where(heads.lane == 2 * g + hd,
                                                      jnp.sum(p_sink * b["delta"][rows_h]), 0.0)
                return dsink

            dsink = lax.fori_loop(0, (dilation * n_blocks) // BWD_BLOCKS, step, dsink)

        if shared_kv:
            @pl.when(g == 0)
            def _():
                dk_ref[...] = dk_acc[...]
                dv_ref[...] = dv_acc[...]

            @pl.when(g != 0)
            def _():
                dk_ref[...] += dk_acc[...]
                dv_ref[...] += dv_acc[...]
        else:
            dk_ref[...] = dk_acc[...]
            dv_ref[...] = dv_acc[...]

        if has_sink:
            @pl.when(g == 0)
            def _():
                dsink_ref[...] = dsink

            @pl.when(g != 0)
            def _():
                dsink_ref[...] += dsink

    pair = pl.BlockSpec((T, LANES), lambda g: (0, g))
    stacked = pl.BlockSpec((n_pat, None, 2 * QBLK, 2 * QBLK), lambda g: (0, g, 0, 0))
    stacked_shape = (n_pat, N_HEAD_GROUP // 2, 2 * QBLK, 2 * QBLK)
    in_specs = _attn_specs(T, qcol, kcol, vcol, shared_kv)
    in_specs += [stacked, pair, pair,
                 pl.BlockSpec((None, T, LANES), lambda g: (2 * g, 0, 0)),
                 pl.BlockSpec((None, T, LANES), lambda g: (2 * g + 1, 0, 0))]
    args = [z, z, z, bias.reshape(stacked_shape), d_out, out, lse, lse]
    kv_out = _full((T, LANES)) if shared_kv else pair
    out_specs = [pair, kv_out, kv_out, stacked]
    out_shape = [jax.ShapeDtypeStruct((T, N_HEAD_GROUP * HEAD_DIM), F32),
                 jax.ShapeDtypeStruct((T, kv_width), F32), jax.ShapeDtypeStruct((T, kv_width), F32),
                 jax.ShapeDtypeStruct(stacked_shape, F32)]
    if has_sink:
        in_specs.insert(0, pl.BlockSpec(memory_space=pltpu.SMEM))
        args.insert(0, sinks)
        out_specs.append(_full((1, LANES)))
        out_shape.append(jax.ShapeDtypeStruct((1, LANES), F32))
    outs, side_outs = _hosted_call(
        body, name, (N_HEAD_GROUP // 2,), in_specs=in_specs, out_specs=out_specs, out_shape=out_shape,
        scratch_shapes=[pltpu.VMEM((T, LANES), F32), pltpu.VMEM((T, LANES), F32)], args=args, side=side)
    outs = list(outs)
    outs[3] = outs[3].reshape(n_pat, N_HEAD_GROUP, QBLK, 2 * QBLK)
    return outs, side_outs


def _outproj_fwd(mix_a, mix_b, w_out, b_out, g_post, h):
    T, D = h.shape
    tm = TOKEN_TILE
    d_mix = w_out.shape[0]

    def body(ma_ref, mb_ref, w_ref, b_ref, g_ref, h_ref, att_ref, hout_ref, mix_ref):
        mix = jnp.concatenate([ma_ref[...], mb_ref[...]], axis=1).astype(BF16)
        mix_ref[...] = mix
        att = _mm(mix, w_ref[...]) + b_ref[...]
        att_ref[...] = att
        hout_ref[...] = h_ref[...] + att * _rstd(att) * g_ref[...]

    def tile(w):
        return pl.BlockSpec((tm, w), lambda i: (i, 0))

    return pl.pallas_call(
        body, name="outproj_fwd", grid=(T // tm,),
        in_specs=[tile(A_Q), tile(B_W), _full((d_mix, D)), _full((1, D)), _full((1, D)), tile(D)],
        out_specs=[tile(D), tile(D), tile(d_mix)],
        out_shape=[jax.ShapeDtypeStruct((T, D), F32), jax.ShapeDtypeStruct((T, D), F32),
                   jax.ShapeDtypeStruct((T, d_mix), BF16)],
        compiler_params=_params(1),
    )(mix_a, mix_b, w_out, b_out, g_post, h)


def _outproj_bwd(dh, att, g_post, w_out):
    T, D = dh.shape
    tm = TOKEN_TILE
    d_mix = w_out.shape[0]

    def body(dh_ref, att_ref, g_ref, w_ref, dma_ref, dmb_ref, datt_ref, dg_ref, db_ref):
        i = pl.program_id(0)

        @pl.when(i == 0)
        def _():
            dg_ref[...] = jnp.zeros_like(dg_ref)
            db_ref[...] = jnp.zeros_like(db_ref)

        att = att_ref[...]
        datt, dgain = _rms_bwd(att, _rstd(att), g_ref[...], dh_ref[...])
        dg_ref[...] += _colsum(dgain)
        db_ref[...] += _colsum(datt)
        dattb = datt.astype(BF16)
        datt_ref[...] = dattb
        dmix = _mm_nt(dattb, w_ref[...])
        dma_ref[...] = dmix[:, :A_Q]
        dmb_ref[...] = dmix[:, A_Q:]

    def tile(w):
        return pl.BlockSpec((tm, w), lambda i: (i, 0))

    return pl.pallas_call(
        body, name="outproj_bwd", grid=(T // tm,),
        in_specs=[tile(D), tile(D), _full((1, D)), _full((d_mix, D))],
        out_specs=[tile(A_Q), tile(B_W), tile(D), _full((1, D)), _full((1, D))],
        out_shape=[jax.ShapeDtypeStruct((T, A_Q), F32), jax.ShapeDtypeStruct((T, B_W), F32),
                   jax.ShapeDtypeStruct((T, D), BF16), jax.ShapeDtypeStruct((1, D), F32),
                   jax.ShapeDtypeStruct((1, D), F32)],
        compiler_params=_params(1),
    )(dh, att, g_post, w_out)


def _ple_fwd_loss(h, g_pre, w_gate, p, w_proj, g_post, target):
    T, D = h.shape
    tm = TOKEN_TILE
    n_proj, ple, db = w_proj.shape

    def body(h_ref, gpre_ref, wg_ref, p_ref, wp_ref, gpost_ref, t_ref,
             a_ref, dpre_ref, de_ref, dh_ref, loss_ref, dgpost_ref):
        i = pl.program_id(0)

        @pl.when(i == 0)
        def _():
            loss_ref[...] = jnp.zeros_like(loss_ref)
            dgpost_ref[...] = jnp.zeros_like(dgpost_ref)

        x = h_ref[...]
        a = (x * _rstd(x) * gpre_ref[...]).astype(BF16)
        a_ref[...] = a
        gate = jax.nn.sigmoid(_mm(a, wg_ref[...]))
        pb = p_ref[...].astype(BF16)
        e = jnp.concatenate([_mm(pb, wp_ref[k]) for k in range(n_proj)], axis=1)
        ge = gate * e
        rg = _rstd(ge)
        diff = x + ge * rg * gpost_ref[...] - t_ref[...]
        loss_ref[...] += 0.5 * jnp.sum(jnp.mean(diff * diff, axis=1, keepdims=True))
        dy = diff * (1.0 / D)
        dh_ref[...] = dy
        dge, dgain = _rms_bwd(ge, rg, gpost_ref[...], dy)
        dgpost_ref[...] += _colsum(dgain)
        de_ref[...] = (dge * gate).astype(BF16)
        dpre_ref[...] = (dge * e * gate * (1.0 - gate)).astype(BF16)

    def tile(w):
        return pl.BlockSpec((tm, w), lambda i: (i, 0))

    return pl.pallas_call(
        body, name="ple_fwd_loss", grid=(T // tm,),
        in_specs=[tile(D), _full((1, D)), _full((D, D)), tile(ple), _full((n_proj, ple, db)), _full((1, D)), tile(D)],
        out_specs=[tile(D), tile(D), tile(D), tile(D), _full((1, LANES)), _full((1, D))],
        out_shape=[jax.ShapeDtypeStruct((T, D), BF16),
                   jax.ShapeDtypeStruct((T, D), BF16),
                   jax.ShapeDtypeStruct((T, D), BF16),
                   jax.ShapeDtypeStruct((T, D), F32),
                   jax.ShapeDtypeStruct((1, LANES), F32),
                   jax.ShapeDtypeStruct((1, D), F32)],
        compiler_params=_params(1),
    )(h, g_pre, w_gate, p, w_proj, g_post, target)


def _ple_bwd(dpre, w_gate, h, g_pre, dres):
    T, D = h.shape
    tm = TOKEN_TILE

    def body(dpre_ref, w_ref, h_ref, g_ref, dres_ref, dh_ref, dg_ref):
        i = pl.program_id(0)

        @pl.when(i == 0)
        def _():
            dg_ref[...] = jnp.zeros_like(dg_ref)

        da = _mm_nt(dpre_ref[...], w_ref[...])
        x = h_ref[...]
        dx, dgain = _rms_bwd(x, _rstd(x), g_ref[...], da)
        dg_ref[...] += _colsum(dgain)
        dh_ref[...] = dres_ref[...] + dx

    tile = pl.BlockSpec((tm, D), lambda i: (i, 0))
    return pl.pallas_call(
        body, name="ple_bwd", grid=(T // tm,),
        in_specs=[tile, _full((D, D)), tile, _full((1, D)), tile],
        out_specs=[tile, _full((1, D))],
        out_shape=[jax.ShapeDtypeStruct((T, D), F32), jax.ShapeDtypeStruct((1, D), F32)],
        compiler_params=_params(1),
    )(dpre, w_gate, h, g_pre, dres)


def _ple_dw_proj(p, de, n_proj):
    T, ple = p.shape
    D = de.shape[1]
    db = D // n_proj
    tk = TOKEN_TILE
    nt = T // tk

    def body(p_ref, de_ref, o_ref, acc):
        t = pl.program_id(0)

        @pl.when(t == 0)
        def _():
            acc[...] = jnp.zeros_like(acc)

        acc[...] += _mm_tn(p_ref[...].astype(BF16), de_ref[...])

        @pl.when(t == nt - 1)
        def _():
            for k in range(n_proj):
                o_ref[k] = acc[:, k * db:(k + 1) * db].astype(BF16)

    return pl.pallas_call(
        body, name="ple_dw_proj", grid=(nt,),
        in_specs=[pl.BlockSpec((tk, ple), lambda t: (t, 0)), pl.BlockSpec((tk, D), lambda t: (t, 0))],
        out_specs=_full((n_proj, ple, db)), out_shape=jax.ShapeDtypeStruct((n_proj, ple, db), BF16),
        scratch_shapes=[pltpu.VMEM((ple, D), F32)], compiler_params=_params(1),
    )(p, de)


def _tok(width):
    return pl.BlockSpec((DW_TILE, width), lambda b, t: (t, 0))


def _dw_gu(a, dgu, name, side=None):
    T, D = a.shape
    nj, _, _, FB = dgu.shape
    return _tn_matmul(
        dgu, a, pl.BlockSpec((None, None, DW_TILE, FB), lambda b, t: (b % nj, b // nj, t, 0)), _tok(D),
        jax.ShapeDtypeStruct((2 * nj, FB, D), BF16), pl.BlockSpec((None, FB, D), lambda b, t: (b, 0, 0)),
        2 * nj, T // DW_TILE, (FB, D), name, side=side)


def _dw_down(hh, df, name, side=None):
    nj, T, FB = hh.shape
    D = df.shape[1]
    return _tn_matmul(
        hh, df, pl.BlockSpec((None, DW_TILE, FB), lambda b, t: (b, t, 0)), _tok(D),
        jax.ShapeDtypeStruct((nj, FB, D), BF16), pl.BlockSpec((None, FB, D), lambda b, t: (b, 0, 0)),
        nj, T // DW_TILE, (FB, D), name, side=side)


def _dw_rows(xm, y, name, rows):
    T, k = xm.shape
    D = y.shape[1]
    out = _tn_matmul(
        xm, y, pl.BlockSpec((DW_TILE, rows), lambda b, t: (t, b)), _tok(D),
        jax.ShapeDtypeStruct((k, D), BF16), pl.BlockSpec((rows, D), lambda b, t: (b, 0)),
        k // rows, T // DW_TILE, (rows, D), name)
    return out.reshape(N_DEV, k // N_DEV, D)


def _cast_bf16(arrays):
    n = len(arrays)

    def body(*refs):
        for a in range(n):
            refs[n + a][...] = refs[a][...].astype(BF16)

    return pl.pallas_call(
        body, name="cast_shards",
        in_specs=[pl.BlockSpec(memory_space=pltpu.VMEM)] * n, out_specs=[pl.BlockSpec(memory_space=pltpu.VMEM)] * n,
        out_shape=[jax.ShapeDtypeStruct(a.shape, BF16) for a in arrays],
        compiler_params=pltpu.CompilerParams(vmem_limit_bytes=VMEM_LIMIT),
    )(*arrays)


def _all_gather_bf16(shards):
    n = len(shards)

    def body(*refs):
        ins, outs, scr = refs[:n], refs[n:2 * n], refs[2 * n:3 * n]
        send_sems, recv_sems, local_sems = refs[3 * n:]
        x, y, c = _mesh_place()
        me, sibling = (x, y, c), (x, y, 1 - c)
        chips = [(1 - x, y), (x, 1 - y), (1 - x, 1 - y)]
        for a in range(n):
            scr[a][...] = ins[a][...].astype(BF16)

        def copy(a, k, block, to, src=None):
            dst = outs[a].at[_slot(block)]
            return pltpu.make_async_remote_copy(
                src_ref=dst if src is None else src, dst_ref=dst,
                send_sem=send_sems.at[a, k], recv_sem=recv_sems.at[a, k], device_id=to, device_id_type=MESH)

        mine = [pltpu.make_async_copy(scr[a], outs[a].at[_slot(me)], local_sems.at[a]) for a in range(n)]
        first = [copy(a, 1 + j, me, (*chip, c), src=scr[a]) for j, chip in enumerate(chips) for a in range(n)]
        first += [copy(a, 0, me, sibling, src=scr[a]) for a in range(n)]
        for cp in first + mine:
            cp.start()
        passed = []
        for j, chip in enumerate(chips):
            for a in range(n):
                copy(a, 1 + j, (*chip, c), me).wait_recv()
                cp = copy(a, 4 + j, (*chip, c), sibling)
                cp.start()
                passed.append(cp)
        for a in range(n):
            copy(a, 0, sibling, me).wait_recv()
        for j, chip in enumerate(chips):
            for a in range(n):
                copy(a, 4 + j, (*chip, 1 - c), me).wait_recv()
        for cp in first + passed:
            cp.wait_send()
        for cp in mine:
            cp.wait()

    return pl.pallas_call(
        body, name="weights_all_gather",
        in_specs=[pl.BlockSpec(memory_space=pltpu.VMEM)] * n,
        out_specs=[pl.BlockSpec(memory_space=pl.ANY)] * n,
        out_shape=[jax.ShapeDtypeStruct((N_DEV,) + s.shape, BF16) for s in shards],
        scratch_shapes=[pltpu.VMEM(s.shape, BF16) for s in shards]
        + [pltpu.SemaphoreType.DMA((n, 7)), pltpu.SemaphoreType.DMA((n, 7)), pltpu.SemaphoreType.DMA((n,))],
        compiler_params=pltpu.CompilerParams(vmem_limit_bytes=VMEM_LIMIT),
    )(*shards)


def _pack_layout(D, n_rel_rows):
    n_bin = -(-D_IN // D)
    row_bin = len(GAINS)
    row_sink = row_bin + n_bin
    row_loss = row_sink + 1
    row_rb = -(-(row_loss + 1) // 8) * 8
    n_rows = row_rb + -(-n_rel_rows // 8) * 8
    bin_parts = [(r, min(D, D_IN - r * D)) for r in range(n_bin)]
    return row_bin, row_sink, row_loss, row_rb, n_rows, bin_parts


def _final_exchange(grad_blocks, partials, loss):
    D = partials["ffn1_pre_g"].shape[1]
    rb_shape = partials["rel_bias"].shape
    row_bin, row_sink, row_loss, row_rb, n_rows, bin_parts = _pack_layout(D, rb_shape[0])
    n_small = len(SMALL)

    def body(*refs):
        g_in = refs[0]
        part = dict(zip(SMALL, refs[1:1 + n_small]))
        loss_ref = refs[1 + n_small]
        landed, gath, pack, send_sems, recv_sems, local_sems = refs[2 + n_small:]

        pack[...] = jnp.zeros_like(pack)
        for i, name in enumerate(GAINS):
            pack[i:i + 1, :] = part[name][...]
        for r, width in bin_parts:
            pack[row_bin + r:row_bin + r + 1, 0:width] = part["b_in"][:, r * D:r * D + width]
        pack[row_sink:row_sink + 1, 0:LANES] = part["sinks"][...]
        pack[row_loss:row_loss + 1, 0:LANES] = loss_ref[...]
        pack[row_rb:row_rb + rb_shape[0], 0:rb_shape[1]] = part["rel_bias"][...]

        small_start, small_wait = _side_copies("gather", [pack], [gath], send_sems, recv_sems, local_sems, sem_row=0)
        big_start, big_wait = _side_copies("exchange", [g_in], [landed], send_sems, recv_sems, local_sems, sem_row=1)
        small_start()
        big_start()
        small_wait()
        big_wait()

    args = [grad_blocks] + [partials[k] for k in SMALL] + [loss]
    vmem = pl.BlockSpec(memory_space=pltpu.VMEM)
    any_spec = pl.BlockSpec(memory_space=pl.ANY)
    return pl.pallas_call(
        body, name="final_exchange",
        in_specs=[any_spec] + [vmem] * (n_small + 1),
        out_specs=[any_spec, any_spec],
        out_shape=[jax.ShapeDtypeStruct(grad_blocks.shape, grad_blocks.dtype),
                   jax.ShapeDtypeStruct((N_DEV, n_rows, D), F32)],
        scratch_shapes=[pltpu.VMEM((n_rows, D), F32), pltpu.SemaphoreType.DMA((2, 7)),
                        pltpu.SemaphoreType.DMA((2, 7)), pltpu.SemaphoreType.DMA((2,))],
    )(*args)


def _adamw(w, g, m, v):
    m = ADAM_B1 * m + (1.0 - ADAM_B1) * g
    v = ADAM_B2 * v + (1.0 - ADAM_B2) * (g * g)
    m_hat = m / (1.0 - ADAM_B1 ** ADAM_STEP)
    v_hat = v / (1.0 - ADAM_B2 ** ADAM_STEP)
    return -ADAM_LR * (m_hat / (jnp.sqrt(v_hat) + ADAM_EPS) + ADAM_WD * w), m, v


def _sum_adamw(partials, w, m, v, rows, name):
    R, C = w.shape

    def body(p_ref, w_ref, m_ref, v_ref, g_ref, d_ref, nm_ref, nv_ref):
        g = p_ref[0].astype(F32)
        for k in range(1, N_DEV):
            g = g + p_ref[k].astype(F32)
        g_ref[...] = g
        d_ref[...], nm_ref[...], nv_ref[...] = _adamw(w_ref[...], g, m_ref[...], v_ref[...])

    tile = pl.BlockSpec((rows, C), lambda i: (i, 0))
    return pl.pallas_call(
        body, name=name, grid=(R // rows,),
        in_specs=[pl.BlockSpec((N_DEV, rows, C), lambda i: (0, i, 0)), tile, tile, tile],
        out_specs=[tile] * 4, out_shape=[jax.ShapeDtypeStruct((R, C), F32)] * 4,
        compiler_params=_params(1),
    )(partials, w, m, v)


def _small_adamw(gathered, ws, ms, vs):
    D = ws["ffn1_pre_g"].shape[1]
    n_sink = ws["sinks"].shape[1]
    rb_shape = ws["rel_bias"].shape
    row_bin, row_sink, row_loss, row_rb, n_rows, bin_parts = _pack_layout(D, rb_shape[0])
    n_small = len(SMALL)

    def body(*refs):
        gath = refs[0]
        pos = 1
        w_ref = dict(zip(SMALL, refs[pos:pos + n_small]))
        m_ref = dict(zip(SMALL, refs[pos + n_small:pos + 2 * n_small]))
        v_ref = dict(zip(SMALL, refs[pos + 2 * n_small:pos + 3 * n_small]))
        pos += 3 * n_small
        outs = {name: refs[pos + 4 * i:pos + 4 * i + 4] for i, name in enumerate(SMALL)}
        loss_out = refs[pos + 4 * n_small]
        pack = refs[pos + 4 * n_small + 1]

        total = gath[0]
        for k in range(1, N_DEV):
            total = total + gath[k]
        pack[...] = total

        def update(name, g):
            g_out, d_out, m_out, v_out = outs[name]
            g_out[...] = g
            d_out[...], m_out[...], v_out[...] = _adamw(w_ref[name][...], g, m_ref[name][...], v_ref[name][...])

        for i, name in enumerate(GAINS):
            update(name, pack[i:i + 1, :])
        update("b_in", jnp.concatenate([pack[row_bin + r:row_bin + r + 1, 0:width] for r, width in bin_parts], axis=1))
        update("sinks", pack[row_sink:row_sink + 1, 0:n_sink])
        update("rel_bias", pack[row_rb:row_rb + rb_shape[0], 0:rb_shape[1]])
        loss_out[...] = pack[row_loss:row_loss + 1, 0:LANES]

    args = [gathered]
    for group in (ws, ms, vs):
        args += [group[k] for k in SMALL]
    out_shape = []
    for name in SMALL:
        out_shape += [jax.ShapeDtypeStruct(ws[name].shape, F32)] * 4
    out_shape.append(jax.ShapeDtypeStruct((1, LANES), F32))
    res = pl.pallas_call(
        body, name="small_adamw",
        in_specs=[pl.BlockSpec(memory_space=pltpu.VMEM)] * len(args),
        out_specs=[pl.BlockSpec(memory_space=pltpu.VMEM)] * len(out_shape),
        out_shape=out_shape,
        scratch_shapes=[pltpu.VMEM((n_rows, D), F32)],
    )(*args)
    per_name = {name: res[4 * i:4 * i + 4] for i, name in enumerate(SMALL)}
    return per_name, res[-1]


COLUMN_SHARDED = ("ffn1_w_gu", "ffn2_w_gu", "w_in")


def _adamw_rows(rows_total):
    return max(r for r in range(16, min(rows_total, 256) + 1, 16) if rows_total % r == 0)


def kernel(x, p, rel_bias, ffn1_pre_g, ffn1_w_gu, ffn1_w_down, ffn1_post_g, attn_pre_g, w_in, b_in, sinks, w_out, b_out, attn_post_g, ffn2_pre_g, ffn2_w_gu, ffn2_w_down, ffn2_post_g, ple_pre_g, w_ple_gate, w_ple_proj, ple_post_g, loss_target, m_rel_bias, m_ffn1_pre_g, m_ffn1_w_gu, m_ffn1_w_down, m_ffn1_post_g, m_attn_pre_g, m_w_in, m_b_in, m_sinks, m_w_out, m_b_out, m_attn_post_g, m_ffn2_pre_g, m_ffn2_w_gu, m_ffn2_w_down, m_ffn2_post_g, m_ple_pre_g, m_w_ple_gate, m_w_ple_proj, m_ple_post_g, v_rel_bias, v_ffn1_pre_g, v_ffn1_w_gu, v_ffn1_w_down, v_ffn1_post_g, v_attn_pre_g, v_w_in, v_b_in, v_sinks, v_w_out, v_b_out, v_attn_post_g, v_ffn2_pre_g, v_ffn2_w_gu, v_ffn2_w_down, v_ffn2_post_g, v_ple_pre_g, v_w_ple_gate, v_w_ple_proj, v_ple_post_g):
    given = dict(locals())
    ws = {k: given[k] for k in WEIGHTS}
    ms = {k: given["m_" + k] for k in WEIGHTS}
    vs = {k: given["v_" + k] for k in WEIGHTS}

    def shard(t):
        return t.reshape(t.shape[1:])

    xs, ps, target = shard(x), shard(shard(p)), shard(loss_target)
    T, D = xs.shape
    small = {k: ws[k] for k in SMALL}

    def local(group, k):
        t = shard(group[k])
        return jnp.swapaxes(t, 0, 1) if k in COLUMN_SHARDED else t

    shards = {k: local(ws, k) for k in BIG}

    w_gu1, w_down1 = _all_gather_bf16([shards["ffn1_w_gu"], shards["ffn1_w_down"]])
    w_down1 = w_down1.reshape(-1, D)
    later = ("w_in", "w_out", "ffn2_w_gu", "ffn2_w_down", "w_ple_gate", "w_ple_proj")
    cast = dict(zip(later, _cast_bf16([shards[k] for k in later])))

    buckets_a = _bucket_tiles(PATTERNS_A)
    buckets_b = _bucket_tiles(PATTERNS_B)
    bias_a = _bias_build(small["rel_bias"], buckets_a, 0, "bias_build_a")
    bias_b = _bias_build(small["rel_bias"], buckets_b, N_HEAD_GROUP, "bias_build_b")
    a_cfg = dict(patterns=PATTERNS_A, qcol=Q_A_COL, kcol=K_A_COL, vcol=V_A_COL, shared_kv=True)
    b_cfg = dict(patterns=PATTERNS_B, qcol=Q_B_COL, kcol=K_B_COL, vcol=V_B_COL, shared_kv=False)

    (h1, f1, a1, gu1), (w_in_g, w_down2) = _ffn_fwd(
        xs, small["ffn1_pre_g"], small["ffn1_post_g"], w_gu1, w_down1, "ffn1_fwd",
        side=("gather", [cast["w_in"], cast["ffn2_w_down"]]))
    w_in_full = w_in_g.reshape(D_IN, D)
    w_down2 = w_down2.reshape(-1, D)
    (z, a2), (w_out_g,) = _inproj_fwd(h1, small["attn_pre_g"], w_in_full, small["b_in"],
                                      side=("gather", [cast["w_out"]]))
    w_out_full = w_out_g.reshape(-1, D)
    (mix_a, lse_a), (w_gate, w_proj) = _attn_fwd(
        z, bias_a, small["sinks"], name="attn_a_fwd", **a_cfg,
        side=("gather", [cast["w_ple_gate"], cast["w_ple_proj"]]))
    w_gate = w_gate.reshape(-1, D)
    (mix_b, lse_b), (w_gu2,) = _attn_fwd(
        z, bias_b, None, name="attn_b_fwd", **b_cfg, side=("gather", [cast["ffn2_w_gu"]]))
    att, h2, mix = _outproj_fwd(mix_a, mix_b, w_out_full, small["b_out"], small["attn_post_g"], h1)
    (h3, f2, a3, gu2), _ = _ffn_fwd(h2, small["ffn2_pre_g"], small["ffn2_post_g"], w_gu2, w_down2, "ffn2_fwd")
    a4, dpre, de, dh4, loss, dg_ple_post = _ple_fwd_loss(
        h3, small["ple_pre_g"], w_gate, ps, w_proj, small["ple_post_g"], target)

    dh3, dg_ple_pre = _ple_bwd(dpre, w_gate, h3, small["ple_pre_g"], dh4)
    d_gate = _dw_rows(a4, dpre, "ple_dw_gate", min(256, D))
    d_proj = _ple_dw_proj(ps, de, N_DEV)
    landed = {}
    (dh2, df2, hh2, dgu2, dg_f2_post, dg_f2_pre), (landed["w_ple_gate"], landed["w_ple_proj"]) = _ffn_bwd(
        dh3, f2, small["ffn2_post_g"], h2, small["ffn2_pre_g"], gu2, w_gu2, w_down2, "ffn2_bwd",
        side=("exchange", [d_gate, d_proj]))
    d_gu2 = _dw_gu(a3, dgu2, "ffn2_dw_gu")
    d_down2 = _dw_down(hh2, df2, "ffn2_dw_down").reshape(N_DEV, -1, D)
    dmix_a, dmix_b, datt, dg_attn_post, db_out = _outproj_bwd(dh2, att, small["attn_post_g"], w_out_full)
    d_out = _dw_rows(mix, datt, "attn_dw_out", 256)
    (dqa, dka, dva, ds_a, dsinks), (landed["ffn2_w_down"],) = _attn_bwd(
        z, bias_a, small["sinks"], dmix_a, mix_a, lse_a, name="attn_a_bwd", **a_cfg,
        side=("exchange", [d_down2]))
    (dqb, dkb, dvb, ds_b), (landed["ffn2_w_gu"],) = _attn_bwd(
        z, bias_b, None, dmix_b, mix_b, lse_b, name="attn_b_bwd", **b_cfg, side=("exchange", [d_gu2]))
    (dh1, dz, db_in, dg_attn_pre), (landed["w_out"],) = _inproj_bwd(
        dqa, dka, dva, dqb, dkb, dvb, w_in_full, h1, small["attn_pre_g"], dh2, side=("exchange", [d_out]))
    cols = D_IN // 3
    d_in = _tn_matmul(
        dz, a2, pl.BlockSpec((DW_TILE, cols), lambda b, t: (t, b)), _tok(D),
        jax.ShapeDtypeStruct((D_IN, D), BF16), pl.BlockSpec((cols, D), lambda b, t: (b, 0)),
        3, T // DW_TILE, (cols, D), "attn_dw_in").reshape(N_DEV, D_IN // N_DEV, D)
    (grad_x, df1, hh1, dgu1, dg_f1_post, dg_f1_pre), _ = _ffn_bwd(
        dh1, f1, small["ffn1_post_g"], xs, small["ffn1_pre_g"], gu1, w_gu1, w_down1, "ffn1_bwd")
    d_down1, (landed["w_in"],) = _dw_down(hh1, df1, "ffn1_dw_down", side=("exchange", [d_in]))
    d_down1 = d_down1.reshape(N_DEV, -1, D)
    d_gu1, (landed["ffn1_w_down"],) = _dw_gu(a1, dgu1, "ffn1_dw_gu", side=("exchange", [d_down1]))

    rb_a = _bias_grad(ds_a, buckets_a, "bias_grad_a")
    rb_b = _bias_grad(ds_b, buckets_b, "bias_grad_b").reshape(len(PATTERNS_B), N_HEAD_GROUP, NUM_BUCKETS)
    d_rel_bias = jnp.concatenate([rb_a.T, jnp.sum(rb_b, axis=0).T], axis=1)
    small_grads = {"ffn1_pre_g": dg_f1_pre, "ffn1_post_g": dg_f1_post, "attn_pre_g": dg_attn_pre,
                   "attn_post_g": dg_attn_post, "ffn2_pre_g": dg_f2_pre, "ffn2_post_g": dg_f2_post,
                   "ple_pre_g": dg_ple_pre, "ple_post_g": dg_ple_post, "b_out": db_out, "b_in": db_in,
                   "sinks": dsinks, "rel_bias": d_rel_bias}
    landed["ffn1_w_gu"], small_gathered = _final_exchange(d_gu1, small_grads, loss)

    result = {}
    for k in BIG:
        outs = _sum_adamw(landed[k], shards[k], local(ms, k), local(vs, k), _adamw_rows(shards[k].shape[0]),
                          k + "_adamw")
        if k in COLUMN_SHARDED:
            outs = [jnp.swapaxes(o, 0, 1) for o in outs]
        result[k] = [o.reshape(ws[k].shape) for o in outs]
    small_res, loss_all = _small_adamw(
        small_gathered, small, {k: ms[k] for k in SMALL}, {k: vs[k] for k in SMALL})
    result.update(small_res)

    out = [loss_all[0, 0], grad_x.reshape(x.shape)]
    for i in range(4):
        out += [result[k][i] for k in WEIGHTS]
    return tuple(out)
```

```python
import functools
import math

import numpy as np
import jax
import jax.numpy as jnp
from jax import lax
from jax.experimental import pallas as pl
from jax.experimental.pallas import tpu as pltpu

F32 = jnp.float32
BF16 = jnp.bfloat16
MESH = pl.DeviceIdType.MESH

N_DEV = 8
EPS = 1e-6
NEG_INF = -1e30
HEAD_DIM = 64
LANES = 128
QBLK = 128
D_IN = 2304
A_Q, A_KV, B_W = 512, 128, 512
N_HEAD_GROUP = 8
NUM_BUCKETS = 32
MAX_DISTANCE = 2048
PATTERNS_A = ((1, 127),)
PATTERNS_B = ((1, 128), (4, 128), (16, 128))
Q_A_COL, K_A_COL, V_A_COL = 0, 4, 5
Q_B_COL, K_B_COL, V_B_COL = 6, 10, 14

ADAM_LR, ADAM_B1, ADAM_B2, ADAM_EPS, ADAM_WD, ADAM_STEP = 0.001, 0.9, 0.999, 1e-08, 0.01, 10

TOKEN_TILE = 512
DW_TILE = 1024
FWD_BLOCKS = 4
BWD_BLOCKS = 2
VMEM_LIMIT = 56 * 1024 * 1024
ARB = "arbitrary"

BIG = ("ffn1_w_gu", "ffn1_w_down", "w_in", "w_out", "ffn2_w_gu", "ffn2_w_down", "w_ple_gate", "w_ple_proj")
GAINS = ("ffn1_pre_g", "ffn1_post_g", "attn_pre_g", "attn_post_g", "ffn2_pre_g", "ffn2_post_g",
         "ple_pre_g", "ple_post_g", "b_out")
SMALL = GAINS + ("b_in", "sinks", "rel_bias")
WEIGHTS = ("rel_bias", "ffn1_pre_g", "ffn1_w_gu", "ffn1_w_down", "ffn1_post_g", "attn_pre_g", "w_in", "b_in",
           "sinks", "w_out", "b_out", "attn_post_g", "ffn2_pre_g", "ffn2_w_gu", "ffn2_w_down", "ffn2_post_g",
           "ple_pre_g", "w_ple_gate", "w_ple_proj", "ple_post_g")


def _params(n_axes):
    return pltpu.CompilerParams(dimension_semantics=(ARB,) * n_axes, vmem_limit_bytes=VMEM_LIMIT)


def _mm(a, b):
    return jnp.dot(a, b, preferred_element_type=F32)


def _mm_nt(a, b):
    return lax.dot_general(a, b, (((1,), (1,)), ((), ())), preferred_element_type=F32)


def _mm_tn(a, b):
    return lax.dot_general(a, b, (((0,), (0,)), ((), ())), preferred_element_type=F32)


def _rstd(x):
    return lax.rsqrt(jnp.mean(x * x, axis=-1, keepdims=True) + EPS)


def _rms_bwd(x, r, gain, dy):
    n = x * r
    gdy = dy * gain
    return r * (gdy - n * jnp.mean(gdy * n, axis=-1, keepdims=True)), dy * n


def _colsum(v):
    return jnp.sum(v, axis=0, keepdims=True)


def _full(shape):
    return pl.BlockSpec(shape, lambda *_: (0,) * len(shape))


def _mesh_place():
    return lax.axis_index("x"), lax.axis_index("y"), lax.axis_index("c")


def _slot(dev):
    return 4 * dev[0] + 2 * dev[1] + dev[2]


def _peers(x, y, c):
    out = []
    for flip in range(1, N_DEV):
        dx, dy, dc = (flip >> 2) & 1, (flip >> 1) & 1, flip & 1
        out.append((1 - x if dx else x, 1 - y if dy else y, 1 - c if dc else c))
    return out


def _side_copies(kind, ins, outs, send_sems, recv_sems, local_sems, sem_row=0):
    n = len(ins)
    x, y, c = _mesh_place()
    me = _slot((x, y, c))
    peers = _peers(x, y, c)

    def src(a, block):
        return ins[a] if kind == "gather" else ins[a].at[block]

    def send(a, k, peer):
        return pltpu.make_async_remote_copy(
            src_ref=src(a, _slot(peer)), dst_ref=outs[a].at[me],
            send_sem=send_sems.at[sem_row + a, k], recv_sem=recv_sems.at[sem_row + a, k],
            device_id=peer, device_id_type=MESH)

    def arrival(a, k, peer):
        return pltpu.make_async_remote_copy(
            src_ref=src(a, _slot(peer)), dst_ref=outs[a].at[_slot(peer)],
            send_sem=send_sems.at[sem_row + a, k], recv_sem=recv_sems.at[sem_row + a, k],
            device_id=peer, device_id_type=MESH)

    def own(a):
        return pltpu.make_async_copy(src(a, me), outs[a].at[me], local_sems.at[sem_row + a])

    def start():
        for k, peer in enumerate(peers):
            for a in range(n):
                send(a, k, peer).start()
        for a in range(n):
            own(a).start()

    def wait():
        for k, peer in enumerate(peers):
            for a in range(n):
                arrival(a, k, peer).wait_recv()
        for k, peer in enumerate(peers):
            for a in range(n):
                send(a, k, peer).wait_send()
        for a in range(n):
            own(a).wait()

    return start, None, wait


def _relay_gather(ins, outs, send_sems, recv_sems, local_sems):
    n = len(ins)
    x, y, c = _mesh_place()
    me, sibling = (x, y, c), (x, y, 1 - c)
    chips = [(1 - x, y), (x, 1 - y), (1 - x, 1 - y)]

    def copy(a, k, block, to, src=None):
        dst = outs[a].at[_slot(block)]
        return pltpu.make_async_remote_copy(
            src_ref=dst if src is None else src, dst_ref=dst,
            send_sem=send_sems.at[a, k], recv_sem=recv_sems.at[a, k], device_id=to, device_id_type=MESH)

    def own(a):
        return pltpu.make_async_copy(ins[a], outs[a].at[_slot(me)], local_sems.at[a])

    def start():
        for j, chip in enumerate(chips):
            for a in range(n):
                copy(a, 1 + j, me, (*chip, c), src=ins[a]).start()
        for a in range(n):
            copy(a, 0, me, sibling, src=ins[a]).start()
            own(a).start()

    def relay():
        for j, chip in enumerate(chips):
            for a in range(n):
                copy(a, 1 + j, (*chip, c), me).wait_recv()
                copy(a, 4 + j, (*chip, c), sibling).start()

    def wait():
        for a in range(n):
            copy(a, 0, sibling, me).wait_recv()
        for j, chip in enumerate(chips):
            for a in range(n):
                copy(a, 4 + j, (*chip, 1 - c), me).wait_recv()
        for j, chip in enumerate(chips):
            for a in range(n):
                copy(a, 1 + j, me, (*chip, c), src=ins[a]).wait_send()
                copy(a, 4 + j, (*chip, c), sibling).wait_send()
        for a in range(n):
            copy(a, 0, me, sibling, src=ins[a]).wait_send()
            own(a).wait()

    return start, relay, wait


def _side_out_shapes(kind, arrays):
    if kind in ("gather", "relay_gather"):
        return [jax.ShapeDtypeStruct((N_DEV,) + a.shape, a.dtype) for a in arrays]
    return [jax.ShapeDtypeStruct(a.shape, a.dtype) for a in arrays]


def _hosted_call(body, name, grid, in_specs, out_specs, out_shape, scratch_shapes, args, side=None):
    if side is None:
        outs = pl.pallas_call(
            body, name=name, grid=grid, in_specs=in_specs, out_specs=out_specs, out_shape=out_shape,
            scratch_shapes=scratch_shapes, compiler_params=_params(len(grid)))(*args)
        return outs, []
    kind, arrays = side
    n_in, n_out, n_scr, n_side = len(in_specs), len(out_specs), len(scratch_shapes), len(arrays)

    def hosted(*refs):
        pos = 0
        groups = []
        for size in (n_in, n_side, n_out, n_side, n_scr):
            groups.append(refs[pos:pos + size])
            pos += size
        ins, side_in, outs, side_out, scr = groups
        send_sems, recv_sems, local_sems = refs[pos:]
        ids = [pl.program_id(d) for d in range(len(grid))]
        is_first = functools.reduce(jnp.logical_and, [i == 0 for i in ids])
        is_last = functools.reduce(jnp.logical_and, [i == g - 1 for i, g in zip(ids, grid)])
        if kind == "relay_gather":
            start, relay, wait = _relay_gather(side_in, side_out, send_sems, recv_sems, local_sems)
        else:
            start, relay, wait = _side_copies(kind, side_in, side_out, send_sems, recv_sems, local_sems)
        pl.when(is_first)(start)
        if relay is not None:
            pl.when(is_last)(relay)
        body(*ins, *outs, *scr)
        pl.when(is_last)(wait)

    any_spec = pl.BlockSpec(memory_space=pl.ANY)
    outs = pl.pallas_call(
        hosted, name=name, grid=grid,
        in_specs=list(in_specs) + [any_spec] * n_side,
        out_specs=list(out_specs) + [any_spec] * n_side,
        out_shape=list(out_shape) + _side_out_shapes(kind, arrays),
        scratch_shapes=list(scratch_shapes) + [pltpu.SemaphoreType.DMA((n_side, 7)), pltpu.SemaphoreType.DMA((n_side, 7)),
                                               pltpu.SemaphoreType.DMA((n_side,))],
        compiler_params=_params(len(grid)))(*args, *arrays)
    return outs[:n_out], outs[n_out:]


def _lane_chunks(width, chunk=2 * LANES):
    return [slice(n0, min(n0 + chunk, width)) for n0 in range(0, width, chunk)]


def _pipelined(chunks, first, middle, last):
    n = len(chunks)
    a, b, total = {}, {}, None
    for step in range(n + 2):
        if step < n:
            a[step] = first(chunks[step])
        if 0 <= step - 1 < n:
            b[step - 1] = middle(chunks[step - 1], a.pop(step - 1))
        if 0 <= step - 2 < n:
            part = last(chunks[step - 2], b.pop(step - 2))
            total = part if total is None else total + part
    return total


def _ffn_fwd(h, g_pre, g_post, w_gu, w_down, name, side=None):
    T, D = h.shape
    nj = w_gu.shape[0] // 2
    FB = w_gu.shape[1]
    tm = TOKEN_TILE

    def body(h_ref, gpre_ref, gpost_ref, wg_ref, wu_ref, wd_ref, hout_ref, f_ref, a_ref, gu_ref, a_scr, acc):
        j = pl.program_id(1)

        @pl.when(j == 0)
        def _():
            x = h_ref[...]
            a = (x * _rstd(x) * gpre_ref[...]).astype(BF16)
            a_scr[...] = a
            a_ref[...] = a
            acc[...] = jnp.zeros_like(acc)

        a = a_scr[...]
        g = _mm_nt(a, wg_ref[...])
        u = _mm_nt(a, wu_ref[...])
        gu_ref[0] = g.astype(BF16)
        gu_ref[1] = u.astype(BF16)
        hh = (g * jax.nn.sigmoid(g) * u).astype(BF16)
        acc[...] += _mm(hh, wd_ref[...])

        @pl.when(j == nj - 1)
        def _():
            f = acc[...]
            f_ref[...] = f
            hout_ref[...] = h_ref[...] + 0.5 * (f * _rstd(f) * gpost_ref[...])

    return _hosted_call(
        body, name, (T // tm, nj),
        in_specs=[
            pl.BlockSpec((tm, D), lambda i, j: (i, 0)),
            _full((1, D)), _full((1, D)),
            pl.BlockSpec((None, FB, D), lambda i, j: (j, 0, 0)),
            pl.BlockSpec((None, FB, D), lambda i, j: (j + nj, 0, 0)),
            pl.BlockSpec((FB, D), lambda i, j: (j, 0)),
        ],
        out_specs=[
            pl.BlockSpec((tm, D), lambda i, j: (i, 0)),
            pl.BlockSpec((tm, D), lambda i, j: (i, 0)),
            pl.BlockSpec((tm, D), lambda i, j: (i, 0)),
            pl.BlockSpec((None, 2, tm, FB), lambda i, j: (j, 0, i, 0)),
        ],
        out_shape=[
            jax.ShapeDtypeStruct((T, D), F32),
            jax.ShapeDtypeStruct((T, D), F32),
            jax.ShapeDtypeStruct((T, D), BF16),
            jax.ShapeDtypeStruct((nj, 2, T, FB), BF16),
        ],
        scratch_shapes=[pltpu.VMEM((tm, D), BF16), pltpu.VMEM((tm, D), F32)],
        args=(h, g_pre, g_post, w_gu, w_gu, w_down), side=side)


def _ffn_bwd(dh_out, f, g_post, h, g_pre, gu, w_gu, w_down, name, side=None):
    T, D = h.shape
    nj = w_gu.shape[0] // 2
    FB = w_gu.shape[1]
    tm = TOKEN_TILE

    def body(dho_ref, f_ref, gpost_ref, h_ref, gpre_ref, gu_ref, wg_ref, wu_ref, wd_ref,
             dhin_ref, df_ref, hh_ref, dgu_ref, dgpost_ref, dgpre_ref, df_scr, da):
        i, j = pl.program_id(0), pl.program_id(1)

        @pl.when(jnp.logical_and(i == 0, j == 0))
        def _():
            dgpost_ref[...] = jnp.zeros_like(dgpost_ref)
            dgpre_ref[...] = jnp.zeros_like(dgpre_ref)

        @pl.when(j == 0)
        def _():
            fv = f_ref[...]
            df, dgain = _rms_bwd(fv, _rstd(fv), gpost_ref[...], 0.5 * dho_ref[...])
            dgpost_ref[...] += _colsum(dgain)
            dfb = df.astype(BF16)
            df_scr[...] = dfb
            df_ref[...] = dfb
            da[...] = jnp.zeros_like(da)

        dfb = df_scr[...]

        def hidden_grad(c):
            return _mm_nt(dfb, wd_ref[c, :])

        def through_swiglu(c, dhh):
            g = gu_ref[0, :, c].astype(F32)
            u = gu_ref[1, :, c].astype(F32)
            sg = jax.nn.sigmoid(g)
            silu = g * sg
            hh_ref[:, c] = (silu * u).astype(BF16)
            dg = (dhh * u * (sg * (1.0 + (g - silu)))).astype(BF16)
            du = (dhh * silu).astype(BF16)
            dgu_ref[0, :, c] = dg
            dgu_ref[1, :, c] = du
            return dg, du

        def input_grad(c, dgu):
            return _mm(dgu[0], wg_ref[c, :]) + _mm(dgu[1], wu_ref[c, :])

        da[...] += _pipelined(_lane_chunks(FB), hidden_grad, through_swiglu, input_grad)

        @pl.when(j == nj - 1)
        def _():
            x = h_ref[...]
            dx, dgain = _rms_bwd(x, _rstd(x), gpre_ref[...], da[...])
            dgpre_ref[...] += _colsum(dgain)
            dhin_ref[...] = dho_ref[...] + dx

    tile = pl.BlockSpec((tm, D), lambda i, j: (i, 0))
    return _hosted_call(
        body, name, (T // tm, nj),
        in_specs=[
            tile, tile, _full((1, D)), tile, _full((1, D)),
            pl.BlockSpec((None, 2, tm, FB), lambda i, j: (j, 0, i, 0)),
            pl.BlockSpec((None, FB, D), lambda i, j: (j, 0, 0)),
            pl.BlockSpec((None, FB, D), lambda i, j: (j + nj, 0, 0)),
            pl.BlockSpec((FB, D), lambda i, j: (j, 0)),
        ],
        out_specs=[
            tile, tile,
            pl.BlockSpec((None, tm, FB), lambda i, j: (j, i, 0)),
            pl.BlockSpec((None, 2, tm, FB), lambda i, j: (j, 0, i, 0)),
            _full((1, D)), _full((1, D)),
        ],
        out_shape=[
            jax.ShapeDtypeStruct((T, D), F32),
            jax.ShapeDtypeStruct((T, D), BF16),
            jax.ShapeDtypeStruct((nj, T, FB), BF16),
            jax.ShapeDtypeStruct((nj, 2, T, FB), BF16),
            jax.ShapeDtypeStruct((1, D), F32),
            jax.ShapeDtypeStruct((1, D), F32),
        ],
        scratch_shapes=[pltpu.VMEM((tm, D), BF16), pltpu.VMEM((tm, D), F32)],
        args=(dh_out, f, g_post, h, g_pre, gu, w_gu, w_gu, w_down), side=side)


def _tn_matmul(x, y, x_spec, y_spec, out_shape, out_spec, n_blocks, n_steps, acc_shape, name, side=None):
    def body(x_ref, y_ref, o_ref, acc):
        t = pl.program_id(1)

        @pl.when(t == 0)
        def _():
            acc[...] = jnp.zeros_like(acc)

        acc[...] += _mm_tn(x_ref[...].astype(BF16), y_ref[...].astype(BF16))

        @pl.when(t == n_steps - 1)
        def _():
            o_ref[...] = acc[...].astype(o_ref.dtype)

    outs, side_outs = _hosted_call(
        body, name, (n_blocks, n_steps), in_specs=[x_spec, y_spec], out_specs=[out_spec], out_shape=[out_shape],
        scratch_shapes=[pltpu.VMEM(acc_shape, F32)], args=(x, y), side=side)
    return (outs[0], side_outs) if side is not None else outs[0]


def _inproj_fwd(h, g_pre, w_in, b_in, side=None):
    T, D = h.shape
    tm = TOKEN_TILE

    def body(h_ref, g_ref, w_ref, b_ref, z_ref, a_ref):
        x = h_ref[...]
        a = (x * _rstd(x) * g_ref[...]).astype(BF16)
        a_ref[...] = a
        z_ref[...] = _mm_nt(a, w_ref[...]) + b_ref[...]

    return _hosted_call(
        body, "inproj_fwd", (T // tm,),
        in_specs=[pl.BlockSpec((tm, D), lambda i: (i, 0)), _full((1, D)), _full((D_IN, D)), _full((1, D_IN))],
        out_specs=[pl.BlockSpec((tm, D_IN), lambda i: (i, 0)), pl.BlockSpec((tm, D), lambda i: (i, 0))],
        out_shape=[jax.ShapeDtypeStruct((T, D_IN), F32), jax.ShapeDtypeStruct((T, D), BF16)],
        scratch_shapes=[], args=(h, g_pre, w_in, b_in), side=side)


def _inproj_bwd(dqa, dka, dva, dqb, dkb, dvb, w_in, h, g_pre, dres, side=None):
    T, D = h.shape
    tm = TOKEN_TILE

    def body(dqa_ref, dka_ref, dva_ref, dqb_ref, dkb_ref, dvb_ref, w_ref, h_ref, g_ref, dres_ref,
             dh_ref, dz_ref, dbin_ref, dg_ref):
        i = pl.program_id(0)

        @pl.when(i == 0)
        def _():
            dbin_ref[...] = jnp.zeros_like(dbin_ref)
            dg_ref[...] = jnp.zeros_like(dg_ref)

        dz = jnp.concatenate([dqa_ref[...], dka_ref[...], dva_ref[...], dqb_ref[...], dkb_ref[...], dvb_ref[...]],
                             axis=1)
        dbin_ref[...] += _colsum(dz)
        dzb = dz.astype(BF16)
        dz_ref[...] = dzb
        da = _mm(dzb, w_ref[...])
        x = h_ref[...]
        dx, dgain = _rms_bwd(x, _rstd(x), g_ref[...], da)
        dg_ref[...] += _colsum(dgain)
        dh_ref[...] = dres_ref[...] + dx

    def tile(w):
        return pl.BlockSpec((tm, w), lambda i: (i, 0))

    return _hosted_call(
        body, "inproj_bwd", (T // tm,),
        in_specs=[tile(A_Q), tile(A_KV), tile(A_KV), tile(B_W), tile(B_W), tile(B_W),
                  _full((D_IN, D)), tile(D), _full((1, D)), tile(D)],
        out_specs=[tile(D), tile(D_IN), _full((1, D_IN)), _full((1, D))],
        out_shape=[jax.ShapeDtypeStruct((T, D), F32), jax.ShapeDtypeStruct((T, D_IN), BF16),
                   jax.ShapeDtypeStruct((1, D_IN), F32), jax.ShapeDtypeStruct((1, D), F32)],
        scratch_shapes=[], args=(dqa, dka, dva, dqb, dkb, dvb, w_in, h, g_pre, dres), side=side)


def _bucket_tiles(patterns):
    i = np.arange(QBLK)[:, None]
    j = np.arange(2 * QBLK)[None, :]
    dist = QBLK + i - j
    max_exact = NUM_BUCKETS // 2
    tiles = []
    for dilation, max_dist in patterns:
        n = np.maximum(dist * dilation, 0)
        nf = np.maximum(n, 1).astype(np.float32)
        large = max_exact + (np.log(nf / np.float32(max_exact)) / np.float32(math.log(MAX_DISTANCE / max_exact))
                             * np.float32(NUM_BUCKETS - max_exact)).astype(np.int32)
        bucket = np.where(n < max_exact, n, np.minimum(large, NUM_BUCKETS - 1))
        tiles.append(np.where((dist >= 0) & (dist <= max_dist), bucket, -1))
    return jnp.asarray(np.stack(tiles).astype(np.int32))


def _bias_build(rel_bias, buckets, head0, name):
    n = buckets.shape[0]

    def body(bk_ref, rb_ref, o_ref):
        bk = bk_ref[...]
        base = jnp.where(bk < 0, NEG_INF, 0.0).astype(F32)
        for hd in range(N_HEAD_GROUP):
            o_ref[hd] = lax.fori_loop(
                0, NUM_BUCKETS, lambda b, acc, hd=hd: jnp.where(bk == b, rb_ref[b, head0 + hd], acc), base)

    return pl.pallas_call(
        body, name=name, grid=(n,),
        in_specs=[pl.BlockSpec((None, QBLK, 2 * QBLK), lambda p: (p, 0, 0)), pl.BlockSpec(memory_space=pltpu.SMEM)],
        out_specs=pl.BlockSpec((None, N_HEAD_GROUP, QBLK, 2 * QBLK), lambda p: (p, 0, 0, 0)),
        out_shape=jax.ShapeDtypeStruct((n, N_HEAD_GROUP, QBLK, 2 * QBLK), F32),
        compiler_params=_params(1),
    )(buckets, rel_bias)


def _bias_grad(ds, buckets, name):
    n = buckets.shape[0]

    def body(ds_ref, bk_ref, o_ref):
        bk = bk_ref[...]
        row = lax.broadcasted_iota(jnp.int32, (NUM_BUCKETS, 2 * QBLK), 0)
        for hd in range(N_HEAD_GROUP):
            d = ds_ref[hd]
            per_key = jnp.zeros((NUM_BUCKETS, 2 * QBLK), F32)
            for b in range(NUM_BUCKETS):
                per_key = jnp.where(row == b, jnp.sum(jnp.where(bk == b, d, 0.0), axis=0, keepdims=True), per_key)
            o_ref[hd] = jnp.broadcast_to(jnp.sum(per_key, axis=1, keepdims=True), (NUM_BUCKETS, LANES))

    out = pl.pallas_call(
        body, name=name, grid=(n,),
        in_specs=[pl.BlockSpec((None, N_HEAD_GROUP, QBLK, 2 * QBLK), lambda p: (p, 0, 0, 0)),
                  pl.BlockSpec((None, QBLK, 2 * QBLK), lambda p: (p, 0, 0))],
        out_specs=pl.BlockSpec((None, N_HEAD_GROUP, NUM_BUCKETS, LANES), lambda p: (p, 0, 0, 0)),
        out_shape=jax.ShapeDtypeStruct((n, N_HEAD_GROUP, NUM_BUCKETS, LANES), F32),
        compiler_params=_params(1),
    )(ds, buckets)
    return out[:, :, :, 0].reshape(n * N_HEAD_GROUP, NUM_BUCKETS)


def _class_rows(start, dilation):
    if dilation == 1:
        return pl.ds(pl.multiple_of(start, QBLK), QBLK)
    return pl.ds(start, QBLK, stride=dilation)


def _block_starts(idx, n_blocks, dilation):
    cls = idx // n_blocks
    n = idx % n_blocks
    cur = cls + dilation * QBLK * n
    prev = cls + dilation * QBLK * jnp.maximum(n - 1, 0)
    return n, cur, prev


class _HeadPair:
    def __init__(self, g, shared_kv):
        self.lane = lax.broadcasted_iota(jnp.int32, (1, LANES), 1)
        self.lower = self.lane < HEAD_DIM
        self.shared_kv = shared_kv
        self.key_lanes = (self.lane >= HEAD_DIM).astype(jnp.int32) == (g // 2)

    def stack(self, t):
        return jnp.concatenate([jnp.where(self.lower, t, 0.0), jnp.where(self.lower, 0.0, t)], axis=0).astype(BF16)

    def unstack(self, t2):
        return jnp.where(self.lower, t2[:QBLK], t2[QBLK:])

    def keys(self, t):
        if self.shared_kv:
            return jnp.where(self.key_lanes, t, pltpu.roll(t, HEAD_DIM, 1))
        return t

    def key_grads(self, t):
        if self.shared_kv:
            return jnp.where(self.key_lanes, t + pltpu.roll(t, HEAD_DIM, 1), 0.0)
        return t


def _attn_specs(T, qcol, kcol, vcol, shared_kv):
    kv = (lambda c: (lambda g: (0, c))) if shared_kv else (lambda c: (lambda g: (0, c + g)))
    return [pl.BlockSpec((T, LANES), lambda g: (0, qcol + g)),
            pl.BlockSpec((T, LANES), kv(kcol)),
            pl.BlockSpec((T, LANES), kv(vcol))]


def _attn_fwd(z, bias, sinks, patterns, qcol, kcol, vcol, shared_kv, name, side=None):
    T = z.shape[0]
    n_pat = len(patterns)
    has_sink = sinks is not None

    def body(*refs):
        if has_sink:
            sink_ref, refs = refs[0], refs[1:]
        q_ref, k_ref, v_ref, b_ref, o_ref, l_ref = refs[:6]
        po_scr = refs[6:6 + n_pat]
        pl_scr = refs[6 + n_pat:]
        g = pl.program_id(0)
        heads = _HeadPair(g, shared_kv)
        in_prev = lax.broadcasted_iota(jnp.int32, (2 * QBLK, 2 * QBLK), 1) < QBLK

        for pi, (dilation, _) in enumerate(patterns):
            n_blocks = T // (QBLK * dilation)

            def step(it, carry, pi=pi, dilation=dilation, n_blocks=n_blocks):
                blocks = []
                for u in range(FWD_BLOCKS):
                    n, cur, prev = _block_starts(it * FWD_BLOCKS + u, n_blocks, dilation)
                    rows_c, rows_p = _class_rows(cur, dilation), _class_rows(prev, dilation)
                    qm = heads.stack(q_ref[rows_c, :])
                    k2 = heads.keys(jnp.concatenate([k_ref[rows_p, :], k_ref[rows_c, :]], axis=0)).astype(BF16)
                    v2 = heads.keys(jnp.concatenate([v_ref[rows_p, :], v_ref[rows_c, :]], axis=0)).astype(BF16)
                    blocks.append(dict(n=n, rows=rows_c, v2=v2, s=_mm_nt(qm, k2)))
                for b in blocks:
                    s = b["s"] * (HEAD_DIM ** -0.5) + b_ref[pi]
                    b["s"] = jnp.where(jnp.logical_and(in_prev, b["n"] == 0), NEG_INF, s)
                    b["m"] = jnp.max(b["s"], axis=1, keepdims=True)
                for b in blocks:
                    b["pr"] = jnp.exp(b["s"] - b["m"])
                    b["den"] = jnp.sum(b["pr"], axis=1, keepdims=True)
                for b in blocks:
                    b["o2"] = _mm(b["pr"].astype(BF16), b["v2"])
                for b in blocks:
                    lse = b["m"] + jnp.log(b["den"])
                    po_scr[pi][b["rows"], :] = heads.unstack(b["o2"] / b["den"])
                    pl_scr[2 * pi][b["rows"], :] = jnp.broadcast_to(lse[:QBLK], (QBLK, LANES))
                    pl_scr[2 * pi + 1][b["rows"], :] = jnp.broadcast_to(lse[QBLK:], (QBLK, LANES))
                return carry

            lax.fori_loop(0, (dilation * n_blocks) // FWD_BLOCKS, step, 0)

        def merge(ci, carry):
            rows = pl.ds(pl.multiple_of(ci * QBLK, QBLK), QBLK)
            weights = []
            for hd in range(2):
                parts = [pl_scr[2 * pi + hd][rows, :] for pi in range(n_pat)]
                m = functools.reduce(jnp.maximum, parts)
                if has_sink:
                    sink = sink_ref[0, 2 * g + hd]
                    m = jnp.maximum(m, sink)
                den = functools.reduce(jnp.add, [jnp.exp(x - m) for x in parts])
                if has_sink:
                    den = den + jnp.exp(sink - m)
                lse = m + jnp.log(den)
                l_ref[hd, rows, :] = lse
                weights.append([jnp.exp(x - lse) for x in parts])
            o_ref[rows, :] = functools.reduce(
                jnp.add, [jnp.where(heads.lower, weights[0][pi], weights[1][pi]) * po_scr[pi][rows, :]
                          for pi in range(n_pat)])
            return carry

        lax.fori_loop(0, T // QBLK, merge, 0)

    in_specs = _attn_specs(T, qcol, kcol, vcol, shared_kv)
    in_specs.append(pl.BlockSpec((n_pat, None, 2 * QBLK, 2 * QBLK), lambda g: (0, g, 0, 0)))
    args = [z, z, z, bias.reshape(n_pat, N_HEAD_GROUP // 2, 2 * QBLK, 2 * QBLK)]
    if has_sink:
        in_specs.insert(0, pl.BlockSpec(memory_space=pltpu.SMEM))
        args.insert(0, sinks)
    return _hosted_call(
        body, name, (N_HEAD_GROUP // 2,),
        in_specs=in_specs,
        out_specs=[pl.BlockSpec((T, LANES), lambda g: (0, g)), pl.BlockSpec((2, T, LANES), lambda g: (g, 0, 0))],
        out_shape=[jax.ShapeDtypeStruct((T, N_HEAD_GROUP * HEAD_DIM), F32),
                   jax.ShapeDtypeStruct((N_HEAD_GROUP, T, LANES), F32)],
        scratch_shapes=[pltpu.VMEM((T, LANES), F32)] * (3 * n_pat), args=args, side=side)


def _attn_bwd(z, bias, sinks, d_out, out, lse, patterns, qcol, kcol, vcol, shared_kv, name, side=None):
    T = z.shape[0]
    n_pat = len(patterns)
    has_sink = sinks is not None
    kv_width = LANES if shared_kv else N_HEAD_GROUP * HEAD_DIM

    def body(*refs):
        if has_sink:
            sink_ref, refs = refs[0], refs[1:]
        q_ref, k_ref, v_ref, b_ref, do_ref, o_ref, l0_ref, l1_ref = refs[:8]
        dq_ref, dk_ref, dv_ref, ds_ref = refs[8:12]
        dsink_ref = refs[12] if has_sink else None
        dk_acc, dv_acc = refs[-2:]
        g = pl.program_id(0)
        heads = _HeadPair(g, shared_kv)
        in_prev = lax.broadcasted_iota(jnp.int32, (2 * QBLK, 2 * QBLK), 1) < QBLK

        dq_ref[...] = jnp.zeros_like(dq_ref)
        ds_ref[...] = jnp.zeros_like(ds_ref)
        dk_acc[...] = jnp.zeros_like(dk_acc)
        dv_acc[...] = jnp.zeros_like(dv_acc)

        dsink = jnp.zeros((1, LANES), F32)
        for pi, (dilation, _) in enumerate(patterns):
            n_blocks = T // (QBLK * dilation)

            def step(idx, dsink, pi=pi, dilation=dilation, n_blocks=n_blocks):
                blocks = []
                for u in range(BWD_BLOCKS):
                    n, cur, prev = _block_starts(idx * BWD_BLOCKS + u, n_blocks, dilation)
                    rows_c, rows_p = _class_rows(cur, dilation), _class_rows(prev, dilation)
                    qm = heads.stack(q_ref[rows_c, :])
                    k2 = heads.keys(jnp.concatenate([k_ref[rows_p, :], k_ref[rows_c, :]], axis=0)).astype(BF16)
                    v2 = heads.keys(jnp.concatenate([v_ref[rows_p, :], v_ref[rows_c, :]], axis=0)).astype(BF16)
                    d_o = do_ref[rows_c, :]
                    dom = heads.stack(d_o)
                    dd = d_o * o_ref[rows_c, :]
                    delta = jnp.concatenate([jnp.sum(jnp.where(heads.lower, dd, 0.0), axis=1, keepdims=True),
                                             jnp.sum(jnp.where(heads.lower, 0.0, dd), axis=1, keepdims=True)], axis=0)
                    lse = jnp.concatenate([l0_ref[rows_c, :], l1_ref[rows_c, :]], axis=0)
                    blocks.append(dict(n=n, rows_c=rows_c, rows_p=rows_p, qm=qm, k2=k2, dom=dom, delta=delta, lse=lse,
                                       s=_mm_nt(qm, k2), dp=_mm_nt(dom, v2)))
                for b in blocks:
                    s = b["s"] * (HEAD_DIM ** -0.5) + b_ref[pi]
                    s = jnp.where(jnp.logical_and(in_prev, b["n"] == 0), NEG_INF, s)
                    b["pr"] = jnp.exp(s - jnp.concatenate([b["lse"], b["lse"]], axis=1))
                    b["ds"] = b["pr"] * (b["dp"] - b["delta"])
                for b in blocks:
                    dsb = b["ds"].astype(BF16)
                    b["dq2"] = _mm(dsb, b["k2"])
                    b["dk2"] = _mm_tn(dsb, b["qm"])
                    b["dv2"] = _mm_tn(b["pr"].astype(BF16), b["dom"])
                for b in blocks:
                    ds_ref[pi] += b["ds"]
                    dq_ref[b["rows_c"], :] += heads.unstack(b["dq2"]) * (HEAD_DIM ** -0.5)
                    dk2 = heads.key_grads(b["dk2"]) * (HEAD_DIM ** -0.5)
                    dv2 = heads.key_grads(b["dv2"])
                    dk_acc[b["rows_p"], :] += dk2[:QBLK]
                    dk_acc[b["rows_c"], :] += dk2[QBLK:]
                    dv_acc[b["rows_p"], :] += dv2[:QBLK]
                    dv_acc[b["rows_c"], :] += dv2[QBLK:]
                    if has_sink:
                        for hd in range(2):
                            rows_h = slice(QBLK * hd, QBLK * (hd + 1))
                            p_sink = jnp.exp(sink_ref[0, 2 * g + hd] - b["lse"][rows_h, 0:1])
                            dsink = dsink - jnp.where(heads.lane == 2 * g + hd,
                                                      jnp.sum(p_sink * b["delta"][rows_h]), 0.0)
                return dsink

            dsink = lax.fori_loop(0, (dilation * n_blocks) // BWD_BLOCKS, step, dsink)

        if shared_kv:
            @pl.when(g == 0)
            def _():
                dk_ref[...] = dk_acc[...]
                dv_ref[...] = dv_acc[...]

            @pl.when(g != 0)
            def _():
                dk_ref[...] += dk_acc[...]
                dv_ref[...] += dv_acc[...]
        else:
            dk_ref[...] = dk_acc[...]
            dv_ref[...] = dv_acc[...]

        if has_sink:
            @pl.when(g == 0)
            def _():
                dsink_ref[...] = dsink

            @pl.when(g != 0)
            def _():
                dsink_ref[...] += dsink

    pair = pl.BlockSpec((T, LANES), lambda g: (0, g))
    stacked = pl.BlockSpec((n_pat, None, 2 * QBLK, 2 * QBLK), lambda g: (0, g, 0, 0))
    stacked_shape = (n_pat, N_HEAD_GROUP // 2, 2 * QBLK, 2 * QBLK)
    in_specs = _attn_specs(T, qcol, kcol, vcol, shared_kv)
    in_specs += [stacked, pair, pair,
                 pl.BlockSpec((None, T, LANES), lambda g: (2 * g, 0, 0)),
                 pl.BlockSpec((None, T, LANES), lambda g: (2 * g + 1, 0, 0))]
    args = [z, z, z, bias.reshape(stacked_shape), d_out, out, lse, lse]
    kv_out = _full((T, LANES)) if shared_kv else pair
    out_specs = [pair, kv_out, kv_out, stacked]
    out_shape = [jax.ShapeDtypeStruct((T, N_HEAD_GROUP * HEAD_DIM), F32),
                 jax.ShapeDtypeStruct((T, kv_width), F32), jax.ShapeDtypeStruct((T, kv_width), F32),
                 jax.ShapeDtypeStruct(stacked_shape, F32)]
    if has_sink:
        in_specs.insert(0, pl.BlockSpec(memory_space=pltpu.SMEM))
        args.insert(0, sinks)
        out_specs.append(_full((1, LANES)))
        out_shape.append(jax.ShapeDtypeStruct((1, LANES), F32))
    outs, side_outs = _hosted_call(
        body, name, (N_HEAD_GROUP // 2,), in_specs=in_specs, out_specs=out_specs, out_shape=out_shape,
        scratch_shapes=[pltpu.VMEM((T, LANES), F32), pltpu.VMEM((T, LANES), F32)], args=args, side=side)
    outs = list(outs)
    outs[3] = outs[3].reshape(n_pat, N_HEAD_GROUP, QBLK, 2 * QBLK)
    return outs, side_outs


def _outproj_fwd(mix_a, mix_b, w_out, b_out, g_post, h):
    T, D = h.shape
    tm = TOKEN_TILE
    d_mix = w_out.shape[0]

    def body(ma_ref, mb_ref, w_ref, b_ref, g_ref, h_ref, att_ref, hout_ref, mix_ref):
        mix = jnp.concatenate([ma_ref[...], mb_ref[...]], axis=1).astype(BF16)
        mix_ref[...] = mix
        att = _mm(mix, w_ref[...]) + b_ref[...]
        att_ref[...] = att
        hout_ref[...] = h_ref[...] + att * _rstd(att) * g_ref[...]

    def tile(w):
        return pl.BlockSpec((tm, w), lambda i: (i, 0))

    return pl.pallas_call(
        body, name="outproj_fwd", grid=(T // tm,),
        in_specs=[tile(A_Q), tile(B_W), _full((d_mix, D)), _full((1, D)), _full((1, D)), tile(D)],
        out_specs=[tile(D), tile(D), tile(d_mix)],
        out_shape=[jax.ShapeDtypeStruct((T, D), F32), jax.ShapeDtypeStruct((T, D), F32),
                   jax.ShapeDtypeStruct((T, d_mix), BF16)],
        compiler_params=_params(1),
    )(mix_a, mix_b, w_out, b_out, g_post, h)


def _outproj_bwd(dh, att, g_post, w_out):
    T, D = dh.shape
    tm = TOKEN_TILE
    d_mix = w_out.shape[0]

    def body(dh_ref, att_ref, g_ref, w_ref, dma_ref, dmb_ref, datt_ref, dg_ref, db_ref):
        i = pl.program_id(0)

        @pl.when(i == 0)
        def _():
            dg_ref[...] = jnp.zeros_like(dg_ref)
            db_ref[...] = jnp.zeros_like(db_ref)

        att = att_ref[...]
        datt, dgain = _rms_bwd(att, _rstd(att), g_ref[...], dh_ref[...])
        dg_ref[...] += _colsum(dgain)
        db_ref[...] += _colsum(datt)
        dattb = datt.astype(BF16)
        datt_ref[...] = dattb
        dmix = _mm_nt(dattb, w_ref[...])
        dma_ref[...] = dmix[:, :A_Q]
        dmb_ref[...] = dmix[:, A_Q:]

    def tile(w):
        return pl.BlockSpec((tm, w), lambda i: (i, 0))

    return pl.pallas_call(
        body, name="outproj_bwd", grid=(T // tm,),
        in_specs=[tile(D), tile(D), _full((1, D)), _full((d_mix, D))],
        out_specs=[tile(A_Q), tile(B_W), tile(D), _full((1, D)), _full((1, D))],
        out_shape=[jax.ShapeDtypeStruct((T, A_Q), F32), jax.ShapeDtypeStruct((T, B_W), F32),
                   jax.ShapeDtypeStruct((T, D), BF16), jax.ShapeDtypeStruct((1, D), F32),
                   jax.ShapeDtypeStruct((1, D), F32)],
        compiler_params=_params(1),
    )(dh, att, g_post, w_out)


def _ple_fwd_loss(h, g_pre, w_gate, p, w_proj, g_post, target):
    T, D = h.shape
    tm = TOKEN_TILE
    n_proj, ple, db = w_proj.shape

    def body(h_ref, gpre_ref, wg_ref, p_ref, wp_ref, gpost_ref, t_ref,
             a_ref, dpre_ref, de_ref, dh_ref, loss_ref, dgpost_ref):
        i = pl.program_id(0)

        @pl.when(i == 0)
        def _():
            loss_ref[...] = jnp.zeros_like(loss_ref)
            dgpost_ref[...] = jnp.zeros_like(dgpost_ref)

        x = h_ref[...]
        a = (x * _rstd(x) * gpre_ref[...]).astype(BF16)
        a_ref[...] = a
        gate = jax.nn.sigmoid(_mm(a, wg_ref[...]))
        pb = p_ref[...].astype(BF16)
        e = jnp.concatenate([_mm(pb, wp_ref[k]) for k in range(n_proj)], axis=1)
        ge = gate * e
        rg = _rstd(ge)
        diff = x + ge * rg * gpost_ref[...] - t_ref[...]
        loss_ref[...] += 0.5 * jnp.sum(jnp.mean(diff * diff, axis=1, keepdims=True))
        dy = diff * (1.0 / D)
        dh_ref[...] = dy
        dge, dgain = _rms_bwd(ge, rg, gpost_ref[...], dy)
        dgpost_ref[...] += _colsum(dgain)
        de_ref[...] = (dge * gate).astype(BF16)
        dpre_ref[...] = (dge * e * gate * (1.0 - gate)).astype(BF16)

    def tile(w):
        return pl.BlockSpec((tm, w), lambda i: (i, 0))

    return pl.pallas_call(
        body, name="ple_fwd_loss", grid=(T // tm,),
        in_specs=[tile(D), _full((1, D)), _full((D, D)), tile(ple), _full((n_proj, ple, db)), _full((1, D)), tile(D)],
        out_specs=[tile(D), tile(D), tile(D), tile(D), _full((1, LANES)), _full((1, D))],
        out_shape=[jax.ShapeDtypeStruct((T, D), BF16),
                   jax.ShapeDtypeStruct((T, D), BF16),
                   jax.ShapeDtypeStruct((T, D), BF16),
                   jax.ShapeDtypeStruct((T, D), F32),
                   jax.ShapeDtypeStruct((1, LANES), F32),
                   jax.ShapeDtypeStruct((1, D), F32)],
        compiler_params=_params(1),
    )(h, g_pre, w_gate, p, w_proj, g_post, target)


def _ple_bwd(dpre, w_gate, h, g_pre, dres):
    T, D = h.shape
    tm = TOKEN_TILE

    def body(dpre_ref, w_ref, h_ref, g_ref, dres_ref, dh_ref, dg_ref):
        i = pl.program_id(0)

        @pl.when(i == 0)
        def _():
            dg_ref[...] = jnp.zeros_like(dg_ref)

        da = _mm_nt(dpre_ref[...], w_ref[...])
        x = h_ref[...]
        dx, dgain = _rms_bwd(x, _rstd(x), g_ref[...], da)
        dg_ref[...] += _colsum(dgain)
        dh_ref[...] = dres_ref[...] + dx

    tile = pl.BlockSpec((tm, D), lambda i: (i, 0))
    return pl.pallas_call(
        body, name="ple_bwd", grid=(T // tm,),
        in_specs=[tile, _full((D, D)), tile, _full((1, D)), tile],
        out_specs=[tile, _full((1, D))],
        out_shape=[jax.ShapeDtypeStruct((T, D), F32), jax.ShapeDtypeStruct((1, D), F32)],
        compiler_params=_params(1),
    )(dpre, w_gate, h, g_pre, dres)


def _ple_dw_proj(p, de, n_proj):
    T, ple = p.shape
    D = de.shape[1]
    db = D // n_proj
    tk = TOKEN_TILE
    nt = T // tk

    def body(p_ref, de_ref, o_ref, acc):
        t = pl.program_id(0)

        @pl.when(t == 0)
        def _():
            acc[...] = jnp.zeros_like(acc)

        acc[...] += _mm_tn(p_ref[...].astype(BF16), de_ref[...])

        @pl.when(t == nt - 1)
        def _():
            for k in range(n_proj):
                o_ref[k] = acc[:, k * db:(k + 1) * db].astype(BF16)

    return pl.pallas_call(
        body, name="ple_dw_proj", grid=(nt,),
        in_specs=[pl.BlockSpec((tk, ple), lambda t: (t, 0)), pl.BlockSpec((tk, D), lambda t: (t, 0))],
        out_specs=_full((n_proj, ple, db)), out_shape=jax.ShapeDtypeStruct((n_proj, ple, db), BF16),
        scratch_shapes=[pltpu.VMEM((ple, D), F32)], compiler_params=_params(1),
    )(p, de)


def _tok(width):
    return pl.BlockSpec((DW_TILE, width), lambda b, t: (t, 0))


def _dw_gu(a, dgu, name, side=None):
    T, D = a.shape
    nj, _, _, FB = dgu.shape
    return _tn_matmul(
        dgu, a, pl.BlockSpec((None, None, DW_TILE, FB), lambda b, t: (b % nj, b // nj, t, 0)), _tok(D),
        jax.ShapeDtypeStruct((2 * nj, FB, D), BF16), pl.BlockSpec((None, FB, D), lambda b, t: (b, 0, 0)),
        2 * nj, T // DW_TILE, (FB, D), name, side=side)


def _dw_down(hh, df, name, side=None):
    nj, T, FB = hh.shape
    D = df.shape[1]
    return _tn_matmul(
        hh, df, pl.BlockSpec((None, DW_TILE, FB), lambda b, t: (b, t, 0)), _tok(D),
        jax.ShapeDtypeStruct((nj, FB, D), BF16), pl.BlockSpec((None, FB, D), lambda b, t: (b, 0, 0)),
        nj, T // DW_TILE, (FB, D), name, side=side)


def _dw_rows(xm, y, name, rows):
    T, k = xm.shape
    D = y.shape[1]
    out = _tn_matmul(
        xm, y, pl.BlockSpec((DW_TILE, rows), lambda b, t: (t, b)), _tok(D),
        jax.ShapeDtypeStruct((k, D), BF16), pl.BlockSpec((rows, D), lambda b, t: (b, 0)),
        k // rows, T // DW_TILE, (rows, D), name)
    return out.reshape(N_DEV, k // N_DEV, D)


def _cast_bf16(arrays):
    n = len(arrays)

    def body(*refs):
        for a in range(n):
            refs[n + a][...] = refs[a][...].astype(BF16)

    return pl.pallas_call(
        body, name="cast_shards",
        in_specs=[pl.BlockSpec(memory_space=pltpu.VMEM)] * n, out_specs=[pl.BlockSpec(memory_space=pltpu.VMEM)] * n,
        out_shape=[jax.ShapeDtypeStruct(a.shape, BF16) for a in arrays],
        compiler_params=pltpu.CompilerParams(vmem_limit_bytes=VMEM_LIMIT),
    )(*arrays)


def _all_gather_bf16(shards):
    n = len(shards)

    def body(*refs):
        ins, outs, scr = refs[:n], refs[n:2 * n], refs[2 * n:3 * n]
        send_sems, recv_sems, local_sems = refs[3 * n:]
        x, y, c = _mesh_place()
        me, sibling = (x, y, c), (x, y, 1 - c)
        chips = [(1 - x, y), (x, 1 - y), (1 - x, 1 - y)]
        for a in range(n):
            scr[a][...] = ins[a][...].astype(BF16)

        def copy(a, k, block, to, src=None):
            dst = outs[a].at[_slot(block)]
            return pltpu.make_async_remote_copy(
                src_ref=dst if src is None else src, dst_ref=dst,
                send_sem=send_sems.at[a, k], recv_sem=recv_sems.at[a, k], device_id=to, device_id_type=MESH)

        mine = [pltpu.make_async_copy(scr[a], outs[a].at[_slot(me)], local_sems.at[a]) for a in range(n)]
        first = [copy(a, 1 + j, me, (*chip, c), src=scr[a]) for j, chip in enumerate(chips) for a in range(n)]
        first += [copy(a, 0, me, sibling, src=scr[a]) for a in range(n)]
        for cp in first + mine:
            cp.start()
        passed = []
        for j, chip in enumerate(chips):
            for a in range(n):
                copy(a, 1 + j, (*chip, c), me).wait_recv()
                cp = copy(a, 4 + j, (*chip, c), sibling)
                cp.start()
                passed.append(cp)
        for a in range(n):
            copy(a, 0, sibling, me).wait_recv()
        for j, chip in enumerate(chips):
            for a in range(n):
                copy(a, 4 + j, (*chip, 1 - c), me).wait_recv()
        for cp in first + passed:
            cp.wait_send()
        for cp in mine:
            cp.wait()

    return pl.pallas_call(
        body, name="weights_all_gather",
        in_specs=[pl.BlockSpec(memory_space=pltpu.VMEM)] * n,
        out_specs=[pl.BlockSpec(memory_space=pl.ANY)] * n,
        out_shape=[jax.ShapeDtypeStruct((N_DEV,) + s.shape, BF16) for s in shards],
        scratch_shapes=[pltpu.VMEM(s.shape, BF16) for s in shards]
        + [pltpu.SemaphoreType.DMA((n, 7)), pltpu.SemaphoreType.DMA((n, 7)), pltpu.SemaphoreType.DMA((n,))],
        compiler_params=pltpu.CompilerParams(vmem_limit_bytes=VMEM_LIMIT),
    )(*shards)


def _pack_layout(D, n_rel_rows):
    n_bin = -(-D_IN // D)
    row_bin = len(GAINS)
    row_sink = row_bin + n_bin
    row_loss = row_sink + 1
    row_rb = -(-(row_loss + 1) // 8) * 8
    n_rows = row_rb + -(-n_rel_rows // 8) * 8
    bin_parts = [(r, min(D, D_IN - r * D)) for r in range(n_bin)]
    return row_bin, row_sink, row_loss, row_rb, n_rows, bin_parts


def _final_exchange(grad_blocks, partials, loss):
    D = partials["ffn1_pre_g"].shape[1]
    rb_shape = partials["rel_bias"].shape
    row_bin, row_sink, row_loss, row_rb, n_rows, bin_parts = _pack_layout(D, rb_shape[0])
    n_small = len(SMALL)

    def body(*refs):
        g_in = refs[0]
        part = dict(zip(SMALL, refs[1:1 + n_small]))
        loss_ref = refs[1 + n_small]
        landed, gath, pack, send_sems, recv_sems, local_sems = refs[2 + n_small:]

        pack[...] = jnp.zeros_like(pack)
        for i, name in enumerate(GAINS):
            pack[i:i + 1, :] = part[name][...]
        for r, width in bin_parts:
            pack[row_bin + r:row_bin + r + 1, 0:width] = part["b_in"][:, r * D:r * D + width]
        pack[row_sink:row_sink + 1, 0:LANES] = part["sinks"][...]
        pack[row_loss:row_loss + 1, 0:LANES] = loss_ref[...]
        pack[row_rb:row_rb + rb_shape[0], 0:rb_shape[1]] = part["rel_bias"][...]

        small_start, _, small_wait = _side_copies("gather", [pack], [gath], send_sems, recv_sems, local_sems, sem_row=0)
        big_start, _, big_wait = _side_copies("exchange", [g_in], [landed], send_sems, recv_sems, local_sems, sem_row=1)
        small_start()
        big_start()
        small_wait()
        big_wait()

    args = [grad_blocks] + [partials[k] for k in SMALL] + [loss]
    vmem = pl.BlockSpec(memory_space=pltpu.VMEM)
    any_spec = pl.BlockSpec(memory_space=pl.ANY)
    return pl.pallas_call(
        body, name="final_exchange",
        in_specs=[any_spec] + [vmem] * (n_small + 1),
        out_specs=[any_spec, any_spec],
        out_shape=[jax.ShapeDtypeStruct(grad_blocks.shape, grad_blocks.dtype),
                   jax.ShapeDtypeStruct((N_DEV, n_rows, D), F32)],
        scratch_shapes=[pltpu.VMEM((n_rows, D), F32), pltpu.SemaphoreType.DMA((2, 7)),
                        pltpu.SemaphoreType.DMA((2, 7)), pltpu.SemaphoreType.DMA((2,))],
    )(*args)


def _adamw(w, g, m, v):
    m = ADAM_B1 * m + (1.0 - ADAM_B1) * g
    v = ADAM_B2 * v + (1.0 - ADAM_B2) * (g * g)
    m_hat = m / (1.0 - ADAM_B1 ** ADAM_STEP)
    v_hat = v / (1.0 - ADAM_B2 ** ADAM_STEP)
    return -ADAM_LR * (m_hat / (jnp.sqrt(v_hat) + ADAM_EPS) + ADAM_WD * w), m, v


def _sum_adamw(partials, w, m, v, rows, name):
    R, C = w.shape

    def body(p_ref, w_ref, m_ref, v_ref, g_ref, d_ref, nm_ref, nv_ref):
        g = p_ref[0].astype(F32)
        for k in range(1, N_DEV):
            g = g + p_ref[k].astype(F32)
        g_ref[...] = g
        d_ref[...], nm_ref[...], nv_ref[...] = _adamw(w_ref[...], g, m_ref[...], v_ref[...])

    tile = pl.BlockSpec((rows, C), lambda i: (i, 0))
    return pl.pallas_call(
        body, name=name, grid=(R // rows,),
        in_specs=[pl.BlockSpec((N_DEV, rows, C), lambda i: (0, i, 0)), tile, tile, tile],
        out_specs=[tile] * 4, out_shape=[jax.ShapeDtypeStruct((R, C), F32)] * 4,
        compiler_params=_params(1),
    )(partials, w, m, v)


def _small_adamw(gathered, ws, ms, vs):
    D = ws["ffn1_pre_g"].shape[1]
    n_sink = ws["sinks"].shape[1]
    rb_shape = ws["rel_bias"].shape
    row_bin, row_sink, row_loss, row_rb, n_rows, bin_parts = _pack_layout(D, rb_shape[0])
    n_small = len(SMALL)

    def body(*refs):
        gath = refs[0]
        pos = 1
        w_ref = dict(zip(SMALL, refs[pos:pos + n_small]))
        m_ref = dict(zip(SMALL, refs[pos + n_small:pos + 2 * n_small]))
        v_ref = dict(zip(SMALL, refs[pos + 2 * n_small:pos + 3 * n_small]))
        pos += 3 * n_small
        outs = {name: refs[pos + 4 * i:pos + 4 * i + 4] for i, name in enumerate(SMALL)}
        loss_out = refs[pos + 4 * n_small]
        pack = refs[pos + 4 * n_small + 1]

        total = gath[0]
        for k in range(1, N_DEV):
            total = total + gath[k]
        pack[...] = total

        def update(name, g):
            g_out, d_out, m_out, v_out = outs[name]
            g_out[...] = g
            d_out[...], m_out[...], v_out[...] = _adamw(w_ref[name][...], g, m_ref[name][...], v_ref[name][...])

        for i, name in enumerate(GAINS):
            update(name, pack[i:i + 1, :])
        update("b_in", jnp.concatenate([pack[row_bin + r:row_bin + r + 1, 0:width] for r, width in bin_parts], axis=1))
        update("sinks", pack[row_sink:row_sink + 1, 0:n_sink])
        update("rel_bias", pack[row_rb:row_rb + rb_shape[0], 0:rb_shape[1]])
        loss_out[...] = pack[row_loss:row_loss + 1, 0:LANES]

    args = [gathered]
    for group in (ws, ms, vs):
        args += [group[k] for k in SMALL]
    out_shape = []
    for name in SMALL:
        out_shape += [jax.ShapeDtypeStruct(ws[name].shape, F32)] * 4
    out_shape.append(jax.ShapeDtypeStruct((1, LANES), F32))
    res = pl.pallas_call(
        body, name="small_adamw",
        in_specs=[pl.BlockSpec(memory_space=pltpu.VMEM)] * len(args),
        out_specs=[pl.BlockSpec(memory_space=pltpu.VMEM)] * len(out_shape),
        out_shape=out_shape,
        scratch_shapes=[pltpu.VMEM((n_rows, D), F32)],
    )(*args)
    per_name = {name: res[4 * i:4 * i + 4] for i, name in enumerate(SMALL)}
    return per_name, res[-1]


COLUMN_SHARDED = ("ffn1_w_gu", "ffn2_w_gu", "w_in")


def _adamw_rows(rows_total):
    return max(r for r in range(16, min(rows_total, 256) + 1, 16) if rows_total % r == 0)


def kernel(x, p, rel_bias, ffn1_pre_g, ffn1_w_gu, ffn1_w_down, ffn1_post_g, attn_pre_g, w_in, b_in, sinks, w_out, b_out, attn_post_g, ffn2_pre_g, ffn2_w_gu, ffn2_w_down, ffn2_post_g, ple_pre_g, w_ple_gate, w_ple_proj, ple_post_g, loss_target, m_rel_bias, m_ffn1_pre_g, m_ffn1_w_gu, m_ffn1_w_down, m_ffn1_post_g, m_attn_pre_g, m_w_in, m_b_in, m_sinks, m_w_out, m_b_out, m_attn_post_g, m_ffn2_pre_g, m_ffn2_w_gu, m_ffn2_w_down, m_ffn2_post_g, m_ple_pre_g, m_w_ple_gate, m_w_ple_proj, m_ple_post_g, v_rel_bias, v_ffn1_pre_g, v_ffn1_w_gu, v_ffn1_w_down, v_ffn1_post_g, v_attn_pre_g, v_w_in, v_b_in, v_sinks, v_w_out, v_b_out, v_attn_post_g, v_ffn2_pre_g, v_ffn2_w_gu, v_ffn2_w_down, v_ffn2_post_g, v_ple_pre_g, v_w_ple_gate, v_w_ple_proj, v_ple_post_g):
    given = dict(locals())
    ws = {k: given[k] for k in WEIGHTS}
    ms = {k: given["m_" + k] for k in WEIGHTS}
    vs = {k: given["v_" + k] for k in WEIGHTS}

    def shard(t):
        return t.reshape(t.shape[1:])

    xs, ps, target = shard(x), shard(shard(p)), shard(loss_target)
    T, D = xs.shape
    small = {k: ws[k] for k in SMALL}

    def local(group, k):
        t = shard(group[k])
        return jnp.swapaxes(t, 0, 1) if k in COLUMN_SHARDED else t

    shards = {k: local(ws, k) for k in BIG}

    w_gu1, w_down1 = _all_gather_bf16([shards["ffn1_w_gu"], shards["ffn1_w_down"]])
    w_down1 = w_down1.reshape(-1, D)
    later = ("w_in", "w_out", "ffn2_w_gu", "ffn2_w_down", "w_ple_gate", "w_ple_proj")
    cast = dict(zip(later, _cast_bf16([shards[k] for k in later])))

    buckets_a = _bucket_tiles(PATTERNS_A)
    buckets_b = _bucket_tiles(PATTERNS_B)
    bias_a = _bias_build(small["rel_bias"], buckets_a, 0, "bias_build_a")
    bias_b = _bias_build(small["rel_bias"], buckets_b, N_HEAD_GROUP, "bias_build_b")
    a_cfg = dict(patterns=PATTERNS_A, qcol=Q_A_COL, kcol=K_A_COL, vcol=V_A_COL, shared_kv=True)
    b_cfg = dict(patterns=PATTERNS_B, qcol=Q_B_COL, kcol=K_B_COL, vcol=V_B_COL, shared_kv=False)

    (h1, f1, a1, gu1), (w_in_g, w_down2) = _ffn_fwd(
        xs, small["ffn1_pre_g"], small["ffn1_post_g"], w_gu1, w_down1, "ffn1_fwd",
        side=("relay_gather", [cast["w_in"], cast["ffn2_w_down"]]))
    w_in_full = w_in_g.reshape(D_IN, D)
    w_down2 = w_down2.reshape(-1, D)
    (z, a2), (w_out_g,) = _inproj_fwd(h1, small["attn_pre_g"], w_in_full, small["b_in"],
                                      side=("relay_gather", [cast["w_out"]]))
    w_out_full = w_out_g.reshape(-1, D)
    (mix_a, lse_a), (w_gate, w_proj) = _attn_fwd(
        z, bias_a, small["sinks"], name="attn_a_fwd", **a_cfg,
        side=("relay_gather", [cast["w_ple_gate"], cast["w_ple_proj"]]))
    w_gate = w_gate.reshape(-1, D)
    (mix_b, lse_b), (w_gu2,) = _attn_fwd(
        z, bias_b, None, name="attn_b_fwd", **b_cfg, side=("relay_gather", [cast["ffn2_w_gu"]]))
    att, h2, mix = _outproj_fwd(mix_a, mix_b, w_out_full, small["b_out"], small["attn_post_g"], h1)
    (h3, f2, a3, gu2), _ = _ffn_fwd(h2, small["ffn2_pre_g"], small["ffn2_post_g"], w_gu2, w_down2, "ffn2_fwd")
    a4, dpre, de, dh4, loss, dg_ple_post = _ple_fwd_loss(
        h3, small["ple_pre_g"], w_gate, ps, w_proj, small["ple_post_g"], target)

    dh3, dg_ple_pre = _ple_bwd(dpre, w_gate, h3, small["ple_pre_g"], dh4)
    d_gate = _dw_rows(a4, dpre, "ple_dw_gate", min(256, D))
    d_proj = _ple_dw_proj(ps, de, N_DEV)
    landed = {}
    (dh2, df2, hh2, dgu2, dg_f2_post, dg_f2_pre), (landed["w_ple_gate"], landed["w_ple_proj"]) = _ffn_bwd(
        dh3, f2, small["ffn2_post_g"], h2, small["ffn2_pre_g"], gu2, w_gu2, w_down2, "ffn2_bwd",
        side=("exchange", [d_gate, d_proj]))
    d_gu2 = _dw_gu(a3, dgu2, "ffn2_dw_gu")
    d_down2 = _dw_down(hh2, df2, "ffn2_dw_down").reshape(N_DEV, -1, D)
    dmix_a, dmix_b, datt, dg_attn_post, db_out = _outproj_bwd(dh2, att, small["attn_post_g"], w_out_full)
    d_out = _dw_rows(mix, datt, "attn_dw_out", 256)
    (dqa, dka, dva, ds_a, dsinks), (landed["ffn2_w_down"],) = _attn_bwd(
        z, bias_a, small["sinks"], dmix_a, mix_a, lse_a, name="attn_a_bwd", **a_cfg,
        side=("exchange", [d_down2]))
    (dqb, dkb, dvb, ds_b), (landed["ffn2_w_gu"],) = _attn_bwd(
        z, bias_b, None, dmix_b, mix_b, lse_b, name="attn_b_bwd", **b_cfg, side=("exchange", [d_gu2]))
    (dh1, dz, db_in, dg_attn_pre), (landed["w_out"],) = _inproj_bwd(
        dqa, dka, dva, dqb, dkb, dvb, w_in_full, h1, small["attn_pre_g"], dh2, side=("exchange", [d_out]))
    cols = D_IN // 3
    d_in = _tn_matmul(
        dz, a2, pl.BlockSpec((DW_TILE, cols), lambda b, t: (t, b)), _tok(D),
        jax.ShapeDtypeStruct((D_IN, D), BF16), pl.BlockSpec((cols, D), lambda b, t: (b, 0)),
        3, T // DW_TILE, (cols, D), "attn_dw_in").reshape(N_DEV, D_IN // N_DEV, D)
    (grad_x, df1, hh1, dgu1, dg_f1_post, dg_f1_pre), (landed["w_in"],) = _ffn_bwd(
        dh1, f1, small["ffn1_post_g"], xs, small["ffn1_pre_g"], gu1, w_gu1, w_down1, "ffn1_bwd",
        side=("exchange", [d_in]))
    d_down1 = _dw_down(hh1, df1, "ffn1_dw_down").reshape(N_DEV, -1, D)
    d_gu1, (landed["ffn1_w_down"],) = _dw_gu(a1, dgu1, "ffn1_dw_gu", side=("exchange", [d_down1]))

    rb_a = _bias_grad(ds_a, buckets_a, "bias_grad_a")
    rb_b = _bias_grad(ds_b, buckets_b, "bias_grad_b").reshape(len(PATTERNS_B), N_HEAD_GROUP, NUM_BUCKETS)
    d_rel_bias = jnp.concatenate([rb_a.T, jnp.sum(rb_b, axis=0).T], axis=1)
    small_grads = {"ffn1_pre_g": dg_f1_pre, "ffn1_post_g": dg_f1_post, "attn_pre_g": dg_attn_pre,
                   "attn_post_g": dg_attn_post, "ffn2_pre_g": dg_f2_pre, "ffn2_post_g": dg_f2_post,
                   "ple_pre_g": dg_ple_pre, "ple_post_g": dg_ple_post, "b_out": db_out, "b_in": db_in,
                   "sinks": dsinks, "rel_bias": d_rel_bias}
    landed["ffn1_w_gu"], small_gathered = _final_exchange(d_gu1, small_grads, loss)

    result = {}
    for k in BIG:
        outs = _sum_adamw(landed[k], shards[k], local(ms, k), local(vs, k), _adamw_rows(shards[k].shape[0]),
                          k + "_adamw")
        if k in COLUMN_SHARDED:
            outs = [jnp.swapaxes(o, 0, 1) for o in outs]
        result[k] = [o.reshape(ws[k].shape) for o in outs]
    small_res, loss_all = _small_adamw(
        small_gathered, small, {k: ms[k] for k in SMALL}, {k: vs[k] for k in SMALL})
    result.update(small_res)

    out = [loss_all[0, 0], grad_x.reshape(x.shape)]
    for i in range(4):
        out += [result[k][i] for k in WEIGHTS]
    return tuple(out)
```

```python
import functools
import math

import numpy as np
import jax
import jax.numpy as jnp
from jax import lax
from jax.experimental import pallas as pl
from jax.experimental.pallas import tpu as pltpu

F32 = jnp.float32
BF16 = jnp.bfloat16
MESH = pl.DeviceIdType.MESH

N_DEV = 8
EPS = 1e-6
NEG_INF = -1e30
HEAD_DIM = 64
LANES = 128
QBLK = 128
D_IN = 2304
A_Q, A_KV, B_W = 512, 128, 512
N_HEAD_GROUP = 8
NUM_BUCKETS = 32
MAX_DISTANCE = 2048
PATTERNS_A = ((1, 127),)
PATTERNS_B = ((1, 128), (4, 128), (16, 128))
Q_A_COL, K_A_COL, V_A_COL = 0, 4, 5
Q_B_COL, K_B_COL, V_B_COL = 6, 10, 14

ADAM_LR, ADAM_B1, ADAM_B2, ADAM_EPS, ADAM_WD, ADAM_STEP = 0.001, 0.9, 0.999, 1e-08, 0.01, 10

TOKEN_TILE = 512
DW_TILE = 1024
FWD_BLOCKS = 4
BWD_BLOCKS = 2
VMEM_LIMIT = 56 * 1024 * 1024
ARB = "arbitrary"

BIG = ("ffn1_w_gu", "ffn1_w_down", "w_in", "w_out", "ffn2_w_gu", "ffn2_w_down", "w_ple_gate", "w_ple_proj")
GAINS = ("ffn1_pre_g", "ffn1_post_g", "attn_pre_g", "attn_post_g", "ffn2_pre_g", "ffn2_post_g",
         "ple_pre_g", "ple_post_g", "b_out")
SMALL = GAINS + ("b_in", "sinks", "rel_bias")
WEIGHTS = ("rel_bias", "ffn1_pre_g", "ffn1_w_gu", "ffn1_w_down", "ffn1_post_g", "attn_pre_g", "w_in", "b_in",
           "sinks", "w_out", "b_out", "attn_post_g", "ffn2_pre_g", "ffn2_w_gu", "ffn2_w_down", "ffn2_post_g",
           "ple_pre_g", "w_ple_gate", "w_ple_proj", "ple_post_g")


def _params(n_axes):
    return pltpu.CompilerParams(dimension_semantics=(ARB,) * n_axes, vmem_limit_bytes=VMEM_LIMIT)


def _mm(a, b):
    return jnp.dot(a, b, preferred_element_type=F32)


def _mm_nt(a, b):
    return lax.dot_general(a, b, (((1,), (1,)), ((), ())), preferred_element_type=F32)


def _mm_tn(a, b):
    return lax.dot_general(a, b, (((0,), (0,)), ((), ())), preferred_element_type=F32)


def _rstd(x):
    return lax.rsqrt(jnp.mean(x * x, axis=-1, keepdims=True) + EPS)


def _rms_bwd(x, r, gain, dy):
    n = x * r
    gdy = dy * gain
    return r * (gdy - n * jnp.mean(gdy * n, axis=-1, keepdims=True)), dy * n


def _colsum(v):
    return jnp.sum(v, axis=0, keepdims=True)


def _full(shape):
    return pl.BlockSpec(shape, lambda *_: (0,) * len(shape))


def _mesh_place():
    return lax.axis_index("x"), lax.axis_index("y"), lax.axis_index("c")


def _slot(dev):
    return 4 * dev[0] + 2 * dev[1] + dev[2]


def _peers(x, y, c):
    out = []
    for flip in range(1, N_DEV):
        dx, dy, dc = (flip >> 2) & 1, (flip >> 1) & 1, flip & 1
        out.append((1 - x if dx else x, 1 - y if dy else y, 1 - c if dc else c))
    return out


def _side_copies(kind, ins, outs, send_sems, recv_sems, local_sems, sem_row=0):
    n = len(ins)
    x, y, c = _mesh_place()
    me = _slot((x, y, c))
    peers = _peers(x, y, c)

    def src(a, block):
        return ins[a] if kind == "gather" else ins[a].at[block]

    def send(a, k, peer):
        return pltpu.make_async_remote_copy(
            src_ref=src(a, _slot(peer)), dst_ref=outs[a].at[me],
            send_sem=send_sems.at[sem_row + a, k], recv_sem=recv_sems.at[sem_row + a, k],
            device_id=peer, device_id_type=MESH)

    def arrival(a, k, peer):
        return pltpu.make_async_remote_copy(
            src_ref=src(a, _slot(peer)), dst_ref=outs[a].at[_slot(peer)],
            send_sem=send_sems.at[sem_row + a, k], recv_sem=recv_sems.at[sem_row + a, k],
            device_id=peer, device_id_type=MESH)

    def own(a):
        return pltpu.make_async_copy(src(a, me), outs[a].at[me], local_sems.at[sem_row + a])

    def start():
        for k, peer in enumerate(peers):
            for a in range(n):
                send(a, k, peer).start()
        for a in range(n):
            own(a).start()

    def wait():
        for k, peer in enumerate(peers):
            for a in range(n):
                arrival(a, k, peer).wait_recv()
        for k, peer in enumerate(peers):
            for a in range(n):
                send(a, k, peer).wait_send()
        for a in range(n):
            own(a).wait()

    return start, None, wait


def _relay_gather(ins, outs, send_sems, recv_sems, local_sems):
    n = len(ins)
    x, y, c = _mesh_place()
    me, sibling = (x, y, c), (x, y, 1 - c)
    chips = [(1 - x, y), (x, 1 - y), (1 - x, 1 - y)]

    def copy(a, k, block, to, src=None):
        dst = outs[a].at[_slot(block)]
        return pltpu.make_async_remote_copy(
            src_ref=dst if src is None else src, dst_ref=dst,
            send_sem=send_sems.at[a, k], recv_sem=recv_sems.at[a, k], device_id=to, device_id_type=MESH)

    def own(a):
        return pltpu.make_async_copy(ins[a], outs[a].at[_slot(me)], local_sems.at[a])

    def start():
        for j, chip in enumerate(chips):
            for a in range(n):
                copy(a, 1 + j, me, (*chip, c), src=ins[a]).start()
        for a in range(n):
            copy(a, 0, me, sibling, src=ins[a]).start()
            own(a).start()

    def relay():
        for j, chip in enumerate(chips):
            for a in range(n):
                copy(a, 1 + j, (*chip, c), me).wait_recv()
                copy(a, 4 + j, (*chip, c), sibling).start()

    def wait():
        for a in range(n):
            copy(a, 0, sibling, me).wait_recv()
        for j, chip in enumerate(chips):
            for a in range(n):
                copy(a, 4 + j, (*chip, 1 - c), me).wait_recv()
        for j, chip in enumerate(chips):
            for a in range(n):
                copy(a, 1 + j, me, (*chip, c), src=ins[a]).wait_send()
                copy(a, 4 + j, (*chip, c), sibling).wait_send()
        for a in range(n):
            copy(a, 0, me, sibling, src=ins[a]).wait_send()
            own(a).wait()

    return start, relay, wait


def _side_out_shapes(kind, arrays):
    if kind in ("gather", "relay_gather"):
        return [jax.ShapeDtypeStruct((N_DEV,) + a.shape, a.dtype) for a in arrays]
    return [jax.ShapeDtypeStruct(a.shape, a.dtype) for a in arrays]


def _hosted_call(body, name, grid, in_specs, out_specs, out_shape, scratch_shapes, args, side=None):
    if side is None:
        outs = pl.pallas_call(
            body, name=name, grid=grid, in_specs=in_specs, out_specs=out_specs, out_shape=out_shape,
            scratch_shapes=scratch_shapes, compiler_params=_params(len(grid)))(*args)
        return outs, []
    kind, arrays = side
    n_in, n_out, n_scr, n_side = len(in_specs), len(out_specs), len(scratch_shapes), len(arrays)

    def hosted(*refs):
        pos = 0
        groups = []
        for size in (n_in, n_side, n_out, n_side, n_scr):
            groups.append(refs[pos:pos + size])
            pos += size
        ins, side_in, outs, side_out, scr = groups
        send_sems, recv_sems, local_sems = refs[pos:]
        ids = [pl.program_id(d) for d in range(len(grid))]
        is_first = functools.reduce(jnp.logical_and, [i == 0 for i in ids])
        is_last = functools.reduce(jnp.logical_and, [i == g - 1 for i, g in zip(ids, grid)])
        if kind == "relay_gather":
            start, relay, wait = _relay_gather(side_in, side_out, send_sems, recv_sems, local_sems)
        else:
            start, relay, wait = _side_copies(kind, side_in, side_out, send_sems, recv_sems, local_sems)
        pl.when(is_first)(start)
        if relay is not None:
            pl.when(is_last)(relay)
        body(*ins, *outs, *scr)
        pl.when(is_last)(wait)

    any_spec = pl.BlockSpec(memory_space=pl.ANY)
    outs = pl.pallas_call(
        hosted, name=name, grid=grid,
        in_specs=list(in_specs) + [any_spec] * n_side,
        out_specs=list(out_specs) + [any_spec] * n_side,
        out_shape=list(out_shape) + _side_out_shapes(kind, arrays),
        scratch_shapes=list(scratch_shapes) + [pltpu.SemaphoreType.DMA((n_side, 7)), pltpu.SemaphoreType.DMA((n_side, 7)),
                                               pltpu.SemaphoreType.DMA((n_side,))],
        compiler_params=_params(len(grid)))(*args, *arrays)
    return outs[:n_out], outs[n_out:]


def _lane_chunks(width, chunk=2 * LANES):
    return [slice(n0, min(n0 + chunk, width)) for n0 in range(0, width, chunk)]


def _pipelined(chunks, first, middle, last):
    n = len(chunks)
    a, b, total = {}, {}, None
    for step in range(n + 2):
        if step < n:
            a[step] = first(chunks[step])
        if 0 <= step - 1 < n:
            b[step - 1] = middle(chunks[step - 1], a.pop(step - 1))
        if 0 <= step - 2 < n:
            part = last(chunks[step - 2], b.pop(step - 2))
            total = part if total is None else total + part
    return total


def _ffn_fwd(h, g_pre, g_post, w_gu, w_down, name, side=None):
    T, D = h.shape
    nj = w_gu.shape[0] // 2
    FB = w_gu.shape[1]
    tm = TOKEN_TILE

    def body(h_ref, gpre_ref, gpost_ref, wg_ref, wu_ref, wd_ref, hout_ref, f_ref, a_ref, gu_ref, a_scr, acc):
        j = pl.program_id(1)

        @pl.when(j == 0)
        def _():
            x = h_ref[...]
            a = (x * _rstd(x) * gpre_ref[...]).astype(BF16)
            a_scr[...] = a
            a_ref[...] = a
            acc[...] = jnp.zeros_like(acc)

        a = a_scr[...]
        g = _mm_nt(a, wg_ref[...])
        u = _mm_nt(a, wu_ref[...])
        gu_ref[0] = g.astype(BF16)
        gu_ref[1] = u.astype(BF16)
        hh = (g * jax.nn.sigmoid(g) * u).astype(BF16)
        acc[...] += _mm(hh, wd_ref[...])

        @pl.when(j == nj - 1)
        def _():
            f = acc[...]
            f_ref[...] = f
            hout_ref[...] = h_ref[...] + 0.5 * (f * _rstd(f) * gpost_ref[...])

    return _hosted_call(
        body, name, (T // tm, nj),
        in_specs=[
            pl.BlockSpec((tm, D), lambda i, j: (i, 0)),
            _full((1, D)), _full((1, D)),
            pl.BlockSpec((None, FB, D), lambda i, j: (j, 0, 0)),
            pl.BlockSpec((None, FB, D), lambda i, j: (j + nj, 0, 0)),
            pl.BlockSpec((FB, D), lambda i, j: (j, 0)),
        ],
        out_specs=[
            pl.BlockSpec((tm, D), lambda i, j: (i, 0)),
            pl.BlockSpec((tm, D), lambda i, j: (i, 0)),
            pl.BlockSpec((tm, D), lambda i, j: (i, 0)),
            pl.BlockSpec((None, 2, tm, FB), lambda i, j: (j, 0, i, 0)),
        ],
        out_shape=[
            jax.ShapeDtypeStruct((T, D), F32),
            jax.ShapeDtypeStruct((T, D), F32),
            jax.ShapeDtypeStruct((T, D), BF16),
            jax.ShapeDtypeStruct((nj, 2, T, FB), BF16),
        ],
        scratch_shapes=[pltpu.VMEM((tm, D), BF16), pltpu.VMEM((tm, D), F32)],
        args=(h, g_pre, g_post, w_gu, w_gu, w_down), side=side)


def _ffn_bwd(dh_out, f, g_post, h, g_pre, gu, w_gu, w_down, name, side=None):
    T, D = h.shape
    nj = w_gu.shape[0] // 2
    FB = w_gu.shape[1]
    tm = TOKEN_TILE

    def body(dho_ref, f_ref, gpost_ref, h_ref, gpre_ref, gu_ref, wg_ref, wu_ref, wd_ref,
             dhin_ref, df_ref, hh_ref, dgu_ref, dgpost_ref, dgpre_ref, df_scr, da):
        i, j = pl.program_id(0), pl.program_id(1)

        @pl.when(jnp.logical_and(i == 0, j == 0))
        def _():
            dgpost_ref[...] = jnp.zeros_like(dgpost_ref)
            dgpre_ref[...] = jnp.zeros_like(dgpre_ref)

        @pl.when(j == 0)
        def _():
            fv = f_ref[...]
            df, dgain = _rms_bwd(fv, _rstd(fv), gpost_ref[...], 0.5 * dho_ref[...])
            dgpost_ref[...] += _colsum(dgain)
            dfb = df.astype(BF16)
            df_scr[...] = dfb
            df_ref[...] = dfb
            da[...] = jnp.zeros_like(da)

        dfb = df_scr[...]

        halves = (slice(0, tm // 2), slice(tm // 2, tm))

        def hidden_grad(c):
            return [_mm_nt(dfb[rows], wd_ref[c, :]) for rows in halves]

        def through_swiglu(c, dhh):
            dhh = jnp.concatenate(dhh, axis=0)
            g = gu_ref[0, :, c].astype(F32)
            u = gu_ref[1, :, c].astype(F32)
            sg = jax.nn.sigmoid(g)
            silu = g * sg
            hh_ref[:, c] = (silu * u).astype(BF16)
            dg = (dhh * u * (sg * (1.0 + (g - silu)))).astype(BF16)
            du = (dhh * silu).astype(BF16)
            dgu_ref[0, :, c] = dg
            dgu_ref[1, :, c] = du
            return dg, du

        def input_grad(c, dgu):
            return jnp.concatenate(
                [_mm(dgu[0][rows], wg_ref[c, :]) + _mm(dgu[1][rows], wu_ref[c, :]) for rows in halves], axis=0)

        da[...] += _pipelined(_lane_chunks(FB), hidden_grad, through_swiglu, input_grad)

        @pl.when(j == nj - 1)
        def _():
            x = h_ref[...]
            dx, dgain = _rms_bwd(x, _rstd(x), gpre_ref[...], da[...])
            dgpre_ref[...] += _colsum(dgain)
            dhin_ref[...] = dho_ref[...] + dx

    tile = pl.BlockSpec((tm, D), lambda i, j: (i, 0))
    return _hosted_call(
        body, name, (T // tm, nj),
        in_specs=[
            tile, tile, _full((1, D)), tile, _full((1, D)),
            pl.BlockSpec((None, 2, tm, FB), lambda i, j: (j, 0, i, 0)),
            pl.BlockSpec((None, FB, D), lambda i, j: (j, 0, 0)),
            pl.BlockSpec((None, FB, D), lambda i, j: (j + nj, 0, 0)),
            pl.BlockSpec((FB, D), lambda i, j: (j, 0)),
        ],
        out_specs=[
            tile, tile,
            pl.BlockSpec((None, tm, FB), lambda i, j: (j, i, 0)),
            pl.BlockSpec((None, 2, tm, FB), lambda i, j: (j, 0, i, 0)),
            _full((1, D)), _full((1, D)),
        ],
        out_shape=[
            jax.ShapeDtypeStruct((T, D), F32),
            jax.ShapeDtypeStruct((T, D), BF16),
            jax.ShapeDtypeStruct((nj, T, FB), BF16),
            jax.ShapeDtypeStruct((nj, 2, T, FB), BF16),
            jax.ShapeDtypeStruct((1, D), F32),
            jax.ShapeDtypeStruct((1, D), F32),
        ],
        scratch_shapes=[pltpu.VMEM((tm, D), BF16), pltpu.VMEM((tm, D), F32)],
        args=(dh_out, f, g_post, h, g_pre, gu, w_gu, w_gu, w_down), side=side)


def _tn_matmul(x, y, x_spec, y_spec, out_shape, out_spec, n_blocks, n_steps, acc_shape, name, side=None):
    def body(x_ref, y_ref, o_ref, acc):
        t = pl.program_id(1)

        @pl.when(t == 0)
        def _():
            acc[...] = jnp.zeros_like(acc)

        acc[...] += _mm_tn(x_ref[...].astype(BF16), y_ref[...].astype(BF16))

        @pl.when(t == n_steps - 1)
        def _():
            o_ref[...] = acc[...].astype(o_ref.dtype)

    outs, side_outs = _hosted_call(
        body, name, (n_blocks, n_steps), in_specs=[x_spec, y_spec], out_specs=[out_spec], out_shape=[out_shape],
        scratch_shapes=[pltpu.VMEM(acc_shape, F32)], args=(x, y), side=side)
    return (outs[0], side_outs) if side is not None else outs[0]


def _inproj_fwd(h, g_pre, w_in, b_in, side=None):
    T, D = h.shape
    tm = TOKEN_TILE

    def body(h_ref, g_ref, w_ref, b_ref, z_ref, a_ref):
        x = h_ref[...]
        a = (x * _rstd(x) * g_ref[...]).astype(BF16)
        a_ref[...] = a
        z_ref[...] = _mm_nt(a, w_ref[...]) + b_ref[...]

    return _hosted_call(
        body, "inproj_fwd", (T // tm,),
        in_specs=[pl.BlockSpec((tm, D), lambda i: (i, 0)), _full((1, D)), _full((D_IN, D)), _full((1, D_IN))],
        out_specs=[pl.BlockSpec((tm, D_IN), lambda i: (i, 0)), pl.BlockSpec((tm, D), lambda i: (i, 0))],
        out_shape=[jax.ShapeDtypeStruct((T, D_IN), F32), jax.ShapeDtypeStruct((T, D), BF16)],
        scratch_shapes=[], args=(h, g_pre, w_in, b_in), side=side)


def _inproj_bwd(dqa, dka, dva, dqb, dkb, dvb, w_in, h, g_pre, dres, side=None):
    T, D = h.shape
    tm = TOKEN_TILE

    def body(dqa_ref, dka_ref, dva_ref, dqb_ref, dkb_ref, dvb_ref, w_ref, h_ref, g_ref, dres_ref,
             dh_ref, dz_ref, dbin_ref, dg_ref):
        i = pl.program_id(0)

        @pl.when(i == 0)
        def _():
            dbin_ref[...] = jnp.zeros_like(dbin_ref)
            dg_ref[...] = jnp.zeros_like(dg_ref)

        dz = jnp.concatenate([dqa_ref[...], dka_ref[...], dva_ref[...], dqb_ref[...], dkb_ref[...], dvb_ref[...]],
                             axis=1)
        dbin_ref[...] += _colsum(dz)
        dzb = dz.astype(BF16)
        dz_ref[...] = dzb
        da = _mm(dzb, w_ref[...])
        x = h_ref[...]
        dx, dgain = _rms_bwd(x, _rstd(x), g_ref[...], da)
        dg_ref[...] += _colsum(dgain)
        dh_ref[...] = dres_ref[...] + dx

    def tile(w):
        return pl.BlockSpec((tm, w), lambda i: (i, 0))

    return _hosted_call(
        body, "inproj_bwd", (T // tm,),
        in_specs=[tile(A_Q), tile(A_KV), tile(A_KV), tile(B_W), tile(B_W), tile(B_W),
                  _full((D_IN, D)), tile(D), _full((1, D)), tile(D)],
        out_specs=[tile(D), tile(D_IN), _full((1, D_IN)), _full((1, D))],
        out_shape=[jax.ShapeDtypeStruct((T, D), F32), jax.ShapeDtypeStruct((T, D_IN), BF16),
                   jax.ShapeDtypeStruct((1, D_IN), F32), jax.ShapeDtypeStruct((1, D), F32)],
        scratch_shapes=[], args=(dqa, dka, dva, dqb, dkb, dvb, w_in, h, g_pre, dres), side=side)


def _bucket_tiles(patterns):
    i = np.arange(QBLK)[:, None]
    j = np.arange(2 * QBLK)[None, :]
    dist = QBLK + i - j
    max_exact = NUM_BUCKETS // 2
    tiles = []
    for dilation, max_dist in patterns:
        n = np.maximum(dist * dilation, 0)
        nf = np.maximum(n, 1).astype(np.float32)
        large = max_exact + (np.log(nf / np.float32(max_exact)) / np.float32(math.log(MAX_DISTANCE / max_exact))
                             * np.float32(NUM_BUCKETS - max_exact)).astype(np.int32)
        bucket = np.where(n < max_exact, n, np.minimum(large, NUM_BUCKETS - 1))
        tiles.append(np.where((dist >= 0) & (dist <= max_dist), bucket, -1))
    return jnp.asarray(np.stack(tiles).astype(np.int32))


def _bias_build(rel_bias, buckets, head0, name, side=None):
    n = buckets.shape[0]

    def body(bk_ref, rb_ref, o_ref):
        bk = bk_ref[...]
        base = jnp.where(bk < 0, NEG_INF, 0.0).astype(F32)
        for hd in range(N_HEAD_GROUP):
            o_ref[hd] = lax.fori_loop(
                0, NUM_BUCKETS, lambda b, acc, hd=hd: jnp.where(bk == b, rb_ref[b, head0 + hd], acc), base)

    outs, side_outs = _hosted_call(
        body, name, (n,),
        in_specs=[pl.BlockSpec((None, QBLK, 2 * QBLK), lambda p: (p, 0, 0)), pl.BlockSpec(memory_space=pltpu.SMEM)],
        out_specs=[pl.BlockSpec((None, N_HEAD_GROUP, QBLK, 2 * QBLK), lambda p: (p, 0, 0, 0))],
        out_shape=[jax.ShapeDtypeStruct((n, N_HEAD_GROUP, QBLK, 2 * QBLK), F32)],
        scratch_shapes=[], args=(buckets, rel_bias), side=side)
    return outs[0], side_outs


def _bias_grad(ds, buckets, name):
    n = buckets.shape[0]

    def body(ds_ref, bk_ref, o_ref):
        bk = bk_ref[...]
        row = lax.broadcasted_iota(jnp.int32, (NUM_BUCKETS, 2 * QBLK), 0)
        for hd in range(N_HEAD_GROUP):
            d = ds_ref[hd]
            per_key = jnp.zeros((NUM_BUCKETS, 2 * QBLK), F32)
            for b in range(NUM_BUCKETS):
                per_key = jnp.where(row == b, jnp.sum(jnp.where(bk == b, d, 0.0), axis=0, keepdims=True), per_key)
            o_ref[hd] = jnp.broadcast_to(jnp.sum(per_key, axis=1, keepdims=True), (NUM_BUCKETS, LANES))

    out = pl.pallas_call(
        body, name=name, grid=(n,),
        in_specs=[pl.BlockSpec((None, N_HEAD_GROUP, QBLK, 2 * QBLK), lambda p: (p, 0, 0, 0)),
                  pl.BlockSpec((None, QBLK, 2 * QBLK), lambda p: (p, 0, 0))],
        out_specs=pl.BlockSpec((None, N_HEAD_GROUP, NUM_BUCKETS, LANES), lambda p: (p, 0, 0, 0)),
        out_shape=jax.ShapeDtypeStruct((n, N_HEAD_GROUP, NUM_BUCKETS, LANES), F32),
        compiler_params=_params(1),
    )(ds, buckets)
    return out[:, :, :, 0].reshape(n * N_HEAD_GROUP, NUM_BUCKETS)


def _class_rows(start, dilation):
    if dilation == 1:
        return pl.ds(pl.multiple_of(start, QBLK), QBLK)
    return pl.ds(start, QBLK, stride=dilation)


def _block_starts(idx, n_blocks, dilation):
    cls = idx // n_blocks
    n = idx % n_blocks
    cur = cls + dilation * QBLK * n
    prev = cls + dilation * QBLK * jnp.maximum(n - 1, 0)
    return n, cur, prev


class _HeadPair:
    def __init__(self, g, shared_kv):
        self.lane = lax.broadcasted_iota(jnp.int32, (1, LANES), 1)
        self.lower = self.lane < HEAD_DIM
        self.shared_kv = shared_kv
        self.key_lanes = (self.lane >= HEAD_DIM).astype(jnp.int32) == (g // 2)

    def stack(self, t):
        return jnp.concatenate([jnp.where(self.lower, t, 0.0), jnp.where(self.lower, 0.0, t)], axis=0).astype(BF16)

    def unstack(self, t2):
        return jnp.where(self.lower, t2[:QBLK], t2[QBLK:])

    def keys(self, t):
        if self.shared_kv:
            return jnp.where(self.key_lanes, t, pltpu.roll(t, HEAD_DIM, 1))
        return t

    def key_grads(self, t):
        if self.shared_kv:
            return jnp.where(self.key_lanes, t + pltpu.roll(t, HEAD_DIM, 1), 0.0)
        return t


def _attn_specs(T, qcol, kcol, vcol, shared_kv):
    kv = (lambda c: (lambda g: (0, c))) if shared_kv else (lambda c: (lambda g: (0, c + g)))
    return [pl.BlockSpec((T, LANES), lambda g: (0, qcol + g)),
            pl.BlockSpec((T, LANES), kv(kcol)),
            pl.BlockSpec((T, LANES), kv(vcol))]


def _attn_fwd(z, bias, sinks, patterns, qcol, kcol, vcol, shared_kv, name, side=None):
    T = z.shape[0]
    n_pat = len(patterns)
    has_sink = sinks is not None

    def body(*refs):
        if has_sink:
            sink_ref, refs = refs[0], refs[1:]
        q_ref, k_ref, v_ref, b_ref, o_ref, l_ref = refs[:6]
        po_scr = refs[6:6 + n_pat]
        pl_scr = refs[6 + n_pat:]
        g = pl.program_id(0)
        heads = _HeadPair(g, shared_kv)
        in_prev = lax.broadcasted_iota(jnp.int32, (2 * QBLK, 2 * QBLK), 1) < QBLK

        for pi, (dilation, _) in enumerate(patterns):
            n_blocks = T // (QBLK * dilation)

            def step(it, carry, pi=pi, dilation=dilation, n_blocks=n_blocks):
                blocks = []
                for u in range(FWD_BLOCKS):
                    n, cur, prev = _block_starts(it * FWD_BLOCKS + u, n_blocks, dilation)
                    rows_c, rows_p = _class_rows(cur, dilation), _class_rows(prev, dilation)
                    qm = heads.stack(q_ref[rows_c, :])
                    k2 = heads.keys(jnp.concatenate([k_ref[rows_p, :], k_ref[rows_c, :]], axis=0)).astype(BF16)
                    v2 = heads.keys(jnp.concatenate([v_ref[rows_p, :], v_ref[rows_c, :]], axis=0)).astype(BF16)
                    blocks.append(dict(n=n, rows=rows_c, v2=v2, s=_mm_nt(qm, k2)))
                for b in blocks:
                    s = b["s"] * (HEAD_DIM ** -0.5) + b_ref[pi]
                    b["s"] = jnp.where(jnp.logical_and(in_prev, b["n"] == 0), NEG_INF, s)
                    b["m"] = jnp.max(b["s"], axis=1, keepdims=True)
                for b in blocks:
                    b["pr"] = jnp.exp(b["s"] - b["m"])
                    b["den"] = jnp.sum(b["pr"], axis=1, keepdims=True)
                for b in blocks:
                    b["o2"] = _mm(b["pr"].astype(BF16), b["v2"])
                for b in blocks:
                    lse = b["m"] + jnp.log(b["den"])
                    po_scr[pi][b["rows"], :] = heads.unstack(b["o2"] / b["den"])
                    pl_scr[2 * pi][b["rows"], :] = jnp.broadcast_to(lse[:QBLK], (QBLK, LANES))
                    pl_scr[2 * pi + 1][b["rows"], :] = jnp.broadcast_to(lse[QBLK:], (QBLK, LANES))
                return carry

            lax.fori_loop(0, (dilation * n_blocks) // FWD_BLOCKS, step, 0)

        def merge(ci, carry):
            rows = pl.ds(pl.multiple_of(ci * QBLK, QBLK), QBLK)
            weights = []
            for hd in range(2):
                parts = [pl_scr[2 * pi + hd][rows, :] for pi in range(n_pat)]
                m = functools.reduce(jnp.maximum, parts)
                if has_sink:
                    sink = sink_ref[0, 2 * g + hd]
                    m = jnp.maximum(m, sink)
                den = functools.reduce(jnp.add, [jnp.exp(x - m) for x in parts])
                if has_sink:
                    den = den + jnp.exp(sink - m)
                lse = m + jnp.log(den)
                l_ref[hd, rows, :] = lse
                weights.append([jnp.exp(x - lse) for x in parts])
            o_ref[rows, :] = functools.reduce(
                jnp.add, [jnp.where(heads.lower, weights[0][pi], weights[1][pi]) * po_scr[pi][rows, :]
                          for pi in range(n_pat)])
            return carry

        lax.fori_loop(0, T // QBLK, merge, 0)

    in_specs = _attn_specs(T, qcol, kcol, vcol, shared_kv)
    in_specs.append(pl.BlockSpec((n_pat, None, 2 * QBLK, 2 * QBLK), lambda g: (0, g, 0, 0)))
    args = [z, z, z, bias.reshape(n_pat, N_HEAD_GROUP // 2, 2 * QBLK, 2 * QBLK)]
    if has_sink:
        in_specs.insert(0, pl.BlockSpec(memory_space=pltpu.SMEM))
        args.insert(0, sinks)
    return _hosted_call(
        body, name, (N_HEAD_GROUP // 2,),
        in_specs=in_specs,
        out_specs=[pl.BlockSpec((T, LANES), lambda g: (0, g)), pl.BlockSpec((2, T, LANES), lambda g: (g, 0, 0))],
        out_shape=[jax.ShapeDtypeStruct((T, N_HEAD_GROUP * HEAD_DIM), F32),
                   jax.ShapeDtypeStruct((N_HEAD_GROUP, T, LANES), F32)],
        scratch_shapes=[pltpu.VMEM((T, LANES), F32)] * (3 * n_pat), args=args, side=side)


def _attn_bwd(z, bias, sinks, d_out, out, lse, patterns, qcol, kcol, vcol, shared_kv, name, side=None):
    T = z.shape[0]
    n_pat = len(patterns)
    has_sink = sinks is not None
    kv_width = LANES if shared_kv else N_HEAD_GROUP * HEAD_DIM

    def body(*refs):
        if has_sink:
            sink_ref, refs = refs[0], refs[1:]
        q_ref, k_ref, v_ref, b_ref, do_ref, o_ref, l0_ref, l1_ref = refs[:8]
        dq_ref, dk_ref, dv_ref, ds_ref = refs[8:12]
        dsink_ref = refs[12] if has_sink else None
        dk_acc, dv_acc = refs[-2:]
        g = pl.program_id(0)
        heads = _HeadPair(g, shared_kv)
        in_prev = lax.broadcasted_iota(jnp.int32, (2 * QBLK, 2 * QBLK), 1) < QBLK

        dq_ref[...] = jnp.zeros_like(dq_ref)
        ds_ref[...] = jnp.zeros_like(ds_ref)
        dk_acc[...] = jnp.zeros_like(dk_acc)
        dv_acc[...] = jnp.zeros_like(dv_acc)

        dsink = jnp.zeros((1, LANES), F32)
        for pi, (dilation, _) in enumerate(patterns):
            n_blocks = T // (QBLK * dilation)

            def step(idx, dsink, pi=pi, dilation=dilation, n_blocks=n_blocks):
                blocks = []
                for u in range(BWD_BLOCKS):
                    n, cur, prev = _block_starts(idx * BWD_BLOCKS + u, n_blocks, dilation)
                    rows_c, rows_p = _class_rows(cur, dilation), _class_rows(prev, dilation)
                    qm = heads.stack(q_ref[rows_c, :])
                    k2 = heads.keys(jnp.concatenate([k_ref[rows_p, :], k_ref[rows_c, :]], axis=0)).astype(BF16)
                    v2 = heads.keys(jnp.concatenate([v_ref[rows_p, :], v_ref[rows_c, :]], axis=0)).astype(BF16)
                    d_o = do_ref[rows_c, :]
                    dom = heads.stack(d_o)
                    dd = d_o * o_ref[rows_c, :]
                    delta = jnp.concatenate([jnp.sum(jnp.where(heads.lower, dd, 0.0), axis=1, keepdims=True),
                                             jnp.sum(jnp.where(heads.lower, 0.0, dd), axis=1, keepdims=True)], axis=0)
                    lse = jnp.concatenate([l0_ref[rows_c, :], l1_ref[rows_c, :]], axis=0)
                    blocks.append(dict(n=n, rows_c=rows_c, rows_p=rows_p, qm=qm, k2=k2, dom=dom, delta=delta, lse=lse,
                                       s=_mm_nt(qm, k2), dp=_mm_nt(dom, v2)))
                for b in blocks:
                    s = b["s"] * (HEAD_DIM ** -0.5) + b_ref[pi]
                    s = jnp.where(jnp.logical_and(in_prev, b["n"] == 0), NEG_INF, s)
                    b["pr"] = jnp.exp(s - jnp.concatenate([b["lse"], b["lse"]], axis=1))
                    b["ds"] = b["pr"] * (b["dp"] - b["delta"])
                for b in blocks:
                    dsb = b["ds"].astype(BF16)
                    b["dq2"] = _mm(dsb, b["k2"])
                    b["dk2"] = _mm_tn(dsb, b["qm"])
                    b["dv2"] = _mm_tn(b["pr"].astype(BF16), b["dom"])
                for b in blocks:
                    ds_ref[pi] += b["ds"]
                    dq_ref[b["rows_c"], :] += heads.unstack(b["dq2"]) * (HEAD_DIM ** -0.5)
                    dk2 = heads.key_grads(b["dk2"]) * (HEAD_DIM ** -0.5)
                    dv2 = heads.key_grads(b["dv2"])
                    dk_acc[b["rows_p"], :] += dk2[:QBLK]
                    dk_acc[b["rows_c"], :] += dk2[QBLK:]
                    dv_acc[b["rows_p"], :] += dv2[:QBLK]
                    dv_acc[b["rows_c"], :] += dv2[QBLK:]
                    if has_sink:
                        for hd in range(2):
                            rows_h = slice(QBLK * hd, QBLK * (hd + 1))
                            p_sink = jnp.exp(sink_ref[0, 2 * g + hd] - b["lse"][rows_h, 0:1])
                            dsink = dsink - jnp.where(heads.lane == 2 * g + hd,
                                                      jnp.sum(p_sink * b["delta"][rows_h]), 0.0)
                return dsink

            dsink = lax.fori_loop(0, (dilation * n_blocks) // BWD_BLOCKS, step, dsink)

        if shared_kv:
            @pl.when(g == 0)
            def _():
                dk_ref[...] = dk_acc[...]
                dv_ref[...] = dv_acc[...]

            @pl.when(g != 0)
            def _():
                dk_ref[...] += dk_acc[...]
                dv_ref[...] += dv_acc[...]
        else:
            dk_ref[...] = dk_acc[...]
            dv_ref[...] = dv_acc[...]

        if has_sink:
            @pl.when(g == 0)
            def _():
                dsink_ref[...] = dsink

            @pl.when(g != 0)
            def _():
                dsink_ref[...] += dsink

    pair = pl.BlockSpec((T, LANES), lambda g: (0, g))
    stacked = pl.BlockSpec((n_pat, None, 2 * QBLK, 2 * QBLK), lambda g: (0, g, 0, 0))
    stacked_shape = (n_pat, N_HEAD_GROUP // 2, 2 * QBLK, 2 * QBLK)
    in_specs = _attn_specs(T, qcol, kcol, vcol, shared_kv)
    in_specs += [stacked, pair, pair,
                 pl.BlockSpec((None, T, LANES), lambda g: (2 * g, 0, 0)),
                 pl.BlockSpec((None, T, LANES), lambda g: (2 * g + 1, 0, 0))]
    args = [z, z, z, bias.reshape(stacked_shape), d_out, out, lse, lse]
    kv_out = _full((T, LANES)) if shared_kv else pair
    out_specs = [pair, kv_out, kv_out, stacked]
    out_shape = [jax.ShapeDtypeStruct((T, N_HEAD_GROUP * HEAD_DIM), F32),
                 jax.ShapeDtypeStruct((T, kv_width), F32), jax.ShapeDtypeStruct((T, kv_width), F32),
                 jax.ShapeDtypeStruct(stacked_shape, F32)]
    if has_sink:
        in_specs.insert(0, pl.BlockSpec(memory_space=pltpu.SMEM))
        args.insert(0, sinks)
        out_specs.append(_full((1, LANES)))
        out_shape.append(jax.ShapeDtypeStruct((1, LANES), F32))
    outs, side_outs = _hosted_call(
        body, name, (N_HEAD_GROUP // 2,), in_specs=in_specs, out_specs=out_specs, out_shape=out_shape,
        scratch_shapes=[pltpu.VMEM((T, LANES), F32), pltpu.VMEM((T, LANES), F32)], args=args, side=side)
    outs = list(outs)
    outs[3] = outs[3].reshape(n_pat, N_HEAD_GROUP, QBLK, 2 * QBLK)
    return outs, side_outs


def _outproj_fwd(mix_a, mix_b, w_out, b_out, g_post, h):
    T, D = h.shape
    tm = TOKEN_TILE
    d_mix = w_out.shape[0]

    def body(ma_ref, mb_ref, w_ref, b_ref, g_ref, h_ref, att_ref, hout_ref, mix_ref):
        mix = jnp.concatenate([ma_ref[...], mb_ref[...]], axis=1).astype(BF16)
        mix_ref[...] = mix
        att = _mm(mix, w_ref[...]) + b_ref[...]
        att_ref[...] = att
        hout_ref[...] = h_ref[...] + att * _rstd(att) * g_ref[...]

    def tile(w):
        return pl.BlockSpec((tm, w), lambda i: (i, 0))

    return pl.pallas_call(
        body, name="outproj_fwd", grid=(T // tm,),
        in_specs=[tile(A_Q), tile(B_W), _full((d_mix, D)), _full((1, D)), _full((1, D)), tile(D)],
        out_specs=[tile(D), tile(D), tile(d_mix)],
        out_shape=[jax.ShapeDtypeStruct((T, D), F32), jax.ShapeDtypeStruct((T, D), F32),
                   jax.ShapeDtypeStruct((T, d_mix), BF16)],
        compiler_params=_params(1),
    )(mix_a, mix_b, w_out, b_out, g_post, h)


def _outproj_bwd(dh, att, g_post, w_out):
    T, D = dh.shape
    tm = TOKEN_TILE
    d_mix = w_out.shape[0]

    def body(dh_ref, att_ref, g_ref, w_ref, dma_ref, dmb_ref, datt_ref, dg_ref, db_ref):
        i = pl.program_id(0)

        @pl.when(i == 0)
        def _():
            dg_ref[...] = jnp.zeros_like(dg_ref)
            db_ref[...] = jnp.zeros_like(db_ref)

        att = att_ref[...]
        datt, dgain = _rms_bwd(att, _rstd(att), g_ref[...], dh_ref[...])
        dg_ref[...] += _colsum(dgain)
        db_ref[...] += _colsum(datt)
        dattb = datt.astype(BF16)
        datt_ref[...] = dattb
        dmix = _mm_nt(dattb, w_ref[...])
        dma_ref[...] = dmix[:, :A_Q]
        dmb_ref[...] = dmix[:, A_Q:]

    def tile(w):
        return pl.BlockSpec((tm, w), lambda i: (i, 0))

    return pl.pallas_call(
        body, name="outproj_bwd", grid=(T // tm,),
        in_specs=[tile(D), tile(D), _full((1, D)), _full((d_mix, D))],
        out_specs=[tile(A_Q), tile(B_W), tile(D), _full((1, D)), _full((1, D))],
        out_shape=[jax.ShapeDtypeStruct((T, A_Q), F32), jax.ShapeDtypeStruct((T, B_W), F32),
                   jax.ShapeDtypeStruct((T, D), BF16), jax.ShapeDtypeStruct((1, D), F32),
                   jax.ShapeDtypeStruct((1, D), F32)],
        compiler_params=_params(1),
    )(dh, att, g_post, w_out)


def _ple_fwd_loss(h, g_pre, w_gate, p, w_proj, g_post, target):
    T, D = h.shape
    tm = TOKEN_TILE
    n_proj, ple, db = w_proj.shape

    def body(h_ref, gpre_ref, wg_ref, p_ref, wp_ref, gpost_ref, t_ref,
             a_ref, dpre_ref, de_ref, dh_ref, loss_ref, dgpost_ref):
        i = pl.program_id(0)

        @pl.when(i == 0)
        def _():
            loss_ref[...] = jnp.zeros_like(loss_ref)
            dgpost_ref[...] = jnp.zeros_like(dgpost_ref)

        x = h_ref[...]
        a = (x * _rstd(x) * gpre_ref[...]).astype(BF16)
        a_ref[...] = a
        gate = jax.nn.sigmoid(_mm(a, wg_ref[...]))
        pb = p_ref[...].astype(BF16)
        e = jnp.concatenate([_mm(pb, wp_ref[k]) for k in range(n_proj)], axis=1)
        ge = gate * e
        rg = _rstd(ge)
        diff = x + ge * rg * gpost_ref[...] - t_ref[...]
        loss_ref[...] += 0.5 * jnp.sum(jnp.mean(diff * diff, axis=1, keepdims=True))
        dy = diff * (1.0 / D)
        dh_ref[...] = dy
        dge, dgain = _rms_bwd(ge, rg, gpost_ref[...], dy)
        dgpost_ref[...] += _colsum(dgain)
        de_ref[...] = (dge * gate).astype(BF16)
        dpre_ref[...] = (dge * e * gate * (1.0 - gate)).astype(BF16)

    def tile(w):
        return pl.BlockSpec((tm, w), lambda i: (i, 0))

    return pl.pallas_call(
        body, name="ple_fwd_loss", grid=(T // tm,),
        in_specs=[tile(D), _full((1, D)), _full((D, D)), tile(ple), _full((n_proj, ple, db)), _full((1, D)), tile(D)],
        out_specs=[tile(D), tile(D), tile(D), tile(D), _full((1, LANES)), _full((1, D))],
        out_shape=[jax.ShapeDtypeStruct((T, D), BF16),
                   jax.ShapeDtypeStruct((T, D), BF16),
                   jax.ShapeDtypeStruct((T, D), BF16),
                   jax.ShapeDtypeStruct((T, D), F32),
                   jax.ShapeDtypeStruct((1, LANES), F32),
                   jax.ShapeDtypeStruct((1, D), F32)],
        compiler_params=_params(1),
    )(h, g_pre, w_gate, p, w_proj, g_post, target)


def _ple_bwd(dpre, w_gate, h, g_pre, dres):
    T, D = h.shape
    tm = TOKEN_TILE

    def body(dpre_ref, w_ref, h_ref, g_ref, dres_ref, dh_ref, dg_ref):
        i = pl.program_id(0)

        @pl.when(i == 0)
        def _():
            dg_ref[...] = jnp.zeros_like(dg_ref)

        da = _mm_nt(dpre_ref[...], w_ref[...])
        x = h_ref[...]
        dx, dgain = _rms_bwd(x, _rstd(x), g_ref[...], da)
        dg_ref[...] += _colsum(dgain)
        dh_ref[...] = dres_ref[...] + dx

    tile = pl.BlockSpec((tm, D), lambda i: (i, 0))
    return pl.pallas_call(
        body, name="ple_bwd", grid=(T // tm,),
        in_specs=[tile, _full((D, D)), tile, _full((1, D)), tile],
        out_specs=[tile, _full((1, D))],
        out_shape=[jax.ShapeDtypeStruct((T, D), F32), jax.ShapeDtypeStruct((1, D), F32)],
        compiler_params=_params(1),
    )(dpre, w_gate, h, g_pre, dres)


def _ple_dw_proj(p, de, n_proj):
    T, ple = p.shape
    D = de.shape[1]
    db = D // n_proj
    tk = TOKEN_TILE
    nt = T // tk

    def body(p_ref, de_ref, o_ref, acc):
        t = pl.program_id(0)

        @pl.when(t == 0)
        def _():
            acc[...] = jnp.zeros_like(acc)

        acc[...] += _mm_tn(p_ref[...].astype(BF16), de_ref[...])

        @pl.when(t == nt - 1)
        def _():
            for k in range(n_proj):
                o_ref[k] = acc[:, k * db:(k + 1) * db].astype(BF16)

    return pl.pallas_call(
        body, name="ple_dw_proj", grid=(nt,),
        in_specs=[pl.BlockSpec((tk, ple), lambda t: (t, 0)), pl.BlockSpec((tk, D), lambda t: (t, 0))],
        out_specs=_full((n_proj, ple, db)), out_shape=jax.ShapeDtypeStruct((n_proj, ple, db), BF16),
        scratch_shapes=[pltpu.VMEM((ple, D), F32)], compiler_params=_params(1),
    )(p, de)


def _tok(width):
    return pl.BlockSpec((DW_TILE, width), lambda b, t: (t, 0))


def _dw_gu(a, dgu, name, side=None):
    T, D = a.shape
    nj, _, _, FB = dgu.shape
    return _tn_matmul(
        dgu, a, pl.BlockSpec((None, None, DW_TILE, FB), lambda b, t: (b % nj, b // nj, t, 0)), _tok(D),
        jax.ShapeDtypeStruct((2 * nj, FB, D), BF16), pl.BlockSpec((None, FB, D), lambda b, t: (b, 0, 0)),
        2 * nj, T // DW_TILE, (FB, D), name, side=side)


def _dw_down(hh, df, name, side=None):
    nj, T, FB = hh.shape
    D = df.shape[1]
    return _tn_matmul(
        hh, df, pl.BlockSpec((None, DW_TILE, FB), lambda b, t: (b, t, 0)), _tok(D),
        jax.ShapeDtypeStruct((nj, FB, D), BF16), pl.BlockSpec((None, FB, D), lambda b, t: (b, 0, 0)),
        nj, T // DW_TILE, (FB, D), name, side=side)


def _dw_rows(xm, y, name, rows):
    T, k = xm.shape
    D = y.shape[1]
    out = _tn_matmul(
        xm, y, pl.BlockSpec((DW_TILE, rows), lambda b, t: (t, b)), _tok(D),
        jax.ShapeDtypeStruct((k, D), BF16), pl.BlockSpec((rows, D), lambda b, t: (b, 0)),
        k // rows, T // DW_TILE, (rows, D), name)
    return out.reshape(N_DEV, k // N_DEV, D)


def _cast_bf16(arrays):
    n = len(arrays)

    def body(*refs):
        for a in range(n):
            refs[n + a][...] = refs[a][...].astype(BF16)

    return pl.pallas_call(
        body, name="cast_shards",
        in_specs=[pl.BlockSpec(memory_space=pltpu.VMEM)] * n, out_specs=[pl.BlockSpec(memory_space=pltpu.VMEM)] * n,
        out_shape=[jax.ShapeDtypeStruct(a.shape, BF16) for a in arrays],
        compiler_params=pltpu.CompilerParams(vmem_limit_bytes=VMEM_LIMIT),
    )(*arrays)


def _pack_layout(D, n_rel_rows):
    n_bin = -(-D_IN // D)
    row_bin = len(GAINS)
    row_sink = row_bin + n_bin
    row_loss = row_sink + 1
    row_rb = -(-(row_loss + 1) // 8) * 8
    n_rows = row_rb + -(-n_rel_rows // 8) * 8
    bin_parts = [(r, min(D, D_IN - r * D)) for r in range(n_bin)]
    return row_bin, row_sink, row_loss, row_rb, n_rows, bin_parts


def _final_exchange(grad_blocks, partials, loss):
    D = partials["ffn1_pre_g"].shape[1]
    rb_shape = partials["rel_bias"].shape
    row_bin, row_sink, row_loss, row_rb, n_rows, bin_parts = _pack_layout(D, rb_shape[0])
    n_small = len(SMALL)

    def body(*refs):
        g_in = refs[0]
        part = dict(zip(SMALL, refs[1:1 + n_small]))
        loss_ref = refs[1 + n_small]
        landed, gath, pack, send_sems, recv_sems, local_sems = refs[2 + n_small:]

        pack[...] = jnp.zeros_like(pack)
        for i, name in enumerate(GAINS):
            pack[i:i + 1, :] = part[name][...]
        for r, width in bin_parts:
            pack[row_bin + r:row_bin + r + 1, 0:width] = part["b_in"][:, r * D:r * D + width]
        pack[row_sink:row_sink + 1, 0:LANES] = part["sinks"][...]
        pack[row_loss:row_loss + 1, 0:LANES] = loss_ref[...]
        pack[row_rb:row_rb + rb_shape[0], 0:rb_shape[1]] = part["rel_bias"][...]

        small_start, _, small_wait = _side_copies("gather", [pack], [gath], send_sems, recv_sems, local_sems, sem_row=0)
        big_start, _, big_wait = _side_copies("exchange", [g_in], [landed], send_sems, recv_sems, local_sems, sem_row=1)
        small_start()
        big_start()
        small_wait()
        big_wait()

    args = [grad_blocks] + [partials[k] for k in SMALL] + [loss]
    vmem = pl.BlockSpec(memory_space=pltpu.VMEM)
    any_spec = pl.BlockSpec(memory_space=pl.ANY)
    return pl.pallas_call(
        body, name="final_exchange",
        in_specs=[any_spec] + [vmem] * (n_small + 1),
        out_specs=[any_spec, any_spec],
        out_shape=[jax.ShapeDtypeStruct(grad_blocks.shape, grad_blocks.dtype),
                   jax.ShapeDtypeStruct((N_DEV, n_rows, D), F32)],
        scratch_shapes=[pltpu.VMEM((n_rows, D), F32), pltpu.SemaphoreType.DMA((2, 7)),
                        pltpu.SemaphoreType.DMA((2, 7)), pltpu.SemaphoreType.DMA((2,))],
    )(*args)


def _adamw(w, g, m, v):
    m = ADAM_B1 * m + (1.0 - ADAM_B1) * g
    v = ADAM_B2 * v + (1.0 - ADAM_B2) * (g * g)
    m_hat = m / (1.0 - ADAM_B1 ** ADAM_STEP)
    v_hat = v / (1.0 - ADAM_B2 ** ADAM_STEP)
    return -ADAM_LR * (m_hat / (jnp.sqrt(v_hat) + ADAM_EPS) + ADAM_WD * w), m, v


def _sum_adamw(partials, w, m, v, rows, name):
    R, C = w.shape

    def body(p_ref, w_ref, m_ref, v_ref, g_ref, d_ref, nm_ref, nv_ref):
        g = p_ref[0].astype(F32)
        for k in range(1, N_DEV):
            g = g + p_ref[k].astype(F32)
        g_ref[...] = g
        d_ref[...], nm_ref[...], nv_ref[...] = _adamw(w_ref[...], g, m_ref[...], v_ref[...])

    tile = pl.BlockSpec((rows, C), lambda i: (i, 0))
    return pl.pallas_call(
        body, name=name, grid=(R // rows,),
        in_specs=[pl.BlockSpec((N_DEV, rows, C), lambda i: (0, i, 0)), tile, tile, tile],
        out_specs=[tile] * 4, out_shape=[jax.ShapeDtypeStruct((R, C), F32)] * 4,
        compiler_params=_params(1),
    )(partials, w, m, v)


def _small_adamw(gathered, ws, ms, vs):
    D = ws["ffn1_pre_g"].shape[1]
    n_sink = ws["sinks"].shape[1]
    rb_shape = ws["rel_bias"].shape
    row_bin, row_sink, row_loss, row_rb, n_rows, bin_parts = _pack_layout(D, rb_shape[0])
    n_small = len(SMALL)

    def body(*refs):
        gath = refs[0]
        pos = 1
        w_ref = dict(zip(SMALL, refs[pos:pos + n_small]))
        m_ref = dict(zip(SMALL, refs[pos + n_small:pos + 2 * n_small]))
        v_ref = dict(zip(SMALL, refs[pos + 2 * n_small:pos + 3 * n_small]))
        pos += 3 * n_small
        outs = {name: refs[pos + 4 * i:pos + 4 * i + 4] for i, name in enumerate(SMALL)}
        loss_out = refs[pos + 4 * n_small]
        pack = refs[pos + 4 * n_small + 1]

        total = gath[0]
        for k in range(1, N_DEV):
            total = total + gath[k]
        pack[...] = total

        def update(name, g):
            g_out, d_out, m_out, v_out = outs[name]
            g_out[...] = g
            d_out[...], m_out[...], v_out[...] = _adamw(w_ref[name][...], g, m_ref[name][...], v_ref[name][...])

        for i, name in enumerate(GAINS):
            update(name, pack[i:i + 1, :])
        update("b_in", jnp.concatenate([pack[row_bin + r:row_bin + r + 1, 0:width] for r, width in bin_parts], axis=1))
        update("sinks", pack[row_sink:row_sink + 1, 0:n_sink])
        update("rel_bias", pack[row_rb:row_rb + rb_shape[0], 0:rb_shape[1]])
        loss_out[...] = pack[row_loss:row_loss + 1, 0:LANES]

    args = [gathered]
    for group in (ws, ms, vs):
        args += [group[k] for k in SMALL]
    out_shape = []
    for name in SMALL:
        out_shape += [jax.ShapeDtypeStruct(ws[name].shape, F32)] * 4
    out_shape.append(jax.ShapeDtypeStruct((1, LANES), F32))
    res = pl.pallas_call(
        body, name="small_adamw",
        in_specs=[pl.BlockSpec(memory_space=pltpu.VMEM)] * len(args),
        out_specs=[pl.BlockSpec(memory_space=pltpu.VMEM)] * len(out_shape),
        out_shape=out_shape,
        scratch_shapes=[pltpu.VMEM((n_rows, D), F32)],
    )(*args)
    per_name = {name: res[4 * i:4 * i + 4] for i, name in enumerate(SMALL)}
    return per_name, res[-1]


COLUMN_SHARDED = ("ffn1_w_gu", "ffn2_w_gu", "w_in")


def _adamw_rows(rows_total):
    return max(r for r in range(16, min(rows_total, 256) + 1, 16) if rows_total % r == 0)


def kernel(x, p, rel_bias, ffn1_pre_g, ffn1_w_gu, ffn1_w_down, ffn1_post_g, attn_pre_g, w_in, b_in, sinks, w_out, b_out, attn_post_g, ffn2_pre_g, ffn2_w_gu, ffn2_w_down, ffn2_post_g, ple_pre_g, w_ple_gate, w_ple_proj, ple_post_g, loss_target, m_rel_bias, m_ffn1_pre_g, m_ffn1_w_gu, m_ffn1_w_down, m_ffn1_post_g, m_attn_pre_g, m_w_in, m_b_in, m_sinks, m_w_out, m_b_out, m_attn_post_g, m_ffn2_pre_g, m_ffn2_w_gu, m_ffn2_w_down, m_ffn2_post_g, m_ple_pre_g, m_w_ple_gate, m_w_ple_proj, m_ple_post_g, v_rel_bias, v_ffn1_pre_g, v_ffn1_w_gu, v_ffn1_w_down, v_ffn1_post_g, v_attn_pre_g, v_w_in, v_b_in, v_sinks, v_w_out, v_b_out, v_attn_post_g, v_ffn2_pre_g, v_ffn2_w_gu, v_ffn2_w_down, v_ffn2_post_g, v_ple_pre_g, v_w_ple_gate, v_w_ple_proj, v_ple_post_g):
    given = dict(locals())
    ws = {k: given[k] for k in WEIGHTS}
    ms = {k: given["m_" + k] for k in WEIGHTS}
    vs = {k: given["v_" + k] for k in WEIGHTS}

    def shard(t):
        return t.reshape(t.shape[1:])

    xs, ps, target = shard(x), shard(shard(p)), shard(loss_target)
    T, D = xs.shape
    small = {k: ws[k] for k in SMALL}

    def local(group, k):
        t = shard(group[k])
        return jnp.swapaxes(t, 0, 1) if k in COLUMN_SHARDED else t

    shards = {k: local(ws, k) for k in BIG}

    cast = dict(zip(BIG, _cast_bf16([shards[k] for k in BIG])))
    buckets_a = _bucket_tiles(PATTERNS_A)
    buckets_b = _bucket_tiles(PATTERNS_B)
    bias_a, _ = _bias_build(small["rel_bias"], buckets_a, 0, "bias_build_a")
    bias_b, (w_gu1, w_down1) = _bias_build(
        small["rel_bias"], buckets_b, N_HEAD_GROUP, "bias_build_b",
        side=("relay_gather", [cast["ffn1_w_gu"], cast["ffn1_w_down"]]))
    w_down1 = w_down1.reshape(-1, D)
    a_cfg = dict(patterns=PATTERNS_A, qcol=Q_A_COL, kcol=K_A_COL, vcol=V_A_COL, shared_kv=True)
    b_cfg = dict(patterns=PATTERNS_B, qcol=Q_B_COL, kcol=K_B_COL, vcol=V_B_COL, shared_kv=False)

    (h1, f1, a1, gu1), (w_in_g, w_down2) = _ffn_fwd(
        xs, small["ffn1_pre_g"], small["ffn1_post_g"], w_gu1, w_down1, "ffn1_fwd",
        side=("relay_gather", [cast["w_in"], cast["ffn2_w_down"]]))
    w_in_full = w_in_g.reshape(D_IN, D)
    w_down2 = w_down2.reshape(-1, D)
    (z, a2), (w_out_g,) = _inproj_fwd(h1, small["attn_pre_g"], w_in_full, small["b_in"],
                                      side=("relay_gather", [cast["w_out"]]))
    w_out_full = w_out_g.reshape(-1, D)
    (mix_a, lse_a), (w_gate, w_proj) = _attn_fwd(
        z, bias_a, small["sinks"], name="attn_a_fwd", **a_cfg,
        side=("relay_gather", [cast["w_ple_gate"], cast["w_ple_proj"]]))
    w_gate = w_gate.reshape(-1, D)
    (mix_b, lse_b), (w_gu2,) = _attn_fwd(
        z, bias_b, None, name="attn_b_fwd", **b_cfg, side=("relay_gather", [cast["ffn2_w_gu"]]))
    att, h2, mix = _outproj_fwd(mix_a, mix_b, w_out_full, small["b_out"], small["attn_post_g"], h1)
    (h3, f2, a3, gu2), _ = _ffn_fwd(h2, small["ffn2_pre_g"], small["ffn2_post_g"], w_gu2, w_down2, "ffn2_fwd")
    a4, dpre, de, dh4, loss, dg_ple_post = _ple_fwd_loss(
        h3, small["ple_pre_g"], w_gate, ps, w_proj, small["ple_post_g"], target)

    dh3, dg_ple_pre = _ple_bwd(dpre, w_gate, h3, small["ple_pre_g"], dh4)
    d_gate = _dw_rows(a4, dpre, "ple_dw_gate", min(256, D))
    d_proj = _ple_dw_proj(ps, de, N_DEV)
    landed = {}
    (dh2, df2, hh2, dgu2, dg_f2_post, dg_f2_pre), (landed["w_ple_gate"], landed["w_ple_proj"]) = _ffn_bwd(
        dh3, f2, small["ffn2_post_g"], h2, small["ffn2_pre_g"], gu2, w_gu2, w_down2, "ffn2_bwd",
        side=("exchange", [d_gate, d_proj]))
    d_gu2 = _dw_gu(a3, dgu2, "ffn2_dw_gu")
    d_down2 = _dw_down(hh2, df2, "ffn2_dw_down").reshape(N_DEV, -1, D)
    dmix_a, dmix_b, datt, dg_attn_post, db_out = _outproj_bwd(dh2, att, small["attn_post_g"], w_out_full)
    d_out = _dw_rows(mix, datt, "attn_dw_out", 256)
    (dqa, dka, dva, ds_a, dsinks), (landed["ffn2_w_down"],) = _attn_bwd(
        z, bias_a, small["sinks"], dmix_a, mix_a, lse_a, name="attn_a_bwd", **a_cfg,
        side=("exchange", [d_down2]))
    (dqb, dkb, dvb, ds_b), (landed["ffn2_w_gu"],) = _attn_bwd(
        z, bias_b, None, dmix_b, mix_b, lse_b, name="attn_b_bwd", **b_cfg, side=("exchange", [d_gu2]))
    (dh1, dz, db_in, dg_attn_pre), (landed["w_out"],) = _inproj_bwd(
        dqa, dka, dva, dqb, dkb, dvb, w_in_full, h1, small["attn_pre_g"], dh2, side=("exchange", [d_out]))
    cols = D_IN // 3
    d_in = _tn_matmul(
        dz, a2, pl.BlockSpec((DW_TILE, cols), lambda b, t: (t, b)), _tok(D),
        jax.ShapeDtypeStruct((D_IN, D), BF16), pl.BlockSpec((cols, D), lambda b, t: (b, 0)),
        3, T // DW_TILE, (cols, D), "attn_dw_in").reshape(N_DEV, D_IN // N_DEV, D)
    (grad_x, df1, hh1, dgu1, dg_f1_post, dg_f1_pre), (landed["w_in"],) = _ffn_bwd(
        dh1, f1, small["ffn1_post_g"], xs, small["ffn1_pre_g"], gu1, w_gu1, w_down1, "ffn1_bwd",
        side=("exchange", [d_in]))
    d_down1 = _dw_down(hh1, df1, "ffn1_dw_down").reshape(N_DEV, -1, D)
    d_gu1, (landed["ffn1_w_down"],) = _dw_gu(a1, dgu1, "ffn1_dw_gu", side=("exchange", [d_down1]))

    rb_a = _bias_grad(ds_a, buckets_a, "bias_grad_a")
    rb_b = _bias_grad(ds_b, buckets_b, "bias_grad_b").reshape(len(PATTERNS_B), N_HEAD_GROUP, NUM_BUCKETS)
    d_rel_bias = jnp.concatenate([rb_a.T, jnp.sum(rb_b, axis=0).T], axis=1)
    small_grads = {"ffn1_pre_g": dg_f1_pre, "ffn1_post_g": dg_f1_post, "attn_pre_g": dg_attn_pre,
                   "attn_post_g": dg_attn_post, "ffn2_pre_g": dg_f2_pre, "ffn2_post_g": dg_f2_post,
                   "ple_pre_g": dg_ple_pre, "ple_post_g": dg_ple_post, "b_out": db_out, "b_in": db_in,
                   "sinks": dsinks, "rel_bias": d_rel_bias}
    landed["ffn1_w_gu"], small_gathered = _final_exchange(d_gu1, small_grads, loss)

    result = {}
    for k in BIG:
        outs = _sum_adamw(landed[k], shards[k], local(ms, k), local(vs, k), _adamw_rows(shards[k].shape[0]),
                          k + "_adamw")
        if k in COLUMN_SHARDED:
            outs = [jnp.swapaxes(o, 0, 1) for o in outs]
        result[k] = [o.reshape(ws[k].shape) for o in outs]
    small_res, loss_all = _small_adamw(
        small_gathered, small, {k: ms[k] for k in SMALL}, {k: vs[k] for k in SMALL})
    result.update(small_res)

    out = [loss_all[0, 0], grad_x.reshape(x.shape)]
    for i in range(4):
        out += [result[k][i] for k in WEIGHTS]
    return tuple(out)
```

```python
import functools
import math

import numpy as np
import jax
import jax.numpy as jnp
from jax import lax
from jax.experimental import pallas as pl
from jax.experimental.pallas import tpu as pltpu

F32 = jnp.float32
BF16 = jnp.bfloat16
MESH = pl.DeviceIdType.MESH

N_DEV = 8
EPS = 1e-6
NEG_INF = -1e30
HEAD_DIM = 64
LANES = 128
QBLK = 128
D_IN = 2304
A_Q, A_KV, B_W = 512, 128, 512
N_HEAD_GROUP = 8
NUM_BUCKETS = 32
MAX_DISTANCE = 2048
PATTERNS_A = ((1, 127),)
PATTERNS_B = ((1, 128), (4, 128), (16, 128))
Q_A_COL, K_A_COL, V_A_COL = 0, 4, 5
Q_B_COL, K_B_COL, V_B_COL = 6, 10, 14

ADAM_LR, ADAM_B1, ADAM_B2, ADAM_EPS, ADAM_WD, ADAM_STEP = 0.001, 0.9, 0.999, 1e-08, 0.01, 10

TOKEN_TILE = 512
DW_TILE = 1024
FWD_BLOCKS = 4
BWD_BLOCKS = 2
VMEM_LIMIT = 56 * 1024 * 1024
ARB = "arbitrary"

BIG = ("ffn1_w_gu", "ffn1_w_down", "w_in", "w_out", "ffn2_w_gu", "ffn2_w_down", "w_ple_gate", "w_ple_proj")
GAINS = ("ffn1_pre_g", "ffn1_post_g", "attn_pre_g", "attn_post_g", "ffn2_pre_g", "ffn2_post_g",
         "ple_pre_g", "ple_post_g", "b_out")
SMALL = GAINS + ("b_in", "sinks", "rel_bias")
WEIGHTS = ("rel_bias", "ffn1_pre_g", "ffn1_w_gu", "ffn1_w_down", "ffn1_post_g", "attn_pre_g", "w_in", "b_in",
           "sinks", "w_out", "b_out", "attn_post_g", "ffn2_pre_g", "ffn2_w_gu", "ffn2_w_down", "ffn2_post_g",
           "ple_pre_g", "w_ple_gate", "w_ple_proj", "ple_post_g")


def _params(n_axes):
    return pltpu.CompilerParams(dimension_semantics=(ARB,) * n_axes, vmem_limit_bytes=VMEM_LIMIT)


def _mm(a, b):
    return jnp.dot(a, b, preferred_element_type=F32)


def _mm_nt(a, b):
    return lax.dot_general(a, b, (((1,), (1,)), ((), ())), preferred_element_type=F32)


def _mm_tn(a, b):
    return lax.dot_general(a, b, (((0,), (0,)), ((), ())), preferred_element_type=F32)


def _rstd(x):
    return lax.rsqrt(jnp.mean(x * x, axis=-1, keepdims=True) + EPS)


def _rms_bwd(x, r, gain, dy):
    n = x * r
    gdy = dy * gain
    return r * (gdy - n * jnp.mean(gdy * n, axis=-1, keepdims=True)), dy * n


def _colsum(v):
    return jnp.sum(v, axis=0, keepdims=True)


def _full(shape):
    return pl.BlockSpec(shape, lambda *_: (0,) * len(shape))


def _mesh_place():
    return lax.axis_index("x"), lax.axis_index("y"), lax.axis_index("c")


def _slot(dev):
    return 4 * dev[0] + 2 * dev[1] + dev[2]


def _peers(x, y, c):
    out = []
    for flip in range(1, N_DEV):
        dx, dy, dc = (flip >> 2) & 1, (flip >> 1) & 1, flip & 1
        out.append((1 - x if dx else x, 1 - y if dy else y, 1 - c if dc else c))
    return out


def _side_copies(kind, ins, outs, send_sems, recv_sems, local_sems, sem_row=0):
    n = len(ins)
    x, y, c = _mesh_place()
    me = _slot((x, y, c))
    peers = _peers(x, y, c)

    def src(a, block):
        return ins[a] if kind == "gather" else ins[a].at[block]

    def send(a, k, peer):
        return pltpu.make_async_remote_copy(
            src_ref=src(a, _slot(peer)), dst_ref=outs[a].at[me],
            send_sem=send_sems.at[sem_row + a, k], recv_sem=recv_sems.at[sem_row + a, k],
            device_id=peer, device_id_type=MESH)

    def arrival(a, k, peer):
        return pltpu.make_async_remote_copy(
            src_ref=src(a, _slot(peer)), dst_ref=outs[a].at[_slot(peer)],
            send_sem=send_sems.at[sem_row + a, k], recv_sem=recv_sems.at[sem_row + a, k],
            device_id=peer, device_id_type=MESH)

    def own(a):
        return pltpu.make_async_copy(src(a, me), outs[a].at[me], local_sems.at[sem_row + a, 0])

    def start():
        for k, peer in enumerate(peers):
            for a in range(n):
                send(a, k, peer).start()
        for a in range(n):
            own(a).start()

    def wait():
        for k, peer in enumerate(peers):
            for a in range(n):
                arrival(a, k, peer).wait_recv()
        for k, peer in enumerate(peers):
            for a in range(n):
                send(a, k, peer).wait_send()
        for a in range(n):
            own(a).wait()

    return start, None, wait


N_CHIPS = N_DEV // 2


def _pair_swap(ins, kept, received, send_sems, recv_sems, local_sems):
    n = len(ins)
    x, y, c = _mesh_place()
    sibling = (x, y, 1 - c)

    def send(a, q):
        return pltpu.make_async_remote_copy(
            src_ref=ins[a].at[2 * q + (1 - c)], dst_ref=received[a].at[q],
            send_sem=send_sems.at[a, q], recv_sem=recv_sems.at[a, q], device_id=sibling, device_id_type=MESH)

    def keep(a, q):
        return pltpu.make_async_copy(ins[a].at[2 * q + c], kept[a].at[q], local_sems.at[a, q])

    def start():
        for a in range(n):
            for q in range(N_CHIPS):
                send(a, q).start()
                keep(a, q).start()

    def wait():
        for a in range(n):
            for q in range(N_CHIPS):
                send(a, q).wait_recv()
        for a in range(n):
            for q in range(N_CHIPS):
                send(a, q).wait_send()
                keep(a, q).wait()

    return start, None, wait


def _quad_exchange(ins, outs, send_sems, recv_sems, local_sems, sem_row=0):
    n = len(ins)
    x, y, c = _mesh_place()
    mine = 2 * x + y
    chips = [(1 - x, y), (x, 1 - y), (1 - x, 1 - y)]

    def send(a, k, chip):
        return pltpu.make_async_remote_copy(
            src_ref=ins[a].at[2 * chip[0] + chip[1]], dst_ref=outs[a].at[mine],
            send_sem=send_sems.at[sem_row + a, k], recv_sem=recv_sems.at[sem_row + a, k],
            device_id=(chip[0], chip[1], c), device_id_type=MESH)

    def arrival(a, k, chip):
        return pltpu.make_async_remote_copy(
            src_ref=ins[a].at[2 * chip[0] + chip[1]], dst_ref=outs[a].at[2 * chip[0] + chip[1]],
            send_sem=send_sems.at[sem_row + a, k], recv_sem=recv_sems.at[sem_row + a, k],
            device_id=(chip[0], chip[1], c), device_id_type=MESH)

    def own(a):
        return pltpu.make_async_copy(ins[a].at[mine], outs[a].at[mine], local_sems.at[sem_row + a, 0])

    def start():
        for k, chip in enumerate(chips):
            for a in range(n):
                send(a, k, chip).start()
        for a in range(n):
            own(a).start()

    def wait():
        for k, chip in enumerate(chips):
            for a in range(n):
                arrival(a, k, chip).wait_recv()
        for k, chip in enumerate(chips):
            for a in range(n):
                send(a, k, chip).wait_send()
        for a in range(n):
            own(a).wait()

    return start, None, wait


def _relay_gather(ins, outs, send_sems, recv_sems, local_sems):
    n = len(ins)
    x, y, c = _mesh_place()
    me, sibling = (x, y, c), (x, y, 1 - c)
    chips = [(1 - x, y), (x, 1 - y), (1 - x, 1 - y)]

    def copy(a, k, block, to, src=None):
        dst = outs[a].at[_slot(block)]
        return pltpu.make_async_remote_copy(
            src_ref=dst if src is None else src, dst_ref=dst,
            send_sem=send_sems.at[a, k], recv_sem=recv_sems.at[a, k], device_id=to, device_id_type=MESH)

    def own(a):
        return pltpu.make_async_copy(ins[a], outs[a].at[_slot(me)], local_sems.at[a, 0])

    def start():
        for j, chip in enumerate(chips):
            for a in range(n):
                copy(a, 1 + j, me, (*chip, c), src=ins[a]).start()
        for a in range(n):
            copy(a, 0, me, sibling, src=ins[a]).start()
            own(a).start()

    def relay():
        for j, chip in enumerate(chips):
            for a in range(n):
                copy(a, 1 + j, (*chip, c), me).wait_recv()
                copy(a, 4 + j, (*chip, c), sibling).start()

    def wait():
        for a in range(n):
            copy(a, 0, sibling, me).wait_recv()
        for j, chip in enumerate(chips):
            for a in range(n):
                copy(a, 4 + j, (*chip, 1 - c), me).wait_recv()
        for j, chip in enumerate(chips):
            for a in range(n):
                copy(a, 1 + j, me, (*chip, c), src=ins[a]).wait_send()
                copy(a, 4 + j, (*chip, c), sibling).wait_send()
        for a in range(n):
            copy(a, 0, me, sibling, src=ins[a]).wait_send()
            own(a).wait()

    return start, relay, wait


def _side_out_shapes(kind, arrays):
    if kind in ("gather", "relay_gather"):
        return [jax.ShapeDtypeStruct((N_DEV,) + a.shape, a.dtype) for a in arrays]
    if kind == "pair_swap":
        return [jax.ShapeDtypeStruct((N_CHIPS,) + a.shape[1:], a.dtype) for a in arrays] * 2
    return [jax.ShapeDtypeStruct(a.shape, a.dtype) for a in arrays]


def _hosted_call(body, name, grid, in_specs, out_specs, out_shape, scratch_shapes, args, side=None):
    if side is None:
        outs = pl.pallas_call(
            body, name=name, grid=grid, in_specs=in_specs, out_specs=out_specs, out_shape=out_shape,
            scratch_shapes=scratch_shapes, compiler_params=_params(len(grid)))(*args)
        return outs, []
    kind, arrays = side
    side_shapes = _side_out_shapes(kind, arrays)
    n_in, n_out, n_scr, n_side = len(in_specs), len(out_specs), len(scratch_shapes), len(arrays)

    def hosted(*refs):
        pos = 0
        groups = []
        for size in (n_in, n_side, n_out, len(side_shapes), n_scr):
            groups.append(refs[pos:pos + size])
            pos += size
        ins, side_in, outs, side_out, scr = groups
        send_sems, recv_sems, local_sems = refs[pos:]
        ids = [pl.program_id(d) for d in range(len(grid))]
        is_first = functools.reduce(jnp.logical_and, [i == 0 for i in ids])
        is_last = functools.reduce(jnp.logical_and, [i == g - 1 for i, g in zip(ids, grid)])
        if kind == "relay_gather":
            start, relay, wait = _relay_gather(side_in, side_out, send_sems, recv_sems, local_sems)
        elif kind == "pair_swap":
            start, relay, wait = _pair_swap(side_in, side_out[:n_side], side_out[n_side:], send_sems, recv_sems,
                                            local_sems)
        elif kind == "quad_exchange":
            start, relay, wait = _quad_exchange(side_in, side_out, send_sems, recv_sems, local_sems)
        else:
            start, relay, wait = _side_copies(kind, side_in, side_out, send_sems, recv_sems, local_sems)
        pl.when(is_first)(start)
        if relay is not None:
            pl.when(is_last)(relay)
        body(*ins, *outs, *scr)
        pl.when(is_last)(wait)

    any_spec = pl.BlockSpec(memory_space=pl.ANY)
    outs = pl.pallas_call(
        hosted, name=name, grid=grid,
        in_specs=list(in_specs) + [any_spec] * n_side,
        out_specs=list(out_specs) + [any_spec] * len(side_shapes),
        out_shape=list(out_shape) + side_shapes,
        scratch_shapes=list(scratch_shapes) + [pltpu.SemaphoreType.DMA((n_side, 7)), pltpu.SemaphoreType.DMA((n_side, 7)),
                                               pltpu.SemaphoreType.DMA((n_side, N_CHIPS))],
        compiler_params=_params(len(grid)))(*args, *arrays)
    return outs[:n_out], outs[n_out:]


def _lane_chunks(width, chunk=2 * LANES):
    return [slice(n0, min(n0 + chunk, width)) for n0 in range(0, width, chunk)]


def _pipelined(chunks, first, middle, last):
    n = len(chunks)
    a, b, total = {}, {}, None
    for step in range(n + 2):
        if step < n:
            a[step] = first(chunks[step])
        if 0 <= step - 1 < n:
            b[step - 1] = middle(chunks[step - 1], a.pop(step - 1))
        if 0 <= step - 2 < n:
            part = last(chunks[step - 2], b.pop(step - 2))
            total = part if total is None else total + part
    return total


def _ffn_fwd(h, g_pre, g_post, w_gu, w_down, name, side=None):
    T, D = h.shape
    nj = w_gu.shape[0] // 2
    FB = w_gu.shape[1]
    tm = TOKEN_TILE

    def body(h_ref, gpre_ref, gpost_ref, wg_ref, wu_ref, wd_ref, hout_ref, f_ref, a_ref, gu_ref, a_scr, acc):
        j = pl.program_id(1)

        @pl.when(j == 0)
        def _():
            x = h_ref[...]
            a = (x * _rstd(x) * gpre_ref[...]).astype(BF16)
            a_scr[...] = a
            a_ref[...] = a
            acc[...] = jnp.zeros_like(acc)

        a = a_scr[...]
        g = _mm_nt(a, wg_ref[...])
        u = _mm_nt(a, wu_ref[...])
        gu_ref[0] = g.astype(BF16)
        gu_ref[1] = u.astype(BF16)
        hh = (g * jax.nn.sigmoid(g) * u).astype(BF16)
        acc[...] += _mm(hh, wd_ref[...])

        @pl.when(j == nj - 1)
        def _():
            f = acc[...]
            f_ref[...] = f
            hout_ref[...] = h_ref[...] + 0.5 * (f * _rstd(f) * gpost_ref[...])

    return _hosted_call(
        body, name, (T // tm, nj),
        in_specs=[
            pl.BlockSpec((tm, D), lambda i, j: (i, 0)),
            _full((1, D)), _full((1, D)),
            pl.BlockSpec((None, FB, D), lambda i, j: (j, 0, 0)),
            pl.BlockSpec((None, FB, D), lambda i, j: (j + nj, 0, 0)),
            pl.BlockSpec((FB, D), lambda i, j: (j, 0)),
        ],
        out_specs=[
            pl.BlockSpec((tm, D), lambda i, j: (i, 0)),
            pl.BlockSpec((tm, D), lambda i, j: (i, 0)),
            pl.BlockSpec((tm, D), lambda i, j: (i, 0)),
            pl.BlockSpec((None, 2, tm, FB), lambda i, j: (j, 0, i, 0)),
        ],
        out_shape=[
            jax.ShapeDtypeStruct((T, D), F32),
            jax.ShapeDtypeStruct((T, D), F32),
            jax.ShapeDtypeStruct((T, D), BF16),
            jax.ShapeDtypeStruct((nj, 2, T, FB), BF16),
        ],
        scratch_shapes=[pltpu.VMEM((tm, D), BF16), pltpu.VMEM((tm, D), F32)],
        args=(h, g_pre, g_post, w_gu, w_gu, w_down), side=side)


def _ffn_bwd(dh_out, f, g_post, h, g_pre, gu, w_gu, w_down, name, side=None):
    T, D = h.shape
    nj = w_gu.shape[0] // 2
    FB = w_gu.shape[1]
    tm = TOKEN_TILE

    def body(dho_ref, f_ref, gpost_ref, h_ref, gpre_ref, gu_ref, wg_ref, wu_ref, wd_ref,
             dhin_ref, df_ref, hh_ref, dgu_ref, dgpost_ref, dgpre_ref, df_scr, da):
        i, j = pl.program_id(0), pl.program_id(1)

        @pl.when(jnp.logical_and(i == 0, j == 0))
        def _():
            dgpost_ref[...] = jnp.zeros_like(dgpost_ref)
            dgpre_ref[...] = jnp.zeros_like(dgpre_ref)

        @pl.when(j == 0)
        def _():
            fv = f_ref[...]
            df, dgain = _rms_bwd(fv, _rstd(fv), gpost_ref[...], 0.5 * dho_ref[...])
            dgpost_ref[...] += _colsum(dgain)
            dfb = df.astype(BF16)
            df_scr[...] = dfb
            df_ref[...] = dfb
            da[...] = jnp.zeros_like(da)

        dfb = df_scr[...]

        halves = (slice(0, tm // 2), slice(tm // 2, tm))

        def hidden_grad(c):
            return [_mm_nt(dfb[rows], wd_ref[c, :]) for rows in halves]

        def through_swiglu(c, dhh):
            dhh = jnp.concatenate(dhh, axis=0)
            g = gu_ref[0, :, c].astype(F32)
            u = gu_ref[1, :, c].astype(F32)
            sg = jax.nn.sigmoid(g)
            silu = g * sg
            hh_ref[:, c] = (silu * u).astype(BF16)
            dg = (dhh * u * (sg * (1.0 + (g - silu)))).astype(BF16)
            du = (dhh * silu).astype(BF16)
            dgu_ref[0, :, c] = dg
            dgu_ref[1, :, c] = du
            return dg, du

        def input_grad(c, dgu):
            return jnp.concatenate(
                [_mm(dgu[0][rows], wg_ref[c, :]) + _mm(dgu[1][rows], wu_ref[c, :]) for rows in halves], axis=0)

        da[...] += _pipelined(_lane_chunks(FB), hidden_grad, through_swiglu, input_grad)

        @pl.when(j == nj - 1)
        def _():
            x = h_ref[...]
            dx, dgain = _rms_bwd(x, _rstd(x), gpre_ref[...], da[...])
            dgpre_ref[...] += _colsum(dgain)
            dhin_ref[...] = dho_ref[...] + dx

    tile = pl.BlockSpec((tm, D), lambda i, j: (i, 0))
    return _hosted_call(
        body, name, (T // tm, nj),
        in_specs=[
            tile, tile, _full((1, D)), tile, _full((1, D)),
            pl.BlockSpec((None, 2, tm, FB), lambda i, j: (j, 0, i, 0)),
            pl.BlockSpec((None, FB, D), lambda i, j: (j, 0, 0)),
            pl.BlockSpec((None, FB, D), lambda i, j: (j + nj, 0, 0)),
            pl.BlockSpec((FB, D), lambda i, j: (j, 0)),
        ],
        out_specs=[
            tile, tile,
            pl.BlockSpec((None, tm, FB), lambda i, j: (j, i, 0)),
            pl.BlockSpec((None, 2, tm, FB), lambda i, j: (j, 0, i, 0)),
            _full((1, D)), _full((1, D)),
        ],
        out_shape=[
            jax.ShapeDtypeStruct((T, D), F32),
            jax.ShapeDtypeStruct((T, D), BF16),
            jax.ShapeDtypeStruct((nj, T, FB), BF16),
            jax.ShapeDtypeStruct((nj, 2, T, FB), BF16),
            jax.ShapeDtypeStruct((1, D), F32),
            jax.ShapeDtypeStruct((1, D), F32),
        ],
        scratch_shapes=[pltpu.VMEM((tm, D), BF16), pltpu.VMEM((tm, D), F32)],
        args=(dh_out, f, g_post, h, g_pre, gu, w_gu, w_gu, w_down), side=side)


def _tn_matmul(x, y, x_spec, y_spec, out_shape, out_spec, n_blocks, n_steps, acc_shape, name, side=None):
    def body(x_ref, y_ref, o_ref, acc):
        t = pl.program_id(1)

        @pl.when(t == 0)
        def _():
            acc[...] = jnp.zeros_like(acc)

        acc[...] += _mm_tn(x_ref[...].astype(BF16), y_ref[...].astype(BF16))

        @pl.when(t == n_steps - 1)
        def _():
            o_ref[...] = acc[...].astype(o_ref.dtype)

    outs, side_outs = _hosted_call(
        body, name, (n_blocks, n_steps), in_specs=[x_spec, y_spec], out_specs=[out_spec], out_shape=[out_shape],
        scratch_shapes=[pltpu.VMEM(acc_shape, F32)], args=(x, y), side=side)
    return (outs[0], side_outs) if side is not None else outs[0]


def _inproj_fwd(h, g_pre, w_in, b_in, side=None):
    T, D = h.shape
    tm = TOKEN_TILE

    def body(h_ref, g_ref, w_ref, b_ref, z_ref, a_ref):
        x = h_ref[...]
        a = (x * _rstd(x) * g_ref[...]).astype(BF16)
        a_ref[...] = a
        z_ref[...] = _mm_nt(a, w_ref[...]) + b_ref[...]

    return _hosted_call(
        body, "inproj_fwd", (T // tm,),
        in_specs=[pl.BlockSpec((tm, D), lambda i: (i, 0)), _full((1, D)), _full((D_IN, D)), _full((1, D_IN))],
        out_specs=[pl.BlockSpec((tm, D_IN), lambda i: (i, 0)), pl.BlockSpec((tm, D), lambda i: (i, 0))],
        out_shape=[jax.ShapeDtypeStruct((T, D_IN), F32), jax.ShapeDtypeStruct((T, D), BF16)],
        scratch_shapes=[], args=(h, g_pre, w_in, b_in), side=side)


def _inproj_bwd(dqa, dka, dva, dqb, dkb, dvb, w_in, h, g_pre, dres, side=None):
    T, D = h.shape
    tm = TOKEN_TILE

    def body(dqa_ref, dka_ref, dva_ref, dqb_ref, dkb_ref, dvb_ref, w_ref, h_ref, g_ref, dres_ref,
             dh_ref, dz_ref, dbin_ref, dg_ref):
        i = pl.program_id(0)

        @pl.when(i == 0)
        def _():
            dbin_ref[...] = jnp.zeros_like(dbin_ref)
            dg_ref[...] = jnp.zeros_like(dg_ref)

        dz = jnp.concatenate([dqa_ref[...], dka_ref[...], dva_ref[...], dqb_ref[...], dkb_ref[...], dvb_ref[...]],
                             axis=1)
        dbin_ref[...] += _colsum(dz)
        dzb = dz.astype(BF16)
        dz_ref[...] = dzb
        da = _mm(dzb, w_ref[...])
        x = h_ref[...]
        dx, dgain = _rms_bwd(x, _rstd(x), g_ref[...], da)
        dg_ref[...] += _colsum(dgain)
        dh_ref[...] = dres_ref[...] + dx

    def tile(w):
        return pl.BlockSpec((tm, w), lambda i: (i, 0))

    return _hosted_call(
        body, "inproj_bwd", (T // tm,),
        in_specs=[tile(A_Q), tile(A_KV), tile(A_KV), tile(B_W), tile(B_W), tile(B_W),
                  _full((D_IN, D)), tile(D), _full((1, D)), tile(D)],
        out_specs=[tile(D), tile(D_IN), _full((1, D_IN)), _full((1, D))],
        out_shape=[jax.ShapeDtypeStruct((T, D), F32), jax.ShapeDtypeStruct((T, D_IN), BF16),
                   jax.ShapeDtypeStruct((1, D_IN), F32), jax.ShapeDtypeStruct((1, D), F32)],
        scratch_shapes=[], args=(dqa, dka, dva, dqb, dkb, dvb, w_in, h, g_pre, dres), side=side)


def _bucket_tiles(patterns):
    i = np.arange(QBLK)[:, None]
    j = np.arange(2 * QBLK)[None, :]
    dist = QBLK + i - j
    max_exact = NUM_BUCKETS // 2
    tiles = []
    for dilation, max_dist in patterns:
        n = np.maximum(dist * dilation, 0)
        nf = np.maximum(n, 1).astype(np.float32)
        large = max_exact + (np.log(nf / np.float32(max_exact)) / np.float32(math.log(MAX_DISTANCE / max_exact))
                             * np.float32(NUM_BUCKETS - max_exact)).astype(np.int32)
        bucket = np.where(n < max_exact, n, np.minimum(large, NUM_BUCKETS - 1))
        tiles.append(np.where((dist >= 0) & (dist <= max_dist), bucket, -1))
    return jnp.asarray(np.stack(tiles).astype(np.int32))


def _bias_build(rel_bias, buckets, head0, name, side=None):
    n = buckets.shape[0]

    def body(bk_ref, rb_ref, o_ref):
        bk = bk_ref[...]
        base = jnp.where(bk < 0, NEG_INF, 0.0).astype(F32)
        for hd in range(N_HEAD_GROUP):
            o_ref[hd] = lax.fori_loop(
                0, NUM_BUCKETS, lambda b, acc, hd=hd: jnp.where(bk == b, rb_ref[b, head0 + hd], acc), base)

    outs, side_outs = _hosted_call(
        body, name, (n,),
        in_specs=[pl.BlockSpec((None, QBLK, 2 * QBLK), lambda p: (p, 0, 0)), pl.BlockSpec(memory_space=pltpu.SMEM)],
        out_specs=[pl.BlockSpec((None, N_HEAD_GROUP, QBLK, 2 * QBLK), lambda p: (p, 0, 0, 0))],
        out_shape=[jax.ShapeDtypeStruct((n, N_HEAD_GROUP, QBLK, 2 * QBLK), F32)],
        scratch_shapes=[], args=(buckets, rel_bias), side=side)
    return outs[0], side_outs


def _bias_grad(ds, buckets, name):
    n = buckets.shape[0]

    def body(ds_ref, bk_ref, o_ref):
        bk = bk_ref[...]
        row = lax.broadcasted_iota(jnp.int32, (NUM_BUCKETS, 2 * QBLK), 0)
        for hd in range(N_HEAD_GROUP):
            d = ds_ref[hd]
            per_key = jnp.zeros((NUM_BUCKETS, 2 * QBLK), F32)
            for b in range(NUM_BUCKETS):
                per_key = jnp.where(row == b, jnp.sum(jnp.where(bk == b, d, 0.0), axis=0, keepdims=True), per_key)
            o_ref[hd] = jnp.broadcast_to(jnp.sum(per_key, axis=1, keepdims=True), (NUM_BUCKETS, LANES))

    out = pl.pallas_call(
        body, name=name, grid=(n,),
        in_specs=[pl.BlockSpec((None, N_HEAD_GROUP, QBLK, 2 * QBLK), lambda p: (p, 0, 0, 0)),
                  pl.BlockSpec((None, QBLK, 2 * QBLK), lambda p: (p, 0, 0))],
        out_specs=pl.BlockSpec((None, N_HEAD_GROUP, NUM_BUCKETS, LANES), lambda p: (p, 0, 0, 0)),
        out_shape=jax.ShapeDtypeStruct((n, N_HEAD_GROUP, NUM_BUCKETS, LANES), F32),
        compiler_params=_params(1),
    )(ds, buckets)
    return out[:, :, :, 0].reshape(n * N_HEAD_GROUP, NUM_BUCKETS)


def _class_rows(start, dilation):
    if dilation == 1:
        return pl.ds(pl.multiple_of(start, QBLK), QBLK)
    return pl.ds(start, QBLK, stride=dilation)


def _block_starts(idx, n_blocks, dilation):
    cls = idx // n_blocks
    n = idx % n_blocks
    cur = cls + dilation * QBLK * n
    prev = cls + dilation * QBLK * jnp.maximum(n - 1, 0)
    return n, cur, prev


class _HeadPair:
    def __init__(self, g, shared_kv):
        self.lane = lax.broadcasted_iota(jnp.int32, (1, LANES), 1)
        self.lower = self.lane < HEAD_DIM
        self.shared_kv = shared_kv
        self.key_lanes = (self.lane >= HEAD_DIM).astype(jnp.int32) == (g // 2)

    def stack(self, t):
        return jnp.concatenate([jnp.where(self.lower, t, 0.0), jnp.where(self.lower, 0.0, t)], axis=0).astype(BF16)

    def unstack(self, t2):
        return jnp.where(self.lower, t2[:QBLK], t2[QBLK:])

    def keys(self, t):
        if self.shared_kv:
            return jnp.where(self.key_lanes, t, pltpu.roll(t, HEAD_DIM, 1))
        return t

    def key_grads(self, t):
        if self.shared_kv:
            return jnp.where(self.key_lanes, t + pltpu.roll(t, HEAD_DIM, 1), 0.0)
        return t


def _attn_specs(T, qcol, kcol, vcol, shared_kv):
    kv = (lambda c: (lambda g: (0, c))) if shared_kv else (lambda c: (lambda g: (0, c + g)))
    return [pl.BlockSpec((T, LANES), lambda g: (0, qcol + g)),
            pl.BlockSpec((T, LANES), kv(kcol)),
            pl.BlockSpec((T, LANES), kv(vcol))]


def _attn_fwd(z, bias, sinks, patterns, qcol, kcol, vcol, shared_kv, name, side=None):
    T = z.shape[0]
    n_pat = len(patterns)
    has_sink = sinks is not None

    def body(*refs):
        if has_sink:
            sink_ref, refs = refs[0], refs[1:]
        q_ref, k_ref, v_ref, b_ref, o_ref, l_ref = refs[:6]
        po_scr = refs[6:6 + n_pat]
        pl_scr = refs[6 + n_pat:]
        g = pl.program_id(0)
        heads = _HeadPair(g, shared_kv)
        in_prev = lax.broadcasted_iota(jnp.int32, (2 * QBLK, 2 * QBLK), 1) < QBLK

        for pi, (dilation, _) in enumerate(patterns):
            n_blocks = T // (QBLK * dilation)

            def step(it, carry, pi=pi, dilation=dilation, n_blocks=n_blocks):
                blocks = []
                for u in range(FWD_BLOCKS):
                    n, cur, prev = _block_starts(it * FWD_BLOCKS + u, n_blocks, dilation)
                    rows_c, rows_p = _class_rows(cur, dilation), _class_rows(prev, dilation)
                    qm = heads.stack(q_ref[rows_c, :])
                    k2 = heads.keys(jnp.concatenate([k_ref[rows_p, :], k_ref[rows_c, :]], axis=0)).astype(BF16)
                    v2 = heads.keys(jnp.concatenate([v_ref[rows_p, :], v_ref[rows_c, :]], axis=0)).astype(BF16)
                    blocks.append(dict(n=n, rows=rows_c, v2=v2, s=_mm_nt(qm, k2)))
                for b in blocks:
                    s = b["s"] * (HEAD_DIM ** -0.5) + b_ref[pi]
                    b["s"] = jnp.where(jnp.logical_and(in_prev, b["n"] == 0), NEG_INF, s)
                    b["m"] = jnp.max(b["s"], axis=1, keepdims=True)
                for b in blocks:
                    b["pr"] = jnp.exp(b["s"] - b["m"])
                    b["den"] = jnp.sum(b["pr"], axis=1, keepdims=True)
                for b in blocks:
                    b["o2"] = _mm(b["pr"].astype(BF16), b["v2"])
                for b in blocks:
                    lse = b["m"] + jnp.log(b["den"])
                    po_scr[pi][b["rows"], :] = heads.unstack(b["o2"] / b["den"])
                    pl_scr[2 * pi][b["rows"], :] = jnp.broadcast_to(lse[:QBLK], (QBLK, LANES))
                    pl_scr[2 * pi + 1][b["rows"], :] = jnp.broadcast_to(lse[QBLK:], (QBLK, LANES))
                return carry

            lax.fori_loop(0, (dilation * n_blocks) // FWD_BLOCKS, step, 0)

        def merge(ci, carry):
            rows = pl.ds(pl.multiple_of(ci * QBLK, QBLK), QBLK)
            weights = []
            for hd in range(2):
                parts = [pl_scr[2 * pi + hd][rows, :] for pi in range(n_pat)]
                m = functools.reduce(jnp.maximum, parts)
                if has_sink:
                    sink = sink_ref[0, 2 * g + hd]
                    m = jnp.maximum(m, sink)
                den = functools.reduce(jnp.add, [jnp.exp(x - m) for x in parts])
                if has_sink:
                    den = den + jnp.exp(sink - m)
                lse = m + jnp.log(den)
                l_ref[hd, rows, :] = lse
                weights.append([jnp.exp(x - lse) for x in parts])
            o_ref[rows, :] = functools.reduce(
                jnp.add, [jnp.where(heads.lower, weights[0][pi], weights[1][pi]) * po_scr[pi][rows, :]
                          for pi in range(n_pat)])
            return carry

        lax.fori_loop(0, T // QBLK, merge, 0)

    in_specs = _attn_specs(T, qcol, kcol, vcol, shared_kv)
    in_specs.append(pl.BlockSpec((n_pat, None, 2 * QBLK, 2 * QBLK), lambda g: (0, g, 0, 0)))
    args = [z, z, z, bias.reshape(n_pat, N_HEAD_GROUP // 2, 2 * QBLK, 2 * QBLK)]
    if has_sink:
        in_specs.insert(0, pl.BlockSpec(memory_space=pltpu.SMEM))
        args.insert(0, sinks)
    return _hosted_call(
        body, name, (N_HEAD_GROUP // 2,),
        in_specs=in_specs,
        out_specs=[pl.BlockSpec((T, LANES), lambda g: (0, g)), pl.BlockSpec((2, T, LANES), lambda g: (g, 0, 0))],
        out_shape=[jax.ShapeDtypeStruct((T, N_HEAD_GROUP * HEAD_DIM), F32),
                   jax.ShapeDtypeStruct((N_HEAD_GROUP, T, LANES), F32)],
        scratch_shapes=[pltpu.VMEM((T, LANES), F32)] * (3 * n_pat), args=args, side=side)


def _attn_bwd(z, bias, sinks, d_out, out, lse, patterns, qcol, kcol, vcol, shared_kv, name, side=None):
    T = z.shape[0]
    n_pat = len(patterns)
    has_sink = sinks is not None
    kv_width = LANES if shared_kv else N_HEAD_GROUP * HEAD_DIM

    def body(*refs):
        if has_sink:
            sink_ref, refs = refs[0], refs[1:]
        q_ref, k_ref, v_ref, b_ref, do_ref, o_ref, l0_ref, l1_ref = refs[:8]
        dq_ref, dk_ref, dv_ref, ds_ref = refs[8:12]
        dsink_ref = refs[12] if has_sink else None
        dk_acc, dv_acc = refs[-2:]
        g = pl.program_id(0)
        heads = _HeadPair(g, shared_kv)
        in_prev = lax.broadcasted_iota(jnp.int32, (2 * QBLK, 2 * QBLK), 1) < QBLK

        dq_ref[...] = jnp.zeros_like(dq_ref)
        ds_ref[...] = jnp.zeros_like(ds_ref)
        dk_acc[...] = jnp.zeros_like(dk_acc)
        dv_acc[...] = jnp.zeros_like(dv_acc)

        dsink = jnp.zeros((1, LANES), F32)
        for pi, (dilation, _) in enumerate(patterns):
            n_blocks = T // (QBLK * dilation)

            def step(idx, dsink, pi=pi, dilation=dilation, n_blocks=n_blocks):
                blocks = []
                for u in range(BWD_BLOCKS):
                    n, cur, prev = _block_starts(idx * BWD_BLOCKS + u, n_blocks, dilation)
                    rows_c, rows_p = _class_rows(cur, dilation), _class_rows(prev, dilation)
                    qm = heads.stack(q_ref[rows_c, :])
                    k2 = heads.keys(jnp.concatenate([k_ref[rows_p, :], k_ref[rows_c, :]], axis=0)).astype(BF16)
                    v2 = heads.keys(jnp.concatenate([v_ref[rows_p, :], v_ref[rows_c, :]], axis=0)).astype(BF16)
                    d_o = do_ref[rows_c, :]
                    dom = heads.stack(d_o)
                    dd = d_o * o_ref[rows_c, :]
                    delta = jnp.concatenate([jnp.sum(jnp.where(heads.lower, dd, 0.0), axis=1, keepdims=True),
                                             jnp.sum(jnp.where(heads.lower, 0.0, dd), axis=1, keepdims=True)], axis=0)
                    lse = jnp.concatenate([l0_ref[rows_c, :], l1_ref[rows_c, :]], axis=0)
                    blocks.append(dict(n=n, rows_c=rows_c, rows_p=rows_p, qm=qm, k2=k2, dom=dom, delta=delta, lse=lse,
                                       s=_mm_nt(qm, k2), dp=_mm_nt(dom, v2)))
                for b in blocks:
                    s = b["s"] * (HEAD_DIM ** -0.5) + b_ref[pi]
                    s = jnp.where(jnp.logical_and(in_prev, b["n"] == 0), NEG_INF, s)
                    b["pr"] = jnp.exp(s - jnp.concatenate([b["lse"], b["lse"]], axis=1))
                    b["ds"] = b["pr"] * (b["dp"] - b["delta"])
                for b in blocks:
                    dsb = b["ds"].astype(BF16)
                    b["dq2"] = _mm(dsb, b["k2"])
                    b["dk2"] = _mm_tn(dsb, b["qm"])
                    b["dv2"] = _mm_tn(b["pr"].astype(BF16), b["dom"])
                for b in blocks:
                    ds_ref[pi] += b["ds"]
                    dq_ref[b["rows_c"], :] += heads.unstack(b["dq2"]) * (HEAD_DIM ** -0.5)
                    dk2 = heads.key_grads(b["dk2"]) * (HEAD_DIM ** -0.5)
                    dv2 = heads.key_grads(b["dv2"])
                    dk_acc[b["rows_p"], :] += dk2[:QBLK]
                    dk_acc[b["rows_c"], :] += dk2[QBLK:]
                    dv_acc[b["rows_p"], :] += dv2[:QBLK]
                    dv_acc[b["rows_c"], :] += dv2[QBLK:]
                    if has_sink:
                        for hd in range(2):
                            rows_h = slice(QBLK * hd, QBLK * (hd + 1))
                            p_sink = jnp.exp(sink_ref[0, 2 * g + hd] - b["lse"][rows_h, 0:1])
                            dsink = dsink - jnp.where(heads.lane == 2 * g + hd,
                                                      jnp.sum(p_sink * b["delta"][rows_h]), 0.0)
                return dsink

            dsink = lax.fori_loop(0, (dilation * n_blocks) // BWD_BLOCKS, step, dsink)

        if shared_kv:
            @pl.when(g == 0)
            def _():
                dk_ref[...] = dk_acc[...]
                dv_ref[...] = dv_acc[...]

            @pl.when(g != 0)
            def _():
                dk_ref[...] += dk_acc[...]
                dv_ref[...] += dv_acc[...]
        else:
            dk_ref[...] = dk_acc[...]
            dv_ref[...] = dv_acc[...]

        if has_sink:
            @pl.when(g == 0)
            def _():
                dsink_ref[...] = dsink

            @pl.when(g != 0)
            def _():
                dsink_ref[...] += dsink

    pair = pl.BlockSpec((T, LANES), lambda g: (0, g))
    stacked = pl.BlockSpec((n_pat, None, 2 * QBLK, 2 * QBLK), lambda g: (0, g, 0, 0))
    stacked_shape = (n_pat, N_HEAD_GROUP // 2, 2 * QBLK, 2 * QBLK)
    in_specs = _attn_specs(T, qcol, kcol, vcol, shared_kv)
    in_specs += [stacked, pair, pair,
                 pl.BlockSpec((None, T, LANES), lambda g: (2 * g, 0, 0)),
                 pl.BlockSpec((None, T, LANES), lambda g: (2 * g + 1, 0, 0))]
    args = [z, z, z, bias.reshape(stacked_shape), d_out, out, lse, lse]
    kv_out = _full((T, LANES)) if shared_kv else pair
    out_specs = [pair, kv_out, kv_out, stacked]
    out_shape = [jax.ShapeDtypeStruct((T, N_HEAD_GROUP * HEAD_DIM), F32),
                 jax.ShapeDtypeStruct((T, kv_width), F32), jax.ShapeDtypeStruct((T, kv_width), F32),
                 jax.ShapeDtypeStruct(stacked_shape, F32)]
    if has_sink:
        in_specs.insert(0, pl.BlockSpec(memory_space=pltpu.SMEM))
        args.insert(0, sinks)
        out_specs.append(_full((1, LANES)))
        out_shape.append(jax.ShapeDtypeStruct((1, LANES), F32))
    outs, side_outs = _hosted_call(
        body, name, (N_HEAD_GROUP // 2,), in_specs=in_specs, out_specs=out_specs, out_shape=out_shape,
        scratch_shapes=[pltpu.VMEM((T, LANES), F32), pltpu.VMEM((T, LANES), F32)], args=args, side=side)
    outs = list(outs)
    outs[3] = outs[3].reshape(n_pat, N_HEAD_GROUP, QBLK, 2 * QBLK)
    return outs, side_outs


def _outproj_fwd(mix_a, mix_b, w_out, b_out, g_post, h):
    T, D = h.shape
    tm = TOKEN_TILE
    d_mix = w_out.shape[0]

    def body(ma_ref, mb_ref, w_ref, b_ref, g_ref, h_ref, att_ref, hout_ref, mix_ref):
        mix = jnp.concatenate([ma_ref[...], mb_ref[...]], axis=1).astype(BF16)
        mix_ref[...] = mix
        att = _mm(mix, w_ref[...]) + b_ref[...]
        att_ref[...] = att
        hout_ref[...] = h_ref[...] + att * _rstd(att) * g_ref[...]

    def tile(w):
        return pl.BlockSpec((tm, w), lambda i: (i, 0))

    return pl.pallas_call(
        body, name="outproj_fwd", grid=(T // tm,),
        in_specs=[tile(A_Q), tile(B_W), _full((d_mix, D)), _full((1, D)), _full((1, D)), tile(D)],
        out_specs=[tile(D), tile(D), tile(d_mix)],
        out_shape=[jax.ShapeDtypeStruct((T, D), F32), jax.ShapeDtypeStruct((T, D), F32),
                   jax.ShapeDtypeStruct((T, d_mix), BF16)],
        compiler_params=_params(1),
    )(mix_a, mix_b, w_out, b_out, g_post, h)


def _outproj_bwd(dh, att, g_post, w_out):
    T, D = dh.shape
    tm = TOKEN_TILE
    d_mix = w_out.shape[0]

    def body(dh_ref, att_ref, g_ref, w_ref, dma_ref, dmb_ref, datt_ref, dg_ref, db_ref):
        i = pl.program_id(0)

        @pl.when(i == 0)
        def _():
            dg_ref[...] = jnp.zeros_like(dg_ref)
            db_ref[...] = jnp.zeros_like(db_ref)

        att = att_ref[...]
        datt, dgain = _rms_bwd(att, _rstd(att), g_ref[...], dh_ref[...])
        dg_ref[...] += _colsum(dgain)
        db_ref[...] += _colsum(datt)
        dattb = datt.astype(BF16)
        datt_ref[...] = dattb
        dmix = _mm_nt(dattb, w_ref[...])
        dma_ref[...] = dmix[:, :A_Q]
        dmb_ref[...] = dmix[:, A_Q:]

    def tile(w):
        return pl.BlockSpec((tm, w), lambda i: (i, 0))

    return pl.pallas_call(
        body, name="outproj_bwd", grid=(T // tm,),
        in_specs=[tile(D), tile(D), _full((1, D)), _full((d_mix, D))],
        out_specs=[tile(A_Q), tile(B_W), tile(D), _full((1, D)), _full((1, D))],
        out_shape=[jax.ShapeDtypeStruct((T, A_Q), F32), jax.ShapeDtypeStruct((T, B_W), F32),
                   jax.ShapeDtypeStruct((T, D), BF16), jax.ShapeDtypeStruct((1, D), F32),
                   jax.ShapeDtypeStruct((1, D), F32)],
        compiler_params=_params(1),
    )(dh, att, g_post, w_out)


def _ple_fwd_loss(h, g_pre, w_gate, p, w_proj, g_post, target):
    T, D = h.shape
    tm = TOKEN_TILE
    n_proj, ple, db = w_proj.shape

    def body(h_ref, gpre_ref, wg_ref, p_ref, wp_ref, gpost_ref, t_ref,
             a_ref, dpre_ref, de_ref, dh_ref, loss_ref, dgpost_ref):
        i = pl.program_id(0)

        @pl.when(i == 0)
        def _():
            loss_ref[...] = jnp.zeros_like(loss_ref)
            dgpost_ref[...] = jnp.zeros_like(dgpost_ref)

        x = h_ref[...]
        a = (x * _rstd(x) * gpre_ref[...]).astype(BF16)
        a_ref[...] = a
        gate = jax.nn.sigmoid(_mm(a, wg_ref[...]))
        pb = p_ref[...].astype(BF16)
        e = jnp.concatenate([_mm(pb, wp_ref[k]) for k in range(n_proj)], axis=1)
        ge = gate * e
        rg = _rstd(ge)
        diff = x + ge * rg * gpost_ref[...] - t_ref[...]
        loss_ref[...] += 0.5 * jnp.sum(jnp.mean(diff * diff, axis=1, keepdims=True))
        dy = diff * (1.0 / D)
        dh_ref[...] = dy
        dge, dgain = _rms_bwd(ge, rg, gpost_ref[...], dy)
        dgpost_ref[...] += _colsum(dgain)
        de_ref[...] = (dge * gate).astype(BF16)
        dpre_ref[...] = (dge * e * gate * (1.0 - gate)).astype(BF16)

    def tile(w):
        return pl.BlockSpec((tm, w), lambda i: (i, 0))

    return pl.pallas_call(
        body, name="ple_fwd_loss", grid=(T // tm,),
        in_specs=[tile(D), _full((1, D)), _full((D, D)), tile(ple), _full((n_proj, ple, db)), _full((1, D)), tile(D)],
        out_specs=[tile(D), tile(D), tile(D), tile(D), _full((1, LANES)), _full((1, D))],
        out_shape=[jax.ShapeDtypeStruct((T, D), BF16),
                   jax.ShapeDtypeStruct((T, D), BF16),
                   jax.ShapeDtypeStruct((T, D), BF16),
                   jax.ShapeDtypeStruct((T, D), F32),
                   jax.ShapeDtypeStruct((1, LANES), F32),
                   jax.ShapeDtypeStruct((1, D), F32)],
        compiler_params=_params(1),
    )(h, g_pre, w_gate, p, w_proj, g_post, target)


def _ple_bwd(dpre, w_gate, h, g_pre, dres):
    T, D = h.shape
    tm = TOKEN_TILE

    def body(dpre_ref, w_ref, h_ref, g_ref, dres_ref, dh_ref, dg_ref):
        i = pl.program_id(0)

        @pl.when(i == 0)
        def _():
            dg_ref[...] = jnp.zeros_like(dg_ref)

        da = _mm_nt(dpre_ref[...], w_ref[...])
        x = h_ref[...]
        dx, dgain = _rms_bwd(x, _rstd(x), g_ref[...], da)
        dg_ref[...] += _colsum(dgain)
        dh_ref[...] = dres_ref[...] + dx

    tile = pl.BlockSpec((tm, D), lambda i: (i, 0))
    return pl.pallas_call(
        body, name="ple_bwd", grid=(T // tm,),
        in_specs=[tile, _full((D, D)), tile, _full((1, D)), tile],
        out_specs=[tile, _full((1, D))],
        out_shape=[jax.ShapeDtypeStruct((T, D), F32), jax.ShapeDtypeStruct((1, D), F32)],
        compiler_params=_params(1),
    )(dpre, w_gate, h, g_pre, dres)


def _ple_dw_proj(p, de, n_proj):
    T, ple = p.shape
    D = de.shape[1]
    db = D // n_proj
    tk = TOKEN_TILE
    nt = T // tk

    def body(p_ref, de_ref, o_ref, acc):
        t = pl.program_id(0)

        @pl.when(t == 0)
        def _():
            acc[...] = jnp.zeros_like(acc)

        acc[...] += _mm_tn(p_ref[...].astype(BF16), de_ref[...])

        @pl.when(t == nt - 1)
        def _():
            for k in range(n_proj):
                o_ref[k] = acc[:, k * db:(k + 1) * db].astype(BF16)

    return pl.pallas_call(
        body, name="ple_dw_proj", grid=(nt,),
        in_specs=[pl.BlockSpec((tk, ple), lambda t: (t, 0)), pl.BlockSpec((tk, D), lambda t: (t, 0))],
        out_specs=_full((n_proj, ple, db)), out_shape=jax.ShapeDtypeStruct((n_proj, ple, db), BF16),
        scratch_shapes=[pltpu.VMEM((ple, D), F32)], compiler_params=_params(1),
    )(p, de)


def _tok(width):
    return pl.BlockSpec((DW_TILE, width), lambda b, t: (t, 0))


def _dw_gu(a, dgu, name, side=None):
    T, D = a.shape
    nj, _, _, FB = dgu.shape
    return _tn_matmul(
        dgu, a, pl.BlockSpec((None, None, DW_TILE, FB), lambda b, t: (b % nj, b // nj, t, 0)), _tok(D),
        jax.ShapeDtypeStruct((2 * nj, FB, D), BF16), pl.BlockSpec((None, FB, D), lambda b, t: (b, 0, 0)),
        2 * nj, T // DW_TILE, (FB, D), name, side=side)


def _dw_down(hh, df, name, side=None):
    nj, T, FB = hh.shape
    D = df.shape[1]
    return _tn_matmul(
        hh, df, pl.BlockSpec((None, DW_TILE, FB), lambda b, t: (b, t, 0)), _tok(D),
        jax.ShapeDtypeStruct((nj, FB, D), BF16), pl.BlockSpec((None, FB, D), lambda b, t: (b, 0, 0)),
        nj, T // DW_TILE, (FB, D), name, side=side)


def _dw_rows(xm, y, name, rows):
    T, k = xm.shape
    D = y.shape[1]
    out = _tn_matmul(
        xm, y, pl.BlockSpec((DW_TILE, rows), lambda b, t: (t, b)), _tok(D),
        jax.ShapeDtypeStruct((k, D), BF16), pl.BlockSpec((rows, D), lambda b, t: (b, 0)),
        k // rows, T // DW_TILE, (rows, D), name)
    return out.reshape(N_DEV, k // N_DEV, D)


def _cast_bf16(arrays):
    n = len(arrays)

    def body(*refs):
        for a in range(n):
            refs[n + a][...] = refs[a][...].astype(BF16)

    return pl.pallas_call(
        body, name="cast_shards",
        in_specs=[pl.BlockSpec(memory_space=pltpu.VMEM)] * n, out_specs=[pl.BlockSpec(memory_space=pltpu.VMEM)] * n,
        out_shape=[jax.ShapeDtypeStruct(a.shape, BF16) for a in arrays],
        compiler_params=pltpu.CompilerParams(vmem_limit_bytes=VMEM_LIMIT),
    )(*arrays)


def _pack_layout(D, n_rel_rows):
    n_bin = -(-D_IN // D)
    row_bin = len(GAINS)
    row_sink = row_bin + n_bin
    row_loss = row_sink + 1
    row_rb = -(-(row_loss + 1) // 8) * 8
    n_rows = row_rb + -(-n_rel_rows // 8) * 8
    bin_parts = [(r, min(D, D_IN - r * D)) for r in range(n_bin)]
    return row_bin, row_sink, row_loss, row_rb, n_rows, bin_parts


def _pair_swap_call(grad_blocks):
    def body(g_in, kept, received, send_sems, recv_sems, local_sems):
        start, _, wait = _pair_swap([g_in], [kept], [received], send_sems, recv_sems, local_sems)
        start()
        wait()

    any_spec = pl.BlockSpec(memory_space=pl.ANY)
    return pl.pallas_call(
        body, name="pair_swap", in_specs=[any_spec], out_specs=[any_spec, any_spec],
        out_shape=_side_out_shapes("pair_swap", [grad_blocks]),
        scratch_shapes=[pltpu.SemaphoreType.DMA((1, N_CHIPS)), pltpu.SemaphoreType.DMA((1, N_CHIPS)),
                        pltpu.SemaphoreType.DMA((1, N_CHIPS))],
    )(grad_blocks)


def _pair_add(kept, received, name):
    n, R, C = kept.shape
    rows = _adamw_rows(R)

    def body(a_ref, b_ref, o_ref):
        o_ref[...] = (a_ref[...].astype(F32) + b_ref[...].astype(F32)).astype(o_ref.dtype)

    tile = pl.BlockSpec((None, rows, C), lambda q, r: (q, r, 0))
    return pl.pallas_call(
        body, name=name, grid=(n, R // rows), in_specs=[tile, tile], out_specs=tile,
        out_shape=jax.ShapeDtypeStruct(kept.shape, kept.dtype), compiler_params=_params(2),
    )(kept, received)


def _final_exchange(grad_blocks, partials, loss):
    D = partials["ffn1_pre_g"].shape[1]
    rb_shape = partials["rel_bias"].shape
    row_bin, row_sink, row_loss, row_rb, n_rows, bin_parts = _pack_layout(D, rb_shape[0])
    n_small = len(SMALL)

    def body(*refs):
        g_in = refs[0]
        part = dict(zip(SMALL, refs[1:1 + n_small]))
        loss_ref = refs[1 + n_small]
        landed, gath, pack, send_sems, recv_sems, local_sems = refs[2 + n_small:]

        pack[...] = jnp.zeros_like(pack)
        for i, name in enumerate(GAINS):
            pack[i:i + 1, :] = part[name][...]
        for r, width in bin_parts:
            pack[row_bin + r:row_bin + r + 1, 0:width] = part["b_in"][:, r * D:r * D + width]
        pack[row_sink:row_sink + 1, 0:LANES] = part["sinks"][...]
        pack[row_loss:row_loss + 1, 0:LANES] = loss_ref[...]
        pack[row_rb:row_rb + rb_shape[0], 0:rb_shape[1]] = part["rel_bias"][...]

        small_start, _, small_wait = _side_copies("gather", [pack], [gath], send_sems, recv_sems, local_sems, sem_row=0)
        big_start, _, big_wait = _quad_exchange([g_in], [landed], send_sems, recv_sems, local_sems, sem_row=1)
        small_start()
        big_start()
        small_wait()
        big_wait()

    args = [grad_blocks] + [partials[k] for k in SMALL] + [loss]
    vmem = pl.BlockSpec(memory_space=pltpu.VMEM)
    any_spec = pl.BlockSpec(memory_space=pl.ANY)
    return pl.pallas_call(
        body, name="final_exchange",
        in_specs=[any_spec] + [vmem] * (n_small + 1),
        out_specs=[any_spec, any_spec],
        out_shape=[jax.ShapeDtypeStruct(grad_blocks.shape, grad_blocks.dtype),
                   jax.ShapeDtypeStruct((N_DEV, n_rows, D), F32)],
        scratch_shapes=[pltpu.VMEM((n_rows, D), F32), pltpu.SemaphoreType.DMA((2, 7)),
                        pltpu.SemaphoreType.DMA((2, 7)), pltpu.SemaphoreType.DMA((2, N_CHIPS))],
    )(*args)


def _adamw(w, g, m, v):
    m = ADAM_B1 * m + (1.0 - ADAM_B1) * g
    v = ADAM_B2 * v + (1.0 - ADAM_B2) * (g * g)
    m_hat = m / (1.0 - ADAM_B1 ** ADAM_STEP)
    v_hat = v / (1.0 - ADAM_B2 ** ADAM_STEP)
    return -ADAM_LR * (m_hat / (jnp.sqrt(v_hat) + ADAM_EPS) + ADAM_WD * w), m, v


def _sum_adamw(partials, w, m, v, rows, name):
    R, C = w.shape
    n = partials.shape[0]

    def body(p_ref, w_ref, m_ref, v_ref, g_ref, d_ref, nm_ref, nv_ref):
        g = p_ref[0].astype(F32)
        for k in range(1, n):
            g = g + p_ref[k].astype(F32)
        g_ref[...] = g
        d_ref[...], nm_ref[...], nv_ref[...] = _adamw(w_ref[...], g, m_ref[...], v_ref[...])

    tile = pl.BlockSpec((rows, C), lambda i: (i, 0))
    return pl.pallas_call(
        body, name=name, grid=(R // rows,),
        in_specs=[pl.BlockSpec((n, rows, C), lambda i: (0, i, 0)), tile, tile, tile],
        out_specs=[tile] * 4, out_shape=[jax.ShapeDtypeStruct((R, C), F32)] * 4,
        compiler_params=_params(1),
    )(partials, w, m, v)


def _small_adamw(gathered, ws, ms, vs):
    D = ws["ffn1_pre_g"].shape[1]
    n_sink = ws["sinks"].shape[1]
    rb_shape = ws["rel_bias"].shape
    row_bin, row_sink, row_loss, row_rb, n_rows, bin_parts = _pack_layout(D, rb_shape[0])
    n_small = len(SMALL)

    def body(*refs):
        gath = refs[0]
        pos = 1
        w_ref = dict(zip(SMALL, refs[pos:pos + n_small]))
        m_ref = dict(zip(SMALL, refs[pos + n_small:pos + 2 * n_small]))
        v_ref = dict(zip(SMALL, refs[pos + 2 * n_small:pos + 3 * n_small]))
        pos += 3 * n_small
        outs = {name: refs[pos + 4 * i:pos + 4 * i + 4] for i, name in enumerate(SMALL)}
        loss_out = refs[pos + 4 * n_small]
        pack = refs[pos + 4 * n_small + 1]

        total = gath[0]
        for k in range(1, N_DEV):
            total = total + gath[k]
        pack[...] = total

        def update(name, g):
            g_out, d_out, m_out, v_out = outs[name]
            g_out[...] = g
            d_out[...], m_out[...], v_out[...] = _adamw(w_ref[name][...], g, m_ref[name][...], v_ref[name][...])

        for i, name in enumerate(GAINS):
            update(name, pack[i:i + 1, :])
        update("b_in", jnp.concatenate([pack[row_bin + r:row_bin + r + 1, 0:width] for r, width in bin_parts], axis=1))
        update("sinks", pack[row_sink:row_sink + 1, 0:n_sink])
        update("rel_bias", pack[row_rb:row_rb + rb_shape[0], 0:rb_shape[1]])
        loss_out[...] = pack[row_loss:row_loss + 1, 0:LANES]

    args = [gathered]
    for group in (ws, ms, vs):
        args += [group[k] for k in SMALL]
    out_shape = []
    for name in SMALL:
        out_shape += [jax.ShapeDtypeStruct(ws[name].shape, F32)] * 4
    out_shape.append(jax.ShapeDtypeStruct((1, LANES), F32))
    res = pl.pallas_call(
        body, name="small_adamw",
        in_specs=[pl.BlockSpec(memory_space=pltpu.VMEM)] * len(args),
        out_specs=[pl.BlockSpec(memory_space=pltpu.VMEM)] * len(out_shape),
        out_shape=out_shape,
        scratch_shapes=[pltpu.VMEM((n_rows, D), F32)],
    )(*args)
    per_name = {name: res[4 * i:4 * i + 4] for i, name in enumerate(SMALL)}
    return per_name, res[-1]


COLUMN_SHARDED = ("ffn1_w_gu", "ffn2_w_gu", "w_in")


def _adamw_rows(rows_total):
    return max(r for r in range(16, min(rows_total, 256) + 1, 16) if rows_total % r == 0)


def kernel(x, p, rel_bias, ffn1_pre_g, ffn1_w_gu, ffn1_w_down, ffn1_post_g, attn_pre_g, w_in, b_in, sinks, w_out, b_out, attn_post_g, ffn2_pre_g, ffn2_w_gu, ffn2_w_down, ffn2_post_g, ple_pre_g, w_ple_gate, w_ple_proj, ple_post_g, loss_target, m_rel_bias, m_ffn1_pre_g, m_ffn1_w_gu, m_ffn1_w_down, m_ffn1_post_g, m_attn_pre_g, m_w_in, m_b_in, m_sinks, m_w_out, m_b_out, m_attn_post_g, m_ffn2_pre_g, m_ffn2_w_gu, m_ffn2_w_down, m_ffn2_post_g, m_ple_pre_g, m_w_ple_gate, m_w_ple_proj, m_ple_post_g, v_rel_bias, v_ffn1_pre_g, v_ffn1_w_gu, v_ffn1_w_down, v_ffn1_post_g, v_attn_pre_g, v_w_in, v_b_in, v_sinks, v_w_out, v_b_out, v_attn_post_g, v_ffn2_pre_g, v_ffn2_w_gu, v_ffn2_w_down, v_ffn2_post_g, v_ple_pre_g, v_w_ple_gate, v_w_ple_proj, v_ple_post_g):
    given = dict(locals())
    ws = {k: given[k] for k in WEIGHTS}
    ms = {k: given["m_" + k] for k in WEIGHTS}
    vs = {k: given["v_" + k] for k in WEIGHTS}

    def shard(t):
        return t.reshape(t.shape[1:])

    xs, ps, target = shard(x), shard(shard(p)), shard(loss_target)
    T, D = xs.shape
    small = {k: ws[k] for k in SMALL}

    def local(group, k):
        t = shard(group[k])
        return jnp.swapaxes(t, 0, 1) if k in COLUMN_SHARDED else t

    shards = {k: local(ws, k) for k in BIG}

    cast = dict(zip(BIG, _cast_bf16([shards[k] for k in BIG])))
    buckets_a = _bucket_tiles(PATTERNS_A)
    buckets_b = _bucket_tiles(PATTERNS_B)
    bias_a, _ = _bias_build(small["rel_bias"], buckets_a, 0, "bias_build_a")
    bias_b, (w_gu1, w_down1) = _bias_build(
        small["rel_bias"], buckets_b, N_HEAD_GROUP, "bias_build_b",
        side=("relay_gather", [cast["ffn1_w_gu"], cast["ffn1_w_down"]]))
    w_down1 = w_down1.reshape(-1, D)
    a_cfg = dict(patterns=PATTERNS_A, qcol=Q_A_COL, kcol=K_A_COL, vcol=V_A_COL, shared_kv=True)
    b_cfg = dict(patterns=PATTERNS_B, qcol=Q_B_COL, kcol=K_B_COL, vcol=V_B_COL, shared_kv=False)

    (h1, f1, a1, gu1), (w_in_g, w_down2) = _ffn_fwd(
        xs, small["ffn1_pre_g"], small["ffn1_post_g"], w_gu1, w_down1, "ffn1_fwd",
        side=("relay_gather", [cast["w_in"], cast["ffn2_w_down"]]))
    w_in_full = w_in_g.reshape(D_IN, D)
    w_down2 = w_down2.reshape(-1, D)
    (z, a2), (w_out_g,) = _inproj_fwd(h1, small["attn_pre_g"], w_in_full, small["b_in"],
                                      side=("relay_gather", [cast["w_out"]]))
    w_out_full = w_out_g.reshape(-1, D)
    (mix_a, lse_a), (w_gate, w_proj) = _attn_fwd(
        z, bias_a, small["sinks"], name="attn_a_fwd", **a_cfg,
        side=("relay_gather", [cast["w_ple_gate"], cast["w_ple_proj"]]))
    w_gate = w_gate.reshape(-1, D)
    (mix_b, lse_b), (w_gu2,) = _attn_fwd(
        z, bias_b, None, name="attn_b_fwd", **b_cfg, side=("relay_gather", [cast["ffn2_w_gu"]]))
    att, h2, mix = _outproj_fwd(mix_a, mix_b, w_out_full, small["b_out"], small["attn_post_g"], h1)
    (h3, f2, a3, gu2), _ = _ffn_fwd(h2, small["ffn2_pre_g"], small["ffn2_post_g"], w_gu2, w_down2, "ffn2_fwd")
    a4, dpre, de, dh4, loss, dg_ple_post = _ple_fwd_loss(
        h3, small["ple_pre_g"], w_gate, ps, w_proj, small["ple_post_g"], target)

    dh3, dg_ple_pre = _ple_bwd(dpre, w_gate, h3, small["ple_pre_g"], dh4)
    d_gate = _dw_rows(a4, dpre, "ple_dw_gate", min(256, D))
    d_proj = _ple_dw_proj(ps, de, N_DEV)
    landed = {}
    (dh2, df2, hh2, dgu2, dg_f2_post, dg_f2_pre), (landed["w_ple_gate"], landed["w_ple_proj"]) = _ffn_bwd(
        dh3, f2, small["ffn2_post_g"], h2, small["ffn2_pre_g"], gu2, w_gu2, w_down2, "ffn2_bwd",
        side=("exchange", [d_gate, d_proj]))
    d_gu2 = _dw_gu(a3, dgu2, "ffn2_dw_gu")
    d_down2, (kept, received) = _dw_down(hh2, df2, "ffn2_dw_down", side=("pair_swap", [d_gu2]))
    d_down2 = d_down2.reshape(N_DEV, -1, D)
    d_gu2_pairs = _pair_add(kept, received, "ffn2_dw_gu_pair_add")
    dmix_a, dmix_b, datt, dg_attn_post, db_out = _outproj_bwd(dh2, att, small["attn_post_g"], w_out_full)
    d_out = _dw_rows(mix, datt, "attn_dw_out", 256)
    (dqa, dka, dva, ds_a, dsinks), (landed["ffn2_w_down"],) = _attn_bwd(
        z, bias_a, small["sinks"], dmix_a, mix_a, lse_a, name="attn_a_bwd", **a_cfg,
        side=("exchange", [d_down2]))
    (dqb, dkb, dvb, ds_b), (landed["ffn2_w_gu"],) = _attn_bwd(
        z, bias_b, None, dmix_b, mix_b, lse_b, name="attn_b_bwd", **b_cfg, side=("quad_exchange", [d_gu2_pairs]))
    (dh1, dz, db_in, dg_attn_pre), (landed["w_out"],) = _inproj_bwd(
        dqa, dka, dva, dqb, dkb, dvb, w_in_full, h1, small["attn_pre_g"], dh2, side=("exchange", [d_out]))
    cols = D_IN // 3
    d_in = _tn_matmul(
        dz, a2, pl.BlockSpec((DW_TILE, cols), lambda b, t: (t, b)), _tok(D),
        jax.ShapeDtypeStruct((D_IN, D), BF16), pl.BlockSpec((cols, D), lambda b, t: (b, 0)),
        3, T // DW_TILE, (cols, D), "attn_dw_in").reshape(N_DEV, D_IN // N_DEV, D)
    (grad_x, df1, hh1, dgu1, dg_f1_post, dg_f1_pre), (landed["w_in"],) = _ffn_bwd(
        dh1, f1, small["ffn1_post_g"], xs, small["ffn1_pre_g"], gu1, w_gu1, w_down1, "ffn1_bwd",
        side=("exchange", [d_in]))
    d_down1 = _dw_down(hh1, df1, "ffn1_dw_down").reshape(N_DEV, -1, D)
    d_gu1, (landed["ffn1_w_down"],) = _dw_gu(a1, dgu1, "ffn1_dw_gu", side=("exchange", [d_down1]))

    rb_a = _bias_grad(ds_a, buckets_a, "bias_grad_a")
    rb_b = _bias_grad(ds_b, buckets_b, "bias_grad_b").reshape(len(PATTERNS_B), N_HEAD_GROUP, NUM_BUCKETS)
    d_rel_bias = jnp.concatenate([rb_a.T, jnp.sum(rb_b, axis=0).T], axis=1)
    small_grads = {"ffn1_pre_g": dg_f1_pre, "ffn1_post_g": dg_f1_post, "attn_pre_g": dg_attn_pre,
                   "attn_post_g": dg_attn_post, "ffn2_pre_g": dg_f2_pre, "ffn2_post_g": dg_f2_post,
                   "ple_pre_g": dg_ple_pre, "ple_post_g": dg_ple_post, "b_out": db_out, "b_in": db_in,
                   "sinks": dsinks, "rel_bias": d_rel_bias}
    d_gu1_pairs = _pair_add(*_pair_swap_call(d_gu1), "ffn1_dw_gu_pair_add")
    landed["ffn1_w_gu"], small_gathered = _final_exchange(d_gu1_pairs, small_grads, loss)

    result = {}
    for k in BIG:
        outs = _sum_adamw(landed[k], shards[k], local(ms, k), local(vs, k), _adamw_rows(shards[k].shape[0]),
                          k + "_adamw")
        if k in COLUMN_SHARDED:
            outs = [jnp.swapaxes(o, 0, 1) for o in outs]
        result[k] = [o.reshape(ws[k].shape) for o in outs]
    small_res, loss_all = _small_adamw(
        small_gathered, small, {k: ms[k] for k in SMALL}, {k: vs[k] for k in SMALL})
    result.update(small_res)

    out = [loss_all[0, 0], grad_x.reshape(x.shape)]
    for i in range(4):
        out += [result[k][i] for k in WEIGHTS]
    return tuple(out)
```

```python
import functools
import math

import numpy as np
import jax
import jax.numpy as jnp
from jax import lax
from jax.experimental import pallas as pl
from jax.experimental.pallas import tpu as pltpu

F32 = jnp.float32
BF16 = jnp.bfloat16
MESH = pl.DeviceIdType.MESH

N_DEV = 8
EPS = 1e-6
NEG_INF = -1e30
HEAD_DIM = 64
LANES = 128
QBLK = 128
D_IN = 2304
A_Q, A_KV, B_W = 512, 128, 512
N_HEAD_GROUP = 8
NUM_BUCKETS = 32
MAX_DISTANCE = 2048
PATTERNS_A = ((1, 127),)
PATTERNS_B = ((1, 128), (4, 128), (16, 128))
Q_A_COL, K_A_COL, V_A_COL = 0, 4, 5
Q_B_COL, K_B_COL, V_B_COL = 6, 10, 14

ADAM_LR, ADAM_B1, ADAM_B2, ADAM_EPS, ADAM_WD, ADAM_STEP = 0.001, 0.9, 0.999, 1e-08, 0.01, 10

TOKEN_TILE = 512
DW_TILE = 1024
FWD_BLOCKS = 4
BWD_BLOCKS = 2
VMEM_LIMIT = 56 * 1024 * 1024
ARB = "arbitrary"

BIG = ("ffn1_w_gu", "ffn1_w_down", "w_in", "w_out", "ffn2_w_gu", "ffn2_w_down", "w_ple_gate", "w_ple_proj")
GAINS = ("ffn1_pre_g", "ffn1_post_g", "attn_pre_g", "attn_post_g", "ffn2_pre_g", "ffn2_post_g",
         "ple_pre_g", "ple_post_g", "b_out")
SMALL = GAINS + ("b_in", "sinks", "rel_bias")
WEIGHTS = ("rel_bias", "ffn1_pre_g", "ffn1_w_gu", "ffn1_w_down", "ffn1_post_g", "attn_pre_g", "w_in", "b_in",
           "sinks", "w_out", "b_out", "attn_post_g", "ffn2_pre_g", "ffn2_w_gu", "ffn2_w_down", "ffn2_post_g",
           "ple_pre_g", "w_ple_gate", "w_ple_proj", "ple_post_g")


def _params(n_axes):
    return pltpu.CompilerParams(dimension_semantics=(ARB,) * n_axes, vmem_limit_bytes=VMEM_LIMIT)


def _mm(a, b):
    return jnp.dot(a, b, preferred_element_type=F32)


def _mm_nt(a, b):
    return lax.dot_general(a, b, (((1,), (1,)), ((), ())), preferred_element_type=F32)


def _mm_tn(a, b):
    return lax.dot_general(a, b, (((0,), (0,)), ((), ())), preferred_element_type=F32)


def _rstd(x):
    return lax.rsqrt(jnp.mean(x * x, axis=-1, keepdims=True) + EPS)


def _rms_bwd(x, r, gain, dy):
    n = x * r
    gdy = dy * gain
    return r * (gdy - n * jnp.mean(gdy * n, axis=-1, keepdims=True)), dy * n


def _colsum(v):
    return jnp.sum(v, axis=0, keepdims=True)


def _full(shape):
    return pl.BlockSpec(shape, lambda *_: (0,) * len(shape))


def _mesh_place():
    return lax.axis_index("x"), lax.axis_index("y"), lax.axis_index("c")


def _slot(dev):
    return 4 * dev[0] + 2 * dev[1] + dev[2]


def _peers(x, y, c):
    out = []
    for flip in range(1, N_DEV):
        dx, dy, dc = (flip >> 2) & 1, (flip >> 1) & 1, flip & 1
        out.append((1 - x if dx else x, 1 - y if dy else y, 1 - c if dc else c))
    return out


def _side_copies(kind, ins, outs, send_sems, recv_sems, local_sems, sem_row=0):
    n = len(ins)
    x, y, c = _mesh_place()
    me = _slot((x, y, c))
    peers = _peers(x, y, c)

    def src(a, block):
        return ins[a] if kind == "gather" else ins[a].at[block]

    def send(a, k, peer):
        return pltpu.make_async_remote_copy(
            src_ref=src(a, _slot(peer)), dst_ref=outs[a].at[me],
            send_sem=send_sems.at[sem_row + a, k], recv_sem=recv_sems.at[sem_row + a, k],
            device_id=peer, device_id_type=MESH)

    def arrival(a, k, peer):
        return pltpu.make_async_remote_copy(
            src_ref=src(a, _slot(peer)), dst_ref=outs[a].at[_slot(peer)],
            send_sem=send_sems.at[sem_row + a, k], recv_sem=recv_sems.at[sem_row + a, k],
            device_id=peer, device_id_type=MESH)

    def own(a):
        return pltpu.make_async_copy(src(a, me), outs[a].at[me], local_sems.at[sem_row + a, 0])

    def start():
        for k, peer in enumerate(peers):
            for a in range(n):
                send(a, k, peer).start()
        for a in range(n):
            own(a).start()

    def wait():
        for k, peer in enumerate(peers):
            for a in range(n):
                arrival(a, k, peer).wait_recv()
        for k, peer in enumerate(peers):
            for a in range(n):
                send(a, k, peer).wait_send()
        for a in range(n):
            own(a).wait()

    return start, None, wait


N_CHIPS = N_DEV // 2


def _pair_swap(ins, received, send_sems, recv_sems):
    n = len(ins)
    x, y, c = _mesh_place()
    sibling = (x, y, 1 - c)

    def send(a, q):
        return pltpu.make_async_remote_copy(
            src_ref=ins[a].at[2 * q + (1 - c)], dst_ref=received[a].at[q],
            send_sem=send_sems.at[a, q], recv_sem=recv_sems.at[a, q], device_id=sibling, device_id_type=MESH)

    def start():
        for a in range(n):
            for q in range(N_CHIPS):
                send(a, q).start()

    def wait():
        for a in range(n):
            for q in range(N_CHIPS):
                send(a, q).wait_recv()
        for a in range(n):
            for q in range(N_CHIPS):
                send(a, q).wait_send()

    return start, None, wait


def _quad_exchange(ins, outs, send_sems, recv_sems, local_sems, sem_row=0):
    n = len(ins)
    x, y, c = _mesh_place()
    mine = 2 * x + y
    chips = [(1 - x, y), (x, 1 - y), (1 - x, 1 - y)]

    def send(a, k, chip):
        return pltpu.make_async_remote_copy(
            src_ref=ins[a].at[2 * chip[0] + chip[1]], dst_ref=outs[a].at[mine],
            send_sem=send_sems.at[sem_row + a, k], recv_sem=recv_sems.at[sem_row + a, k],
            device_id=(chip[0], chip[1], c), device_id_type=MESH)

    def arrival(a, k, chip):
        return pltpu.make_async_remote_copy(
            src_ref=ins[a].at[2 * chip[0] + chip[1]], dst_ref=outs[a].at[2 * chip[0] + chip[1]],
            send_sem=send_sems.at[sem_row + a, k], recv_sem=recv_sems.at[sem_row + a, k],
            device_id=(chip[0], chip[1], c), device_id_type=MESH)

    def own(a):
        return pltpu.make_async_copy(ins[a].at[mine], outs[a].at[mine], local_sems.at[sem_row + a, 0])

    def start():
        for k, chip in enumerate(chips):
            for a in range(n):
                send(a, k, chip).start()
        for a in range(n):
            own(a).start()

    def wait():
        for k, chip in enumerate(chips):
            for a in range(n):
                arrival(a, k, chip).wait_recv()
        for k, chip in enumerate(chips):
            for a in range(n):
                send(a, k, chip).wait_send()
        for a in range(n):
            own(a).wait()

    return start, None, wait


def _relay_gather(ins, outs, send_sems, recv_sems, local_sems):
    n = len(ins)
    x, y, c = _mesh_place()
    me, sibling = (x, y, c), (x, y, 1 - c)
    chips = [(1 - x, y), (x, 1 - y), (1 - x, 1 - y)]

    def copy(a, k, block, to, src=None):
        dst = outs[a].at[_slot(block)]
        return pltpu.make_async_remote_copy(
            src_ref=dst if src is None else src, dst_ref=dst,
            send_sem=send_sems.at[a, k], recv_sem=recv_sems.at[a, k], device_id=to, device_id_type=MESH)

    def own(a):
        return pltpu.make_async_copy(ins[a], outs[a].at[_slot(me)], local_sems.at[a, 0])

    def start():
        for j, chip in enumerate(chips):
            for a in range(n):
                copy(a, 1 + j, me, (*chip, c), src=ins[a]).start()
        for a in range(n):
            copy(a, 0, me, sibling, src=ins[a]).start()
            own(a).start()

    def relay():
        for j, chip in enumerate(chips):
            for a in range(n):
                copy(a, 1 + j, (*chip, c), me).wait_recv()
                copy(a, 4 + j, (*chip, c), sibling).start()

    def wait():
        for a in range(n):
            copy(a, 0, sibling, me).wait_recv()
        for j, chip in enumerate(chips):
            for a in range(n):
                copy(a, 4 + j, (*chip, 1 - c), me).wait_recv()
        for j, chip in enumerate(chips):
            for a in range(n):
                copy(a, 1 + j, me, (*chip, c), src=ins[a]).wait_send()
                copy(a, 4 + j, (*chip, c), sibling).wait_send()
        for a in range(n):
            copy(a, 0, me, sibling, src=ins[a]).wait_send()
            own(a).wait()

    return start, relay, wait


def _side_out_shapes(kind, arrays):
    if kind in ("gather", "relay_gather"):
        return [jax.ShapeDtypeStruct((N_DEV,) + a.shape, a.dtype) for a in arrays]
    if kind == "pair_swap":
        return [jax.ShapeDtypeStruct((N_CHIPS,) + a.shape[1:], a.dtype) for a in arrays]
    return [jax.ShapeDtypeStruct(a.shape, a.dtype) for a in arrays]


def _hosted_call(body, name, grid, in_specs, out_specs, out_shape, scratch_shapes, args, side=None):
    if side is None:
        outs = pl.pallas_call(
            body, name=name, grid=grid, in_specs=in_specs, out_specs=out_specs, out_shape=out_shape,
            scratch_shapes=scratch_shapes, compiler_params=_params(len(grid)))(*args)
        return outs, []
    kind, arrays = side
    side_shapes = _side_out_shapes(kind, arrays)
    n_in, n_out, n_scr, n_side = len(in_specs), len(out_specs), len(scratch_shapes), len(arrays)

    def hosted(*refs):
        pos = 0
        groups = []
        for size in (n_in, n_side, n_out, len(side_shapes), n_scr):
            groups.append(refs[pos:pos + size])
            pos += size
        ins, side_in, outs, side_out, scr = groups
        send_sems, recv_sems, local_sems = refs[pos:]
        ids = [pl.program_id(d) for d in range(len(grid))]
        is_first = functools.reduce(jnp.logical_and, [i == 0 for i in ids])
        is_last = functools.reduce(jnp.logical_and, [i == g - 1 for i, g in zip(ids, grid)])
        if kind == "relay_gather":
            start, relay, wait = _relay_gather(side_in, side_out, send_sems, recv_sems, local_sems)
        elif kind == "pair_swap":
            start, relay, wait = _pair_swap(side_in, side_out, send_sems, recv_sems)
        elif kind == "quad_exchange":
            start, relay, wait = _quad_exchange(side_in, side_out, send_sems, recv_sems, local_sems)
        else:
            start, relay, wait = _side_copies(kind, side_in, side_out, send_sems, recv_sems, local_sems)
        pl.when(is_first)(start)
        if relay is not None:
            pl.when(is_last)(relay)
        body(*ins, *outs, *scr)
        pl.when(is_last)(wait)

    any_spec = pl.BlockSpec(memory_space=pl.ANY)
    outs = pl.pallas_call(
        hosted, name=name, grid=grid,
        in_specs=list(in_specs) + [any_spec] * n_side,
        out_specs=list(out_specs) + [any_spec] * len(side_shapes),
        out_shape=list(out_shape) + side_shapes,
        scratch_shapes=list(scratch_shapes) + [pltpu.SemaphoreType.DMA((n_side, 7)), pltpu.SemaphoreType.DMA((n_side, 7)),
                                               pltpu.SemaphoreType.DMA((n_side, N_CHIPS))],
        compiler_params=_params(len(grid)))(*args, *arrays)
    return outs[:n_out], outs[n_out:]


def _lane_chunks(width, chunk=2 * LANES):
    return [slice(n0, min(n0 + chunk, width)) for n0 in range(0, width, chunk)]


def _pipelined(chunks, first, middle, last):
    n = len(chunks)
    a, b, total = {}, {}, None
    for step in range(n + 2):
        if step < n:
            a[step] = first(chunks[step])
        if 0 <= step - 1 < n:
            b[step - 1] = middle(chunks[step - 1], a.pop(step - 1))
        if 0 <= step - 2 < n:
            part = last(chunks[step - 2], b.pop(step - 2))
            total = part if total is None else total + part
    return total


def _ffn_fwd(h, g_pre, g_post, w_gu, w_down, name, side=None):
    T, D = h.shape
    nj = w_gu.shape[0] // 2
    FB = w_gu.shape[1]
    tm = TOKEN_TILE

    def body(h_ref, gpre_ref, gpost_ref, wg_ref, wu_ref, wd_ref, hout_ref, f_ref, a_ref, gu_ref, a_scr, acc):
        j = pl.program_id(1)

        @pl.when(j == 0)
        def _():
            x = h_ref[...]
            a = (x * _rstd(x) * gpre_ref[...]).astype(BF16)
            a_scr[...] = a
            a_ref[...] = a
            acc[...] = jnp.zeros_like(acc)

        a = a_scr[...]
        g = _mm_nt(a, wg_ref[...])
        u = _mm_nt(a, wu_ref[...])
        gu_ref[0] = g.astype(BF16)
        gu_ref[1] = u.astype(BF16)
        hh = (g * jax.nn.sigmoid(g) * u).astype(BF16)
        acc[...] += _mm(hh, wd_ref[...])

        @pl.when(j == nj - 1)
        def _():
            f = acc[...]
            f_ref[...] = f
            hout_ref[...] = h_ref[...] + 0.5 * (f * _rstd(f) * gpost_ref[...])

    return _hosted_call(
        body, name, (T // tm, nj),
        in_specs=[
            pl.BlockSpec((tm, D), lambda i, j: (i, 0)),
            _full((1, D)), _full((1, D)),
            pl.BlockSpec((None, FB, D), lambda i, j: (j, 0, 0)),
            pl.BlockSpec((None, FB, D), lambda i, j: (j + nj, 0, 0)),
            pl.BlockSpec((FB, D), lambda i, j: (j, 0)),
        ],
        out_specs=[
            pl.BlockSpec((tm, D), lambda i, j: (i, 0)),
            pl.BlockSpec((tm, D), lambda i, j: (i, 0)),
            pl.BlockSpec((tm, D), lambda i, j: (i, 0)),
            pl.BlockSpec((None, 2, tm, FB), lambda i, j: (j, 0, i, 0)),
        ],
        out_shape=[
            jax.ShapeDtypeStruct((T, D), F32),
            jax.ShapeDtypeStruct((T, D), F32),
            jax.ShapeDtypeStruct((T, D), BF16),
            jax.ShapeDtypeStruct((nj, 2, T, FB), BF16),
        ],
        scratch_shapes=[pltpu.VMEM((tm, D), BF16), pltpu.VMEM((tm, D), F32)],
        args=(h, g_pre, g_post, w_gu, w_gu, w_down), side=side)


def _ffn_bwd(dh_out, f, g_post, h, g_pre, gu, w_gu, w_down, name, side=None):
    T, D = h.shape
    nj = w_gu.shape[0] // 2
    FB = w_gu.shape[1]
    tm = TOKEN_TILE

    def body(dho_ref, f_ref, gpost_ref, h_ref, gpre_ref, gu_ref, wg_ref, wu_ref, wd_ref,
             dhin_ref, df_ref, hh_ref, dgu_ref, dgpost_ref, dgpre_ref, df_scr, da):
        i, j = pl.program_id(0), pl.program_id(1)

        @pl.when(jnp.logical_and(i == 0, j == 0))
        def _():
            dgpost_ref[...] = jnp.zeros_like(dgpost_ref)
            dgpre_ref[...] = jnp.zeros_like(dgpre_ref)

        @pl.when(j == 0)
        def _():
            fv = f_ref[...]
            df, dgain = _rms_bwd(fv, _rstd(fv), gpost_ref[...], 0.5 * dho_ref[...])
            dgpost_ref[...] += _colsum(dgain)
            dfb = df.astype(BF16)
            df_scr[...] = dfb
            df_ref[...] = dfb
            da[...] = jnp.zeros_like(da)

        dfb = df_scr[...]

        halves = (slice(0, tm // 2), slice(tm // 2, tm))

        def hidden_grad(c):
            return [_mm_nt(dfb[rows], wd_ref[c, :]) for rows in halves]

        def through_swiglu(c, dhh):
            dhh = jnp.concatenate(dhh, axis=0)
            g = gu_ref[0, :, c].astype(F32)
            u = gu_ref[1, :, c].astype(F32)
            sg = jax.nn.sigmoid(g)
            silu = g * sg
            hh_ref[:, c] = (silu * u).astype(BF16)
            dg = (dhh * u * (sg * (1.0 + (g - silu)))).astype(BF16)
            du = (dhh * silu).astype(BF16)
            dgu_ref[0, :, c] = dg
            dgu_ref[1, :, c] = du
            return dg, du

        def input_grad(c, dgu):
            return jnp.concatenate(
                [_mm(dgu[0][rows], wg_ref[c, :]) + _mm(dgu[1][rows], wu_ref[c, :]) for rows in halves], axis=0)

        da[...] += _pipelined(_lane_chunks(FB), hidden_grad, through_swiglu, input_grad)

        @pl.when(j == nj - 1)
        def _():
            x = h_ref[...]
            dx, dgain = _rms_bwd(x, _rstd(x), gpre_ref[...], da[...])
            dgpre_ref[...] += _colsum(dgain)
            dhin_ref[...] = dho_ref[...] + dx

    tile = pl.BlockSpec((tm, D), lambda i, j: (i, 0))
    return _hosted_call(
        body, name, (T // tm, nj),
        in_specs=[
            tile, tile, _full((1, D)), tile, _full((1, D)),
            pl.BlockSpec((None, 2, tm, FB), lambda i, j: (j, 0, i, 0)),
            pl.BlockSpec((None, FB, D), lambda i, j: (j, 0, 0)),
            pl.BlockSpec((None, FB, D), lambda i, j: (j + nj, 0, 0)),
            pl.BlockSpec((FB, D), lambda i, j: (j, 0)),
        ],
        out_specs=[
            tile, tile,
            pl.BlockSpec((None, tm, FB), lambda i, j: (j, i, 0)),
            pl.BlockSpec((None, 2, tm, FB), lambda i, j: (j, 0, i, 0)),
            _full((1, D)), _full((1, D)),
        ],
        out_shape=[
            jax.ShapeDtypeStruct((T, D), F32),
            jax.ShapeDtypeStruct((T, D), BF16),
            jax.ShapeDtypeStruct((nj, T, FB), BF16),
            jax.ShapeDtypeStruct((nj, 2, T, FB), BF16),
            jax.ShapeDtypeStruct((1, D), F32),
            jax.ShapeDtypeStruct((1, D), F32),
        ],
        scratch_shapes=[pltpu.VMEM((tm, D), BF16), pltpu.VMEM((tm, D), F32)],
        args=(dh_out, f, g_post, h, g_pre, gu, w_gu, w_gu, w_down), side=side)


def _tn_matmul(x, y, x_spec, y_spec, out_shape, out_spec, n_blocks, n_steps, acc_shape, name, side=None):
    def body(x_ref, y_ref, o_ref, acc):
        t = pl.program_id(1)

        @pl.when(t == 0)
        def _():
            acc[...] = jnp.zeros_like(acc)

        acc[...] += _mm_tn(x_ref[...].astype(BF16), y_ref[...].astype(BF16))

        @pl.when(t == n_steps - 1)
        def _():
            o_ref[...] = acc[...].astype(o_ref.dtype)

    outs, side_outs = _hosted_call(
        body, name, (n_blocks, n_steps), in_specs=[x_spec, y_spec], out_specs=[out_spec], out_shape=[out_shape],
        scratch_shapes=[pltpu.VMEM(acc_shape, F32)], args=(x, y), side=side)
    return (outs[0], side_outs) if side is not None else outs[0]


def _inproj_fwd(h, g_pre, w_in, b_in, side=None):
    T, D = h.shape
    tm = TOKEN_TILE

    def body(h_ref, g_ref, w_ref, b_ref, z_ref, a_ref):
        x = h_ref[...]
        a = (x * _rstd(x) * g_ref[...]).astype(BF16)
        a_ref[...] = a
        z_ref[...] = _mm_nt(a, w_ref[...]) + b_ref[...]

    return _hosted_call(
        body, "inproj_fwd", (T // tm,),
        in_specs=[pl.BlockSpec((tm, D), lambda i: (i, 0)), _full((1, D)), _full((D_IN, D)), _full((1, D_IN))],
        out_specs=[pl.BlockSpec((tm, D_IN), lambda i: (i, 0)), pl.BlockSpec((tm, D), lambda i: (i, 0))],
        out_shape=[jax.ShapeDtypeStruct((T, D_IN), F32), jax.ShapeDtypeStruct((T, D), BF16)],
        scratch_shapes=[], args=(h, g_pre, w_in, b_in), side=side)


def _inproj_bwd(dqa, dka, dva, dqb, dkb, dvb, w_in, h, g_pre, dres, side=None):
    T, D = h.shape
    tm = TOKEN_TILE

    def body(dqa_ref, dka_ref, dva_ref, dqb_ref, dkb_ref, dvb_ref, w_ref, h_ref, g_ref, dres_ref,
             dh_ref, dz_ref, dbin_ref, dg_ref):
        i = pl.program_id(0)

        @pl.when(i == 0)
        def _():
            dbin_ref[...] = jnp.zeros_like(dbin_ref)
            dg_ref[...] = jnp.zeros_like(dg_ref)

        dz = jnp.concatenate([dqa_ref[...], dka_ref[...], dva_ref[...], dqb_ref[...], dkb_ref[...], dvb_ref[...]],
                             axis=1)
        dbin_ref[...] += _colsum(dz)
        dzb = dz.astype(BF16)
        dz_ref[...] = dzb
        da = _mm(dzb, w_ref[...])
        x = h_ref[...]
        dx, dgain = _rms_bwd(x, _rstd(x), g_ref[...], da)
        dg_ref[...] += _colsum(dgain)
        dh_ref[...] = dres_ref[...] + dx

    def tile(w):
        return pl.BlockSpec((tm, w), lambda i: (i, 0))

    return _hosted_call(
        body, "inproj_bwd", (T // tm,),
        in_specs=[tile(A_Q), tile(A_KV), tile(A_KV), tile(B_W), tile(B_W), tile(B_W),
                  _full((D_IN, D)), tile(D), _full((1, D)), tile(D)],
        out_specs=[tile(D), tile(D_IN), _full((1, D_IN)), _full((1, D))],
        out_shape=[jax.ShapeDtypeStruct((T, D), F32), jax.ShapeDtypeStruct((T, D_IN), BF16),
                   jax.ShapeDtypeStruct((1, D_IN), F32), jax.ShapeDtypeStruct((1, D), F32)],
        scratch_shapes=[], args=(dqa, dka, dva, dqb, dkb, dvb, w_in, h, g_pre, dres), side=side)


def _bucket_tiles(patterns):
    i = np.arange(QBLK)[:, None]
    j = np.arange(2 * QBLK)[None, :]
    dist = QBLK + i - j
    max_exact = NUM_BUCKETS // 2
    tiles = []
    for dilation, max_dist in patterns:
        n = np.maximum(dist * dilation, 0)
        nf = np.maximum(n, 1).astype(np.float32)
        large = max_exact + (np.log(nf / np.float32(max_exact)) / np.float32(math.log(MAX_DISTANCE / max_exact))
                             * np.float32(NUM_BUCKETS - max_exact)).astype(np.int32)
        bucket = np.where(n < max_exact, n, np.minimum(large, NUM_BUCKETS - 1))
        tiles.append(np.where((dist >= 0) & (dist <= max_dist), bucket, -1))
    return jnp.asarray(np.stack(tiles).astype(np.int32))


def _bias_build(rel_bias, buckets, head0, name, side=None):
    n = buckets.shape[0]

    def body(bk_ref, rb_ref, o_ref):
        bk = bk_ref[...]
        base = jnp.where(bk < 0, NEG_INF, 0.0).astype(F32)
        for hd in range(N_HEAD_GROUP):
            o_ref[hd] = lax.fori_loop(
                0, NUM_BUCKETS, lambda b, acc, hd=hd: jnp.where(bk == b, rb_ref[b, head0 + hd], acc), base)

    outs, side_outs = _hosted_call(
        body, name, (n,),
        in_specs=[pl.BlockSpec((None, QBLK, 2 * QBLK), lambda p: (p, 0, 0)), pl.BlockSpec(memory_space=pltpu.SMEM)],
        out_specs=[pl.BlockSpec((None, N_HEAD_GROUP, QBLK, 2 * QBLK), lambda p: (p, 0, 0, 0))],
        out_shape=[jax.ShapeDtypeStruct((n, N_HEAD_GROUP, QBLK, 2 * QBLK), F32)],
        scratch_shapes=[], args=(buckets, rel_bias), side=side)
    return outs[0], side_outs


def _bias_grad(ds, buckets, name):
    n = buckets.shape[0]

    def body(ds_ref, bk_ref, o_ref):
        bk = bk_ref[...]
        row = lax.broadcasted_iota(jnp.int32, (NUM_BUCKETS, 2 * QBLK), 0)
        for hd in range(N_HEAD_GROUP):
            d = ds_ref[hd]
            per_key = jnp.zeros((NUM_BUCKETS, 2 * QBLK), F32)
            for b in range(NUM_BUCKETS):
                per_key = jnp.where(row == b, jnp.sum(jnp.where(bk == b, d, 0.0), axis=0, keepdims=True), per_key)
            o_ref[hd] = jnp.broadcast_to(jnp.sum(per_key, axis=1, keepdims=True), (NUM_BUCKETS, LANES))

    out = pl.pallas_call(
        body, name=name, grid=(n,),
        in_specs=[pl.BlockSpec((None, N_HEAD_GROUP, QBLK, 2 * QBLK), lambda p: (p, 0, 0, 0)),
                  pl.BlockSpec((None, QBLK, 2 * QBLK), lambda p: (p, 0, 0))],
        out_specs=pl.BlockSpec((None, N_HEAD_GROUP, NUM_BUCKETS, LANES), lambda p: (p, 0, 0, 0)),
        out_shape=jax.ShapeDtypeStruct((n, N_HEAD_GROUP, NUM_BUCKETS, LANES), F32),
        compiler_params=_params(1),
    )(ds, buckets)
    return out[:, :, :, 0].reshape(n * N_HEAD_GROUP, NUM_BUCKETS)


def _class_rows(start, dilation):
    if dilation == 1:
        return pl.ds(pl.multiple_of(start, QBLK), QBLK)
    return pl.ds(start, QBLK, stride=dilation)


def _block_starts(idx, n_blocks, dilation):
    cls = idx // n_blocks
    n = idx % n_blocks
    cur = cls + dilation * QBLK * n
    prev = cls + dilation * QBLK * jnp.maximum(n - 1, 0)
    return n, cur, prev


class _HeadPair:
    def __init__(self, g, shared_kv):
        self.lane = lax.broadcasted_iota(jnp.int32, (1, LANES), 1)
        self.lower = self.lane < HEAD_DIM
        self.shared_kv = shared_kv
        self.key_lanes = (self.lane >= HEAD_DIM).astype(jnp.int32) == (g // 2)

    def stack(self, t):
        return jnp.concatenate([jnp.where(self.lower, t, 0.0), jnp.where(self.lower, 0.0, t)], axis=0).astype(BF16)

    def unstack(self, t2):
        return jnp.where(self.lower, t2[:QBLK], t2[QBLK:])

    def keys(self, t):
        if self.shared_kv:
            return jnp.where(self.key_lanes, t, pltpu.roll(t, HEAD_DIM, 1))
        return t

    def key_grads(self, t):
        if self.shared_kv:
            return jnp.where(self.key_lanes, t + pltpu.roll(t, HEAD_DIM, 1), 0.0)
        return t


def _attn_specs(T, qcol, kcol, vcol, shared_kv):
    kv = (lambda c: (lambda g: (0, c))) if shared_kv else (lambda c: (lambda g: (0, c + g)))
    return [pl.BlockSpec((T, LANES), lambda g: (0, qcol + g)),
            pl.BlockSpec((T, LANES), kv(kcol)),
            pl.BlockSpec((T, LANES), kv(vcol))]


def _attn_fwd(z, bias, sinks, patterns, qcol, kcol, vcol, shared_kv, name, side=None):
    T = z.shape[0]
    n_pat = len(patterns)
    has_sink = sinks is not None

    def body(*refs):
        if has_sink:
            sink_ref, refs = refs[0], refs[1:]
        q_ref, k_ref, v_ref, b_ref, o_ref, l_ref = refs[:6]
        po_scr = refs[6:6 + n_pat]
        pl_scr = refs[6 + n_pat:]
        g = pl.program_id(0)
        heads = _HeadPair(g, shared_kv)
        in_prev = lax.broadcasted_iota(jnp.int32, (2 * QBLK, 2 * QBLK), 1) < QBLK

        for pi, (dilation, _) in enumerate(patterns):
            n_blocks = T // (QBLK * dilation)

            def step(it, carry, pi=pi, dilation=dilation, n_blocks=n_blocks):
                blocks = []
                for u in range(FWD_BLOCKS):
                    n, cur, prev = _block_starts(it * FWD_BLOCKS + u, n_blocks, dilation)
                    rows_c, rows_p = _class_rows(cur, dilation), _class_rows(prev, dilation)
                    qm = heads.stack(q_ref[rows_c, :])
                    k2 = heads.keys(jnp.concatenate([k_ref[rows_p, :], k_ref[rows_c, :]], axis=0)).astype(BF16)
                    v2 = heads.keys(jnp.concatenate([v_ref[rows_p, :], v_ref[rows_c, :]], axis=0)).astype(BF16)
                    blocks.append(dict(n=n, rows=rows_c, v2=v2, s=_mm_nt(qm, k2)))
                for b in blocks:
                    s = b["s"] * (HEAD_DIM ** -0.5) + b_ref[pi]
                    b["s"] = jnp.where(jnp.logical_and(in_prev, b["n"] == 0), NEG_INF, s)
                    b["m"] = jnp.max(b["s"], axis=1, keepdims=True)
                for b in blocks:
                    b["pr"] = jnp.exp(b["s"] - b["m"])
                    b["den"] = jnp.sum(b["pr"], axis=1, keepdims=True)
                for b in blocks:
                    b["o2"] = _mm(b["pr"].astype(BF16), b["v2"])
                for b in blocks:
                    lse = b["m"] + jnp.log(b["den"])
                    po_scr[pi][b["rows"], :] = heads.unstack(b["o2"] / b["den"])
                    pl_scr[2 * pi][b["rows"], :] = jnp.broadcast_to(lse[:QBLK], (QBLK, LANES))
                    pl_scr[2 * pi + 1][b["rows"], :] = jnp.broadcast_to(lse[QBLK:], (QBLK, LANES))
                return carry

            lax.fori_loop(0, (dilation * n_blocks) // FWD_BLOCKS, step, 0)

        def merge(ci, carry):
            rows = pl.ds(pl.multiple_of(ci * QBLK, QBLK), QBLK)
            weights = []
            for hd in range(2):
                parts = [pl_scr[2 * pi + hd][rows, :] for pi in range(n_pat)]
                m = functools.reduce(jnp.maximum, parts)
                if has_sink:
                    sink = sink_ref[0, 2 * g + hd]
                    m = jnp.maximum(m, sink)
                den = functools.reduce(jnp.add, [jnp.exp(x - m) for x in parts])
                if has_sink:
                    den = den + jnp.exp(sink - m)
                lse = m + jnp.log(den)
                l_ref[hd, rows, :] = lse
                weights.append([jnp.exp(x - lse) for x in parts])
            o_ref[rows, :] = functools.reduce(
                jnp.add, [jnp.where(heads.lower, weights[0][pi], weights[1][pi]) * po_scr[pi][rows, :]
                          for pi in range(n_pat)])
            return carry

        lax.fori_loop(0, T // QBLK, merge, 0)

    in_specs = _attn_specs(T, qcol, kcol, vcol, shared_kv)
    in_specs.append(pl.BlockSpec((n_pat, None, 2 * QBLK, 2 * QBLK), lambda g: (0, g, 0, 0)))
    args = [z, z, z, bias.reshape(n_pat, N_HEAD_GROUP // 2, 2 * QBLK, 2 * QBLK)]
    if has_sink:
        in_specs.insert(0, pl.BlockSpec(memory_space=pltpu.SMEM))
        args.insert(0, sinks)
    return _hosted_call(
        body, name, (N_HEAD_GROUP // 2,),
        in_specs=in_specs,
        out_specs=[pl.BlockSpec((T, LANES), lambda g: (0, g)), pl.BlockSpec((2, T, LANES), lambda g: (g, 0, 0))],
        out_shape=[jax.ShapeDtypeStruct((T, N_HEAD_GROUP * HEAD_DIM), F32),
                   jax.ShapeDtypeStruct((N_HEAD_GROUP, T, LANES), F32)],
        scratch_shapes=[pltpu.VMEM((T, LANES), F32)] * (3 * n_pat), args=args, side=side)


def _attn_bwd(z, bias, sinks, d_out, out, lse, patterns, qcol, kcol, vcol, shared_kv, name, side=None):
    T = z.shape[0]
    n_pat = len(patterns)
    has_sink = sinks is not None
    kv_width = LANES if shared_kv else N_HEAD_GROUP * HEAD_DIM

    def body(*refs):
        if has_sink:
            sink_ref, refs = refs[0], refs[1:]
        q_ref, k_ref, v_ref, b_ref, do_ref, o_ref, l0_ref, l1_ref = refs[:8]
        dq_ref, dk_ref, dv_ref, ds_ref = refs[8:12]
        dsink_ref = refs[12] if has_sink else None
        dk_acc, dv_acc = refs[-2:]
        g = pl.program_id(0)
        heads = _HeadPair(g, shared_kv)
        in_prev = lax.broadcasted_iota(jnp.int32, (2 * QBLK, 2 * QBLK), 1) < QBLK

        dq_ref[...] = jnp.zeros_like(dq_ref)
        ds_ref[...] = jnp.zeros_like(ds_ref)
        dk_acc[...] = jnp.zeros_like(dk_acc)
        dv_acc[...] = jnp.zeros_like(dv_acc)

        dsink = jnp.zeros((1, LANES), F32)
        for pi, (dilation, _) in enumerate(patterns):
            n_blocks = T // (QBLK * dilation)

            def step(idx, dsink, pi=pi, dilation=dilation, n_blocks=n_blocks):
                blocks = []
                for u in range(BWD_BLOCKS):
                    n, cur, prev = _block_starts(idx * BWD_BLOCKS + u, n_blocks, dilation)
                    rows_c, rows_p = _class_rows(cur, dilation), _class_rows(prev, dilation)
                    qm = heads.stack(q_ref[rows_c, :])
                    k2 = heads.keys(jnp.concatenate([k_ref[rows_p, :], k_ref[rows_c, :]], axis=0)).astype(BF16)
                    v2 = heads.keys(jnp.concatenate([v_ref[rows_p, :], v_ref[rows_c, :]], axis=0)).astype(BF16)
                    d_o = do_ref[rows_c, :]
                    dom = heads.stack(d_o)
                    dd = d_o * o_ref[rows_c, :]
                    delta = jnp.concatenate([jnp.sum(jnp.where(heads.lower, dd, 0.0), axis=1, keepdims=True),
                                             jnp.sum(jnp.where(heads.lower, 0.0, dd), axis=1, keepdims=True)], axis=0)
                    lse = jnp.concatenate([l0_ref[rows_c, :], l1_ref[rows_c, :]], axis=0)
                    blocks.append(dict(n=n, rows_c=rows_c, rows_p=rows_p, qm=qm, k2=k2, dom=dom, delta=delta, lse=lse,
                                       s=_mm_nt(qm, k2), dp=_mm_nt(dom, v2)))
                for b in blocks:
                    s = b["s"] * (HEAD_DIM ** -0.5) + b_ref[pi]
                    s = jnp.where(jnp.logical_and(in_prev, b["n"] == 0), NEG_INF, s)
                    b["pr"] = jnp.exp(s - jnp.concatenate([b["lse"], b["lse"]], axis=1))
                    b["ds"] = b["pr"] * (b["dp"] - b["delta"])
                for b in blocks:
                    dsb = b["ds"].astype(BF16)
                    b["dq2"] = _mm(dsb, b["k2"])
                    b["dk2"] = _mm_tn(dsb, b["qm"])
                    b["dv2"] = _mm_tn(b["pr"].astype(BF16), b["dom"])
                for b in blocks:
                    ds_ref[pi] += b["ds"]
                    dq_ref[b["rows_c"], :] += heads.unstack(b["dq2"]) * (HEAD_DIM ** -0.5)
                    dk2 = heads.key_grads(b["dk2"]) * (HEAD_DIM ** -0.5)
                    dv2 = heads.key_grads(b["dv2"])
                    dk_acc[b["rows_p"], :] += dk2[:QBLK]
                    dk_acc[b["rows_c"], :] += dk2[QBLK:]
                    dv_acc[b["rows_p"], :] += dv2[:QBLK]
                    dv_acc[b["rows_c"], :] += dv2[QBLK:]
                    if has_sink:
                        for hd in range(2):
                            rows_h = slice(QBLK * hd, QBLK * (hd + 1))
                            p_sink = jnp.exp(sink_ref[0, 2 * g + hd] - b["lse"][rows_h, 0:1])
                            dsink = dsink - jnp.where(heads.lane == 2 * g + hd,
                                                      jnp.sum(p_sink * b["delta"][rows_h]), 0.0)
                return dsink

            dsink = lax.fori_loop(0, (dilation * n_blocks) // BWD_BLOCKS, step, dsink)

        if shared_kv:
            @pl.when(g == 0)
            def _():
                dk_ref[...] = dk_acc[...]
                dv_ref[...] = dv_acc[...]

            @pl.when(g != 0)
            def _():
                dk_ref[...] += dk_acc[...]
                dv_ref[...] += dv_acc[...]
        else:
            dk_ref[...] = dk_acc[...]
            dv_ref[...] = dv_acc[...]

        if has_sink:
            @pl.when(g == 0)
            def _():
                dsink_ref[...] = dsink

            @pl.when(g != 0)
            def _():
                dsink_ref[...] += dsink

    pair = pl.BlockSpec((T, LANES), lambda g: (0, g))
    stacked = pl.BlockSpec((n_pat, None, 2 * QBLK, 2 * QBLK), lambda g: (0, g, 0, 0))
    stacked_shape = (n_pat, N_HEAD_GROUP // 2, 2 * QBLK, 2 * QBLK)
    in_specs = _attn_specs(T, qcol, kcol, vcol, shared_kv)
    in_specs += [stacked, pair, pair,
                 pl.BlockSpec((None, T, LANES), lambda g: (2 * g, 0, 0)),
                 pl.BlockSpec((None, T, LANES), lambda g: (2 * g + 1, 0, 0))]
    args = [z, z, z, bias.reshape(stacked_shape), d_out, out, lse, lse]
    kv_out = _full((T, LANES)) if shared_kv else pair
    out_specs = [pair, kv_out, kv_out, stacked]
    out_shape = [jax.ShapeDtypeStruct((T, N_HEAD_GROUP * HEAD_DIM), F32),
                 jax.ShapeDtypeStruct((T, kv_width), F32), jax.ShapeDtypeStruct((T, kv_width), F32),
                 jax.ShapeDtypeStruct(stacked_shape, F32)]
    if has_sink:
        in_specs.insert(0, pl.BlockSpec(memory_space=pltpu.SMEM))
        args.insert(0, sinks)
        out_specs.append(_full((1, LANES)))
        out_shape.append(jax.ShapeDtypeStruct((1, LANES), F32))
    outs, side_outs = _hosted_call(
        body, name, (N_HEAD_GROUP // 2,), in_specs=in_specs, out_specs=out_specs, out_shape=out_shape,
        scratch_shapes=[pltpu.VMEM((T, LANES), F32), pltpu.VMEM((T, LANES), F32)], args=args, side=side)
    outs = list(outs)
    outs[3] = outs[3].reshape(n_pat, N_HEAD_GROUP, QBLK, 2 * QBLK)
    return outs, side_outs


def _outproj_fwd(mix_a, mix_b, w_out, b_out, g_post, h):
    T, D = h.shape
    tm = TOKEN_TILE
    d_mix = w_out.shape[0]

    def body(ma_ref, mb_ref, w_ref, b_ref, g_ref, h_ref, att_ref, hout_ref, mix_ref):
        mix = jnp.concatenate([ma_ref[...], mb_ref[...]], axis=1).astype(BF16)
        mix_ref[...] = mix
        att = _mm(mix, w_ref[...]) + b_ref[...]
        att_ref[...] = att
        hout_ref[...] = h_ref[...] + att * _rstd(att) * g_ref[...]

    def tile(w):
        return pl.BlockSpec((tm, w), lambda i: (i, 0))

    return pl.pallas_call(
        body, name="outproj_fwd", grid=(T // tm,),
        in_specs=[tile(A_Q), tile(B_W), _full((d_mix, D)), _full((1, D)), _full((1, D)), tile(D)],
        out_specs=[tile(D), tile(D), tile(d_mix)],
        out_shape=[jax.ShapeDtypeStruct((T, D), F32), jax.ShapeDtypeStruct((T, D), F32),
                   jax.ShapeDtypeStruct((T, d_mix), BF16)],
        compiler_params=_params(1),
    )(mix_a, mix_b, w_out, b_out, g_post, h)


def _outproj_bwd(dh, att, g_post, w_out):
    T, D = dh.shape
    tm = TOKEN_TILE
    d_mix = w_out.shape[0]

    def body(dh_ref, att_ref, g_ref, w_ref, dma_ref, dmb_ref, datt_ref, dg_ref, db_ref):
        i = pl.program_id(0)

        @pl.when(i == 0)
        def _():
            dg_ref[...] = jnp.zeros_like(dg_ref)
            db_ref[...] = jnp.zeros_like(db_ref)

        att = att_ref[...]
        datt, dgain = _rms_bwd(att, _rstd(att), g_ref[...], dh_ref[...])
        dg_ref[...] += _colsum(dgain)
        db_ref[...] += _colsum(datt)
        dattb = datt.astype(BF16)
        datt_ref[...] = dattb
        dmix = _mm_nt(dattb, w_ref[...])
        dma_ref[...] = dmix[:, :A_Q]
        dmb_ref[...] = dmix[:, A_Q:]

    def tile(w):
        return pl.BlockSpec((tm, w), lambda i: (i, 0))

    return pl.pallas_call(
        body, name="outproj_bwd", grid=(T // tm,),
        in_specs=[tile(D), tile(D), _full((1, D)), _full((d_mix, D))],
        out_specs=[tile(A_Q), tile(B_W), tile(D), _full((1, D)), _full((1, D))],
        out_shape=[jax.ShapeDtypeStruct((T, A_Q), F32), jax.ShapeDtypeStruct((T, B_W), F32),
                   jax.ShapeDtypeStruct((T, D), BF16), jax.ShapeDtypeStruct((1, D), F32),
                   jax.ShapeDtypeStruct((1, D), F32)],
        compiler_params=_params(1),
    )(dh, att, g_post, w_out)


def _ple_fwd_loss(h, g_pre, w_gate, p, w_proj, g_post, target):
    T, D = h.shape
    tm = TOKEN_TILE
    n_proj, ple, db = w_proj.shape

    def body(h_ref, gpre_ref, wg_ref, p_ref, wp_ref, gpost_ref, t_ref,
             a_ref, dpre_ref, de_ref, dh_ref, loss_ref, dgpost_ref):
        i = pl.program_id(0)

        @pl.when(i == 0)
        def _():
            loss_ref[...] = jnp.zeros_like(loss_ref)
            dgpost_ref[...] = jnp.zeros_like(dgpost_ref)

        x = h_ref[...]
        a = (x * _rstd(x) * gpre_ref[...]).astype(BF16)
        a_ref[...] = a
        gate = jax.nn.sigmoid(_mm(a, wg_ref[...]))
        pb = p_ref[...].astype(BF16)
        e = jnp.concatenate([_mm(pb, wp_ref[k]) for k in range(n_proj)], axis=1)
        ge = gate * e
        rg = _rstd(ge)
        diff = x + ge * rg * gpost_ref[...] - t_ref[...]
        loss_ref[...] += 0.5 * jnp.sum(jnp.mean(diff * diff, axis=1, keepdims=True))
        dy = diff * (1.0 / D)
        dh_ref[...] = dy
        dge, dgain = _rms_bwd(ge, rg, gpost_ref[...], dy)
        dgpost_ref[...] += _colsum(dgain)
        de_ref[...] = (dge * gate).astype(BF16)
        dpre_ref[...] = (dge * e * gate * (1.0 - gate)).astype(BF16)

    def tile(w):
        return pl.BlockSpec((tm, w), lambda i: (i, 0))

    return pl.pallas_call(
        body, name="ple_fwd_loss", grid=(T // tm,),
        in_specs=[tile(D), _full((1, D)), _full((D, D)), tile(ple), _full((n_proj, ple, db)), _full((1, D)), tile(D)],
        out_specs=[tile(D), tile(D), tile(D), tile(D), _full((1, LANES)), _full((1, D))],
        out_shape=[jax.ShapeDtypeStruct((T, D), BF16),
                   jax.ShapeDtypeStruct((T, D), BF16),
                   jax.ShapeDtypeStruct((T, D), BF16),
                   jax.ShapeDtypeStruct((T, D), F32),
                   jax.ShapeDtypeStruct((1, LANES), F32),
                   jax.ShapeDtypeStruct((1, D), F32)],
        compiler_params=_params(1),
    )(h, g_pre, w_gate, p, w_proj, g_post, target)


def _ple_bwd(dpre, w_gate, h, g_pre, dres):
    T, D = h.shape
    tm = TOKEN_TILE

    def body(dpre_ref, w_ref, h_ref, g_ref, dres_ref, dh_ref, dg_ref):
        i = pl.program_id(0)

        @pl.when(i == 0)
        def _():
            dg_ref[...] = jnp.zeros_like(dg_ref)

        da = _mm_nt(dpre_ref[...], w_ref[...])
        x = h_ref[...]
        dx, dgain = _rms_bwd(x, _rstd(x), g_ref[...], da)
        dg_ref[...] += _colsum(dgain)
        dh_ref[...] = dres_ref[...] + dx

    tile = pl.BlockSpec((tm, D), lambda i: (i, 0))
    return pl.pallas_call(
        body, name="ple_bwd", grid=(T // tm,),
        in_specs=[tile, _full((D, D)), tile, _full((1, D)), tile],
        out_specs=[tile, _full((1, D))],
        out_shape=[jax.ShapeDtypeStruct((T, D), F32), jax.ShapeDtypeStruct((1, D), F32)],
        compiler_params=_params(1),
    )(dpre, w_gate, h, g_pre, dres)


def _ple_dw_proj(p, de, n_proj):
    T, ple = p.shape
    D = de.shape[1]
    db = D // n_proj
    tk = TOKEN_TILE
    nt = T // tk

    def body(p_ref, de_ref, o_ref, acc):
        t = pl.program_id(0)

        @pl.when(t == 0)
        def _():
            acc[...] = jnp.zeros_like(acc)

        acc[...] += _mm_tn(p_ref[...].astype(BF16), de_ref[...])

        @pl.when(t == nt - 1)
        def _():
            for k in range(n_proj):
                o_ref[k] = acc[:, k * db:(k + 1) * db].astype(BF16)

    return pl.pallas_call(
        body, name="ple_dw_proj", grid=(nt,),
        in_specs=[pl.BlockSpec((tk, ple), lambda t: (t, 0)), pl.BlockSpec((tk, D), lambda t: (t, 0))],
        out_specs=_full((n_proj, ple, db)), out_shape=jax.ShapeDtypeStruct((n_proj, ple, db), BF16),
        scratch_shapes=[pltpu.VMEM((ple, D), F32)], compiler_params=_params(1),
    )(p, de)


def _tok(width):
    return pl.BlockSpec((DW_TILE, width), lambda b, t: (t, 0))


def _dw_gu(a, dgu, name, side=None):
    T, D = a.shape
    nj, _, _, FB = dgu.shape
    return _tn_matmul(
        dgu, a, pl.BlockSpec((None, None, DW_TILE, FB), lambda b, t: (b % nj, b // nj, t, 0)), _tok(D),
        jax.ShapeDtypeStruct((2 * nj, FB, D), BF16), pl.BlockSpec((None, FB, D), lambda b, t: (b, 0, 0)),
        2 * nj, T // DW_TILE, (FB, D), name, side=side)


def _dw_down(hh, df, name, side=None):
    nj, T, FB = hh.shape
    D = df.shape[1]
    return _tn_matmul(
        hh, df, pl.BlockSpec((None, DW_TILE, FB), lambda b, t: (b, t, 0)), _tok(D),
        jax.ShapeDtypeStruct((nj, FB, D), BF16), pl.BlockSpec((None, FB, D), lambda b, t: (b, 0, 0)),
        nj, T // DW_TILE, (FB, D), name, side=side)


def _dw_rows(xm, y, name, rows):
    T, k = xm.shape
    D = y.shape[1]
    out = _tn_matmul(
        xm, y, pl.BlockSpec((DW_TILE, rows), lambda b, t: (t, b)), _tok(D),
        jax.ShapeDtypeStruct((k, D), BF16), pl.BlockSpec((rows, D), lambda b, t: (b, 0)),
        k // rows, T // DW_TILE, (rows, D), name)
    return out.reshape(N_DEV, k // N_DEV, D)


def _cast_bf16(arrays):
    n = len(arrays)

    def body(*refs):
        for a in range(n):
            refs[n + a][...] = refs[a][...].astype(BF16)

    return pl.pallas_call(
        body, name="cast_shards",
        in_specs=[pl.BlockSpec(memory_space=pltpu.VMEM)] * n, out_specs=[pl.BlockSpec(memory_space=pltpu.VMEM)] * n,
        out_shape=[jax.ShapeDtypeStruct(a.shape, BF16) for a in arrays],
        compiler_params=pltpu.CompilerParams(vmem_limit_bytes=VMEM_LIMIT),
    )(*arrays)


def _pack_layout(D, n_rel_rows):
    n_bin = -(-D_IN // D)
    row_bin = len(GAINS)
    row_sink = row_bin + n_bin
    row_loss = row_sink + 1
    row_rb = -(-(row_loss + 1) // 8) * 8
    n_rows = row_rb + -(-n_rel_rows // 8) * 8
    bin_parts = [(r, min(D, D_IN - r * D)) for r in range(n_bin)]
    return row_bin, row_sink, row_loss, row_rb, n_rows, bin_parts


def _pair_swap_call(grad_blocks):
    def body(g_in, received, send_sems, recv_sems):
        start, _, wait = _pair_swap([g_in], [received], send_sems, recv_sems)
        start()
        wait()

    any_spec = pl.BlockSpec(memory_space=pl.ANY)
    return pl.pallas_call(
        body, name="pair_swap", in_specs=[any_spec], out_specs=any_spec,
        out_shape=_side_out_shapes("pair_swap", [grad_blocks])[0],
        scratch_shapes=[pltpu.SemaphoreType.DMA((1, N_CHIPS)), pltpu.SemaphoreType.DMA((1, N_CHIPS))],
    )(grad_blocks)


def _pair_add(blocks, received, name):
    n, R, C = received.shape
    rows = _adamw_rows(R)
    core = lax.axis_index("c").astype(jnp.int32).reshape(1)

    def body(core_ref, a_ref, b_ref, o_ref):
        o_ref[...] = (a_ref[...].astype(F32) + b_ref[...].astype(F32)).astype(o_ref.dtype)

    tile = pl.BlockSpec((None, rows, C), lambda q, r, core_ref: (q, r, 0))
    return pl.pallas_call(
        body, name=name,
        grid_spec=pltpu.PrefetchScalarGridSpec(
            num_scalar_prefetch=1, grid=(n, R // rows),
            in_specs=[pl.BlockSpec((None, rows, C), lambda q, r, core_ref: (2 * q + core_ref[0], r, 0)), tile],
            out_specs=tile),
        out_shape=jax.ShapeDtypeStruct(received.shape, received.dtype), compiler_params=_params(2),
    )(core, blocks, received)


def _final_exchange(grad_blocks, partials, loss):
    D = partials["ffn1_pre_g"].shape[1]
    rb_shape = partials["rel_bias"].shape
    row_bin, row_sink, row_loss, row_rb, n_rows, bin_parts = _pack_layout(D, rb_shape[0])
    n_small = len(SMALL)

    def body(*refs):
        g_in = refs[0]
        part = dict(zip(SMALL, refs[1:1 + n_small]))
        loss_ref = refs[1 + n_small]
        landed, gath, pack, send_sems, recv_sems, local_sems = refs[2 + n_small:]

        pack[...] = jnp.zeros_like(pack)
        for i, name in enumerate(GAINS):
            pack[i:i + 1, :] = part[name][...]
        for r, width in bin_parts:
            pack[row_bin + r:row_bin + r + 1, 0:width] = part["b_in"][:, r * D:r * D + width]
        pack[row_sink:row_sink + 1, 0:LANES] = part["sinks"][...]
        pack[row_loss:row_loss + 1, 0:LANES] = loss_ref[...]
        pack[row_rb:row_rb + rb_shape[0], 0:rb_shape[1]] = part["rel_bias"][...]

        small_start, _, small_wait = _side_copies("gather", [pack], [gath], send_sems, recv_sems, local_sems, sem_row=0)
        big_start, _, big_wait = _quad_exchange([g_in], [landed], send_sems, recv_sems, local_sems, sem_row=1)
        small_start()
        big_start()
        small_wait()
        big_wait()

    args = [grad_blocks] + [partials[k] for k in SMALL] + [loss]
    vmem = pl.BlockSpec(memory_space=pltpu.VMEM)
    any_spec = pl.BlockSpec(memory_space=pl.ANY)
    return pl.pallas_call(
        body, name="final_exchange",
        in_specs=[any_spec] + [vmem] * (n_small + 1),
        out_specs=[any_spec, any_spec],
        out_shape=[jax.ShapeDtypeStruct(grad_blocks.shape, grad_blocks.dtype),
                   jax.ShapeDtypeStruct((N_DEV, n_rows, D), F32)],
        scratch_shapes=[pltpu.VMEM((n_rows, D), F32), pltpu.SemaphoreType.DMA((2, 7)),
                        pltpu.SemaphoreType.DMA((2, 7)), pltpu.SemaphoreType.DMA((2, N_CHIPS))],
    )(*args)


def _adamw(w, g, m, v):
    m = ADAM_B1 * m + (1.0 - ADAM_B1) * g
    v = ADAM_B2 * v + (1.0 - ADAM_B2) * (g * g)
    m_hat = m / (1.0 - ADAM_B1 ** ADAM_STEP)
    v_hat = v / (1.0 - ADAM_B2 ** ADAM_STEP)
    return -ADAM_LR * (m_hat / (jnp.sqrt(v_hat) + ADAM_EPS) + ADAM_WD * w), m, v


def _sum_adamw(partials, w, m, v, rows, name):
    R, C = w.shape
    n = partials.shape[0]

    def body(p_ref, w_ref, m_ref, v_ref, g_ref, d_ref, nm_ref, nv_ref):
        g = p_ref[0].astype(F32)
        for k in range(1, n):
            g = g + p_ref[k].astype(F32)
        g_ref[...] = g
        d_ref[...], nm_ref[...], nv_ref[...] = _adamw(w_ref[...], g, m_ref[...], v_ref[...])

    tile = pl.BlockSpec((rows, C), lambda i: (i, 0))
    return pl.pallas_call(
        body, name=name, grid=(R // rows,),
        in_specs=[pl.BlockSpec((n, rows, C), lambda i: (0, i, 0)), tile, tile, tile],
        out_specs=[tile] * 4, out_shape=[jax.ShapeDtypeStruct((R, C), F32)] * 4,
        compiler_params=_params(1),
    )(partials, w, m, v)


def _small_adamw(gathered, ws, ms, vs):
    D = ws["ffn1_pre_g"].shape[1]
    n_sink = ws["sinks"].shape[1]
    rb_shape = ws["rel_bias"].shape
    row_bin, row_sink, row_loss, row_rb, n_rows, bin_parts = _pack_layout(D, rb_shape[0])
    n_small = len(SMALL)

    def body(*refs):
        gath = refs[0]
        pos = 1
        w_ref = dict(zip(SMALL, refs[pos:pos + n_small]))
        m_ref = dict(zip(SMALL, refs[pos + n_small:pos + 2 * n_small]))
        v_ref = dict(zip(SMALL, refs[pos + 2 * n_small:pos + 3 * n_small]))
        pos += 3 * n_small
        outs = {name: refs[pos + 4 * i:pos + 4 * i + 4] for i, name in enumerate(SMALL)}
        loss_out = refs[pos + 4 * n_small]
        pack = refs[pos + 4 * n_small + 1]

        total = gath[0]
        for k in range(1, N_DEV):
            total = total + gath[k]
        pack[...] = total

        def update(name, g):
            g_out, d_out, m_out, v_out = outs[name]
            g_out[...] = g
            d_out[...], m_out[...], v_out[...] = _adamw(w_ref[name][...], g, m_ref[name][...], v_ref[name][...])

        for i, name in enumerate(GAINS):
            update(name, pack[i:i + 1, :])
        update("b_in", jnp.concatenate([pack[row_bin + r:row_bin + r + 1, 0:width] for r, width in bin_parts], axis=1))
        update("sinks", pack[row_sink:row_sink + 1, 0:n_sink])
        update("rel_bias", pack[row_rb:row_rb + rb_shape[0], 0:rb_shape[1]])
        loss_out[...] = pack[row_loss:row_loss + 1, 0:LANES]

    args = [gathered]
    for group in (ws, ms, vs):
        args += [group[k] for k in SMALL]
    out_shape = []
    for name in SMALL:
        out_shape += [jax.ShapeDtypeStruct(ws[name].shape, F32)] * 4
    out_shape.append(jax.ShapeDtypeStruct((1, LANES), F32))
    res = pl.pallas_call(
        body, name="small_adamw",
        in_specs=[pl.BlockSpec(memory_space=pltpu.VMEM)] * len(args),
        out_specs=[pl.BlockSpec(memory_space=pltpu.VMEM)] * len(out_shape),
        out_shape=out_shape,
        scratch_shapes=[pltpu.VMEM((n_rows, D), F32)],
    )(*args)
    per_name = {name: res[4 * i:4 * i + 4] for i, name in enumerate(SMALL)}
    return per_name, res[-1]


COLUMN_SHARDED = ("ffn1_w_gu", "ffn2_w_gu", "w_in")


def _adamw_rows(rows_total):
    return max(r for r in range(16, min(rows_total, 256) + 1, 16) if rows_total % r == 0)


def kernel(x, p, rel_bias, ffn1_pre_g, ffn1_w_gu, ffn1_w_down, ffn1_post_g, attn_pre_g, w_in, b_in, sinks, w_out, b_out, attn_post_g, ffn2_pre_g, ffn2_w_gu, ffn2_w_down, ffn2_post_g, ple_pre_g, w_ple_gate, w_ple_proj, ple_post_g, loss_target, m_rel_bias, m_ffn1_pre_g, m_ffn1_w_gu, m_ffn1_w_down, m_ffn1_post_g, m_attn_pre_g, m_w_in, m_b_in, m_sinks, m_w_out, m_b_out, m_attn_post_g, m_ffn2_pre_g, m_ffn2_w_gu, m_ffn2_w_down, m_ffn2_post_g, m_ple_pre_g, m_w_ple_gate, m_w_ple_proj, m_ple_post_g, v_rel_bias, v_ffn1_pre_g, v_ffn1_w_gu, v_ffn1_w_down, v_ffn1_post_g, v_attn_pre_g, v_w_in, v_b_in, v_sinks, v_w_out, v_b_out, v_attn_post_g, v_ffn2_pre_g, v_ffn2_w_gu, v_ffn2_w_down, v_ffn2_post_g, v_ple_pre_g, v_w_ple_gate, v_w_ple_proj, v_ple_post_g):
    given = dict(locals())
    ws = {k: given[k] for k in WEIGHTS}
    ms = {k: given["m_" + k] for k in WEIGHTS}
    vs = {k: given["v_" + k] for k in WEIGHTS}

    def shard(t):
        return t.reshape(t.shape[1:])

    xs, ps, target = shard(x), shard(shard(p)), shard(loss_target)
    T, D = xs.shape
    small = {k: ws[k] for k in SMALL}

    def local(group, k):
        t = shard(group[k])
        return jnp.swapaxes(t, 0, 1) if k in COLUMN_SHARDED else t

    shards = {k: local(ws, k) for k in BIG}

    cast = dict(zip(BIG, _cast_bf16([shards[k] for k in BIG])))
    buckets_a = _bucket_tiles(PATTERNS_A)
    buckets_b = _bucket_tiles(PATTERNS_B)
    bias_a, _ = _bias_build(small["rel_bias"], buckets_a, 0, "bias_build_a")
    bias_b, (w_gu1, w_down1) = _bias_build(
        small["rel_bias"], buckets_b, N_HEAD_GROUP, "bias_build_b",
        side=("relay_gather", [cast["ffn1_w_gu"], cast["ffn1_w_down"]]))
    w_down1 = w_down1.reshape(-1, D)
    a_cfg = dict(patterns=PATTERNS_A, qcol=Q_A_COL, kcol=K_A_COL, vcol=V_A_COL, shared_kv=True)
    b_cfg = dict(patterns=PATTERNS_B, qcol=Q_B_COL, kcol=K_B_COL, vcol=V_B_COL, shared_kv=False)

    (h1, f1, a1, gu1), (w_in_g, w_down2) = _ffn_fwd(
        xs, small["ffn1_pre_g"], small["ffn1_post_g"], w_gu1, w_down1, "ffn1_fwd",
        side=("relay_gather", [cast["w_in"], cast["ffn2_w_down"]]))
    w_in_full = w_in_g.reshape(D_IN, D)
    w_down2 = w_down2.reshape(-1, D)
    (z, a2), (w_out_g,) = _inproj_fwd(h1, small["attn_pre_g"], w_in_full, small["b_in"],
                                      side=("relay_gather", [cast["w_out"]]))
    w_out_full = w_out_g.reshape(-1, D)
    (mix_a, lse_a), (w_gate, w_proj) = _attn_fwd(
        z, bias_a, small["sinks"], name="attn_a_fwd", **a_cfg,
        side=("relay_gather", [cast["w_ple_gate"], cast["w_ple_proj"]]))
    w_gate = w_gate.reshape(-1, D)
    (mix_b, lse_b), (w_gu2,) = _attn_fwd(
        z, bias_b, None, name="attn_b_fwd", **b_cfg, side=("relay_gather", [cast["ffn2_w_gu"]]))
    att, h2, mix = _outproj_fwd(mix_a, mix_b, w_out_full, small["b_out"], small["attn_post_g"], h1)
    (h3, f2, a3, gu2), _ = _ffn_fwd(h2, small["ffn2_pre_g"], small["ffn2_post_g"], w_gu2, w_down2, "ffn2_fwd")
    a4, dpre, de, dh4, loss, dg_ple_post = _ple_fwd_loss(
        h3, small["ple_pre_g"], w_gate, ps, w_proj, small["ple_post_g"], target)

    dh3, dg_ple_pre = _ple_bwd(dpre, w_gate, h3, small["ple_pre_g"], dh4)
    d_gate = _dw_rows(a4, dpre, "ple_dw_gate", min(256, D))
    d_proj = _ple_dw_proj(ps, de, N_DEV)
    landed = {}
    (dh2, df2, hh2, dgu2, dg_f2_post, dg_f2_pre), (landed["w_ple_gate"], landed["w_ple_proj"]) = _ffn_bwd(
        dh3, f2, small["ffn2_post_g"], h2, small["ffn2_pre_g"], gu2, w_gu2, w_down2, "ffn2_bwd",
        side=("exchange", [d_gate, d_proj]))
    d_gu2 = _dw_gu(a3, dgu2, "ffn2_dw_gu")
    d_down2 = _dw_down(hh2, df2, "ffn2_dw_down").reshape(N_DEV, -1, D)
    dmix_a, dmix_b, datt, dg_attn_post, db_out = _outproj_bwd(dh2, att, small["attn_post_g"], w_out_full)
    d_out = _dw_rows(mix, datt, "attn_dw_out", 256)
    (dqa, dka, dva, ds_a, dsinks), (landed["ffn2_w_down"],) = _attn_bwd(
        z, bias_a, small["sinks"], dmix_a, mix_a, lse_a, name="attn_a_bwd", **a_cfg,
        side=("exchange", [d_down2]))
    (dqb, dkb, dvb, ds_b), (landed["ffn2_w_gu"],) = _attn_bwd(
        z, bias_b, None, dmix_b, mix_b, lse_b, name="attn_b_bwd", **b_cfg, side=("exchange", [d_gu2]))
    (dh1, dz, db_in, dg_attn_pre), (landed["w_out"],) = _inproj_bwd(
        dqa, dka, dva, dqb, dkb, dvb, w_in_full, h1, small["attn_pre_g"], dh2, side=("exchange", [d_out]))
    cols = D_IN // 3
    d_in = _tn_matmul(
        dz, a2, pl.BlockSpec((DW_TILE, cols), lambda b, t: (t, b)), _tok(D),
        jax.ShapeDtypeStruct((D_IN, D), BF16), pl.BlockSpec((cols, D), lambda b, t: (b, 0)),
        3, T // DW_TILE, (cols, D), "attn_dw_in").reshape(N_DEV, D_IN // N_DEV, D)
    (grad_x, df1, hh1, dgu1, dg_f1_post, dg_f1_pre), (landed["w_in"],) = _ffn_bwd(
        dh1, f1, small["ffn1_post_g"], xs, small["ffn1_pre_g"], gu1, w_gu1, w_down1, "ffn1_bwd",
        side=("exchange", [d_in]))
    d_down1 = _dw_down(hh1, df1, "ffn1_dw_down").reshape(N_DEV, -1, D)
    d_gu1, (landed["ffn1_w_down"],) = _dw_gu(a1, dgu1, "ffn1_dw_gu", side=("exchange", [d_down1]))

    rb_a = _bias_grad(ds_a, buckets_a, "bias_grad_a")
    rb_b = _bias_grad(ds_b, buckets_b, "bias_grad_b").reshape(len(PATTERNS_B), N_HEAD_GROUP, NUM_BUCKETS)
    d_rel_bias = jnp.concatenate([rb_a.T, jnp.sum(rb_b, axis=0).T], axis=1)
    small_grads = {"ffn1_pre_g": dg_f1_pre, "ffn1_post_g": dg_f1_post, "attn_pre_g": dg_attn_pre,
                   "attn_post_g": dg_attn_post, "ffn2_pre_g": dg_f2_pre, "ffn2_post_g": dg_f2_post,
                   "ple_pre_g": dg_ple_pre, "ple_post_g": dg_ple_post, "b_out": db_out, "b_in": db_in,
                   "sinks": dsinks, "rel_bias": d_rel_bias}
    d_gu1_pairs = _pair_add(d_gu1, _pair_swap_call(d_gu1), "ffn1_dw_gu_pair_add")
    landed["ffn1_w_gu"], small_gathered = _final_exchange(d_gu1_pairs, small_grads, loss)

    result = {}
    for k in BIG:
        outs = _sum_adamw(landed[k], shards[k], local(ms, k), local(vs, k), _adamw_rows(shards[k].shape[0]),
                          k + "_adamw")
        if k in COLUMN_SHARDED:
            outs = [jnp.swapaxes(o, 0, 1) for o in outs]
        result[k] = [o.reshape(ws[k].shape) for o in outs]
    small_res, loss_all = _small_adamw(
        small_gathered, small, {k: ms[k] for k in SMALL}, {k: vs[k] for k in SMALL})
    result.update(small_res)

    out = [loss_all[0, 0], grad_x.reshape(x.shape)]
    for i in range(4):
        out += [result[k][i] for k in WEIGHTS]
    return tuple(out)
```

```python
import functools
import math

import numpy as np
import jax
import jax.numpy as jnp
from jax import lax
from jax.experimental import pallas as pl
from jax.experimental.pallas import tpu as pltpu

F32 = jnp.float32
BF16 = jnp.bfloat16
MESH = pl.DeviceIdType.MESH

N_DEV = 8
EPS = 1e-6
NEG_INF = -1e30
HEAD_DIM = 64
LANES = 128
QBLK = 128
D_IN = 2304
A_Q, A_KV, B_W = 512, 128, 512
N_HEAD_GROUP = 8
NUM_BUCKETS = 32
MAX_DISTANCE = 2048
PATTERNS_A = ((1, 127),)
PATTERNS_B = ((1, 128), (4, 128), (16, 128))
Q_A_COL, K_A_COL, V_A_COL = 0, 4, 5
Q_B_COL, K_B_COL, V_B_COL = 6, 10, 14

ADAM_LR, ADAM_B1, ADAM_B2, ADAM_EPS, ADAM_WD, ADAM_STEP = 0.001, 0.9, 0.999, 1e-08, 0.01, 10

TOKEN_TILE = 512
DW_TILE = 1024
FWD_BLOCKS = 4
BWD_BLOCKS = 4
VMEM_LIMIT = 56 * 1024 * 1024
ARB = "arbitrary"

BIG = ("ffn1_w_gu", "ffn1_w_down", "w_in", "w_out", "ffn2_w_gu", "ffn2_w_down", "w_ple_gate", "w_ple_proj")
GAINS = ("ffn1_pre_g", "ffn1_post_g", "attn_pre_g", "attn_post_g", "ffn2_pre_g", "ffn2_post_g",
         "ple_pre_g", "ple_post_g", "b_out")
SMALL = GAINS + ("b_in", "sinks", "rel_bias")
WEIGHTS = ("rel_bias", "ffn1_pre_g", "ffn1_w_gu", "ffn1_w_down", "ffn1_post_g", "attn_pre_g", "w_in", "b_in",
           "sinks", "w_out", "b_out", "attn_post_g", "ffn2_pre_g", "ffn2_w_gu", "ffn2_w_down", "ffn2_post_g",
           "ple_pre_g", "w_ple_gate", "w_ple_proj", "ple_post_g")


def _params(n_axes):
    return pltpu.CompilerParams(dimension_semantics=(ARB,) * n_axes, vmem_limit_bytes=VMEM_LIMIT)


def _mm(a, b):
    return jnp.dot(a, b, preferred_element_type=F32)


def _mm_nt(a, b):
    return lax.dot_general(a, b, (((1,), (1,)), ((), ())), preferred_element_type=F32)


def _mm_tn(a, b):
    return lax.dot_general(a, b, (((0,), (0,)), ((), ())), preferred_element_type=F32)


def _rstd(x):
    return lax.rsqrt(jnp.mean(x * x, axis=-1, keepdims=True) + EPS)


def _rms_bwd(x, r, gain, dy):
    n = x * r
    gdy = dy * gain
    return r * (gdy - n * jnp.mean(gdy * n, axis=-1, keepdims=True)), dy * n


def _colsum(v):
    return jnp.sum(v, axis=0, keepdims=True)


def _full(shape):
    return pl.BlockSpec(shape, lambda *_: (0,) * len(shape))


def _mesh_place():
    return lax.axis_index("x"), lax.axis_index("y"), lax.axis_index("c")


def _slot(dev):
    return 4 * dev[0] + 2 * dev[1] + dev[2]


def _peers(x, y, c):
    out = []
    for flip in range(1, N_DEV):
        dx, dy, dc = (flip >> 2) & 1, (flip >> 1) & 1, flip & 1
        out.append((1 - x if dx else x, 1 - y if dy else y, 1 - c if dc else c))
    return out


def _side_copies(kind, ins, outs, send_sems, recv_sems, local_sems, sem_row=0):
    n = len(ins)
    x, y, c = _mesh_place()
    me = _slot((x, y, c))
    peers = _peers(x, y, c)

    def src(a, block):
        return ins[a] if kind == "gather" else ins[a].at[block]

    def send(a, k, peer):
        return pltpu.make_async_remote_copy(
            src_ref=src(a, _slot(peer)), dst_ref=outs[a].at[me],
            send_sem=send_sems.at[sem_row + a, k], recv_sem=recv_sems.at[sem_row + a, k],
            device_id=peer, device_id_type=MESH)

    def arrival(a, k, peer):
        return pltpu.make_async_remote_copy(
            src_ref=src(a, _slot(peer)), dst_ref=outs[a].at[_slot(peer)],
            send_sem=send_sems.at[sem_row + a, k], recv_sem=recv_sems.at[sem_row + a, k],
            device_id=peer, device_id_type=MESH)

    def own(a):
        return pltpu.make_async_copy(src(a, me), outs[a].at[me], local_sems.at[sem_row + a, 0])

    def start():
        for k, peer in enumerate(peers):
            for a in range(n):
                send(a, k, peer).start()
        for a in range(n):
            own(a).start()

    def wait():
        for k, peer in enumerate(peers):
            for a in range(n):
                arrival(a, k, peer).wait_recv()
        for k, peer in enumerate(peers):
            for a in range(n):
                send(a, k, peer).wait_send()
        for a in range(n):
            own(a).wait()

    return start, None, wait


N_CHIPS = N_DEV // 2


def _pair_swap(ins, received, send_sems, recv_sems):
    n = len(ins)
    x, y, c = _mesh_place()
    sibling = (x, y, 1 - c)

    def send(a, q):
        return pltpu.make_async_remote_copy(
            src_ref=ins[a].at[2 * q + (1 - c)], dst_ref=received[a].at[q],
            send_sem=send_sems.at[a, q], recv_sem=recv_sems.at[a, q], device_id=sibling, device_id_type=MESH)

    def start():
        for a in range(n):
            for q in range(N_CHIPS):
                send(a, q).start()

    def wait():
        for a in range(n):
            for q in range(N_CHIPS):
                send(a, q).wait_recv()
        for a in range(n):
            for q in range(N_CHIPS):
                send(a, q).wait_send()

    return start, None, wait


def _quad_exchange(ins, outs, send_sems, recv_sems, local_sems, sem_row=0):
    n = len(ins)
    x, y, c = _mesh_place()
    mine = 2 * x + y
    chips = [(1 - x, y), (x, 1 - y), (1 - x, 1 - y)]

    def send(a, k, chip):
        return pltpu.make_async_remote_copy(
            src_ref=ins[a].at[2 * chip[0] + chip[1]], dst_ref=outs[a].at[mine],
            send_sem=send_sems.at[sem_row + a, k], recv_sem=recv_sems.at[sem_row + a, k],
            device_id=(chip[0], chip[1], c), device_id_type=MESH)

    def arrival(a, k, chip):
        return pltpu.make_async_remote_copy(
            src_ref=ins[a].at[2 * chip[0] + chip[1]], dst_ref=outs[a].at[2 * chip[0] + chip[1]],
            send_sem=send_sems.at[sem_row + a, k], recv_sem=recv_sems.at[sem_row + a, k],
            device_id=(chip[0], chip[1], c), device_id_type=MESH)

    def own(a):
        return pltpu.make_async_copy(ins[a].at[mine], outs[a].at[mine], local_sems.at[sem_row + a, 0])

    def start():
        for k, chip in enumerate(chips):
            for a in range(n):
                send(a, k, chip).start()
        for a in range(n):
            own(a).start()

    def wait():
        for k, chip in enumerate(chips):
            for a in range(n):
                arrival(a, k, chip).wait_recv()
        for k, chip in enumerate(chips):
            for a in range(n):
                send(a, k, chip).wait_send()
        for a in range(n):
            own(a).wait()

    return start, None, wait


def _relay_gather(ins, outs, send_sems, recv_sems, local_sems):
    n = len(ins)
    x, y, c = _mesh_place()
    me, sibling = (x, y, c), (x, y, 1 - c)
    chips = [(1 - x, y), (x, 1 - y), (1 - x, 1 - y)]

    def copy(a, k, block, to, src=None):
        dst = outs[a].at[_slot(block)]
        return pltpu.make_async_remote_copy(
            src_ref=dst if src is None else src, dst_ref=dst,
            send_sem=send_sems.at[a, k], recv_sem=recv_sems.at[a, k], device_id=to, device_id_type=MESH)

    def own(a):
        return pltpu.make_async_copy(ins[a], outs[a].at[_slot(me)], local_sems.at[a, 0])

    def start():
        for j, chip in enumerate(chips):
            for a in range(n):
                copy(a, 1 + j, me, (*chip, c), src=ins[a]).start()
        for a in range(n):
            copy(a, 0, me, sibling, src=ins[a]).start()
            own(a).start()

    def relay():
        for j, chip in enumerate(chips):
            for a in range(n):
                copy(a, 1 + j, (*chip, c), me).wait_recv()
                copy(a, 4 + j, (*chip, c), sibling).start()

    def wait():
        for a in range(n):
            copy(a, 0, sibling, me).wait_recv()
        for j, chip in enumerate(chips):
            for a in range(n):
                copy(a, 4 + j, (*chip, 1 - c), me).wait_recv()
        for j, chip in enumerate(chips):
            for a in range(n):
                copy(a, 1 + j, me, (*chip, c), src=ins[a]).wait_send()
                copy(a, 4 + j, (*chip, c), sibling).wait_send()
        for a in range(n):
            copy(a, 0, me, sibling, src=ins[a]).wait_send()
            own(a).wait()

    return start, relay, wait


def _side_out_shapes(kind, arrays):
    if kind in ("gather", "relay_gather"):
        return [jax.ShapeDtypeStruct((N_DEV,) + a.shape, a.dtype) for a in arrays]
    if kind == "pair_swap":
        return [jax.ShapeDtypeStruct((N_CHIPS,) + a.shape[1:], a.dtype) for a in arrays]
    return [jax.ShapeDtypeStruct(a.shape, a.dtype) for a in arrays]


def _hosted_call(body, name, grid, in_specs, out_specs, out_shape, scratch_shapes, args, side=None):
    if side is None:
        outs = pl.pallas_call(
            body, name=name, grid=grid, in_specs=in_specs, out_specs=out_specs, out_shape=out_shape,
            scratch_shapes=scratch_shapes, compiler_params=_params(len(grid)))(*args)
        return outs, []
    kind, arrays = side
    side_shapes = _side_out_shapes(kind, arrays)
    n_in, n_out, n_scr, n_side = len(in_specs), len(out_specs), len(scratch_shapes), len(arrays)

    def hosted(*refs):
        pos = 0
        groups = []
        for size in (n_in, n_side, n_out, len(side_shapes), n_scr):
            groups.append(refs[pos:pos + size])
            pos += size
        ins, side_in, outs, side_out, scr = groups
        send_sems, recv_sems, local_sems = refs[pos:]
        ids = [pl.program_id(d) for d in range(len(grid))]
        is_first = functools.reduce(jnp.logical_and, [i == 0 for i in ids])
        is_last = functools.reduce(jnp.logical_and, [i == g - 1 for i, g in zip(ids, grid)])
        if kind == "relay_gather":
            start, relay, wait = _relay_gather(side_in, side_out, send_sems, recv_sems, local_sems)
        elif kind == "pair_swap":
            start, relay, wait = _pair_swap(side_in, side_out, send_sems, recv_sems)
        elif kind == "quad_exchange":
            start, relay, wait = _quad_exchange(side_in, side_out, send_sems, recv_sems, local_sems)
        else:
            start, relay, wait = _side_copies(kind, side_in, side_out, send_sems, recv_sems, local_sems)
        pl.when(is_first)(start)
        if relay is not None:
            pl.when(is_last)(relay)
        body(*ins, *outs, *scr)
        pl.when(is_last)(wait)

    any_spec = pl.BlockSpec(memory_space=pl.ANY)
    outs = pl.pallas_call(
        hosted, name=name, grid=grid,
        in_specs=list(in_specs) + [any_spec] * n_side,
        out_specs=list(out_specs) + [any_spec] * len(side_shapes),
        out_shape=list(out_shape) + side_shapes,
        scratch_shapes=list(scratch_shapes) + [pltpu.SemaphoreType.DMA((n_side, 7)), pltpu.SemaphoreType.DMA((n_side, 7)),
                                               pltpu.SemaphoreType.DMA((n_side, N_CHIPS))],
        compiler_params=_params(len(grid)))(*args, *arrays)
    return outs[:n_out], outs[n_out:]


def _lane_chunks(width, chunk=2 * LANES):
    return [slice(n0, min(n0 + chunk, width)) for n0 in range(0, width, chunk)]


def _pipelined(chunks, first, middle, last):
    n = len(chunks)
    a, b, total = {}, {}, None
    for step in range(n + 2):
        if step < n:
            a[step] = first(chunks[step])
        if 0 <= step - 1 < n:
            b[step - 1] = middle(chunks[step - 1], a.pop(step - 1))
        if 0 <= step - 2 < n:
            part = last(chunks[step - 2], b.pop(step - 2))
            total = part if total is None else total + part
    return total


def _ffn_fwd(h, g_pre, g_post, w_gu, w_down, name, side=None):
    T, D = h.shape
    nj = w_gu.shape[0] // 2
    FB = w_gu.shape[1]
    tm = TOKEN_TILE

    def body(h_ref, gpre_ref, gpost_ref, wg_ref, wu_ref, wd_ref, hout_ref, f_ref, a_ref, gu_ref, a_scr, acc):
        j = pl.program_id(1)

        @pl.when(j == 0)
        def _():
            x = h_ref[...]
            a = (x * _rstd(x) * gpre_ref[...]).astype(BF16)
            a_scr[...] = a
            a_ref[...] = a
            acc[...] = jnp.zeros_like(acc)

        a = a_scr[...]
        g = _mm_nt(a, wg_ref[...])
        u = _mm_nt(a, wu_ref[...])
        gu_ref[0] = g.astype(BF16)
        gu_ref[1] = u.astype(BF16)
        hh = (g * jax.nn.sigmoid(g) * u).astype(BF16)
        acc[...] += _mm(hh, wd_ref[...])

        @pl.when(j == nj - 1)
        def _():
            f = acc[...]
            f_ref[...] = f
            hout_ref[...] = h_ref[...] + 0.5 * (f * _rstd(f) * gpost_ref[...])

    return _hosted_call(
        body, name, (T // tm, nj),
        in_specs=[
            pl.BlockSpec((tm, D), lambda i, j: (i, 0)),
            _full((1, D)), _full((1, D)),
            pl.BlockSpec((None, FB, D), lambda i, j: (j, 0, 0)),
            pl.BlockSpec((None, FB, D), lambda i, j: (j + nj, 0, 0)),
            pl.BlockSpec((FB, D), lambda i, j: (j, 0)),
        ],
        out_specs=[
            pl.BlockSpec((tm, D), lambda i, j: (i, 0)),
            pl.BlockSpec((tm, D), lambda i, j: (i, 0)),
            pl.BlockSpec((tm, D), lambda i, j: (i, 0)),
            pl.BlockSpec((None, 2, tm, FB), lambda i, j: (j, 0, i, 0)),
        ],
        out_shape=[
            jax.ShapeDtypeStruct((T, D), F32),
            jax.ShapeDtypeStruct((T, D), F32),
            jax.ShapeDtypeStruct((T, D), BF16),
            jax.ShapeDtypeStruct((nj, 2, T, FB), BF16),
        ],
        scratch_shapes=[pltpu.VMEM((tm, D), BF16), pltpu.VMEM((tm, D), F32)],
        args=(h, g_pre, g_post, w_gu, w_gu, w_down), side=side)


def _ffn_bwd(dh_out, f, g_post, h, g_pre, gu, w_gu, w_down, name, side=None):
    T, D = h.shape
    nj = w_gu.shape[0] // 2
    FB = w_gu.shape[1]
    tm = TOKEN_TILE

    def body(dho_ref, f_ref, gpost_ref, h_ref, gpre_ref, gu_ref, wg_ref, wu_ref, wd_ref,
             dhin_ref, df_ref, hh_ref, dgu_ref, dgpost_ref, dgpre_ref, df_scr, da):
        i, j = pl.program_id(0), pl.program_id(1)

        @pl.when(jnp.logical_and(i == 0, j == 0))
        def _():
            dgpost_ref[...] = jnp.zeros_like(dgpost_ref)
            dgpre_ref[...] = jnp.zeros_like(dgpre_ref)

        @pl.when(j == 0)
        def _():
            fv = f_ref[...]
            df, dgain = _rms_bwd(fv, _rstd(fv), gpost_ref[...], 0.5 * dho_ref[...])
            dgpost_ref[...] += _colsum(dgain)
            dfb = df.astype(BF16)
            df_scr[...] = dfb
            df_ref[...] = dfb
            da[...] = jnp.zeros_like(da)

        dfb = df_scr[...]

        halves = (slice(0, tm // 2), slice(tm // 2, tm))

        def hidden_grad(c):
            return [_mm_nt(dfb[rows], wd_ref[c, :]) for rows in halves]

        def through_swiglu(c, dhh):
            dhh = jnp.concatenate(dhh, axis=0)
            g = gu_ref[0, :, c].astype(F32)
            u = gu_ref[1, :, c].astype(F32)
            sg = jax.nn.sigmoid(g)
            silu = g * sg
            hh_ref[:, c] = (silu * u).astype(BF16)
            dg = (dhh * u * (sg * (1.0 + (g - silu)))).astype(BF16)
            du = (dhh * silu).astype(BF16)
            dgu_ref[0, :, c] = dg
            dgu_ref[1, :, c] = du
            return dg, du

        def input_grad(c, dgu):
            return jnp.concatenate(
                [_mm(dgu[0][rows], wg_ref[c, :]) + _mm(dgu[1][rows], wu_ref[c, :]) for rows in halves], axis=0)

        da[...] += _pipelined(_lane_chunks(FB), hidden_grad, through_swiglu, input_grad)

        @pl.when(j == nj - 1)
        def _():
            x = h_ref[...]
            dx, dgain = _rms_bwd(x, _rstd(x), gpre_ref[...], da[...])
            dgpre_ref[...] += _colsum(dgain)
            dhin_ref[...] = dho_ref[...] + dx

    tile = pl.BlockSpec((tm, D), lambda i, j: (i, 0))
    return _hosted_call(
        body, name, (T // tm, nj),
        in_specs=[
            tile, tile, _full((1, D)), tile, _full((1, D)),
            pl.BlockSpec((None, 2, tm, FB), lambda i, j: (j, 0, i, 0)),
            pl.BlockSpec((None, FB, D), lambda i, j: (j, 0, 0)),
            pl.BlockSpec((None, FB, D), lambda i, j: (j + nj, 0, 0)),
            pl.BlockSpec((FB, D), lambda i, j: (j, 0)),
        ],
        out_specs=[
            tile, tile,
            pl.BlockSpec((None, tm, FB), lambda i, j: (j, i, 0)),
            pl.BlockSpec((None, 2, tm, FB), lambda i, j: (j, 0, i, 0)),
            _full((1, D)), _full((1, D)),
        ],
        out_shape=[
            jax.ShapeDtypeStruct((T, D), F32),
            jax.ShapeDtypeStruct((T, D), BF16),
            jax.ShapeDtypeStruct((nj, T, FB), BF16),
            jax.ShapeDtypeStruct((nj, 2, T, FB), BF16),
            jax.ShapeDtypeStruct((1, D), F32),
            jax.ShapeDtypeStruct((1, D), F32),
        ],
        scratch_shapes=[pltpu.VMEM((tm, D), BF16), pltpu.VMEM((tm, D), F32)],
        args=(dh_out, f, g_post, h, g_pre, gu, w_gu, w_gu, w_down), side=side)


def _tn_matmul(x, y, x_spec, y_spec, out_shape, out_spec, n_blocks, n_steps, acc_shape, name, side=None):
    def body(x_ref, y_ref, o_ref, acc):
        t = pl.program_id(1)

        @pl.when(t == 0)
        def _():
            acc[...] = jnp.zeros_like(acc)

        acc[...] += _mm_tn(x_ref[...].astype(BF16), y_ref[...].astype(BF16))

        @pl.when(t == n_steps - 1)
        def _():
            o_ref[...] = acc[...].astype(o_ref.dtype)

    outs, side_outs = _hosted_call(
        body, name, (n_blocks, n_steps), in_specs=[x_spec, y_spec], out_specs=[out_spec], out_shape=[out_shape],
        scratch_shapes=[pltpu.VMEM(acc_shape, F32)], args=(x, y), side=side)
    return (outs[0], side_outs) if side is not None else outs[0]


def _inproj_fwd(h, g_pre, w_in, b_in, side=None):
    T, D = h.shape
    tm = TOKEN_TILE

    def body(h_ref, g_ref, w_ref, b_ref, z_ref, a_ref):
        x = h_ref[...]
        a = (x * _rstd(x) * g_ref[...]).astype(BF16)
        a_ref[...] = a
        z_ref[...] = _mm_nt(a, w_ref[...]) + b_ref[...]

    return _hosted_call(
        body, "inproj_fwd", (T // tm,),
        in_specs=[pl.BlockSpec((tm, D), lambda i: (i, 0)), _full((1, D)), _full((D_IN, D)), _full((1, D_IN))],
        out_specs=[pl.BlockSpec((tm, D_IN), lambda i: (i, 0)), pl.BlockSpec((tm, D), lambda i: (i, 0))],
        out_shape=[jax.ShapeDtypeStruct((T, D_IN), F32), jax.ShapeDtypeStruct((T, D), BF16)],
        scratch_shapes=[], args=(h, g_pre, w_in, b_in), side=side)


def _inproj_bwd(dqa, dka, dva, dqb, dkb, dvb, w_in, h, g_pre, dres, side=None):
    T, D = h.shape
    tm = TOKEN_TILE

    def body(dqa_ref, dka_ref, dva_ref, dqb_ref, dkb_ref, dvb_ref, w_ref, h_ref, g_ref, dres_ref,
             dh_ref, dz_ref, dbin_ref, dg_ref):
        i = pl.program_id(0)

        @pl.when(i == 0)
        def _():
            dbin_ref[...] = jnp.zeros_like(dbin_ref)
            dg_ref[...] = jnp.zeros_like(dg_ref)

        dz = jnp.concatenate([dqa_ref[...], dka_ref[...], dva_ref[...], dqb_ref[...], dkb_ref[...], dvb_ref[...]],
                             axis=1)
        dbin_ref[...] += _colsum(dz)
        dzb = dz.astype(BF16)
        dz_ref[...] = dzb
        da = _mm(dzb, w_ref[...])
        x = h_ref[...]
        dx, dgain = _rms_bwd(x, _rstd(x), g_ref[...], da)
        dg_ref[...] += _colsum(dgain)
        dh_ref[...] = dres_ref[...] + dx

    def tile(w):
        return pl.BlockSpec((tm, w), lambda i: (i, 0))

    return _hosted_call(
        body, "inproj_bwd", (T // tm,),
        in_specs=[tile(A_Q), tile(A_KV), tile(A_KV), tile(B_W), tile(B_W), tile(B_W),
                  _full((D_IN, D)), tile(D), _full((1, D)), tile(D)],
        out_specs=[tile(D), tile(D_IN), _full((1, D_IN)), _full((1, D))],
        out_shape=[jax.ShapeDtypeStruct((T, D), F32), jax.ShapeDtypeStruct((T, D_IN), BF16),
                   jax.ShapeDtypeStruct((1, D_IN), F32), jax.ShapeDtypeStruct((1, D), F32)],
        scratch_shapes=[], args=(dqa, dka, dva, dqb, dkb, dvb, w_in, h, g_pre, dres), side=side)


def _bucket_tiles(patterns):
    i = np.arange(QBLK)[:, None]
    j = np.arange(2 * QBLK)[None, :]
    dist = QBLK + i - j
    max_exact = NUM_BUCKETS // 2
    tiles = []
    for dilation, max_dist in patterns:
        n = np.maximum(dist * dilation, 0)
        nf = np.maximum(n, 1).astype(np.float32)
        large = max_exact + (np.log(nf / np.float32(max_exact)) / np.float32(math.log(MAX_DISTANCE / max_exact))
                             * np.float32(NUM_BUCKETS - max_exact)).astype(np.int32)
        bucket = np.where(n < max_exact, n, np.minimum(large, NUM_BUCKETS - 1))
        tiles.append(np.where((dist >= 0) & (dist <= max_dist), bucket, -1))
    return jnp.asarray(np.stack(tiles).astype(np.int32))


def _bias_build(rel_bias, buckets, head0, name, side=None):
    n = buckets.shape[0]

    def body(bk_ref, rb_ref, o_ref):
        bk = bk_ref[...]
        base = jnp.where(bk < 0, NEG_INF, 0.0).astype(F32)
        for hd in range(N_HEAD_GROUP):
            o_ref[hd] = lax.fori_loop(
                0, NUM_BUCKETS, lambda b, acc, hd=hd: jnp.where(bk == b, rb_ref[b, head0 + hd], acc), base)

    outs, side_outs = _hosted_call(
        body, name, (n,),
        in_specs=[pl.BlockSpec((None, QBLK, 2 * QBLK), lambda p: (p, 0, 0)), pl.BlockSpec(memory_space=pltpu.SMEM)],
        out_specs=[pl.BlockSpec((None, N_HEAD_GROUP, QBLK, 2 * QBLK), lambda p: (p, 0, 0, 0))],
        out_shape=[jax.ShapeDtypeStruct((n, N_HEAD_GROUP, QBLK, 2 * QBLK), F32)],
        scratch_shapes=[], args=(buckets, rel_bias), side=side)
    return outs[0], side_outs


def _bias_grad(ds, buckets, name):
    n = buckets.shape[0]

    def body(ds_ref, bk_ref, o_ref):
        bk = bk_ref[...]
        row = lax.broadcasted_iota(jnp.int32, (NUM_BUCKETS, 2 * QBLK), 0)
        for hd in range(N_HEAD_GROUP):
            d = ds_ref[hd]
            per_key = jnp.zeros((NUM_BUCKETS, 2 * QBLK), F32)
            for b in range(NUM_BUCKETS):
                per_key = jnp.where(row == b, jnp.sum(jnp.where(bk == b, d, 0.0), axis=0, keepdims=True), per_key)
            o_ref[hd] = jnp.broadcast_to(jnp.sum(per_key, axis=1, keepdims=True), (NUM_BUCKETS, LANES))

    out = pl.pallas_call(
        body, name=name, grid=(n,),
        in_specs=[pl.BlockSpec((None, N_HEAD_GROUP, QBLK, 2 * QBLK), lambda p: (p, 0, 0, 0)),
                  pl.BlockSpec((None, QBLK, 2 * QBLK), lambda p: (p, 0, 0))],
        out_specs=pl.BlockSpec((None, N_HEAD_GROUP, NUM_BUCKETS, LANES), lambda p: (p, 0, 0, 0)),
        out_shape=jax.ShapeDtypeStruct((n, N_HEAD_GROUP, NUM_BUCKETS, LANES), F32),
        compiler_params=_params(1),
    )(ds, buckets)
    return out[:, :, :, 0].reshape(n * N_HEAD_GROUP, NUM_BUCKETS)


def _class_rows(start, dilation):
    if dilation == 1:
        return pl.ds(pl.multiple_of(start, QBLK), QBLK)
    return pl.ds(start, QBLK, stride=dilation)


def _block_starts(idx, n_blocks, dilation):
    cls = idx // n_blocks
    n = idx % n_blocks
    cur = cls + dilation * QBLK * n
    prev = cls + dilation * QBLK * jnp.maximum(n - 1, 0)
    return n, cur, prev


class _HeadPair:
    def __init__(self, g, shared_kv):
        self.lane = lax.broadcasted_iota(jnp.int32, (1, LANES), 1)
        self.lower = self.lane < HEAD_DIM
        self.shared_kv = shared_kv
        self.key_lanes = (self.lane >= HEAD_DIM).astype(jnp.int32) == (g // 2)

    def stack(self, t):
        return jnp.concatenate([jnp.where(self.lower, t, 0.0), jnp.where(self.lower, 0.0, t)], axis=0).astype(BF16)

    def unstack(self, t2):
        return jnp.where(self.lower, t2[:QBLK], t2[QBLK:])

    def keys(self, t):
        if self.shared_kv:
            return jnp.where(self.key_lanes, t, pltpu.roll(t, HEAD_DIM, 1))
        return t

    def key_grads(self, t):
        if self.shared_kv:
            return jnp.where(self.key_lanes, t + pltpu.roll(t, HEAD_DIM, 1), 0.0)
        return t


def _attn_specs(T, qcol, kcol, vcol, shared_kv):
    kv = (lambda c: (lambda g: (0, c))) if shared_kv else (lambda c: (lambda g: (0, c + g)))
    return [pl.BlockSpec((T, LANES), lambda g: (0, qcol + g)),
            pl.BlockSpec((T, LANES), kv(kcol)),
            pl.BlockSpec((T, LANES), kv(vcol))]


def _attn_fwd(z, bias, sinks, patterns, qcol, kcol, vcol, shared_kv, name, side=None):
    T = z.shape[0]
    n_pat = len(patterns)
    has_sink = sinks is not None

    def body(*refs):
        if has_sink:
            sink_ref, refs = refs[0], refs[1:]
        q_ref, k_ref, v_ref, b_ref, o_ref, l_ref = refs[:6]
        po_scr = refs[6:6 + n_pat]
        pl_scr = refs[6 + n_pat:]
        g = pl.program_id(0)
        heads = _HeadPair(g, shared_kv)
        in_prev = lax.broadcasted_iota(jnp.int32, (2 * QBLK, 2 * QBLK), 1) < QBLK

        for pi, (dilation, _) in enumerate(patterns):
            n_blocks = T // (QBLK * dilation)

            def step(it, carry, pi=pi, dilation=dilation, n_blocks=n_blocks):
                blocks = []
                for u in range(FWD_BLOCKS):
                    n, cur, prev = _block_starts(it * FWD_BLOCKS + u, n_blocks, dilation)
                    rows_c, rows_p = _class_rows(cur, dilation), _class_rows(prev, dilation)
                    qm = heads.stack(q_ref[rows_c, :])
                    k_cur, v_cur = k_ref[rows_c, :], v_ref[rows_c, :]
                    if u % min(FWD_BLOCKS, n_blocks) == 0:
                        k_prev, v_prev = k_ref[rows_p, :], v_ref[rows_p, :]
                    k2 = heads.keys(jnp.concatenate([k_prev, k_cur], axis=0)).astype(BF16)
                    v2 = heads.keys(jnp.concatenate([v_prev, v_cur], axis=0)).astype(BF16)
                    k_prev, v_prev = k_cur, v_cur
                    blocks.append(dict(n=n, rows=rows_c, v2=v2, s=_mm_nt(qm, k2)))
                for b in blocks:
                    s = b["s"] * (HEAD_DIM ** -0.5) + b_ref[pi]
                    b["s"] = jnp.where(jnp.logical_and(in_prev, b["n"] == 0), NEG_INF, s)
                    b["m"] = jnp.max(b["s"], axis=1, keepdims=True)
                for b in blocks:
                    b["pr"] = jnp.exp(b["s"] - b["m"])
                    b["den"] = jnp.sum(b["pr"], axis=1, keepdims=True)
                for b in blocks:
                    b["o2"] = _mm(b["pr"].astype(BF16), b["v2"])
                for b in blocks:
                    lse = b["m"] + jnp.log(b["den"])
                    po_scr[pi][b["rows"], :] = heads.unstack(b["o2"] / b["den"])
                    pl_scr[2 * pi][b["rows"], :] = jnp.broadcast_to(lse[:QBLK], (QBLK, LANES))
                    pl_scr[2 * pi + 1][b["rows"], :] = jnp.broadcast_to(lse[QBLK:], (QBLK, LANES))
                return carry

            lax.fori_loop(0, (dilation * n_blocks) // FWD_BLOCKS, step, 0)

        def merge(ci, carry):
            rows = pl.ds(pl.multiple_of(ci * QBLK, QBLK), QBLK)
            weights = []
            for hd in range(2):
                parts = [pl_scr[2 * pi + hd][rows, :] for pi in range(n_pat)]
                m = functools.reduce(jnp.maximum, parts)
                if has_sink:
                    sink = sink_ref[0, 2 * g + hd]
                    m = jnp.maximum(m, sink)
                terms = [jnp.exp(x - m) for x in parts]
                den = functools.reduce(jnp.add, terms)
                if has_sink:
                    den = den + jnp.exp(sink - m)
                l_ref[hd, rows, :] = m + jnp.log(den)
                inv = 1.0 / den
                weights.append([t * inv for t in terms])
            o_ref[rows, :] = functools.reduce(
                jnp.add, [jnp.where(heads.lower, weights[0][pi], weights[1][pi]) * po_scr[pi][rows, :]
                          for pi in range(n_pat)])
            return carry

        lax.fori_loop(0, T // QBLK, merge, 0)

    in_specs = _attn_specs(T, qcol, kcol, vcol, shared_kv)
    in_specs.append(pl.BlockSpec((n_pat, None, 2 * QBLK, 2 * QBLK), lambda g: (0, g, 0, 0)))
    args = [z, z, z, bias.reshape(n_pat, N_HEAD_GROUP // 2, 2 * QBLK, 2 * QBLK)]
    if has_sink:
        in_specs.insert(0, pl.BlockSpec(memory_space=pltpu.SMEM))
        args.insert(0, sinks)
    return _hosted_call(
        body, name, (N_HEAD_GROUP // 2,),
        in_specs=in_specs,
        out_specs=[pl.BlockSpec((T, LANES), lambda g: (0, g)), pl.BlockSpec((2, T, LANES), lambda g: (g, 0, 0))],
        out_shape=[jax.ShapeDtypeStruct((T, N_HEAD_GROUP * HEAD_DIM), F32),
                   jax.ShapeDtypeStruct((N_HEAD_GROUP, T, LANES), F32)],
        scratch_shapes=[pltpu.VMEM((T, LANES), F32)] * (3 * n_pat), args=args, side=side)


def _attn_bwd(z, bias, sinks, d_out, out, lse, patterns, qcol, kcol, vcol, shared_kv, name, side=None):
    T = z.shape[0]
    n_pat = len(patterns)
    has_sink = sinks is not None
    kv_width = LANES if shared_kv else N_HEAD_GROUP * HEAD_DIM

    def body(*refs):
        if has_sink:
            sink_ref, refs = refs[0], refs[1:]
        q_ref, k_ref, v_ref, b_ref, do_ref, o_ref, l0_ref, l1_ref = refs[:8]
        dq_ref, dk_ref, dv_ref, ds_ref = refs[8:12]
        dsink_ref = refs[12] if has_sink else None
        dk_acc, dv_acc = refs[-2:]
        g = pl.program_id(0)
        heads = _HeadPair(g, shared_kv)
        in_prev = lax.broadcasted_iota(jnp.int32, (2 * QBLK, 2 * QBLK), 1) < QBLK

        dq_ref[...] = jnp.zeros_like(dq_ref)
        ds_ref[...] = jnp.zeros_like(ds_ref)
        dk_acc[...] = jnp.zeros_like(dk_acc)
        dv_acc[...] = jnp.zeros_like(dv_acc)

        dsink = jnp.zeros((1, LANES), F32)
        for pi, (dilation, _) in enumerate(patterns):
            n_blocks = T // (QBLK * dilation)

            def step(idx, dsink, pi=pi, dilation=dilation, n_blocks=n_blocks):
                blocks = []
                for u in range(BWD_BLOCKS):
                    n, cur, prev = _block_starts(idx * BWD_BLOCKS + u, n_blocks, dilation)
                    rows_c, rows_p = _class_rows(cur, dilation), _class_rows(prev, dilation)
                    qm = heads.stack(q_ref[rows_c, :])
                    k_cur, v_cur = k_ref[rows_c, :], v_ref[rows_c, :]
                    first = u % min(BWD_BLOCKS, n_blocks) == 0
                    if first:
                        k_prev, v_prev = k_ref[rows_p, :], v_ref[rows_p, :]
                    k2 = heads.keys(jnp.concatenate([k_prev, k_cur], axis=0)).astype(BF16)
                    v2 = heads.keys(jnp.concatenate([v_prev, v_cur], axis=0)).astype(BF16)
                    k_prev, v_prev = k_cur, v_cur
                    d_o = do_ref[rows_c, :]
                    dom = heads.stack(d_o)
                    dd = d_o * o_ref[rows_c, :]
                    delta = jnp.concatenate([jnp.sum(jnp.where(heads.lower, dd, 0.0), axis=1, keepdims=True),
                                             jnp.sum(jnp.where(heads.lower, 0.0, dd), axis=1, keepdims=True)], axis=0)
                    lse = jnp.concatenate([l0_ref[rows_c, :], l1_ref[rows_c, :]], axis=0)
                    blocks.append(dict(n=n, first=first, rows_c=rows_c, rows_p=rows_p, qm=qm, k2=k2, dom=dom,
                                       delta=delta, lse=lse, s=_mm_nt(qm, k2), dp=_mm_nt(dom, v2)))
                for b in blocks:
                    s = b["s"] * (HEAD_DIM ** -0.5) + b_ref[pi]
                    s = jnp.where(jnp.logical_and(in_prev, b["n"] == 0), NEG_INF, s)
                    b["pr"] = jnp.exp(s - jnp.concatenate([b["lse"], b["lse"]], axis=1))
                    b["ds"] = b["pr"] * (b["dp"] - b["delta"])
                for b in blocks:
                    dsb = b["ds"].astype(BF16)
                    b["dq2"] = _mm(dsb, b["k2"])
                    b["dk2"] = _mm_tn(dsb, b["qm"])
                    b["dv2"] = _mm_tn(b["pr"].astype(BF16), b["dom"])
                for b in blocks:
                    b["dk2"] = heads.key_grads(b["dk2"]) * (HEAD_DIM ** -0.5)
                    b["dv2"] = heads.key_grads(b["dv2"])
                for u, b in enumerate(blocks):
                    ds_ref[pi] += b["ds"]
                    dq_ref[b["rows_c"], :] += heads.unstack(b["dq2"]) * (HEAD_DIM ** -0.5)
                    dk_own, dv_own = b["dk2"][QBLK:], b["dv2"][QBLK:]
                    if u + 1 < len(blocks) and not blocks[u + 1]["first"]:
                        dk_own = dk_own + blocks[u + 1]["dk2"][:QBLK]
                        dv_own = dv_own + blocks[u + 1]["dv2"][:QBLK]
                    if b["first"]:
                        dk_acc[b["rows_p"], :] += b["dk2"][:QBLK]
                        dv_acc[b["rows_p"], :] += b["dv2"][:QBLK]
                    dk_acc[b["rows_c"], :] += dk_own
                    dv_acc[b["rows_c"], :] += dv_own
                    if has_sink:
                        for hd in range(2):
                            rows_h = slice(QBLK * hd, QBLK * (hd + 1))
                            p_sink = jnp.exp(sink_ref[0, 2 * g + hd] - b["lse"][rows_h, 0:1])
                            dsink = dsink - jnp.where(heads.lane == 2 * g + hd,
                                                      jnp.sum(p_sink * b["delta"][rows_h]), 0.0)
                return dsink

            dsink = lax.fori_loop(0, (dilation * n_blocks) // BWD_BLOCKS, step, dsink)

        if shared_kv:
            @pl.when(g == 0)
            def _():
                dk_ref[...] = dk_acc[...]
                dv_ref[...] = dv_acc[...]

            @pl.when(g != 0)
            def _():
                dk_ref[...] += dk_acc[...]
                dv_ref[...] += dv_acc[...]
        else:
            dk_ref[...] = dk_acc[...]
            dv_ref[...] = dv_acc[...]

        if has_sink:
            @pl.when(g == 0)
            def _():
                dsink_ref[...] = dsink

            @pl.when(g != 0)
            def _():
                dsink_ref[...] += dsink

    pair = pl.BlockSpec((T, LANES), lambda g: (0, g))
    stacked = pl.BlockSpec((n_pat, None, 2 * QBLK, 2 * QBLK), lambda g: (0, g, 0, 0))
    stacked_shape = (n_pat, N_HEAD_GROUP // 2, 2 * QBLK, 2 * QBLK)
    in_specs = _attn_specs(T, qcol, kcol, vcol, shared_kv)
    in_specs += [stacked, pair, pair,
                 pl.BlockSpec((None, T, LANES), lambda g: (2 * g, 0, 0)),
                 pl.BlockSpec((None, T, LANES), lambda g: (2 * g + 1, 0, 0))]
    args = [z, z, z, bias.reshape(stacked_shape), d_out, out, lse, lse]
    kv_out = _full((T, LANES)) if shared_kv else pair
    out_specs = [pair, kv_out, kv_out, stacked]
    out_shape = [jax.ShapeDtypeStruct((T, N_HEAD_GROUP * HEAD_DIM), F32),
                 jax.ShapeDtypeStruct((T, kv_width), F32), jax.ShapeDtypeStruct((T, kv_width), F32),
                 jax.ShapeDtypeStruct(stacked_shape, F32)]
    if has_sink:
        in_specs.insert(0, pl.BlockSpec(memory_space=pltpu.SMEM))
        args.insert(0, sinks)
        out_specs.append(_full((1, LANES)))
        out_shape.append(jax.ShapeDtypeStruct((1, LANES), F32))
    outs, side_outs = _hosted_call(
        body, name, (N_HEAD_GROUP // 2,), in_specs=in_specs, out_specs=out_specs, out_shape=out_shape,
        scratch_shapes=[pltpu.VMEM((T, LANES), F32), pltpu.VMEM((T, LANES), F32)], args=args, side=side)
    outs = list(outs)
    outs[3] = outs[3].reshape(n_pat, N_HEAD_GROUP, QBLK, 2 * QBLK)
    return outs, side_outs


def _outproj_fwd(mix_a, mix_b, w_out, b_out, g_post, h):
    T, D = h.shape
    tm = TOKEN_TILE
    d_mix = w_out.shape[0]

    def body(ma_ref, mb_ref, w_ref, b_ref, g_ref, h_ref, att_ref, hout_ref, mix_ref):
        mix = jnp.concatenate([ma_ref[...], mb_ref[...]], axis=1).astype(BF16)
        mix_ref[...] = mix
        att = _mm(mix, w_ref[...]) + b_ref[...]
        att_ref[...] = att
        hout_ref[...] = h_ref[...] + att * _rstd(att) * g_ref[...]

    def tile(w):
        return pl.BlockSpec((tm, w), lambda i: (i, 0))

    return pl.pallas_call(
        body, name="outproj_fwd", grid=(T // tm,),
        in_specs=[tile(A_Q), tile(B_W), _full((d_mix, D)), _full((1, D)), _full((1, D)), tile(D)],
        out_specs=[tile(D), tile(D), tile(d_mix)],
        out_shape=[jax.ShapeDtypeStruct((T, D), F32), jax.ShapeDtypeStruct((T, D), F32),
                   jax.ShapeDtypeStruct((T, d_mix), BF16)],
        compiler_params=_params(1),
    )(mix_a, mix_b, w_out, b_out, g_post, h)


def _outproj_bwd(dh, att, g_post, w_out):
    T, D = dh.shape
    tm = TOKEN_TILE
    d_mix = w_out.shape[0]

    def body(dh_ref, att_ref, g_ref, w_ref, dma_ref, dmb_ref, datt_ref, dg_ref, db_ref):
        i = pl.program_id(0)

        @pl.when(i == 0)
        def _():
            dg_ref[...] = jnp.zeros_like(dg_ref)
            db_ref[...] = jnp.zeros_like(db_ref)

        att = att_ref[...]
        datt, dgain = _rms_bwd(att, _rstd(att), g_ref[...], dh_ref[...])
        dg_ref[...] += _colsum(dgain)
        db_ref[...] += _colsum(datt)
        dattb = datt.astype(BF16)
        datt_ref[...] = dattb
        dmix = _mm_nt(dattb, w_ref[...])
        dma_ref[...] = dmix[:, :A_Q]
        dmb_ref[...] = dmix[:, A_Q:]

    def tile(w):
        return pl.BlockSpec((tm, w), lambda i: (i, 0))

    return pl.pallas_call(
        body, name="outproj_bwd", grid=(T // tm,),
        in_specs=[tile(D), tile(D), _full((1, D)), _full((d_mix, D))],
        out_specs=[tile(A_Q), tile(B_W), tile(D), _full((1, D)), _full((1, D))],
        out_shape=[jax.ShapeDtypeStruct((T, A_Q), F32), jax.ShapeDtypeStruct((T, B_W), F32),
                   jax.ShapeDtypeStruct((T, D), BF16), jax.ShapeDtypeStruct((1, D), F32),
                   jax.ShapeDtypeStruct((1, D), F32)],
        compiler_params=_params(1),
    )(dh, att, g_post, w_out)


def _ple_fwd_loss(h, g_pre, w_gate, p, w_proj, g_post, target):
    T, D = h.shape
    tm = TOKEN_TILE
    n_proj, ple, db = w_proj.shape

    def body(h_ref, gpre_ref, wg_ref, p_ref, wp_ref, gpost_ref, t_ref,
             a_ref, dpre_ref, de_ref, dh_ref, loss_ref, dgpost_ref):
        i = pl.program_id(0)

        @pl.when(i == 0)
        def _():
            loss_ref[...] = jnp.zeros_like(loss_ref)
            dgpost_ref[...] = jnp.zeros_like(dgpost_ref)

        x = h_ref[...]
        a = (x * _rstd(x) * gpre_ref[...]).astype(BF16)
        a_ref[...] = a
        gate = jax.nn.sigmoid(_mm(a, wg_ref[...]))
        pb = p_ref[...].astype(BF16)
        e = jnp.concatenate([_mm(pb, wp_ref[k]) for k in range(n_proj)], axis=1)
        ge = gate * e
        rg = _rstd(ge)
        diff = x + ge * rg * gpost_ref[...] - t_ref[...]
        loss_ref[...] += 0.5 * jnp.sum(jnp.mean(diff * diff, axis=1, keepdims=True))
        dy = diff * (1.0 / D)
        dh_ref[...] = dy
        dge, dgain = _rms_bwd(ge, rg, gpost_ref[...], dy)
        dgpost_ref[...] += _colsum(dgain)
        de_ref[...] = (dge * gate).astype(BF16)
        dpre_ref[...] = (dge * e * gate * (1.0 - gate)).astype(BF16)

    def tile(w):
        return pl.BlockSpec((tm, w), lambda i: (i, 0))

    return pl.pallas_call(
        body, name="ple_fwd_loss", grid=(T // tm,),
        in_specs=[tile(D), _full((1, D)), _full((D, D)), tile(ple), _full((n_proj, ple, db)), _full((1, D)), tile(D)],
        out_specs=[tile(D), tile(D), tile(D), tile(D), _full((1, LANES)), _full((1, D))],
        out_shape=[jax.ShapeDtypeStruct((T, D), BF16),
                   jax.ShapeDtypeStruct((T, D), BF16),
                   jax.ShapeDtypeStruct((T, D), BF16),
                   jax.ShapeDtypeStruct((T, D), F32),
                   jax.ShapeDtypeStruct((1, LANES), F32),
                   jax.ShapeDtypeStruct((1, D), F32)],
        compiler_params=_params(1),
    )(h, g_pre, w_gate, p, w_proj, g_post, target)


def _ple_bwd(dpre, w_gate, h, g_pre, dres):
    T, D = h.shape
    tm = TOKEN_TILE

    def body(dpre_ref, w_ref, h_ref, g_ref, dres_ref, dh_ref, dg_ref):
        i = pl.program_id(0)

        @pl.when(i == 0)
        def _():
            dg_ref[...] = jnp.zeros_like(dg_ref)

        da = _mm_nt(dpre_ref[...], w_ref[...])
        x = h_ref[...]
        dx, dgain = _rms_bwd(x, _rstd(x), g_ref[...], da)
        dg_ref[...] += _colsum(dgain)
        dh_ref[...] = dres_ref[...] + dx

    tile = pl.BlockSpec((tm, D), lambda i: (i, 0))
    return pl.pallas_call(
        body, name="ple_bwd", grid=(T // tm,),
        in_specs=[tile, _full((D, D)), tile, _full((1, D)), tile],
        out_specs=[tile, _full((1, D))],
        out_shape=[jax.ShapeDtypeStruct((T, D), F32), jax.ShapeDtypeStruct((1, D), F32)],
        compiler_params=_params(1),
    )(dpre, w_gate, h, g_pre, dres)


def _ple_dw_proj(p, de, n_proj):
    T, ple = p.shape
    D = de.shape[1]
    db = D // n_proj
    tk = TOKEN_TILE
    nt = T // tk

    def body(p_ref, de_ref, o_ref, acc):
        t = pl.program_id(0)

        @pl.when(t == 0)
        def _():
            acc[...] = jnp.zeros_like(acc)

        acc[...] += _mm_tn(p_ref[...].astype(BF16), de_ref[...])

        @pl.when(t == nt - 1)
        def _():
            for k in range(n_proj):
                o_ref[k] = acc[:, k * db:(k + 1) * db].astype(BF16)

    return pl.pallas_call(
        body, name="ple_dw_proj", grid=(nt,),
        in_specs=[pl.BlockSpec((tk, ple), lambda t: (t, 0)), pl.BlockSpec((tk, D), lambda t: (t, 0))],
        out_specs=_full((n_proj, ple, db)), out_shape=jax.ShapeDtypeStruct((n_proj, ple, db), BF16),
        scratch_shapes=[pltpu.VMEM((ple, D), F32)], compiler_params=_params(1),
    )(p, de)


def _tok(width):
    return pl.BlockSpec((DW_TILE, width), lambda b, t: (t, 0))


def _dw_gu(a, dgu, name, side=None):
    T, D = a.shape
    nj, _, _, FB = dgu.shape
    return _tn_matmul(
        dgu, a, pl.BlockSpec((None, None, DW_TILE, FB), lambda b, t: (b % nj, b // nj, t, 0)), _tok(D),
        jax.ShapeDtypeStruct((2 * nj, FB, D), BF16), pl.BlockSpec((None, FB, D), lambda b, t: (b, 0, 0)),
        2 * nj, T // DW_TILE, (FB, D), name, side=side)


def _dw_down(hh, df, name, side=None):
    nj, T, FB = hh.shape
    D = df.shape[1]
    return _tn_matmul(
        hh, df, pl.BlockSpec((None, DW_TILE, FB), lambda b, t: (b, t, 0)), _tok(D),
        jax.ShapeDtypeStruct((nj, FB, D), BF16), pl.BlockSpec((None, FB, D), lambda b, t: (b, 0, 0)),
        nj, T // DW_TILE, (FB, D), name, side=side)


def _dw_rows(xm, y, name, rows):
    T, k = xm.shape
    D = y.shape[1]
    out = _tn_matmul(
        xm, y, pl.BlockSpec((DW_TILE, rows), lambda b, t: (t, b)), _tok(D),
        jax.ShapeDtypeStruct((k, D), BF16), pl.BlockSpec((rows, D), lambda b, t: (b, 0)),
        k // rows, T // DW_TILE, (rows, D), name)
    return out.reshape(N_DEV, k // N_DEV, D)


def _cast_bf16(arrays):
    n = len(arrays)

    def body(*refs):
        for a in range(n):
            refs[n + a][...] = refs[a][...].astype(BF16)

    return pl.pallas_call(
        body, name="cast_shards",
        in_specs=[pl.BlockSpec(memory_space=pltpu.VMEM)] * n, out_specs=[pl.BlockSpec(memory_space=pltpu.VMEM)] * n,
        out_shape=[jax.ShapeDtypeStruct(a.shape, BF16) for a in arrays],
        compiler_params=pltpu.CompilerParams(vmem_limit_bytes=VMEM_LIMIT),
    )(*arrays)


def _pack_layout(D, n_rel_rows):
    n_bin = -(-D_IN // D)
    row_bin = len(GAINS)
    row_sink = row_bin + n_bin
    row_loss = row_sink + 1
    row_rb = -(-(row_loss + 1) // 8) * 8
    n_rows = row_rb + -(-n_rel_rows // 8) * 8
    bin_parts = [(r, min(D, D_IN - r * D)) for r in range(n_bin)]
    return row_bin, row_sink, row_loss, row_rb, n_rows, bin_parts


def _pair_swap_call(grad_blocks):
    def body(g_in, received, send_sems, recv_sems):
        start, _, wait = _pair_swap([g_in], [received], send_sems, recv_sems)
        start()
        wait()

    any_spec = pl.BlockSpec(memory_space=pl.ANY)
    return pl.pallas_call(
        body, name="pair_swap", in_specs=[any_spec], out_specs=any_spec,
        out_shape=_side_out_shapes("pair_swap", [grad_blocks])[0],
        scratch_shapes=[pltpu.SemaphoreType.DMA((1, N_CHIPS)), pltpu.SemaphoreType.DMA((1, N_CHIPS))],
    )(grad_blocks)


def _pair_add(blocks, received, name):
    n, R, C = received.shape
    rows = _adamw_rows(R)
    core = lax.axis_index("c").astype(jnp.int32).reshape(1)

    def body(core_ref, a_ref, b_ref, o_ref):
        o_ref[...] = (a_ref[...].astype(F32) + b_ref[...].astype(F32)).astype(o_ref.dtype)

    tile = pl.BlockSpec((None, rows, C), lambda q, r, core_ref: (q, r, 0))
    return pl.pallas_call(
        body, name=name,
        grid_spec=pltpu.PrefetchScalarGridSpec(
            num_scalar_prefetch=1, grid=(n, R // rows),
            in_specs=[pl.BlockSpec((None, rows, C), lambda q, r, core_ref: (2 * q + core_ref[0], r, 0)), tile],
            out_specs=tile),
        out_shape=jax.ShapeDtypeStruct(received.shape, received.dtype), compiler_params=_params(2),
    )(core, blocks, received)


def _final_exchange(grad_blocks, partials, loss):
    D = partials["ffn1_pre_g"].shape[1]
    rb_shape = partials["rel_bias"].shape
    row_bin, row_sink, row_loss, row_rb, n_rows, bin_parts = _pack_layout(D, rb_shape[0])
    n_small = len(SMALL)

    def body(*refs):
        g_in = refs[0]
        part = dict(zip(SMALL, refs[1:1 + n_small]))
        loss_ref = refs[1 + n_small]
        landed, gath, pack, send_sems, recv_sems, local_sems = refs[2 + n_small:]

        pack[...] = jnp.zeros_like(pack)
        for i, name in enumerate(GAINS):
            pack[i:i + 1, :] = part[name][...]
        for r, width in bin_parts:
            pack[row_bin + r:row_bin + r + 1, 0:width] = part["b_in"][:, r * D:r * D + width]
        pack[row_sink:row_sink + 1, 0:LANES] = part["sinks"][...]
        pack[row_loss:row_loss + 1, 0:LANES] = loss_ref[...]
        pack[row_rb:row_rb + rb_shape[0], 0:rb_shape[1]] = part["rel_bias"][...]

        small_start, _, small_wait = _side_copies("gather", [pack], [gath], send_sems, recv_sems, local_sems, sem_row=0)
        big_start, _, big_wait = _quad_exchange([g_in], [landed], send_sems, recv_sems, local_sems, sem_row=1)
        small_start()
        big_start()
        small_wait()
        big_wait()

    args = [grad_blocks] + [partials[k] for k in SMALL] + [loss]
    vmem = pl.BlockSpec(memory_space=pltpu.VMEM)
    any_spec = pl.BlockSpec(memory_space=pl.ANY)
    return pl.pallas_call(
        body, name="final_exchange",
        in_specs=[any_spec] + [vmem] * (n_small + 1),
        out_specs=[any_spec, any_spec],
        out_shape=[jax.ShapeDtypeStruct(grad_blocks.shape, grad_blocks.dtype),
                   jax.ShapeDtypeStruct((N_DEV, n_rows, D), F32)],
        scratch_shapes=[pltpu.VMEM((n_rows, D), F32), pltpu.SemaphoreType.DMA((2, 7)),
                        pltpu.SemaphoreType.DMA((2, 7)), pltpu.SemaphoreType.DMA((2, N_CHIPS))],
    )(*args)


def _adamw(w, g, m, v):
    m = ADAM_B1 * m + (1.0 - ADAM_B1) * g
    v = ADAM_B2 * v + (1.0 - ADAM_B2) * (g * g)
    m_hat = m / (1.0 - ADAM_B1 ** ADAM_STEP)
    v_hat = v / (1.0 - ADAM_B2 ** ADAM_STEP)
    return -ADAM_LR * (m_hat / (jnp.sqrt(v_hat) + ADAM_EPS) + ADAM_WD * w), m, v


def _sum_adamw(partials, w, m, v, rows, name):
    R, C = w.shape
    n = partials.shape[0]

    def body(p_ref, w_ref, m_ref, v_ref, g_ref, d_ref, nm_ref, nv_ref):
        g = p_ref[0].astype(F32)
        for k in range(1, n):
            g = g + p_ref[k].astype(F32)
        g_ref[...] = g
        d_ref[...], nm_ref[...], nv_ref[...] = _adamw(w_ref[...], g, m_ref[...], v_ref[...])

    tile = pl.BlockSpec((rows, C), lambda i: (i, 0))
    return pl.pallas_call(
        body, name=name, grid=(R // rows,),
        in_specs=[pl.BlockSpec((n, rows, C), lambda i: (0, i, 0)), tile, tile, tile],
        out_specs=[tile] * 4, out_shape=[jax.ShapeDtypeStruct((R, C), F32)] * 4,
        compiler_params=_params(1),
    )(partials, w, m, v)


def _small_adamw(gathered, ws, ms, vs):
    D = ws["ffn1_pre_g"].shape[1]
    n_sink = ws["sinks"].shape[1]
    rb_shape = ws["rel_bias"].shape
    row_bin, row_sink, row_loss, row_rb, n_rows, bin_parts = _pack_layout(D, rb_shape[0])
    n_small = len(SMALL)

    def body(*refs):
        gath = refs[0]
        pos = 1
        w_ref = dict(zip(SMALL, refs[pos:pos + n_small]))
        m_ref = dict(zip(SMALL, refs[pos + n_small:pos + 2 * n_small]))
        v_ref = dict(zip(SMALL, refs[pos + 2 * n_small:pos + 3 * n_small]))
        pos += 3 * n_small
        outs = {name: refs[pos + 4 * i:pos + 4 * i + 4] for i, name in enumerate(SMALL)}
        loss_out = refs[pos + 4 * n_small]
        pack = refs[pos + 4 * n_small + 1]

        total = gath[0]
        for k in range(1, N_DEV):
            total = total + gath[k]
        pack[...] = total

        def update(name, g):
            g_out, d_out, m_out, v_out = outs[name]
            g_out[...] = g
            d_out[...], m_out[...], v_out[...] = _adamw(w_ref[name][...], g, m_ref[name][...], v_ref[name][...])

        for i, name in enumerate(GAINS):
            update(name, pack[i:i + 1, :])
        update("b_in", jnp.concatenate([pack[row_bin + r:row_bin + r + 1, 0:width] for r, width in bin_parts], axis=1))
        update("sinks", pack[row_sink:row_sink + 1, 0:n_sink])
        update("rel_bias", pack[row_rb:row_rb + rb_shape[0], 0:rb_shape[1]])
        loss_out[...] = pack[row_loss:row_loss + 1, 0:LANES]

    args = [gathered]
    for group in (ws, ms, vs):
        args += [group[k] for k in SMALL]
    out_shape = []
    for name in SMALL:
        out_shape += [jax.ShapeDtypeStruct(ws[name].shape, F32)] * 4
    out_shape.append(jax.ShapeDtypeStruct((1, LANES), F32))
    res = pl.pallas_call(
        body, name="small_adamw",
        in_specs=[pl.BlockSpec(memory_space=pltpu.VMEM)] * len(args),
        out_specs=[pl.BlockSpec(memory_space=pltpu.VMEM)] * len(out_shape),
        out_shape=out_shape,
        scratch_shapes=[pltpu.VMEM((n_rows, D), F32)],
    )(*args)
    per_name = {name: res[4 * i:4 * i + 4] for i, name in enumerate(SMALL)}
    return per_name, res[-1]


COLUMN_SHARDED = ("ffn1_w_gu", "ffn2_w_gu", "w_in")


def _adamw_rows(rows_total):
    return max(r for r in range(16, min(rows_total, 256) + 1, 16) if rows_total % r == 0)


def kernel(x, p, rel_bias, ffn1_pre_g, ffn1_w_gu, ffn1_w_down, ffn1_post_g, attn_pre_g, w_in, b_in, sinks, w_out, b_out, attn_post_g, ffn2_pre_g, ffn2_w_gu, ffn2_w_down, ffn2_post_g, ple_pre_g, w_ple_gate, w_ple_proj, ple_post_g, loss_target, m_rel_bias, m_ffn1_pre_g, m_ffn1_w_gu, m_ffn1_w_down, m_ffn1_post_g, m_attn_pre_g, m_w_in, m_b_in, m_sinks, m_w_out, m_b_out, m_attn_post_g, m_ffn2_pre_g, m_ffn2_w_gu, m_ffn2_w_down, m_ffn2_post_g, m_ple_pre_g, m_w_ple_gate, m_w_ple_proj, m_ple_post_g, v_rel_bias, v_ffn1_pre_g, v_ffn1_w_gu, v_ffn1_w_down, v_ffn1_post_g, v_attn_pre_g, v_w_in, v_b_in, v_sinks, v_w_out, v_b_out, v_attn_post_g, v_ffn2_pre_g, v_ffn2_w_gu, v_ffn2_w_down, v_ffn2_post_g, v_ple_pre_g, v_w_ple_gate, v_w_ple_proj, v_ple_post_g):
    given = dict(locals())
    ws = {k: given[k] for k in WEIGHTS}
    ms = {k: given["m_" + k] for k in WEIGHTS}
    vs = {k: given["v_" + k] for k in WEIGHTS}

    def shard(t):
        return t.reshape(t.shape[1:])

    xs, ps, target = shard(x), shard(shard(p)), shard(loss_target)
    T, D = xs.shape
    small = {k: ws[k] for k in SMALL}

    def local(group, k):
        t = shard(group[k])
        return jnp.swapaxes(t, 0, 1) if k in COLUMN_SHARDED else t

    shards = {k: local(ws, k) for k in BIG}

    cast = dict(zip(BIG, _cast_bf16([shards[k] for k in BIG])))
    buckets_a = _bucket_tiles(PATTERNS_A)
    buckets_b = _bucket_tiles(PATTERNS_B)
    bias_a, _ = _bias_build(small["rel_bias"], buckets_a, 0, "bias_build_a")
    bias_b, (w_gu1, w_down1) = _bias_build(
        small["rel_bias"], buckets_b, N_HEAD_GROUP, "bias_build_b",
        side=("relay_gather", [cast["ffn1_w_gu"], cast["ffn1_w_down"]]))
    w_down1 = w_down1.reshape(-1, D)
    a_cfg = dict(patterns=PATTERNS_A, qcol=Q_A_COL, kcol=K_A_COL, vcol=V_A_COL, shared_kv=True)
    b_cfg = dict(patterns=PATTERNS_B, qcol=Q_B_COL, kcol=K_B_COL, vcol=V_B_COL, shared_kv=False)

    (h1, f1, a1, gu1), (w_in_g, w_down2) = _ffn_fwd(
        xs, small["ffn1_pre_g"], small["ffn1_post_g"], w_gu1, w_down1, "ffn1_fwd",
        side=("relay_gather", [cast["w_in"], cast["ffn2_w_down"]]))
    w_in_full = w_in_g.reshape(D_IN, D)
    w_down2 = w_down2.reshape(-1, D)
    (z, a2), (w_out_g,) = _inproj_fwd(h1, small["attn_pre_g"], w_in_full, small["b_in"],
                                      side=("relay_gather", [cast["w_out"]]))
    w_out_full = w_out_g.reshape(-1, D)
    (mix_a, lse_a), (w_gate, w_proj) = _attn_fwd(
        z, bias_a, small["sinks"], name="attn_a_fwd", **a_cfg,
        side=("relay_gather", [cast["w_ple_gate"], cast["w_ple_proj"]]))
    w_gate = w_gate.reshape(-1, D)
    (mix_b, lse_b), (w_gu2,) = _attn_fwd(
        z, bias_b, None, name="attn_b_fwd", **b_cfg, side=("relay_gather", [cast["ffn2_w_gu"]]))
    att, h2, mix = _outproj_fwd(mix_a, mix_b, w_out_full, small["b_out"], small["attn_post_g"], h1)
    (h3, f2, a3, gu2), _ = _ffn_fwd(h2, small["ffn2_pre_g"], small["ffn2_post_g"], w_gu2, w_down2, "ffn2_fwd")
    a4, dpre, de, dh4, loss, dg_ple_post = _ple_fwd_loss(
        h3, small["ple_pre_g"], w_gate, ps, w_proj, small["ple_post_g"], target)

    dh3, dg_ple_pre = _ple_bwd(dpre, w_gate, h3, small["ple_pre_g"], dh4)
    d_gate = _dw_rows(a4, dpre, "ple_dw_gate", min(256, D))
    d_proj = _ple_dw_proj(ps, de, N_DEV)
    landed = {}
    (dh2, df2, hh2, dgu2, dg_f2_post, dg_f2_pre), (landed["w_ple_gate"], landed["w_ple_proj"]) = _ffn_bwd(
        dh3, f2, small["ffn2_post_g"], h2, small["ffn2_pre_g"], gu2, w_gu2, w_down2, "ffn2_bwd",
        side=("exchange", [d_gate, d_proj]))
    d_gu2 = _dw_gu(a3, dgu2, "ffn2_dw_gu")
    d_down2 = _dw_down(hh2, df2, "ffn2_dw_down").reshape(N_DEV, -1, D)
    dmix_a, dmix_b, datt, dg_attn_post, db_out = _outproj_bwd(dh2, att, small["attn_post_g"], w_out_full)
    d_out = _dw_rows(mix, datt, "attn_dw_out", 256)
    (dqa, dka, dva, ds_a, dsinks), (landed["ffn2_w_down"],) = _attn_bwd(
        z, bias_a, small["sinks"], dmix_a, mix_a, lse_a, name="attn_a_bwd", **a_cfg,
        side=("exchange", [d_down2]))
    (dqb, dkb, dvb, ds_b), (landed["ffn2_w_gu"],) = _attn_bwd(
        z, bias_b, None, dmix_b, mix_b, lse_b, name="attn_b_bwd", **b_cfg, side=("exchange", [d_gu2]))
    (dh1, dz, db_in, dg_attn_pre), (landed["w_out"],) = _inproj_bwd(
        dqa, dka, dva, dqb, dkb, dvb, w_in_full, h1, small["attn_pre_g"], dh2, side=("exchange", [d_out]))
    cols = D_IN // 3
    d_in = _tn_matmul(
        dz, a2, pl.BlockSpec((DW_TILE, cols), lambda b, t: (t, b)), _tok(D),
        jax.ShapeDtypeStruct((D_IN, D), BF16), pl.BlockSpec((cols, D), lambda b, t: (b, 0)),
        3, T // DW_TILE, (cols, D), "attn_dw_in").reshape(N_DEV, D_IN // N_DEV, D)
    (grad_x, df1, hh1, dgu1, dg_f1_post, dg_f1_pre), (landed["w_in"],) = _ffn_bwd(
        dh1, f1, small["ffn1_post_g"], xs, small["ffn1_pre_g"], gu1, w_gu1, w_down1, "ffn1_bwd",
        side=("exchange", [d_in]))
    d_down1 = _dw_down(hh1, df1, "ffn1_dw_down").reshape(N_DEV, -1, D)
    d_gu1, (landed["ffn1_w_down"],) = _dw_gu(a1, dgu1, "ffn1_dw_gu", side=("exchange", [d_down1]))

    rb_a = _bias_grad(ds_a, buckets_a, "bias_grad_a")
    rb_b = _bias_grad(ds_b, buckets_b, "bias_grad_b").reshape(len(PATTERNS_B), N_HEAD_GROUP, NUM_BUCKETS)
    d_rel_bias = jnp.concatenate([rb_a.T, jnp.sum(rb_b, axis=0).T], axis=1)
    small_grads = {"ffn1_pre_g": dg_f1_pre, "ffn1_post_g": dg_f1_post, "attn_pre_g": dg_attn_pre,
                   "attn_post_g": dg_attn_post, "ffn2_pre_g": dg_f2_pre, "ffn2_post_g": dg_f2_post,
                   "ple_pre_g": dg_ple_pre, "ple_post_g": dg_ple_post, "b_out": db_out, "b_in": db_in,
                   "sinks": dsinks, "rel_bias": d_rel_bias}
    d_gu1_pairs = _pair_add(d_gu1, _pair_swap_call(d_gu1), "ffn1_dw_gu_pair_add")
    landed["ffn1_w_gu"], small_gathered = _final_exchange(d_gu1_pairs, small_grads, loss)

    result = {}
    for k in BIG:
        outs = _sum_adamw(landed[k], shards[k], local(ms, k), local(vs, k), _adamw_rows(shards[k].shape[0]),
                          k + "_adamw")
        if k in COLUMN_SHARDED:
            outs = [jnp.swapaxes(o, 0, 1) for o in outs]
        result[k] = [o.reshape(ws[k].shape) for o in outs]
    small_res, loss_all = _small_adamw(
        small_gathered, small, {k: ms[k] for k in SMALL}, {k: vs[k] for k in SMALL})
    result.update(small_res)

    out = [loss_all[0, 0], grad_x.reshape(x.shape)]
    for i in range(4):
        out += [result[k][i] for k in WEIGHTS]
    return tuple(out)
```

```python
import functools
import math

import numpy as np
import jax
import jax.numpy as jnp
from jax import lax
from jax.experimental import pallas as pl
from jax.experimental.pallas import tpu as pltpu

F32 = jnp.float32
BF16 = jnp.bfloat16
MESH = pl.DeviceIdType.MESH

N_DEV = 8
EPS = 1e-6
NEG_INF = -1e30
HEAD_DIM = 64
LANES = 128
QBLK = 128
D_IN = 2304
A_Q, A_KV, B_W = 512, 128, 512
N_HEAD_GROUP = 8
NUM_BUCKETS = 32
MAX_DISTANCE = 2048
PATTERNS_A = ((1, 127),)
PATTERNS_B = ((1, 128), (4, 128), (16, 128))
Q_A_COL, K_A_COL, V_A_COL = 0, 4, 5
Q_B_COL, K_B_COL, V_B_COL = 6, 10, 14

ADAM_LR, ADAM_B1, ADAM_B2, ADAM_EPS, ADAM_WD, ADAM_STEP = 0.001, 0.9, 0.999, 1e-08, 0.01, 10

TOKEN_TILE = 512
DW_TILE = 1024
FWD_BLOCKS = 4
BWD_BLOCKS = 4
VMEM_LIMIT = 56 * 1024 * 1024
ARB = "arbitrary"

BIG = ("ffn1_w_gu", "ffn1_w_down", "w_in", "w_out", "ffn2_w_gu", "ffn2_w_down", "w_ple_gate", "w_ple_proj")
GAINS = ("ffn1_pre_g", "ffn1_post_g", "attn_pre_g", "attn_post_g", "ffn2_pre_g", "ffn2_post_g",
         "ple_pre_g", "ple_post_g", "b_out")
SMALL = GAINS + ("b_in", "sinks", "rel_bias")
WEIGHTS = ("rel_bias", "ffn1_pre_g", "ffn1_w_gu", "ffn1_w_down", "ffn1_post_g", "attn_pre_g", "w_in", "b_in",
           "sinks", "w_out", "b_out", "attn_post_g", "ffn2_pre_g", "ffn2_w_gu", "ffn2_w_down", "ffn2_post_g",
           "ple_pre_g", "w_ple_gate", "w_ple_proj", "ple_post_g")


def _params(n_axes):
    return pltpu.CompilerParams(dimension_semantics=(ARB,) * n_axes, vmem_limit_bytes=VMEM_LIMIT)


def _mm(a, b):
    return jnp.dot(a, b, preferred_element_type=F32)


def _mm_nt(a, b):
    return lax.dot_general(a, b, (((1,), (1,)), ((), ())), preferred_element_type=F32)


def _mm_tn(a, b):
    return lax.dot_general(a, b, (((0,), (0,)), ((), ())), preferred_element_type=F32)


def _rstd(x):
    return lax.rsqrt(jnp.mean(x * x, axis=-1, keepdims=True) + EPS)


def _rms_bwd(x, r, gain, dy):
    n = x * r
    gdy = dy * gain
    return r * (gdy - n * jnp.mean(gdy * n, axis=-1, keepdims=True)), dy * n


def _colsum(v):
    return jnp.sum(v, axis=0, keepdims=True)


def _full(shape):
    return pl.BlockSpec(shape, lambda *_: (0,) * len(shape))


def _mesh_place():
    return lax.axis_index("x"), lax.axis_index("y"), lax.axis_index("c")


def _slot(dev):
    return 4 * dev[0] + 2 * dev[1] + dev[2]


def _peers(x, y, c):
    out = []
    for flip in range(1, N_DEV):
        dx, dy, dc = (flip >> 2) & 1, (flip >> 1) & 1, flip & 1
        out.append((1 - x if dx else x, 1 - y if dy else y, 1 - c if dc else c))
    return out


def _side_copies(kind, ins, outs, send_sems, recv_sems, local_sems, sem_row=0):
    n = len(ins)
    x, y, c = _mesh_place()
    me = _slot((x, y, c))
    peers = _peers(x, y, c)

    def src(a, block):
        return ins[a] if kind == "gather" else ins[a].at[block]

    def send(a, k, peer):
        return pltpu.make_async_remote_copy(
            src_ref=src(a, _slot(peer)), dst_ref=outs[a].at[me],
            send_sem=send_sems.at[sem_row + a, k], recv_sem=recv_sems.at[sem_row + a, k],
            device_id=peer, device_id_type=MESH)

    def arrival(a, k, peer):
        return pltpu.make_async_remote_copy(
            src_ref=src(a, _slot(peer)), dst_ref=outs[a].at[_slot(peer)],
            send_sem=send_sems.at[sem_row + a, k], recv_sem=recv_sems.at[sem_row + a, k],
            device_id=peer, device_id_type=MESH)

    def own(a):
        return pltpu.make_async_copy(src(a, me), outs[a].at[me], local_sems.at[sem_row + a, 0])

    def start():
        for k, peer in enumerate(peers):
            for a in range(n):
                send(a, k, peer).start()
        for a in range(n):
            own(a).start()

    def wait():
        for k, peer in enumerate(peers):
            for a in range(n):
                arrival(a, k, peer).wait_recv()
        for k, peer in enumerate(peers):
            for a in range(n):
                send(a, k, peer).wait_send()
        for a in range(n):
            own(a).wait()

    return start, None, wait


N_CHIPS = N_DEV // 2


def _pair_swap(ins, received, send_sems, recv_sems):
    n = len(ins)
    x, y, c = _mesh_place()
    sibling = (x, y, 1 - c)

    def send(a, q):
        return pltpu.make_async_remote_copy(
            src_ref=ins[a].at[2 * q + (1 - c)], dst_ref=received[a].at[q],
            send_sem=send_sems.at[a, q], recv_sem=recv_sems.at[a, q], device_id=sibling, device_id_type=MESH)

    def start():
        for a in range(n):
            for q in range(N_CHIPS):
                send(a, q).start()

    def wait():
        for a in range(n):
            for q in range(N_CHIPS):
                send(a, q).wait_recv()
        for a in range(n):
            for q in range(N_CHIPS):
                send(a, q).wait_send()

    return start, None, wait


def _quad_exchange(ins, outs, send_sems, recv_sems, local_sems, sem_row=0):
    n = len(ins)
    x, y, c = _mesh_place()
    mine = 2 * x + y
    chips = [(1 - x, y), (x, 1 - y), (1 - x, 1 - y)]

    def send(a, k, chip):
        return pltpu.make_async_remote_copy(
            src_ref=ins[a].at[2 * chip[0] + chip[1]], dst_ref=outs[a].at[mine],
            send_sem=send_sems.at[sem_row + a, k], recv_sem=recv_sems.at[sem_row + a, k],
            device_id=(chip[0], chip[1], c), device_id_type=MESH)

    def arrival(a, k, chip):
        return pltpu.make_async_remote_copy(
            src_ref=ins[a].at[2 * chip[0] + chip[1]], dst_ref=outs[a].at[2 * chip[0] + chip[1]],
            send_sem=send_sems.at[sem_row + a, k], recv_sem=recv_sems.at[sem_row + a, k],
            device_id=(chip[0], chip[1], c), device_id_type=MESH)

    def own(a):
        return pltpu.make_async_copy(ins[a].at[mine], outs[a].at[mine], local_sems.at[sem_row + a, 0])

    def start():
        for k, chip in enumerate(chips):
            for a in range(n):
                send(a, k, chip).start()
        for a in range(n):
            own(a).start()

    def wait():
        for k, chip in enumerate(chips):
            for a in range(n):
                arrival(a, k, chip).wait_recv()
        for k, chip in enumerate(chips):
            for a in range(n):
                send(a, k, chip).wait_send()
        for a in range(n):
            own(a).wait()

    return start, None, wait


def _relay_gather(ins, outs, send_sems, recv_sems, local_sems):
    n = len(ins)
    x, y, c = _mesh_place()
    me, sibling = (x, y, c), (x, y, 1 - c)
    chips = [(1 - x, y), (x, 1 - y), (1 - x, 1 - y)]

    def copy(a, k, block, to, src=None):
        dst = outs[a].at[_slot(block)]
        return pltpu.make_async_remote_copy(
            src_ref=dst if src is None else src, dst_ref=dst,
            send_sem=send_sems.at[a, k], recv_sem=recv_sems.at[a, k], device_id=to, device_id_type=MESH)

    def own(a):
        return pltpu.make_async_copy(ins[a], outs[a].at[_slot(me)], local_sems.at[a, 0])

    def start():
        for j, chip in enumerate(chips):
            for a in range(n):
                copy(a, 1 + j, me, (*chip, c), src=ins[a]).start()
        for a in range(n):
            copy(a, 0, me, sibling, src=ins[a]).start()
            own(a).start()

    def relay():
        for j, chip in enumerate(chips):
            for a in range(n):
                copy(a, 1 + j, (*chip, c), me).wait_recv()
                copy(a, 4 + j, (*chip, c), sibling).start()

    def wait():
        for a in range(n):
            copy(a, 0, sibling, me).wait_recv()
        for j, chip in enumerate(chips):
            for a in range(n):
                copy(a, 4 + j, (*chip, 1 - c), me).wait_recv()
        for j, chip in enumerate(chips):
            for a in range(n):
                copy(a, 1 + j, me, (*chip, c), src=ins[a]).wait_send()
                copy(a, 4 + j, (*chip, c), sibling).wait_send()
        for a in range(n):
            copy(a, 0, me, sibling, src=ins[a]).wait_send()
            own(a).wait()

    return start, relay, wait


def _side_out_shapes(kind, arrays):
    if kind in ("gather", "relay_gather"):
        return [jax.ShapeDtypeStruct((N_DEV,) + a.shape, a.dtype) for a in arrays]
    if kind == "pair_swap":
        return [jax.ShapeDtypeStruct((N_CHIPS,) + a.shape[1:], a.dtype) for a in arrays]
    return [jax.ShapeDtypeStruct(a.shape, a.dtype) for a in arrays]


def _hosted_call(body, name, grid, in_specs, out_specs, out_shape, scratch_shapes, args, side=None):
    if side is None:
        outs = pl.pallas_call(
            body, name=name, grid=grid, in_specs=in_specs, out_specs=out_specs, out_shape=out_shape,
            scratch_shapes=scratch_shapes, compiler_params=_params(len(grid)))(*args)
        return outs, []
    kind, arrays = side
    side_shapes = _side_out_shapes(kind, arrays)
    n_in, n_out, n_scr, n_side = len(in_specs), len(out_specs), len(scratch_shapes), len(arrays)

    def hosted(*refs):
        pos = 0
        groups = []
        for size in (n_in, n_side, n_out, len(side_shapes), n_scr):
            groups.append(refs[pos:pos + size])
            pos += size
        ins, side_in, outs, side_out, scr = groups
        send_sems, recv_sems, local_sems = refs[pos:]
        ids = [pl.program_id(d) for d in range(len(grid))]
        is_first = functools.reduce(jnp.logical_and, [i == 0 for i in ids])
        is_last = functools.reduce(jnp.logical_and, [i == g - 1 for i, g in zip(ids, grid)])
        if kind == "relay_gather":
            start, relay, wait = _relay_gather(side_in, side_out, send_sems, recv_sems, local_sems)
        elif kind == "pair_swap":
            start, relay, wait = _pair_swap(side_in, side_out, send_sems, recv_sems)
        elif kind == "quad_exchange":
            start, relay, wait = _quad_exchange(side_in, side_out, send_sems, recv_sems, local_sems)
        else:
            start, relay, wait = _side_copies(kind, side_in, side_out, send_sems, recv_sems, local_sems)
        pl.when(is_first)(start)
        if relay is not None:
            pl.when(is_last)(relay)
        body(*ins, *outs, *scr)
        pl.when(is_last)(wait)

    any_spec = pl.BlockSpec(memory_space=pl.ANY)
    outs = pl.pallas_call(
        hosted, name=name, grid=grid,
        in_specs=list(in_specs) + [any_spec] * n_side,
        out_specs=list(out_specs) + [any_spec] * len(side_shapes),
        out_shape=list(out_shape) + side_shapes,
        scratch_shapes=list(scratch_shapes) + [pltpu.SemaphoreType.DMA((n_side, 7)), pltpu.SemaphoreType.DMA((n_side, 7)),
                                               pltpu.SemaphoreType.DMA((n_side, N_CHIPS))],
        compiler_params=_params(len(grid)))(*args, *arrays)
    return outs[:n_out], outs[n_out:]


def _lane_chunks(width, chunk=2 * LANES):
    return [slice(n0, min(n0 + chunk, width)) for n0 in range(0, width, chunk)]


def _pipelined(chunks, first, middle, last):
    n = len(chunks)
    a, b, total = {}, {}, None
    for step in range(n + 2):
        if step < n:
            a[step] = first(chunks[step])
        if 0 <= step - 1 < n:
            b[step - 1] = middle(chunks[step - 1], a.pop(step - 1))
        if 0 <= step - 2 < n:
            part = last(chunks[step - 2], b.pop(step - 2))
            total = part if total is None else total + part
    return total


def _ffn_fwd(h, g_pre, g_post, w_gu, w_down, name, side=None):
    T, D = h.shape
    nj = w_gu.shape[0] // 2
    FB = w_gu.shape[1]
    tm = TOKEN_TILE

    def body(h_ref, gpre_ref, gpost_ref, wg_ref, wu_ref, wd_ref, hout_ref, f_ref, a_ref, gu_ref, a_scr, acc):
        j = pl.program_id(1)

        @pl.when(j == 0)
        def _():
            x = h_ref[...]
            a = (x * _rstd(x) * gpre_ref[...]).astype(BF16)
            a_scr[...] = a
            a_ref[...] = a
            acc[...] = jnp.zeros_like(acc)

        a = a_scr[...]
        g = _mm_nt(a, wg_ref[...])
        u = _mm_nt(a, wu_ref[...])
        gu_ref[0] = g.astype(BF16)
        gu_ref[1] = u.astype(BF16)
        hh = (g * jax.nn.sigmoid(g) * u).astype(BF16)
        acc[...] += _mm(hh, wd_ref[...])

        @pl.when(j == nj - 1)
        def _():
            f = acc[...]
            f_ref[...] = f
            hout_ref[...] = h_ref[...] + 0.5 * (f * _rstd(f) * gpost_ref[...])

    return _hosted_call(
        body, name, (T // tm, nj),
        in_specs=[
            pl.BlockSpec((tm, D), lambda i, j: (i, 0)),
            _full((1, D)), _full((1, D)),
            pl.BlockSpec((None, FB, D), lambda i, j: (j, 0, 0)),
            pl.BlockSpec((None, FB, D), lambda i, j: (j + nj, 0, 0)),
            pl.BlockSpec((FB, D), lambda i, j: (j, 0)),
        ],
        out_specs=[
            pl.BlockSpec((tm, D), lambda i, j: (i, 0)),
            pl.BlockSpec((tm, D), lambda i, j: (i, 0)),
            pl.BlockSpec((tm, D), lambda i, j: (i, 0)),
            pl.BlockSpec((None, 2, tm, FB), lambda i, j: (j, 0, i, 0)),
        ],
        out_shape=[
            jax.ShapeDtypeStruct((T, D), F32),
            jax.ShapeDtypeStruct((T, D), F32),
            jax.ShapeDtypeStruct((T, D), BF16),
            jax.ShapeDtypeStruct((nj, 2, T, FB), BF16),
        ],
        scratch_shapes=[pltpu.VMEM((tm, D), BF16), pltpu.VMEM((tm, D), F32)],
        args=(h, g_pre, g_post, w_gu, w_gu, w_down), side=side)


def _ffn_bwd(dh_out, f, g_post, h, g_pre, gu, w_gu, w_down, name, side=None):
    T, D = h.shape
    nj = w_gu.shape[0] // 2
    FB = w_gu.shape[1]
    tm = TOKEN_TILE

    def body(dho_ref, f_ref, gpost_ref, h_ref, gpre_ref, gu_ref, wg_ref, wu_ref, wd_ref,
             dhin_ref, df_ref, hh_ref, dgu_ref, dgpost_ref, dgpre_ref, df_scr, da):
        i, j = pl.program_id(0), pl.program_id(1)

        @pl.when(jnp.logical_and(i == 0, j == 0))
        def _():
            dgpost_ref[...] = jnp.zeros_like(dgpost_ref)
            dgpre_ref[...] = jnp.zeros_like(dgpre_ref)

        @pl.when(j == 0)
        def _():
            fv = f_ref[...]
            df, dgain = _rms_bwd(fv, _rstd(fv), gpost_ref[...], 0.5 * dho_ref[...])
            dgpost_ref[...] += _colsum(dgain)
            dfb = df.astype(BF16)
            df_scr[...] = dfb
            df_ref[...] = dfb
            da[...] = jnp.zeros_like(da)

        dfb = df_scr[...]

        halves = (slice(0, tm // 2), slice(tm // 2, tm))

        def hidden_grad(c):
            return [_mm_nt(dfb[rows], wd_ref[c, :]) for rows in halves]

        def through_swiglu(c, dhh):
            dhh = jnp.concatenate(dhh, axis=0)
            g = gu_ref[0, :, c].astype(F32)
            u = gu_ref[1, :, c].astype(F32)
            sg = jax.nn.sigmoid(g)
            silu = g * sg
            hh_ref[:, c] = (silu * u).astype(BF16)
            dg = (dhh * u * (sg * (1.0 + (g - silu)))).astype(BF16)
            du = (dhh * silu).astype(BF16)
            dgu_ref[0, :, c] = dg
            dgu_ref[1, :, c] = du
            return dg, du

        def input_grad(c, dgu):
            return jnp.concatenate(
                [_mm(dgu[0][rows], wg_ref[c, :]) + _mm(dgu[1][rows], wu_ref[c, :]) for rows in halves], axis=0)

        da[...] += _pipelined(_lane_chunks(FB), hidden_grad, through_swiglu, input_grad)

        @pl.when(j == nj - 1)
        def _():
            x = h_ref[...]
            dx, dgain = _rms_bwd(x, _rstd(x), gpre_ref[...], da[...])
            dgpre_ref[...] += _colsum(dgain)
            dhin_ref[...] = dho_ref[...] + dx

    tile = pl.BlockSpec((tm, D), lambda i, j: (i, 0))
    return _hosted_call(
        body, name, (T // tm, nj),
        in_specs=[
            tile, tile, _full((1, D)), tile, _full((1, D)),
            pl.BlockSpec((None, 2, tm, FB), lambda i, j: (j, 0, i, 0)),
            pl.BlockSpec((None, FB, D), lambda i, j: (j, 0, 0)),
            pl.BlockSpec((None, FB, D), lambda i, j: (j + nj, 0, 0)),
            pl.BlockSpec((FB, D), lambda i, j: (j, 0)),
        ],
        out_specs=[
            tile, tile,
            pl.BlockSpec((None, tm, FB), lambda i, j: (j, i, 0)),
            pl.BlockSpec((None, 2, tm, FB), lambda i, j: (j, 0, i, 0)),
            _full((1, D)), _full((1, D)),
        ],
        out_shape=[
            jax.ShapeDtypeStruct((T, D), F32),
            jax.ShapeDtypeStruct((T, D), BF16),
            jax.ShapeDtypeStruct((nj, T, FB), BF16),
            jax.ShapeDtypeStruct((nj, 2, T, FB), BF16),
            jax.ShapeDtypeStruct((1, D), F32),
            jax.ShapeDtypeStruct((1, D), F32),
        ],
        scratch_shapes=[pltpu.VMEM((tm, D), BF16), pltpu.VMEM((tm, D), F32)],
        args=(dh_out, f, g_post, h, g_pre, gu, w_gu, w_gu, w_down), side=side)


def _tn_matmul(x, y, x_spec, y_spec, out_shape, out_spec, n_blocks, n_steps, acc_shape, name, side=None):
    def body(x_ref, y_ref, o_ref, acc):
        t = pl.program_id(1)

        @pl.when(t == 0)
        def _():
            acc[...] = jnp.zeros_like(acc)

        acc[...] += _mm_tn(x_ref[...].astype(BF16), y_ref[...].astype(BF16))

        @pl.when(t == n_steps - 1)
        def _():
            o_ref[...] = acc[...].astype(o_ref.dtype)

    outs, side_outs = _hosted_call(
        body, name, (n_blocks, n_steps), in_specs=[x_spec, y_spec], out_specs=[out_spec], out_shape=[out_shape],
        scratch_shapes=[pltpu.VMEM(acc_shape, F32)], args=(x, y), side=side)
    return (outs[0], side_outs) if side is not None else outs[0]


def _inproj_fwd(h, g_pre, w_in, b_in, side=None):
    T, D = h.shape
    tm = TOKEN_TILE

    def body(h_ref, g_ref, w_ref, b_ref, z_ref, a_ref):
        x = h_ref[...]
        a = (x * _rstd(x) * g_ref[...]).astype(BF16)
        a_ref[...] = a
        z_ref[...] = _mm_nt(a, w_ref[...]) + b_ref[...]

    return _hosted_call(
        body, "inproj_fwd", (T // tm,),
        in_specs=[pl.BlockSpec((tm, D), lambda i: (i, 0)), _full((1, D)), _full((D_IN, D)), _full((1, D_IN))],
        out_specs=[pl.BlockSpec((tm, D_IN), lambda i: (i, 0)), pl.BlockSpec((tm, D), lambda i: (i, 0))],
        out_shape=[jax.ShapeDtypeStruct((T, D_IN), F32), jax.ShapeDtypeStruct((T, D), BF16)],
        scratch_shapes=[], args=(h, g_pre, w_in, b_in), side=side)


def _inproj_bwd(dqa, dka, dva, dqb, dkb, dvb, w_in, h, g_pre, dres, side=None):
    T, D = h.shape
    tm = TOKEN_TILE

    def body(dqa_ref, dka_ref, dva_ref, dqb_ref, dkb_ref, dvb_ref, w_ref, h_ref, g_ref, dres_ref,
             dh_ref, dz_ref, dbin_ref, dg_ref):
        i = pl.program_id(0)

        @pl.when(i == 0)
        def _():
            dbin_ref[...] = jnp.zeros_like(dbin_ref)
            dg_ref[...] = jnp.zeros_like(dg_ref)

        dz = jnp.concatenate([dqa_ref[...], dka_ref[...], dva_ref[...], dqb_ref[...], dkb_ref[...], dvb_ref[...]],
                             axis=1)
        dbin_ref[...] += _colsum(dz)
        dzb = dz.astype(BF16)
        dz_ref[...] = dzb
        da = _mm(dzb, w_ref[...])
        x = h_ref[...]
        dx, dgain = _rms_bwd(x, _rstd(x), g_ref[...], da)
        dg_ref[...] += _colsum(dgain)
        dh_ref[...] = dres_ref[...] + dx

    def tile(w):
        return pl.BlockSpec((tm, w), lambda i: (i, 0))

    return _hosted_call(
        body, "inproj_bwd", (T // tm,),
        in_specs=[tile(A_Q), tile(A_KV), tile(A_KV), tile(B_W), tile(B_W), tile(B_W),
                  _full((D_IN, D)), tile(D), _full((1, D)), tile(D)],
        out_specs=[tile(D), tile(D_IN), _full((1, D_IN)), _full((1, D))],
        out_shape=[jax.ShapeDtypeStruct((T, D), F32), jax.ShapeDtypeStruct((T, D_IN), BF16),
                   jax.ShapeDtypeStruct((1, D_IN), F32), jax.ShapeDtypeStruct((1, D), F32)],
        scratch_shapes=[], args=(dqa, dka, dva, dqb, dkb, dvb, w_in, h, g_pre, dres), side=side)


def _bucket_tiles(patterns):
    i = np.arange(QBLK)[:, None]
    j = np.arange(2 * QBLK)[None, :]
    dist = QBLK + i - j
    max_exact = NUM_BUCKETS // 2
    tiles = []
    for dilation, max_dist in patterns:
        n = np.maximum(dist * dilation, 0)
        nf = np.maximum(n, 1).astype(np.float32)
        large = max_exact + (np.log(nf / np.float32(max_exact)) / np.float32(math.log(MAX_DISTANCE / max_exact))
                             * np.float32(NUM_BUCKETS - max_exact)).astype(np.int32)
        bucket = np.where(n < max_exact, n, np.minimum(large, NUM_BUCKETS - 1))
        tiles.append(np.where((dist >= 0) & (dist <= max_dist), bucket, -1))
    return jnp.asarray(np.stack(tiles).astype(np.int32))


def _bias_build(rel_bias, buckets, head0, name, side=None):
    n = buckets.shape[0]

    def body(bk_ref, rb_ref, o_ref):
        bk = bk_ref[...]
        base = jnp.where(bk < 0, NEG_INF, 0.0).astype(F32)
        for hd in range(N_HEAD_GROUP):
            o_ref[hd] = lax.fori_loop(
                0, NUM_BUCKETS, lambda b, acc, hd=hd: jnp.where(bk == b, rb_ref[b, head0 + hd], acc), base)

    outs, side_outs = _hosted_call(
        body, name, (n,),
        in_specs=[pl.BlockSpec((None, QBLK, 2 * QBLK), lambda p: (p, 0, 0)), pl.BlockSpec(memory_space=pltpu.SMEM)],
        out_specs=[pl.BlockSpec((None, N_HEAD_GROUP, QBLK, 2 * QBLK), lambda p: (p, 0, 0, 0))],
        out_shape=[jax.ShapeDtypeStruct((n, N_HEAD_GROUP, QBLK, 2 * QBLK), F32)],
        scratch_shapes=[], args=(buckets, rel_bias), side=side)
    return outs[0], side_outs


def _bias_grad(ds, buckets, name):
    n = buckets.shape[0]

    def body(ds_ref, bk_ref, o_ref):
        bk = bk_ref[...]
        row = lax.broadcasted_iota(jnp.int32, (NUM_BUCKETS, 2 * QBLK), 0)
        for hd in range(N_HEAD_GROUP):
            d = ds_ref[hd]
            per_key = jnp.zeros((NUM_BUCKETS, 2 * QBLK), F32)
            for b in range(NUM_BUCKETS):
                per_key = jnp.where(row == b, jnp.sum(jnp.where(bk == b, d, 0.0), axis=0, keepdims=True), per_key)
            o_ref[hd] = jnp.broadcast_to(jnp.sum(per_key, axis=1, keepdims=True), (NUM_BUCKETS, LANES))

    out = pl.pallas_call(
        body, name=name, grid=(n,),
        in_specs=[pl.BlockSpec((None, N_HEAD_GROUP, QBLK, 2 * QBLK), lambda p: (p, 0, 0, 0)),
                  pl.BlockSpec((None, QBLK, 2 * QBLK), lambda p: (p, 0, 0))],
        out_specs=pl.BlockSpec((None, N_HEAD_GROUP, NUM_BUCKETS, LANES), lambda p: (p, 0, 0, 0)),
        out_shape=jax.ShapeDtypeStruct((n, N_HEAD_GROUP, NUM_BUCKETS, LANES), F32),
        compiler_params=_params(1),
    )(ds, buckets)
    return out[:, :, :, 0].reshape(n * N_HEAD_GROUP, NUM_BUCKETS)


def _class_rows(start, dilation):
    if dilation == 1:
        return pl.ds(pl.multiple_of(start, QBLK), QBLK)
    return pl.ds(start, QBLK, stride=dilation)


def _starts_class(u, blocks_per_pass, n_blocks):
    return blocks_per_pass % n_blocks == 0 and u % n_blocks == 0


def _block_starts(idx, n_blocks, dilation):
    cls = idx // n_blocks
    n = idx % n_blocks
    cur = cls + dilation * QBLK * n
    prev = cls + dilation * QBLK * jnp.maximum(n - 1, 0)
    return n, cur, prev


class _HeadPair:
    def __init__(self, g, shared_kv):
        self.lane = lax.broadcasted_iota(jnp.int32, (1, LANES), 1)
        self.lower = self.lane < HEAD_DIM
        self.shared_kv = shared_kv
        self.key_lanes = (self.lane >= HEAD_DIM).astype(jnp.int32) == (g // 2)

    def stack(self, t):
        return jnp.concatenate([jnp.where(self.lower, t, 0.0), jnp.where(self.lower, 0.0, t)], axis=0).astype(BF16)

    def unstack(self, t2):
        return jnp.where(self.lower, t2[:QBLK], t2[QBLK:])

    def keys(self, t):
        if self.shared_kv:
            return jnp.where(self.key_lanes, t, pltpu.roll(t, HEAD_DIM, 1))
        return t

    def key_grads(self, t):
        if self.shared_kv:
            return jnp.where(self.key_lanes, t + pltpu.roll(t, HEAD_DIM, 1), 0.0)
        return t


def _attn_specs(T, qcol, kcol, vcol, shared_kv):
    kv = (lambda c: (lambda g: (0, c))) if shared_kv else (lambda c: (lambda g: (0, c + g)))
    return [pl.BlockSpec((T, LANES), lambda g: (0, qcol + g)),
            pl.BlockSpec((T, LANES), kv(kcol)),
            pl.BlockSpec((T, LANES), kv(vcol))]


def _attn_fwd(z, bias, sinks, patterns, qcol, kcol, vcol, shared_kv, name, side=None):
    T = z.shape[0]
    n_pat = len(patterns)
    has_sink = sinks is not None

    def body(*refs):
        if has_sink:
            sink_ref, refs = refs[0], refs[1:]
        q_ref, k_ref, v_ref, b_ref, o_ref, l_ref = refs[:6]
        po_scr = refs[6:6 + n_pat]
        pl_scr = refs[6 + n_pat:]
        g = pl.program_id(0)
        heads = _HeadPair(g, shared_kv)
        in_prev = lax.broadcasted_iota(jnp.int32, (2 * QBLK, 2 * QBLK), 1) < QBLK

        for pi, (dilation, _) in enumerate(patterns):
            n_blocks = T // (QBLK * dilation)

            def step(it, carry, pi=pi, dilation=dilation, n_blocks=n_blocks):
                blocks = []
                for u in range(FWD_BLOCKS):
                    n, cur, prev = _block_starts(it * FWD_BLOCKS + u, n_blocks, dilation)
                    rows_c, rows_p = _class_rows(cur, dilation), _class_rows(prev, dilation)
                    qm = heads.stack(q_ref[rows_c, :])
                    k_cur, v_cur = k_ref[rows_c, :], v_ref[rows_c, :]
                    no_past = _starts_class(u, FWD_BLOCKS, n_blocks)
                    if no_past:
                        k2, v2 = k_cur, v_cur
                    else:
                        if u % min(FWD_BLOCKS, n_blocks) == 0:
                            k_prev, v_prev = k_ref[rows_p, :], v_ref[rows_p, :]
                        k2 = jnp.concatenate([k_prev, k_cur], axis=0)
                        v2 = jnp.concatenate([v_prev, v_cur], axis=0)
                    k2, v2 = heads.keys(k2).astype(BF16), heads.keys(v2).astype(BF16)
                    k_prev, v_prev = k_cur, v_cur
                    blocks.append(dict(n=n, no_past=no_past, rows=rows_c, v2=v2, s=_mm_nt(qm, k2)))
                for b in blocks:
                    if b["no_past"]:
                        b["s"] = b["s"] * (HEAD_DIM ** -0.5) + b_ref[pi, :, QBLK:]
                    else:
                        s = b["s"] * (HEAD_DIM ** -0.5) + b_ref[pi]
                        b["s"] = jnp.where(jnp.logical_and(in_prev, b["n"] == 0), NEG_INF, s)
                    b["m"] = jnp.max(b["s"], axis=1, keepdims=True)
                for b in blocks:
                    b["pr"] = jnp.exp(b["s"] - b["m"])
                    b["den"] = jnp.sum(b["pr"], axis=1, keepdims=True)
                for b in blocks:
                    b["o2"] = _mm(b["pr"].astype(BF16), b["v2"])
                for b in blocks:
                    lse = b["m"] + jnp.log(b["den"])
                    po_scr[pi][b["rows"], :] = heads.unstack(b["o2"] / b["den"])
                    pl_scr[2 * pi][b["rows"], :] = jnp.broadcast_to(lse[:QBLK], (QBLK, LANES))
                    pl_scr[2 * pi + 1][b["rows"], :] = jnp.broadcast_to(lse[QBLK:], (QBLK, LANES))
                return carry

            lax.fori_loop(0, (dilation * n_blocks) // FWD_BLOCKS, step, 0)

        def merge(ci, carry):
            rows = pl.ds(pl.multiple_of(ci * QBLK, QBLK), QBLK)
            weights = []
            for hd in range(2):
                parts = [pl_scr[2 * pi + hd][rows, :] for pi in range(n_pat)]
                m = functools.reduce(jnp.maximum, parts)
                if has_sink:
                    sink = sink_ref[0, 2 * g + hd]
                    m = jnp.maximum(m, sink)
                terms = [jnp.exp(x - m) for x in parts]
                den = functools.reduce(jnp.add, terms)
                if has_sink:
                    den = den + jnp.exp(sink - m)
                l_ref[hd, rows, :] = m + jnp.log(den)
                inv = 1.0 / den
                weights.append([t * inv for t in terms])
            o_ref[rows, :] = functools.reduce(
                jnp.add, [jnp.where(heads.lower, weights[0][pi], weights[1][pi]) * po_scr[pi][rows, :]
                          for pi in range(n_pat)])
            return carry

        lax.fori_loop(0, T // QBLK, merge, 0)

    in_specs = _attn_specs(T, qcol, kcol, vcol, shared_kv)
    in_specs.append(pl.BlockSpec((n_pat, None, 2 * QBLK, 2 * QBLK), lambda g: (0, g, 0, 0)))
    args = [z, z, z, bias.reshape(n_pat, N_HEAD_GROUP // 2, 2 * QBLK, 2 * QBLK)]
    if has_sink:
        in_specs.insert(0, pl.BlockSpec(memory_space=pltpu.SMEM))
        args.insert(0, sinks)
    return _hosted_call(
        body, name, (N_HEAD_GROUP // 2,),
        in_specs=in_specs,
        out_specs=[pl.BlockSpec((T, LANES), lambda g: (0, g)), pl.BlockSpec((2, T, LANES), lambda g: (g, 0, 0))],
        out_shape=[jax.ShapeDtypeStruct((T, N_HEAD_GROUP * HEAD_DIM), F32),
                   jax.ShapeDtypeStruct((N_HEAD_GROUP, T, LANES), F32)],
        scratch_shapes=[pltpu.VMEM((T, LANES), F32)] * (3 * n_pat), args=args, side=side)


def _attn_bwd(z, bias, sinks, d_out, out, lse, patterns, qcol, kcol, vcol, shared_kv, name, side=None):
    T = z.shape[0]
    n_pat = len(patterns)
    has_sink = sinks is not None
    kv_width = LANES if shared_kv else N_HEAD_GROUP * HEAD_DIM

    def body(*refs):
        if has_sink:
            sink_ref, refs = refs[0], refs[1:]
        q_ref, k_ref, v_ref, b_ref, do_ref, o_ref, l0_ref, l1_ref = refs[:8]
        dq_ref, dk_ref, dv_ref, ds_ref = refs[8:12]
        dsink_ref = refs[12] if has_sink else None
        dk_acc, dv_acc = refs[-2:]
        g = pl.program_id(0)
        heads = _HeadPair(g, shared_kv)
        in_prev = lax.broadcasted_iota(jnp.int32, (2 * QBLK, 2 * QBLK), 1) < QBLK

        dq_ref[...] = jnp.zeros_like(dq_ref)
        ds_ref[...] = jnp.zeros_like(ds_ref)
        dk_acc[...] = jnp.zeros_like(dk_acc)
        dv_acc[...] = jnp.zeros_like(dv_acc)

        dsink = jnp.zeros((1, LANES), F32)
        for pi, (dilation, _) in enumerate(patterns):
            n_blocks = T // (QBLK * dilation)

            def step(idx, dsink, pi=pi, dilation=dilation, n_blocks=n_blocks):
                blocks = []
                for u in range(BWD_BLOCKS):
                    n, cur, prev = _block_starts(idx * BWD_BLOCKS + u, n_blocks, dilation)
                    rows_c, rows_p = _class_rows(cur, dilation), _class_rows(prev, dilation)
                    qm = heads.stack(q_ref[rows_c, :])
                    k_cur, v_cur = k_ref[rows_c, :], v_ref[rows_c, :]
                    first = u % min(BWD_BLOCKS, n_blocks) == 0
                    no_past = _starts_class(u, BWD_BLOCKS, n_blocks)
                    if no_past:
                        k2, v2 = k_cur, v_cur
                    else:
                        if first:
                            k_prev, v_prev = k_ref[rows_p, :], v_ref[rows_p, :]
                        k2 = jnp.concatenate([k_prev, k_cur], axis=0)
                        v2 = jnp.concatenate([v_prev, v_cur], axis=0)
                    k2, v2 = heads.keys(k2).astype(BF16), heads.keys(v2).astype(BF16)
                    k_prev, v_prev = k_cur, v_cur
                    d_o = do_ref[rows_c, :]
                    dom = heads.stack(d_o)
                    dd = d_o * o_ref[rows_c, :]
                    delta = jnp.concatenate([jnp.sum(jnp.where(heads.lower, dd, 0.0), axis=1, keepdims=True),
                                             jnp.sum(jnp.where(heads.lower, 0.0, dd), axis=1, keepdims=True)], axis=0)
                    lse = jnp.concatenate([l0_ref[rows_c, :], l1_ref[rows_c, :]], axis=0)
                    blocks.append(dict(n=n, first=first, no_past=no_past, rows_c=rows_c, rows_p=rows_p, qm=qm, k2=k2,
                                       dom=dom, delta=delta, lse=lse, s=_mm_nt(qm, k2), dp=_mm_nt(dom, v2)))
                for b in blocks:
                    if b["no_past"]:
                        s = b["s"] * (HEAD_DIM ** -0.5) + b_ref[pi, :, QBLK:]
                        b["pr"] = jnp.exp(s - b["lse"])
                    else:
                        s = b["s"] * (HEAD_DIM ** -0.5) + b_ref[pi]
                        s = jnp.where(jnp.logical_and(in_prev, b["n"] == 0), NEG_INF, s)
                        b["pr"] = jnp.exp(s - jnp.concatenate([b["lse"], b["lse"]], axis=1))
                    b["ds"] = b["pr"] * (b["dp"] - b["delta"])
                for b in blocks:
                    dsb = b["ds"].astype(BF16)
                    b["dq2"] = _mm(dsb, b["k2"])
                    b["dk2"] = _mm_tn(dsb, b["qm"])
                    b["dv2"] = _mm_tn(b["pr"].astype(BF16), b["dom"])
                for b in blocks:
                    b["dk2"] = heads.key_grads(b["dk2"]) * (HEAD_DIM ** -0.5)
                    b["dv2"] = heads.key_grads(b["dv2"])
                for u, b in enumerate(blocks):
                    dq_ref[b["rows_c"], :] += heads.unstack(b["dq2"]) * (HEAD_DIM ** -0.5)
                    if b["no_past"]:
                        ds_ref[pi, :, QBLK:] += b["ds"]
                        dk_own, dv_own = b["dk2"], b["dv2"]
                    else:
                        ds_ref[pi] += b["ds"]
                        dk_own, dv_own = b["dk2"][QBLK:], b["dv2"][QBLK:]
                    if u + 1 < len(blocks) and not blocks[u + 1]["first"]:
                        dk_own = dk_own + blocks[u + 1]["dk2"][:QBLK]
                        dv_own = dv_own + blocks[u + 1]["dv2"][:QBLK]
                    if b["first"] and not b["no_past"]:
                        dk_acc[b["rows_p"], :] += b["dk2"][:QBLK]
                        dv_acc[b["rows_p"], :] += b["dv2"][:QBLK]
                    dk_acc[b["rows_c"], :] += dk_own
                    dv_acc[b["rows_c"], :] += dv_own
                    if has_sink:
                        for hd in range(2):
                            rows_h = slice(QBLK * hd, QBLK * (hd + 1))
                            p_sink = jnp.exp(sink_ref[0, 2 * g + hd] - b["lse"][rows_h, 0:1])
                            dsink = dsink - jnp.where(heads.lane == 2 * g + hd,
                                                      jnp.sum(p_sink * b["delta"][rows_h]), 0.0)
                return dsink

            dsink = lax.fori_loop(0, (dilation * n_blocks) // BWD_BLOCKS, step, dsink)

        if shared_kv:
            @pl.when(g == 0)
            def _():
                dk_ref[...] = dk_acc[...]
                dv_ref[...] = dv_acc[...]

            @pl.when(g != 0)
            def _():
                dk_ref[...] += dk_acc[...]
                dv_ref[...] += dv_acc[...]
        else:
            dk_ref[...] = dk_acc[...]
            dv_ref[...] = dv_acc[...]

        if has_sink:
            @pl.when(g == 0)
            def _():
                dsink_ref[...] = dsink

            @pl.when(g != 0)
            def _():
                dsink_ref[...] += dsink

    pair = pl.BlockSpec((T, LANES), lambda g: (0, g))
    stacked = pl.BlockSpec((n_pat, None, 2 * QBLK, 2 * QBLK), lambda g: (0, g, 0, 0))
    stacked_shape = (n_pat, N_HEAD_GROUP // 2, 2 * QBLK, 2 * QBLK)
    in_specs = _attn_specs(T, qcol, kcol, vcol, shared_kv)
    in_specs += [stacked, pair, pair,
                 pl.BlockSpec((None, T, LANES), lambda g: (2 * g, 0, 0)),
                 pl.BlockSpec((None, T, LANES), lambda g: (2 * g + 1, 0, 0))]
    args = [z, z, z, bias.reshape(stacked_shape), d_out, out, lse, lse]
    kv_out = _full((T, LANES)) if shared_kv else pair
    out_specs = [pair, kv_out, kv_out, stacked]
    out_shape = [jax.ShapeDtypeStruct((T, N_HEAD_GROUP * HEAD_DIM), F32),
                 jax.ShapeDtypeStruct((T, kv_width), F32), jax.ShapeDtypeStruct((T, kv_width), F32),
                 jax.ShapeDtypeStruct(stacked_shape, F32)]
    if has_sink:
        in_specs.insert(0, pl.BlockSpec(memory_space=pltpu.SMEM))
        args.insert(0, sinks)
        out_specs.append(_full((1, LANES)))
        out_shape.append(jax.ShapeDtypeStruct((1, LANES), F32))
    outs, side_outs = _hosted_call(
        body, name, (N_HEAD_GROUP // 2,), in_specs=in_specs, out_specs=out_specs, out_shape=out_shape,
        scratch_shapes=[pltpu.VMEM((T, LANES), F32), pltpu.VMEM((T, LANES), F32)], args=args, side=side)
    outs = list(outs)
    outs[3] = outs[3].reshape(n_pat, N_HEAD_GROUP, QBLK, 2 * QBLK)
    return outs, side_outs


def _outproj_fwd(mix_a, mix_b, w_out, b_out, g_post, h):
    T, D = h.shape
    tm = TOKEN_TILE
    d_mix = w_out.shape[0]

    def body(ma_ref, mb_ref, w_ref, b_ref, g_ref, h_ref, att_ref, hout_ref, mix_ref):
        mix = jnp.concatenate([ma_ref[...], mb_ref[...]], axis=1).astype(BF16)
        mix_ref[...] = mix
        att = _mm(mix, w_ref[...]) + b_ref[...]
        att_ref[...] = att
        hout_ref[...] = h_ref[...] + att * _rstd(att) * g_ref[...]

    def tile(w):
        return pl.BlockSpec((tm, w), lambda i: (i, 0))

    return pl.pallas_call(
        body, name="outproj_fwd", grid=(T // tm,),
        in_specs=[tile(A_Q), tile(B_W), _full((d_mix, D)), _full((1, D)), _full((1, D)), tile(D)],
        out_specs=[tile(D), tile(D), tile(d_mix)],
        out_shape=[jax.ShapeDtypeStruct((T, D), F32), jax.ShapeDtypeStruct((T, D), F32),
                   jax.ShapeDtypeStruct((T, d_mix), BF16)],
        compiler_params=_params(1),
    )(mix_a, mix_b, w_out, b_out, g_post, h)


def _outproj_bwd(dh, att, g_post, w_out):
    T, D = dh.shape
    tm = TOKEN_TILE
    d_mix = w_out.shape[0]

    def body(dh_ref, att_ref, g_ref, w_ref, dma_ref, dmb_ref, datt_ref, dg_ref, db_ref):
        i = pl.program_id(0)

        @pl.when(i == 0)
        def _():
            dg_ref[...] = jnp.zeros_like(dg_ref)
            db_ref[...] = jnp.zeros_like(db_ref)

        att = att_ref[...]
        datt, dgain = _rms_bwd(att, _rstd(att), g_ref[...], dh_ref[...])
        dg_ref[...] += _colsum(dgain)
        db_ref[...] += _colsum(datt)
        dattb = datt.astype(BF16)
        datt_ref[...] = dattb
        dmix = _mm_nt(dattb, w_ref[...])
        dma_ref[...] = dmix[:, :A_Q]
        dmb_ref[...] = dmix[:, A_Q:]

    def tile(w):
        return pl.BlockSpec((tm, w), lambda i: (i, 0))

    return pl.pallas_call(
        body, name="outproj_bwd", grid=(T // tm,),
        in_specs=[tile(D), tile(D), _full((1, D)), _full((d_mix, D))],
        out_specs=[tile(A_Q), tile(B_W), tile(D), _full((1, D)), _full((1, D))],
        out_shape=[jax.ShapeDtypeStruct((T, A_Q), F32), jax.ShapeDtypeStruct((T, B_W), F32),
                   jax.ShapeDtypeStruct((T, D), BF16), jax.ShapeDtypeStruct((1, D), F32),
                   jax.ShapeDtypeStruct((1, D), F32)],
        compiler_params=_params(1),
    )(dh, att, g_post, w_out)


def _ple_fwd_loss(h, g_pre, w_gate, p, w_proj, g_post, target):
    T, D = h.shape
    tm = TOKEN_TILE
    n_proj, ple, db = w_proj.shape

    def body(h_ref, gpre_ref, wg_ref, p_ref, wp_ref, gpost_ref, t_ref,
             a_ref, dpre_ref, de_ref, dh_ref, loss_ref, dgpost_ref):
        i = pl.program_id(0)

        @pl.when(i == 0)
        def _():
            loss_ref[...] = jnp.zeros_like(loss_ref)
            dgpost_ref[...] = jnp.zeros_like(dgpost_ref)

        x = h_ref[...]
        a = (x * _rstd(x) * gpre_ref[...]).astype(BF16)
        a_ref[...] = a
        gate = jax.nn.sigmoid(_mm(a, wg_ref[...]))
        pb = p_ref[...].astype(BF16)
        e = jnp.concatenate([_mm(pb, wp_ref[k]) for k in range(n_proj)], axis=1)
        ge = gate * e
        rg = _rstd(ge)
        diff = x + ge * rg * gpost_ref[...] - t_ref[...]
        loss_ref[...] += 0.5 * jnp.sum(jnp.mean(diff * diff, axis=1, keepdims=True))
        dy = diff * (1.0 / D)
        dh_ref[...] = dy
        dge, dgain = _rms_bwd(ge, rg, gpost_ref[...], dy)
        dgpost_ref[...] += _colsum(dgain)
        de_ref[...] = (dge * gate).astype(BF16)
        dpre_ref[...] = (dge * e * gate * (1.0 - gate)).astype(BF16)

    def tile(w):
        return pl.BlockSpec((tm, w), lambda i: (i, 0))

    return pl.pallas_call(
        body, name="ple_fwd_loss", grid=(T // tm,),
        in_specs=[tile(D), _full((1, D)), _full((D, D)), tile(ple), _full((n_proj, ple, db)), _full((1, D)), tile(D)],
        out_specs=[tile(D), tile(D), tile(D), tile(D), _full((1, LANES)), _full((1, D))],
        out_shape=[jax.ShapeDtypeStruct((T, D), BF16),
                   jax.ShapeDtypeStruct((T, D), BF16),
                   jax.ShapeDtypeStruct((T, D), BF16),
                   jax.ShapeDtypeStruct((T, D), F32),
                   jax.ShapeDtypeStruct((1, LANES), F32),
                   jax.ShapeDtypeStruct((1, D), F32)],
        compiler_params=_params(1),
    )(h, g_pre, w_gate, p, w_proj, g_post, target)


def _ple_bwd(dpre, w_gate, h, g_pre, dres):
    T, D = h.shape
    tm = TOKEN_TILE

    def body(dpre_ref, w_ref, h_ref, g_ref, dres_ref, dh_ref, dg_ref):
        i = pl.program_id(0)

        @pl.when(i == 0)
        def _():
            dg_ref[...] = jnp.zeros_like(dg_ref)

        da = _mm_nt(dpre_ref[...], w_ref[...])
        x = h_ref[...]
        dx, dgain = _rms_bwd(x, _rstd(x), g_ref[...], da)
        dg_ref[...] += _colsum(dgain)
        dh_ref[...] = dres_ref[...] + dx

    tile = pl.BlockSpec((tm, D), lambda i: (i, 0))
    return pl.pallas_call(
        body, name="ple_bwd", grid=(T // tm,),
        in_specs=[tile, _full((D, D)), tile, _full((1, D)), tile],
        out_specs=[tile, _full((1, D))],
        out_shape=[jax.ShapeDtypeStruct((T, D), F32), jax.ShapeDtypeStruct((1, D), F32)],
        compiler_params=_params(1),
    )(dpre, w_gate, h, g_pre, dres)


def _ple_dw_proj(p, de, n_proj):
    T, ple = p.shape
    D = de.shape[1]
    db = D // n_proj
    tk = TOKEN_TILE
    nt = T // tk

    def body(p_ref, de_ref, o_ref, acc):
        t = pl.program_id(0)

        @pl.when(t == 0)
        def _():
            acc[...] = jnp.zeros_like(acc)

        acc[...] += _mm_tn(p_ref[...].astype(BF16), de_ref[...])

        @pl.when(t == nt - 1)
        def _():
            for k in range(n_proj):
                o_ref[k] = acc[:, k * db:(k + 1) * db].astype(BF16)

    return pl.pallas_call(
        body, name="ple_dw_proj", grid=(nt,),
        in_specs=[pl.BlockSpec((tk, ple), lambda t: (t, 0)), pl.BlockSpec((tk, D), lambda t: (t, 0))],
        out_specs=_full((n_proj, ple, db)), out_shape=jax.ShapeDtypeStruct((n_proj, ple, db), BF16),
        scratch_shapes=[pltpu.VMEM((ple, D), F32)], compiler_params=_params(1),
    )(p, de)


def _tok(width):
    return pl.BlockSpec((DW_TILE, width), lambda b, t: (t, 0))


def _dw_gu(a, dgu, name, side=None):
    T, D = a.shape
    nj, _, _, FB = dgu.shape
    return _tn_matmul(
        dgu, a, pl.BlockSpec((None, None, DW_TILE, FB), lambda b, t: (b % nj, b // nj, t, 0)), _tok(D),
        jax.ShapeDtypeStruct((2 * nj, FB, D), BF16), pl.BlockSpec((None, FB, D), lambda b, t: (b, 0, 0)),
        2 * nj, T // DW_TILE, (FB, D), name, side=side)


def _dw_down(hh, df, name, side=None):
    nj, T, FB = hh.shape
    D = df.shape[1]
    return _tn_matmul(
        hh, df, pl.BlockSpec((None, DW_TILE, FB), lambda b, t: (b, t, 0)), _tok(D),
        jax.ShapeDtypeStruct((nj, FB, D), BF16), pl.BlockSpec((None, FB, D), lambda b, t: (b, 0, 0)),
        nj, T // DW_TILE, (FB, D), name, side=side)


def _dw_rows(xm, y, name, rows):
    T, k = xm.shape
    D = y.shape[1]
    out = _tn_matmul(
        xm, y, pl.BlockSpec((DW_TILE, rows), lambda b, t: (t, b)), _tok(D),
        jax.ShapeDtypeStruct((k, D), BF16), pl.BlockSpec((rows, D), lambda b, t: (b, 0)),
        k // rows, T // DW_TILE, (rows, D), name)
    return out.reshape(N_DEV, k // N_DEV, D)


def _cast_bf16(arrays):
    n = len(arrays)

    def body(*refs):
        for a in range(n):
            refs[n + a][...] = refs[a][...].astype(BF16)

    return pl.pallas_call(
        body, name="cast_shards",
        in_specs=[pl.BlockSpec(memory_space=pltpu.VMEM)] * n, out_specs=[pl.BlockSpec(memory_space=pltpu.VMEM)] * n,
        out_shape=[jax.ShapeDtypeStruct(a.shape, BF16) for a in arrays],
        compiler_params=pltpu.CompilerParams(vmem_limit_bytes=VMEM_LIMIT),
    )(*arrays)


def _pack_layout(D, n_rel_rows):
    n_bin = -(-D_IN // D)
    row_bin = len(GAINS)
    row_sink = row_bin + n_bin
    row_loss = row_sink + 1
    row_rb = -(-(row_loss + 1) // 8) * 8
    n_rows = row_rb + -(-n_rel_rows // 8) * 8
    bin_parts = [(r, min(D, D_IN - r * D)) for r in range(n_bin)]
    return row_bin, row_sink, row_loss, row_rb, n_rows, bin_parts


def _pair_swap_call(grad_blocks):
    def body(g_in, received, send_sems, recv_sems):
        start, _, wait = _pair_swap([g_in], [received], send_sems, recv_sems)
        start()
        wait()

    any_spec = pl.BlockSpec(memory_space=pl.ANY)
    return pl.pallas_call(
        body, name="pair_swap", in_specs=[any_spec], out_specs=any_spec,
        out_shape=_side_out_shapes("pair_swap", [grad_blocks])[0],
        scratch_shapes=[pltpu.SemaphoreType.DMA((1, N_CHIPS)), pltpu.SemaphoreType.DMA((1, N_CHIPS))],
    )(grad_blocks)


def _pair_add(blocks, received, name):
    n, R, C = received.shape
    rows = _adamw_rows(R)
    core = lax.axis_index("c").astype(jnp.int32).reshape(1)

    def body(core_ref, a_ref, b_ref, o_ref):
        o_ref[...] = (a_ref[...].astype(F32) + b_ref[...].astype(F32)).astype(o_ref.dtype)

    tile = pl.BlockSpec((None, rows, C), lambda q, r, core_ref: (q, r, 0))
    return pl.pallas_call(
        body, name=name,
        grid_spec=pltpu.PrefetchScalarGridSpec(
            num_scalar_prefetch=1, grid=(n, R // rows),
            in_specs=[pl.BlockSpec((None, rows, C), lambda q, r, core_ref: (2 * q + core_ref[0], r, 0)), tile],
            out_specs=tile),
        out_shape=jax.ShapeDtypeStruct(received.shape, received.dtype), compiler_params=_params(2),
    )(core, blocks, received)


def _final_exchange(grad_blocks, partials, loss):
    D = partials["ffn1_pre_g"].shape[1]
    rb_shape = partials["rel_bias"].shape
    row_bin, row_sink, row_loss, row_rb, n_rows, bin_parts = _pack_layout(D, rb_shape[0])
    n_small = len(SMALL)

    def body(*refs):
        g_in = refs[0]
        part = dict(zip(SMALL, refs[1:1 + n_small]))
        loss_ref = refs[1 + n_small]
        landed, gath, pack, send_sems, recv_sems, local_sems = refs[2 + n_small:]

        pack[...] = jnp.zeros_like(pack)
        for i, name in enumerate(GAINS):
            pack[i:i + 1, :] = part[name][...]
        for r, width in bin_parts:
            pack[row_bin + r:row_bin + r + 1, 0:width] = part["b_in"][:, r * D:r * D + width]
        pack[row_sink:row_sink + 1, 0:LANES] = part["sinks"][...]
        pack[row_loss:row_loss + 1, 0:LANES] = loss_ref[...]
        pack[row_rb:row_rb + rb_shape[0], 0:rb_shape[1]] = part["rel_bias"][...]

        small_start, _, small_wait = _side_copies("gather", [pack], [gath], send_sems, recv_sems, local_sems, sem_row=0)
        big_start, _, big_wait = _quad_exchange([g_in], [landed], send_sems, recv_sems, local_sems, sem_row=1)
        small_start()
        big_start()
        small_wait()
        big_wait()

    args = [grad_blocks] + [partials[k] for k in SMALL] + [loss]
    vmem = pl.BlockSpec(memory_space=pltpu.VMEM)
    any_spec = pl.BlockSpec(memory_space=pl.ANY)
    return pl.pallas_call(
        body, name="final_exchange",
        in_specs=[any_spec] + [vmem] * (n_small + 1),
        out_specs=[any_spec, any_spec],
        out_shape=[jax.ShapeDtypeStruct(grad_blocks.shape, grad_blocks.dtype),
                   jax.ShapeDtypeStruct((N_DEV, n_rows, D), F32)],
        scratch_shapes=[pltpu.VMEM((n_rows, D), F32), pltpu.SemaphoreType.DMA((2, 7)),
                        pltpu.SemaphoreType.DMA((2, 7)), pltpu.SemaphoreType.DMA((2, N_CHIPS))],
    )(*args)


def _adamw(w, g, m, v):
    m = ADAM_B1 * m + (1.0 - ADAM_B1) * g
    v = ADAM_B2 * v + (1.0 - ADAM_B2) * (g * g)
    m_hat = m / (1.0 - ADAM_B1 ** ADAM_STEP)
    v_hat = v / (1.0 - ADAM_B2 ** ADAM_STEP)
    return -ADAM_LR * (m_hat / (jnp.sqrt(v_hat) + ADAM_EPS) + ADAM_WD * w), m, v


def _sum_adamw(partials, w, m, v, rows, name):
    R, C = w.shape
    n = partials.shape[0]

    def body(p_ref, w_ref, m_ref, v_ref, g_ref, d_ref, nm_ref, nv_ref):
        g = p_ref[0].astype(F32)
        for k in range(1, n):
            g = g + p_ref[k].astype(F32)
        g_ref[...] = g
        d_ref[...], nm_ref[...], nv_ref[...] = _adamw(w_ref[...], g, m_ref[...], v_ref[...])

    tile = pl.BlockSpec((rows, C), lambda i: (i, 0))
    return pl.pallas_call(
        body, name=name, grid=(R // rows,),
        in_specs=[pl.BlockSpec((n, rows, C), lambda i: (0, i, 0)), tile, tile, tile],
        out_specs=[tile] * 4, out_shape=[jax.ShapeDtypeStruct((R, C), F32)] * 4,
        compiler_params=_params(1),
    )(partials, w, m, v)


def _small_adamw(gathered, ws, ms, vs):
    D = ws["ffn1_pre_g"].shape[1]
    n_sink = ws["sinks"].shape[1]
    rb_shape = ws["rel_bias"].shape
    row_bin, row_sink, row_loss, row_rb, n_rows, bin_parts = _pack_layout(D, rb_shape[0])
    n_small = len(SMALL)

    def body(*refs):
        gath = refs[0]
        pos = 1
        w_ref = dict(zip(SMALL, refs[pos:pos + n_small]))
        m_ref = dict(zip(SMALL, refs[pos + n_small:pos + 2 * n_small]))
        v_ref = dict(zip(SMALL, refs[pos + 2 * n_small:pos + 3 * n_small]))
        pos += 3 * n_small
        outs = {name: refs[pos + 4 * i:pos + 4 * i + 4] for i, name in enumerate(SMALL)}
        loss_out = refs[pos + 4 * n_small]
        pack = refs[pos + 4 * n_small + 1]

        total = gath[0]
        for k in range(1, N_DEV):
            total = total + gath[k]
        pack[...] = total

        def update(name, g):
            g_out, d_out, m_out, v_out = outs[name]
            g_out[...] = g
            d_out[...], m_out[...], v_out[...] = _adamw(w_ref[name][...], g, m_ref[name][...], v_ref[name][...])

        for i, name in enumerate(GAINS):
            update(name, pack[i:i + 1, :])
        update("b_in", jnp.concatenate([pack[row_bin + r:row_bin + r + 1, 0:width] for r, width in bin_parts], axis=1))
        update("sinks", pack[row_sink:row_sink + 1, 0:n_sink])
        update("rel_bias", pack[row_rb:row_rb + rb_shape[0], 0:rb_shape[1]])
        loss_out[...] = pack[row_loss:row_loss + 1, 0:LANES]

    args = [gathered]
    for group in (ws, ms, vs):
        args += [group[k] for k in SMALL]
    out_shape = []
    for name in SMALL:
        out_shape += [jax.ShapeDtypeStruct(ws[name].shape, F32)] * 4
    out_shape.append(jax.ShapeDtypeStruct((1, LANES), F32))
    res = pl.pallas_call(
        body, name="small_adamw",
        in_specs=[pl.BlockSpec(memory_space=pltpu.VMEM)] * len(args),
        out_specs=[pl.BlockSpec(memory_space=pltpu.VMEM)] * len(out_shape),
        out_shape=out_shape,
        scratch_shapes=[pltpu.VMEM((n_rows, D), F32)],
    )(*args)
    per_name = {name: res[4 * i:4 * i + 4] for i, name in enumerate(SMALL)}
    return per_name, res[-1]


COLUMN_SHARDED = ("ffn1_w_gu", "ffn2_w_gu", "w_in")


def _adamw_rows(rows_total):
    return max(r for r in range(16, min(rows_total, 256) + 1, 16) if rows_total % r == 0)


def kernel(x, p, rel_bias, ffn1_pre_g, ffn1_w_gu, ffn1_w_down, ffn1_post_g, attn_pre_g, w_in, b_in, sinks, w_out, b_out, attn_post_g, ffn2_pre_g, ffn2_w_gu, ffn2_w_down, ffn2_post_g, ple_pre_g, w_ple_gate, w_ple_proj, ple_post_g, loss_target, m_rel_bias, m_ffn1_pre_g, m_ffn1_w_gu, m_ffn1_w_down, m_ffn1_post_g, m_attn_pre_g, m_w_in, m_b_in, m_sinks, m_w_out, m_b_out, m_attn_post_g, m_ffn2_pre_g, m_ffn2_w_gu, m_ffn2_w_down, m_ffn2_post_g, m_ple_pre_g, m_w_ple_gate, m_w_ple_proj, m_ple_post_g, v_rel_bias, v_ffn1_pre_g, v_ffn1_w_gu, v_ffn1_w_down, v_ffn1_post_g, v_attn_pre_g, v_w_in, v_b_in, v_sinks, v_w_out, v_b_out, v_attn_post_g, v_ffn2_pre_g, v_ffn2_w_gu, v_ffn2_w_down, v_ffn2_post_g, v_ple_pre_g, v_w_ple_gate, v_w_ple_proj, v_ple_post_g):
    given = dict(locals())
    ws = {k: given[k] for k in WEIGHTS}
    ms = {k: given["m_" + k] for k in WEIGHTS}
    vs = {k: given["v_" + k] for k in WEIGHTS}

    def shard(t):
        return t.reshape(t.shape[1:])

    xs, ps, target = shard(x), shard(shard(p)), shard(loss_target)
    T, D = xs.shape
    small = {k: ws[k] for k in SMALL}

    def local(group, k):
        t = shard(group[k])
        return jnp.swapaxes(t, 0, 1) if k in COLUMN_SHARDED else t

    shards = {k: local(ws, k) for k in BIG}

    cast = dict(zip(BIG, _cast_bf16([shards[k] for k in BIG])))
    buckets_a = _bucket_tiles(PATTERNS_A)
    buckets_b = _bucket_tiles(PATTERNS_B)
    bias_a, _ = _bias_build(small["rel_bias"], buckets_a, 0, "bias_build_a")
    bias_b, (w_gu1, w_down1) = _bias_build(
        small["rel_bias"], buckets_b, N_HEAD_GROUP, "bias_build_b",
        side=("relay_gather", [cast["ffn1_w_gu"], cast["ffn1_w_down"]]))
    w_down1 = w_down1.reshape(-1, D)
    a_cfg = dict(patterns=PATTERNS_A, qcol=Q_A_COL, kcol=K_A_COL, vcol=V_A_COL, shared_kv=True)
    b_cfg = dict(patterns=PATTERNS_B, qcol=Q_B_COL, kcol=K_B_COL, vcol=V_B_COL, shared_kv=False)

    (h1, f1, a1, gu1), (w_in_g, w_down2) = _ffn_fwd(
        xs, small["ffn1_pre_g"], small["ffn1_post_g"], w_gu1, w_down1, "ffn1_fwd",
        side=("relay_gather", [cast["w_in"], cast["ffn2_w_down"]]))
    w_in_full = w_in_g.reshape(D_IN, D)
    w_down2 = w_down2.reshape(-1, D)
    (z, a2), (w_out_g,) = _inproj_fwd(h1, small["attn_pre_g"], w_in_full, small["b_in"],
                                      side=("relay_gather", [cast["w_out"]]))
    w_out_full = w_out_g.reshape(-1, D)
    (mix_a, lse_a), (w_gate, w_proj) = _attn_fwd(
        z, bias_a, small["sinks"], name="attn_a_fwd", **a_cfg,
        side=("relay_gather", [cast["w_ple_gate"], cast["w_ple_proj"]]))
    w_gate = w_gate.reshape(-1, D)
    (mix_b, lse_b), (w_gu2,) = _attn_fwd(
        z, bias_b, None, name="attn_b_fwd", **b_cfg, side=("relay_gather", [cast["ffn2_w_gu"]]))
    att, h2, mix = _outproj_fwd(mix_a, mix_b, w_out_full, small["b_out"], small["attn_post_g"], h1)
    (h3, f2, a3, gu2), _ = _ffn_fwd(h2, small["ffn2_pre_g"], small["ffn2_post_g"], w_gu2, w_down2, "ffn2_fwd")
    a4, dpre, de, dh4, loss, dg_ple_post = _ple_fwd_loss(
        h3, small["ple_pre_g"], w_gate, ps, w_proj, small["ple_post_g"], target)

    dh3, dg_ple_pre = _ple_bwd(dpre, w_gate, h3, small["ple_pre_g"], dh4)
    d_gate = _dw_rows(a4, dpre, "ple_dw_gate", min(256, D))
    d_proj = _ple_dw_proj(ps, de, N_DEV)
    landed = {}
    (dh2, df2, hh2, dgu2, dg_f2_post, dg_f2_pre), (landed["w_ple_gate"], landed["w_ple_proj"]) = _ffn_bwd(
        dh3, f2, small["ffn2_post_g"], h2, small["ffn2_pre_g"], gu2, w_gu2, w_down2, "ffn2_bwd",
        side=("exchange", [d_gate, d_proj]))
    d_gu2 = _dw_gu(a3, dgu2, "ffn2_dw_gu")
    d_down2 = _dw_down(hh2, df2, "ffn2_dw_down").reshape(N_DEV, -1, D)
    dmix_a, dmix_b, datt, dg_attn_post, db_out = _outproj_bwd(dh2, att, small["attn_post_g"], w_out_full)
    d_out = _dw_rows(mix, datt, "attn_dw_out", 256)
    (dqa, dka, dva, ds_a, dsinks), (landed["ffn2_w_down"],) = _attn_bwd(
        z, bias_a, small["sinks"], dmix_a, mix_a, lse_a, name="attn_a_bwd", **a_cfg,
        side=("exchange", [d_down2]))
    (dqb, dkb, dvb, ds_b), (landed["ffn2_w_gu"],) = _attn_bwd(
        z, bias_b, None, dmix_b, mix_b, lse_b, name="attn_b_bwd", **b_cfg, side=("exchange", [d_gu2]))
    (dh1, dz, db_in, dg_attn_pre), (landed["w_out"],) = _inproj_bwd(
        dqa, dka, dva, dqb, dkb, dvb, w_in_full, h1, small["attn_pre_g"], dh2, side=("exchange", [d_out]))
    cols = D_IN // 3
    d_in = _tn_matmul(
        dz, a2, pl.BlockSpec((DW_TILE, cols), lambda b, t: (t, b)), _tok(D),
        jax.ShapeDtypeStruct((D_IN, D), BF16), pl.BlockSpec((cols, D), lambda b, t: (b, 0)),
        3, T // DW_TILE, (cols, D), "attn_dw_in").reshape(N_DEV, D_IN // N_DEV, D)
    (grad_x, df1, hh1, dgu1, dg_f1_post, dg_f1_pre), (landed["w_in"],) = _ffn_bwd(
        dh1, f1, small["ffn1_post_g"], xs, small["ffn1_pre_g"], gu1, w_gu1, w_down1, "ffn1_bwd",
        side=("exchange", [d_in]))
    d_down1 = _dw_down(hh1, df1, "ffn1_dw_down").reshape(N_DEV, -1, D)
    d_gu1, (landed["ffn1_w_down"],) = _dw_gu(a1, dgu1, "ffn1_dw_gu", side=("exchange", [d_down1]))

    rb_a = _bias_grad(ds_a, buckets_a, "bias_grad_a")
    rb_b = _bias_grad(ds_b, buckets_b, "bias_grad_b").reshape(len(PATTERNS_B), N_HEAD_GROUP, NUM_BUCKETS)
    d_rel_bias = jnp.concatenate([rb_a.T, jnp.sum(rb_b, axis=0).T], axis=1)
    small_grads = {"ffn1_pre_g": dg_f1_pre, "ffn1_post_g": dg_f1_post, "attn_pre_g": dg_attn_pre,
                   "attn_post_g": dg_attn_post, "ffn2_pre_g": dg_f2_pre, "ffn2_post_g": dg_f2_post,
                   "ple_pre_g": dg_ple_pre, "ple_post_g": dg_ple_post, "b_out": db_out, "b_in": db_in,
                   "sinks": dsinks, "rel_bias": d_rel_bias}
    d_gu1_pairs = _pair_add(d_gu1, _pair_swap_call(d_gu1), "ffn1_dw_gu_pair_add")
    landed["ffn1_w_gu"], small_gathered = _final_exchange(d_gu1_pairs, small_grads, loss)

    result = {}
    for k in BIG:
        outs = _sum_adamw(landed[k], shards[k], local(ms, k), local(vs, k), _adamw_rows(shards[k].shape[0]),
                          k + "_adamw")
        if k in COLUMN_SHARDED:
            outs = [jnp.swapaxes(o, 0, 1) for o in outs]
        result[k] = [o.reshape(ws[k].shape) for o in outs]
    small_res, loss_all = _small_adamw(
        small_gathered, small, {k: ms[k] for k in SMALL}, {k: vs[k] for k in SMALL})
    result.update(small_res)

    out = [loss_all[0, 0], grad_x.reshape(x.shape)]
    for i in range(4):
        out += [result[k][i] for k in WEIGHTS]
    return tuple(out)
```

```python
import functools
import math

import numpy as np
import jax
import jax.numpy as jnp
from jax import lax
from jax.experimental import pallas as pl
from jax.experimental.pallas import tpu as pltpu

F32 = jnp.float32
BF16 = jnp.bfloat16
MESH = pl.DeviceIdType.MESH

N_DEV = 8
EPS = 1e-6
NEG_INF = -1e30
HEAD_DIM = 64
LANES = 128
QBLK = 128
D_IN = 2304
A_Q, A_KV, B_W = 512, 128, 512
N_HEAD_GROUP = 8
NUM_BUCKETS = 32
MAX_DISTANCE = 2048
PATTERNS_A = ((1, 127),)
PATTERNS_B = ((1, 128), (4, 128), (16, 128))
Q_A_COL, K_A_COL, V_A_COL = 0, 4, 5
Q_B_COL, K_B_COL, V_B_COL = 6, 10, 14

ADAM_LR, ADAM_B1, ADAM_B2, ADAM_EPS, ADAM_WD, ADAM_STEP = 0.001, 0.9, 0.999, 1e-08, 0.01, 10

TOKEN_TILE = 512
DW_TILE = 1024
FWD_BLOCKS = 4
BWD_BLOCKS = 4
VMEM_LIMIT = 56 * 1024 * 1024
ARB = "arbitrary"

BIG = ("ffn1_w_gu", "ffn1_w_down", "w_in", "w_out", "ffn2_w_gu", "ffn2_w_down", "w_ple_gate", "w_ple_proj")
GAINS = ("ffn1_pre_g", "ffn1_post_g", "attn_pre_g", "attn_post_g", "ffn2_pre_g", "ffn2_post_g",
         "ple_pre_g", "ple_post_g", "b_out")
SMALL = GAINS + ("b_in", "sinks", "rel_bias")
WEIGHTS = ("rel_bias", "ffn1_pre_g", "ffn1_w_gu", "ffn1_w_down", "ffn1_post_g", "attn_pre_g", "w_in", "b_in",
           "sinks", "w_out", "b_out", "attn_post_g", "ffn2_pre_g", "ffn2_w_gu", "ffn2_w_down", "ffn2_post_g",
           "ple_pre_g", "w_ple_gate", "w_ple_proj", "ple_post_g")


def _params(n_axes):
    return pltpu.CompilerParams(dimension_semantics=(ARB,) * n_axes, vmem_limit_bytes=VMEM_LIMIT)


def _mm(a, b):
    return jnp.dot(a, b, preferred_element_type=F32)


def _mm_nt(a, b):
    return lax.dot_general(a, b, (((1,), (1,)), ((), ())), preferred_element_type=F32)


def _mm_tn(a, b):
    return lax.dot_general(a, b, (((0,), (0,)), ((), ())), preferred_element_type=F32)


def _rstd(x):
    return lax.rsqrt(jnp.mean(x * x, axis=-1, keepdims=True) + EPS)


def _rms_bwd(x, r, gain, dy):
    n = x * r
    gdy = dy * gain
    return r * (gdy - n * jnp.mean(gdy * n, axis=-1, keepdims=True)), dy * n


def _colsum(v):
    return jnp.sum(v, axis=0, keepdims=True)


def _full(shape):
    return pl.BlockSpec(shape, lambda *_: (0,) * len(shape))


def _mesh_place():
    return lax.axis_index("x"), lax.axis_index("y"), lax.axis_index("c")


def _slot(dev):
    return 4 * dev[0] + 2 * dev[1] + dev[2]


def _peers(x, y, c):
    out = []
    for flip in range(1, N_DEV):
        dx, dy, dc = (flip >> 2) & 1, (flip >> 1) & 1, flip & 1
        out.append((1 - x if dx else x, 1 - y if dy else y, 1 - c if dc else c))
    return out


def _side_copies(kind, ins, outs, send_sems, recv_sems, local_sems, sem_row=0):
    n = len(ins)
    x, y, c = _mesh_place()
    me = _slot((x, y, c))
    peers = _peers(x, y, c)

    def src(a, block):
        return ins[a] if kind == "gather" else ins[a].at[block]

    def send(a, k, peer):
        return pltpu.make_async_remote_copy(
            src_ref=src(a, _slot(peer)), dst_ref=outs[a].at[me],
            send_sem=send_sems.at[sem_row + a, k], recv_sem=recv_sems.at[sem_row + a, k],
            device_id=peer, device_id_type=MESH)

    def arrival(a, k, peer):
        return pltpu.make_async_remote_copy(
            src_ref=src(a, _slot(peer)), dst_ref=outs[a].at[_slot(peer)],
            send_sem=send_sems.at[sem_row + a, k], recv_sem=recv_sems.at[sem_row + a, k],
            device_id=peer, device_id_type=MESH)

    def own(a):
        return pltpu.make_async_copy(src(a, me), outs[a].at[me], local_sems.at[sem_row + a, 0])

    def start():
        for k, peer in enumerate(peers):
            for a in range(n):
                send(a, k, peer).start()
        for a in range(n):
            own(a).start()

    def wait():
        for k, peer in enumerate(peers):
            for a in range(n):
                arrival(a, k, peer).wait_recv()
        for k, peer in enumerate(peers):
            for a in range(n):
                send(a, k, peer).wait_send()
        for a in range(n):
            own(a).wait()

    return start, None, wait


N_CHIPS = N_DEV // 2


def _pair_swap(ins, received, send_sems, recv_sems):
    n = len(ins)
    x, y, c = _mesh_place()
    sibling = (x, y, 1 - c)

    def send(a, q):
        return pltpu.make_async_remote_copy(
            src_ref=ins[a].at[2 * q + (1 - c)], dst_ref=received[a].at[q],
            send_sem=send_sems.at[a, q], recv_sem=recv_sems.at[a, q], device_id=sibling, device_id_type=MESH)

    def start():
        for a in range(n):
            for q in range(N_CHIPS):
                send(a, q).start()

    def wait():
        for a in range(n):
            for q in range(N_CHIPS):
                send(a, q).wait_recv()
        for a in range(n):
            for q in range(N_CHIPS):
                send(a, q).wait_send()

    return start, None, wait


def _quad_exchange(ins, outs, send_sems, recv_sems, local_sems, sem_row=0):
    n = len(ins)
    x, y, c = _mesh_place()
    mine = 2 * x + y
    chips = [(1 - x, y), (x, 1 - y), (1 - x, 1 - y)]

    def send(a, k, chip):
        return pltpu.make_async_remote_copy(
            src_ref=ins[a].at[2 * chip[0] + chip[1]], dst_ref=outs[a].at[mine],
            send_sem=send_sems.at[sem_row + a, k], recv_sem=recv_sems.at[sem_row + a, k],
            device_id=(chip[0], chip[1], c), device_id_type=MESH)

    def arrival(a, k, chip):
        return pltpu.make_async_remote_copy(
            src_ref=ins[a].at[2 * chip[0] + chip[1]], dst_ref=outs[a].at[2 * chip[0] + chip[1]],
            send_sem=send_sems.at[sem_row + a, k], recv_sem=recv_sems.at[sem_row + a, k],
            device_id=(chip[0], chip[1], c), device_id_type=MESH)

    def own(a):
        return pltpu.make_async_copy(ins[a].at[mine], outs[a].at[mine], local_sems.at[sem_row + a, 0])

    def start():
        for k, chip in enumerate(chips):
            for a in range(n):
                send(a, k, chip).start()
        for a in range(n):
            own(a).start()

    def wait():
        for k, chip in enumerate(chips):
            for a in range(n):
                arrival(a, k, chip).wait_recv()
        for k, chip in enumerate(chips):
            for a in range(n):
                send(a, k, chip).wait_send()
        for a in range(n):
            own(a).wait()

    return start, None, wait


def _relay_gather(ins, outs, send_sems, recv_sems, local_sems):
    n = len(ins)
    x, y, c = _mesh_place()
    me, sibling = (x, y, c), (x, y, 1 - c)
    chips = [(1 - x, y), (x, 1 - y), (1 - x, 1 - y)]

    def copy(a, k, block, to, src=None):
        dst = outs[a].at[_slot(block)]
        return pltpu.make_async_remote_copy(
            src_ref=dst if src is None else src, dst_ref=dst,
            send_sem=send_sems.at[a, k], recv_sem=recv_sems.at[a, k], device_id=to, device_id_type=MESH)

    def own(a):
        return pltpu.make_async_copy(ins[a], outs[a].at[_slot(me)], local_sems.at[a, 0])

    def start():
        for j, chip in enumerate(chips):
            for a in range(n):
                copy(a, 1 + j, me, (*chip, c), src=ins[a]).start()
        for a in range(n):
            copy(a, 0, me, sibling, src=ins[a]).start()
            own(a).start()

    def relay():
        for j, chip in enumerate(chips):
            for a in range(n):
                copy(a, 1 + j, (*chip, c), me).wait_recv()
                copy(a, 4 + j, (*chip, c), sibling).start()

    def wait():
        for a in range(n):
            copy(a, 0, sibling, me).wait_recv()
        for j, chip in enumerate(chips):
            for a in range(n):
                copy(a, 4 + j, (*chip, 1 - c), me).wait_recv()
        for j, chip in enumerate(chips):
            for a in range(n):
                copy(a, 1 + j, me, (*chip, c), src=ins[a]).wait_send()
                copy(a, 4 + j, (*chip, c), sibling).wait_send()
        for a in range(n):
            copy(a, 0, me, sibling, src=ins[a]).wait_send()
            own(a).wait()

    return start, relay, wait


def _side_out_shapes(kind, arrays):
    if kind in ("gather", "relay_gather"):
        return [jax.ShapeDtypeStruct((N_DEV,) + a.shape, a.dtype) for a in arrays]
    if kind == "pair_swap":
        return [jax.ShapeDtypeStruct((N_CHIPS,) + a.shape[1:], a.dtype) for a in arrays]
    return [jax.ShapeDtypeStruct(a.shape, a.dtype) for a in arrays]


def _hosted_call(body, name, grid, in_specs, out_specs, out_shape, scratch_shapes, args, side=None):
    if side is None:
        outs = pl.pallas_call(
            body, name=name, grid=grid, in_specs=in_specs, out_specs=out_specs, out_shape=out_shape,
            scratch_shapes=scratch_shapes, compiler_params=_params(len(grid)))(*args)
        return outs, []
    kind, arrays = side
    side_shapes = _side_out_shapes(kind, arrays)
    n_in, n_out, n_scr, n_side = len(in_specs), len(out_specs), len(scratch_shapes), len(arrays)

    def hosted(*refs):
        pos = 0
        groups = []
        for size in (n_in, n_side, n_out, len(side_shapes), n_scr):
            groups.append(refs[pos:pos + size])
            pos += size
        ins, side_in, outs, side_out, scr = groups
        send_sems, recv_sems, local_sems = refs[pos:]
        ids = [pl.program_id(d) for d in range(len(grid))]
        is_first = functools.reduce(jnp.logical_and, [i == 0 for i in ids])
        is_last = functools.reduce(jnp.logical_and, [i == g - 1 for i, g in zip(ids, grid)])
        if kind == "relay_gather":
            start, relay, wait = _relay_gather(side_in, side_out, send_sems, recv_sems, local_sems)
        elif kind == "pair_swap":
            start, relay, wait = _pair_swap(side_in, side_out, send_sems, recv_sems)
        elif kind == "quad_exchange":
            start, relay, wait = _quad_exchange(side_in, side_out, send_sems, recv_sems, local_sems)
        else:
            start, relay, wait = _side_copies(kind, side_in, side_out, send_sems, recv_sems, local_sems)
        pl.when(is_first)(start)
        if relay is not None:
            pl.when(is_last)(relay)
        body(*ins, *outs, *scr)
        pl.when(is_last)(wait)

    any_spec = pl.BlockSpec(memory_space=pl.ANY)
    outs = pl.pallas_call(
        hosted, name=name, grid=grid,
        in_specs=list(in_specs) + [any_spec] * n_side,
        out_specs=list(out_specs) + [any_spec] * len(side_shapes),
        out_shape=list(out_shape) + side_shapes,
        scratch_shapes=list(scratch_shapes) + [pltpu.SemaphoreType.DMA((n_side, 7)), pltpu.SemaphoreType.DMA((n_side, 7)),
                                               pltpu.SemaphoreType.DMA((n_side, N_CHIPS))],
        compiler_params=_params(len(grid)))(*args, *arrays)
    return outs[:n_out], outs[n_out:]


def _lane_chunks(width, chunk=2 * LANES):
    return [slice(n0, min(n0 + chunk, width)) for n0 in range(0, width, chunk)]


def _pipelined(chunks, first, middle, last):
    n = len(chunks)
    a, b, total = {}, {}, None
    for step in range(n + 2):
        if step < n:
            a[step] = first(chunks[step])
        if 0 <= step - 1 < n:
            b[step - 1] = middle(chunks[step - 1], a.pop(step - 1))
        if 0 <= step - 2 < n:
            part = last(chunks[step - 2], b.pop(step - 2))
            total = part if total is None else total + part
    return total


def _ffn_fwd(h, g_pre, g_post, w_gu, w_down, name, side=None):
    T, D = h.shape
    nj = w_gu.shape[0] // 2
    FB = w_gu.shape[1]
    tm = TOKEN_TILE

    def body(h_ref, gpre_ref, gpost_ref, wg_ref, wu_ref, wd_ref, hout_ref, f_ref, a_ref, gu_ref, a_scr, acc):
        j = pl.program_id(1)

        @pl.when(j == 0)
        def _():
            x = h_ref[...]
            a = (x * _rstd(x) * gpre_ref[...]).astype(BF16)
            a_scr[...] = a
            a_ref[...] = a
            acc[...] = jnp.zeros_like(acc)

        a = a_scr[...]
        g = _mm_nt(a, wg_ref[...])
        u = _mm_nt(a, wu_ref[...])
        gu_ref[0] = g.astype(BF16)
        gu_ref[1] = u.astype(BF16)
        hh = (g * jax.nn.sigmoid(g) * u).astype(BF16)
        acc[...] += _mm(hh, wd_ref[...])

        @pl.when(j == nj - 1)
        def _():
            f = acc[...]
            f_ref[...] = f
            hout_ref[...] = h_ref[...] + 0.5 * (f * _rstd(f) * gpost_ref[...])

    return _hosted_call(
        body, name, (T // tm, nj),
        in_specs=[
            pl.BlockSpec((tm, D), lambda i, j: (i, 0)),
            _full((1, D)), _full((1, D)),
            pl.BlockSpec((None, FB, D), lambda i, j: (j, 0, 0)),
            pl.BlockSpec((None, FB, D), lambda i, j: (j + nj, 0, 0)),
            pl.BlockSpec((FB, D), lambda i, j: (j, 0)),
        ],
        out_specs=[
            pl.BlockSpec((tm, D), lambda i, j: (i, 0)),
            pl.BlockSpec((tm, D), lambda i, j: (i, 0)),
            pl.BlockSpec((tm, D), lambda i, j: (i, 0)),
            pl.BlockSpec((None, 2, tm, FB), lambda i, j: (j, 0, i, 0)),
        ],
        out_shape=[
            jax.ShapeDtypeStruct((T, D), F32),
            jax.ShapeDtypeStruct((T, D), F32),
            jax.ShapeDtypeStruct((T, D), BF16),
            jax.ShapeDtypeStruct((nj, 2, T, FB), BF16),
        ],
        scratch_shapes=[pltpu.VMEM((tm, D), BF16), pltpu.VMEM((tm, D), F32)],
        args=(h, g_pre, g_post, w_gu, w_gu, w_down), side=side)


def _ffn_bwd(dh_out, f, g_post, h, g_pre, gu, w_gu, w_down, name, side=None):
    T, D = h.shape
    nj = w_gu.shape[0] // 2
    FB = w_gu.shape[1]
    tm = TOKEN_TILE

    def body(dho_ref, f_ref, gpost_ref, h_ref, gpre_ref, gu_ref, wg_ref, wu_ref, wd_ref,
             dhin_ref, df_ref, hh_ref, dgu_ref, dgpost_ref, dgpre_ref, df_scr, da):
        i, j = pl.program_id(0), pl.program_id(1)

        @pl.when(jnp.logical_and(i == 0, j == 0))
        def _():
            dgpost_ref[...] = jnp.zeros_like(dgpost_ref)
            dgpre_ref[...] = jnp.zeros_like(dgpre_ref)

        @pl.when(j == 0)
        def _():
            fv = f_ref[...]
            df, dgain = _rms_bwd(fv, _rstd(fv), gpost_ref[...], 0.5 * dho_ref[...])
            dgpost_ref[...] += _colsum(dgain)
            dfb = df.astype(BF16)
            df_scr[...] = dfb
            df_ref[...] = dfb
            da[...] = jnp.zeros_like(da)

        dfb = df_scr[...]

        halves = (slice(0, tm // 2), slice(tm // 2, tm))

        def hidden_grad(c):
            return [_mm_nt(dfb[rows], wd_ref[c, :]) for rows in halves]

        def through_swiglu(c, dhh):
            dhh = jnp.concatenate(dhh, axis=0)
            g = gu_ref[0, :, c].astype(F32)
            u = gu_ref[1, :, c].astype(F32)
            sg = jax.nn.sigmoid(g)
            silu = g * sg
            hh_ref[:, c] = (silu * u).astype(BF16)
            dg = (dhh * u * (sg * (1.0 + (g - silu)))).astype(BF16)
            du = (dhh * silu).astype(BF16)
            dgu_ref[0, :, c] = dg
            dgu_ref[1, :, c] = du
            return dg, du

        def input_grad(c, dgu):
            return jnp.concatenate(
                [_mm(dgu[0][rows], wg_ref[c, :]) + _mm(dgu[1][rows], wu_ref[c, :]) for rows in halves], axis=0)

        da[...] += _pipelined(_lane_chunks(FB), hidden_grad, through_swiglu, input_grad)

        @pl.when(j == nj - 1)
        def _():
            x = h_ref[...]
            dx, dgain = _rms_bwd(x, _rstd(x), gpre_ref[...], da[...])
            dgpre_ref[...] += _colsum(dgain)
            dhin_ref[...] = dho_ref[...] + dx

    tile = pl.BlockSpec((tm, D), lambda i, j: (i, 0))
    return _hosted_call(
        body, name, (T // tm, nj),
        in_specs=[
            tile, tile, _full((1, D)), tile, _full((1, D)),
            pl.BlockSpec((None, 2, tm, FB), lambda i, j: (j, 0, i, 0)),
            pl.BlockSpec((None, FB, D), lambda i, j: (j, 0, 0)),
            pl.BlockSpec((None, FB, D), lambda i, j: (j + nj, 0, 0)),
            pl.BlockSpec((FB, D), lambda i, j: (j, 0)),
        ],
        out_specs=[
            tile, tile,
            pl.BlockSpec((None, tm, FB), lambda i, j: (j, i, 0)),
            pl.BlockSpec((None, 2, tm, FB), lambda i, j: (j, 0, i, 0)),
            _full((1, D)), _full((1, D)),
        ],
        out_shape=[
            jax.ShapeDtypeStruct((T, D), F32),
            jax.ShapeDtypeStruct((T, D), BF16),
            jax.ShapeDtypeStruct((nj, T, FB), BF16),
            jax.ShapeDtypeStruct((nj, 2, T, FB), BF16),
            jax.ShapeDtypeStruct((1, D), F32),
            jax.ShapeDtypeStruct((1, D), F32),
        ],
        scratch_shapes=[pltpu.VMEM((tm, D), BF16), pltpu.VMEM((tm, D), F32)],
        args=(dh_out, f, g_post, h, g_pre, gu, w_gu, w_gu, w_down), side=side)


def _tn_matmul(x, y, x_spec, y_spec, out_shape, out_spec, n_blocks, n_steps, acc_shape, name, side=None):
    def body(x_ref, y_ref, o_ref, acc):
        t = pl.program_id(1)

        @pl.when(t == 0)
        def _():
            acc[...] = jnp.zeros_like(acc)

        acc[...] += _mm_tn(x_ref[...].astype(BF16), y_ref[...].astype(BF16))

        @pl.when(t == n_steps - 1)
        def _():
            o_ref[...] = acc[...].astype(o_ref.dtype)

    outs, side_outs = _hosted_call(
        body, name, (n_blocks, n_steps), in_specs=[x_spec, y_spec], out_specs=[out_spec], out_shape=[out_shape],
        scratch_shapes=[pltpu.VMEM(acc_shape, F32)], args=(x, y), side=side)
    return (outs[0], side_outs) if side is not None else outs[0]


def _inproj_fwd(h, g_pre, w_in, b_in, side=None):
    T, D = h.shape
    tm = TOKEN_TILE

    def body(h_ref, g_ref, w_ref, b_ref, z_ref, a_ref):
        x = h_ref[...]
        a = (x * _rstd(x) * g_ref[...]).astype(BF16)
        a_ref[...] = a
        z_ref[...] = _mm_nt(a, w_ref[...]) + b_ref[...]

    return _hosted_call(
        body, "inproj_fwd", (T // tm,),
        in_specs=[pl.BlockSpec((tm, D), lambda i: (i, 0)), _full((1, D)), _full((D_IN, D)), _full((1, D_IN))],
        out_specs=[pl.BlockSpec((tm, D_IN), lambda i: (i, 0)), pl.BlockSpec((tm, D), lambda i: (i, 0))],
        out_shape=[jax.ShapeDtypeStruct((T, D_IN), F32), jax.ShapeDtypeStruct((T, D), BF16)],
        scratch_shapes=[], args=(h, g_pre, w_in, b_in), side=side)


def _inproj_bwd(dqa, dka, dva, dqb, dkb, dvb, w_in, h, g_pre, dres, side=None):
    T, D = h.shape
    tm = TOKEN_TILE

    def body(dqa_ref, dka_ref, dva_ref, dqb_ref, dkb_ref, dvb_ref, w_ref, h_ref, g_ref, dres_ref,
             dh_ref, dz_ref, dbin_ref, dg_ref):
        i = pl.program_id(0)

        @pl.when(i == 0)
        def _():
            dbin_ref[...] = jnp.zeros_like(dbin_ref)
            dg_ref[...] = jnp.zeros_like(dg_ref)

        dz = jnp.concatenate([dqa_ref[...], dka_ref[...], dva_ref[...], dqb_ref[...], dkb_ref[...], dvb_ref[...]],
                             axis=1)
        dbin_ref[...] += _colsum(dz)
        dzb = dz.astype(BF16)
        dz_ref[...] = dzb
        da = _mm(dzb, w_ref[...])
        x = h_ref[...]
        dx, dgain = _rms_bwd(x, _rstd(x), g_ref[...], da)
        dg_ref[...] += _colsum(dgain)
        dh_ref[...] = dres_ref[...] + dx

    def tile(w):
        return pl.BlockSpec((tm, w), lambda i: (i, 0))

    return _hosted_call(
        body, "inproj_bwd", (T // tm,),
        in_specs=[tile(A_Q), tile(A_KV), tile(A_KV), tile(B_W), tile(B_W), tile(B_W),
                  _full((D_IN, D)), tile(D), _full((1, D)), tile(D)],
        out_specs=[tile(D), tile(D_IN), _full((1, D_IN)), _full((1, D))],
        out_shape=[jax.ShapeDtypeStruct((T, D), F32), jax.ShapeDtypeStruct((T, D_IN), BF16),
                   jax.ShapeDtypeStruct((1, D_IN), F32), jax.ShapeDtypeStruct((1, D), F32)],
        scratch_shapes=[], args=(dqa, dka, dva, dqb, dkb, dvb, w_in, h, g_pre, dres), side=side)


def _bucket_tiles(patterns):
    i = np.arange(QBLK)[:, None]
    j = np.arange(2 * QBLK)[None, :]
    dist = QBLK + i - j
    max_exact = NUM_BUCKETS // 2
    tiles = []
    for dilation, max_dist in patterns:
        n = np.maximum(dist * dilation, 0)
        nf = np.maximum(n, 1).astype(np.float32)
        large = max_exact + (np.log(nf / np.float32(max_exact)) / np.float32(math.log(MAX_DISTANCE / max_exact))
                             * np.float32(NUM_BUCKETS - max_exact)).astype(np.int32)
        bucket = np.where(n < max_exact, n, np.minimum(large, NUM_BUCKETS - 1))
        tiles.append(np.where((dist >= 0) & (dist <= max_dist), bucket, -1))
    return jnp.asarray(np.stack(tiles).astype(np.int32))


def _bias_build(rel_bias, buckets, head0, name, side=None):
    n = buckets.shape[0]

    def body(bk_ref, rb_ref, o_ref):
        bk = bk_ref[...]
        base = jnp.where(bk < 0, NEG_INF, 0.0).astype(F32)
        for hd in range(N_HEAD_GROUP):
            o_ref[hd] = lax.fori_loop(
                0, NUM_BUCKETS, lambda b, acc, hd=hd: jnp.where(bk == b, rb_ref[b, head0 + hd], acc), base)

    outs, side_outs = _hosted_call(
        body, name, (n,),
        in_specs=[pl.BlockSpec((None, QBLK, 2 * QBLK), lambda p: (p, 0, 0)), pl.BlockSpec(memory_space=pltpu.SMEM)],
        out_specs=[pl.BlockSpec((None, N_HEAD_GROUP, QBLK, 2 * QBLK), lambda p: (p, 0, 0, 0))],
        out_shape=[jax.ShapeDtypeStruct((n, N_HEAD_GROUP, QBLK, 2 * QBLK), F32)],
        scratch_shapes=[], args=(buckets, rel_bias), side=side)
    return outs[0], side_outs


def _bias_grad(ds, buckets, name):
    n = buckets.shape[0]

    def body(ds_ref, bk_ref, o_ref):
        bk = bk_ref[...]
        row = lax.broadcasted_iota(jnp.int32, (NUM_BUCKETS, 2 * QBLK), 0)
        for hd in range(N_HEAD_GROUP):
            d = ds_ref[hd]
            per_key = jnp.zeros((NUM_BUCKETS, 2 * QBLK), F32)
            for b in range(NUM_BUCKETS):
                per_key = jnp.where(row == b, jnp.sum(jnp.where(bk == b, d, 0.0), axis=0, keepdims=True), per_key)
            o_ref[hd] = jnp.broadcast_to(jnp.sum(per_key, axis=1, keepdims=True), (NUM_BUCKETS, LANES))

    out = pl.pallas_call(
        body, name=name, grid=(n,),
        in_specs=[pl.BlockSpec((None, N_HEAD_GROUP, QBLK, 2 * QBLK), lambda p: (p, 0, 0, 0)),
                  pl.BlockSpec((None, QBLK, 2 * QBLK), lambda p: (p, 0, 0))],
        out_specs=pl.BlockSpec((None, N_HEAD_GROUP, NUM_BUCKETS, LANES), lambda p: (p, 0, 0, 0)),
        out_shape=jax.ShapeDtypeStruct((n, N_HEAD_GROUP, NUM_BUCKETS, LANES), F32),
        compiler_params=_params(1),
    )(ds, buckets)
    return out[:, :, :, 0].reshape(n * N_HEAD_GROUP, NUM_BUCKETS)


def _class_rows(start, dilation):
    if dilation == 1:
        return pl.ds(pl.multiple_of(start, QBLK), QBLK)
    return pl.ds(start, QBLK, stride=dilation)


def _starts_class(u, blocks_per_pass, n_blocks):
    return blocks_per_pass % n_blocks == 0 and u % n_blocks == 0


def _block_starts(idx, n_blocks, dilation):
    cls = idx // n_blocks
    n = idx % n_blocks
    cur = cls + dilation * QBLK * n
    prev = cls + dilation * QBLK * jnp.maximum(n - 1, 0)
    return n, cur, prev


class _HeadPair:
    def __init__(self, g, shared_kv):
        self.lane = lax.broadcasted_iota(jnp.int32, (1, LANES), 1)
        self.lower = self.lane < HEAD_DIM
        self.shared_kv = shared_kv
        self.key_lanes = (self.lane >= HEAD_DIM).astype(jnp.int32) == (g // 2)

    def stack(self, t):
        return jnp.concatenate([jnp.where(self.lower, t, 0.0), jnp.where(self.lower, 0.0, t)], axis=0).astype(BF16)

    def unstack(self, t2):
        return jnp.where(self.lower, t2[:QBLK], t2[QBLK:])

    def keys(self, t):
        if self.shared_kv:
            return jnp.where(self.key_lanes, t, pltpu.roll(t, HEAD_DIM, 1))
        return t

    def key_grads(self, t):
        if self.shared_kv:
            return jnp.where(self.key_lanes, t + pltpu.roll(t, HEAD_DIM, 1), 0.0)
        return t


def _attn_specs(T, qcol, kcol, vcol, shared_kv):
    kv = (lambda c: (lambda g: (0, c))) if shared_kv else (lambda c: (lambda g: (0, c + g)))
    return [pl.BlockSpec((T, LANES), lambda g: (0, qcol + g)),
            pl.BlockSpec((T, LANES), kv(kcol)),
            pl.BlockSpec((T, LANES), kv(vcol))]


def _attn_fwd(z, bias, sinks, patterns, qcol, kcol, vcol, shared_kv, name, side=None):
    T = z.shape[0]
    n_pat = len(patterns)
    has_sink = sinks is not None

    def body(*refs):
        if has_sink:
            sink_ref, refs = refs[0], refs[1:]
        q_ref, k_ref, v_ref, b_ref, o_ref, l_ref = refs[:6]
        po_scr = refs[6:6 + n_pat]
        pl_scr = refs[6 + n_pat:]
        g = pl.program_id(0)
        heads = _HeadPair(g, shared_kv)
        in_prev = lax.broadcasted_iota(jnp.int32, (2 * QBLK, 2 * QBLK), 1) < QBLK

        for pi, (dilation, _) in enumerate(patterns):
            n_blocks = T // (QBLK * dilation)

            def step(it, carry, pi=pi, dilation=dilation, n_blocks=n_blocks):
                blocks = []
                for u in range(FWD_BLOCKS):
                    n, cur, prev = _block_starts(it * FWD_BLOCKS + u, n_blocks, dilation)
                    rows_c, rows_p = _class_rows(cur, dilation), _class_rows(prev, dilation)
                    qm = heads.stack(q_ref[rows_c, :])
                    k_cur, v_cur = k_ref[rows_c, :], v_ref[rows_c, :]
                    no_past = _starts_class(u, FWD_BLOCKS, n_blocks)
                    if no_past:
                        k2, v2 = k_cur, v_cur
                    else:
                        if u % min(FWD_BLOCKS, n_blocks) == 0:
                            k_prev, v_prev = k_ref[rows_p, :], v_ref[rows_p, :]
                        k2 = jnp.concatenate([k_prev, k_cur], axis=0)
                        v2 = jnp.concatenate([v_prev, v_cur], axis=0)
                    k2, v2 = heads.keys(k2).astype(BF16), heads.keys(v2).astype(BF16)
                    k_prev, v_prev = k_cur, v_cur
                    blocks.append(dict(n=n, no_past=no_past, rows=rows_c, v2=v2, s=_mm_nt(qm, k2)))
                for b in blocks:
                    if b["no_past"]:
                        b["s"] = b["s"] * (HEAD_DIM ** -0.5) + b_ref[pi, :, QBLK:]
                    else:
                        s = b["s"] * (HEAD_DIM ** -0.5) + b_ref[pi]
                        b["s"] = jnp.where(jnp.logical_and(in_prev, b["n"] == 0), NEG_INF, s)
                    b["m"] = jnp.max(b["s"], axis=1, keepdims=True)
                for b in blocks:
                    b["pr"] = jnp.exp(b["s"] - b["m"])
                    b["den"] = jnp.sum(b["pr"], axis=1, keepdims=True)
                for b in blocks:
                    b["o2"] = _mm(b["pr"].astype(BF16), b["v2"])
                for b in blocks:
                    lse = b["m"] + jnp.log(b["den"])
                    po_scr[pi][b["rows"], :] = heads.unstack(b["o2"] / b["den"])
                    pl_scr[2 * pi][b["rows"], :] = jnp.broadcast_to(lse[:QBLK], (QBLK, LANES))
                    pl_scr[2 * pi + 1][b["rows"], :] = jnp.broadcast_to(lse[QBLK:], (QBLK, LANES))
                return carry

            lax.fori_loop(0, (dilation * n_blocks) // FWD_BLOCKS, step, 0)

        def merge(ci, carry):
            rows = pl.ds(pl.multiple_of(ci * QBLK, QBLK), QBLK)
            weights = []
            for hd in range(2):
                parts = [pl_scr[2 * pi + hd][rows, :] for pi in range(n_pat)]
                m = functools.reduce(jnp.maximum, parts)
                if has_sink:
                    sink = sink_ref[0, 2 * g + hd]
                    m = jnp.maximum(m, sink)
                terms = [jnp.exp(x - m) for x in parts]
                den = functools.reduce(jnp.add, terms)
                if has_sink:
                    den = den + jnp.exp(sink - m)
                l_ref[hd, rows, :] = m + jnp.log(den)
                inv = 1.0 / den
                weights.append([t * inv for t in terms])
            o_ref[rows, :] = functools.reduce(
                jnp.add, [jnp.where(heads.lower, weights[0][pi], weights[1][pi]) * po_scr[pi][rows, :]
                          for pi in range(n_pat)])
            return carry

        lax.fori_loop(0, T // QBLK, merge, 0)

    in_specs = _attn_specs(T, qcol, kcol, vcol, shared_kv)
    in_specs.append(pl.BlockSpec((n_pat, None, 2 * QBLK, 2 * QBLK), lambda g: (0, g, 0, 0)))
    args = [z, z, z, bias.reshape(n_pat, N_HEAD_GROUP // 2, 2 * QBLK, 2 * QBLK)]
    if has_sink:
        in_specs.insert(0, pl.BlockSpec(memory_space=pltpu.SMEM))
        args.insert(0, sinks)
    return _hosted_call(
        body, name, (N_HEAD_GROUP // 2,),
        in_specs=in_specs,
        out_specs=[pl.BlockSpec((T, LANES), lambda g: (0, g)), pl.BlockSpec((2, T, LANES), lambda g: (g, 0, 0))],
        out_shape=[jax.ShapeDtypeStruct((T, N_HEAD_GROUP * HEAD_DIM), F32),
                   jax.ShapeDtypeStruct((N_HEAD_GROUP, T, LANES), F32)],
        scratch_shapes=[pltpu.VMEM((T, LANES), F32)] * (3 * n_pat), args=args, side=side)


def _attn_bwd(z, bias, sinks, d_out, out, lse, patterns, qcol, kcol, vcol, shared_kv, name, side=None):
    T = z.shape[0]
    n_pat = len(patterns)
    has_sink = sinks is not None
    kv_width = LANES if shared_kv else N_HEAD_GROUP * HEAD_DIM

    def body(*refs):
        if has_sink:
            sink_ref, refs = refs[0], refs[1:]
        q_ref, k_ref, v_ref, b_ref, do_ref, o_ref, l0_ref, l1_ref = refs[:8]
        dq_ref, dk_ref, dv_ref, ds_ref = refs[8:12]
        dsink_ref = refs[12] if has_sink else None
        dk_acc, dv_acc = refs[-2:]
        g = pl.program_id(0)
        heads = _HeadPair(g, shared_kv)
        in_prev = lax.broadcasted_iota(jnp.int32, (2 * QBLK, 2 * QBLK), 1) < QBLK

        dq_ref[...] = jnp.zeros_like(dq_ref)
        ds_ref[...] = jnp.zeros_like(ds_ref)
        dk_acc[...] = jnp.zeros_like(dk_acc)
        dv_acc[...] = jnp.zeros_like(dv_acc)

        dsink = jnp.zeros((1, LANES), F32)
        for pi, (dilation, _) in enumerate(patterns):
            n_blocks = T // (QBLK * dilation)

            def step(idx, dsink, pi=pi, dilation=dilation, n_blocks=n_blocks):
                blocks = []
                for u in range(BWD_BLOCKS):
                    n, cur, prev = _block_starts(idx * BWD_BLOCKS + u, n_blocks, dilation)
                    rows_c, rows_p = _class_rows(cur, dilation), _class_rows(prev, dilation)
                    qm = heads.stack(q_ref[rows_c, :])
                    k_cur, v_cur = k_ref[rows_c, :], v_ref[rows_c, :]
                    first = u % min(BWD_BLOCKS, n_blocks) == 0
                    no_past = _starts_class(u, BWD_BLOCKS, n_blocks)
                    if no_past:
                        k2, v2 = k_cur, v_cur
                    else:
                        if first:
                            k_prev, v_prev = k_ref[rows_p, :], v_ref[rows_p, :]
                        k2 = jnp.concatenate([k_prev, k_cur], axis=0)
                        v2 = jnp.concatenate([v_prev, v_cur], axis=0)
                    k2, v2 = heads.keys(k2).astype(BF16), heads.keys(v2).astype(BF16)
                    k_prev, v_prev = k_cur, v_cur
                    d_o = do_ref[rows_c, :]
                    dom = heads.stack(d_o)
                    dd = d_o * o_ref[rows_c, :]
                    delta = jnp.concatenate([jnp.sum(jnp.where(heads.lower, dd, 0.0), axis=1, keepdims=True),
                                             jnp.sum(jnp.where(heads.lower, 0.0, dd), axis=1, keepdims=True)], axis=0)
                    lse = jnp.concatenate([l0_ref[rows_c, :], l1_ref[rows_c, :]], axis=0)
                    blocks.append(dict(n=n, first=first, no_past=no_past, rows_c=rows_c, rows_p=rows_p, qm=qm, k2=k2,
                                       dom=dom, delta=delta, lse=lse, s=_mm_nt(qm, k2), dp=_mm_nt(dom, v2)))
                for b in blocks:
                    if b["no_past"]:
                        s = b["s"] * (HEAD_DIM ** -0.5) + b_ref[pi, :, QBLK:]
                        b["pr"] = jnp.exp(s - b["lse"])
                    else:
                        s = b["s"] * (HEAD_DIM ** -0.5) + b_ref[pi]
                        s = jnp.where(jnp.logical_and(in_prev, b["n"] == 0), NEG_INF, s)
                        b["pr"] = jnp.exp(s - jnp.concatenate([b["lse"], b["lse"]], axis=1))
                    b["ds"] = b["pr"] * (b["dp"] - b["delta"])
                for b in blocks:
                    dsb = b["ds"].astype(BF16)
                    b["dq2"] = _mm(dsb, b["k2"])
                    b["dk2"] = _mm_tn(dsb, b["qm"])
                    b["dv2"] = _mm_tn(b["pr"].astype(BF16), b["dom"])
                for b in blocks:
                    b["dk2"] = heads.key_grads(b["dk2"]) * (HEAD_DIM ** -0.5)
                    b["dv2"] = heads.key_grads(b["dv2"])
                for u, b in enumerate(blocks):
                    dq_ref[b["rows_c"], :] += heads.unstack(b["dq2"]) * (HEAD_DIM ** -0.5)
                    if b["no_past"]:
                        ds_ref[pi, :, QBLK:] += b["ds"]
                        dk_own, dv_own = b["dk2"], b["dv2"]
                    else:
                        ds_ref[pi] += b["ds"]
                        dk_own, dv_own = b["dk2"][QBLK:], b["dv2"][QBLK:]
                    if u + 1 < len(blocks) and not blocks[u + 1]["first"]:
                        dk_own = dk_own + blocks[u + 1]["dk2"][:QBLK]
                        dv_own = dv_own + blocks[u + 1]["dv2"][:QBLK]
                    if b["first"] and not b["no_past"]:
                        dk_acc[b["rows_p"], :] += b["dk2"][:QBLK]
                        dv_acc[b["rows_p"], :] += b["dv2"][:QBLK]
                    dk_acc[b["rows_c"], :] += dk_own
                    dv_acc[b["rows_c"], :] += dv_own
                    if has_sink:
                        for hd in range(2):
                            rows_h = slice(QBLK * hd, QBLK * (hd + 1))
                            p_sink = jnp.exp(sink_ref[0, 2 * g + hd] - b["lse"][rows_h, 0:1])
                            dsink = dsink - jnp.where(heads.lane == 2 * g + hd,
                                                      jnp.sum(p_sink * b["delta"][rows_h]), 0.0)
                return dsink

            dsink = lax.fori_loop(0, (dilation * n_blocks) // BWD_BLOCKS, step, dsink)

        if shared_kv:
            @pl.when(g == 0)
            def _():
                dk_ref[...] = dk_acc[...]
                dv_ref[...] = dv_acc[...]

            @pl.when(g != 0)
            def _():
                dk_ref[...] += dk_acc[...]
                dv_ref[...] += dv_acc[...]
        else:
            dk_ref[...] = dk_acc[...]
            dv_ref[...] = dv_acc[...]

        if has_sink:
            @pl.when(g == 0)
            def _():
                dsink_ref[...] = dsink

            @pl.when(g != 0)
            def _():
                dsink_ref[...] += dsink

    pair = pl.BlockSpec((T, LANES), lambda g: (0, g))
    stacked = pl.BlockSpec((n_pat, None, 2 * QBLK, 2 * QBLK), lambda g: (0, g, 0, 0))
    stacked_shape = (n_pat, N_HEAD_GROUP // 2, 2 * QBLK, 2 * QBLK)
    in_specs = _attn_specs(T, qcol, kcol, vcol, shared_kv)
    in_specs += [stacked, pair, pair,
                 pl.BlockSpec((None, T, LANES), lambda g: (2 * g, 0, 0)),
                 pl.BlockSpec((None, T, LANES), lambda g: (2 * g + 1, 0, 0))]
    args = [z, z, z, bias.reshape(stacked_shape), d_out, out, lse, lse]
    kv_out = _full((T, LANES)) if shared_kv else pair
    out_specs = [pair, kv_out, kv_out, stacked]
    out_shape = [jax.ShapeDtypeStruct((T, N_HEAD_GROUP * HEAD_DIM), F32),
                 jax.ShapeDtypeStruct((T, kv_width), F32), jax.ShapeDtypeStruct((T, kv_width), F32),
                 jax.ShapeDtypeStruct(stacked_shape, F32)]
    if has_sink:
        in_specs.insert(0, pl.BlockSpec(memory_space=pltpu.SMEM))
        args.insert(0, sinks)
        out_specs.append(_full((1, LANES)))
        out_shape.append(jax.ShapeDtypeStruct((1, LANES), F32))
    outs, side_outs = _hosted_call(
        body, name, (N_HEAD_GROUP // 2,), in_specs=in_specs, out_specs=out_specs, out_shape=out_shape,
        scratch_shapes=[pltpu.VMEM((T, LANES), F32), pltpu.VMEM((T, LANES), F32)], args=args, side=side)
    outs = list(outs)
    outs[3] = outs[3].reshape(n_pat, N_HEAD_GROUP, QBLK, 2 * QBLK)
    return outs, side_outs


def _outproj_fwd(mix_a, mix_b, w_out, b_out, g_post, h):
    T, D = h.shape
    tm = TOKEN_TILE
    d_mix = w_out.shape[0]

    def body(ma_ref, mb_ref, w_ref, b_ref, g_ref, h_ref, att_ref, hout_ref, mix_ref):
        mix = jnp.concatenate([ma_ref[...], mb_ref[...]], axis=1).astype(BF16)
        mix_ref[...] = mix
        att = _mm(mix, w_ref[...]) + b_ref[...]
        att_ref[...] = att
        hout_ref[...] = h_ref[...] + att * _rstd(att) * g_ref[...]

    def tile(w):
        return pl.BlockSpec((tm, w), lambda i: (i, 0))

    return pl.pallas_call(
        body, name="outproj_fwd", grid=(T // tm,),
        in_specs=[tile(A_Q), tile(B_W), _full((d_mix, D)), _full((1, D)), _full((1, D)), tile(D)],
        out_specs=[tile(D), tile(D), tile(d_mix)],
        out_shape=[jax.ShapeDtypeStruct((T, D), F32), jax.ShapeDtypeStruct((T, D), F32),
                   jax.ShapeDtypeStruct((T, d_mix), BF16)],
        compiler_params=_params(1),
    )(mix_a, mix_b, w_out, b_out, g_post, h)


def _outproj_bwd(dh, att, g_post, w_out):
    T, D = dh.shape
    tm = TOKEN_TILE
    d_mix = w_out.shape[0]

    def body(dh_ref, att_ref, g_ref, w_ref, dma_ref, dmb_ref, datt_ref, dg_ref, db_ref):
        i = pl.program_id(0)

        @pl.when(i == 0)
        def _():
            dg_ref[...] = jnp.zeros_like(dg_ref)
            db_ref[...] = jnp.zeros_like(db_ref)

        att = att_ref[...]
        datt, dgain = _rms_bwd(att, _rstd(att), g_ref[...], dh_ref[...])
        dg_ref[...] += _colsum(dgain)
        db_ref[...] += _colsum(datt)
        dattb = datt.astype(BF16)
        datt_ref[...] = dattb
        dmix = _mm_nt(dattb, w_ref[...])
        dma_ref[...] = dmix[:, :A_Q]
        dmb_ref[...] = dmix[:, A_Q:]

    def tile(w):
        return pl.BlockSpec((tm, w), lambda i: (i, 0))

    return pl.pallas_call(
        body, name="outproj_bwd", grid=(T // tm,),
        in_specs=[tile(D), tile(D), _full((1, D)), _full((d_mix, D))],
        out_specs=[tile(A_Q), tile(B_W), tile(D), _full((1, D)), _full((1, D))],
        out_shape=[jax.ShapeDtypeStruct((T, A_Q), F32), jax.ShapeDtypeStruct((T, B_W), F32),
                   jax.ShapeDtypeStruct((T, D), BF16), jax.ShapeDtypeStruct((1, D), F32),
                   jax.ShapeDtypeStruct((1, D), F32)],
        compiler_params=_params(1),
    )(dh, att, g_post, w_out)


def _ple_fwd_loss(h, g_pre, w_gate, p, w_proj, g_post, target):
    T, D = h.shape
    tm = TOKEN_TILE
    n_proj, ple, db = w_proj.shape

    def body(h_ref, gpre_ref, wg_ref, p_ref, wp_ref, gpost_ref, t_ref,
             a_ref, dpre_ref, de_ref, dh_ref, loss_ref, dgpost_ref, dgpre_ref):
        i = pl.program_id(0)

        @pl.when(i == 0)
        def _():
            loss_ref[...] = jnp.zeros_like(loss_ref)
            dgpost_ref[...] = jnp.zeros_like(dgpost_ref)
            dgpre_ref[...] = jnp.zeros_like(dgpre_ref)

        x = h_ref[...]
        rx = _rstd(x)
        a = (x * rx * gpre_ref[...]).astype(BF16)
        a_ref[...] = a
        gate = jax.nn.sigmoid(_mm(a, wg_ref[...]))
        pb = p_ref[...].astype(BF16)
        e = jnp.concatenate([_mm(pb, wp_ref[k]) for k in range(n_proj)], axis=1)
        ge = gate * e
        rg = _rstd(ge)
        diff = x + ge * rg * gpost_ref[...] - t_ref[...]
        loss_ref[...] += 0.5 * jnp.sum(jnp.mean(diff * diff, axis=1, keepdims=True))
        dy = diff * (1.0 / D)
        dge, dgain = _rms_bwd(ge, rg, gpost_ref[...], dy)
        dgpost_ref[...] += _colsum(dgain)
        de_ref[...] = (dge * gate).astype(BF16)
        dpre = (dge * e * gate * (1.0 - gate)).astype(BF16)
        dpre_ref[...] = dpre
        dx, dgain = _rms_bwd(x, rx, gpre_ref[...], _mm_nt(dpre, wg_ref[...]))
        dgpre_ref[...] += _colsum(dgain)
        dh_ref[...] = dy + dx

    def tile(w):
        return pl.BlockSpec((tm, w), lambda i: (i, 0))

    return pl.pallas_call(
        body, name="ple_fwd_bwd", grid=(T // tm,),
        in_specs=[tile(D), _full((1, D)), _full((D, D)), tile(ple), _full((n_proj, ple, db)), _full((1, D)), tile(D)],
        out_specs=[tile(D), tile(D), tile(D), tile(D), _full((1, LANES)), _full((1, D)), _full((1, D))],
        out_shape=[jax.ShapeDtypeStruct((T, D), BF16),
                   jax.ShapeDtypeStruct((T, D), BF16),
                   jax.ShapeDtypeStruct((T, D), BF16),
                   jax.ShapeDtypeStruct((T, D), F32),
                   jax.ShapeDtypeStruct((1, LANES), F32),
                   jax.ShapeDtypeStruct((1, D), F32),
                   jax.ShapeDtypeStruct((1, D), F32)],
        compiler_params=_params(1),
    )(h, g_pre, w_gate, p, w_proj, g_post, target)


def _ple_dw_proj(p, de, n_proj):
    T, ple = p.shape
    D = de.shape[1]
    db = D // n_proj
    tk = TOKEN_TILE
    nt = T // tk

    def body(p_ref, de_ref, o_ref, acc):
        t = pl.program_id(0)

        @pl.when(t == 0)
        def _():
            acc[...] = jnp.zeros_like(acc)

        acc[...] += _mm_tn(p_ref[...].astype(BF16), de_ref[...])

        @pl.when(t == nt - 1)
        def _():
            for k in range(n_proj):
                o_ref[k] = acc[:, k * db:(k + 1) * db].astype(BF16)

    return pl.pallas_call(
        body, name="ple_dw_proj", grid=(nt,),
        in_specs=[pl.BlockSpec((tk, ple), lambda t: (t, 0)), pl.BlockSpec((tk, D), lambda t: (t, 0))],
        out_specs=_full((n_proj, ple, db)), out_shape=jax.ShapeDtypeStruct((n_proj, ple, db), BF16),
        scratch_shapes=[pltpu.VMEM((ple, D), F32)], compiler_params=_params(1),
    )(p, de)


def _tok(width):
    return pl.BlockSpec((DW_TILE, width), lambda b, t: (t, 0))


def _dw_gu(a, dgu, name, side=None):
    T, D = a.shape
    nj, _, _, FB = dgu.shape
    return _tn_matmul(
        dgu, a, pl.BlockSpec((None, None, DW_TILE, FB), lambda b, t: (b % nj, b // nj, t, 0)), _tok(D),
        jax.ShapeDtypeStruct((2 * nj, FB, D), BF16), pl.BlockSpec((None, FB, D), lambda b, t: (b, 0, 0)),
        2 * nj, T // DW_TILE, (FB, D), name, side=side)


def _dw_down(hh, df, name, side=None):
    nj, T, FB = hh.shape
    D = df.shape[1]
    return _tn_matmul(
        hh, df, pl.BlockSpec((None, DW_TILE, FB), lambda b, t: (b, t, 0)), _tok(D),
        jax.ShapeDtypeStruct((nj, FB, D), BF16), pl.BlockSpec((None, FB, D), lambda b, t: (b, 0, 0)),
        nj, T // DW_TILE, (FB, D), name, side=side)


def _dw_rows(xm, y, name, rows):
    T, k = xm.shape
    D = y.shape[1]
    out = _tn_matmul(
        xm, y, pl.BlockSpec((DW_TILE, rows), lambda b, t: (t, b)), _tok(D),
        jax.ShapeDtypeStruct((k, D), BF16), pl.BlockSpec((rows, D), lambda b, t: (b, 0)),
        k // rows, T // DW_TILE, (rows, D), name)
    return out.reshape(N_DEV, k // N_DEV, D)


def _cast_bf16(arrays):
    n = len(arrays)

    def body(*refs):
        for a in range(n):
            refs[n + a][...] = refs[a][...].astype(BF16)

    return pl.pallas_call(
        body, name="cast_shards",
        in_specs=[pl.BlockSpec(memory_space=pltpu.VMEM)] * n, out_specs=[pl.BlockSpec(memory_space=pltpu.VMEM)] * n,
        out_shape=[jax.ShapeDtypeStruct(a.shape, BF16) for a in arrays],
        compiler_params=pltpu.CompilerParams(vmem_limit_bytes=VMEM_LIMIT),
    )(*arrays)


def _pack_layout(D, n_rel_rows):
    n_bin = -(-D_IN // D)
    row_bin = len(GAINS)
    row_sink = row_bin + n_bin
    row_loss = row_sink + 1
    row_rb = -(-(row_loss + 1) // 8) * 8
    n_rows = row_rb + -(-n_rel_rows // 8) * 8
    bin_parts = [(r, min(D, D_IN - r * D)) for r in range(n_bin)]
    return row_bin, row_sink, row_loss, row_rb, n_rows, bin_parts


def _pair_swap_call(grad_blocks):
    def body(g_in, received, send_sems, recv_sems):
        start, _, wait = _pair_swap([g_in], [received], send_sems, recv_sems)
        start()
        wait()

    any_spec = pl.BlockSpec(memory_space=pl.ANY)
    return pl.pallas_call(
        body, name="pair_swap", in_specs=[any_spec], out_specs=any_spec,
        out_shape=_side_out_shapes("pair_swap", [grad_blocks])[0],
        scratch_shapes=[pltpu.SemaphoreType.DMA((1, N_CHIPS)), pltpu.SemaphoreType.DMA((1, N_CHIPS))],
    )(grad_blocks)


def _pair_add(blocks, received, name):
    n, R, C = received.shape
    rows = _adamw_rows(R)
    core = lax.axis_index("c").astype(jnp.int32).reshape(1)

    def body(core_ref, a_ref, b_ref, o_ref):
        o_ref[...] = (a_ref[...].astype(F32) + b_ref[...].astype(F32)).astype(o_ref.dtype)

    tile = pl.BlockSpec((None, rows, C), lambda q, r, core_ref: (q, r, 0))
    return pl.pallas_call(
        body, name=name,
        grid_spec=pltpu.PrefetchScalarGridSpec(
            num_scalar_prefetch=1, grid=(n, R // rows),
            in_specs=[pl.BlockSpec((None, rows, C), lambda q, r, core_ref: (2 * q + core_ref[0], r, 0)), tile],
            out_specs=tile),
        out_shape=jax.ShapeDtypeStruct(received.shape, received.dtype), compiler_params=_params(2),
    )(core, blocks, received)


def _final_exchange(grad_blocks, partials, loss):
    D = partials["ffn1_pre_g"].shape[1]
    rb_shape = partials["rel_bias"].shape
    row_bin, row_sink, row_loss, row_rb, n_rows, bin_parts = _pack_layout(D, rb_shape[0])
    n_small = len(SMALL)

    def body(*refs):
        g_in = refs[0]
        part = dict(zip(SMALL, refs[1:1 + n_small]))
        loss_ref = refs[1 + n_small]
        landed, gath, pack, send_sems, recv_sems, local_sems = refs[2 + n_small:]

        pack[...] = jnp.zeros_like(pack)
        for i, name in enumerate(GAINS):
            pack[i:i + 1, :] = part[name][...]
        for r, width in bin_parts:
            pack[row_bin + r:row_bin + r + 1, 0:width] = part["b_in"][:, r * D:r * D + width]
        pack[row_sink:row_sink + 1, 0:LANES] = part["sinks"][...]
        pack[row_loss:row_loss + 1, 0:LANES] = loss_ref[...]
        pack[row_rb:row_rb + rb_shape[0], 0:rb_shape[1]] = part["rel_bias"][...]

        small_start, _, small_wait = _side_copies("gather", [pack], [gath], send_sems, recv_sems, local_sems, sem_row=0)
        big_start, _, big_wait = _quad_exchange([g_in], [landed], send_sems, recv_sems, local_sems, sem_row=1)
        small_start()
        big_start()
        small_wait()
        big_wait()

    args = [grad_blocks] + [partials[k] for k in SMALL] + [loss]
    vmem = pl.BlockSpec(memory_space=pltpu.VMEM)
    any_spec = pl.BlockSpec(memory_space=pl.ANY)
    return pl.pallas_call(
        body, name="final_exchange",
        in_specs=[any_spec] + [vmem] * (n_small + 1),
        out_specs=[any_spec, any_spec],
        out_shape=[jax.ShapeDtypeStruct(grad_blocks.shape, grad_blocks.dtype),
                   jax.ShapeDtypeStruct((N_DEV, n_rows, D), F32)],
        scratch_shapes=[pltpu.VMEM((n_rows, D), F32), pltpu.SemaphoreType.DMA((2, 7)),
                        pltpu.SemaphoreType.DMA((2, 7)), pltpu.SemaphoreType.DMA((2, N_CHIPS))],
    )(*args)


def _adamw(w, g, m, v):
    m = ADAM_B1 * m + (1.0 - ADAM_B1) * g
    v = ADAM_B2 * v + (1.0 - ADAM_B2) * (g * g)
    m_hat = m / (1.0 - ADAM_B1 ** ADAM_STEP)
    v_hat = v / (1.0 - ADAM_B2 ** ADAM_STEP)
    return -ADAM_LR * (m_hat / (jnp.sqrt(v_hat) + ADAM_EPS) + ADAM_WD * w), m, v


def _sum_adamw(partials, w, m, v, rows, name):
    R, C = w.shape
    n = partials.shape[0]

    def body(p_ref, w_ref, m_ref, v_ref, g_ref, d_ref, nm_ref, nv_ref):
        g = p_ref[0].astype(F32)
        for k in range(1, n):
            g = g + p_ref[k].astype(F32)
        g_ref[...] = g
        d_ref[...], nm_ref[...], nv_ref[...] = _adamw(w_ref[...], g, m_ref[...], v_ref[...])

    tile = pl.BlockSpec((rows, C), lambda i: (i, 0))
    return pl.pallas_call(
        body, name=name, grid=(R // rows,),
        in_specs=[pl.BlockSpec((n, rows, C), lambda i: (0, i, 0)), tile, tile, tile],
        out_specs=[tile] * 4, out_shape=[jax.ShapeDtypeStruct((R, C), F32)] * 4,
        compiler_params=_params(1),
    )(partials, w, m, v)


def _small_adamw(gathered, ws, ms, vs):
    D = ws["ffn1_pre_g"].shape[1]
    n_sink = ws["sinks"].shape[1]
    rb_shape = ws["rel_bias"].shape
    row_bin, row_sink, row_loss, row_rb, n_rows, bin_parts = _pack_layout(D, rb_shape[0])
    n_small = len(SMALL)

    def body(*refs):
        gath = refs[0]
        pos = 1
        w_ref = dict(zip(SMALL, refs[pos:pos + n_small]))
        m_ref = dict(zip(SMALL, refs[pos + n_small:pos + 2 * n_small]))
        v_ref = dict(zip(SMALL, refs[pos + 2 * n_small:pos + 3 * n_small]))
        pos += 3 * n_small
        outs = {name: refs[pos + 4 * i:pos + 4 * i + 4] for i, name in enumerate(SMALL)}
        loss_out = refs[pos + 4 * n_small]
        pack = refs[pos + 4 * n_small + 1]

        total = gath[0]
        for k in range(1, N_DEV):
            total = total + gath[k]
        pack[...] = total

        def update(name, g):
            g_out, d_out, m_out, v_out = outs[name]
            g_out[...] = g
            d_out[...], m_out[...], v_out[...] = _adamw(w_ref[name][...], g, m_ref[name][...], v_ref[name][...])

        for i, name in enumerate(GAINS):
            update(name, pack[i:i + 1, :])
        update("b_in", jnp.concatenate([pack[row_bin + r:row_bin + r + 1, 0:width] for r, width in bin_parts], axis=1))
        update("sinks", pack[row_sink:row_sink + 1, 0:n_sink])
        update("rel_bias", pack[row_rb:row_rb + rb_shape[0], 0:rb_shape[1]])
        loss_out[...] = pack[row_loss:row_loss + 1, 0:LANES]

    args = [gathered]
    for group in (ws, ms, vs):
        args += [group[k] for k in SMALL]
    out_shape = []
    for name in SMALL:
        out_shape += [jax.ShapeDtypeStruct(ws[name].shape, F32)] * 4
    out_shape.append(jax.ShapeDtypeStruct((1, LANES), F32))
    res = pl.pallas_call(
        body, name="small_adamw",
        in_specs=[pl.BlockSpec(memory_space=pltpu.VMEM)] * len(args),
        out_specs=[pl.BlockSpec(memory_space=pltpu.VMEM)] * len(out_shape),
        out_shape=out_shape,
        scratch_shapes=[pltpu.VMEM((n_rows, D), F32)],
    )(*args)
    per_name = {name: res[4 * i:4 * i + 4] for i, name in enumerate(SMALL)}
    return per_name, res[-1]


COLUMN_SHARDED = ("ffn1_w_gu", "ffn2_w_gu", "w_in")


def _adamw_rows(rows_total):
    return max(r for r in range(16, min(rows_total, 256) + 1, 16) if rows_total % r == 0)


def kernel(x, p, rel_bias, ffn1_pre_g, ffn1_w_gu, ffn1_w_down, ffn1_post_g, attn_pre_g, w_in, b_in, sinks, w_out, b_out, attn_post_g, ffn2_pre_g, ffn2_w_gu, ffn2_w_down, ffn2_post_g, ple_pre_g, w_ple_gate, w_ple_proj, ple_post_g, loss_target, m_rel_bias, m_ffn1_pre_g, m_ffn1_w_gu, m_ffn1_w_down, m_ffn1_post_g, m_attn_pre_g, m_w_in, m_b_in, m_sinks, m_w_out, m_b_out, m_attn_post_g, m_ffn2_pre_g, m_ffn2_w_gu, m_ffn2_w_down, m_ffn2_post_g, m_ple_pre_g, m_w_ple_gate, m_w_ple_proj, m_ple_post_g, v_rel_bias, v_ffn1_pre_g, v_ffn1_w_gu, v_ffn1_w_down, v_ffn1_post_g, v_attn_pre_g, v_w_in, v_b_in, v_sinks, v_w_out, v_b_out, v_attn_post_g, v_ffn2_pre_g, v_ffn2_w_gu, v_ffn2_w_down, v_ffn2_post_g, v_ple_pre_g, v_w_ple_gate, v_w_ple_proj, v_ple_post_g):
    given = dict(locals())
    ws = {k: given[k] for k in WEIGHTS}
    ms = {k: given["m_" + k] for k in WEIGHTS}
    vs = {k: given["v_" + k] for k in WEIGHTS}

    def shard(t):
        return t.reshape(t.shape[1:])

    xs, ps, target = shard(x), shard(shard(p)), shard(loss_target)
    T, D = xs.shape
    small = {k: ws[k] for k in SMALL}

    def local(group, k):
        t = shard(group[k])
        return jnp.swapaxes(t, 0, 1) if k in COLUMN_SHARDED else t

    shards = {k: local(ws, k) for k in BIG}

    cast = dict(zip(BIG, _cast_bf16([shards[k] for k in BIG])))
    buckets_a = _bucket_tiles(PATTERNS_A)
    buckets_b = _bucket_tiles(PATTERNS_B)
    bias_a, _ = _bias_build(small["rel_bias"], buckets_a, 0, "bias_build_a")
    bias_b, (w_gu1, w_down1) = _bias_build(
        small["rel_bias"], buckets_b, N_HEAD_GROUP, "bias_build_b",
        side=("relay_gather", [cast["ffn1_w_gu"], cast["ffn1_w_down"]]))
    w_down1 = w_down1.reshape(-1, D)
    a_cfg = dict(patterns=PATTERNS_A, qcol=Q_A_COL, kcol=K_A_COL, vcol=V_A_COL, shared_kv=True)
    b_cfg = dict(patterns=PATTERNS_B, qcol=Q_B_COL, kcol=K_B_COL, vcol=V_B_COL, shared_kv=False)

    (h1, f1, a1, gu1), (w_in_g, w_down2) = _ffn_fwd(
        xs, small["ffn1_pre_g"], small["ffn1_post_g"], w_gu1, w_down1, "ffn1_fwd",
        side=("relay_gather", [cast["w_in"], cast["ffn2_w_down"]]))
    w_in_full = w_in_g.reshape(D_IN, D)
    w_down2 = w_down2.reshape(-1, D)
    (z, a2), (w_out_g,) = _inproj_fwd(h1, small["attn_pre_g"], w_in_full, small["b_in"],
                                      side=("relay_gather", [cast["w_out"]]))
    w_out_full = w_out_g.reshape(-1, D)
    (mix_a, lse_a), (w_gate, w_proj) = _attn_fwd(
        z, bias_a, small["sinks"], name="attn_a_fwd", **a_cfg,
        side=("relay_gather", [cast["w_ple_gate"], cast["w_ple_proj"]]))
    w_gate = w_gate.reshape(-1, D)
    (mix_b, lse_b), (w_gu2,) = _attn_fwd(
        z, bias_b, None, name="attn_b_fwd", **b_cfg, side=("relay_gather", [cast["ffn2_w_gu"]]))
    att, h2, mix = _outproj_fwd(mix_a, mix_b, w_out_full, small["b_out"], small["attn_post_g"], h1)
    (h3, f2, a3, gu2), _ = _ffn_fwd(h2, small["ffn2_pre_g"], small["ffn2_post_g"], w_gu2, w_down2, "ffn2_fwd")
    a4, dpre, de, dh3, loss, dg_ple_post, dg_ple_pre = _ple_fwd_loss(
        h3, small["ple_pre_g"], w_gate, ps, w_proj, small["ple_post_g"], target)

    d_gate = _dw_rows(a4, dpre, "ple_dw_gate", min(256, D))
    d_proj = _ple_dw_proj(ps, de, N_DEV)
    landed = {}
    (dh2, df2, hh2, dgu2, dg_f2_post, dg_f2_pre), (landed["w_ple_gate"], landed["w_ple_proj"]) = _ffn_bwd(
        dh3, f2, small["ffn2_post_g"], h2, small["ffn2_pre_g"], gu2, w_gu2, w_down2, "ffn2_bwd",
        side=("exchange", [d_gate, d_proj]))
    d_gu2 = _dw_gu(a3, dgu2, "ffn2_dw_gu")
    d_down2 = _dw_down(hh2, df2, "ffn2_dw_down").reshape(N_DEV, -1, D)
    dmix_a, dmix_b, datt, dg_attn_post, db_out = _outproj_bwd(dh2, att, small["attn_post_g"], w_out_full)
    d_out = _dw_rows(mix, datt, "attn_dw_out", 256)
    (dqa, dka, dva, ds_a, dsinks), _ = _attn_bwd(
        z, bias_a, small["sinks"], dmix_a, mix_a, lse_a, name="attn_a_bwd", **a_cfg)
    (dqb, dkb, dvb, ds_b), (landed["ffn2_w_gu"],) = _attn_bwd(
        z, bias_b, None, dmix_b, mix_b, lse_b, name="attn_b_bwd", **b_cfg, side=("exchange", [d_gu2]))
    (dh1, dz, db_in, dg_attn_pre), (landed["w_out"],) = _inproj_bwd(
        dqa, dka, dva, dqb, dkb, dvb, w_in_full, h1, small["attn_pre_g"], dh2, side=("exchange", [d_out]))
    cols = D_IN // 3
    d_in = _tn_matmul(
        dz, a2, pl.BlockSpec((DW_TILE, cols), lambda b, t: (t, b)), _tok(D),
        jax.ShapeDtypeStruct((D_IN, D), BF16), pl.BlockSpec((cols, D), lambda b, t: (b, 0)),
        3, T // DW_TILE, (cols, D), "attn_dw_in").reshape(N_DEV, D_IN // N_DEV, D)
    (grad_x, df1, hh1, dgu1, dg_f1_post, dg_f1_pre), (landed["w_in"], landed["ffn2_w_down"]) = _ffn_bwd(
        dh1, f1, small["ffn1_post_g"], xs, small["ffn1_pre_g"], gu1, w_gu1, w_down1, "ffn1_bwd",
        side=("exchange", [d_in, d_down2]))
    d_down1 = _dw_down(hh1, df1, "ffn1_dw_down").reshape(N_DEV, -1, D)
    d_gu1, (landed["ffn1_w_down"],) = _dw_gu(a1, dgu1, "ffn1_dw_gu", side=("exchange", [d_down1]))

    rb_a = _bias_grad(ds_a, buckets_a, "bias_grad_a")
    rb_b = _bias_grad(ds_b, buckets_b, "bias_grad_b").reshape(len(PATTERNS_B), N_HEAD_GROUP, NUM_BUCKETS)
    d_rel_bias = jnp.concatenate([rb_a.T, jnp.sum(rb_b, axis=0).T], axis=1)
    small_grads = {"ffn1_pre_g": dg_f1_pre, "ffn1_post_g": dg_f1_post, "attn_pre_g": dg_attn_pre,
                   "attn_post_g": dg_attn_post, "ffn2_pre_g": dg_f2_pre, "ffn2_post_g": dg_f2_post,
                   "ple_pre_g": dg_ple_pre, "ple_post_g": dg_ple_post, "b_out": db_out, "b_in": db_in,
                   "sinks": dsinks, "rel_bias": d_rel_bias}
    d_gu1_pairs = _pair_add(d_gu1, _pair_swap_call(d_gu1), "ffn1_dw_gu_pair_add")
    landed["ffn1_w_gu"], small_gathered = _final_exchange(d_gu1_pairs, small_grads, loss)

    result = {}
    for k in BIG:
        outs = _sum_adamw(landed[k], shards[k], local(ms, k), local(vs, k), _adamw_rows(shards[k].shape[0]),
                          k + "_adamw")
        if k in COLUMN_SHARDED:
            outs = [jnp.swapaxes(o, 0, 1) for o in outs]
        result[k] = [o.reshape(ws[k].shape) for o in outs]
    small_res, loss_all = _small_adamw(
        small_gathered, small, {k: ms[k] for k in SMALL}, {k: vs[k] for k in SMALL})
    result.update(small_res)

    out = [loss_all[0, 0], grad_x.reshape(x.shape)]
    for i in range(4):
        out += [result[k][i] for k in WEIGHTS]
    return tuple(out)
```

```python
import functools
import math

import numpy as np
import jax
import jax.numpy as jnp
from jax import lax
from jax.experimental import pallas as pl
from jax.experimental.pallas import tpu as pltpu

F32 = jnp.float32
BF16 = jnp.bfloat16
MESH = pl.DeviceIdType.MESH

N_DEV = 8
EPS = 1e-6
NEG_INF = -1e30
HEAD_DIM = 64
LANES = 128
QBLK = 128
D_IN = 2304
A_Q, A_KV, B_W = 512, 128, 512
N_HEAD_GROUP = 8
NUM_BUCKETS = 32
MAX_DISTANCE = 2048
PATTERNS_A = ((1, 127),)
PATTERNS_B = ((1, 128), (4, 128), (16, 128))
Q_A_COL, K_A_COL, V_A_COL = 0, 4, 5
Q_B_COL, K_B_COL, V_B_COL = 6, 10, 14

ADAM_LR, ADAM_B1, ADAM_B2, ADAM_EPS, ADAM_WD, ADAM_STEP = 0.001, 0.9, 0.999, 1e-08, 0.01, 10

TOKEN_TILE = 512
DW_TILE = 1024
FWD_BLOCKS = 4
BWD_BLOCKS = 4
VMEM_LIMIT = 56 * 1024 * 1024
ARB = "arbitrary"

BIG = ("ffn1_w_gu", "ffn1_w_down", "w_in", "w_out", "ffn2_w_gu", "ffn2_w_down", "w_ple_gate", "w_ple_proj")
GAINS = ("ffn1_pre_g", "ffn1_post_g", "attn_pre_g", "attn_post_g", "ffn2_pre_g", "ffn2_post_g",
         "ple_pre_g", "ple_post_g", "b_out")
SMALL = GAINS + ("b_in", "sinks", "rel_bias")
WEIGHTS = ("rel_bias", "ffn1_pre_g", "ffn1_w_gu", "ffn1_w_down", "ffn1_post_g", "attn_pre_g", "w_in", "b_in",
           "sinks", "w_out", "b_out", "attn_post_g", "ffn2_pre_g", "ffn2_w_gu", "ffn2_w_down", "ffn2_post_g",
           "ple_pre_g", "w_ple_gate", "w_ple_proj", "ple_post_g")


def _params(n_axes):
    return pltpu.CompilerParams(dimension_semantics=(ARB,) * n_axes, vmem_limit_bytes=VMEM_LIMIT)


def _mm(a, b):
    return jnp.dot(a, b, preferred_element_type=F32)


def _mm_nt(a, b):
    return lax.dot_general(a, b, (((1,), (1,)), ((), ())), preferred_element_type=F32)


def _mm_tn(a, b):
    return lax.dot_general(a, b, (((0,), (0,)), ((), ())), preferred_element_type=F32)


def _rstd(x):
    return lax.rsqrt(jnp.mean(x * x, axis=-1, keepdims=True) + EPS)


def _rms_bwd(x, r, gain, dy):
    n = x * r
    gdy = dy * gain
    return r * (gdy - n * jnp.mean(gdy * n, axis=-1, keepdims=True)), dy * n


def _colsum(v):
    return jnp.sum(v, axis=0, keepdims=True)


def _full(shape):
    return pl.BlockSpec(shape, lambda *_: (0,) * len(shape))


def _mesh_place():
    return lax.axis_index("x"), lax.axis_index("y"), lax.axis_index("c")


def _slot(dev):
    return 4 * dev[0] + 2 * dev[1] + dev[2]


def _peers(x, y, c):
    out = []
    for flip in range(1, N_DEV):
        dx, dy, dc = (flip >> 2) & 1, (flip >> 1) & 1, flip & 1
        out.append((1 - x if dx else x, 1 - y if dy else y, 1 - c if dc else c))
    return out


def _side_copies(kind, ins, outs, send_sems, recv_sems, local_sems, sem_row=0):
    n = len(ins)
    x, y, c = _mesh_place()
    me = _slot((x, y, c))
    peers = _peers(x, y, c)

    def src(a, block):
        return ins[a] if kind == "gather" else ins[a].at[block]

    def send(a, k, peer):
        return pltpu.make_async_remote_copy(
            src_ref=src(a, _slot(peer)), dst_ref=outs[a].at[me],
            send_sem=send_sems.at[sem_row + a, k], recv_sem=recv_sems.at[sem_row + a, k],
            device_id=peer, device_id_type=MESH)

    def arrival(a, k, peer):
        return pltpu.make_async_remote_copy(
            src_ref=src(a, _slot(peer)), dst_ref=outs[a].at[_slot(peer)],
            send_sem=send_sems.at[sem_row + a, k], recv_sem=recv_sems.at[sem_row + a, k],
            device_id=peer, device_id_type=MESH)

    def own(a):
        return pltpu.make_async_copy(src(a, me), outs[a].at[me], local_sems.at[sem_row + a, 0])

    def start():
        for k, peer in enumerate(peers):
            for a in range(n):
                send(a, k, peer).start()
        for a in range(n):
            own(a).start()

    def wait():
        for k, peer in enumerate(peers):
            for a in range(n):
                arrival(a, k, peer).wait_recv()
        for k, peer in enumerate(peers):
            for a in range(n):
                send(a, k, peer).wait_send()
        for a in range(n):
            own(a).wait()

    return start, None, wait


N_CHIPS = N_DEV // 2


def _pair_swap(ins, received, send_sems, recv_sems):
    n = len(ins)
    x, y, c = _mesh_place()
    sibling = (x, y, 1 - c)

    def send(a, q):
        return pltpu.make_async_remote_copy(
            src_ref=ins[a].at[2 * q + (1 - c)], dst_ref=received[a].at[q],
            send_sem=send_sems.at[a, q], recv_sem=recv_sems.at[a, q], device_id=sibling, device_id_type=MESH)

    def start():
        for a in range(n):
            for q in range(N_CHIPS):
                send(a, q).start()

    def wait():
        for a in range(n):
            for q in range(N_CHIPS):
                send(a, q).wait_recv()
        for a in range(n):
            for q in range(N_CHIPS):
                send(a, q).wait_send()

    return start, None, wait


def _quad_exchange(ins, outs, send_sems, recv_sems, local_sems, sem_row=0):
    n = len(ins)
    x, y, c = _mesh_place()
    mine = 2 * x + y
    chips = [(1 - x, y), (x, 1 - y), (1 - x, 1 - y)]

    def send(a, k, chip):
        return pltpu.make_async_remote_copy(
            src_ref=ins[a].at[2 * chip[0] + chip[1]], dst_ref=outs[a].at[mine],
            send_sem=send_sems.at[sem_row + a, k], recv_sem=recv_sems.at[sem_row + a, k],
            device_id=(chip[0], chip[1], c), device_id_type=MESH)

    def arrival(a, k, chip):
        return pltpu.make_async_remote_copy(
            src_ref=ins[a].at[2 * chip[0] + chip[1]], dst_ref=outs[a].at[2 * chip[0] + chip[1]],
            send_sem=send_sems.at[sem_row + a, k], recv_sem=recv_sems.at[sem_row + a, k],
            device_id=(chip[0], chip[1], c), device_id_type=MESH)

    def own(a):
        return pltpu.make_async_copy(ins[a].at[mine], outs[a].at[mine], local_sems.at[sem_row + a, 0])

    def start():
        for k, chip in enumerate(chips):
            for a in range(n):
                send(a, k, chip).start()
        for a in range(n):
            own(a).start()

    def wait():
        for k, chip in enumerate(chips):
            for a in range(n):
                arrival(a, k, chip).wait_recv()
        for k, chip in enumerate(chips):
            for a in range(n):
                send(a, k, chip).wait_send()
        for a in range(n):
            own(a).wait()

    return start, None, wait


def _relay_gather(ins, outs, send_sems, recv_sems, local_sems):
    n = len(ins)
    x, y, c = _mesh_place()
    me, sibling = (x, y, c), (x, y, 1 - c)
    chips = [(1 - x, y), (x, 1 - y), (1 - x, 1 - y)]

    def copy(a, k, block, to, src=None):
        dst = outs[a].at[_slot(block)]
        return pltpu.make_async_remote_copy(
            src_ref=dst if src is None else src, dst_ref=dst,
            send_sem=send_sems.at[a, k], recv_sem=recv_sems.at[a, k], device_id=to, device_id_type=MESH)

    def own(a):
        return pltpu.make_async_copy(ins[a], outs[a].at[_slot(me)], local_sems.at[a, 0])

    def start():
        for j, chip in enumerate(chips):
            for a in range(n):
                copy(a, 1 + j, me, (*chip, c), src=ins[a]).start()
        for a in range(n):
            copy(a, 0, me, sibling, src=ins[a]).start()
            own(a).start()

    def relay():
        for j, chip in enumerate(chips):
            for a in range(n):
                copy(a, 1 + j, (*chip, c), me).wait_recv()
                copy(a, 4 + j, (*chip, c), sibling).start()

    def wait():
        for a in range(n):
            copy(a, 0, sibling, me).wait_recv()
        for j, chip in enumerate(chips):
            for a in range(n):
                copy(a, 4 + j, (*chip, 1 - c), me).wait_recv()
        for j, chip in enumerate(chips):
            for a in range(n):
                copy(a, 1 + j, me, (*chip, c), src=ins[a]).wait_send()
                copy(a, 4 + j, (*chip, c), sibling).wait_send()
        for a in range(n):
            copy(a, 0, me, sibling, src=ins[a]).wait_send()
            own(a).wait()

    return start, relay, wait


def _side_out_shapes(kind, arrays):
    if kind in ("gather", "relay_gather"):
        return [jax.ShapeDtypeStruct((N_DEV,) + a.shape, a.dtype) for a in arrays]
    if kind == "pair_swap":
        return [jax.ShapeDtypeStruct((N_CHIPS,) + a.shape[1:], a.dtype) for a in arrays]
    return [jax.ShapeDtypeStruct(a.shape, a.dtype) for a in arrays]


def _hosted_call(body, name, grid, in_specs, out_specs, out_shape, scratch_shapes, args, side=None):
    if side is None:
        outs = pl.pallas_call(
            body, name=name, grid=grid, in_specs=in_specs, out_specs=out_specs, out_shape=out_shape,
            scratch_shapes=scratch_shapes, compiler_params=_params(len(grid)))(*args)
        return outs, []
    kind, arrays = side
    side_shapes = _side_out_shapes(kind, arrays)
    n_in, n_out, n_scr, n_side = len(in_specs), len(out_specs), len(scratch_shapes), len(arrays)

    def hosted(*refs):
        pos = 0
        groups = []
        for size in (n_in, n_side, n_out, len(side_shapes), n_scr):
            groups.append(refs[pos:pos + size])
            pos += size
        ins, side_in, outs, side_out, scr = groups
        send_sems, recv_sems, local_sems = refs[pos:]
        ids = [pl.program_id(d) for d in range(len(grid))]
        is_first = functools.reduce(jnp.logical_and, [i == 0 for i in ids])
        is_last = functools.reduce(jnp.logical_and, [i == g - 1 for i, g in zip(ids, grid)])
        if kind == "relay_gather":
            start, relay, wait = _relay_gather(side_in, side_out, send_sems, recv_sems, local_sems)
        elif kind == "pair_swap":
            start, relay, wait = _pair_swap(side_in, side_out, send_sems, recv_sems)
        elif kind == "quad_exchange":
            start, relay, wait = _quad_exchange(side_in, side_out, send_sems, recv_sems, local_sems)
        else:
            start, relay, wait = _side_copies(kind, side_in, side_out, send_sems, recv_sems, local_sems)
        pl.when(is_first)(start)
        if relay is not None:
            pl.when(is_last)(relay)
        body(*ins, *outs, *scr)
        pl.when(is_last)(wait)

    any_spec = pl.BlockSpec(memory_space=pl.ANY)
    outs = pl.pallas_call(
        hosted, name=name, grid=grid,
        in_specs=list(in_specs) + [any_spec] * n_side,
        out_specs=list(out_specs) + [any_spec] * len(side_shapes),
        out_shape=list(out_shape) + side_shapes,
        scratch_shapes=list(scratch_shapes) + [pltpu.SemaphoreType.DMA((n_side, 7)), pltpu.SemaphoreType.DMA((n_side, 7)),
                                               pltpu.SemaphoreType.DMA((n_side, N_CHIPS))],
        compiler_params=_params(len(grid)))(*args, *arrays)
    return outs[:n_out], outs[n_out:]


def _lane_chunks(width, chunk=2 * LANES):
    return [slice(n0, min(n0 + chunk, width)) for n0 in range(0, width, chunk)]


def _pipelined(chunks, first, middle, last):
    n = len(chunks)
    a, b, total = {}, {}, None
    for step in range(n + 2):
        if step < n:
            a[step] = first(chunks[step])
        if 0 <= step - 1 < n:
            b[step - 1] = middle(chunks[step - 1], a.pop(step - 1))
        if 0 <= step - 2 < n:
            part = last(chunks[step - 2], b.pop(step - 2))
            total = part if total is None else total + part
    return total


def _ffn_fwd(h, g_pre, g_post, w_gu, w_down, name, side=None, attn=None):
    T, D = h.shape
    nj = w_gu.shape[0] // 2
    FB = w_gu.shape[1]
    tm = TOKEN_TILE
    n_attn = 0 if attn is None else 5

    def body(*refs):
        h_ref, gpre_ref, gpost_ref, wg_ref, wu_ref, wd_ref = refs[:6]
        hout_ref, f_ref, a_ref, gu_ref = refs[6 + n_attn:10 + n_attn]
        a_scr, acc = refs[-2:]
        j = pl.program_id(1)

        @pl.when(j == 0)
        def _():
            if attn is None:
                x = h_ref[...]
            else:
                ma_ref, mb_ref, wo_ref, bo_ref, ga_ref = refs[6:11]
                att_ref, hmid_ref, mix_ref = refs[15:18]
                mix = jnp.concatenate([ma_ref[...], mb_ref[...]], axis=1).astype(BF16)
                mix_ref[...] = mix
                att = _mm(mix, wo_ref[...]) + bo_ref[...]
                att_ref[...] = att
                x = h_ref[...] + att * _rstd(att) * ga_ref[...]
                hmid_ref[...] = x
            a = (x * _rstd(x) * gpre_ref[...]).astype(BF16)
            a_scr[...] = a
            a_ref[...] = a
            acc[...] = jnp.zeros_like(acc)

        a = a_scr[...]
        g = _mm_nt(a, wg_ref[...])
        u = _mm_nt(a, wu_ref[...])
        gu_ref[0] = g.astype(BF16)
        gu_ref[1] = u.astype(BF16)
        hh = (g * jax.nn.sigmoid(g) * u).astype(BF16)
        acc[...] += _mm(hh, wd_ref[...])

        @pl.when(j == nj - 1)
        def _():
            f = acc[...]
            f_ref[...] = f
            x = h_ref[...] if attn is None else refs[16][...]
            hout_ref[...] = x + 0.5 * (f * _rstd(f) * gpost_ref[...])

    tile = pl.BlockSpec((tm, D), lambda i, j: (i, 0))
    in_specs = [tile, _full((1, D)), _full((1, D)),
                pl.BlockSpec((None, FB, D), lambda i, j: (j, 0, 0)),
                pl.BlockSpec((None, FB, D), lambda i, j: (j + nj, 0, 0)),
                pl.BlockSpec((FB, D), lambda i, j: (j, 0))]
    out_specs = [tile, tile, tile, pl.BlockSpec((None, 2, tm, FB), lambda i, j: (j, 0, i, 0))]
    out_shape = [
        jax.ShapeDtypeStruct((T, D), F32),
        jax.ShapeDtypeStruct((T, D), F32),
        jax.ShapeDtypeStruct((T, D), BF16),
        jax.ShapeDtypeStruct((nj, 2, T, FB), BF16),
    ]
    args = [h, g_pre, g_post, w_gu, w_gu, w_down]
    if attn is not None:
        mix_a, mix_b, w_out, b_out, g_attn = attn
        d_mix = w_out.shape[0]
        in_specs += [pl.BlockSpec((tm, mix_a.shape[1]), lambda i, j: (i, 0)),
                     pl.BlockSpec((tm, mix_b.shape[1]), lambda i, j: (i, 0)),
                     _full((d_mix, D)), _full((1, D)), _full((1, D))]
        out_specs += [tile, tile, pl.BlockSpec((tm, d_mix), lambda i, j: (i, 0))]
        out_shape += [jax.ShapeDtypeStruct((T, D), F32),
                      jax.ShapeDtypeStruct((T, D), F32),
                      jax.ShapeDtypeStruct((T, d_mix), BF16)]
        args += [mix_a, mix_b, w_out, b_out, g_attn]
    return _hosted_call(
        body, name, (T // tm, nj), in_specs=in_specs, out_specs=out_specs, out_shape=out_shape,
        scratch_shapes=[pltpu.VMEM((tm, D), BF16), pltpu.VMEM((tm, D), F32)], args=args, side=side)


def _ffn_bwd(dh_out, f, g_post, h, g_pre, gu, w_gu, w_down, name, side=None):
    T, D = h.shape
    nj = w_gu.shape[0] // 2
    FB = w_gu.shape[1]
    tm = TOKEN_TILE

    def body(dho_ref, f_ref, gpost_ref, h_ref, gpre_ref, gu_ref, wg_ref, wu_ref, wd_ref,
             dhin_ref, df_ref, hh_ref, dgu_ref, dgpost_ref, dgpre_ref, df_scr, da):
        i, j = pl.program_id(0), pl.program_id(1)

        @pl.when(jnp.logical_and(i == 0, j == 0))
        def _():
            dgpost_ref[...] = jnp.zeros_like(dgpost_ref)
            dgpre_ref[...] = jnp.zeros_like(dgpre_ref)

        @pl.when(j == 0)
        def _():
            fv = f_ref[...]
            df, dgain = _rms_bwd(fv, _rstd(fv), gpost_ref[...], 0.5 * dho_ref[...])
            dgpost_ref[...] += _colsum(dgain)
            dfb = df.astype(BF16)
            df_scr[...] = dfb
            df_ref[...] = dfb
            da[...] = jnp.zeros_like(da)

        dfb = df_scr[...]

        halves = (slice(0, tm // 2), slice(tm // 2, tm))

        def hidden_grad(c):
            return [_mm_nt(dfb[rows], wd_ref[c, :]) for rows in halves]

        def through_swiglu(c, dhh):
            dhh = jnp.concatenate(dhh, axis=0)
            g = gu_ref[0, :, c].astype(F32)
            u = gu_ref[1, :, c].astype(F32)
            sg = jax.nn.sigmoid(g)
            silu = g * sg
            hh_ref[:, c] = (silu * u).astype(BF16)
            dg = (dhh * u * (sg * (1.0 + (g - silu)))).astype(BF16)
            du = (dhh * silu).astype(BF16)
            dgu_ref[0, :, c] = dg
            dgu_ref[1, :, c] = du
            return dg, du

        def input_grad(c, dgu):
            return jnp.concatenate(
                [_mm(dgu[0][rows], wg_ref[c, :]) + _mm(dgu[1][rows], wu_ref[c, :]) for rows in halves], axis=0)

        da[...] += _pipelined(_lane_chunks(FB), hidden_grad, through_swiglu, input_grad)

        @pl.when(j == nj - 1)
        def _():
            x = h_ref[...]
            dx, dgain = _rms_bwd(x, _rstd(x), gpre_ref[...], da[...])
            dgpre_ref[...] += _colsum(dgain)
            dhin_ref[...] = dho_ref[...] + dx

    tile = pl.BlockSpec((tm, D), lambda i, j: (i, 0))
    return _hosted_call(
        body, name, (T // tm, nj),
        in_specs=[
            tile, tile, _full((1, D)), tile, _full((1, D)),
            pl.BlockSpec((None, 2, tm, FB), lambda i, j: (j, 0, i, 0)),
            pl.BlockSpec((None, FB, D), lambda i, j: (j, 0, 0)),
            pl.BlockSpec((None, FB, D), lambda i, j: (j + nj, 0, 0)),
            pl.BlockSpec((FB, D), lambda i, j: (j, 0)),
        ],
        out_specs=[
            tile, tile,
            pl.BlockSpec((None, tm, FB), lambda i, j: (j, i, 0)),
            pl.BlockSpec((None, 2, tm, FB), lambda i, j: (j, 0, i, 0)),
            _full((1, D)), _full((1, D)),
        ],
        out_shape=[
            jax.ShapeDtypeStruct((T, D), F32),
            jax.ShapeDtypeStruct((T, D), BF16),
            jax.ShapeDtypeStruct((nj, T, FB), BF16),
            jax.ShapeDtypeStruct((nj, 2, T, FB), BF16),
            jax.ShapeDtypeStruct((1, D), F32),
            jax.ShapeDtypeStruct((1, D), F32),
        ],
        scratch_shapes=[pltpu.VMEM((tm, D), BF16), pltpu.VMEM((tm, D), F32)],
        args=(dh_out, f, g_post, h, g_pre, gu, w_gu, w_gu, w_down), side=side)


def _tn_matmul(x, y, x_spec, y_spec, out_shape, out_spec, n_blocks, n_steps, acc_shape, name, side=None):
    def body(x_ref, y_ref, o_ref, acc):
        t = pl.program_id(1)

        @pl.when(t == 0)
        def _():
            acc[...] = jnp.zeros_like(acc)

        acc[...] += _mm_tn(x_ref[...].astype(BF16), y_ref[...].astype(BF16))

        @pl.when(t == n_steps - 1)
        def _():
            o_ref[...] = acc[...].astype(o_ref.dtype)

    outs, side_outs = _hosted_call(
        body, name, (n_blocks, n_steps), in_specs=[x_spec, y_spec], out_specs=[out_spec], out_shape=[out_shape],
        scratch_shapes=[pltpu.VMEM(acc_shape, F32)], args=(x, y), side=side)
    return (outs[0], side_outs) if side is not None else outs[0]


def _inproj_fwd(h, g_pre, w_in, b_in, side=None):
    T, D = h.shape
    tm = TOKEN_TILE

    def body(h_ref, g_ref, w_ref, b_ref, z_ref, a_ref):
        x = h_ref[...]
        a = (x * _rstd(x) * g_ref[...]).astype(BF16)
        a_ref[...] = a
        z_ref[...] = _mm_nt(a, w_ref[...]) + b_ref[...]

    return _hosted_call(
        body, "inproj_fwd", (T // tm,),
        in_specs=[pl.BlockSpec((tm, D), lambda i: (i, 0)), _full((1, D)), _full((D_IN, D)), _full((1, D_IN))],
        out_specs=[pl.BlockSpec((tm, D_IN), lambda i: (i, 0)), pl.BlockSpec((tm, D), lambda i: (i, 0))],
        out_shape=[jax.ShapeDtypeStruct((T, D_IN), F32), jax.ShapeDtypeStruct((T, D), BF16)],
        scratch_shapes=[], args=(h, g_pre, w_in, b_in), side=side)


def _inproj_bwd(dqa, dka, dva, dqb, dkb, dvb, w_in, h, g_pre, dres, side=None):
    T, D = h.shape
    tm = TOKEN_TILE

    def body(dqa_ref, dka_ref, dva_ref, dqb_ref, dkb_ref, dvb_ref, w_ref, h_ref, g_ref, dres_ref,
             dh_ref, dz_ref, dbin_ref, dg_ref):
        i = pl.program_id(0)

        @pl.when(i == 0)
        def _():
            dbin_ref[...] = jnp.zeros_like(dbin_ref)
            dg_ref[...] = jnp.zeros_like(dg_ref)

        dz = jnp.concatenate([dqa_ref[...], dka_ref[...], dva_ref[...], dqb_ref[...], dkb_ref[...], dvb_ref[...]],
                             axis=1)
        dbin_ref[...] += _colsum(dz)
        dzb = dz.astype(BF16)
        dz_ref[...] = dzb
        da = _mm(dzb, w_ref[...])
        x = h_ref[...]
        dx, dgain = _rms_bwd(x, _rstd(x), g_ref[...], da)
        dg_ref[...] += _colsum(dgain)
        dh_ref[...] = dres_ref[...] + dx

    def tile(w):
        return pl.BlockSpec((tm, w), lambda i: (i, 0))

    return _hosted_call(
        body, "inproj_bwd", (T // tm,),
        in_specs=[tile(A_Q), tile(A_KV), tile(A_KV), tile(B_W), tile(B_W), tile(B_W),
                  _full((D_IN, D)), tile(D), _full((1, D)), tile(D)],
        out_specs=[tile(D), tile(D_IN), _full((1, D_IN)), _full((1, D))],
        out_shape=[jax.ShapeDtypeStruct((T, D), F32), jax.ShapeDtypeStruct((T, D_IN), BF16),
                   jax.ShapeDtypeStruct((1, D_IN), F32), jax.ShapeDtypeStruct((1, D), F32)],
        scratch_shapes=[], args=(dqa, dka, dva, dqb, dkb, dvb, w_in, h, g_pre, dres), side=side)


def _bucket_tiles(patterns):
    i = np.arange(QBLK)[:, None]
    j = np.arange(2 * QBLK)[None, :]
    dist = QBLK + i - j
    max_exact = NUM_BUCKETS // 2
    tiles = []
    for dilation, max_dist in patterns:
        n = np.maximum(dist * dilation, 0)
        nf = np.maximum(n, 1).astype(np.float32)
        large = max_exact + (np.log(nf / np.float32(max_exact)) / np.float32(math.log(MAX_DISTANCE / max_exact))
                             * np.float32(NUM_BUCKETS - max_exact)).astype(np.int32)
        bucket = np.where(n < max_exact, n, np.minimum(large, NUM_BUCKETS - 1))
        tiles.append(np.where((dist >= 0) & (dist <= max_dist), bucket, -1))
    return jnp.asarray(np.stack(tiles).astype(np.int32))


def _bias_build(rel_bias, buckets, head0, name, side=None):
    n = buckets.shape[0]

    def body(bk_ref, rb_ref, o_ref):
        bk = bk_ref[...]
        base = jnp.where(bk < 0, NEG_INF, 0.0).astype(F32)
        for hd in range(N_HEAD_GROUP):
            o_ref[hd] = lax.fori_loop(
                0, NUM_BUCKETS, lambda b, acc, hd=hd: jnp.where(bk == b, rb_ref[b, head0 + hd], acc), base)

    outs, side_outs = _hosted_call(
        body, name, (n,),
        in_specs=[pl.BlockSpec((None, QBLK, 2 * QBLK), lambda p: (p, 0, 0)), pl.BlockSpec(memory_space=pltpu.SMEM)],
        out_specs=[pl.BlockSpec((None, N_HEAD_GROUP, QBLK, 2 * QBLK), lambda p: (p, 0, 0, 0))],
        out_shape=[jax.ShapeDtypeStruct((n, N_HEAD_GROUP, QBLK, 2 * QBLK), F32)],
        scratch_shapes=[], args=(buckets, rel_bias), side=side)
    return outs[0], side_outs


def _bias_grad(ds, buckets, name):
    n = buckets.shape[0]

    def body(ds_ref, bk_ref, o_ref):
        bk = bk_ref[...]
        row = lax.broadcasted_iota(jnp.int32, (NUM_BUCKETS, 2 * QBLK), 0)
        for hd in range(N_HEAD_GROUP):
            d = ds_ref[hd]
            per_key = jnp.zeros((NUM_BUCKETS, 2 * QBLK), F32)
            for b in range(NUM_BUCKETS):
                per_key = jnp.where(row == b, jnp.sum(jnp.where(bk == b, d, 0.0), axis=0, keepdims=True), per_key)
            o_ref[hd] = jnp.broadcast_to(jnp.sum(per_key, axis=1, keepdims=True), (NUM_BUCKETS, LANES))

    out = pl.pallas_call(
        body, name=name, grid=(n,),
        in_specs=[pl.BlockSpec((None, N_HEAD_GROUP, QBLK, 2 * QBLK), lambda p: (p, 0, 0, 0)),
                  pl.BlockSpec((None, QBLK, 2 * QBLK), lambda p: (p, 0, 0))],
        out_specs=pl.BlockSpec((None, N_HEAD_GROUP, NUM_BUCKETS, LANES), lambda p: (p, 0, 0, 0)),
        out_shape=jax.ShapeDtypeStruct((n, N_HEAD_GROUP, NUM_BUCKETS, LANES), F32),
        compiler_params=_params(1),
    )(ds, buckets)
    return out[:, :, :, 0].reshape(n * N_HEAD_GROUP, NUM_BUCKETS)


def _class_rows(start, dilation):
    if dilation == 1:
        return pl.ds(pl.multiple_of(start, QBLK), QBLK)
    return pl.ds(start, QBLK, stride=dilation)


def _starts_class(u, blocks_per_pass, n_blocks):
    return blocks_per_pass % n_blocks == 0 and u % n_blocks == 0


def _block_starts(idx, n_blocks, dilation):
    cls = idx // n_blocks
    n = idx % n_blocks
    cur = cls + dilation * QBLK * n
    prev = cls + dilation * QBLK * jnp.maximum(n - 1, 0)
    return n, cur, prev


class _HeadPair:
    def __init__(self, g, shared_kv):
        self.lane = lax.broadcasted_iota(jnp.int32, (1, LANES), 1)
        self.lower = self.lane < HEAD_DIM
        self.shared_kv = shared_kv
        self.key_lanes = (self.lane >= HEAD_DIM).astype(jnp.int32) == (g // 2)

    def stack(self, t):
        return jnp.concatenate([jnp.where(self.lower, t, 0.0), jnp.where(self.lower, 0.0, t)], axis=0).astype(BF16)

    def unstack(self, t2):
        return jnp.where(self.lower, t2[:QBLK], t2[QBLK:])

    def keys(self, t):
        if self.shared_kv:
            return jnp.where(self.key_lanes, t, pltpu.roll(t, HEAD_DIM, 1))
        return t

    def key_grads(self, t):
        if self.shared_kv:
            return jnp.where(self.key_lanes, t + pltpu.roll(t, HEAD_DIM, 1), 0.0)
        return t


def _attn_specs(T, qcol, kcol, vcol, shared_kv):
    kv = (lambda c: (lambda g: (0, c))) if shared_kv else (lambda c: (lambda g: (0, c + g)))
    return [pl.BlockSpec((T, LANES), lambda g: (0, qcol + g)),
            pl.BlockSpec((T, LANES), kv(kcol)),
            pl.BlockSpec((T, LANES), kv(vcol))]


def _attn_fwd(z, bias, sinks, patterns, qcol, kcol, vcol, shared_kv, name, side=None):
    T = z.shape[0]
    n_pat = len(patterns)
    has_sink = sinks is not None

    def body(*refs):
        if has_sink:
            sink_ref, refs = refs[0], refs[1:]
        q_ref, k_ref, v_ref, b_ref, o_ref, l_ref = refs[:6]
        po_scr = refs[6:6 + n_pat]
        pl_scr = refs[6 + n_pat:]
        g = pl.program_id(0)
        heads = _HeadPair(g, shared_kv)
        in_prev = lax.broadcasted_iota(jnp.int32, (2 * QBLK, 2 * QBLK), 1) < QBLK

        for pi, (dilation, _) in enumerate(patterns):
            n_blocks = T // (QBLK * dilation)

            def step(it, carry, pi=pi, dilation=dilation, n_blocks=n_blocks):
                blocks = []
                for u in range(FWD_BLOCKS):
                    n, cur, prev = _block_starts(it * FWD_BLOCKS + u, n_blocks, dilation)
                    rows_c, rows_p = _class_rows(cur, dilation), _class_rows(prev, dilation)
                    qm = heads.stack(q_ref[rows_c, :])
                    k_cur, v_cur = k_ref[rows_c, :], v_ref[rows_c, :]
                    no_past = _starts_class(u, FWD_BLOCKS, n_blocks)
                    if no_past:
                        k2, v2 = k_cur, v_cur
                    else:
                        if u % min(FWD_BLOCKS, n_blocks) == 0:
                            k_prev, v_prev = k_ref[rows_p, :], v_ref[rows_p, :]
                        k2 = jnp.concatenate([k_prev, k_cur], axis=0)
                        v2 = jnp.concatenate([v_prev, v_cur], axis=0)
                    k2, v2 = heads.keys(k2).astype(BF16), heads.keys(v2).astype(BF16)
                    k_prev, v_prev = k_cur, v_cur
                    blocks.append(dict(n=n, no_past=no_past, rows=rows_c, v2=v2, s=_mm_nt(qm, k2)))
                for b in blocks:
                    if b["no_past"]:
                        b["s"] = b["s"] * (HEAD_DIM ** -0.5) + b_ref[pi, :, QBLK:]
                    else:
                        s = b["s"] * (HEAD_DIM ** -0.5) + b_ref[pi]
                        b["s"] = jnp.where(jnp.logical_and(in_prev, b["n"] == 0), NEG_INF, s)
                    b["m"] = jnp.max(b["s"], axis=1, keepdims=True)
                for b in blocks:
                    b["pr"] = jnp.exp(b["s"] - b["m"])
                    b["den"] = jnp.sum(b["pr"], axis=1, keepdims=True)
                for b in blocks:
                    b["o2"] = _mm(b["pr"].astype(BF16), b["v2"])
                for b in blocks:
                    lse = b["m"] + jnp.log(b["den"])
                    po_scr[pi][b["rows"], :] = heads.unstack(b["o2"] / b["den"])
                    pl_scr[2 * pi][b["rows"], :] = jnp.broadcast_to(lse[:QBLK], (QBLK, LANES))
                    pl_scr[2 * pi + 1][b["rows"], :] = jnp.broadcast_to(lse[QBLK:], (QBLK, LANES))
                return carry

            lax.fori_loop(0, (dilation * n_blocks) // FWD_BLOCKS, step, 0)

        def merge(ci, carry):
            rows = pl.ds(pl.multiple_of(ci * QBLK, QBLK), QBLK)
            weights = []
            for hd in range(2):
                parts = [pl_scr[2 * pi + hd][rows, :] for pi in range(n_pat)]
                m = functools.reduce(jnp.maximum, parts)
                if has_sink:
                    sink = sink_ref[0, 2 * g + hd]
                    m = jnp.maximum(m, sink)
                terms = [jnp.exp(x - m) for x in parts]
                den = functools.reduce(jnp.add, terms)
                if has_sink:
                    den = den + jnp.exp(sink - m)
                l_ref[hd, rows, :] = m + jnp.log(den)
                inv = 1.0 / den
                weights.append([t * inv for t in terms])
            o_ref[rows, :] = functools.reduce(
                jnp.add, [jnp.where(heads.lower, weights[0][pi], weights[1][pi]) * po_scr[pi][rows, :]
                          for pi in range(n_pat)])
            return carry

        lax.fori_loop(0, T // QBLK, merge, 0)

    in_specs = _attn_specs(T, qcol, kcol, vcol, shared_kv)
    in_specs.append(pl.BlockSpec((n_pat, None, 2 * QBLK, 2 * QBLK), lambda g: (0, g, 0, 0)))
    args = [z, z, z, bias.reshape(n_pat, N_HEAD_GROUP // 2, 2 * QBLK, 2 * QBLK)]
    if has_sink:
        in_specs.insert(0, pl.BlockSpec(memory_space=pltpu.SMEM))
        args.insert(0, sinks)
    return _hosted_call(
        body, name, (N_HEAD_GROUP // 2,),
        in_specs=in_specs,
        out_specs=[pl.BlockSpec((T, LANES), lambda g: (0, g)), pl.BlockSpec((2, T, LANES), lambda g: (g, 0, 0))],
        out_shape=[jax.ShapeDtypeStruct((T, N_HEAD_GROUP * HEAD_DIM), F32),
                   jax.ShapeDtypeStruct((N_HEAD_GROUP, T, LANES), F32)],
        scratch_shapes=[pltpu.VMEM((T, LANES), F32)] * (3 * n_pat), args=args, side=side)


def _attn_bwd(z, bias, sinks, d_out, out, lse, patterns, qcol, kcol, vcol, shared_kv, name, side=None):
    T = z.shape[0]
    n_pat = len(patterns)
    has_sink = sinks is not None
    kv_width = LANES if shared_kv else N_HEAD_GROUP * HEAD_DIM

    def body(*refs):
        if has_sink:
            sink_ref, refs = refs[0], refs[1:]
        q_ref, k_ref, v_ref, b_ref, do_ref, o_ref, l0_ref, l1_ref = refs[:8]
        dq_ref, dk_ref, dv_ref, ds_ref = refs[8:12]
        dsink_ref = refs[12] if has_sink else None
        dk_acc, dv_acc = refs[-2:]
        g = pl.program_id(0)
        heads = _HeadPair(g, shared_kv)
        in_prev = lax.broadcasted_iota(jnp.int32, (2 * QBLK, 2 * QBLK), 1) < QBLK

        dq_ref[...] = jnp.zeros_like(dq_ref)
        ds_ref[...] = jnp.zeros_like(ds_ref)
        dk_acc[...] = jnp.zeros_like(dk_acc)
        dv_acc[...] = jnp.zeros_like(dv_acc)

        dsink = jnp.zeros((1, LANES), F32)
        for pi, (dilation, _) in enumerate(patterns):
            n_blocks = T // (QBLK * dilation)

            def step(idx, dsink, pi=pi, dilation=dilation, n_blocks=n_blocks):
                blocks = []
                for u in range(BWD_BLOCKS):
                    n, cur, prev = _block_starts(idx * BWD_BLOCKS + u, n_blocks, dilation)
                    rows_c, rows_p = _class_rows(cur, dilation), _class_rows(prev, dilation)
                    qm = heads.stack(q_ref[rows_c, :])
                    k_cur, v_cur = k_ref[rows_c, :], v_ref[rows_c, :]
                    first = u % min(BWD_BLOCKS, n_blocks) == 0
                    no_past = _starts_class(u, BWD_BLOCKS, n_blocks)
                    if no_past:
                        k2, v2 = k_cur, v_cur
                    else:
                        if first:
                            k_prev, v_prev = k_ref[rows_p, :], v_ref[rows_p, :]
                        k2 = jnp.concatenate([k_prev, k_cur], axis=0)
                        v2 = jnp.concatenate([v_prev, v_cur], axis=0)
                    k2, v2 = heads.keys(k2).astype(BF16), heads.keys(v2).astype(BF16)
                    k_prev, v_prev = k_cur, v_cur
                    d_o = do_ref[rows_c, :]
                    dom = heads.stack(d_o)
                    dd = d_o * o_ref[rows_c, :]
                    delta = jnp.concatenate([jnp.sum(jnp.where(heads.lower, dd, 0.0), axis=1, keepdims=True),
                                             jnp.sum(jnp.where(heads.lower, 0.0, dd), axis=1, keepdims=True)], axis=0)
                    lse = jnp.concatenate([l0_ref[rows_c, :], l1_ref[rows_c, :]], axis=0)
                    blocks.append(dict(n=n, first=first, no_past=no_past, rows_c=rows_c, rows_p=rows_p, qm=qm, k2=k2,
                                       dom=dom, delta=delta, lse=lse, s=_mm_nt(qm, k2), dp=_mm_nt(dom, v2)))
                for b in blocks:
                    if b["no_past"]:
                        s = b["s"] * (HEAD_DIM ** -0.5) + b_ref[pi, :, QBLK:]
                        b["pr"] = jnp.exp(s - b["lse"])
                    else:
                        s = b["s"] * (HEAD_DIM ** -0.5) + b_ref[pi]
                        s = jnp.where(jnp.logical_and(in_prev, b["n"] == 0), NEG_INF, s)
                        b["pr"] = jnp.exp(s - jnp.concatenate([b["lse"], b["lse"]], axis=1))
                    b["ds"] = b["pr"] * (b["dp"] - b["delta"])
                for b in blocks:
                    dsb = b["ds"].astype(BF16)
                    b["dq2"] = _mm(dsb, b["k2"])
                    b["dk2"] = _mm_tn(dsb, b["qm"])
                    b["dv2"] = _mm_tn(b["pr"].astype(BF16), b["dom"])
                for b in blocks:
                    b["dk2"] = heads.key_grads(b["dk2"]) * (HEAD_DIM ** -0.5)
                    b["dv2"] = heads.key_grads(b["dv2"])
                for u, b in enumerate(blocks):
                    dq_ref[b["rows_c"], :] += heads.unstack(b["dq2"]) * (HEAD_DIM ** -0.5)
                    if b["no_past"]:
                        ds_ref[pi, :, QBLK:] += b["ds"]
                        dk_own, dv_own = b["dk2"], b["dv2"]
                    else:
                        ds_ref[pi] += b["ds"]
                        dk_own, dv_own = b["dk2"][QBLK:], b["dv2"][QBLK:]
                    if u + 1 < len(blocks) and not blocks[u + 1]["first"]:
                        dk_own = dk_own + blocks[u + 1]["dk2"][:QBLK]
                        dv_own = dv_own + blocks[u + 1]["dv2"][:QBLK]
                    if b["first"] and not b["no_past"]:
                        dk_acc[b["rows_p"], :] += b["dk2"][:QBLK]
                        dv_acc[b["rows_p"], :] += b["dv2"][:QBLK]
                    dk_acc[b["rows_c"], :] += dk_own
                    dv_acc[b["rows_c"], :] += dv_own
                    if has_sink:
                        for hd in range(2):
                            rows_h = slice(QBLK * hd, QBLK * (hd + 1))
                            p_sink = jnp.exp(sink_ref[0, 2 * g + hd] - b["lse"][rows_h, 0:1])
                            dsink = dsink - jnp.where(heads.lane == 2 * g + hd,
                                                      jnp.sum(p_sink * b["delta"][rows_h]), 0.0)
                return dsink

            dsink = lax.fori_loop(0, (dilation * n_blocks) // BWD_BLOCKS, step, dsink)

        if shared_kv:
            @pl.when(g == 0)
            def _():
                dk_ref[...] = dk_acc[...]
                dv_ref[...] = dv_acc[...]

            @pl.when(g != 0)
            def _():
                dk_ref[...] += dk_acc[...]
                dv_ref[...] += dv_acc[...]
        else:
            dk_ref[...] = dk_acc[...]
            dv_ref[...] = dv_acc[...]

        if has_sink:
            @pl.when(g == 0)
            def _():
                dsink_ref[...] = dsink

            @pl.when(g != 0)
            def _():
                dsink_ref[...] += dsink

    pair = pl.BlockSpec((T, LANES), lambda g: (0, g))
    stacked = pl.BlockSpec((n_pat, None, 2 * QBLK, 2 * QBLK), lambda g: (0, g, 0, 0))
    stacked_shape = (n_pat, N_HEAD_GROUP // 2, 2 * QBLK, 2 * QBLK)
    in_specs = _attn_specs(T, qcol, kcol, vcol, shared_kv)
    in_specs += [stacked, pair, pair,
                 pl.BlockSpec((None, T, LANES), lambda g: (2 * g, 0, 0)),
                 pl.BlockSpec((None, T, LANES), lambda g: (2 * g + 1, 0, 0))]
    args = [z, z, z, bias.reshape(stacked_shape), d_out, out, lse, lse]
    kv_out = _full((T, LANES)) if shared_kv else pair
    out_specs = [pair, kv_out, kv_out, stacked]
    out_shape = [jax.ShapeDtypeStruct((T, N_HEAD_GROUP * HEAD_DIM), F32),
                 jax.ShapeDtypeStruct((T, kv_width), F32), jax.ShapeDtypeStruct((T, kv_width), F32),
                 jax.ShapeDtypeStruct(stacked_shape, F32)]
    if has_sink:
        in_specs.insert(0, pl.BlockSpec(memory_space=pltpu.SMEM))
        args.insert(0, sinks)
        out_specs.append(_full((1, LANES)))
        out_shape.append(jax.ShapeDtypeStruct((1, LANES), F32))
    outs, side_outs = _hosted_call(
        body, name, (N_HEAD_GROUP // 2,), in_specs=in_specs, out_specs=out_specs, out_shape=out_shape,
        scratch_shapes=[pltpu.VMEM((T, LANES), F32), pltpu.VMEM((T, LANES), F32)], args=args, side=side)
    outs = list(outs)
    outs[3] = outs[3].reshape(n_pat, N_HEAD_GROUP, QBLK, 2 * QBLK)
    return outs, side_outs


def _outproj_bwd(dh, att, g_post, w_out):
    T, D = dh.shape
    tm = TOKEN_TILE
    d_mix = w_out.shape[0]

    def body(dh_ref, att_ref, g_ref, w_ref, dma_ref, dmb_ref, datt_ref, dg_ref, db_ref):
        i = pl.program_id(0)

        @pl.when(i == 0)
        def _():
            dg_ref[...] = jnp.zeros_like(dg_ref)
            db_ref[...] = jnp.zeros_like(db_ref)

        att = att_ref[...]
        datt, dgain = _rms_bwd(att, _rstd(att), g_ref[...], dh_ref[...])
        dg_ref[...] += _colsum(dgain)
        db_ref[...] += _colsum(datt)
        dattb = datt.astype(BF16)
        datt_ref[...] = dattb
        dmix = _mm_nt(dattb, w_ref[...])
        dma_ref[...] = dmix[:, :A_Q]
        dmb_ref[...] = dmix[:, A_Q:]

    def tile(w):
        return pl.BlockSpec((tm, w), lambda i: (i, 0))

    return pl.pallas_call(
        body, name="outproj_bwd", grid=(T // tm,),
        in_specs=[tile(D), tile(D), _full((1, D)), _full((d_mix, D))],
        out_specs=[tile(A_Q), tile(B_W), tile(D), _full((1, D)), _full((1, D))],
        out_shape=[jax.ShapeDtypeStruct((T, A_Q), F32), jax.ShapeDtypeStruct((T, B_W), F32),
                   jax.ShapeDtypeStruct((T, D), BF16), jax.ShapeDtypeStruct((1, D), F32),
                   jax.ShapeDtypeStruct((1, D), F32)],
        compiler_params=_params(1),
    )(dh, att, g_post, w_out)


def _ple_fwd_loss(h, g_pre, w_gate, p, w_proj, g_post, target):
    T, D = h.shape
    tm = TOKEN_TILE
    n_proj, ple, db = w_proj.shape

    def body(h_ref, gpre_ref, wg_ref, p_ref, wp_ref, gpost_ref, t_ref,
             a_ref, dpre_ref, de_ref, dh_ref, loss_ref, dgpost_ref, dgpre_ref):
        i = pl.program_id(0)

        @pl.when(i == 0)
        def _():
            loss_ref[...] = jnp.zeros_like(loss_ref)
            dgpost_ref[...] = jnp.zeros_like(dgpost_ref)
            dgpre_ref[...] = jnp.zeros_like(dgpre_ref)

        x = h_ref[...]
        rx = _rstd(x)
        a = (x * rx * gpre_ref[...]).astype(BF16)
        a_ref[...] = a
        gate = jax.nn.sigmoid(_mm(a, wg_ref[...]))
        pb = p_ref[...].astype(BF16)
        e = jnp.concatenate([_mm(pb, wp_ref[k]) for k in range(n_proj)], axis=1)
        ge = gate * e
        rg = _rstd(ge)
        diff = x + ge * rg * gpost_ref[...] - t_ref[...]
        loss_ref[...] += 0.5 * jnp.sum(jnp.mean(diff * diff, axis=1, keepdims=True))
        dy = diff * (1.0 / D)
        dge, dgain = _rms_bwd(ge, rg, gpost_ref[...], dy)
        dgpost_ref[...] += _colsum(dgain)
        de_ref[...] = (dge * gate).astype(BF16)
        dpre = (dge * e * gate * (1.0 - gate)).astype(BF16)
        dpre_ref[...] = dpre
        dx, dgain = _rms_bwd(x, rx, gpre_ref[...], _mm_nt(dpre, wg_ref[...]))
        dgpre_ref[...] += _colsum(dgain)
        dh_ref[...] = dy + dx

    def tile(w):
        return pl.BlockSpec((tm, w), lambda i: (i, 0))

    return pl.pallas_call(
        body, name="ple_fwd_bwd", grid=(T // tm,),
        in_specs=[tile(D), _full((1, D)), _full((D, D)), tile(ple), _full((n_proj, ple, db)), _full((1, D)), tile(D)],
        out_specs=[tile(D), tile(D), tile(D), tile(D), _full((1, LANES)), _full((1, D)), _full((1, D))],
        out_shape=[jax.ShapeDtypeStruct((T, D), BF16),
                   jax.ShapeDtypeStruct((T, D), BF16),
                   jax.ShapeDtypeStruct((T, D), BF16),
                   jax.ShapeDtypeStruct((T, D), F32),
                   jax.ShapeDtypeStruct((1, LANES), F32),
                   jax.ShapeDtypeStruct((1, D), F32),
                   jax.ShapeDtypeStruct((1, D), F32)],
        compiler_params=_params(1),
    )(h, g_pre, w_gate, p, w_proj, g_post, target)


def _ple_dw_proj(p, de, n_proj):
    T, ple = p.shape
    D = de.shape[1]
    db = D // n_proj
    tk = TOKEN_TILE
    nt = T // tk

    def body(p_ref, de_ref, o_ref, acc):
        t = pl.program_id(0)

        @pl.when(t == 0)
        def _():
            acc[...] = jnp.zeros_like(acc)

        acc[...] += _mm_tn(p_ref[...].astype(BF16), de_ref[...])

        @pl.when(t == nt - 1)
        def _():
            for k in range(n_proj):
                o_ref[k] = acc[:, k * db:(k + 1) * db].astype(BF16)

    return pl.pallas_call(
        body, name="ple_dw_proj", grid=(nt,),
        in_specs=[pl.BlockSpec((tk, ple), lambda t: (t, 0)), pl.BlockSpec((tk, D), lambda t: (t, 0))],
        out_specs=_full((n_proj, ple, db)), out_shape=jax.ShapeDtypeStruct((n_proj, ple, db), BF16),
        scratch_shapes=[pltpu.VMEM((ple, D), F32)], compiler_params=_params(1),
    )(p, de)


def _tok(width):
    return pl.BlockSpec((DW_TILE, width), lambda b, t: (t, 0))


def _dw_gu(a, dgu, name, side=None):
    T, D = a.shape
    nj, _, _, FB = dgu.shape
    return _tn_matmul(
        dgu, a, pl.BlockSpec((None, None, DW_TILE, FB), lambda b, t: (b % nj, b // nj, t, 0)), _tok(D),
        jax.ShapeDtypeStruct((2 * nj, FB, D), BF16), pl.BlockSpec((None, FB, D), lambda b, t: (b, 0, 0)),
        2 * nj, T // DW_TILE, (FB, D), name, side=side)


def _dw_down(hh, df, name, side=None):
    nj, T, FB = hh.shape
    D = df.shape[1]
    return _tn_matmul(
        hh, df, pl.BlockSpec((None, DW_TILE, FB), lambda b, t: (b, t, 0)), _tok(D),
        jax.ShapeDtypeStruct((nj, FB, D), BF16), pl.BlockSpec((None, FB, D), lambda b, t: (b, 0, 0)),
        nj, T // DW_TILE, (FB, D), name, side=side)


def _dw_rows(xm, y, name, rows):
    T, k = xm.shape
    D = y.shape[1]
    out = _tn_matmul(
        xm, y, pl.BlockSpec((DW_TILE, rows), lambda b, t: (t, b)), _tok(D),
        jax.ShapeDtypeStruct((k, D), BF16), pl.BlockSpec((rows, D), lambda b, t: (b, 0)),
        k // rows, T // DW_TILE, (rows, D), name)
    return out.reshape(N_DEV, k // N_DEV, D)


def _cast_bf16(arrays):
    n = len(arrays)

    def body(*refs):
        for a in range(n):
            refs[n + a][...] = refs[a][...].astype(BF16)

    return pl.pallas_call(
        body, name="cast_shards",
        in_specs=[pl.BlockSpec(memory_space=pltpu.VMEM)] * n, out_specs=[pl.BlockSpec(memory_space=pltpu.VMEM)] * n,
        out_shape=[jax.ShapeDtypeStruct(a.shape, BF16) for a in arrays],
        compiler_params=pltpu.CompilerParams(vmem_limit_bytes=VMEM_LIMIT),
    )(*arrays)


def _pack_layout(D, n_rel_rows):
    n_bin = -(-D_IN // D)
    row_bin = len(GAINS)
    row_sink = row_bin + n_bin
    row_loss = row_sink + 1
    row_rb = -(-(row_loss + 1) // 8) * 8
    n_rows = row_rb + -(-n_rel_rows // 8) * 8
    bin_parts = [(r, min(D, D_IN - r * D)) for r in range(n_bin)]
    return row_bin, row_sink, row_loss, row_rb, n_rows, bin_parts


def _pair_swap_call(grad_blocks):
    def body(g_in, received, send_sems, recv_sems):
        start, _, wait = _pair_swap([g_in], [received], send_sems, recv_sems)
        start()
        wait()

    any_spec = pl.BlockSpec(memory_space=pl.ANY)
    return pl.pallas_call(
        body, name="pair_swap", in_specs=[any_spec], out_specs=any_spec,
        out_shape=_side_out_shapes("pair_swap", [grad_blocks])[0],
        scratch_shapes=[pltpu.SemaphoreType.DMA((1, N_CHIPS)), pltpu.SemaphoreType.DMA((1, N_CHIPS))],
    )(grad_blocks)


def _pair_add(blocks, received, name):
    n, R, C = received.shape
    rows = _adamw_rows(R)
    core = lax.axis_index("c").astype(jnp.int32).reshape(1)

    def body(core_ref, a_ref, b_ref, o_ref):
        o_ref[...] = (a_ref[...].astype(F32) + b_ref[...].astype(F32)).astype(o_ref.dtype)

    tile = pl.BlockSpec((None, rows, C), lambda q, r, core_ref: (q, r, 0))
    return pl.pallas_call(
        body, name=name,
        grid_spec=pltpu.PrefetchScalarGridSpec(
            num_scalar_prefetch=1, grid=(n, R // rows),
            in_specs=[pl.BlockSpec((None, rows, C), lambda q, r, core_ref: (2 * q + core_ref[0], r, 0)), tile],
            out_specs=tile),
        out_shape=jax.ShapeDtypeStruct(received.shape, received.dtype), compiler_params=_params(2),
    )(core, blocks, received)


def _final_exchange(grad_blocks, partials, loss):
    D = partials["ffn1_pre_g"].shape[1]
    rb_shape = partials["rel_bias"].shape
    row_bin, row_sink, row_loss, row_rb, n_rows, bin_parts = _pack_layout(D, rb_shape[0])
    n_small = len(SMALL)

    def body(*refs):
        g_in = refs[0]
        part = dict(zip(SMALL, refs[1:1 + n_small]))
        loss_ref = refs[1 + n_small]
        landed, gath, pack, send_sems, recv_sems, local_sems = refs[2 + n_small:]

        pack[...] = jnp.zeros_like(pack)
        for i, name in enumerate(GAINS):
            pack[i:i + 1, :] = part[name][...]
        for r, width in bin_parts:
            pack[row_bin + r:row_bin + r + 1, 0:width] = part["b_in"][:, r * D:r * D + width]
        pack[row_sink:row_sink + 1, 0:LANES] = part["sinks"][...]
        pack[row_loss:row_loss + 1, 0:LANES] = loss_ref[...]
        pack[row_rb:row_rb + rb_shape[0], 0:rb_shape[1]] = part["rel_bias"][...]

        small_start, _, small_wait = _side_copies("gather", [pack], [gath], send_sems, recv_sems, local_sems, sem_row=0)
        big_start, _, big_wait = _quad_exchange([g_in], [landed], send_sems, recv_sems, local_sems, sem_row=1)
        small_start()
        big_start()
        small_wait()
        big_wait()

    args = [grad_blocks] + [partials[k] for k in SMALL] + [loss]
    vmem = pl.BlockSpec(memory_space=pltpu.VMEM)
    any_spec = pl.BlockSpec(memory_space=pl.ANY)
    return pl.pallas_call(
        body, name="final_exchange",
        in_specs=[any_spec] + [vmem] * (n_small + 1),
        out_specs=[any_spec, any_spec],
        out_shape=[jax.ShapeDtypeStruct(grad_blocks.shape, grad_blocks.dtype),
                   jax.ShapeDtypeStruct((N_DEV, n_rows, D), F32)],
        scratch_shapes=[pltpu.VMEM((n_rows, D), F32), pltpu.SemaphoreType.DMA((2, 7)),
                        pltpu.SemaphoreType.DMA((2, 7)), pltpu.SemaphoreType.DMA((2, N_CHIPS))],
    )(*args)


def _adamw(w, g, m, v):
    m = ADAM_B1 * m + (1.0 - ADAM_B1) * g
    v = ADAM_B2 * v + (1.0 - ADAM_B2) * (g * g)
    m_hat = m / (1.0 - ADAM_B1 ** ADAM_STEP)
    v_hat = v / (1.0 - ADAM_B2 ** ADAM_STEP)
    return -ADAM_LR * (m_hat / (jnp.sqrt(v_hat) + ADAM_EPS) + ADAM_WD * w), m, v


def _sum_adamw(partials, w, m, v, rows, name):
    R, C = w.shape
    n = partials.shape[0]

    def body(p_ref, w_ref, m_ref, v_ref, g_ref, d_ref, nm_ref, nv_ref):
        g = p_ref[0].astype(F32)
        for k in range(1, n):
            g = g + p_ref[k].astype(F32)
        g_ref[...] = g
        d_ref[...], nm_ref[...], nv_ref[...] = _adamw(w_ref[...], g, m_ref[...], v_ref[...])

    tile = pl.BlockSpec((rows, C), lambda i: (i, 0))
    return pl.pallas_call(
        body, name=name, grid=(R // rows,),
        in_specs=[pl.BlockSpec((n, rows, C), lambda i: (0, i, 0)), tile, tile, tile],
        out_specs=[tile] * 4, out_shape=[jax.ShapeDtypeStruct((R, C), F32)] * 4,
        compiler_params=_params(1),
    )(partials, w, m, v)


def _small_adamw(gathered, ws, ms, vs):
    D = ws["ffn1_pre_g"].shape[1]
    n_sink = ws["sinks"].shape[1]
    rb_shape = ws["rel_bias"].shape
    row_bin, row_sink, row_loss, row_rb, n_rows, bin_parts = _pack_layout(D, rb_shape[0])
    n_small = len(SMALL)

    def body(*refs):
        gath = refs[0]
        pos = 1
        w_ref = dict(zip(SMALL, refs[pos:pos + n_small]))
        m_ref = dict(zip(SMALL, refs[pos + n_small:pos + 2 * n_small]))
        v_ref = dict(zip(SMALL, refs[pos + 2 * n_small:pos + 3 * n_small]))
        pos += 3 * n_small
        outs = {name: refs[pos + 4 * i:pos + 4 * i + 4] for i, name in enumerate(SMALL)}
        loss_out = refs[pos + 4 * n_small]
        pack = refs[pos + 4 * n_small + 1]

        total = gath[0]
        for k in range(1, N_DEV):
            total = total + gath[k]
        pack[...] = total

        def update(name, g):
            g_out, d_out, m_out, v_out = outs[name]
            g_out[...] = g
            d_out[...], m_out[...], v_out[...] = _adamw(w_ref[name][...], g, m_ref[name][...], v_ref[name][...])

        for i, name in enumerate(GAINS):
            update(name, pack[i:i + 1, :])
        update("b_in", jnp.concatenate([pack[row_bin + r:row_bin + r + 1, 0:width] for r, width in bin_parts], axis=1))
        update("sinks", pack[row_sink:row_sink + 1, 0:n_sink])
        update("rel_bias", pack[row_rb:row_rb + rb_shape[0], 0:rb_shape[1]])
        loss_out[...] = pack[row_loss:row_loss + 1, 0:LANES]

    args = [gathered]
    for group in (ws, ms, vs):
        args += [group[k] for k in SMALL]
    out_shape = []
    for name in SMALL:
        out_shape += [jax.ShapeDtypeStruct(ws[name].shape, F32)] * 4
    out_shape.append(jax.ShapeDtypeStruct((1, LANES), F32))
    res = pl.pallas_call(
        body, name="small_adamw",
        in_specs=[pl.BlockSpec(memory_space=pltpu.VMEM)] * len(args),
        out_specs=[pl.BlockSpec(memory_space=pltpu.VMEM)] * len(out_shape),
        out_shape=out_shape,
        scratch_shapes=[pltpu.VMEM((n_rows, D), F32)],
    )(*args)
    per_name = {name: res[4 * i:4 * i + 4] for i, name in enumerate(SMALL)}
    return per_name, res[-1]


COLUMN_SHARDED = ("ffn1_w_gu", "ffn2_w_gu", "w_in")


def _adamw_rows(rows_total):
    return max(r for r in range(16, min(rows_total, 256) + 1, 16) if rows_total % r == 0)


def kernel(x, p, rel_bias, ffn1_pre_g, ffn1_w_gu, ffn1_w_down, ffn1_post_g, attn_pre_g, w_in, b_in, sinks, w_out, b_out, attn_post_g, ffn2_pre_g, ffn2_w_gu, ffn2_w_down, ffn2_post_g, ple_pre_g, w_ple_gate, w_ple_proj, ple_post_g, loss_target, m_rel_bias, m_ffn1_pre_g, m_ffn1_w_gu, m_ffn1_w_down, m_ffn1_post_g, m_attn_pre_g, m_w_in, m_b_in, m_sinks, m_w_out, m_b_out, m_attn_post_g, m_ffn2_pre_g, m_ffn2_w_gu, m_ffn2_w_down, m_ffn2_post_g, m_ple_pre_g, m_w_ple_gate, m_w_ple_proj, m_ple_post_g, v_rel_bias, v_ffn1_pre_g, v_ffn1_w_gu, v_ffn1_w_down, v_ffn1_post_g, v_attn_pre_g, v_w_in, v_b_in, v_sinks, v_w_out, v_b_out, v_attn_post_g, v_ffn2_pre_g, v_ffn2_w_gu, v_ffn2_w_down, v_ffn2_post_g, v_ple_pre_g, v_w_ple_gate, v_w_ple_proj, v_ple_post_g):
    given = dict(locals())
    ws = {k: given[k] for k in WEIGHTS}
    ms = {k: given["m_" + k] for k in WEIGHTS}
    vs = {k: given["v_" + k] for k in WEIGHTS}

    def shard(t):
        return t.reshape(t.shape[1:])

    xs, ps, target = shard(x), shard(shard(p)), shard(loss_target)
    T, D = xs.shape
    small = {k: ws[k] for k in SMALL}

    def local(group, k):
        t = shard(group[k])
        return jnp.swapaxes(t, 0, 1) if k in COLUMN_SHARDED else t

    shards = {k: local(ws, k) for k in BIG}

    cast = dict(zip(BIG, _cast_bf16([shards[k] for k in BIG])))
    buckets_a = _bucket_tiles(PATTERNS_A)
    buckets_b = _bucket_tiles(PATTERNS_B)
    bias_a, _ = _bias_build(small["rel_bias"], buckets_a, 0, "bias_build_a")
    bias_b, (w_gu1, w_down1) = _bias_build(
        small["rel_bias"], buckets_b, N_HEAD_GROUP, "bias_build_b",
        side=("relay_gather", [cast["ffn1_w_gu"], cast["ffn1_w_down"]]))
    w_down1 = w_down1.reshape(-1, D)
    a_cfg = dict(patterns=PATTERNS_A, qcol=Q_A_COL, kcol=K_A_COL, vcol=V_A_COL, shared_kv=True)
    b_cfg = dict(patterns=PATTERNS_B, qcol=Q_B_COL, kcol=K_B_COL, vcol=V_B_COL, shared_kv=False)

    (h1, f1, a1, gu1), (w_in_g, w_down2) = _ffn_fwd(
        xs, small["ffn1_pre_g"], small["ffn1_post_g"], w_gu1, w_down1, "ffn1_fwd",
        side=("relay_gather", [cast["w_in"], cast["ffn2_w_down"]]))
    w_in_full = w_in_g.reshape(D_IN, D)
    w_down2 = w_down2.reshape(-1, D)
    (z, a2), (w_out_g,) = _inproj_fwd(h1, small["attn_pre_g"], w_in_full, small["b_in"],
                                      side=("relay_gather", [cast["w_out"]]))
    w_out_full = w_out_g.reshape(-1, D)
    (mix_a, lse_a), (w_gate, w_proj) = _attn_fwd(
        z, bias_a, small["sinks"], name="attn_a_fwd", **a_cfg,
        side=("relay_gather", [cast["w_ple_gate"], cast["w_ple_proj"]]))
    w_gate = w_gate.reshape(-1, D)
    (mix_b, lse_b), (w_gu2,) = _attn_fwd(
        z, bias_b, None, name="attn_b_fwd", **b_cfg, side=("relay_gather", [cast["ffn2_w_gu"]]))
    (h3, f2, a3, gu2, att, h2, mix), _ = _ffn_fwd(
        h1, small["ffn2_pre_g"], small["ffn2_post_g"], w_gu2, w_down2, "ffn2_fwd",
        attn=(mix_a, mix_b, w_out_full, small["b_out"], small["attn_post_g"]))
    a4, dpre, de, dh3, loss, dg_ple_post, dg_ple_pre = _ple_fwd_loss(
        h3, small["ple_pre_g"], w_gate, ps, w_proj, small["ple_post_g"], target)

    d_gate = _dw_rows(a4, dpre, "ple_dw_gate", min(256, D))
    d_proj = _ple_dw_proj(ps, de, N_DEV)
    landed = {}
    (dh2, df2, hh2, dgu2, dg_f2_post, dg_f2_pre), (landed["w_ple_gate"], landed["w_ple_proj"]) = _ffn_bwd(
        dh3, f2, small["ffn2_post_g"], h2, small["ffn2_pre_g"], gu2, w_gu2, w_down2, "ffn2_bwd",
        side=("exchange", [d_gate, d_proj]))
    d_gu2 = _dw_gu(a3, dgu2, "ffn2_dw_gu")
    d_down2 = _dw_down(hh2, df2, "ffn2_dw_down").reshape(N_DEV, -1, D)
    dmix_a, dmix_b, datt, dg_attn_post, db_out = _outproj_bwd(dh2, att, small["attn_post_g"], w_out_full)
    d_out = _dw_rows(mix, datt, "attn_dw_out", 256)
    (dqa, dka, dva, ds_a, dsinks), _ = _attn_bwd(
        z, bias_a, small["sinks"], dmix_a, mix_a, lse_a, name="attn_a_bwd", **a_cfg)
    (dqb, dkb, dvb, ds_b), (landed["ffn2_w_gu"],) = _attn_bwd(
        z, bias_b, None, dmix_b, mix_b, lse_b, name="attn_b_bwd", **b_cfg, side=("exchange", [d_gu2]))
    (dh1, dz, db_in, dg_attn_pre), (landed["w_out"],) = _inproj_bwd(
        dqa, dka, dva, dqb, dkb, dvb, w_in_full, h1, small["attn_pre_g"], dh2, side=("exchange", [d_out]))
    cols = D_IN // 3
    d_in = _tn_matmul(
        dz, a2, pl.BlockSpec((DW_TILE, cols), lambda b, t: (t, b)), _tok(D),
        jax.ShapeDtypeStruct((D_IN, D), BF16), pl.BlockSpec((cols, D), lambda b, t: (b, 0)),
        3, T // DW_TILE, (cols, D), "attn_dw_in").reshape(N_DEV, D_IN // N_DEV, D)
    (grad_x, df1, hh1, dgu1, dg_f1_post, dg_f1_pre), (landed["w_in"], landed["ffn2_w_down"]) = _ffn_bwd(
        dh1, f1, small["ffn1_post_g"], xs, small["ffn1_pre_g"], gu1, w_gu1, w_down1, "ffn1_bwd",
        side=("exchange", [d_in, d_down2]))
    d_down1 = _dw_down(hh1, df1, "ffn1_dw_down").reshape(N_DEV, -1, D)
    d_gu1, (landed["ffn1_w_down"],) = _dw_gu(a1, dgu1, "ffn1_dw_gu", side=("exchange", [d_down1]))

    rb_a = _bias_grad(ds_a, buckets_a, "bias_grad_a")
    rb_b = _bias_grad(ds_b, buckets_b, "bias_grad_b").reshape(len(PATTERNS_B), N_HEAD_GROUP, NUM_BUCKETS)
    d_rel_bias = jnp.concatenate([rb_a.T, jnp.sum(rb_b, axis=0).T], axis=1)
    small_grads = {"ffn1_pre_g": dg_f1_pre, "ffn1_post_g": dg_f1_post, "attn_pre_g": dg_attn_pre,
                   "attn_post_g": dg_attn_post, "ffn2_pre_g": dg_f2_pre, "ffn2_post_g": dg_f2_post,
                   "ple_pre_g": dg_ple_pre, "ple_post_g": dg_ple_post, "b_out": db_out, "b_in": db_in,
                   "sinks": dsinks, "rel_bias": d_rel_bias}
    d_gu1_pairs = _pair_add(d_gu1, _pair_swap_call(d_gu1), "ffn1_dw_gu_pair_add")
    landed["ffn1_w_gu"], small_gathered = _final_exchange(d_gu1_pairs, small_grads, loss)

    result = {}
    for k in BIG:
        outs = _sum_adamw(landed[k], shards[k], local(ms, k), local(vs, k), _adamw_rows(shards[k].shape[0]),
                          k + "_adamw")
        if k in COLUMN_SHARDED:
            outs = [jnp.swapaxes(o, 0, 1) for o in outs]
        result[k] = [o.reshape(ws[k].shape) for o in outs]
    small_res, loss_all = _small_adamw(
        small_gathered, small, {k: ms[k] for k in SMALL}, {k: vs[k] for k in SMALL})
    result.update(small_res)

    out = [loss_all[0, 0], grad_x.reshape(x.shape)]
    for i in range(4):
        out += [result[k][i] for k in WEIGHTS]
    return tuple(out)
```

```python
import functools
import math

import numpy as np
import jax
import jax.numpy as jnp
from jax import lax
from jax.experimental import pallas as pl
from jax.experimental.pallas import tpu as pltpu

F32 = jnp.float32
BF16 = jnp.bfloat16
MESH = pl.DeviceIdType.MESH

N_DEV = 8
EPS = 1e-6
NEG_INF = -1e30
HEAD_DIM = 64
LANES = 128
QBLK = 128
D_IN = 2304
A_Q, A_KV, B_W = 512, 128, 512
N_HEAD_GROUP = 8
NUM_BUCKETS = 32
MAX_DISTANCE = 2048
PATTERNS_A = ((1, 127),)
PATTERNS_B = ((1, 128), (4, 128), (16, 128))
Q_A_COL, K_A_COL, V_A_COL = 0, 4, 5
Q_B_COL, K_B_COL, V_B_COL = 6, 10, 14

ADAM_LR, ADAM_B1, ADAM_B2, ADAM_EPS, ADAM_WD, ADAM_STEP = 0.001, 0.9, 0.999, 1e-08, 0.01, 10

TOKEN_TILE = 512
DW_TILE = 1024
FWD_BLOCKS = 4
BWD_BLOCKS = 4
VMEM_LIMIT = 56 * 1024 * 1024
ARB = "arbitrary"

BIG = ("ffn1_w_gu", "ffn1_w_down", "w_in", "w_out", "ffn2_w_gu", "ffn2_w_down", "w_ple_gate", "w_ple_proj")
GAINS = ("ffn1_pre_g", "ffn1_post_g", "attn_pre_g", "attn_post_g", "ffn2_pre_g", "ffn2_post_g",
         "ple_pre_g", "ple_post_g", "b_out")
SMALL = GAINS + ("b_in", "sinks", "rel_bias")
WEIGHTS = ("rel_bias", "ffn1_pre_g", "ffn1_w_gu", "ffn1_w_down", "ffn1_post_g", "attn_pre_g", "w_in", "b_in",
           "sinks", "w_out", "b_out", "attn_post_g", "ffn2_pre_g", "ffn2_w_gu", "ffn2_w_down", "ffn2_post_g",
           "ple_pre_g", "w_ple_gate", "w_ple_proj", "ple_post_g")


def _params(n_axes):
    return pltpu.CompilerParams(dimension_semantics=(ARB,) * n_axes, vmem_limit_bytes=VMEM_LIMIT)


def _mm(a, b):
    return jnp.dot(a, b, preferred_element_type=F32)


def _mm_nt(a, b):
    return lax.dot_general(a, b, (((1,), (1,)), ((), ())), preferred_element_type=F32)


def _mm_tn(a, b):
    return lax.dot_general(a, b, (((0,), (0,)), ((), ())), preferred_element_type=F32)


def _rstd(x):
    return lax.rsqrt(jnp.mean(x * x, axis=-1, keepdims=True) + EPS)


def _rms_bwd(x, r, gain, dy):
    n = x * r
    gdy = dy * gain
    return r * (gdy - n * jnp.mean(gdy * n, axis=-1, keepdims=True)), dy * n


def _colsum(v):
    return jnp.sum(v, axis=0, keepdims=True)


def _full(shape):
    return pl.BlockSpec(shape, lambda *_: (0,) * len(shape))


def _mesh_place():
    return lax.axis_index("x"), lax.axis_index("y"), lax.axis_index("c")


def _slot(dev):
    return 4 * dev[0] + 2 * dev[1] + dev[2]


def _peers(x, y, c):
    out = []
    for flip in range(1, N_DEV):
        dx, dy, dc = (flip >> 2) & 1, (flip >> 1) & 1, flip & 1
        out.append((1 - x if dx else x, 1 - y if dy else y, 1 - c if dc else c))
    return out


def _side_copies(kind, ins, outs, send_sems, recv_sems, local_sems, sem_row=0):
    n = len(ins)
    x, y, c = _mesh_place()
    me = _slot((x, y, c))
    peers = _peers(x, y, c)

    def src(a, block):
        return ins[a] if kind == "gather" else ins[a].at[block]

    def send(a, k, peer):
        return pltpu.make_async_remote_copy(
            src_ref=src(a, _slot(peer)), dst_ref=outs[a].at[me],
            send_sem=send_sems.at[sem_row + a, k], recv_sem=recv_sems.at[sem_row + a, k],
            device_id=peer, device_id_type=MESH)

    def arrival(a, k, peer):
        return pltpu.make_async_remote_copy(
            src_ref=src(a, _slot(peer)), dst_ref=outs[a].at[_slot(peer)],
            send_sem=send_sems.at[sem_row + a, k], recv_sem=recv_sems.at[sem_row + a, k],
            device_id=peer, device_id_type=MESH)

    def own(a):
        return pltpu.make_async_copy(src(a, me), outs[a].at[me], local_sems.at[sem_row + a, 0])

    def start():
        for k, peer in enumerate(peers):
            for a in range(n):
                send(a, k, peer).start()
        for a in range(n):
            own(a).start()

    def wait():
        for k, peer in enumerate(peers):
            for a in range(n):
                arrival(a, k, peer).wait_recv()
        for k, peer in enumerate(peers):
            for a in range(n):
                send(a, k, peer).wait_send()
        for a in range(n):
            own(a).wait()

    return start, None, wait


N_CHIPS = N_DEV // 2


def _pair_swap(ins, received, send_sems, recv_sems):
    n = len(ins)
    x, y, c = _mesh_place()
    sibling = (x, y, 1 - c)

    def send(a, q):
        return pltpu.make_async_remote_copy(
            src_ref=ins[a].at[2 * q + (1 - c)], dst_ref=received[a].at[q],
            send_sem=send_sems.at[a, q], recv_sem=recv_sems.at[a, q], device_id=sibling, device_id_type=MESH)

    def start():
        for a in range(n):
            for q in range(N_CHIPS):
                send(a, q).start()

    def wait():
        for a in range(n):
            for q in range(N_CHIPS):
                send(a, q).wait_recv()
        for a in range(n):
            for q in range(N_CHIPS):
                send(a, q).wait_send()

    return start, None, wait


def _quad_exchange(ins, outs, send_sems, recv_sems, local_sems, sem_row=0):
    n = len(ins)
    x, y, c = _mesh_place()
    mine = 2 * x + y
    chips = [(1 - x, y), (x, 1 - y), (1 - x, 1 - y)]

    def send(a, k, chip):
        return pltpu.make_async_remote_copy(
            src_ref=ins[a].at[2 * chip[0] + chip[1]], dst_ref=outs[a].at[mine],
            send_sem=send_sems.at[sem_row + a, k], recv_sem=recv_sems.at[sem_row + a, k],
            device_id=(chip[0], chip[1], c), device_id_type=MESH)

    def arrival(a, k, chip):
        return pltpu.make_async_remote_copy(
            src_ref=ins[a].at[2 * chip[0] + chip[1]], dst_ref=outs[a].at[2 * chip[0] + chip[1]],
            send_sem=send_sems.at[sem_row + a, k], recv_sem=recv_sems.at[sem_row + a, k],
            device_id=(chip[0], chip[1], c), device_id_type=MESH)

    def own(a):
        return pltpu.make_async_copy(ins[a].at[mine], outs[a].at[mine], local_sems.at[sem_row + a, 0])

    def start():
        for k, chip in enumerate(chips):
            for a in range(n):
                send(a, k, chip).start()
        for a in range(n):
            own(a).start()

    def wait():
        for k, chip in enumerate(chips):
            for a in range(n):
                arrival(a, k, chip).wait_recv()
        for k, chip in enumerate(chips):
            for a in range(n):
                send(a, k, chip).wait_send()
        for a in range(n):
            own(a).wait()

    return start, None, wait


def _relay_gather(ins, outs, send_sems, recv_sems, local_sems):
    n = len(ins)
    x, y, c = _mesh_place()
    me, sibling = (x, y, c), (x, y, 1 - c)
    chips = [(1 - x, y), (x, 1 - y), (1 - x, 1 - y)]

    def copy(a, k, block, to, src=None):
        dst = outs[a].at[_slot(block)]
        return pltpu.make_async_remote_copy(
            src_ref=dst if src is None else src, dst_ref=dst,
            send_sem=send_sems.at[a, k], recv_sem=recv_sems.at[a, k], device_id=to, device_id_type=MESH)

    def own(a):
        return pltpu.make_async_copy(ins[a], outs[a].at[_slot(me)], local_sems.at[a, 0])

    def start():
        for j, chip in enumerate(chips):
            for a in range(n):
                copy(a, 1 + j, me, (*chip, c), src=ins[a]).start()
        for a in range(n):
            copy(a, 0, me, sibling, src=ins[a]).start()
            own(a).start()

    def relay():
        for j, chip in enumerate(chips):
            for a in range(n):
                copy(a, 1 + j, (*chip, c), me).wait_recv()
                copy(a, 4 + j, (*chip, c), sibling).start()

    def wait():
        for a in range(n):
            copy(a, 0, sibling, me).wait_recv()
        for j, chip in enumerate(chips):
            for a in range(n):
                copy(a, 4 + j, (*chip, 1 - c), me).wait_recv()
        for j, chip in enumerate(chips):
            for a in range(n):
                copy(a, 1 + j, me, (*chip, c), src=ins[a]).wait_send()
                copy(a, 4 + j, (*chip, c), sibling).wait_send()
        for a in range(n):
            copy(a, 0, me, sibling, src=ins[a]).wait_send()
            own(a).wait()

    return start, relay, wait


def _side_out_shapes(kind, arrays):
    if kind in ("gather", "relay_gather"):
        return [jax.ShapeDtypeStruct((N_DEV,) + a.shape, a.dtype) for a in arrays]
    return [jax.ShapeDtypeStruct(a.shape, a.dtype) for a in arrays]


def _hosted_call(body, name, grid, in_specs, out_specs, out_shape, scratch_shapes, args, side=None):
    if side is None:
        outs = pl.pallas_call(
            body, name=name, grid=grid, in_specs=in_specs, out_specs=out_specs, out_shape=out_shape,
            scratch_shapes=scratch_shapes, compiler_params=_params(len(grid)))(*args)
        return outs, []
    kind, arrays = side
    side_shapes = _side_out_shapes(kind, arrays)
    n_in, n_out, n_scr, n_side = len(in_specs), len(out_specs), len(scratch_shapes), len(arrays)

    def hosted(*refs):
        pos = 0
        groups = []
        for size in (n_in, n_side, n_out, len(side_shapes), n_scr):
            groups.append(refs[pos:pos + size])
            pos += size
        ins, side_in, outs, side_out, scr = groups
        send_sems, recv_sems, local_sems = refs[pos:]
        ids = [pl.program_id(d) for d in range(len(grid))]
        is_first = functools.reduce(jnp.logical_and, [i == 0 for i in ids])
        is_last = functools.reduce(jnp.logical_and, [i == g - 1 for i, g in zip(ids, grid)])
        if kind == "relay_gather":
            start, relay, wait = _relay_gather(side_in, side_out, send_sems, recv_sems, local_sems)
        else:
            start, relay, wait = _side_copies(kind, side_in, side_out, send_sems, recv_sems, local_sems)
        pl.when(is_first)(start)
        if relay is not None:
            pl.when(is_last)(relay)
        body(*ins, *outs, *scr)
        pl.when(is_last)(wait)

    any_spec = pl.BlockSpec(memory_space=pl.ANY)
    outs = pl.pallas_call(
        hosted, name=name, grid=grid,
        in_specs=list(in_specs) + [any_spec] * n_side,
        out_specs=list(out_specs) + [any_spec] * len(side_shapes),
        out_shape=list(out_shape) + side_shapes,
        scratch_shapes=list(scratch_shapes) + [pltpu.SemaphoreType.DMA((n_side, 7)), pltpu.SemaphoreType.DMA((n_side, 7)),
                                               pltpu.SemaphoreType.DMA((n_side, N_CHIPS))],
        compiler_params=_params(len(grid)))(*args, *arrays)
    return outs[:n_out], outs[n_out:]


def _lane_chunks(width, chunk=2 * LANES):
    return [slice(n0, min(n0 + chunk, width)) for n0 in range(0, width, chunk)]


def _pipelined(chunks, first, middle, last):
    n = len(chunks)
    a, b, total = {}, {}, None
    for step in range(n + 2):
        if step < n:
            a[step] = first(chunks[step])
        if 0 <= step - 1 < n:
            b[step - 1] = middle(chunks[step - 1], a.pop(step - 1))
        if 0 <= step - 2 < n:
            part = last(chunks[step - 2], b.pop(step - 2))
            total = part if total is None else total + part
    return total


def _ffn_fwd(h, g_pre, g_post, w_gu, w_down, name, side=None, attn=None):
    T, D = h.shape
    nj = w_gu.shape[0] // 2
    FB = w_gu.shape[1]
    tm = TOKEN_TILE
    n_attn = 0 if attn is None else 5

    def body(*refs):
        h_ref, gpre_ref, gpost_ref, wg_ref, wu_ref, wd_ref = refs[:6]
        hout_ref, f_ref, a_ref, gu_ref = refs[6 + n_attn:10 + n_attn]
        a_scr, acc = refs[-2:]
        j = pl.program_id(1)

        @pl.when(j == 0)
        def _():
            if attn is None:
                x = h_ref[...]
            else:
                ma_ref, mb_ref, wo_ref, bo_ref, ga_ref = refs[6:11]
                att_ref, hmid_ref, mix_ref = refs[15:18]
                mix = jnp.concatenate([ma_ref[...], mb_ref[...]], axis=1).astype(BF16)
                mix_ref[...] = mix
                att = _mm(mix, wo_ref[...]) + bo_ref[...]
                att_ref[...] = att
                x = h_ref[...] + att * _rstd(att) * ga_ref[...]
                hmid_ref[...] = x
            a = (x * _rstd(x) * gpre_ref[...]).astype(BF16)
            a_scr[...] = a
            a_ref[...] = a
            acc[...] = jnp.zeros_like(acc)

        a = a_scr[...]
        g = _mm_nt(a, wg_ref[...])
        u = _mm_nt(a, wu_ref[...])
        gu_ref[0] = g.astype(BF16)
        gu_ref[1] = u.astype(BF16)
        hh = (g * jax.nn.sigmoid(g) * u).astype(BF16)
        acc[...] += _mm(hh, wd_ref[...])

        @pl.when(j == nj - 1)
        def _():
            f = acc[...]
            f_ref[...] = f
            x = h_ref[...] if attn is None else refs[16][...]
            hout_ref[...] = x + 0.5 * (f * _rstd(f) * gpost_ref[...])

    tile = pl.BlockSpec((tm, D), lambda i, j: (i, 0))
    in_specs = [tile, _full((1, D)), _full((1, D)),
                pl.BlockSpec((None, FB, D), lambda i, j: (j, 0, 0)),
                pl.BlockSpec((None, FB, D), lambda i, j: (j + nj, 0, 0)),
                pl.BlockSpec((FB, D), lambda i, j: (j, 0))]
    out_specs = [tile, tile, tile, pl.BlockSpec((None, 2, tm, FB), lambda i, j: (j, 0, i, 0))]
    out_shape = [
        jax.ShapeDtypeStruct((T, D), F32),
        jax.ShapeDtypeStruct((T, D), F32),
        jax.ShapeDtypeStruct((T, D), BF16),
        jax.ShapeDtypeStruct((nj, 2, T, FB), BF16),
    ]
    args = [h, g_pre, g_post, w_gu, w_gu, w_down]
    if attn is not None:
        mix_a, mix_b, w_out, b_out, g_attn = attn
        d_mix = w_out.shape[0]
        in_specs += [pl.BlockSpec((tm, mix_a.shape[1]), lambda i, j: (i, 0)),
                     pl.BlockSpec((tm, mix_b.shape[1]), lambda i, j: (i, 0)),
                     _full((d_mix, D)), _full((1, D)), _full((1, D))]
        out_specs += [tile, tile, pl.BlockSpec((tm, d_mix), lambda i, j: (i, 0))]
        out_shape += [jax.ShapeDtypeStruct((T, D), F32),
                      jax.ShapeDtypeStruct((T, D), F32),
                      jax.ShapeDtypeStruct((T, d_mix), BF16)]
        args += [mix_a, mix_b, w_out, b_out, g_attn]
    return _hosted_call(
        body, name, (T // tm, nj), in_specs=in_specs, out_specs=out_specs, out_shape=out_shape,
        scratch_shapes=[pltpu.VMEM((tm, D), BF16), pltpu.VMEM((tm, D), F32)], args=args, side=side)


def _ffn_bwd(dh_out, f, g_post, h, g_pre, gu, w_gu, w_down, name, side=None):
    T, D = h.shape
    nj = w_gu.shape[0] // 2
    FB = w_gu.shape[1]
    tm = TOKEN_TILE

    def body(dho_ref, f_ref, gpost_ref, h_ref, gpre_ref, gu_ref, wg_ref, wu_ref, wd_ref,
             dhin_ref, df_ref, hh_ref, dgu_ref, dgpost_ref, dgpre_ref, df_scr, da):
        i, j = pl.program_id(0), pl.program_id(1)

        @pl.when(jnp.logical_and(i == 0, j == 0))
        def _():
            dgpost_ref[...] = jnp.zeros_like(dgpost_ref)
            dgpre_ref[...] = jnp.zeros_like(dgpre_ref)

        @pl.when(j == 0)
        def _():
            fv = f_ref[...]
            df, dgain = _rms_bwd(fv, _rstd(fv), gpost_ref[...], 0.5 * dho_ref[...])
            dgpost_ref[...] += _colsum(dgain)
            dfb = df.astype(BF16)
            df_scr[...] = dfb
            df_ref[...] = dfb
            da[...] = jnp.zeros_like(da)

        dfb = df_scr[...]

        halves = (slice(0, tm // 2), slice(tm // 2, tm))

        def hidden_grad(c):
            return [_mm_nt(dfb[rows], wd_ref[c, :]) for rows in halves]

        def through_swiglu(c, dhh):
            dhh = jnp.concatenate(dhh, axis=0)
            g = gu_ref[0, :, c].astype(F32)
            u = gu_ref[1, :, c].astype(F32)
            sg = jax.nn.sigmoid(g)
            silu = g * sg
            hh_ref[:, c] = (silu * u).astype(BF16)
            dg = (dhh * u * (sg * (1.0 + (g - silu)))).astype(BF16)
            du = (dhh * silu).astype(BF16)
            dgu_ref[0, :, c] = dg
            dgu_ref[1, :, c] = du
            return dg, du

        def input_grad(c, dgu):
            return jnp.concatenate(
                [_mm(dgu[0][rows], wg_ref[c, :]) + _mm(dgu[1][rows], wu_ref[c, :]) for rows in halves], axis=0)

        da[...] += _pipelined(_lane_chunks(FB), hidden_grad, through_swiglu, input_grad)

        @pl.when(j == nj - 1)
        def _():
            x = h_ref[...]
            dx, dgain = _rms_bwd(x, _rstd(x), gpre_ref[...], da[...])
            dgpre_ref[...] += _colsum(dgain)
            dhin_ref[...] = dho_ref[...] + dx

    tile = pl.BlockSpec((tm, D), lambda i, j: (i, 0))
    return _hosted_call(
        body, name, (T // tm, nj),
        in_specs=[
            tile, tile, _full((1, D)), tile, _full((1, D)),
            pl.BlockSpec((None, 2, tm, FB), lambda i, j: (j, 0, i, 0)),
            pl.BlockSpec((None, FB, D), lambda i, j: (j, 0, 0)),
            pl.BlockSpec((None, FB, D), lambda i, j: (j + nj, 0, 0)),
            pl.BlockSpec((FB, D), lambda i, j: (j, 0)),
        ],
        out_specs=[
            tile, tile,
            pl.BlockSpec((None, tm, FB), lambda i, j: (j, i, 0)),
            pl.BlockSpec((None, 2, tm, FB), lambda i, j: (j, 0, i, 0)),
            _full((1, D)), _full((1, D)),
        ],
        out_shape=[
            jax.ShapeDtypeStruct((T, D), F32),
            jax.ShapeDtypeStruct((T, D), BF16),
            jax.ShapeDtypeStruct((nj, T, FB), BF16),
            jax.ShapeDtypeStruct((nj, 2, T, FB), BF16),
            jax.ShapeDtypeStruct((1, D), F32),
            jax.ShapeDtypeStruct((1, D), F32),
        ],
        scratch_shapes=[pltpu.VMEM((tm, D), BF16), pltpu.VMEM((tm, D), F32)],
        args=(dh_out, f, g_post, h, g_pre, gu, w_gu, w_gu, w_down), side=side)


def _tn_matmul(x, y, x_spec, y_spec, out_shape, out_spec, n_blocks, n_steps, acc_shape, name, side=None):
    def body(x_ref, y_ref, o_ref, acc):
        t = pl.program_id(1)

        @pl.when(t == 0)
        def _():
            acc[...] = jnp.zeros_like(acc)

        acc[...] += _mm_tn(x_ref[...].astype(BF16), y_ref[...].astype(BF16))

        @pl.when(t == n_steps - 1)
        def _():
            o_ref[...] = acc[...].astype(o_ref.dtype)

    outs, side_outs = _hosted_call(
        body, name, (n_blocks, n_steps), in_specs=[x_spec, y_spec], out_specs=[out_spec], out_shape=[out_shape],
        scratch_shapes=[pltpu.VMEM(acc_shape, F32)], args=(x, y), side=side)
    return (outs[0], side_outs) if side is not None else outs[0]


def _inproj_fwd(h, g_pre, w_in, b_in, side=None):
    T, D = h.shape
    tm = TOKEN_TILE

    def body(h_ref, g_ref, w_ref, b_ref, z_ref, a_ref):
        x = h_ref[...]
        a = (x * _rstd(x) * g_ref[...]).astype(BF16)
        a_ref[...] = a
        z_ref[...] = _mm_nt(a, w_ref[...]) + b_ref[...]

    return _hosted_call(
        body, "inproj_fwd", (T // tm,),
        in_specs=[pl.BlockSpec((tm, D), lambda i: (i, 0)), _full((1, D)), _full((D_IN, D)), _full((1, D_IN))],
        out_specs=[pl.BlockSpec((tm, D_IN), lambda i: (i, 0)), pl.BlockSpec((tm, D), lambda i: (i, 0))],
        out_shape=[jax.ShapeDtypeStruct((T, D_IN), F32), jax.ShapeDtypeStruct((T, D), BF16)],
        scratch_shapes=[], args=(h, g_pre, w_in, b_in), side=side)


def _inproj_bwd(dqa, dka, dva, dqb, dkb, dvb, w_in, h, g_pre, dres, side=None):
    T, D = h.shape
    tm = TOKEN_TILE

    def body(dqa_ref, dka_ref, dva_ref, dqb_ref, dkb_ref, dvb_ref, w_ref, h_ref, g_ref, dres_ref,
             dh_ref, dz_ref, dbin_ref, dg_ref):
        i = pl.program_id(0)

        @pl.when(i == 0)
        def _():
            dbin_ref[...] = jnp.zeros_like(dbin_ref)
            dg_ref[...] = jnp.zeros_like(dg_ref)

        dz = jnp.concatenate([dqa_ref[...], dka_ref[...], dva_ref[...], dqb_ref[...], dkb_ref[...], dvb_ref[...]],
                             axis=1)
        dbin_ref[...] += _colsum(dz)
        dzb = dz.astype(BF16)
        dz_ref[...] = dzb
        da = _mm(dzb, w_ref[...])
        x = h_ref[...]
        dx, dgain = _rms_bwd(x, _rstd(x), g_ref[...], da)
        dg_ref[...] += _colsum(dgain)
        dh_ref[...] = dres_ref[...] + dx

    def tile(w):
        return pl.BlockSpec((tm, w), lambda i: (i, 0))

    return _hosted_call(
        body, "inproj_bwd", (T // tm,),
        in_specs=[tile(A_Q), tile(A_KV), tile(A_KV), tile(B_W), tile(B_W), tile(B_W),
                  _full((D_IN, D)), tile(D), _full((1, D)), tile(D)],
        out_specs=[tile(D), tile(D_IN), _full((1, D_IN)), _full((1, D))],
        out_shape=[jax.ShapeDtypeStruct((T, D), F32), jax.ShapeDtypeStruct((T, D_IN), BF16),
                   jax.ShapeDtypeStruct((1, D_IN), F32), jax.ShapeDtypeStruct((1, D), F32)],
        scratch_shapes=[], args=(dqa, dka, dva, dqb, dkb, dvb, w_in, h, g_pre, dres), side=side)


def _bucket_tiles(patterns):
    i = np.arange(QBLK)[:, None]
    j = np.arange(2 * QBLK)[None, :]
    dist = QBLK + i - j
    max_exact = NUM_BUCKETS // 2
    tiles = []
    for dilation, max_dist in patterns:
        n = np.maximum(dist * dilation, 0)
        nf = np.maximum(n, 1).astype(np.float32)
        large = max_exact + (np.log(nf / np.float32(max_exact)) / np.float32(math.log(MAX_DISTANCE / max_exact))
                             * np.float32(NUM_BUCKETS - max_exact)).astype(np.int32)
        bucket = np.where(n < max_exact, n, np.minimum(large, NUM_BUCKETS - 1))
        tiles.append(np.where((dist >= 0) & (dist <= max_dist), bucket, -1))
    return jnp.asarray(np.stack(tiles).astype(np.int32))


def _bias_build(rel_bias, buckets, head0, name, side=None):
    n = buckets.shape[0]

    def body(bk_ref, rb_ref, o_ref):
        bk = bk_ref[...]
        base = jnp.where(bk < 0, NEG_INF, 0.0).astype(F32)
        for hd in range(N_HEAD_GROUP):
            o_ref[hd] = lax.fori_loop(
                0, NUM_BUCKETS, lambda b, acc, hd=hd: jnp.where(bk == b, rb_ref[b, head0 + hd], acc), base)

    outs, side_outs = _hosted_call(
        body, name, (n,),
        in_specs=[pl.BlockSpec((None, QBLK, 2 * QBLK), lambda p: (p, 0, 0)), pl.BlockSpec(memory_space=pltpu.SMEM)],
        out_specs=[pl.BlockSpec((None, N_HEAD_GROUP, QBLK, 2 * QBLK), lambda p: (p, 0, 0, 0))],
        out_shape=[jax.ShapeDtypeStruct((n, N_HEAD_GROUP, QBLK, 2 * QBLK), F32)],
        scratch_shapes=[], args=(buckets, rel_bias), side=side)
    return outs[0], side_outs


def _bias_grad(ds, buckets, name):
    n = buckets.shape[0]

    def body(ds_ref, bk_ref, o_ref):
        bk = bk_ref[...]
        row = lax.broadcasted_iota(jnp.int32, (NUM_BUCKETS, 2 * QBLK), 0)
        for hd in range(N_HEAD_GROUP):
            d = ds_ref[hd]
            per_key = jnp.zeros((NUM_BUCKETS, 2 * QBLK), F32)
            for b in range(NUM_BUCKETS):
                per_key = jnp.where(row == b, jnp.sum(jnp.where(bk == b, d, 0.0), axis=0, keepdims=True), per_key)
            o_ref[hd] = jnp.broadcast_to(jnp.sum(per_key, axis=1, keepdims=True), (NUM_BUCKETS, LANES))

    out = pl.pallas_call(
        body, name=name, grid=(n,),
        in_specs=[pl.BlockSpec((None, N_HEAD_GROUP, QBLK, 2 * QBLK), lambda p: (p, 0, 0, 0)),
                  pl.BlockSpec((None, QBLK, 2 * QBLK), lambda p: (p, 0, 0))],
        out_specs=pl.BlockSpec((None, N_HEAD_GROUP, NUM_BUCKETS, LANES), lambda p: (p, 0, 0, 0)),
        out_shape=jax.ShapeDtypeStruct((n, N_HEAD_GROUP, NUM_BUCKETS, LANES), F32),
        compiler_params=_params(1),
    )(ds, buckets)
    return out[:, :, :, 0].reshape(n * N_HEAD_GROUP, NUM_BUCKETS)


def _class_rows(start, dilation):
    if dilation == 1:
        return pl.ds(pl.multiple_of(start, QBLK), QBLK)
    return pl.ds(start, QBLK, stride=dilation)


def _starts_class(u, blocks_per_pass, n_blocks):
    return blocks_per_pass % n_blocks == 0 and u % n_blocks == 0


def _block_starts(idx, n_blocks, dilation):
    cls = idx // n_blocks
    n = idx % n_blocks
    cur = cls + dilation * QBLK * n
    prev = cls + dilation * QBLK * jnp.maximum(n - 1, 0)
    return n, cur, prev


class _HeadPair:
    def __init__(self, g, shared_kv):
        self.lane = lax.broadcasted_iota(jnp.int32, (1, LANES), 1)
        self.lower = self.lane < HEAD_DIM
        self.shared_kv = shared_kv
        self.key_lanes = (self.lane >= HEAD_DIM).astype(jnp.int32) == (g // 2)

    def stack(self, t):
        return jnp.concatenate([jnp.where(self.lower, t, 0.0), jnp.where(self.lower, 0.0, t)], axis=0).astype(BF16)

    def unstack(self, t2):
        return jnp.where(self.lower, t2[:QBLK], t2[QBLK:])

    def keys(self, t):
        if self.shared_kv:
            return jnp.where(self.key_lanes, t, pltpu.roll(t, HEAD_DIM, 1))
        return t

    def key_grads(self, t):
        if self.shared_kv:
            return jnp.where(self.key_lanes, t + pltpu.roll(t, HEAD_DIM, 1), 0.0)
        return t


def _attn_specs(T, qcol, kcol, vcol, shared_kv):
    kv = (lambda c: (lambda g: (0, c))) if shared_kv else (lambda c: (lambda g: (0, c + g)))
    return [pl.BlockSpec((T, LANES), lambda g: (0, qcol + g)),
            pl.BlockSpec((T, LANES), kv(kcol)),
            pl.BlockSpec((T, LANES), kv(vcol))]


def _attn_fwd(z, bias, sinks, patterns, qcol, kcol, vcol, shared_kv, name, side=None):
    T = z.shape[0]
    n_pat = len(patterns)
    has_sink = sinks is not None

    def body(*refs):
        if has_sink:
            sink_ref, refs = refs[0], refs[1:]
        q_ref, k_ref, v_ref, b_ref, o_ref, l_ref = refs[:6]
        po_scr = refs[6:6 + n_pat]
        pl_scr = refs[6 + n_pat:]
        g = pl.program_id(0)
        heads = _HeadPair(g, shared_kv)
        in_prev = lax.broadcasted_iota(jnp.int32, (2 * QBLK, 2 * QBLK), 1) < QBLK

        for pi, (dilation, _) in enumerate(patterns):
            n_blocks = T // (QBLK * dilation)

            def step(it, carry, pi=pi, dilation=dilation, n_blocks=n_blocks):
                blocks = []
                for u in range(FWD_BLOCKS):
                    n, cur, prev = _block_starts(it * FWD_BLOCKS + u, n_blocks, dilation)
                    rows_c, rows_p = _class_rows(cur, dilation), _class_rows(prev, dilation)
                    qm = heads.stack(q_ref[rows_c, :])
                    k_cur, v_cur = k_ref[rows_c, :], v_ref[rows_c, :]
                    no_past = _starts_class(u, FWD_BLOCKS, n_blocks)
                    if no_past:
                        k2, v2 = k_cur, v_cur
                    else:
                        if u % min(FWD_BLOCKS, n_blocks) == 0:
                            k_prev, v_prev = k_ref[rows_p, :], v_ref[rows_p, :]
                        k2 = jnp.concatenate([k_prev, k_cur], axis=0)
                        v2 = jnp.concatenate([v_prev, v_cur], axis=0)
                    k2, v2 = heads.keys(k2).astype(BF16), heads.keys(v2).astype(BF16)
                    k_prev, v_prev = k_cur, v_cur
                    blocks.append(dict(n=n, no_past=no_past, rows=rows_c, v2=v2, s=_mm_nt(qm, k2)))
                for b in blocks:
                    if b["no_past"]:
                        b["s"] = b["s"] * (HEAD_DIM ** -0.5) + b_ref[pi, :, QBLK:]
                    else:
                        s = b["s"] * (HEAD_DIM ** -0.5) + b_ref[pi]
                        b["s"] = jnp.where(jnp.logical_and(in_prev, b["n"] == 0), NEG_INF, s)
                    b["m"] = jnp.max(b["s"], axis=1, keepdims=True)
                for b in blocks:
                    b["pr"] = jnp.exp(b["s"] - b["m"])
                    b["den"] = jnp.sum(b["pr"], axis=1, keepdims=True)
                for b in blocks:
                    b["o2"] = _mm(b["pr"].astype(BF16), b["v2"])
                for b in blocks:
                    lse = b["m"] + jnp.log(b["den"])
                    po_scr[pi][b["rows"], :] = heads.unstack(b["o2"] / b["den"])
                    pl_scr[2 * pi][b["rows"], :] = jnp.broadcast_to(lse[:QBLK], (QBLK, LANES))
                    pl_scr[2 * pi + 1][b["rows"], :] = jnp.broadcast_to(lse[QBLK:], (QBLK, LANES))
                return carry

            lax.fori_loop(0, (dilation * n_blocks) // FWD_BLOCKS, step, 0)

        def merge(ci, carry):
            rows = pl.ds(pl.multiple_of(ci * QBLK, QBLK), QBLK)
            weights = []
            for hd in range(2):
                parts = [pl_scr[2 * pi + hd][rows, :] for pi in range(n_pat)]
                m = functools.reduce(jnp.maximum, parts)
                if has_sink:
                    sink = sink_ref[0, 2 * g + hd]
                    m = jnp.maximum(m, sink)
                terms = [jnp.exp(x - m) for x in parts]
                den = functools.reduce(jnp.add, terms)
                if has_sink:
                    den = den + jnp.exp(sink - m)
                l_ref[hd, rows, :] = m + jnp.log(den)
                inv = 1.0 / den
                weights.append([t * inv for t in terms])
            o_ref[rows, :] = functools.reduce(
                jnp.add, [jnp.where(heads.lower, weights[0][pi], weights[1][pi]) * po_scr[pi][rows, :]
                          for pi in range(n_pat)])
            return carry

        lax.fori_loop(0, T // QBLK, merge, 0)

    in_specs = _attn_specs(T, qcol, kcol, vcol, shared_kv)
    in_specs.append(pl.BlockSpec((n_pat, None, 2 * QBLK, 2 * QBLK), lambda g: (0, g, 0, 0)))
    args = [z, z, z, bias.reshape(n_pat, N_HEAD_GROUP // 2, 2 * QBLK, 2 * QBLK)]
    if has_sink:
        in_specs.insert(0, pl.BlockSpec(memory_space=pltpu.SMEM))
        args.insert(0, sinks)
    return _hosted_call(
        body, name, (N_HEAD_GROUP // 2,),
        in_specs=in_specs,
        out_specs=[pl.BlockSpec((T, LANES), lambda g: (0, g)), pl.BlockSpec((2, T, LANES), lambda g: (g, 0, 0))],
        out_shape=[jax.ShapeDtypeStruct((T, N_HEAD_GROUP * HEAD_DIM), F32),
                   jax.ShapeDtypeStruct((N_HEAD_GROUP, T, LANES), F32)],
        scratch_shapes=[pltpu.VMEM((T, LANES), F32)] * (3 * n_pat), args=args, side=side)


def _attn_bwd(z, bias, sinks, d_out, out, lse, patterns, qcol, kcol, vcol, shared_kv, name, side=None):
    T = z.shape[0]
    n_pat = len(patterns)
    has_sink = sinks is not None
    kv_width = LANES if shared_kv else N_HEAD_GROUP * HEAD_DIM

    def body(*refs):
        if has_sink:
            sink_ref, refs = refs[0], refs[1:]
        q_ref, k_ref, v_ref, b_ref, do_ref, o_ref, l0_ref, l1_ref = refs[:8]
        dq_ref, dk_ref, dv_ref, ds_ref = refs[8:12]
        dsink_ref = refs[12] if has_sink else None
        dk_acc, dv_acc = refs[-2:]
        g = pl.program_id(0)
        heads = _HeadPair(g, shared_kv)
        in_prev = lax.broadcasted_iota(jnp.int32, (2 * QBLK, 2 * QBLK), 1) < QBLK

        dq_ref[...] = jnp.zeros_like(dq_ref)
        ds_ref[...] = jnp.zeros_like(ds_ref)
        dk_acc[...] = jnp.zeros_like(dk_acc)
        dv_acc[...] = jnp.zeros_like(dv_acc)

        dsink = jnp.zeros((1, LANES), F32)
        for pi, (dilation, _) in enumerate(patterns):
            n_blocks = T // (QBLK * dilation)

            def step(idx, dsink, pi=pi, dilation=dilation, n_blocks=n_blocks):
                blocks = []
                for u in range(BWD_BLOCKS):
                    n, cur, prev = _block_starts(idx * BWD_BLOCKS + u, n_blocks, dilation)
                    rows_c, rows_p = _class_rows(cur, dilation), _class_rows(prev, dilation)
                    qm = heads.stack(q_ref[rows_c, :])
                    k_cur, v_cur = k_ref[rows_c, :], v_ref[rows_c, :]
                    first = u % min(BWD_BLOCKS, n_blocks) == 0
                    no_past = _starts_class(u, BWD_BLOCKS, n_blocks)
                    if no_past:
                        k2, v2 = k_cur, v_cur
                    else:
                        if first:
                            k_prev, v_prev = k_ref[rows_p, :], v_ref[rows_p, :]
                        k2 = jnp.concatenate([k_prev, k_cur], axis=0)
                        v2 = jnp.concatenate([v_prev, v_cur], axis=0)
                    k2, v2 = heads.keys(k2).astype(BF16), heads.keys(v2).astype(BF16)
                    k_prev, v_prev = k_cur, v_cur
                    d_o = do_ref[rows_c, :]
                    dom = heads.stack(d_o)
                    dd = d_o * o_ref[rows_c, :]
                    delta = jnp.concatenate([jnp.sum(jnp.where(heads.lower, dd, 0.0), axis=1, keepdims=True),
                                             jnp.sum(jnp.where(heads.lower, 0.0, dd), axis=1, keepdims=True)], axis=0)
                    lse = jnp.concatenate([l0_ref[rows_c, :], l1_ref[rows_c, :]], axis=0)
                    blocks.append(dict(n=n, first=first, no_past=no_past, rows_c=rows_c, rows_p=rows_p, qm=qm, k2=k2,
                                       dom=dom, delta=delta, lse=lse, s=_mm_nt(qm, k2), dp=_mm_nt(dom, v2)))
                for b in blocks:
                    if b["no_past"]:
                        s = b["s"] * (HEAD_DIM ** -0.5) + b_ref[pi, :, QBLK:]
                        b["pr"] = jnp.exp(s - b["lse"])
                    else:
                        s = b["s"] * (HEAD_DIM ** -0.5) + b_ref[pi]
                        s = jnp.where(jnp.logical_and(in_prev, b["n"] == 0), NEG_INF, s)
                        b["pr"] = jnp.exp(s - jnp.concatenate([b["lse"], b["lse"]], axis=1))
                    b["ds"] = b["pr"] * (b["dp"] - b["delta"])
                for b in blocks:
                    dsb = b["ds"].astype(BF16)
                    b["dq2"] = _mm(dsb, b["k2"])
                    b["dk2"] = _mm_tn(dsb, b["qm"])
                    b["dv2"] = _mm_tn(b["pr"].astype(BF16), b["dom"])
                for b in blocks:
                    b["dk2"] = heads.key_grads(b["dk2"]) * (HEAD_DIM ** -0.5)
                    b["dv2"] = heads.key_grads(b["dv2"])
                for u, b in enumerate(blocks):
                    dq_ref[b["rows_c"], :] += heads.unstack(b["dq2"]) * (HEAD_DIM ** -0.5)
                    if b["no_past"]:
                        ds_ref[pi, :, QBLK:] += b["ds"]
                        dk_own, dv_own = b["dk2"], b["dv2"]
                    else:
                        ds_ref[pi] += b["ds"]
                        dk_own, dv_own = b["dk2"][QBLK:], b["dv2"][QBLK:]
                    if u + 1 < len(blocks) and not blocks[u + 1]["first"]:
                        dk_own = dk_own + blocks[u + 1]["dk2"][:QBLK]
                        dv_own = dv_own + blocks[u + 1]["dv2"][:QBLK]
                    if b["first"] and not b["no_past"]:
                        dk_acc[b["rows_p"], :] += b["dk2"][:QBLK]
                        dv_acc[b["rows_p"], :] += b["dv2"][:QBLK]
                    dk_acc[b["rows_c"], :] += dk_own
                    dv_acc[b["rows_c"], :] += dv_own
                    if has_sink:
                        for hd in range(2):
                            rows_h = slice(QBLK * hd, QBLK * (hd + 1))
                            p_sink = jnp.exp(sink_ref[0, 2 * g + hd] - b["lse"][rows_h, 0:1])
                            dsink = dsink - jnp.where(heads.lane == 2 * g + hd,
                                                      jnp.sum(p_sink * b["delta"][rows_h]), 0.0)
                return dsink

            dsink = lax.fori_loop(0, (dilation * n_blocks) // BWD_BLOCKS, step, dsink)

        if shared_kv:
            @pl.when(g == 0)
            def _():
                dk_ref[...] = dk_acc[...]
                dv_ref[...] = dv_acc[...]

            @pl.when(g != 0)
            def _():
                dk_ref[...] += dk_acc[...]
                dv_ref[...] += dv_acc[...]
        else:
            dk_ref[...] = dk_acc[...]
            dv_ref[...] = dv_acc[...]

        if has_sink:
            @pl.when(g == 0)
            def _():
                dsink_ref[...] = dsink

            @pl.when(g != 0)
            def _():
                dsink_ref[...] += dsink

    pair = pl.BlockSpec((T, LANES), lambda g: (0, g))
    stacked = pl.BlockSpec((n_pat, None, 2 * QBLK, 2 * QBLK), lambda g: (0, g, 0, 0))
    stacked_shape = (n_pat, N_HEAD_GROUP // 2, 2 * QBLK, 2 * QBLK)
    in_specs = _attn_specs(T, qcol, kcol, vcol, shared_kv)
    in_specs += [stacked, pair, pair,
                 pl.BlockSpec((None, T, LANES), lambda g: (2 * g, 0, 0)),
                 pl.BlockSpec((None, T, LANES), lambda g: (2 * g + 1, 0, 0))]
    args = [z, z, z, bias.reshape(stacked_shape), d_out, out, lse, lse]
    kv_out = _full((T, LANES)) if shared_kv else pair
    out_specs = [pair, kv_out, kv_out, stacked]
    out_shape = [jax.ShapeDtypeStruct((T, N_HEAD_GROUP * HEAD_DIM), F32),
                 jax.ShapeDtypeStruct((T, kv_width), F32), jax.ShapeDtypeStruct((T, kv_width), F32),
                 jax.ShapeDtypeStruct(stacked_shape, F32)]
    if has_sink:
        in_specs.insert(0, pl.BlockSpec(memory_space=pltpu.SMEM))
        args.insert(0, sinks)
        out_specs.append(_full((1, LANES)))
        out_shape.append(jax.ShapeDtypeStruct((1, LANES), F32))
    outs, side_outs = _hosted_call(
        body, name, (N_HEAD_GROUP // 2,), in_specs=in_specs, out_specs=out_specs, out_shape=out_shape,
        scratch_shapes=[pltpu.VMEM((T, LANES), F32), pltpu.VMEM((T, LANES), F32)], args=args, side=side)
    outs = list(outs)
    outs[3] = outs[3].reshape(n_pat, N_HEAD_GROUP, QBLK, 2 * QBLK)
    return outs, side_outs


def _outproj_bwd(dh, att, g_post, w_out):
    T, D = dh.shape
    tm = TOKEN_TILE
    d_mix = w_out.shape[0]

    def body(dh_ref, att_ref, g_ref, w_ref, dma_ref, dmb_ref, datt_ref, dg_ref, db_ref):
        i = pl.program_id(0)

        @pl.when(i == 0)
        def _():
            dg_ref[...] = jnp.zeros_like(dg_ref)
            db_ref[...] = jnp.zeros_like(db_ref)

        att = att_ref[...]
        datt, dgain = _rms_bwd(att, _rstd(att), g_ref[...], dh_ref[...])
        dg_ref[...] += _colsum(dgain)
        db_ref[...] += _colsum(datt)
        dattb = datt.astype(BF16)
        datt_ref[...] = dattb
        dmix = _mm_nt(dattb, w_ref[...])
        dma_ref[...] = dmix[:, :A_Q]
        dmb_ref[...] = dmix[:, A_Q:]

    def tile(w):
        return pl.BlockSpec((tm, w), lambda i: (i, 0))

    return pl.pallas_call(
        body, name="outproj_bwd", grid=(T // tm,),
        in_specs=[tile(D), tile(D), _full((1, D)), _full((d_mix, D))],
        out_specs=[tile(A_Q), tile(B_W), tile(D), _full((1, D)), _full((1, D))],
        out_shape=[jax.ShapeDtypeStruct((T, A_Q), F32), jax.ShapeDtypeStruct((T, B_W), F32),
                   jax.ShapeDtypeStruct((T, D), BF16), jax.ShapeDtypeStruct((1, D), F32),
                   jax.ShapeDtypeStruct((1, D), F32)],
        compiler_params=_params(1),
    )(dh, att, g_post, w_out)


def _ple_fwd_bwd(h, g_pre, w_gate, p, w_proj, g_post, target):
    T, D = h.shape
    tm = TOKEN_TILE
    n_proj, ple, db = w_proj.shape

    def body(h_ref, gpre_ref, wg_ref, p_ref, wp_ref, gpost_ref, t_ref,
             a_ref, dpre_ref, de_ref, dh_ref, loss_ref, dgpost_ref, dgpre_ref):
        i = pl.program_id(0)

        @pl.when(i == 0)
        def _():
            loss_ref[...] = jnp.zeros_like(loss_ref)
            dgpost_ref[...] = jnp.zeros_like(dgpost_ref)
            dgpre_ref[...] = jnp.zeros_like(dgpre_ref)

        x = h_ref[...]
        rx = _rstd(x)
        a = (x * rx * gpre_ref[...]).astype(BF16)
        a_ref[...] = a
        gate = jax.nn.sigmoid(_mm(a, wg_ref[...]))
        pb = p_ref[...].astype(BF16)
        e = jnp.concatenate([_mm(pb, wp_ref[k]) for k in range(n_proj)], axis=1)
        ge = gate * e
        rg = _rstd(ge)
        diff = x + ge * rg * gpost_ref[...] - t_ref[...]
        loss_ref[...] += 0.5 * jnp.sum(jnp.mean(diff * diff, axis=1, keepdims=True))
        dy = diff * (1.0 / D)
        dge, dgain = _rms_bwd(ge, rg, gpost_ref[...], dy)
        dgpost_ref[...] += _colsum(dgain)
        de_ref[...] = (dge * gate).astype(BF16)
        dpre = (dge * e * gate * (1.0 - gate)).astype(BF16)
        dpre_ref[...] = dpre
        dx, dgain = _rms_bwd(x, rx, gpre_ref[...], _mm_nt(dpre, wg_ref[...]))
        dgpre_ref[...] += _colsum(dgain)
        dh_ref[...] = dy + dx

    def tile(w):
        return pl.BlockSpec((tm, w), lambda i: (i, 0))

    return pl.pallas_call(
        body, name="ple_fwd_bwd", grid=(T // tm,),
        in_specs=[tile(D), _full((1, D)), _full((D, D)), tile(ple), _full((n_proj, ple, db)), _full((1, D)), tile(D)],
        out_specs=[tile(D), tile(D), tile(D), tile(D), _full((1, LANES)), _full((1, D)), _full((1, D))],
        out_shape=[jax.ShapeDtypeStruct((T, D), BF16),
                   jax.ShapeDtypeStruct((T, D), BF16),
                   jax.ShapeDtypeStruct((T, D), BF16),
                   jax.ShapeDtypeStruct((T, D), F32),
                   jax.ShapeDtypeStruct((1, LANES), F32),
                   jax.ShapeDtypeStruct((1, D), F32),
                   jax.ShapeDtypeStruct((1, D), F32)],
        compiler_params=_params(1),
    )(h, g_pre, w_gate, p, w_proj, g_post, target)


def _ple_dw_proj(p, de, n_proj):
    T, ple = p.shape
    D = de.shape[1]
    db = D // n_proj
    tk = TOKEN_TILE
    nt = T // tk

    def body(p_ref, de_ref, o_ref, acc):
        t = pl.program_id(0)

        @pl.when(t == 0)
        def _():
            acc[...] = jnp.zeros_like(acc)

        acc[...] += _mm_tn(p_ref[...].astype(BF16), de_ref[...])

        @pl.when(t == nt - 1)
        def _():
            for k in range(n_proj):
                o_ref[k] = acc[:, k * db:(k + 1) * db].astype(BF16)

    return pl.pallas_call(
        body, name="ple_dw_proj", grid=(nt,),
        in_specs=[pl.BlockSpec((tk, ple), lambda t: (t, 0)), pl.BlockSpec((tk, D), lambda t: (t, 0))],
        out_specs=_full((n_proj, ple, db)), out_shape=jax.ShapeDtypeStruct((n_proj, ple, db), BF16),
        scratch_shapes=[pltpu.VMEM((ple, D), F32)], compiler_params=_params(1),
    )(p, de)


def _tok(width):
    return pl.BlockSpec((DW_TILE, width), lambda b, t: (t, 0))


def _dw_gu(a, dgu, name, side=None):
    T, D = a.shape
    nj, _, _, FB = dgu.shape
    return _tn_matmul(
        dgu, a, pl.BlockSpec((None, None, DW_TILE, FB), lambda b, t: (b % nj, b // nj, t, 0)), _tok(D),
        jax.ShapeDtypeStruct((2 * nj, FB, D), BF16), pl.BlockSpec((None, FB, D), lambda b, t: (b, 0, 0)),
        2 * nj, T // DW_TILE, (FB, D), name, side=side)


def _dw_down(hh, df, name, side=None):
    nj, T, FB = hh.shape
    D = df.shape[1]
    return _tn_matmul(
        hh, df, pl.BlockSpec((None, DW_TILE, FB), lambda b, t: (b, t, 0)), _tok(D),
        jax.ShapeDtypeStruct((nj, FB, D), BF16), pl.BlockSpec((None, FB, D), lambda b, t: (b, 0, 0)),
        nj, T // DW_TILE, (FB, D), name, side=side)


def _dw_rows(xm, y, name):
    T, k = xm.shape
    D = y.shape[1]
    out = _tn_matmul(
        xm, y, _tok(k), _tok(D), jax.ShapeDtypeStruct((k, D), BF16), _full((k, D)),
        1, T // DW_TILE, (k, D), name)
    return out.reshape(N_DEV, k // N_DEV, D)


def _cast_bf16(arrays):
    n = len(arrays)

    def body(*refs):
        for a in range(n):
            refs[n + a][...] = refs[a][...].astype(BF16)

    return pl.pallas_call(
        body, name="cast_shards",
        in_specs=[pl.BlockSpec(memory_space=pltpu.VMEM)] * n, out_specs=[pl.BlockSpec(memory_space=pltpu.VMEM)] * n,
        out_shape=[jax.ShapeDtypeStruct(a.shape, BF16) for a in arrays],
        compiler_params=pltpu.CompilerParams(vmem_limit_bytes=VMEM_LIMIT),
    )(*arrays)


def _pack_layout(D, n_rel_rows):
    n_bin = -(-D_IN // D)
    row_bin = len(GAINS)
    row_sink = row_bin + n_bin
    row_loss = row_sink + 1
    row_rb = -(-(row_loss + 1) // 8) * 8
    n_rows = row_rb + -(-n_rel_rows // 8) * 8
    bin_parts = [(r, min(D, D_IN - r * D)) for r in range(n_bin)]
    return row_bin, row_sink, row_loss, row_rb, n_rows, bin_parts


def _pair_swap_call(grad_blocks):
    def body(g_in, received, send_sems, recv_sems):
        start, _, wait = _pair_swap([g_in], [received], send_sems, recv_sems)
        start()
        wait()

    any_spec = pl.BlockSpec(memory_space=pl.ANY)
    return pl.pallas_call(
        body, name="pair_swap", in_specs=[any_spec], out_specs=any_spec,
        out_shape=jax.ShapeDtypeStruct((N_CHIPS,) + grad_blocks.shape[1:], grad_blocks.dtype),
        scratch_shapes=[pltpu.SemaphoreType.DMA((1, N_CHIPS)), pltpu.SemaphoreType.DMA((1, N_CHIPS))],
    )(grad_blocks)


def _pair_add(blocks, received, name):
    n, R, C = received.shape
    rows = _adamw_rows(R)
    core = lax.axis_index("c").astype(jnp.int32).reshape(1)

    def body(core_ref, a_ref, b_ref, o_ref):
        o_ref[...] = (a_ref[...].astype(F32) + b_ref[...].astype(F32)).astype(o_ref.dtype)

    tile = pl.BlockSpec((None, rows, C), lambda q, r, core_ref: (q, r, 0))
    return pl.pallas_call(
        body, name=name,
        grid_spec=pltpu.PrefetchScalarGridSpec(
            num_scalar_prefetch=1, grid=(n, R // rows),
            in_specs=[pl.BlockSpec((None, rows, C), lambda q, r, core_ref: (2 * q + core_ref[0], r, 0)), tile],
            out_specs=tile),
        out_shape=jax.ShapeDtypeStruct(received.shape, received.dtype), compiler_params=_params(2),
    )(core, blocks, received)


def _final_exchange(grad_blocks, partials, loss):
    D = partials["ffn1_pre_g"].shape[1]
    rb_shape = partials["rel_bias"].shape
    row_bin, row_sink, row_loss, row_rb, n_rows, bin_parts = _pack_layout(D, rb_shape[0])
    n_small = len(SMALL)

    def body(*refs):
        g_in = refs[0]
        part = dict(zip(SMALL, refs[1:1 + n_small]))
        loss_ref = refs[1 + n_small]
        landed, gath, pack, send_sems, recv_sems, local_sems = refs[2 + n_small:]

        pack[...] = jnp.zeros_like(pack)
        for i, name in enumerate(GAINS):
            pack[i:i + 1, :] = part[name][...]
        for r, width in bin_parts:
            pack[row_bin + r:row_bin + r + 1, 0:width] = part["b_in"][:, r * D:r * D + width]
        pack[row_sink:row_sink + 1, 0:LANES] = part["sinks"][...]
        pack[row_loss:row_loss + 1, 0:LANES] = loss_ref[...]
        pack[row_rb:row_rb + rb_shape[0], 0:rb_shape[1]] = part["rel_bias"][...]

        small_start, _, small_wait = _side_copies("gather", [pack], [gath], send_sems, recv_sems, local_sems, sem_row=0)
        big_start, _, big_wait = _quad_exchange([g_in], [landed], send_sems, recv_sems, local_sems, sem_row=1)
        small_start()
        big_start()
        small_wait()
        big_wait()

    args = [grad_blocks] + [partials[k] for k in SMALL] + [loss]
    vmem = pl.BlockSpec(memory_space=pltpu.VMEM)
    any_spec = pl.BlockSpec(memory_space=pl.ANY)
    return pl.pallas_call(
        body, name="final_exchange",
        in_specs=[any_spec] + [vmem] * (n_small + 1),
        out_specs=[any_spec, any_spec],
        out_shape=[jax.ShapeDtypeStruct(grad_blocks.shape, grad_blocks.dtype),
                   jax.ShapeDtypeStruct((N_DEV, n_rows, D), F32)],
        scratch_shapes=[pltpu.VMEM((n_rows, D), F32), pltpu.SemaphoreType.DMA((2, 7)),
                        pltpu.SemaphoreType.DMA((2, 7)), pltpu.SemaphoreType.DMA((2, N_CHIPS))],
    )(*args)


def _adamw(w, g, m, v):
    m = ADAM_B1 * m + (1.0 - ADAM_B1) * g
    v = ADAM_B2 * v + (1.0 - ADAM_B2) * (g * g)
    m_hat = m / (1.0 - ADAM_B1 ** ADAM_STEP)
    v_hat = v / (1.0 - ADAM_B2 ** ADAM_STEP)
    return -ADAM_LR * (m_hat / (jnp.sqrt(v_hat) + ADAM_EPS) + ADAM_WD * w), m, v


def _sum_adamw(partials, w, m, v, rows, name):
    R, C = w.shape
    n = partials.shape[0]

    def body(p_ref, w_ref, m_ref, v_ref, g_ref, d_ref, nm_ref, nv_ref):
        g = p_ref[0].astype(F32)
        for k in range(1, n):
            g = g + p_ref[k].astype(F32)
        g_ref[...] = g
        d_ref[...], nm_ref[...], nv_ref[...] = _adamw(w_ref[...], g, m_ref[...], v_ref[...])

    tile = pl.BlockSpec((rows, C), lambda i: (i, 0))
    return pl.pallas_call(
        body, name=name, grid=(R // rows,),
        in_specs=[pl.BlockSpec((n, rows, C), lambda i: (0, i, 0)), tile, tile, tile],
        out_specs=[tile] * 4, out_shape=[jax.ShapeDtypeStruct((R, C), F32)] * 4,
        compiler_params=_params(1),
    )(partials, w, m, v)


def _small_adamw(gathered, ws, ms, vs):
    D = ws["ffn1_pre_g"].shape[1]
    n_sink = ws["sinks"].shape[1]
    rb_shape = ws["rel_bias"].shape
    row_bin, row_sink, row_loss, row_rb, n_rows, bin_parts = _pack_layout(D, rb_shape[0])
    n_small = len(SMALL)

    def body(*refs):
        gath = refs[0]
        pos = 1
        w_ref = dict(zip(SMALL, refs[pos:pos + n_small]))
        m_ref = dict(zip(SMALL, refs[pos + n_small:pos + 2 * n_small]))
        v_ref = dict(zip(SMALL, refs[pos + 2 * n_small:pos + 3 * n_small]))
        pos += 3 * n_small
        outs = {name: refs[pos + 4 * i:pos + 4 * i + 4] for i, name in enumerate(SMALL)}
        loss_out = refs[pos + 4 * n_small]
        pack = refs[pos + 4 * n_small + 1]

        total = gath[0]
        for k in range(1, N_DEV):
            total = total + gath[k]
        pack[...] = total

        def update(name, g):
            g_out, d_out, m_out, v_out = outs[name]
            g_out[...] = g
            d_out[...], m_out[...], v_out[...] = _adamw(w_ref[name][...], g, m_ref[name][...], v_ref[name][...])

        for i, name in enumerate(GAINS):
            update(name, pack[i:i + 1, :])
        update("b_in", jnp.concatenate([pack[row_bin + r:row_bin + r + 1, 0:width] for r, width in bin_parts], axis=1))
        update("sinks", pack[row_sink:row_sink + 1, 0:n_sink])
        update("rel_bias", pack[row_rb:row_rb + rb_shape[0], 0:rb_shape[1]])
        loss_out[...] = pack[row_loss:row_loss + 1, 0:LANES]

    args = [gathered]
    for group in (ws, ms, vs):
        args += [group[k] for k in SMALL]
    out_shape = []
    for name in SMALL:
        out_shape += [jax.ShapeDtypeStruct(ws[name].shape, F32)] * 4
    out_shape.append(jax.ShapeDtypeStruct((1, LANES), F32))
    res = pl.pallas_call(
        body, name="small_adamw",
        in_specs=[pl.BlockSpec(memory_space=pltpu.VMEM)] * len(args),
        out_specs=[pl.BlockSpec(memory_space=pltpu.VMEM)] * len(out_shape),
        out_shape=out_shape,
        scratch_shapes=[pltpu.VMEM((n_rows, D), F32)],
    )(*args)
    per_name = {name: res[4 * i:4 * i + 4] for i, name in enumerate(SMALL)}
    return per_name, res[-1]


COLUMN_SHARDED = ("ffn1_w_gu", "ffn2_w_gu", "w_in")


def _adamw_rows(rows_total):
    return max(r for r in range(16, min(rows_total, 256) + 1, 16) if rows_total % r == 0)


def kernel(x, p, rel_bias, ffn1_pre_g, ffn1_w_gu, ffn1_w_down, ffn1_post_g, attn_pre_g, w_in, b_in, sinks, w_out, b_out, attn_post_g, ffn2_pre_g, ffn2_w_gu, ffn2_w_down, ffn2_post_g, ple_pre_g, w_ple_gate, w_ple_proj, ple_post_g, loss_target, m_rel_bias, m_ffn1_pre_g, m_ffn1_w_gu, m_ffn1_w_down, m_ffn1_post_g, m_attn_pre_g, m_w_in, m_b_in, m_sinks, m_w_out, m_b_out, m_attn_post_g, m_ffn2_pre_g, m_ffn2_w_gu, m_ffn2_w_down, m_ffn2_post_g, m_ple_pre_g, m_w_ple_gate, m_w_ple_proj, m_ple_post_g, v_rel_bias, v_ffn1_pre_g, v_ffn1_w_gu, v_ffn1_w_down, v_ffn1_post_g, v_attn_pre_g, v_w_in, v_b_in, v_sinks, v_w_out, v_b_out, v_attn_post_g, v_ffn2_pre_g, v_ffn2_w_gu, v_ffn2_w_down, v_ffn2_post_g, v_ple_pre_g, v_w_ple_gate, v_w_ple_proj, v_ple_post_g):
    given = dict(locals())
    ws = {k: given[k] for k in WEIGHTS}
    ms = {k: given["m_" + k] for k in WEIGHTS}
    vs = {k: given["v_" + k] for k in WEIGHTS}

    def shard(t):
        return t.reshape(t.shape[1:])

    xs, ps, target = shard(x), shard(shard(p)), shard(loss_target)
    T, D = xs.shape
    small = {k: ws[k] for k in SMALL}

    def local(group, k):
        t = shard(group[k])
        return jnp.swapaxes(t, 0, 1) if k in COLUMN_SHARDED else t

    shards = {k: local(ws, k) for k in BIG}

    cast = dict(zip(BIG, _cast_bf16([shards[k] for k in BIG])))
    buckets_a = _bucket_tiles(PATTERNS_A)
    buckets_b = _bucket_tiles(PATTERNS_B)
    bias_a, _ = _bias_build(small["rel_bias"], buckets_a, 0, "bias_build_a")
    bias_b, (w_gu1, w_down1) = _bias_build(
        small["rel_bias"], buckets_b, N_HEAD_GROUP, "bias_build_b",
        side=("relay_gather", [cast["ffn1_w_gu"], cast["ffn1_w_down"]]))
    w_down1 = w_down1.reshape(-1, D)
    a_cfg = dict(patterns=PATTERNS_A, qcol=Q_A_COL, kcol=K_A_COL, vcol=V_A_COL, shared_kv=True)
    b_cfg = dict(patterns=PATTERNS_B, qcol=Q_B_COL, kcol=K_B_COL, vcol=V_B_COL, shared_kv=False)

    (h1, f1, a1, gu1), (w_in_g, w_down2) = _ffn_fwd(
        xs, small["ffn1_pre_g"], small["ffn1_post_g"], w_gu1, w_down1, "ffn1_fwd",
        side=("relay_gather", [cast["w_in"], cast["ffn2_w_down"]]))
    w_in_full = w_in_g.reshape(D_IN, D)
    w_down2 = w_down2.reshape(-1, D)
    (z, a2), (w_out_g,) = _inproj_fwd(h1, small["attn_pre_g"], w_in_full, small["b_in"],
                                      side=("relay_gather", [cast["w_out"]]))
    w_out_full = w_out_g.reshape(-1, D)
    (mix_a, lse_a), (w_gate, w_proj) = _attn_fwd(
        z, bias_a, small["sinks"], name="attn_a_fwd", **a_cfg,
        side=("relay_gather", [cast["w_ple_gate"], cast["w_ple_proj"]]))
    w_gate = w_gate.reshape(-1, D)
    (mix_b, lse_b), (w_gu2,) = _attn_fwd(
        z, bias_b, None, name="attn_b_fwd", **b_cfg, side=("relay_gather", [cast["ffn2_w_gu"]]))
    (h3, f2, a3, gu2, att, h2, mix), _ = _ffn_fwd(
        h1, small["ffn2_pre_g"], small["ffn2_post_g"], w_gu2, w_down2, "ffn2_fwd",
        attn=(mix_a, mix_b, w_out_full, small["b_out"], small["attn_post_g"]))
    a4, dpre, de, dh3, loss, dg_ple_post, dg_ple_pre = _ple_fwd_bwd(
        h3, small["ple_pre_g"], w_gate, ps, w_proj, small["ple_post_g"], target)

    d_gate = _dw_rows(a4, dpre, "ple_dw_gate")
    d_proj = _ple_dw_proj(ps, de, N_DEV)
    landed = {}
    (dh2, df2, hh2, dgu2, dg_f2_post, dg_f2_pre), (landed["w_ple_gate"], landed["w_ple_proj"]) = _ffn_bwd(
        dh3, f2, small["ffn2_post_g"], h2, small["ffn2_pre_g"], gu2, w_gu2, w_down2, "ffn2_bwd",
        side=("exchange", [d_gate, d_proj]))
    d_gu2 = _dw_gu(a3, dgu2, "ffn2_dw_gu")
    d_down2 = _dw_down(hh2, df2, "ffn2_dw_down").reshape(N_DEV, -1, D)
    dmix_a, dmix_b, datt, dg_attn_post, db_out = _outproj_bwd(dh2, att, small["attn_post_g"], w_out_full)
    d_out = _dw_rows(mix, datt, "attn_dw_out")
    (dqa, dka, dva, ds_a, dsinks), _ = _attn_bwd(
        z, bias_a, small["sinks"], dmix_a, mix_a, lse_a, name="attn_a_bwd", **a_cfg)
    (dqb, dkb, dvb, ds_b), (landed["ffn2_w_gu"],) = _attn_bwd(
        z, bias_b, None, dmix_b, mix_b, lse_b, name="attn_b_bwd", **b_cfg, side=("exchange", [d_gu2]))
    (dh1, dz, db_in, dg_attn_pre), (landed["w_out"],) = _inproj_bwd(
        dqa, dka, dva, dqb, dkb, dvb, w_in_full, h1, small["attn_pre_g"], dh2, side=("exchange", [d_out]))
    cols = D_IN // 3
    d_in = _tn_matmul(
        dz, a2, pl.BlockSpec((DW_TILE, cols), lambda b, t: (t, b)), _tok(D),
        jax.ShapeDtypeStruct((D_IN, D), BF16), pl.BlockSpec((cols, D), lambda b, t: (b, 0)),
        3, T // DW_TILE, (cols, D), "attn_dw_in").reshape(N_DEV, D_IN // N_DEV, D)
    (grad_x, df1, hh1, dgu1, dg_f1_post, dg_f1_pre), (landed["w_in"], landed["ffn2_w_down"]) = _ffn_bwd(
        dh1, f1, small["ffn1_post_g"], xs, small["ffn1_pre_g"], gu1, w_gu1, w_down1, "ffn1_bwd",
        side=("exchange", [d_in, d_down2]))
    d_down1 = _dw_down(hh1, df1, "ffn1_dw_down").reshape(N_DEV, -1, D)
    d_gu1, (landed["ffn1_w_down"],) = _dw_gu(a1, dgu1, "ffn1_dw_gu", side=("exchange", [d_down1]))

    rb_a = _bias_grad(ds_a, buckets_a, "bias_grad_a")
    rb_b = _bias_grad(ds_b, buckets_b, "bias_grad_b").reshape(len(PATTERNS_B), N_HEAD_GROUP, NUM_BUCKETS)
    d_rel_bias = jnp.concatenate([rb_a.T, jnp.sum(rb_b, axis=0).T], axis=1)
    small_grads = {"ffn1_pre_g": dg_f1_pre, "ffn1_post_g": dg_f1_post, "attn_pre_g": dg_attn_pre,
                   "attn_post_g": dg_attn_post, "ffn2_pre_g": dg_f2_pre, "ffn2_post_g": dg_f2_post,
                   "ple_pre_g": dg_ple_pre, "ple_post_g": dg_ple_post, "b_out": db_out, "b_in": db_in,
                   "sinks": dsinks, "rel_bias": d_rel_bias}
    d_gu1_pairs = _pair_add(d_gu1, _pair_swap_call(d_gu1), "ffn1_dw_gu_pair_add")
    landed["ffn1_w_gu"], small_gathered = _final_exchange(d_gu1_pairs, small_grads, loss)

    result = {}
    for k in BIG:
        outs = _sum_adamw(landed[k], shards[k], local(ms, k), local(vs, k), _adamw_rows(shards[k].shape[0]),
                          k + "_adamw")
        if k in COLUMN_SHARDED:
            outs = [jnp.swapaxes(o, 0, 1) for o in outs]
        result[k] = [o.reshape(ws[k].shape) for o in outs]
    small_res, loss_all = _small_adamw(
        small_gathered, small, {k: ms[k] for k in SMALL}, {k: vs[k] for k in SMALL})
    result.update(small_res)

    out = [loss_all[0, 0], grad_x.reshape(x.shape)]
    for i in range(4):
        out += [result[k][i] for k in WEIGHTS]
    return tuple(out)
```

```python
import functools
import math

import numpy as np
import jax
import jax.numpy as jnp
from jax import lax
from jax.experimental import pallas as pl
from jax.experimental.pallas import tpu as pltpu

F32 = jnp.float32
BF16 = jnp.bfloat16
MESH = pl.DeviceIdType.MESH

N_DEV = 8
EPS = 1e-6
NEG_INF = -1e30
HEAD_DIM = 64
LANES = 128
QBLK = 128
D_IN = 2304
A_Q, A_KV, B_W = 512, 128, 512
N_HEAD_GROUP = 8
NUM_BUCKETS = 32
MAX_DISTANCE = 2048
PATTERNS_A = ((1, 127),)
PATTERNS_B = ((1, 128), (4, 128), (16, 128))
Q_A_COL, K_A_COL, V_A_COL = 0, 4, 5
Q_B_COL, K_B_COL, V_B_COL = 6, 10, 14

ADAM_LR, ADAM_B1, ADAM_B2, ADAM_EPS, ADAM_WD, ADAM_STEP = 0.001, 0.9, 0.999, 1e-08, 0.01, 10

TOKEN_TILE = 512
DW_TILE = 1024
FWD_BLOCKS = 4
BWD_BLOCKS = 4
VMEM_LIMIT = 56 * 1024 * 1024
ARB = "arbitrary"

BIG = ("ffn1_w_gu", "ffn1_w_down", "w_in", "w_out", "ffn2_w_gu", "ffn2_w_down", "w_ple_gate", "w_ple_proj")
GAINS = ("ffn1_pre_g", "ffn1_post_g", "attn_pre_g", "attn_post_g", "ffn2_pre_g", "ffn2_post_g",
         "ple_pre_g", "ple_post_g", "b_out")
SMALL = GAINS + ("b_in", "sinks", "rel_bias")
WEIGHTS = ("rel_bias", "ffn1_pre_g", "ffn1_w_gu", "ffn1_w_down", "ffn1_post_g", "attn_pre_g", "w_in", "b_in",
           "sinks", "w_out", "b_out", "attn_post_g", "ffn2_pre_g", "ffn2_w_gu", "ffn2_w_down", "ffn2_post_g",
           "ple_pre_g", "w_ple_gate", "w_ple_proj", "ple_post_g")


def _params(n_axes):
    return pltpu.CompilerParams(dimension_semantics=(ARB,) * n_axes, vmem_limit_bytes=VMEM_LIMIT)


def _mm(a, b):
    return jnp.dot(a, b, preferred_element_type=F32)


def _mm_nt(a, b):
    return lax.dot_general(a, b, (((1,), (1,)), ((), ())), preferred_element_type=F32)


def _mm_tn(a, b):
    return lax.dot_general(a, b, (((0,), (0,)), ((), ())), preferred_element_type=F32)


def _rstd(x):
    return lax.rsqrt(jnp.mean(x * x, axis=-1, keepdims=True) + EPS)


def _rms_bwd(x, r, gain, dy):
    n = x * r
    gdy = dy * gain
    return r * (gdy - n * jnp.mean(gdy * n, axis=-1, keepdims=True)), dy * n


def _colsum(v):
    return jnp.sum(v, axis=0, keepdims=True)


def _full(shape):
    return pl.BlockSpec(shape, lambda *_: (0,) * len(shape))


def _mesh_place():
    return lax.axis_index("x"), lax.axis_index("y"), lax.axis_index("c")


def _slot(dev):
    return 4 * dev[0] + 2 * dev[1] + dev[2]


def _peers(x, y, c):
    out = []
    for flip in range(1, N_DEV):
        dx, dy, dc = (flip >> 2) & 1, (flip >> 1) & 1, flip & 1
        out.append((1 - x if dx else x, 1 - y if dy else y, 1 - c if dc else c))
    return out


def _side_copies(kind, ins, outs, send_sems, recv_sems, local_sems, sem_row=0):
    n = len(ins)
    x, y, c = _mesh_place()
    me = _slot((x, y, c))
    peers = _peers(x, y, c)

    def src(a, block):
        return ins[a] if kind == "gather" else ins[a].at[block]

    def send(a, k, peer):
        return pltpu.make_async_remote_copy(
            src_ref=src(a, _slot(peer)), dst_ref=outs[a].at[me],
            send_sem=send_sems.at[sem_row + a, k], recv_sem=recv_sems.at[sem_row + a, k],
            device_id=peer, device_id_type=MESH)

    def arrival(a, k, peer):
        return pltpu.make_async_remote_copy(
            src_ref=src(a, _slot(peer)), dst_ref=outs[a].at[_slot(peer)],
            send_sem=send_sems.at[sem_row + a, k], recv_sem=recv_sems.at[sem_row + a, k],
            device_id=peer, device_id_type=MESH)

    def own(a):
        return pltpu.make_async_copy(src(a, me), outs[a].at[me], local_sems.at[sem_row + a, 0])

    def start():
        for k, peer in enumerate(peers):
            for a in range(n):
                send(a, k, peer).start()
        for a in range(n):
            own(a).start()

    def wait():
        for k, peer in enumerate(peers):
            for a in range(n):
                arrival(a, k, peer).wait_recv()
        for k, peer in enumerate(peers):
            for a in range(n):
                send(a, k, peer).wait_send()
        for a in range(n):
            own(a).wait()

    return start, None, wait


N_CHIPS = N_DEV // 2


def _pair_swap(ins, received, send_sems, recv_sems):
    n = len(ins)
    x, y, c = _mesh_place()
    sibling = (x, y, 1 - c)

    def send(a, q):
        return pltpu.make_async_remote_copy(
            src_ref=ins[a].at[2 * q + (1 - c)], dst_ref=received[a].at[q],
            send_sem=send_sems.at[a, q], recv_sem=recv_sems.at[a, q], device_id=sibling, device_id_type=MESH)

    def start():
        for a in range(n):
            for q in range(N_CHIPS):
                send(a, q).start()

    def wait():
        for a in range(n):
            for q in range(N_CHIPS):
                send(a, q).wait_recv()
        for a in range(n):
            for q in range(N_CHIPS):
                send(a, q).wait_send()

    return start, None, wait


def _quad_exchange(ins, outs, send_sems, recv_sems, local_sems, sem_row=0):
    n = len(ins)
    x, y, c = _mesh_place()
    mine = 2 * x + y
    chips = [(1 - x, y), (x, 1 - y), (1 - x, 1 - y)]

    def send(a, k, chip):
        return pltpu.make_async_remote_copy(
            src_ref=ins[a].at[2 * chip[0] + chip[1]], dst_ref=outs[a].at[mine],
            send_sem=send_sems.at[sem_row + a, k], recv_sem=recv_sems.at[sem_row + a, k],
            device_id=(chip[0], chip[1], c), device_id_type=MESH)

    def arrival(a, k, chip):
        return pltpu.make_async_remote_copy(
            src_ref=ins[a].at[2 * chip[0] + chip[1]], dst_ref=outs[a].at[2 * chip[0] + chip[1]],
            send_sem=send_sems.at[sem_row + a, k], recv_sem=recv_sems.at[sem_row + a, k],
            device_id=(chip[0], chip[1], c), device_id_type=MESH)

    def own(a):
        return pltpu.make_async_copy(ins[a].at[mine], outs[a].at[mine], local_sems.at[sem_row + a, 0])

    def start():
        for k, chip in enumerate(chips):
            for a in range(n):
                send(a, k, chip).start()
        for a in range(n):
            own(a).start()

    def wait():
        for k, chip in enumerate(chips):
            for a in range(n):
                arrival(a, k, chip).wait_recv()
        for k, chip in enumerate(chips):
            for a in range(n):
                send(a, k, chip).wait_send()
        for a in range(n):
            own(a).wait()

    return start, None, wait


def _relay_gather(ins, outs, send_sems, recv_sems, local_sems):
    n = len(ins)
    x, y, c = _mesh_place()
    me, sibling = (x, y, c), (x, y, 1 - c)
    chips = [(1 - x, y), (x, 1 - y), (1 - x, 1 - y)]

    def copy(a, k, block, to, src=None):
        dst = outs[a].at[_slot(block)]
        return pltpu.make_async_remote_copy(
            src_ref=dst if src is None else src, dst_ref=dst,
            send_sem=send_sems.at[a, k], recv_sem=recv_sems.at[a, k], device_id=to, device_id_type=MESH)

    def own(a):
        return pltpu.make_async_copy(ins[a], outs[a].at[_slot(me)], local_sems.at[a, 0])

    def start():
        for j, chip in enumerate(chips):
            for a in range(n):
                copy(a, 1 + j, me, (*chip, c), src=ins[a]).start()
        for a in range(n):
            copy(a, 0, me, sibling, src=ins[a]).start()
            own(a).start()

    def relay():
        for j, chip in enumerate(chips):
            for a in range(n):
                copy(a, 1 + j, (*chip, c), me).wait_recv()
                copy(a, 4 + j, (*chip, c), sibling).start()

    def wait():
        for a in range(n):
            copy(a, 0, sibling, me).wait_recv()
        for j, chip in enumerate(chips):
            for a in range(n):
                copy(a, 4 + j, (*chip, 1 - c), me).wait_recv()
        for j, chip in enumerate(chips):
            for a in range(n):
                copy(a, 1 + j, me, (*chip, c), src=ins[a]).wait_send()
                copy(a, 4 + j, (*chip, c), sibling).wait_send()
        for a in range(n):
            copy(a, 0, me, sibling, src=ins[a]).wait_send()
            own(a).wait()

    return start, relay, wait


def _side_out_shapes(kind, arrays):
    if kind in ("gather", "relay_gather"):
        return [jax.ShapeDtypeStruct((N_DEV,) + a.shape, a.dtype) for a in arrays]
    return [jax.ShapeDtypeStruct(a.shape, a.dtype) for a in arrays]


def _hosted_call(body, name, grid, in_specs, out_specs, out_shape, scratch_shapes, args, side=None):
    if side is None:
        outs = pl.pallas_call(
            body, name=name, grid=grid, in_specs=in_specs, out_specs=out_specs, out_shape=out_shape,
            scratch_shapes=scratch_shapes, compiler_params=_params(len(grid)))(*args)
        return outs, []
    kind, arrays = side
    side_shapes = _side_out_shapes(kind, arrays)
    n_in, n_out, n_scr, n_side = len(in_specs), len(out_specs), len(scratch_shapes), len(arrays)

    def hosted(*refs):
        pos = 0
        groups = []
        for size in (n_in, n_side, n_out, len(side_shapes), n_scr):
            groups.append(refs[pos:pos + size])
            pos += size
        ins, side_in, outs, side_out, scr = groups
        send_sems, recv_sems, local_sems = refs[pos:]
        ids = [pl.program_id(d) for d in range(len(grid))]
        is_first = functools.reduce(jnp.logical_and, [i == 0 for i in ids])
        is_last = functools.reduce(jnp.logical_and, [i == g - 1 for i, g in zip(ids, grid)])
        if kind == "relay_gather":
            start, relay, wait = _relay_gather(side_in, side_out, send_sems, recv_sems, local_sems)
        else:
            start, relay, wait = _side_copies(kind, side_in, side_out, send_sems, recv_sems, local_sems)
        pl.when(is_first)(start)
        if relay is not None:
            pl.when(is_last)(relay)
        body(*ins, *outs, *scr)
        pl.when(is_last)(wait)

    any_spec = pl.BlockSpec(memory_space=pl.ANY)
    outs = pl.pallas_call(
        hosted, name=name, grid=grid,
        in_specs=list(in_specs) + [any_spec] * n_side,
        out_specs=list(out_specs) + [any_spec] * len(side_shapes),
        out_shape=list(out_shape) + side_shapes,
        scratch_shapes=list(scratch_shapes) + [pltpu.SemaphoreType.DMA((n_side, 7)), pltpu.SemaphoreType.DMA((n_side, 7)),
                                               pltpu.SemaphoreType.DMA((n_side, N_CHIPS))],
        compiler_params=_params(len(grid)))(*args, *arrays)
    return outs[:n_out], outs[n_out:]


def _lane_chunks(width, chunk=2 * LANES):
    return [slice(n0, min(n0 + chunk, width)) for n0 in range(0, width, chunk)]


def _pipelined(chunks, first, middle, last):
    n = len(chunks)
    a, b, total = {}, {}, None
    for step in range(n + 2):
        if step < n:
            a[step] = first(chunks[step])
        if 0 <= step - 1 < n:
            b[step - 1] = middle(chunks[step - 1], a.pop(step - 1))
        if 0 <= step - 2 < n:
            part = last(chunks[step - 2], b.pop(step - 2))
            total = part if total is None else total + part
    return total


def _ffn_fwd(h, g_pre, g_post, w_gu, w_down, name, side=None, attn=None):
    T, D = h.shape
    nj = w_gu.shape[0] // 2
    FB = w_gu.shape[1]
    tm = TOKEN_TILE
    n_attn = 0 if attn is None else 5

    def body(*refs):
        h_ref, gpre_ref, gpost_ref, wg_ref, wu_ref, wd_ref = refs[:6]
        hout_ref, f_ref, a_ref, gu_ref = refs[6 + n_attn:10 + n_attn]
        a_scr, acc = refs[-2:]
        j = pl.program_id(1)

        @pl.when(j == 0)
        def _():
            if attn is None:
                x = h_ref[...]
            else:
                ma_ref, mb_ref, wo_ref, bo_ref, ga_ref = refs[6:11]
                att_ref, hmid_ref, mix_ref = refs[15:18]
                mix = jnp.concatenate([ma_ref[...], mb_ref[...]], axis=1).astype(BF16)
                mix_ref[...] = mix
                att = _mm(mix, wo_ref[...]) + bo_ref[...]
                att_ref[...] = att
                x = h_ref[...] + att * _rstd(att) * ga_ref[...]
                hmid_ref[...] = x
            a = (x * _rstd(x) * gpre_ref[...]).astype(BF16)
            a_scr[...] = a
            a_ref[...] = a
            acc[...] = jnp.zeros_like(acc)

        a = a_scr[...]
        g = _mm_nt(a, wg_ref[...])
        u = _mm_nt(a, wu_ref[...])
        gu_ref[0] = g.astype(BF16)
        gu_ref[1] = u.astype(BF16)
        hh = (g * jax.nn.sigmoid(g) * u).astype(BF16)
        acc[...] += _mm(hh, wd_ref[...])

        @pl.when(j == nj - 1)
        def _():
            f = acc[...]
            f_ref[...] = f
            x = h_ref[...] if attn is None else refs[16][...]
            hout_ref[...] = x + 0.5 * (f * _rstd(f) * gpost_ref[...])

    tile = pl.BlockSpec((tm, D), lambda i, j: (i, 0))
    in_specs = [tile, _full((1, D)), _full((1, D)),
                pl.BlockSpec((None, FB, D), lambda i, j: (j, 0, 0)),
                pl.BlockSpec((None, FB, D), lambda i, j: (j + nj, 0, 0)),
                pl.BlockSpec((FB, D), lambda i, j: (j, 0))]
    out_specs = [tile, tile, tile, pl.BlockSpec((None, 2, tm, FB), lambda i, j: (j, 0, i, 0))]
    out_shape = [
        jax.ShapeDtypeStruct((T, D), F32),
        jax.ShapeDtypeStruct((T, D), F32),
        jax.ShapeDtypeStruct((T, D), BF16),
        jax.ShapeDtypeStruct((nj, 2, T, FB), BF16),
    ]
    args = [h, g_pre, g_post, w_gu, w_gu, w_down]
    if attn is not None:
        mix_a, mix_b, w_out, b_out, g_attn = attn
        d_mix = w_out.shape[0]
        in_specs += [pl.BlockSpec((tm, mix_a.shape[1]), lambda i, j: (i, 0)),
                     pl.BlockSpec((tm, mix_b.shape[1]), lambda i, j: (i, 0)),
                     _full((d_mix, D)), _full((1, D)), _full((1, D))]
        out_specs += [tile, tile, pl.BlockSpec((tm, d_mix), lambda i, j: (i, 0))]
        out_shape += [jax.ShapeDtypeStruct((T, D), F32),
                      jax.ShapeDtypeStruct((T, D), F32),
                      jax.ShapeDtypeStruct((T, d_mix), BF16)]
        args += [mix_a, mix_b, w_out, b_out, g_attn]
    return _hosted_call(
        body, name, (T // tm, nj), in_specs=in_specs, out_specs=out_specs, out_shape=out_shape,
        scratch_shapes=[pltpu.VMEM((tm, D), BF16), pltpu.VMEM((tm, D), F32)], args=args, side=side)


def _ffn_bwd(dh_out, f, g_post, h, g_pre, gu, w_gu, w_down, name, side=None):
    T, D = h.shape
    nj = w_gu.shape[0] // 2
    FB = w_gu.shape[1]
    tm = TOKEN_TILE

    def body(dho_ref, f_ref, gpost_ref, h_ref, gpre_ref, gu_ref, wg_ref, wu_ref, wd_ref,
             dhin_ref, df_ref, hh_ref, dgu_ref, dgpost_ref, dgpre_ref, df_scr, da):
        i, j = pl.program_id(0), pl.program_id(1)

        @pl.when(jnp.logical_and(i == 0, j == 0))
        def _():
            dgpost_ref[...] = jnp.zeros_like(dgpost_ref)
            dgpre_ref[...] = jnp.zeros_like(dgpre_ref)

        @pl.when(j == 0)
        def _():
            fv = f_ref[...]
            df, dgain = _rms_bwd(fv, _rstd(fv), gpost_ref[...], 0.5 * dho_ref[...])
            dgpost_ref[...] += _colsum(dgain)
            dfb = df.astype(BF16)
            df_scr[...] = dfb
            df_ref[...] = dfb
            da[...] = jnp.zeros_like(da)

        dfb = df_scr[...]

        halves = (slice(0, tm // 2), slice(tm // 2, tm))

        def hidden_grad(c):
            return [_mm_nt(dfb[rows], wd_ref[c, :]) for rows in halves]

        def through_swiglu(c, dhh):
            dhh = jnp.concatenate(dhh, axis=0)
            g = gu_ref[0, :, c].astype(F32)
            u = gu_ref[1, :, c].astype(F32)
            sg = jax.nn.sigmoid(g)
            silu = g * sg
            hh_ref[:, c] = (silu * u).astype(BF16)
            dg = (dhh * u * (sg * (1.0 + (g - silu)))).astype(BF16)
            du = (dhh * silu).astype(BF16)
            dgu_ref[0, :, c] = dg
            dgu_ref[1, :, c] = du
            return dg, du

        def input_grad(c, dgu):
            return jnp.concatenate(
                [_mm(dgu[0][rows], wg_ref[c, :]) + _mm(dgu[1][rows], wu_ref[c, :]) for rows in halves], axis=0)

        da[...] += _pipelined(_lane_chunks(FB), hidden_grad, through_swiglu, input_grad)

        @pl.when(j == nj - 1)
        def _():
            x = h_ref[...]
            dx, dgain = _rms_bwd(x, _rstd(x), gpre_ref[...], da[...])
            dgpre_ref[...] += _colsum(dgain)
            dhin_ref[...] = dho_ref[...] + dx

    tile = pl.BlockSpec((tm, D), lambda i, j: (i, 0))
    return _hosted_call(
        body, name, (T // tm, nj),
        in_specs=[
            tile, tile, _full((1, D)), tile, _full((1, D)),
            pl.BlockSpec((None, 2, tm, FB), lambda i, j: (j, 0, i, 0)),
            pl.BlockSpec((None, FB, D), lambda i, j: (j, 0, 0)),
            pl.BlockSpec((None, FB, D), lambda i, j: (j + nj, 0, 0)),
            pl.BlockSpec((FB, D), lambda i, j: (j, 0)),
        ],
        out_specs=[
            tile, tile,
            pl.BlockSpec((None, tm, FB), lambda i, j: (j, i, 0)),
            pl.BlockSpec((None, 2, tm, FB), lambda i, j: (j, 0, i, 0)),
            _full((1, D)), _full((1, D)),
        ],
        out_shape=[
            jax.ShapeDtypeStruct((T, D), F32),
            jax.ShapeDtypeStruct((T, D), BF16),
            jax.ShapeDtypeStruct((nj, T, FB), BF16),
            jax.ShapeDtypeStruct((nj, 2, T, FB), BF16),
            jax.ShapeDtypeStruct((1, D), F32),
            jax.ShapeDtypeStruct((1, D), F32),
        ],
        scratch_shapes=[pltpu.VMEM((tm, D), BF16), pltpu.VMEM((tm, D), F32)],
        args=(dh_out, f, g_post, h, g_pre, gu, w_gu, w_gu, w_down), side=side)


def _tn_matmul(x, y, x_spec, y_spec, out_shape, out_spec, n_blocks, n_steps, acc_shape, name, side=None):
    def body(x_ref, y_ref, o_ref, acc):
        t = pl.program_id(1)

        @pl.when(t == 0)
        def _():
            acc[...] = jnp.zeros_like(acc)

        acc[...] += _mm_tn(x_ref[...].astype(BF16), y_ref[...].astype(BF16))

        @pl.when(t == n_steps - 1)
        def _():
            o_ref[...] = acc[...].astype(o_ref.dtype)

    outs, side_outs = _hosted_call(
        body, name, (n_blocks, n_steps), in_specs=[x_spec, y_spec], out_specs=[out_spec], out_shape=[out_shape],
        scratch_shapes=[pltpu.VMEM(acc_shape, F32)], args=(x, y), side=side)
    return (outs[0], side_outs) if side is not None else outs[0]


def _inproj_fwd(h, g_pre, w_in, b_in, side=None):
    T, D = h.shape
    tm = TOKEN_TILE

    def body(h_ref, g_ref, w_ref, b_ref, z_ref, a_ref):
        x = h_ref[...]
        a = (x * _rstd(x) * g_ref[...]).astype(BF16)
        a_ref[...] = a
        z_ref[...] = _mm_nt(a, w_ref[...]) + b_ref[...]

    return _hosted_call(
        body, "inproj_fwd", (T // tm,),
        in_specs=[pl.BlockSpec((tm, D), lambda i: (i, 0)), _full((1, D)), _full((D_IN, D)), _full((1, D_IN))],
        out_specs=[pl.BlockSpec((tm, D_IN), lambda i: (i, 0)), pl.BlockSpec((tm, D), lambda i: (i, 0))],
        out_shape=[jax.ShapeDtypeStruct((T, D_IN), F32), jax.ShapeDtypeStruct((T, D), BF16)],
        scratch_shapes=[], args=(h, g_pre, w_in, b_in), side=side)


def _inproj_bwd(dqa, dka, dva, dqb, dkb, dvb, w_in, h, g_pre, dres, side=None):
    T, D = h.shape
    tm = TOKEN_TILE

    def body(dqa_ref, dka_ref, dva_ref, dqb_ref, dkb_ref, dvb_ref, w_ref, h_ref, g_ref, dres_ref,
             dh_ref, dz_ref, dbin_ref, dg_ref):
        i = pl.program_id(0)

        @pl.when(i == 0)
        def _():
            dbin_ref[...] = jnp.zeros_like(dbin_ref)
            dg_ref[...] = jnp.zeros_like(dg_ref)

        dz = jnp.concatenate([dqa_ref[...], dka_ref[...], dva_ref[...], dqb_ref[...], dkb_ref[...], dvb_ref[...]],
                             axis=1)
        dbin_ref[...] += _colsum(dz)
        dzb = dz.astype(BF16)
        dz_ref[...] = dzb
        da = _mm(dzb, w_ref[...])
        x = h_ref[...]
        dx, dgain = _rms_bwd(x, _rstd(x), g_ref[...], da)
        dg_ref[...] += _colsum(dgain)
        dh_ref[...] = dres_ref[...] + dx

    def tile(w):
        return pl.BlockSpec((tm, w), lambda i: (i, 0))

    return _hosted_call(
        body, "inproj_bwd", (T // tm,),
        in_specs=[tile(A_Q), tile(A_KV), tile(A_KV), tile(B_W), tile(B_W), tile(B_W),
                  _full((D_IN, D)), tile(D), _full((1, D)), tile(D)],
        out_specs=[tile(D), tile(D_IN), _full((1, D_IN)), _full((1, D))],
        out_shape=[jax.ShapeDtypeStruct((T, D), F32), jax.ShapeDtypeStruct((T, D_IN), BF16),
                   jax.ShapeDtypeStruct((1, D_IN), F32), jax.ShapeDtypeStruct((1, D), F32)],
        scratch_shapes=[], args=(dqa, dka, dva, dqb, dkb, dvb, w_in, h, g_pre, dres), side=side)


def _bucket_tiles(patterns):
    i = np.arange(QBLK)[:, None]
    j = np.arange(2 * QBLK)[None, :]
    dist = QBLK + i - j
    max_exact = NUM_BUCKETS // 2
    tiles = []
    for dilation, max_dist in patterns:
        n = np.maximum(dist * dilation, 0)
        nf = np.maximum(n, 1).astype(np.float32)
        large = max_exact + (np.log(nf / np.float32(max_exact)) / np.float32(math.log(MAX_DISTANCE / max_exact))
                             * np.float32(NUM_BUCKETS - max_exact)).astype(np.int32)
        bucket = np.where(n < max_exact, n, np.minimum(large, NUM_BUCKETS - 1))
        tiles.append(np.where((dist >= 0) & (dist <= max_dist), bucket, -1))
    return jnp.asarray(np.stack(tiles).astype(np.int32))


def _bias_build(rel_bias, buckets, head0, name, side=None):
    n = buckets.shape[0]

    def body(bk_ref, rb_ref, o_ref):
        bk = bk_ref[...]
        base = jnp.where(bk < 0, NEG_INF, 0.0).astype(F32)
        for hd in range(N_HEAD_GROUP):
            o_ref[hd] = lax.fori_loop(
                0, NUM_BUCKETS, lambda b, acc, hd=hd: jnp.where(bk == b, rb_ref[b, head0 + hd], acc), base)

    outs, side_outs = _hosted_call(
        body, name, (n,),
        in_specs=[pl.BlockSpec((None, QBLK, 2 * QBLK), lambda p: (p, 0, 0)), pl.BlockSpec(memory_space=pltpu.SMEM)],
        out_specs=[pl.BlockSpec((None, N_HEAD_GROUP, QBLK, 2 * QBLK), lambda p: (p, 0, 0, 0))],
        out_shape=[jax.ShapeDtypeStruct((n, N_HEAD_GROUP, QBLK, 2 * QBLK), F32)],
        scratch_shapes=[], args=(buckets, rel_bias), side=side)
    return outs[0], side_outs


def _bias_grad(ds, buckets, name):
    n = buckets.shape[0]

    def body(ds_ref, bk_ref, o_ref):
        bk = bk_ref[...]
        row = lax.broadcasted_iota(jnp.int32, (NUM_BUCKETS, 2 * QBLK), 0)
        for hd in range(N_HEAD_GROUP):
            d = ds_ref[hd]
            per_key = jnp.zeros((NUM_BUCKETS, 2 * QBLK), F32)
            for b in range(NUM_BUCKETS):
                per_key = jnp.where(row == b, jnp.sum(jnp.where(bk == b, d, 0.0), axis=0, keepdims=True), per_key)
            o_ref[hd] = jnp.broadcast_to(jnp.sum(per_key, axis=1, keepdims=True), (NUM_BUCKETS, LANES))

    out = pl.pallas_call(
        body, name=name, grid=(n,),
        in_specs=[pl.BlockSpec((None, N_HEAD_GROUP, QBLK, 2 * QBLK), lambda p: (p, 0, 0, 0)),
                  pl.BlockSpec((None, QBLK, 2 * QBLK), lambda p: (p, 0, 0))],
        out_specs=pl.BlockSpec((None, N_HEAD_GROUP, NUM_BUCKETS, LANES), lambda p: (p, 0, 0, 0)),
        out_shape=jax.ShapeDtypeStruct((n, N_HEAD_GROUP, NUM_BUCKETS, LANES), F32),
        compiler_params=_params(1),
    )(ds, buckets)
    return out[:, :, :, 0].reshape(n * N_HEAD_GROUP, NUM_BUCKETS)


def _class_rows(start, dilation):
    if dilation == 1:
        return pl.ds(pl.multiple_of(start, QBLK), QBLK)
    return pl.ds(start, QBLK, stride=dilation)


def _starts_class(u, blocks_per_pass, n_blocks):
    return blocks_per_pass % n_blocks == 0 and u % n_blocks == 0


def _block_starts(idx, n_blocks, dilation):
    cls = idx // n_blocks
    n = idx % n_blocks
    cur = cls + dilation * QBLK * n
    prev = cls + dilation * QBLK * jnp.maximum(n - 1, 0)
    return n, cur, prev


class _HeadPair:
    def __init__(self, g, shared_kv):
        self.lane = lax.broadcasted_iota(jnp.int32, (1, LANES), 1)
        self.lower = self.lane < HEAD_DIM
        self.shared_kv = shared_kv
        self.key_lanes = (self.lane >= HEAD_DIM).astype(jnp.int32) == (g // 2)

    def stack(self, t):
        return jnp.concatenate([jnp.where(self.lower, t, 0.0), jnp.where(self.lower, 0.0, t)], axis=0).astype(BF16)

    def unstack(self, t2):
        return jnp.where(self.lower, t2[:QBLK], t2[QBLK:])

    def keys(self, t):
        if self.shared_kv:
            return jnp.where(self.key_lanes, t, pltpu.roll(t, HEAD_DIM, 1))
        return t

    def key_grads(self, t):
        if self.shared_kv:
            return jnp.where(self.key_lanes, t + pltpu.roll(t, HEAD_DIM, 1), 0.0)
        return t


def _attn_specs(T, qcol, kcol, vcol, shared_kv):
    kv = (lambda c: (lambda g: (0, c))) if shared_kv else (lambda c: (lambda g: (0, c + g)))
    return [pl.BlockSpec((T, LANES), lambda g: (0, qcol + g)),
            pl.BlockSpec((T, LANES), kv(kcol)),
            pl.BlockSpec((T, LANES), kv(vcol))]


def _attn_fwd(z, bias, sinks, patterns, qcol, kcol, vcol, shared_kv, name, side=None):
    T = z.shape[0]
    n_pat = len(patterns)
    has_sink = sinks is not None

    def body(*refs):
        if has_sink:
            sink_ref, refs = refs[0], refs[1:]
        q_ref, k_ref, v_ref, b_ref, o_ref, l_ref = refs[:6]
        po_scr = refs[6:6 + n_pat]
        pl_scr = refs[6 + n_pat:]
        g = pl.program_id(0)
        heads = _HeadPair(g, shared_kv)
        in_prev = lax.broadcasted_iota(jnp.int32, (2 * QBLK, 2 * QBLK), 1) < QBLK

        for pi, (dilation, _) in enumerate(patterns):
            n_blocks = T // (QBLK * dilation)

            def step(it, carry, pi=pi, dilation=dilation, n_blocks=n_blocks):
                blocks = []
                for u in range(FWD_BLOCKS):
                    n, cur, prev = _block_starts(it * FWD_BLOCKS + u, n_blocks, dilation)
                    rows_c, rows_p = _class_rows(cur, dilation), _class_rows(prev, dilation)
                    qm = heads.stack(q_ref[rows_c, :])
                    k_cur, v_cur = k_ref[rows_c, :], v_ref[rows_c, :]
                    no_past = _starts_class(u, FWD_BLOCKS, n_blocks)
                    if no_past:
                        k2, v2 = k_cur, v_cur
                    else:
                        if u % min(FWD_BLOCKS, n_blocks) == 0:
                            k_prev, v_prev = k_ref[rows_p, :], v_ref[rows_p, :]
                        k2 = jnp.concatenate([k_prev, k_cur], axis=0)
                        v2 = jnp.concatenate([v_prev, v_cur], axis=0)
                    k2, v2 = heads.keys(k2).astype(BF16), heads.keys(v2).astype(BF16)
                    k_prev, v_prev = k_cur, v_cur
                    blocks.append(dict(n=n, no_past=no_past, rows=rows_c, v2=v2, s=_mm_nt(qm, k2)))
                for b in blocks:
                    if b["no_past"]:
                        b["s"] = b["s"] * (HEAD_DIM ** -0.5) + b_ref[pi, :, QBLK:]
                    else:
                        s = b["s"] * (HEAD_DIM ** -0.5) + b_ref[pi]
                        b["s"] = jnp.where(jnp.logical_and(in_prev, b["n"] == 0), NEG_INF, s)
                    b["m"] = jnp.max(b["s"], axis=1, keepdims=True)
                for b in blocks:
                    b["pr"] = jnp.exp(b["s"] - b["m"])
                    b["den"] = jnp.sum(b["pr"], axis=1, keepdims=True)
                for b in blocks:
                    b["o2"] = _mm(b["pr"].astype(BF16), b["v2"])
                for b in blocks:
                    lse = b["m"] + jnp.log(b["den"])
                    po_scr[pi][b["rows"], :] = heads.unstack(b["o2"] / b["den"])
                    pl_scr[2 * pi][b["rows"], :] = jnp.broadcast_to(lse[:QBLK], (QBLK, LANES))
                    pl_scr[2 * pi + 1][b["rows"], :] = jnp.broadcast_to(lse[QBLK:], (QBLK, LANES))
                return carry

            lax.fori_loop(0, (dilation * n_blocks) // FWD_BLOCKS, step, 0)

        def merge(ci, carry):
            rows = pl.ds(pl.multiple_of(ci * QBLK, QBLK), QBLK)
            weights = []
            for hd in range(2):
                parts = [pl_scr[2 * pi + hd][rows, :] for pi in range(n_pat)]
                m = functools.reduce(jnp.maximum, parts)
                if has_sink:
                    sink = sink_ref[0, 2 * g + hd]
                    m = jnp.maximum(m, sink)
                terms = [jnp.exp(x - m) for x in parts]
                den = functools.reduce(jnp.add, terms)
                if has_sink:
                    den = den + jnp.exp(sink - m)
                l_ref[hd, rows, :] = m + jnp.log(den)
                inv = 1.0 / den
                weights.append([t * inv for t in terms])
            o_ref[rows, :] = functools.reduce(
                jnp.add, [jnp.where(heads.lower, weights[0][pi], weights[1][pi]) * po_scr[pi][rows, :]
                          for pi in range(n_pat)])
            return carry

        lax.fori_loop(0, T // QBLK, merge, 0)

    in_specs = _attn_specs(T, qcol, kcol, vcol, shared_kv)
    in_specs.append(pl.BlockSpec((n_pat, None, 2 * QBLK, 2 * QBLK), lambda g: (0, g, 0, 0)))
    args = [z, z, z, bias.reshape(n_pat, N_HEAD_GROUP // 2, 2 * QBLK, 2 * QBLK)]
    if has_sink:
        in_specs.insert(0, pl.BlockSpec(memory_space=pltpu.SMEM))
        args.insert(0, sinks)
    return _hosted_call(
        body, name, (N_HEAD_GROUP // 2,),
        in_specs=in_specs,
        out_specs=[pl.BlockSpec((T, LANES), lambda g: (0, g)), pl.BlockSpec((2, T, LANES), lambda g: (g, 0, 0))],
        out_shape=[jax.ShapeDtypeStruct((T, N_HEAD_GROUP * HEAD_DIM), F32),
                   jax.ShapeDtypeStruct((N_HEAD_GROUP, T, LANES), F32)],
        scratch_shapes=[pltpu.VMEM((T, LANES), F32)] * (3 * n_pat), args=args, side=side)


def _attn_bwd(z, bias, sinks, d_out, out, lse, patterns, qcol, kcol, vcol, shared_kv, name, side=None):
    T = z.shape[0]
    n_pat = len(patterns)
    has_sink = sinks is not None
    kv_width = LANES if shared_kv else N_HEAD_GROUP * HEAD_DIM

    def body(*refs):
        if has_sink:
            sink_ref, refs = refs[0], refs[1:]
        q_ref, k_ref, v_ref, b_ref, do_ref, o_ref, l0_ref, l1_ref = refs[:8]
        dq_ref, dk_ref, dv_ref, ds_ref = refs[8:12]
        dsink_ref = refs[12] if has_sink else None
        dk_acc, dv_acc = refs[-2:]
        g = pl.program_id(0)
        heads = _HeadPair(g, shared_kv)
        in_prev = lax.broadcasted_iota(jnp.int32, (2 * QBLK, 2 * QBLK), 1) < QBLK

        dq_ref[...] = jnp.zeros_like(dq_ref)
        ds_ref[...] = jnp.zeros_like(ds_ref)
        dk_acc[...] = jnp.zeros_like(dk_acc)
        dv_acc[...] = jnp.zeros_like(dv_acc)

        dsink = jnp.zeros((1, LANES), F32)
        for pi, (dilation, _) in enumerate(patterns):
            n_blocks = T // (QBLK * dilation)

            def step(idx, dsink, pi=pi, dilation=dilation, n_blocks=n_blocks):
                blocks = []
                for u in range(BWD_BLOCKS):
                    n, cur, prev = _block_starts(idx * BWD_BLOCKS + u, n_blocks, dilation)
                    rows_c, rows_p = _class_rows(cur, dilation), _class_rows(prev, dilation)
                    qm = heads.stack(q_ref[rows_c, :])
                    k_cur, v_cur = k_ref[rows_c, :], v_ref[rows_c, :]
                    first = u % min(BWD_BLOCKS, n_blocks) == 0
                    no_past = _starts_class(u, BWD_BLOCKS, n_blocks)
                    if no_past:
                        k2, v2 = k_cur, v_cur
                    else:
                        if first:
                            k_prev, v_prev = k_ref[rows_p, :], v_ref[rows_p, :]
                        k2 = jnp.concatenate([k_prev, k_cur], axis=0)
                        v2 = jnp.concatenate([v_prev, v_cur], axis=0)
                    k2, v2 = heads.keys(k2).astype(BF16), heads.keys(v2).astype(BF16)
                    k_prev, v_prev = k_cur, v_cur
                    d_o = do_ref[rows_c, :]
                    dom = heads.stack(d_o)
                    dd = d_o * o_ref[rows_c, :]
                    delta = jnp.concatenate([jnp.sum(jnp.where(heads.lower, dd, 0.0), axis=1, keepdims=True),
                                             jnp.sum(jnp.where(heads.lower, 0.0, dd), axis=1, keepdims=True)], axis=0)
                    lse = jnp.concatenate([l0_ref[rows_c, :], l1_ref[rows_c, :]], axis=0)
                    blocks.append(dict(n=n, first=first, no_past=no_past, rows_c=rows_c, rows_p=rows_p, qm=qm, k2=k2,
                                       dom=dom, delta=delta, lse=lse, s=_mm_nt(qm, k2), dp=_mm_nt(dom, v2)))
                for b in blocks:
                    if b["no_past"]:
                        s = b["s"] * (HEAD_DIM ** -0.5) + b_ref[pi, :, QBLK:]
                        b["pr"] = jnp.exp(s - b["lse"])
                    else:
                        s = b["s"] * (HEAD_DIM ** -0.5) + b_ref[pi]
                        s = jnp.where(jnp.logical_and(in_prev, b["n"] == 0), NEG_INF, s)
                        b["pr"] = jnp.exp(s - jnp.concatenate([b["lse"], b["lse"]], axis=1))
                    b["ds"] = b["pr"] * (b["dp"] - b["delta"])
                for b in blocks:
                    dsb = b["ds"].astype(BF16)
                    b["dq2"] = _mm(dsb, b["k2"])
                    b["dk2"] = _mm_tn(dsb, b["qm"])
                    b["dv2"] = _mm_tn(b["pr"].astype(BF16), b["dom"])
                for b in blocks:
                    b["dk2"] = heads.key_grads(b["dk2"]) * (HEAD_DIM ** -0.5)
                    b["dv2"] = heads.key_grads(b["dv2"])
                for u, b in enumerate(blocks):
                    dq_ref[b["rows_c"], :] += heads.unstack(b["dq2"]) * (HEAD_DIM ** -0.5)
                    if b["no_past"]:
                        ds_ref[pi, :, QBLK:] += b["ds"]
                        dk_own, dv_own = b["dk2"], b["dv2"]
                    else:
                        ds_ref[pi] += b["ds"]
                        dk_own, dv_own = b["dk2"][QBLK:], b["dv2"][QBLK:]
                    if u + 1 < len(blocks) and not blocks[u + 1]["first"]:
                        dk_own = dk_own + blocks[u + 1]["dk2"][:QBLK]
                        dv_own = dv_own + blocks[u + 1]["dv2"][:QBLK]
                    if b["first"] and not b["no_past"]:
                        dk_acc[b["rows_p"], :] += b["dk2"][:QBLK]
                        dv_acc[b["rows_p"], :] += b["dv2"][:QBLK]
                    dk_acc[b["rows_c"], :] += dk_own
                    dv_acc[b["rows_c"], :] += dv_own
                    if has_sink:
                        for hd in range(2):
                            rows_h = slice(QBLK * hd, QBLK * (hd + 1))
                            p_sink = jnp.exp(sink_ref[0, 2 * g + hd] - b["lse"][rows_h, 0:1])
                            dsink = dsink - jnp.where(heads.lane == 2 * g + hd,
                                                      jnp.sum(p_sink * b["delta"][rows_h]), 0.0)
                return dsink

            dsink = lax.fori_loop(0, (dilation * n_blocks) // BWD_BLOCKS, step, dsink)

        if shared_kv:
            @pl.when(g == 0)
            def _():
                dk_ref[...] = dk_acc[...]
                dv_ref[...] = dv_acc[...]

            @pl.when(g != 0)
            def _():
                dk_ref[...] += dk_acc[...]
                dv_ref[...] += dv_acc[...]
        else:
            dk_ref[...] = dk_acc[...]
            dv_ref[...] = dv_acc[...]

        if has_sink:
            @pl.when(g == 0)
            def _():
                dsink_ref[...] = dsink

            @pl.when(g != 0)
            def _():
                dsink_ref[...] += dsink

    pair = pl.BlockSpec((T, LANES), lambda g: (0, g))
    stacked = pl.BlockSpec((n_pat, None, 2 * QBLK, 2 * QBLK), lambda g: (0, g, 0, 0))
    stacked_shape = (n_pat, N_HEAD_GROUP // 2, 2 * QBLK, 2 * QBLK)
    in_specs = _attn_specs(T, qcol, kcol, vcol, shared_kv)
    in_specs += [stacked, pair, pair,
                 pl.BlockSpec((None, T, LANES), lambda g: (2 * g, 0, 0)),
                 pl.BlockSpec((None, T, LANES), lambda g: (2 * g + 1, 0, 0))]
    args = [z, z, z, bias.reshape(stacked_shape), d_out, out, lse, lse]
    kv_out = _full((T, LANES)) if shared_kv else pair
    out_specs = [pair, kv_out, kv_out, stacked]
    out_shape = [jax.ShapeDtypeStruct((T, N_HEAD_GROUP * HEAD_DIM), F32),
                 jax.ShapeDtypeStruct((T, kv_width), F32), jax.ShapeDtypeStruct((T, kv_width), F32),
                 jax.ShapeDtypeStruct(stacked_shape, F32)]
    if has_sink:
        in_specs.insert(0, pl.BlockSpec(memory_space=pltpu.SMEM))
        args.insert(0, sinks)
        out_specs.append(_full((1, LANES)))
        out_shape.append(jax.ShapeDtypeStruct((1, LANES), F32))
    outs, side_outs = _hosted_call(
        body, name, (N_HEAD_GROUP // 2,), in_specs=in_specs, out_specs=out_specs, out_shape=out_shape,
        scratch_shapes=[pltpu.VMEM((T, LANES), F32), pltpu.VMEM((T, LANES), F32)], args=args, side=side)
    outs = list(outs)
    outs[3] = outs[3].reshape(n_pat, N_HEAD_GROUP, QBLK, 2 * QBLK)
    return outs, side_outs


def _outproj_bwd(dh, att, g_post, w_out):
    T, D = dh.shape
    tm = TOKEN_TILE
    d_mix = w_out.shape[0]

    def body(dh_ref, att_ref, g_ref, w_ref, dma_ref, dmb_ref, datt_ref, dg_ref, db_ref):
        i = pl.program_id(0)

        @pl.when(i == 0)
        def _():
            dg_ref[...] = jnp.zeros_like(dg_ref)
            db_ref[...] = jnp.zeros_like(db_ref)

        halves = (slice(0, tm // 2), slice(tm // 2, tm))
        datts = []
        for rows in halves:
            att = att_ref[rows, :]
            datt, dgain = _rms_bwd(att, _rstd(att), g_ref[...], dh_ref[rows, :])
            dg_ref[...] += _colsum(dgain)
            db_ref[...] += _colsum(datt)
            datts.append(datt.astype(BF16))
            datt_ref[rows, :] = datts[-1]
        for rows, dattb in zip(halves, datts):
            dmix = _mm_nt(dattb, w_ref[...])
            dma_ref[rows, :] = dmix[:, :A_Q]
            dmb_ref[rows, :] = dmix[:, A_Q:]

    def tile(w):
        return pl.BlockSpec((tm, w), lambda i: (i, 0))

    return pl.pallas_call(
        body, name="outproj_bwd", grid=(T // tm,),
        in_specs=[tile(D), tile(D), _full((1, D)), _full((d_mix, D))],
        out_specs=[tile(A_Q), tile(B_W), tile(D), _full((1, D)), _full((1, D))],
        out_shape=[jax.ShapeDtypeStruct((T, A_Q), F32), jax.ShapeDtypeStruct((T, B_W), F32),
                   jax.ShapeDtypeStruct((T, D), BF16), jax.ShapeDtypeStruct((1, D), F32),
                   jax.ShapeDtypeStruct((1, D), F32)],
        compiler_params=_params(1),
    )(dh, att, g_post, w_out)


def _ple_fwd_bwd(h, g_pre, w_gate, p, w_proj, g_post, target):
    T, D = h.shape
    tm = TOKEN_TILE
    n_proj, ple, db = w_proj.shape

    def body(h_ref, gpre_ref, wg_ref, p_ref, wp_ref, gpost_ref, t_ref,
             a_ref, dpre_ref, de_ref, dh_ref, loss_ref, dgpost_ref, dgpre_ref):
        i = pl.program_id(0)

        @pl.when(i == 0)
        def _():
            loss_ref[...] = jnp.zeros_like(loss_ref)
            dgpost_ref[...] = jnp.zeros_like(dgpost_ref)
            dgpre_ref[...] = jnp.zeros_like(dgpre_ref)

        halves = (slice(0, tm // 2), slice(tm // 2, tm))
        parts = []
        for rows in halves:
            x = h_ref[rows, :]
            rx = _rstd(x)
            a = (x * rx * gpre_ref[...]).astype(BF16)
            a_ref[rows, :] = a
            pb = p_ref[rows, :].astype(BF16)
            parts.append(dict(rows=rows, x=x, rx=rx, pre=_mm(a, wg_ref[...]),
                              e=jnp.concatenate([_mm(pb, wp_ref[k]) for k in range(n_proj)], axis=1)))
        for s in parts:
            rows, x, e = s["rows"], s["x"], s["e"]
            gate = jax.nn.sigmoid(s["pre"])
            ge = gate * e
            rg = _rstd(ge)
            diff = x + ge * rg * gpost_ref[...] - t_ref[rows, :]
            loss_ref[...] += 0.5 * jnp.sum(jnp.mean(diff * diff, axis=1, keepdims=True))
            s["dy"] = diff * (1.0 / D)
            dge, dgain = _rms_bwd(ge, rg, gpost_ref[...], s["dy"])
            dgpost_ref[...] += _colsum(dgain)
            de_ref[rows, :] = (dge * gate).astype(BF16)
            s["dpre"] = (dge * e * gate * (1.0 - gate)).astype(BF16)
            dpre_ref[rows, :] = s["dpre"]
        for s in parts:
            dx, dgain = _rms_bwd(s["x"], s["rx"], gpre_ref[...], _mm_nt(s["dpre"], wg_ref[...]))
            dgpre_ref[...] += _colsum(dgain)
            dh_ref[s["rows"], :] = s["dy"] + dx

    def tile(w):
        return pl.BlockSpec((tm, w), lambda i: (i, 0))

    return pl.pallas_call(
        body, name="ple_fwd_bwd", grid=(T // tm,),
        in_specs=[tile(D), _full((1, D)), _full((D, D)), tile(ple), _full((n_proj, ple, db)), _full((1, D)), tile(D)],
        out_specs=[tile(D), tile(D), tile(D), tile(D), _full((1, LANES)), _full((1, D)), _full((1, D))],
        out_shape=[jax.ShapeDtypeStruct((T, D), BF16),
                   jax.ShapeDtypeStruct((T, D), BF16),
                   jax.ShapeDtypeStruct((T, D), BF16),
                   jax.ShapeDtypeStruct((T, D), F32),
                   jax.ShapeDtypeStruct((1, LANES), F32),
                   jax.ShapeDtypeStruct((1, D), F32),
                   jax.ShapeDtypeStruct((1, D), F32)],
        compiler_params=_params(1),
    )(h, g_pre, w_gate, p, w_proj, g_post, target)


def _ple_dw_proj(p, de, n_proj):
    T, ple = p.shape
    D = de.shape[1]
    db = D // n_proj
    tk = TOKEN_TILE
    nt = T // tk

    def body(p_ref, de_ref, o_ref, acc):
        t = pl.program_id(0)

        @pl.when(t == 0)
        def _():
            acc[...] = jnp.zeros_like(acc)

        acc[...] += _mm_tn(p_ref[...].astype(BF16), de_ref[...])

        @pl.when(t == nt - 1)
        def _():
            for k in range(n_proj):
                o_ref[k] = acc[:, k * db:(k + 1) * db].astype(BF16)

    return pl.pallas_call(
        body, name="ple_dw_proj", grid=(nt,),
        in_specs=[pl.BlockSpec((tk, ple), lambda t: (t, 0)), pl.BlockSpec((tk, D), lambda t: (t, 0))],
        out_specs=_full((n_proj, ple, db)), out_shape=jax.ShapeDtypeStruct((n_proj, ple, db), BF16),
        scratch_shapes=[pltpu.VMEM((ple, D), F32)], compiler_params=_params(1),
    )(p, de)


def _tok(width):
    return pl.BlockSpec((DW_TILE, width), lambda b, t: (t, 0))


def _dw_gu(a, dgu, name, side=None):
    T, D = a.shape
    nj, _, _, FB = dgu.shape
    return _tn_matmul(
        dgu, a, pl.BlockSpec((None, None, DW_TILE, FB), lambda b, t: (b % nj, b // nj, t, 0)), _tok(D),
        jax.ShapeDtypeStruct((2 * nj, FB, D), BF16), pl.BlockSpec((None, FB, D), lambda b, t: (b, 0, 0)),
        2 * nj, T // DW_TILE, (FB, D), name, side=side)


def _dw_down(hh, df, name, side=None):
    nj, T, FB = hh.shape
    D = df.shape[1]
    return _tn_matmul(
        hh, df, pl.BlockSpec((None, DW_TILE, FB), lambda b, t: (b, t, 0)), _tok(D),
        jax.ShapeDtypeStruct((nj, FB, D), BF16), pl.BlockSpec((None, FB, D), lambda b, t: (b, 0, 0)),
        nj, T // DW_TILE, (FB, D), name, side=side)


def _dw_rows(xm, y, name):
    T, k = xm.shape
    D = y.shape[1]
    out = _tn_matmul(
        xm, y, _tok(k), _tok(D), jax.ShapeDtypeStruct((k, D), BF16), _full((k, D)),
        1, T // DW_TILE, (k, D), name)
    return out.reshape(N_DEV, k // N_DEV, D)


def _cast_bf16(arrays):
    n = len(arrays)

    def body(*refs):
        for a in range(n):
            refs[n + a][...] = refs[a][...].astype(BF16)

    return pl.pallas_call(
        body, name="cast_shards",
        in_specs=[pl.BlockSpec(memory_space=pltpu.VMEM)] * n, out_specs=[pl.BlockSpec(memory_space=pltpu.VMEM)] * n,
        out_shape=[jax.ShapeDtypeStruct(a.shape, BF16) for a in arrays],
        compiler_params=pltpu.CompilerParams(vmem_limit_bytes=VMEM_LIMIT),
    )(*arrays)


def _pack_layout(D, n_rel_rows):
    n_bin = -(-D_IN // D)
    row_bin = len(GAINS)
    row_sink = row_bin + n_bin
    row_loss = row_sink + 1
    row_rb = -(-(row_loss + 1) // 8) * 8
    n_rows = row_rb + -(-n_rel_rows // 8) * 8
    bin_parts = [(r, min(D, D_IN - r * D)) for r in range(n_bin)]
    return row_bin, row_sink, row_loss, row_rb, n_rows, bin_parts


def _pair_swap_call(grad_blocks):
    def body(g_in, received, send_sems, recv_sems):
        start, _, wait = _pair_swap([g_in], [received], send_sems, recv_sems)
        start()
        wait()

    any_spec = pl.BlockSpec(memory_space=pl.ANY)
    return pl.pallas_call(
        body, name="pair_swap", in_specs=[any_spec], out_specs=any_spec,
        out_shape=jax.ShapeDtypeStruct((N_CHIPS,) + grad_blocks.shape[1:], grad_blocks.dtype),
        scratch_shapes=[pltpu.SemaphoreType.DMA((1, N_CHIPS)), pltpu.SemaphoreType.DMA((1, N_CHIPS))],
    )(grad_blocks)


def _pair_add(blocks, received, name):
    n, R, C = received.shape
    rows = _adamw_rows(R)
    core = lax.axis_index("c").astype(jnp.int32).reshape(1)

    def body(core_ref, a_ref, b_ref, o_ref):
        o_ref[...] = (a_ref[...].astype(F32) + b_ref[...].astype(F32)).astype(o_ref.dtype)

    tile = pl.BlockSpec((None, rows, C), lambda q, r, core_ref: (q, r, 0))
    return pl.pallas_call(
        body, name=name,
        grid_spec=pltpu.PrefetchScalarGridSpec(
            num_scalar_prefetch=1, grid=(n, R // rows),
            in_specs=[pl.BlockSpec((None, rows, C), lambda q, r, core_ref: (2 * q + core_ref[0], r, 0)), tile],
            out_specs=tile),
        out_shape=jax.ShapeDtypeStruct(received.shape, received.dtype), compiler_params=_params(2),
    )(core, blocks, received)


def _final_exchange(grad_blocks, partials, loss):
    D = partials["ffn1_pre_g"].shape[1]
    rb_shape = partials["rel_bias"].shape
    row_bin, row_sink, row_loss, row_rb, n_rows, bin_parts = _pack_layout(D, rb_shape[0])
    n_small = len(SMALL)

    def body(*refs):
        g_in = refs[0]
        part = dict(zip(SMALL, refs[1:1 + n_small]))
        loss_ref = refs[1 + n_small]
        landed, gath, pack, send_sems, recv_sems, local_sems = refs[2 + n_small:]

        pack[...] = jnp.zeros_like(pack)
        for i, name in enumerate(GAINS):
            pack[i:i + 1, :] = part[name][...]
        for r, width in bin_parts:
            pack[row_bin + r:row_bin + r + 1, 0:width] = part["b_in"][:, r * D:r * D + width]
        pack[row_sink:row_sink + 1, 0:LANES] = part["sinks"][...]
        pack[row_loss:row_loss + 1, 0:LANES] = loss_ref[...]
        pack[row_rb:row_rb + rb_shape[0], 0:rb_shape[1]] = part["rel_bias"][...]

        small_start, _, small_wait = _side_copies("gather", [pack], [gath], send_sems, recv_sems, local_sems, sem_row=0)
        big_start, _, big_wait = _quad_exchange([g_in], [landed], send_sems, recv_sems, local_sems, sem_row=1)
        small_start()
        big_start()
        small_wait()
        big_wait()

    args = [grad_blocks] + [partials[k] for k in SMALL] + [loss]
    vmem = pl.BlockSpec(memory_space=pltpu.VMEM)
    any_spec = pl.BlockSpec(memory_space=pl.ANY)
    return pl.pallas_call(
        body, name="final_exchange",
        in_specs=[any_spec] + [vmem] * (n_small + 1),
        out_specs=[any_spec, any_spec],
        out_shape=[jax.ShapeDtypeStruct(grad_blocks.shape, grad_blocks.dtype),
                   jax.ShapeDtypeStruct((N_DEV, n_rows, D), F32)],
        scratch_shapes=[pltpu.VMEM((n_rows, D), F32), pltpu.SemaphoreType.DMA((2, 7)),
                        pltpu.SemaphoreType.DMA((2, 7)), pltpu.SemaphoreType.DMA((2, N_CHIPS))],
    )(*args)


def _adamw(w, g, m, v):
    m = ADAM_B1 * m + (1.0 - ADAM_B1) * g
    v = ADAM_B2 * v + (1.0 - ADAM_B2) * (g * g)
    m_hat = m / (1.0 - ADAM_B1 ** ADAM_STEP)
    v_hat = v / (1.0 - ADAM_B2 ** ADAM_STEP)
    return -ADAM_LR * (m_hat / (jnp.sqrt(v_hat) + ADAM_EPS) + ADAM_WD * w), m, v


def _sum_adamw(partials, w, m, v, rows, name):
    R, C = w.shape
    n = partials.shape[0]

    def body(p_ref, w_ref, m_ref, v_ref, g_ref, d_ref, nm_ref, nv_ref):
        g = p_ref[0].astype(F32)
        for k in range(1, n):
            g = g + p_ref[k].astype(F32)
        g_ref[...] = g
        d_ref[...], nm_ref[...], nv_ref[...] = _adamw(w_ref[...], g, m_ref[...], v_ref[...])

    tile = pl.BlockSpec((rows, C), lambda i: (i, 0))
    return pl.pallas_call(
        body, name=name, grid=(R // rows,),
        in_specs=[pl.BlockSpec((n, rows, C), lambda i: (0, i, 0)), tile, tile, tile],
        out_specs=[tile] * 4, out_shape=[jax.ShapeDtypeStruct((R, C), F32)] * 4,
        compiler_params=_params(1),
    )(partials, w, m, v)


def _small_adamw(gathered, ws, ms, vs):
    D = ws["ffn1_pre_g"].shape[1]
    n_sink = ws["sinks"].shape[1]
    rb_shape = ws["rel_bias"].shape
    row_bin, row_sink, row_loss, row_rb, n_rows, bin_parts = _pack_layout(D, rb_shape[0])
    n_small = len(SMALL)

    def body(*refs):
        gath = refs[0]
        pos = 1
        w_ref = dict(zip(SMALL, refs[pos:pos + n_small]))
        m_ref = dict(zip(SMALL, refs[pos + n_small:pos + 2 * n_small]))
        v_ref = dict(zip(SMALL, refs[pos + 2 * n_small:pos + 3 * n_small]))
        pos += 3 * n_small
        outs = {name: refs[pos + 4 * i:pos + 4 * i + 4] for i, name in enumerate(SMALL)}
        loss_out = refs[pos + 4 * n_small]
        pack = refs[pos + 4 * n_small + 1]

        total = gath[0]
        for k in range(1, N_DEV):
            total = total + gath[k]
        pack[...] = total

        def update(name, g):
            g_out, d_out, m_out, v_out = outs[name]
            g_out[...] = g
            d_out[...], m_out[...], v_out[...] = _adamw(w_ref[name][...], g, m_ref[name][...], v_ref[name][...])

        for i, name in enumerate(GAINS):
            update(name, pack[i:i + 1, :])
        update("b_in", jnp.concatenate([pack[row_bin + r:row_bin + r + 1, 0:width] for r, width in bin_parts], axis=1))
        update("sinks", pack[row_sink:row_sink + 1, 0:n_sink])
        update("rel_bias", pack[row_rb:row_rb + rb_shape[0], 0:rb_shape[1]])
        loss_out[...] = pack[row_loss:row_loss + 1, 0:LANES]

    args = [gathered]
    for group in (ws, ms, vs):
        args += [group[k] for k in SMALL]
    out_shape = []
    for name in SMALL:
        out_shape += [jax.ShapeDtypeStruct(ws[name].shape, F32)] * 4
    out_shape.append(jax.ShapeDtypeStruct((1, LANES), F32))
    res = pl.pallas_call(
        body, name="small_adamw",
        in_specs=[pl.BlockSpec(memory_space=pltpu.VMEM)] * len(args),
        out_specs=[pl.BlockSpec(memory_space=pltpu.VMEM)] * len(out_shape),
        out_shape=out_shape,
        scratch_shapes=[pltpu.VMEM((n_rows, D), F32)],
    )(*args)
    per_name = {name: res[4 * i:4 * i + 4] for i, name in enumerate(SMALL)}
    return per_name, res[-1]


COLUMN_SHARDED = ("ffn1_w_gu", "ffn2_w_gu", "w_in")


def _adamw_rows(rows_total):
    return max(r for r in range(16, min(rows_total, 256) + 1, 16) if rows_total % r == 0)


def kernel(x, p, rel_bias, ffn1_pre_g, ffn1_w_gu, ffn1_w_down, ffn1_post_g, attn_pre_g, w_in, b_in, sinks, w_out, b_out, attn_post_g, ffn2_pre_g, ffn2_w_gu, ffn2_w_down, ffn2_post_g, ple_pre_g, w_ple_gate, w_ple_proj, ple_post_g, loss_target, m_rel_bias, m_ffn1_pre_g, m_ffn1_w_gu, m_ffn1_w_down, m_ffn1_post_g, m_attn_pre_g, m_w_in, m_b_in, m_sinks, m_w_out, m_b_out, m_attn_post_g, m_ffn2_pre_g, m_ffn2_w_gu, m_ffn2_w_down, m_ffn2_post_g, m_ple_pre_g, m_w_ple_gate, m_w_ple_proj, m_ple_post_g, v_rel_bias, v_ffn1_pre_g, v_ffn1_w_gu, v_ffn1_w_down, v_ffn1_post_g, v_attn_pre_g, v_w_in, v_b_in, v_sinks, v_w_out, v_b_out, v_attn_post_g, v_ffn2_pre_g, v_ffn2_w_gu, v_ffn2_w_down, v_ffn2_post_g, v_ple_pre_g, v_w_ple_gate, v_w_ple_proj, v_ple_post_g):
    given = dict(locals())
    ws = {k: given[k] for k in WEIGHTS}
    ms = {k: given["m_" + k] for k in WEIGHTS}
    vs = {k: given["v_" + k] for k in WEIGHTS}

    def shard(t):
        return t.reshape(t.shape[1:])

    xs, ps, target = shard(x), shard(shard(p)), shard(loss_target)
    T, D = xs.shape
    small = {k: ws[k] for k in SMALL}

    def local(group, k):
        t = shard(group[k])
        return jnp.swapaxes(t, 0, 1) if k in COLUMN_SHARDED else t

    shards = {k: local(ws, k) for k in BIG}

    cast = dict(zip(BIG, _cast_bf16([shards[k] for k in BIG])))
    buckets_a = _bucket_tiles(PATTERNS_A)
    buckets_b = _bucket_tiles(PATTERNS_B)
    bias_a, _ = _bias_build(small["rel_bias"], buckets_a, 0, "bias_build_a")
    bias_b, (w_gu1, w_down1) = _bias_build(
        small["rel_bias"], buckets_b, N_HEAD_GROUP, "bias_build_b",
        side=("relay_gather", [cast["ffn1_w_gu"], cast["ffn1_w_down"]]))
    w_down1 = w_down1.reshape(-1, D)
    a_cfg = dict(patterns=PATTERNS_A, qcol=Q_A_COL, kcol=K_A_COL, vcol=V_A_COL, shared_kv=True)
    b_cfg = dict(patterns=PATTERNS_B, qcol=Q_B_COL, kcol=K_B_COL, vcol=V_B_COL, shared_kv=False)

    (h1, f1, a1, gu1), (w_in_g, w_down2) = _ffn_fwd(
        xs, small["ffn1_pre_g"], small["ffn1_post_g"], w_gu1, w_down1, "ffn1_fwd",
        side=("relay_gather", [cast["w_in"], cast["ffn2_w_down"]]))
    w_in_full = w_in_g.reshape(D_IN, D)
    w_down2 = w_down2.reshape(-1, D)
    (z, a2), (w_out_g,) = _inproj_fwd(h1, small["attn_pre_g"], w_in_full, small["b_in"],
                                      side=("relay_gather", [cast["w_out"]]))
    w_out_full = w_out_g.reshape(-1, D)
    (mix_a, lse_a), (w_gate, w_proj) = _attn_fwd(
        z, bias_a, small["sinks"], name="attn_a_fwd", **a_cfg,
        side=("relay_gather", [cast["w_ple_gate"], cast["w_ple_proj"]]))
    w_gate = w_gate.reshape(-1, D)
    (mix_b, lse_b), (w_gu2,) = _attn_fwd(
        z, bias_b, None, name="attn_b_fwd", **b_cfg, side=("relay_gather", [cast["ffn2_w_gu"]]))
    (h3, f2, a3, gu2, att, h2, mix), _ = _ffn_fwd(
        h1, small["ffn2_pre_g"], small["ffn2_post_g"], w_gu2, w_down2, "ffn2_fwd",
        attn=(mix_a, mix_b, w_out_full, small["b_out"], small["attn_post_g"]))
    a4, dpre, de, dh3, loss, dg_ple_post, dg_ple_pre = _ple_fwd_bwd(
        h3, small["ple_pre_g"], w_gate, ps, w_proj, small["ple_post_g"], target)

    d_gate = _dw_rows(a4, dpre, "ple_dw_gate")
    d_proj = _ple_dw_proj(ps, de, N_DEV)
    landed = {}
    (dh2, df2, hh2, dgu2, dg_f2_post, dg_f2_pre), (landed["w_ple_gate"], landed["w_ple_proj"]) = _ffn_bwd(
        dh3, f2, small["ffn2_post_g"], h2, small["ffn2_pre_g"], gu2, w_gu2, w_down2, "ffn2_bwd",
        side=("exchange", [d_gate, d_proj]))
    d_gu2 = _dw_gu(a3, dgu2, "ffn2_dw_gu")
    d_down2 = _dw_down(hh2, df2, "ffn2_dw_down").reshape(N_DEV, -1, D)
    dmix_a, dmix_b, datt, dg_attn_post, db_out = _outproj_bwd(dh2, att, small["attn_post_g"], w_out_full)
    d_out = _dw_rows(mix, datt, "attn_dw_out")
    (dqa, dka, dva, ds_a, dsinks), _ = _attn_bwd(
        z, bias_a, small["sinks"], dmix_a, mix_a, lse_a, name="attn_a_bwd", **a_cfg)
    (dqb, dkb, dvb, ds_b), (landed["ffn2_w_gu"],) = _attn_bwd(
        z, bias_b, None, dmix_b, mix_b, lse_b, name="attn_b_bwd", **b_cfg, side=("exchange", [d_gu2]))
    (dh1, dz, db_in, dg_attn_pre), (landed["w_out"],) = _inproj_bwd(
        dqa, dka, dva, dqb, dkb, dvb, w_in_full, h1, small["attn_pre_g"], dh2, side=("exchange", [d_out]))
    cols = D_IN // 3
    d_in = _tn_matmul(
        dz, a2, pl.BlockSpec((DW_TILE, cols), lambda b, t: (t, b)), _tok(D),
        jax.ShapeDtypeStruct((D_IN, D), BF16), pl.BlockSpec((cols, D), lambda b, t: (b, 0)),
        3, T // DW_TILE, (cols, D), "attn_dw_in").reshape(N_DEV, D_IN // N_DEV, D)
    (grad_x, df1, hh1, dgu1, dg_f1_post, dg_f1_pre), (landed["w_in"], landed["ffn2_w_down"]) = _ffn_bwd(
        dh1, f1, small["ffn1_post_g"], xs, small["ffn1_pre_g"], gu1, w_gu1, w_down1, "ffn1_bwd",
        side=("exchange", [d_in, d_down2]))
    d_down1 = _dw_down(hh1, df1, "ffn1_dw_down").reshape(N_DEV, -1, D)
    d_gu1, (landed["ffn1_w_down"],) = _dw_gu(a1, dgu1, "ffn1_dw_gu", side=("exchange", [d_down1]))

    rb_a = _bias_grad(ds_a, buckets_a, "bias_grad_a")
    rb_b = _bias_grad(ds_b, buckets_b, "bias_grad_b").reshape(len(PATTERNS_B), N_HEAD_GROUP, NUM_BUCKETS)
    d_rel_bias = jnp.concatenate([rb_a.T, jnp.sum(rb_b, axis=0).T], axis=1)
    small_grads = {"ffn1_pre_g": dg_f1_pre, "ffn1_post_g": dg_f1_post, "attn_pre_g": dg_attn_pre,
                   "attn_post_g": dg_attn_post, "ffn2_pre_g": dg_f2_pre, "ffn2_post_g": dg_f2_post,
                   "ple_pre_g": dg_ple_pre, "ple_post_g": dg_ple_post, "b_out": db_out, "b_in": db_in,
                   "sinks": dsinks, "rel_bias": d_rel_bias}
    d_gu1_pairs = _pair_add(d_gu1, _pair_swap_call(d_gu1), "ffn1_dw_gu_pair_add")
    landed["ffn1_w_gu"], small_gathered = _final_exchange(d_gu1_pairs, small_grads, loss)

    result = {}
    for k in BIG:
        outs = _sum_adamw(landed[k], shards[k], local(ms, k), local(vs, k), _adamw_rows(shards[k].shape[0]),
                          k + "_adamw")
        if k in COLUMN_SHARDED:
            outs = [jnp.swapaxes(o, 0, 1) for o in outs]
        result[k] = [o.reshape(ws[k].shape) for o in outs]
    small_res, loss_all = _small_adamw(
        small_gathered, small, {k: ms[k] for k in SMALL}, {k: vs[k] for k in SMALL})
    result.update(small_res)

    out = [loss_all[0, 0], grad_x.reshape(x.shape)]
    for i in range(4):
        out += [result[k][i] for k in WEIGHTS]
    return tuple(out)
```

```python
import functools
import math

import numpy as np
import jax
import jax.numpy as jnp
from jax import lax
from jax.experimental import pallas as pl
from jax.experimental.pallas import tpu as pltpu

F32 = jnp.float32
BF16 = jnp.bfloat16
MESH = pl.DeviceIdType.MESH

N_DEV = 8
EPS = 1e-6
NEG_INF = -1e30
HEAD_DIM = 64
LANES = 128
QBLK = 128
D_IN = 2304
A_Q, A_KV, B_W = 512, 128, 512
N_HEAD_GROUP = 8
NUM_BUCKETS = 32
MAX_DISTANCE = 2048
PATTERNS_A = ((1, 127),)
PATTERNS_B = ((1, 128), (4, 128), (16, 128))
Q_A_COL, K_A_COL, V_A_COL = 0, 4, 5
Q_B_COL, K_B_COL, V_B_COL = 6, 10, 14

ADAM_LR, ADAM_B1, ADAM_B2, ADAM_EPS, ADAM_WD, ADAM_STEP = 0.001, 0.9, 0.999, 1e-08, 0.01, 10

TOKEN_TILE = 512
DW_TILE = 1024
FWD_BLOCKS = 4
BWD_BLOCKS = 4
VMEM_LIMIT = 56 * 1024 * 1024
ARB = "arbitrary"

BIG = ("ffn1_w_gu", "ffn1_w_down", "w_in", "w_out", "ffn2_w_gu", "ffn2_w_down", "w_ple_gate", "w_ple_proj")
GAINS = ("ffn1_pre_g", "ffn1_post_g", "attn_pre_g", "attn_post_g", "ffn2_pre_g", "ffn2_post_g",
         "ple_pre_g", "ple_post_g", "b_out")
SMALL = GAINS + ("b_in", "sinks", "rel_bias")
WEIGHTS = ("rel_bias", "ffn1_pre_g", "ffn1_w_gu", "ffn1_w_down", "ffn1_post_g", "attn_pre_g", "w_in", "b_in",
           "sinks", "w_out", "b_out", "attn_post_g", "ffn2_pre_g", "ffn2_w_gu", "ffn2_w_down", "ffn2_post_g",
           "ple_pre_g", "w_ple_gate", "w_ple_proj", "ple_post_g")


def _params(n_axes):
    return pltpu.CompilerParams(dimension_semantics=(ARB,) * n_axes, vmem_limit_bytes=VMEM_LIMIT)


def _mm(a, b):
    return jnp.dot(a, b, preferred_element_type=F32)


def _mm_nt(a, b):
    return lax.dot_general(a, b, (((1,), (1,)), ((), ())), preferred_element_type=F32)


def _mm_tn(a, b):
    return lax.dot_general(a, b, (((0,), (0,)), ((), ())), preferred_element_type=F32)


def _rstd(x):
    return lax.rsqrt(jnp.mean(x * x, axis=-1, keepdims=True) + EPS)


def _rms_bwd(x, r, gain, dy):
    n = x * r
    gdy = dy * gain
    return r * (gdy - n * jnp.mean(gdy * n, axis=-1, keepdims=True)), dy * n


def _colsum(v):
    return jnp.sum(v, axis=0, keepdims=True)


def _full(shape):
    return pl.BlockSpec(shape, lambda *_: (0,) * len(shape))


def _mesh_place():
    return lax.axis_index("x"), lax.axis_index("y"), lax.axis_index("c")


def _slot(dev):
    return 4 * dev[0] + 2 * dev[1] + dev[2]


def _peers(x, y, c):
    out = []
    for flip in range(1, N_DEV):
        dx, dy, dc = (flip >> 2) & 1, (flip >> 1) & 1, flip & 1
        out.append((1 - x if dx else x, 1 - y if dy else y, 1 - c if dc else c))
    return out


def _side_copies(kind, ins, outs, send_sems, recv_sems, local_sems, sem_row=0):
    n = len(ins)
    x, y, c = _mesh_place()
    me = _slot((x, y, c))
    peers = _peers(x, y, c)

    def src(a, block):
        return ins[a] if kind == "gather" else ins[a].at[block]

    def send(a, k, peer):
        return pltpu.make_async_remote_copy(
            src_ref=src(a, _slot(peer)), dst_ref=outs[a].at[me],
            send_sem=send_sems.at[sem_row + a, k], recv_sem=recv_sems.at[sem_row + a, k],
            device_id=peer, device_id_type=MESH)

    def arrival(a, k, peer):
        return pltpu.make_async_remote_copy(
            src_ref=src(a, _slot(peer)), dst_ref=outs[a].at[_slot(peer)],
            send_sem=send_sems.at[sem_row + a, k], recv_sem=recv_sems.at[sem_row + a, k],
            device_id=peer, device_id_type=MESH)

    def own(a):
        return pltpu.make_async_copy(src(a, me), outs[a].at[me], local_sems.at[sem_row + a, 0])

    def start():
        for k, peer in enumerate(peers):
            for a in range(n):
                send(a, k, peer).start()
        for a in range(n):
            own(a).start()

    def wait():
        for k, peer in enumerate(peers):
            for a in range(n):
                arrival(a, k, peer).wait_recv()
        for k, peer in enumerate(peers):
            for a in range(n):
                send(a, k, peer).wait_send()
        for a in range(n):
            own(a).wait()

    return start, None, wait


N_CHIPS = N_DEV // 2


def _pair_swap(ins, received, send_sems, recv_sems):
    n = len(ins)
    x, y, c = _mesh_place()
    sibling = (x, y, 1 - c)

    def send(a, q):
        return pltpu.make_async_remote_copy(
            src_ref=ins[a].at[2 * q + (1 - c)], dst_ref=received[a].at[q],
            send_sem=send_sems.at[a, q], recv_sem=recv_sems.at[a, q], device_id=sibling, device_id_type=MESH)

    def start():
        for a in range(n):
            for q in range(N_CHIPS):
                send(a, q).start()

    def wait():
        for a in range(n):
            for q in range(N_CHIPS):
                send(a, q).wait_recv()
        for a in range(n):
            for q in range(N_CHIPS):
                send(a, q).wait_send()

    return start, None, wait


def _quad_exchange(ins, outs, send_sems, recv_sems, local_sems, sem_row=0):
    n = len(ins)
    x, y, c = _mesh_place()
    mine = 2 * x + y
    chips = [(1 - x, y), (x, 1 - y), (1 - x, 1 - y)]

    def send(a, k, chip):
        return pltpu.make_async_remote_copy(
            src_ref=ins[a].at[2 * chip[0] + chip[1]], dst_ref=outs[a].at[mine],
            send_sem=send_sems.at[sem_row + a, k], recv_sem=recv_sems.at[sem_row + a, k],
            device_id=(chip[0], chip[1], c), device_id_type=MESH)

    def arrival(a, k, chip):
        return pltpu.make_async_remote_copy(
            src_ref=ins[a].at[2 * chip[0] + chip[1]], dst_ref=outs[a].at[2 * chip[0] + chip[1]],
            send_sem=send_sems.at[sem_row + a, k], recv_sem=recv_sems.at[sem_row + a, k],
            device_id=(chip[0], chip[1], c), device_id_type=MESH)

    def own(a):
        return pltpu.make_async_copy(ins[a].at[mine], outs[a].at[mine], local_sems.at[sem_row + a, 0])

    def start():
        for k, chip in enumerate(chips):
            for a in range(n):
                send(a, k, chip).start()
        for a in range(n):
            own(a).start()

    def wait():
        for k, chip in enumerate(chips):
            for a in range(n):
                arrival(a, k, chip).wait_recv()
        for k, chip in enumerate(chips):
            for a in range(n):
                send(a, k, chip).wait_send()
        for a in range(n):
            own(a).wait()

    return start, None, wait


def _relay_gather(ins, outs, send_sems, recv_sems, local_sems):
    n = len(ins)
    x, y, c = _mesh_place()
    me, sibling = (x, y, c), (x, y, 1 - c)
    chips = [(1 - x, y), (x, 1 - y), (1 - x, 1 - y)]

    def copy(a, k, block, to, src=None):
        dst = outs[a].at[_slot(block)]
        return pltpu.make_async_remote_copy(
            src_ref=dst if src is None else src, dst_ref=dst,
            send_sem=send_sems.at[a, k], recv_sem=recv_sems.at[a, k], device_id=to, device_id_type=MESH)

    def own(a):
        return pltpu.make_async_copy(ins[a], outs[a].at[_slot(me)], local_sems.at[a, 0])

    def start():
        for j, chip in enumerate(chips):
            for a in range(n):
                copy(a, 1 + j, me, (*chip, c), src=ins[a]).start()
        for a in range(n):
            copy(a, 0, me, sibling, src=ins[a]).start()
            own(a).start()

    def relay():
        for j, chip in enumerate(chips):
            for a in range(n):
                copy(a, 1 + j, (*chip, c), me).wait_recv()
                copy(a, 4 + j, (*chip, c), sibling).start()

    def wait():
        for a in range(n):
            copy(a, 0, sibling, me).wait_recv()
        for j, chip in enumerate(chips):
            for a in range(n):
                copy(a, 4 + j, (*chip, 1 - c), me).wait_recv()
        for j, chip in enumerate(chips):
            for a in range(n):
                copy(a, 1 + j, me, (*chip, c), src=ins[a]).wait_send()
                copy(a, 4 + j, (*chip, c), sibling).wait_send()
        for a in range(n):
            copy(a, 0, me, sibling, src=ins[a]).wait_send()
            own(a).wait()

    return start, relay, wait


def _side_out_shapes(kind, arrays):
    if kind in ("gather", "relay_gather"):
        return [jax.ShapeDtypeStruct((N_DEV,) + a.shape, a.dtype) for a in arrays]
    return [jax.ShapeDtypeStruct(a.shape, a.dtype) for a in arrays]


def _hosted_call(body, name, grid, in_specs, out_specs, out_shape, scratch_shapes, args, side=None):
    if side is None:
        outs = pl.pallas_call(
            body, name=name, grid=grid, in_specs=in_specs, out_specs=out_specs, out_shape=out_shape,
            scratch_shapes=scratch_shapes, compiler_params=_params(len(grid)))(*args)
        return outs, []
    kind, arrays = side
    side_shapes = _side_out_shapes(kind, arrays)
    n_in, n_out, n_scr, n_side = len(in_specs), len(out_specs), len(scratch_shapes), len(arrays)

    def hosted(*refs):
        pos = 0
        groups = []
        for size in (n_in, n_side, n_out, len(side_shapes), n_scr):
            groups.append(refs[pos:pos + size])
            pos += size
        ins, side_in, outs, side_out, scr = groups
        send_sems, recv_sems, local_sems = refs[pos:]
        ids = [pl.program_id(d) for d in range(len(grid))]
        is_first = functools.reduce(jnp.logical_and, [i == 0 for i in ids])
        is_last = functools.reduce(jnp.logical_and, [i == g - 1 for i, g in zip(ids, grid)])
        if kind == "relay_gather":
            start, relay, wait = _relay_gather(side_in, side_out, send_sems, recv_sems, local_sems)
        else:
            start, relay, wait = _side_copies(kind, side_in, side_out, send_sems, recv_sems, local_sems)
        pl.when(is_first)(start)
        if relay is not None:
            pl.when(is_last)(relay)
        body(*ins, *outs, *scr)
        pl.when(is_last)(wait)

    any_spec = pl.BlockSpec(memory_space=pl.ANY)
    outs = pl.pallas_call(
        hosted, name=name, grid=grid,
        in_specs=list(in_specs) + [any_spec] * n_side,
        out_specs=list(out_specs) + [any_spec] * len(side_shapes),
        out_shape=list(out_shape) + side_shapes,
        scratch_shapes=list(scratch_shapes) + [pltpu.SemaphoreType.DMA((n_side, 7)), pltpu.SemaphoreType.DMA((n_side, 7)),
                                               pltpu.SemaphoreType.DMA((n_side, N_CHIPS))],
        compiler_params=_params(len(grid)))(*args, *arrays)
    return outs[:n_out], outs[n_out:]


def _lane_chunks(width, chunk=2 * LANES):
    return [slice(n0, min(n0 + chunk, width)) for n0 in range(0, width, chunk)]


def _pipelined(chunks, first, middle, last):
    n = len(chunks)
    a, b, total = {}, {}, None
    for step in range(n + 2):
        if step < n:
            a[step] = first(chunks[step])
        if 0 <= step - 1 < n:
            b[step - 1] = middle(chunks[step - 1], a.pop(step - 1))
        if 0 <= step - 2 < n:
            part = last(chunks[step - 2], b.pop(step - 2))
            total = part if total is None else total + part
    return total


def _ffn_fwd(h, g_pre, g_post, w_gu, w_down, name, side=None, attn=None):
    T, D = h.shape
    nj = w_gu.shape[0] // 2
    FB = w_gu.shape[1]
    tm = TOKEN_TILE
    n_attn = 0 if attn is None else 5

    def body(*refs):
        h_ref, gpre_ref, gpost_ref, wg_ref, wu_ref, wd_ref = refs[:6]
        hout_ref, f_ref, a_ref, gu_ref = refs[6 + n_attn:10 + n_attn]
        a_scr, acc = refs[-2:]
        j = pl.program_id(1)

        @pl.when(j == 0)
        def _():
            if attn is None:
                x = h_ref[...]
            else:
                ma_ref, mb_ref, wo_ref, bo_ref, ga_ref = refs[6:11]
                att_ref, hmid_ref, mix_ref = refs[15:18]
                mix = jnp.concatenate([ma_ref[...], mb_ref[...]], axis=1).astype(BF16)
                mix_ref[...] = mix
                att = _mm(mix, wo_ref[...]) + bo_ref[...]
                att_ref[...] = att
                x = h_ref[...] + att * _rstd(att) * ga_ref[...]
                hmid_ref[...] = x
            a = (x * _rstd(x) * gpre_ref[...]).astype(BF16)
            a_scr[...] = a
            a_ref[...] = a
            acc[...] = jnp.zeros_like(acc)

        a = a_scr[...]
        g = _mm_nt(a, wg_ref[...])
        u = _mm_nt(a, wu_ref[...])
        gu_ref[0] = g.astype(BF16)
        gu_ref[1] = u.astype(BF16)
        hh = (g * jax.nn.sigmoid(g) * u).astype(BF16)
        acc[...] += _mm(hh, wd_ref[...])

        @pl.when(j == nj - 1)
        def _():
            f = acc[...]
            f_ref[...] = f
            x = h_ref[...] if attn is None else refs[16][...]
            hout_ref[...] = x + 0.5 * (f * _rstd(f) * gpost_ref[...])

    tile = pl.BlockSpec((tm, D), lambda i, j: (i, 0))
    in_specs = [tile, _full((1, D)), _full((1, D)),
                pl.BlockSpec((None, FB, D), lambda i, j: (j, 0, 0)),
                pl.BlockSpec((None, FB, D), lambda i, j: (j + nj, 0, 0)),
                pl.BlockSpec((FB, D), lambda i, j: (j, 0))]
    out_specs = [tile, tile, tile, pl.BlockSpec((None, 2, tm, FB), lambda i, j: (j, 0, i, 0))]
    out_shape = [
        jax.ShapeDtypeStruct((T, D), F32),
        jax.ShapeDtypeStruct((T, D), F32),
        jax.ShapeDtypeStruct((T, D), BF16),
        jax.ShapeDtypeStruct((nj, 2, T, FB), BF16),
    ]
    args = [h, g_pre, g_post, w_gu, w_gu, w_down]
    if attn is not None:
        mix_a, mix_b, w_out, b_out, g_attn = attn
        d_mix = w_out.shape[0]
        in_specs += [pl.BlockSpec((tm, mix_a.shape[1]), lambda i, j: (i, 0)),
                     pl.BlockSpec((tm, mix_b.shape[1]), lambda i, j: (i, 0)),
                     _full((d_mix, D)), _full((1, D)), _full((1, D))]
        out_specs += [tile, tile, pl.BlockSpec((tm, d_mix), lambda i, j: (i, 0))]
        out_shape += [jax.ShapeDtypeStruct((T, D), F32),
                      jax.ShapeDtypeStruct((T, D), F32),
                      jax.ShapeDtypeStruct((T, d_mix), BF16)]
        args += [mix_a, mix_b, w_out, b_out, g_attn]
    return _hosted_call(
        body, name, (T // tm, nj), in_specs=in_specs, out_specs=out_specs, out_shape=out_shape,
        scratch_shapes=[pltpu.VMEM((tm, D), BF16), pltpu.VMEM((tm, D), F32)], args=args, side=side)


def _ffn_bwd(dh_out, f, g_post, h, g_pre, gu, w_gu, w_down, name, side=None):
    T, D = h.shape
    nj = w_gu.shape[0] // 2
    FB = w_gu.shape[1]
    tm = TOKEN_TILE

    def body(dho_ref, f_ref, gpost_ref, h_ref, gpre_ref, gu_ref, wg_ref, wu_ref, wd_ref,
             dhin_ref, df_ref, hh_ref, dgu_ref, dgpost_ref, dgpre_ref, df_scr, da):
        i, j = pl.program_id(0), pl.program_id(1)

        @pl.when(jnp.logical_and(i == 0, j == 0))
        def _():
            dgpost_ref[...] = jnp.zeros_like(dgpost_ref)
            dgpre_ref[...] = jnp.zeros_like(dgpre_ref)

        @pl.when(j == 0)
        def _():
            fv = f_ref[...]
            df, dgain = _rms_bwd(fv, _rstd(fv), gpost_ref[...], 0.5 * dho_ref[...])
            dgpost_ref[...] += _colsum(dgain)
            dfb = df.astype(BF16)
            df_scr[...] = dfb
            df_ref[...] = dfb
            da[...] = jnp.zeros_like(da)

        dfb = df_scr[...]

        halves = (slice(0, tm // 2), slice(tm // 2, tm))

        def hidden_grad(c):
            return [_mm_nt(dfb[rows], wd_ref[c, :]) for rows in halves]

        def through_swiglu(c, dhh):
            dhh = jnp.concatenate(dhh, axis=0)
            g = gu_ref[0, :, c].astype(F32)
            u = gu_ref[1, :, c].astype(F32)
            sg = jax.nn.sigmoid(g)
            silu = g * sg
            hh_ref[:, c] = (silu * u).astype(BF16)
            dg = (dhh * u * (sg * (1.0 + (g - silu)))).astype(BF16)
            du = (dhh * silu).astype(BF16)
            dgu_ref[0, :, c] = dg
            dgu_ref[1, :, c] = du
            return dg, du

        def input_grad(c, dgu):
            return jnp.concatenate(
                [_mm(dgu[0][rows], wg_ref[c, :]) + _mm(dgu[1][rows], wu_ref[c, :]) for rows in halves], axis=0)

        da[...] += _pipelined(_lane_chunks(FB), hidden_grad, through_swiglu, input_grad)

        @pl.when(j == nj - 1)
        def _():
            x = h_ref[...]
            dx, dgain = _rms_bwd(x, _rstd(x), gpre_ref[...], da[...])
            dgpre_ref[...] += _colsum(dgain)
            dhin_ref[...] = dho_ref[...] + dx

    tile = pl.BlockSpec((tm, D), lambda i, j: (i, 0))
    return _hosted_call(
        body, name, (T // tm, nj),
        in_specs=[
            tile, tile, _full((1, D)), tile, _full((1, D)),
            pl.BlockSpec((None, 2, tm, FB), lambda i, j: (j, 0, i, 0)),
            pl.BlockSpec((None, FB, D), lambda i, j: (j, 0, 0)),
            pl.BlockSpec((None, FB, D), lambda i, j: (j + nj, 0, 0)),
            pl.BlockSpec((FB, D), lambda i, j: (j, 0)),
        ],
        out_specs=[
            tile, tile,
            pl.BlockSpec((None, tm, FB), lambda i, j: (j, i, 0)),
            pl.BlockSpec((None, 2, tm, FB), lambda i, j: (j, 0, i, 0)),
            _full((1, D)), _full((1, D)),
        ],
        out_shape=[
            jax.ShapeDtypeStruct((T, D), F32),
            jax.ShapeDtypeStruct((T, D), BF16),
            jax.ShapeDtypeStruct((nj, T, FB), BF16),
            jax.ShapeDtypeStruct((nj, 2, T, FB), BF16),
            jax.ShapeDtypeStruct((1, D), F32),
            jax.ShapeDtypeStruct((1, D), F32),
        ],
        scratch_shapes=[pltpu.VMEM((tm, D), BF16), pltpu.VMEM((tm, D), F32)],
        args=(dh_out, f, g_post, h, g_pre, gu, w_gu, w_gu, w_down), side=side)


def _tn_matmul(x, y, x_spec, y_spec, out_shape, out_spec, n_blocks, n_steps, acc_shape, name, side=None):
    def body(x_ref, y_ref, o_ref, acc):
        t = pl.program_id(1)

        @pl.when(t == 0)
        def _():
            acc[...] = jnp.zeros_like(acc)

        acc[...] += _mm_tn(x_ref[...].astype(BF16), y_ref[...].astype(BF16))

        @pl.when(t == n_steps - 1)
        def _():
            o_ref[...] = acc[...].astype(o_ref.dtype)

    outs, side_outs = _hosted_call(
        body, name, (n_blocks, n_steps), in_specs=[x_spec, y_spec], out_specs=[out_spec], out_shape=[out_shape],
        scratch_shapes=[pltpu.VMEM(acc_shape, F32)], args=(x, y), side=side)
    return (outs[0], side_outs) if side is not None else outs[0]


def _inproj_fwd(h, g_pre, w_in, b_in, side=None):
    T, D = h.shape
    tm = TOKEN_TILE

    def body(h_ref, g_ref, w_ref, b_ref, z_ref, a_ref):
        x = h_ref[...]
        a = (x * _rstd(x) * g_ref[...]).astype(BF16)
        a_ref[...] = a
        z_ref[...] = _mm_nt(a, w_ref[...]) + b_ref[...]

    return _hosted_call(
        body, "inproj_fwd", (T // tm,),
        in_specs=[pl.BlockSpec((tm, D), lambda i: (i, 0)), _full((1, D)), _full((D_IN, D)), _full((1, D_IN))],
        out_specs=[pl.BlockSpec((tm, D_IN), lambda i: (i, 0)), pl.BlockSpec((tm, D), lambda i: (i, 0))],
        out_shape=[jax.ShapeDtypeStruct((T, D_IN), F32), jax.ShapeDtypeStruct((T, D), BF16)],
        scratch_shapes=[], args=(h, g_pre, w_in, b_in), side=side)


def _inproj_bwd(dqa, dka, dva, dqb, dkb, dvb, w_in, h, g_pre, dres, side=None):
    T, D = h.shape
    tm = TOKEN_TILE

    def body(dqa_ref, dka_ref, dva_ref, dqb_ref, dkb_ref, dvb_ref, w_ref, h_ref, g_ref, dres_ref,
             dh_ref, dz_ref, dbin_ref, dg_ref):
        i = pl.program_id(0)

        @pl.when(i == 0)
        def _():
            dbin_ref[...] = jnp.zeros_like(dbin_ref)
            dg_ref[...] = jnp.zeros_like(dg_ref)

        pieces = (dqa_ref, dka_ref, dva_ref, dqb_ref, dkb_ref, dvb_ref)
        dzb = jnp.concatenate([r[...].astype(BF16) for r in pieces], axis=1)
        dbin_ref[...] += _colsum(dzb.astype(F32))
        dz_ref[...] = dzb
        da = _mm(dzb, w_ref[...])
        x = h_ref[...]
        dx, dgain = _rms_bwd(x, _rstd(x), g_ref[...], da)
        dg_ref[...] += _colsum(dgain)
        dh_ref[...] = dres_ref[...] + dx

    def tile(w):
        return pl.BlockSpec((tm, w), lambda i: (i, 0))

    return _hosted_call(
        body, "inproj_bwd", (T // tm,),
        in_specs=[tile(A_Q), tile(A_KV), tile(A_KV), tile(B_W), tile(B_W), tile(B_W),
                  _full((D_IN, D)), tile(D), _full((1, D)), tile(D)],
        out_specs=[tile(D), tile(D_IN), _full((1, D_IN)), _full((1, D))],
        out_shape=[jax.ShapeDtypeStruct((T, D), F32), jax.ShapeDtypeStruct((T, D_IN), BF16),
                   jax.ShapeDtypeStruct((1, D_IN), F32), jax.ShapeDtypeStruct((1, D), F32)],
        scratch_shapes=[], args=(dqa, dka, dva, dqb, dkb, dvb, w_in, h, g_pre, dres), side=side)


def _bucket_tiles(patterns):
    i = np.arange(QBLK)[:, None]
    j = np.arange(2 * QBLK)[None, :]
    dist = QBLK + i - j
    max_exact = NUM_BUCKETS // 2
    tiles = []
    for dilation, max_dist in patterns:
        n = np.maximum(dist * dilation, 0)
        nf = np.maximum(n, 1).astype(np.float32)
        large = max_exact + (np.log(nf / np.float32(max_exact)) / np.float32(math.log(MAX_DISTANCE / max_exact))
                             * np.float32(NUM_BUCKETS - max_exact)).astype(np.int32)
        bucket = np.where(n < max_exact, n, np.minimum(large, NUM_BUCKETS - 1))
        tiles.append(np.where((dist >= 0) & (dist <= max_dist), bucket, -1))
    return jnp.asarray(np.stack(tiles).astype(np.int32))


def _bias_build(rel_bias, buckets, head0, name, side=None):
    n = buckets.shape[0]

    def body(bk_ref, rb_ref, o_ref):
        bk = bk_ref[...]
        base = jnp.where(bk < 0, NEG_INF, 0.0).astype(F32)
        for hd in range(N_HEAD_GROUP):
            o_ref[hd] = lax.fori_loop(
                0, NUM_BUCKETS, lambda b, acc, hd=hd: jnp.where(bk == b, rb_ref[b, head0 + hd], acc), base)

    outs, side_outs = _hosted_call(
        body, name, (n,),
        in_specs=[pl.BlockSpec((None, QBLK, 2 * QBLK), lambda p: (p, 0, 0)), pl.BlockSpec(memory_space=pltpu.SMEM)],
        out_specs=[pl.BlockSpec((None, N_HEAD_GROUP, QBLK, 2 * QBLK), lambda p: (p, 0, 0, 0))],
        out_shape=[jax.ShapeDtypeStruct((n, N_HEAD_GROUP, QBLK, 2 * QBLK), F32)],
        scratch_shapes=[], args=(buckets, rel_bias), side=side)
    return outs[0], side_outs


def _bias_grad(ds, buckets, name):
    n = buckets.shape[0]

    def body(ds_ref, bk_ref, o_ref):
        bk = bk_ref[...]
        row = lax.broadcasted_iota(jnp.int32, (NUM_BUCKETS, 2 * QBLK), 0)
        for hd in range(N_HEAD_GROUP):
            d = ds_ref[hd]
            per_key = jnp.zeros((NUM_BUCKETS, 2 * QBLK), F32)
            for b in range(NUM_BUCKETS):
                per_key = jnp.where(row == b, jnp.sum(jnp.where(bk == b, d, 0.0), axis=0, keepdims=True), per_key)
            o_ref[hd] = jnp.broadcast_to(jnp.sum(per_key, axis=1, keepdims=True), (NUM_BUCKETS, LANES))

    out = pl.pallas_call(
        body, name=name, grid=(n,),
        in_specs=[pl.BlockSpec((None, N_HEAD_GROUP, QBLK, 2 * QBLK), lambda p: (p, 0, 0, 0)),
                  pl.BlockSpec((None, QBLK, 2 * QBLK), lambda p: (p, 0, 0))],
        out_specs=pl.BlockSpec((None, N_HEAD_GROUP, NUM_BUCKETS, LANES), lambda p: (p, 0, 0, 0)),
        out_shape=jax.ShapeDtypeStruct((n, N_HEAD_GROUP, NUM_BUCKETS, LANES), F32),
        compiler_params=_params(1),
    )(ds, buckets)
    return out[:, :, :, 0].reshape(n * N_HEAD_GROUP, NUM_BUCKETS)


def _class_rows(start, dilation):
    if dilation == 1:
        return pl.ds(pl.multiple_of(start, QBLK), QBLK)
    return pl.ds(start, QBLK, stride=dilation)


def _starts_class(u, blocks_per_pass, n_blocks):
    return blocks_per_pass % n_blocks == 0 and u % n_blocks == 0


def _block_starts(idx, n_blocks, dilation):
    cls = idx // n_blocks
    n = idx % n_blocks
    cur = cls + dilation * QBLK * n
    prev = cls + dilation * QBLK * jnp.maximum(n - 1, 0)
    return n, cur, prev


class _HeadPair:
    def __init__(self, g, shared_kv):
        self.lane = lax.broadcasted_iota(jnp.int32, (1, LANES), 1)
        self.lower = self.lane < HEAD_DIM
        self.shared_kv = shared_kv
        self.key_lanes = (self.lane >= HEAD_DIM).astype(jnp.int32) == (g // 2)

    def stack(self, t):
        return jnp.concatenate([jnp.where(self.lower, t, 0.0), jnp.where(self.lower, 0.0, t)], axis=0).astype(BF16)

    def unstack(self, t2):
        return jnp.where(self.lower, t2[:QBLK], t2[QBLK:])

    def keys(self, t):
        if self.shared_kv:
            return jnp.where(self.key_lanes, t, pltpu.roll(t, HEAD_DIM, 1))
        return t

    def key_grads(self, t):
        if self.shared_kv:
            return jnp.where(self.key_lanes, t + pltpu.roll(t, HEAD_DIM, 1), 0.0)
        return t


def _attn_specs(T, qcol, kcol, vcol, shared_kv):
    kv = (lambda c: (lambda g: (0, c))) if shared_kv else (lambda c: (lambda g: (0, c + g)))
    return [pl.BlockSpec((T, LANES), lambda g: (0, qcol + g)),
            pl.BlockSpec((T, LANES), kv(kcol)),
            pl.BlockSpec((T, LANES), kv(vcol))]


def _attn_fwd(z, bias, sinks, patterns, qcol, kcol, vcol, shared_kv, name, side=None):
    T = z.shape[0]
    n_pat = len(patterns)
    has_sink = sinks is not None

    def body(*refs):
        if has_sink:
            sink_ref, refs = refs[0], refs[1:]
        q_ref, k_ref, v_ref, b_ref, o_ref, l_ref = refs[:6]
        po_scr = refs[6:6 + n_pat]
        pl_scr = refs[6 + n_pat:]
        g = pl.program_id(0)
        heads = _HeadPair(g, shared_kv)
        in_prev = lax.broadcasted_iota(jnp.int32, (2 * QBLK, 2 * QBLK), 1) < QBLK

        for pi, (dilation, _) in enumerate(patterns):
            n_blocks = T // (QBLK * dilation)

            def step(it, carry, pi=pi, dilation=dilation, n_blocks=n_blocks):
                blocks = []
                for u in range(FWD_BLOCKS):
                    n, cur, prev = _block_starts(it * FWD_BLOCKS + u, n_blocks, dilation)
                    rows_c, rows_p = _class_rows(cur, dilation), _class_rows(prev, dilation)
                    qm = heads.stack(q_ref[rows_c, :])
                    k_cur, v_cur = k_ref[rows_c, :], v_ref[rows_c, :]
                    no_past = _starts_class(u, FWD_BLOCKS, n_blocks)
                    if no_past:
                        k2, v2 = k_cur, v_cur
                    else:
                        if u % min(FWD_BLOCKS, n_blocks) == 0:
                            k_prev, v_prev = k_ref[rows_p, :], v_ref[rows_p, :]
                        k2 = jnp.concatenate([k_prev, k_cur], axis=0)
                        v2 = jnp.concatenate([v_prev, v_cur], axis=0)
                    k2, v2 = heads.keys(k2).astype(BF16), heads.keys(v2).astype(BF16)
                    k_prev, v_prev = k_cur, v_cur
                    blocks.append(dict(n=n, no_past=no_past, rows=rows_c, v2=v2, s=_mm_nt(qm, k2)))
                for b in blocks:
                    if b["no_past"]:
                        b["s"] = b["s"] * (HEAD_DIM ** -0.5) + b_ref[pi, :, QBLK:]
                    else:
                        s = b["s"] * (HEAD_DIM ** -0.5) + b_ref[pi]
                        b["s"] = jnp.where(jnp.logical_and(in_prev, b["n"] == 0), NEG_INF, s)
                    b["m"] = jnp.max(b["s"], axis=1, keepdims=True)
                for b in blocks:
                    b["pr"] = jnp.exp(b["s"] - b["m"])
                    b["den"] = jnp.sum(b["pr"], axis=1, keepdims=True)
                for b in blocks:
                    b["o2"] = _mm(b["pr"].astype(BF16), b["v2"])
                for b in blocks:
                    lse = b["m"] + jnp.log(b["den"])
                    po_scr[pi][b["rows"], :] = heads.unstack(b["o2"] / b["den"])
                    pl_scr[2 * pi][b["rows"], :] = jnp.broadcast_to(lse[:QBLK], (QBLK, LANES))
                    pl_scr[2 * pi + 1][b["rows"], :] = jnp.broadcast_to(lse[QBLK:], (QBLK, LANES))
                return carry

            lax.fori_loop(0, (dilation * n_blocks) // FWD_BLOCKS, step, 0)

        def merge(ci, carry):
            rows = pl.ds(pl.multiple_of(ci * QBLK, QBLK), QBLK)
            weights = []
            for hd in range(2):
                parts = [pl_scr[2 * pi + hd][rows, :] for pi in range(n_pat)]
                m = functools.reduce(jnp.maximum, parts)
                if has_sink:
                    sink = sink_ref[0, 2 * g + hd]
                    m = jnp.maximum(m, sink)
                terms = [jnp.exp(x - m) for x in parts]
                den = functools.reduce(jnp.add, terms)
                if has_sink:
                    den = den + jnp.exp(sink - m)
                l_ref[hd, rows, :] = m + jnp.log(den)
                inv = 1.0 / den
                weights.append([t * inv for t in terms])
            o_ref[rows, :] = functools.reduce(
                jnp.add, [jnp.where(heads.lower, weights[0][pi], weights[1][pi]) * po_scr[pi][rows, :]
                          for pi in range(n_pat)])
            return carry

        lax.fori_loop(0, T // QBLK, merge, 0)

    in_specs = _attn_specs(T, qcol, kcol, vcol, shared_kv)
    in_specs.append(pl.BlockSpec((n_pat, None, 2 * QBLK, 2 * QBLK), lambda g: (0, g, 0, 0)))
    args = [z, z, z, bias.reshape(n_pat, N_HEAD_GROUP // 2, 2 * QBLK, 2 * QBLK)]
    if has_sink:
        in_specs.insert(0, pl.BlockSpec(memory_space=pltpu.SMEM))
        args.insert(0, sinks)
    return _hosted_call(
        body, name, (N_HEAD_GROUP // 2,),
        in_specs=in_specs,
        out_specs=[pl.BlockSpec((T, LANES), lambda g: (0, g)), pl.BlockSpec((2, T, LANES), lambda g: (g, 0, 0))],
        out_shape=[jax.ShapeDtypeStruct((T, N_HEAD_GROUP * HEAD_DIM), F32),
                   jax.ShapeDtypeStruct((N_HEAD_GROUP, T, LANES), F32)],
        scratch_shapes=[pltpu.VMEM((T, LANES), F32)] * (3 * n_pat), args=args, side=side)


def _attn_bwd(z, bias, sinks, d_out, out, lse, patterns, qcol, kcol, vcol, shared_kv, name, side=None):
    T = z.shape[0]
    n_pat = len(patterns)
    has_sink = sinks is not None
    kv_width = LANES if shared_kv else N_HEAD_GROUP * HEAD_DIM

    def body(*refs):
        if has_sink:
            sink_ref, refs = refs[0], refs[1:]
        q_ref, k_ref, v_ref, b_ref, do_ref, o_ref, l0_ref, l1_ref = refs[:8]
        dq_ref, dk_ref, dv_ref, ds_ref = refs[8:12]
        dsink_ref = refs[12] if has_sink else None
        dq_acc, dk_acc, dv_acc = refs[-3:]
        g = pl.program_id(0)
        heads = _HeadPair(g, shared_kv)
        in_prev = lax.broadcasted_iota(jnp.int32, (2 * QBLK, 2 * QBLK), 1) < QBLK

        dq_acc[...] = jnp.zeros_like(dq_acc)
        ds_ref[...] = jnp.zeros_like(ds_ref)
        dk_acc[...] = jnp.zeros_like(dk_acc)
        dv_acc[...] = jnp.zeros_like(dv_acc)

        dsink = jnp.zeros((1, LANES), F32)
        for pi, (dilation, _) in enumerate(patterns):
            n_blocks = T // (QBLK * dilation)

            def step(idx, dsink, pi=pi, dilation=dilation, n_blocks=n_blocks):
                blocks = []
                for u in range(BWD_BLOCKS):
                    n, cur, prev = _block_starts(idx * BWD_BLOCKS + u, n_blocks, dilation)
                    rows_c, rows_p = _class_rows(cur, dilation), _class_rows(prev, dilation)
                    qm = heads.stack(q_ref[rows_c, :])
                    k_cur, v_cur = k_ref[rows_c, :], v_ref[rows_c, :]
                    first = u % min(BWD_BLOCKS, n_blocks) == 0
                    no_past = _starts_class(u, BWD_BLOCKS, n_blocks)
                    if no_past:
                        k2, v2 = k_cur, v_cur
                    else:
                        if first:
                            k_prev, v_prev = k_ref[rows_p, :], v_ref[rows_p, :]
                        k2 = jnp.concatenate([k_prev, k_cur], axis=0)
                        v2 = jnp.concatenate([v_prev, v_cur], axis=0)
                    k2, v2 = heads.keys(k2).astype(BF16), heads.keys(v2).astype(BF16)
                    k_prev, v_prev = k_cur, v_cur
                    d_o = do_ref[rows_c, :]
                    dom = heads.stack(d_o)
                    dd = d_o * o_ref[rows_c, :]
                    delta = jnp.concatenate([jnp.sum(jnp.where(heads.lower, dd, 0.0), axis=1, keepdims=True),
                                             jnp.sum(jnp.where(heads.lower, 0.0, dd), axis=1, keepdims=True)], axis=0)
                    lse = jnp.concatenate([l0_ref[rows_c, :], l1_ref[rows_c, :]], axis=0)
                    blocks.append(dict(n=n, first=first, no_past=no_past, rows_c=rows_c, rows_p=rows_p, qm=qm, k2=k2,
                                       dom=dom, delta=delta, lse=lse, s=_mm_nt(qm, k2), dp=_mm_nt(dom, v2)))
                for b in blocks:
                    if b["no_past"]:
                        s = b["s"] * (HEAD_DIM ** -0.5) + b_ref[pi, :, QBLK:]
                        b["pr"] = jnp.exp(s - b["lse"])
                    else:
                        s = b["s"] * (HEAD_DIM ** -0.5) + b_ref[pi]
                        s = jnp.where(jnp.logical_and(in_prev, b["n"] == 0), NEG_INF, s)
                        b["pr"] = jnp.exp(s - jnp.concatenate([b["lse"], b["lse"]], axis=1))
                    b["ds"] = b["pr"] * (b["dp"] - b["delta"])
                for b in blocks:
                    dsb = b["ds"].astype(BF16)
                    b["dq2"] = _mm(dsb, b["k2"])
                    b["dk2"] = _mm_tn(dsb, b["qm"])
                    b["dv2"] = _mm_tn(b["pr"].astype(BF16), b["dom"])
                for b in blocks:
                    b["dk2"] = heads.key_grads(b["dk2"]) * (HEAD_DIM ** -0.5)
                    b["dv2"] = heads.key_grads(b["dv2"])
                for u, b in enumerate(blocks):
                    dq_acc[b["rows_c"], :] += heads.unstack(b["dq2"]) * (HEAD_DIM ** -0.5)
                    if b["no_past"]:
                        ds_ref[pi, :, QBLK:] += b["ds"]
                        dk_own, dv_own = b["dk2"], b["dv2"]
                    else:
                        ds_ref[pi] += b["ds"]
                        dk_own, dv_own = b["dk2"][QBLK:], b["dv2"][QBLK:]
                    if u + 1 < len(blocks) and not blocks[u + 1]["first"]:
                        dk_own = dk_own + blocks[u + 1]["dk2"][:QBLK]
                        dv_own = dv_own + blocks[u + 1]["dv2"][:QBLK]
                    if b["first"] and not b["no_past"]:
                        dk_acc[b["rows_p"], :] += b["dk2"][:QBLK]
                        dv_acc[b["rows_p"], :] += b["dv2"][:QBLK]
                    dk_acc[b["rows_c"], :] += dk_own
                    dv_acc[b["rows_c"], :] += dv_own
                    if has_sink:
                        for hd in range(2):
                            rows_h = slice(QBLK * hd, QBLK * (hd + 1))
                            p_sink = jnp.exp(sink_ref[0, 2 * g + hd] - b["lse"][rows_h, 0:1])
                            dsink = dsink - jnp.where(heads.lane == 2 * g + hd,
                                                      jnp.sum(p_sink * b["delta"][rows_h]), 0.0)
                return dsink

            dsink = lax.fori_loop(0, (dilation * n_blocks) // BWD_BLOCKS, step, dsink)

        dq_ref[...] = dq_acc[...].astype(dq_ref.dtype)
        if shared_kv:
            @pl.when(g == 0)
            def _():
                dk_ref[...] = dk_acc[...]
                dv_ref[...] = dv_acc[...]

            @pl.when(g != 0)
            def _():
                dk_ref[...] += dk_acc[...]
                dv_ref[...] += dv_acc[...]
        else:
            dk_ref[...] = dk_acc[...].astype(dk_ref.dtype)
            dv_ref[...] = dv_acc[...].astype(dv_ref.dtype)

        if has_sink:
            @pl.when(g == 0)
            def _():
                dsink_ref[...] = dsink

            @pl.when(g != 0)
            def _():
                dsink_ref[...] += dsink

    pair = pl.BlockSpec((T, LANES), lambda g: (0, g))
    stacked = pl.BlockSpec((n_pat, None, 2 * QBLK, 2 * QBLK), lambda g: (0, g, 0, 0))
    stacked_shape = (n_pat, N_HEAD_GROUP // 2, 2 * QBLK, 2 * QBLK)
    in_specs = _attn_specs(T, qcol, kcol, vcol, shared_kv)
    in_specs += [stacked, pair, pair,
                 pl.BlockSpec((None, T, LANES), lambda g: (2 * g, 0, 0)),
                 pl.BlockSpec((None, T, LANES), lambda g: (2 * g + 1, 0, 0))]
    args = [z, z, z, bias.reshape(stacked_shape), d_out, out, lse, lse]
    kv_out = _full((T, LANES)) if shared_kv else pair
    out_specs = [pair, kv_out, kv_out, stacked]
    kv_dtype = F32 if shared_kv else BF16
    out_shape = [jax.ShapeDtypeStruct((T, N_HEAD_GROUP * HEAD_DIM), BF16),
                 jax.ShapeDtypeStruct((T, kv_width), kv_dtype), jax.ShapeDtypeStruct((T, kv_width), kv_dtype),
                 jax.ShapeDtypeStruct(stacked_shape, F32)]
    if has_sink:
        in_specs.insert(0, pl.BlockSpec(memory_space=pltpu.SMEM))
        args.insert(0, sinks)
        out_specs.append(_full((1, LANES)))
        out_shape.append(jax.ShapeDtypeStruct((1, LANES), F32))
    outs, side_outs = _hosted_call(
        body, name, (N_HEAD_GROUP // 2,), in_specs=in_specs, out_specs=out_specs, out_shape=out_shape,
        scratch_shapes=[pltpu.VMEM((T, LANES), F32)] * 3, args=args, side=side)
    outs = list(outs)
    outs[3] = outs[3].reshape(n_pat, N_HEAD_GROUP, QBLK, 2 * QBLK)
    return outs, side_outs


def _outproj_bwd(dh, att, g_post, w_out):
    T, D = dh.shape
    tm = TOKEN_TILE
    d_mix = w_out.shape[0]

    def body(dh_ref, att_ref, g_ref, w_ref, dma_ref, dmb_ref, datt_ref, dg_ref, db_ref):
        i = pl.program_id(0)

        @pl.when(i == 0)
        def _():
            dg_ref[...] = jnp.zeros_like(dg_ref)
            db_ref[...] = jnp.zeros_like(db_ref)

        att = att_ref[...]
        datt, dgain = _rms_bwd(att, _rstd(att), g_ref[...], dh_ref[...])
        dg_ref[...] += _colsum(dgain)
        db_ref[...] += _colsum(datt)
        dattb = datt.astype(BF16)
        datt_ref[...] = dattb
        dmix = _mm_nt(dattb, w_ref[...])
        dma_ref[...] = dmix[:, :A_Q]
        dmb_ref[...] = dmix[:, A_Q:]

    def tile(w):
        return pl.BlockSpec((tm, w), lambda i: (i, 0))

    return pl.pallas_call(
        body, name="outproj_bwd", grid=(T // tm,),
        in_specs=[tile(D), tile(D), _full((1, D)), _full((d_mix, D))],
        out_specs=[tile(A_Q), tile(B_W), tile(D), _full((1, D)), _full((1, D))],
        out_shape=[jax.ShapeDtypeStruct((T, A_Q), F32), jax.ShapeDtypeStruct((T, B_W), F32),
                   jax.ShapeDtypeStruct((T, D), BF16), jax.ShapeDtypeStruct((1, D), F32),
                   jax.ShapeDtypeStruct((1, D), F32)],
        compiler_params=_params(1),
    )(dh, att, g_post, w_out)


def _ple_fwd_bwd(h, g_pre, w_gate, p, w_proj, g_post, target):
    T, D = h.shape
    tm = TOKEN_TILE
    n_proj, ple, db = w_proj.shape

    def body(h_ref, gpre_ref, wg_ref, p_ref, wp_ref, gpost_ref, t_ref,
             a_ref, dpre_ref, de_ref, dh_ref, loss_ref, dgpost_ref, dgpre_ref):
        i = pl.program_id(0)

        @pl.when(i == 0)
        def _():
            loss_ref[...] = jnp.zeros_like(loss_ref)
            dgpost_ref[...] = jnp.zeros_like(dgpost_ref)
            dgpre_ref[...] = jnp.zeros_like(dgpre_ref)

        x = h_ref[...]
        rx = _rstd(x)
        a = (x * rx * gpre_ref[...]).astype(BF16)
        a_ref[...] = a
        gate = jax.nn.sigmoid(_mm(a, wg_ref[...]))
        pb = p_ref[...].astype(BF16)
        e = jnp.concatenate([_mm(pb, wp_ref[k]) for k in range(n_proj)], axis=1)
        ge = gate * e
        rg = _rstd(ge)
        diff = x + ge * rg * gpost_ref[...] - t_ref[...]
        loss_ref[...] += 0.5 * jnp.sum(jnp.mean(diff * diff, axis=1, keepdims=True))
        dy = diff * (1.0 / D)
        dge, dgain = _rms_bwd(ge, rg, gpost_ref[...], dy)
        dgpost_ref[...] += _colsum(dgain)
        de_ref[...] = (dge * gate).astype(BF16)
        dpre = (dge * e * gate * (1.0 - gate)).astype(BF16)
        dpre_ref[...] = dpre
        dx, dgain = _rms_bwd(x, rx, gpre_ref[...], _mm_nt(dpre, wg_ref[...]))
        dgpre_ref[...] += _colsum(dgain)
        dh_ref[...] = dy + dx

    def tile(w):
        return pl.BlockSpec((tm, w), lambda i: (i, 0))

    return pl.pallas_call(
        body, name="ple_fwd_bwd", grid=(T // tm,),
        in_specs=[tile(D), _full((1, D)), _full((D, D)), tile(ple), _full((n_proj, ple, db)), _full((1, D)), tile(D)],
        out_specs=[tile(D), tile(D), tile(D), tile(D), _full((1, LANES)), _full((1, D)), _full((1, D))],
        out_shape=[jax.ShapeDtypeStruct((T, D), BF16),
                   jax.ShapeDtypeStruct((T, D), BF16),
                   jax.ShapeDtypeStruct((T, D), BF16),
                   jax.ShapeDtypeStruct((T, D), F32),
                   jax.ShapeDtypeStruct((1, LANES), F32),
                   jax.ShapeDtypeStruct((1, D), F32),
                   jax.ShapeDtypeStruct((1, D), F32)],
        compiler_params=_params(1),
    )(h, g_pre, w_gate, p, w_proj, g_post, target)


def _ple_dw_proj(p, de, n_proj):
    T, ple = p.shape
    D = de.shape[1]
    db = D // n_proj
    tk = TOKEN_TILE
    nt = T // tk

    def body(p_ref, de_ref, o_ref, acc):
        t = pl.program_id(0)

        @pl.when(t == 0)
        def _():
            acc[...] = jnp.zeros_like(acc)

        acc[...] += _mm_tn(p_ref[...].astype(BF16), de_ref[...])

        @pl.when(t == nt - 1)
        def _():
            for k in range(n_proj):
                o_ref[k] = acc[:, k * db:(k + 1) * db].astype(BF16)

    return pl.pallas_call(
        body, name="ple_dw_proj", grid=(nt,),
        in_specs=[pl.BlockSpec((tk, ple), lambda t: (t, 0)), pl.BlockSpec((tk, D), lambda t: (t, 0))],
        out_specs=_full((n_proj, ple, db)), out_shape=jax.ShapeDtypeStruct((n_proj, ple, db), BF16),
        scratch_shapes=[pltpu.VMEM((ple, D), F32)], compiler_params=_params(1),
    )(p, de)


def _tok(width):
    return pl.BlockSpec((DW_TILE, width), lambda b, t: (t, 0))


def _dw_gu(a, dgu, name, side=None):
    T, D = a.shape
    nj, _, _, FB = dgu.shape
    return _tn_matmul(
        dgu, a, pl.BlockSpec((None, None, DW_TILE, FB), lambda b, t: (b % nj, b // nj, t, 0)), _tok(D),
        jax.ShapeDtypeStruct((2 * nj, FB, D), BF16), pl.BlockSpec((None, FB, D), lambda b, t: (b, 0, 0)),
        2 * nj, T // DW_TILE, (FB, D), name, side=side)


def _dw_down(hh, df, name, side=None):
    nj, T, FB = hh.shape
    D = df.shape[1]
    return _tn_matmul(
        hh, df, pl.BlockSpec((None, DW_TILE, FB), lambda b, t: (b, t, 0)), _tok(D),
        jax.ShapeDtypeStruct((nj, FB, D), BF16), pl.BlockSpec((None, FB, D), lambda b, t: (b, 0, 0)),
        nj, T // DW_TILE, (FB, D), name, side=side)


def _dw_rows(xm, y, name):
    T, k = xm.shape
    D = y.shape[1]
    out = _tn_matmul(
        xm, y, _tok(k), _tok(D), jax.ShapeDtypeStruct((k, D), BF16), _full((k, D)),
        1, T // DW_TILE, (k, D), name)
    return out.reshape(N_DEV, k // N_DEV, D)


def _cast_bf16(arrays):
    n = len(arrays)

    def body(*refs):
        for a in range(n):
            refs[n + a][...] = refs[a][...].astype(BF16)

    return pl.pallas_call(
        body, name="cast_shards",
        in_specs=[pl.BlockSpec(memory_space=pltpu.VMEM)] * n, out_specs=[pl.BlockSpec(memory_space=pltpu.VMEM)] * n,
        out_shape=[jax.ShapeDtypeStruct(a.shape, BF16) for a in arrays],
        compiler_params=pltpu.CompilerParams(vmem_limit_bytes=VMEM_LIMIT),
    )(*arrays)


def _pack_layout(D, n_rel_rows):
    n_bin = -(-D_IN // D)
    row_bin = len(GAINS)
    row_sink = row_bin + n_bin
    row_loss = row_sink + 1
    row_rb = -(-(row_loss + 1) // 8) * 8
    n_rows = row_rb + -(-n_rel_rows // 8) * 8
    bin_parts = [(r, min(D, D_IN - r * D)) for r in range(n_bin)]
    return row_bin, row_sink, row_loss, row_rb, n_rows, bin_parts


def _pair_swap_call(grad_blocks):
    def body(g_in, received, send_sems, recv_sems):
        start, _, wait = _pair_swap([g_in], [received], send_sems, recv_sems)
        start()
        wait()

    any_spec = pl.BlockSpec(memory_space=pl.ANY)
    return pl.pallas_call(
        body, name="pair_swap", in_specs=[any_spec], out_specs=any_spec,
        out_shape=jax.ShapeDtypeStruct((N_CHIPS,) + grad_blocks.shape[1:], grad_blocks.dtype),
        scratch_shapes=[pltpu.SemaphoreType.DMA((1, N_CHIPS)), pltpu.SemaphoreType.DMA((1, N_CHIPS))],
    )(grad_blocks)


def _pair_add(blocks, received, name):
    n, R, C = received.shape
    rows = _adamw_rows(R)
    core = lax.axis_index("c").astype(jnp.int32).reshape(1)

    def body(core_ref, a_ref, b_ref, o_ref):
        o_ref[...] = (a_ref[...].astype(F32) + b_ref[...].astype(F32)).astype(o_ref.dtype)

    tile = pl.BlockSpec((None, rows, C), lambda q, r, core_ref: (q, r, 0))
    return pl.pallas_call(
        body, name=name,
        grid_spec=pltpu.PrefetchScalarGridSpec(
            num_scalar_prefetch=1, grid=(n, R // rows),
            in_specs=[pl.BlockSpec((None, rows, C), lambda q, r, core_ref: (2 * q + core_ref[0], r, 0)), tile],
            out_specs=tile),
        out_shape=jax.ShapeDtypeStruct(received.shape, received.dtype), compiler_params=_params(2),
    )(core, blocks, received)


def _final_exchange(grad_blocks, partials, loss):
    D = partials["ffn1_pre_g"].shape[1]
    rb_shape = partials["rel_bias"].shape
    row_bin, row_sink, row_loss, row_rb, n_rows, bin_parts = _pack_layout(D, rb_shape[0])
    n_small = len(SMALL)

    def body(*refs):
        g_in = refs[0]
        part = dict(zip(SMALL, refs[1:1 + n_small]))
        loss_ref = refs[1 + n_small]
        landed, gath, pack, send_sems, recv_sems, local_sems = refs[2 + n_small:]

        pack[...] = jnp.zeros_like(pack)
        for i, name in enumerate(GAINS):
            pack[i:i + 1, :] = part[name][...]
        for r, width in bin_parts:
            pack[row_bin + r:row_bin + r + 1, 0:width] = part["b_in"][:, r * D:r * D + width]
        pack[row_sink:row_sink + 1, 0:LANES] = part["sinks"][...]
        pack[row_loss:row_loss + 1, 0:LANES] = loss_ref[...]
        pack[row_rb:row_rb + rb_shape[0], 0:rb_shape[1]] = part["rel_bias"][...]

        small_start, _, small_wait = _side_copies("gather", [pack], [gath], send_sems, recv_sems, local_sems, sem_row=0)
        big_start, _, big_wait = _quad_exchange([g_in], [landed], send_sems, recv_sems, local_sems, sem_row=1)
        small_start()
        big_start()
        small_wait()
        big_wait()

    args = [grad_blocks] + [partials[k] for k in SMALL] + [loss]
    vmem = pl.BlockSpec(memory_space=pltpu.VMEM)
    any_spec = pl.BlockSpec(memory_space=pl.ANY)
    return pl.pallas_call(
        body, name="final_exchange",
        in_specs=[any_spec] + [vmem] * (n_small + 1),
        out_specs=[any_spec, any_spec],
        out_shape=[jax.ShapeDtypeStruct(grad_blocks.shape, grad_blocks.dtype),
                   jax.ShapeDtypeStruct((N_DEV, n_rows, D), F32)],
        scratch_shapes=[pltpu.VMEM((n_rows, D), F32), pltpu.SemaphoreType.DMA((2, 7)),
                        pltpu.SemaphoreType.DMA((2, 7)), pltpu.SemaphoreType.DMA((2, N_CHIPS))],
    )(*args)


def _adamw(w, g, m, v):
    m = ADAM_B1 * m + (1.0 - ADAM_B1) * g
    v = ADAM_B2 * v + (1.0 - ADAM_B2) * (g * g)
    m_hat = m / (1.0 - ADAM_B1 ** ADAM_STEP)
    v_hat = v / (1.0 - ADAM_B2 ** ADAM_STEP)
    return -ADAM_LR * (m_hat / (jnp.sqrt(v_hat) + ADAM_EPS) + ADAM_WD * w), m, v


def _sum_adamw(partials, w, m, v, rows, name):
    R, C = w.shape
    n = partials.shape[0]

    def body(p_ref, w_ref, m_ref, v_ref, g_ref, d_ref, nm_ref, nv_ref):
        g = p_ref[0].astype(F32)
        for k in range(1, n):
            g = g + p_ref[k].astype(F32)
        g_ref[...] = g
        d_ref[...], nm_ref[...], nv_ref[...] = _adamw(w_ref[...], g, m_ref[...], v_ref[...])

    tile = pl.BlockSpec((rows, C), lambda i: (i, 0))
    return pl.pallas_call(
        body, name=name, grid=(R // rows,),
        in_specs=[pl.BlockSpec((n, rows, C), lambda i: (0, i, 0)), tile, tile, tile],
        out_specs=[tile] * 4, out_shape=[jax.ShapeDtypeStruct((R, C), F32)] * 4,
        compiler_params=_params(1),
    )(partials, w, m, v)


def _small_adamw(gathered, ws, ms, vs):
    D = ws["ffn1_pre_g"].shape[1]
    n_sink = ws["sinks"].shape[1]
    rb_shape = ws["rel_bias"].shape
    row_bin, row_sink, row_loss, row_rb, n_rows, bin_parts = _pack_layout(D, rb_shape[0])
    n_small = len(SMALL)

    def body(*refs):
        gath = refs[0]
        pos = 1
        w_ref = dict(zip(SMALL, refs[pos:pos + n_small]))
        m_ref = dict(zip(SMALL, refs[pos + n_small:pos + 2 * n_small]))
        v_ref = dict(zip(SMALL, refs[pos + 2 * n_small:pos + 3 * n_small]))
        pos += 3 * n_small
        outs = {name: refs[pos + 4 * i:pos + 4 * i + 4] for i, name in enumerate(SMALL)}
        loss_out = refs[pos + 4 * n_small]
        pack = refs[pos + 4 * n_small + 1]

        total = gath[0]
        for k in range(1, N_DEV):
            total = total + gath[k]
        pack[...] = total

        def update(name, g):
            g_out, d_out, m_out, v_out = outs[name]
            g_out[...] = g
            d_out[...], m_out[...], v_out[...] = _adamw(w_ref[name][...], g, m_ref[name][...], v_ref[name][...])

        for i, name in enumerate(GAINS):
            update(name, pack[i:i + 1, :])
        update("b_in", jnp.concatenate([pack[row_bin + r:row_bin + r + 1, 0:width] for r, width in bin_parts], axis=1))
        update("sinks", pack[row_sink:row_sink + 1, 0:n_sink])
        update("rel_bias", pack[row_rb:row_rb + rb_shape[0], 0:rb_shape[1]])
        loss_out[...] = pack[row_loss:row_loss + 1, 0:LANES]

    args = [gathered]
    for group in (ws, ms, vs):
        args += [group[k] for k in SMALL]
    out_shape = []
    for name in SMALL:
        out_shape += [jax.ShapeDtypeStruct(ws[name].shape, F32)] * 4
    out_shape.append(jax.ShapeDtypeStruct((1, LANES), F32))
    res = pl.pallas_call(
        body, name="small_adamw",
        in_specs=[pl.BlockSpec(memory_space=pltpu.VMEM)] * len(args),
        out_specs=[pl.BlockSpec(memory_space=pltpu.VMEM)] * len(out_shape),
        out_shape=out_shape,
        scratch_shapes=[pltpu.VMEM((n_rows, D), F32)],
    )(*args)
    per_name = {name: res[4 * i:4 * i + 4] for i, name in enumerate(SMALL)}
    return per_name, res[-1]


COLUMN_SHARDED = ("ffn1_w_gu", "ffn2_w_gu", "w_in")


def _adamw_rows(rows_total):
    return max(r for r in range(16, min(rows_total, 256) + 1, 16) if rows_total % r == 0)


def kernel(x, p, rel_bias, ffn1_pre_g, ffn1_w_gu, ffn1_w_down, ffn1_post_g, attn_pre_g, w_in, b_in, sinks, w_out, b_out, attn_post_g, ffn2_pre_g, ffn2_w_gu, ffn2_w_down, ffn2_post_g, ple_pre_g, w_ple_gate, w_ple_proj, ple_post_g, loss_target, m_rel_bias, m_ffn1_pre_g, m_ffn1_w_gu, m_ffn1_w_down, m_ffn1_post_g, m_attn_pre_g, m_w_in, m_b_in, m_sinks, m_w_out, m_b_out, m_attn_post_g, m_ffn2_pre_g, m_ffn2_w_gu, m_ffn2_w_down, m_ffn2_post_g, m_ple_pre_g, m_w_ple_gate, m_w_ple_proj, m_ple_post_g, v_rel_bias, v_ffn1_pre_g, v_ffn1_w_gu, v_ffn1_w_down, v_ffn1_post_g, v_attn_pre_g, v_w_in, v_b_in, v_sinks, v_w_out, v_b_out, v_attn_post_g, v_ffn2_pre_g, v_ffn2_w_gu, v_ffn2_w_down, v_ffn2_post_g, v_ple_pre_g, v_w_ple_gate, v_w_ple_proj, v_ple_post_g):
    given = dict(locals())
    ws = {k: given[k] for k in WEIGHTS}
    ms = {k: given["m_" + k] for k in WEIGHTS}
    vs = {k: given["v_" + k] for k in WEIGHTS}

    def shard(t):
        return t.reshape(t.shape[1:])

    xs, ps, target = shard(x), shard(shard(p)), shard(loss_target)
    T, D = xs.shape
    small = {k: ws[k] for k in SMALL}

    def local(group, k):
        t = shard(group[k])
        return jnp.swapaxes(t, 0, 1) if k in COLUMN_SHARDED else t

    shards = {k: local(ws, k) for k in BIG}

    cast = dict(zip(BIG, _cast_bf16([shards[k] for k in BIG])))
    buckets_a = _bucket_tiles(PATTERNS_A)
    buckets_b = _bucket_tiles(PATTERNS_B)
    bias_a, _ = _bias_build(small["rel_bias"], buckets_a, 0, "bias_build_a")
    bias_b, (w_gu1, w_down1) = _bias_build(
        small["rel_bias"], buckets_b, N_HEAD_GROUP, "bias_build_b",
        side=("relay_gather", [cast["ffn1_w_gu"], cast["ffn1_w_down"]]))
    w_down1 = w_down1.reshape(-1, D)
    a_cfg = dict(patterns=PATTERNS_A, qcol=Q_A_COL, kcol=K_A_COL, vcol=V_A_COL, shared_kv=True)
    b_cfg = dict(patterns=PATTERNS_B, qcol=Q_B_COL, kcol=K_B_COL, vcol=V_B_COL, shared_kv=False)

    (h1, f1, a1, gu1), (w_in_g, w_down2) = _ffn_fwd(
        xs, small["ffn1_pre_g"], small["ffn1_post_g"], w_gu1, w_down1, "ffn1_fwd",
        side=("relay_gather", [cast["w_in"], cast["ffn2_w_down"]]))
    w_in_full = w_in_g.reshape(D_IN, D)
    w_down2 = w_down2.reshape(-1, D)
    (z, a2), (w_out_g,) = _inproj_fwd(h1, small["attn_pre_g"], w_in_full, small["b_in"],
                                      side=("relay_gather", [cast["w_out"]]))
    w_out_full = w_out_g.reshape(-1, D)
    (mix_a, lse_a), (w_gate, w_proj) = _attn_fwd(
        z, bias_a, small["sinks"], name="attn_a_fwd", **a_cfg,
        side=("relay_gather", [cast["w_ple_gate"], cast["w_ple_proj"]]))
    w_gate = w_gate.reshape(-1, D)
    (mix_b, lse_b), (w_gu2,) = _attn_fwd(
        z, bias_b, None, name="attn_b_fwd", **b_cfg, side=("relay_gather", [cast["ffn2_w_gu"]]))
    (h3, f2, a3, gu2, att, h2, mix), _ = _ffn_fwd(
        h1, small["ffn2_pre_g"], small["ffn2_post_g"], w_gu2, w_down2, "ffn2_fwd",
        attn=(mix_a, mix_b, w_out_full, small["b_out"], small["attn_post_g"]))
    a4, dpre, de, dh3, loss, dg_ple_post, dg_ple_pre = _ple_fwd_bwd(
        h3, small["ple_pre_g"], w_gate, ps, w_proj, small["ple_post_g"], target)

    d_gate = _dw_rows(a4, dpre, "ple_dw_gate")
    d_proj = _ple_dw_proj(ps, de, N_DEV)
    landed = {}
    (dh2, df2, hh2, dgu2, dg_f2_post, dg_f2_pre), (landed["w_ple_gate"], landed["w_ple_proj"]) = _ffn_bwd(
        dh3, f2, small["ffn2_post_g"], h2, small["ffn2_pre_g"], gu2, w_gu2, w_down2, "ffn2_bwd",
        side=("exchange", [d_gate, d_proj]))
    d_gu2 = _dw_gu(a3, dgu2, "ffn2_dw_gu")
    d_down2 = _dw_down(hh2, df2, "ffn2_dw_down").reshape(N_DEV, -1, D)
    dmix_a, dmix_b, datt, dg_attn_post, db_out = _outproj_bwd(dh2, att, small["attn_post_g"], w_out_full)
    d_out = _dw_rows(mix, datt, "attn_dw_out")
    (dqa, dka, dva, ds_a, dsinks), _ = _attn_bwd(
        z, bias_a, small["sinks"], dmix_a, mix_a, lse_a, name="attn_a_bwd", **a_cfg)
    (dqb, dkb, dvb, ds_b), (landed["ffn2_w_gu"],) = _attn_bwd(
        z, bias_b, None, dmix_b, mix_b, lse_b, name="attn_b_bwd", **b_cfg, side=("exchange", [d_gu2]))
    (dh1, dz, db_in, dg_attn_pre), (landed["w_out"],) = _inproj_bwd(
        dqa, dka, dva, dqb, dkb, dvb, w_in_full, h1, small["attn_pre_g"], dh2, side=("exchange", [d_out]))
    cols = D_IN // 3
    d_in = _tn_matmul(
        dz, a2, pl.BlockSpec((DW_TILE, cols), lambda b, t: (t, b)), _tok(D),
        jax.ShapeDtypeStruct((D_IN, D), BF16), pl.BlockSpec((cols, D), lambda b, t: (b, 0)),
        3, T // DW_TILE, (cols, D), "attn_dw_in").reshape(N_DEV, D_IN // N_DEV, D)
    (grad_x, df1, hh1, dgu1, dg_f1_post, dg_f1_pre), (landed["w_in"], landed["ffn2_w_down"]) = _ffn_bwd(
        dh1, f1, small["ffn1_post_g"], xs, small["ffn1_pre_g"], gu1, w_gu1, w_down1, "ffn1_bwd",
        side=("exchange", [d_in, d_down2]))
    d_down1 = _dw_down(hh1, df1, "ffn1_dw_down").reshape(N_DEV, -1, D)
    d_gu1, (landed["ffn1_w_down"],) = _dw_gu(a1, dgu1, "ffn1_dw_gu", side=("exchange", [d_down1]))

    rb_a = _bias_grad(ds_a, buckets_a, "bias_grad_a")
    rb_b = _bias_grad(ds_b, buckets_b, "bias_grad_b").reshape(len(PATTERNS_B), N_HEAD_GROUP, NUM_BUCKETS)
    d_rel_bias = jnp.concatenate([rb_a.T, jnp.sum(rb_b, axis=0).T], axis=1)
    small_grads = {"ffn1_pre_g": dg_f1_pre, "ffn1_post_g": dg_f1_post, "attn_pre_g": dg_attn_pre,
                   "attn_post_g": dg_attn_post, "ffn2_pre_g": dg_f2_pre, "ffn2_post_g": dg_f2_post,
                   "ple_pre_g": dg_ple_pre, "ple_post_g": dg_ple_post, "b_out": db_out, "b_in": db_in,
                   "sinks": dsinks, "rel_bias": d_rel_bias}
    d_gu1_pairs = _pair_add(d_gu1, _pair_swap_call(d_gu1), "ffn1_dw_gu_pair_add")
    landed["ffn1_w_gu"], small_gathered = _final_exchange(d_gu1_pairs, small_grads, loss)

    result = {}
    for k in BIG:
        outs = _sum_adamw(landed[k], shards[k], local(ms, k), local(vs, k), _adamw_rows(shards[k].shape[0]),
                          k + "_adamw")
        if k in COLUMN_SHARDED:
            outs = [jnp.swapaxes(o, 0, 1) for o in outs]
        result[k] = [o.reshape(ws[k].shape) for o in outs]
    small_res, loss_all = _small_adamw(
        small_gathered, small, {k: ms[k] for k in SMALL}, {k: vs[k] for k in SMALL})
    result.update(small_res)

    out = [loss_all[0, 0], grad_x.reshape(x.shape)]
    for i in range(4):
        out += [result[k][i] for k in WEIGHTS]
    return tuple(out)
```

```python
import functools
import math

import numpy as np
import jax
import jax.numpy as jnp
from jax import lax
from jax.experimental import pallas as pl
from jax.experimental.pallas import tpu as pltpu

F32 = jnp.float32
BF16 = jnp.bfloat16
MESH = pl.DeviceIdType.MESH

N_DEV = 8
EPS = 1e-6
NEG_INF = -1e30
HEAD_DIM = 64
LANES = 128
QBLK = 128
D_IN = 2304
A_Q, A_KV, B_W = 512, 128, 512
N_HEAD_GROUP = 8
NUM_BUCKETS = 32
MAX_DISTANCE = 2048
PATTERNS_A = ((1, 127),)
PATTERNS_B = ((1, 128), (4, 128), (16, 128))
Q_A_COL, K_A_COL, V_A_COL = 0, 4, 5
Q_B_COL, K_B_COL, V_B_COL = 6, 10, 14

ADAM_LR, ADAM_B1, ADAM_B2, ADAM_EPS, ADAM_WD, ADAM_STEP = 0.001, 0.9, 0.999, 1e-08, 0.01, 10

TOKEN_TILE = 512
DW_TILE = 1024
FWD_BLOCKS = 4
BWD_BLOCKS = 4
VMEM_LIMIT = 56 * 1024 * 1024
ARB = "arbitrary"

BIG = ("ffn1_w_gu", "ffn1_w_down", "w_in", "w_out", "ffn2_w_gu", "ffn2_w_down", "w_ple_gate", "w_ple_proj")
GAINS = ("ffn1_pre_g", "ffn1_post_g", "attn_pre_g", "attn_post_g", "ffn2_pre_g", "ffn2_post_g",
         "ple_pre_g", "ple_post_g", "b_out")
SMALL = GAINS + ("b_in", "sinks", "rel_bias")
WEIGHTS = ("rel_bias", "ffn1_pre_g", "ffn1_w_gu", "ffn1_w_down", "ffn1_post_g", "attn_pre_g", "w_in", "b_in",
           "sinks", "w_out", "b_out", "attn_post_g", "ffn2_pre_g", "ffn2_w_gu", "ffn2_w_down", "ffn2_post_g",
           "ple_pre_g", "w_ple_gate", "w_ple_proj", "ple_post_g")


def _params(n_axes):
    return pltpu.CompilerParams(dimension_semantics=(ARB,) * n_axes, vmem_limit_bytes=VMEM_LIMIT)


def _mm(a, b):
    return jnp.dot(a, b, preferred_element_type=F32)


def _mm_nt(a, b):
    return lax.dot_general(a, b, (((1,), (1,)), ((), ())), preferred_element_type=F32)


def _mm_tn(a, b):
    return lax.dot_general(a, b, (((0,), (0,)), ((), ())), preferred_element_type=F32)


def _rstd(x):
    return lax.rsqrt(jnp.mean(x * x, axis=-1, keepdims=True) + EPS)


def _rms_bwd(x, r, gain, dy):
    n = x * r
    gdy = dy * gain
    return r * (gdy - n * jnp.mean(gdy * n, axis=-1, keepdims=True)), dy * n


def _colsum(v):
    return jnp.sum(v, axis=0, keepdims=True)


def _full(shape):
    return pl.BlockSpec(shape, lambda *_: (0,) * len(shape))


def _mesh_place():
    return lax.axis_index("x"), lax.axis_index("y"), lax.axis_index("c")


def _slot(dev):
    return 4 * dev[0] + 2 * dev[1] + dev[2]


def _peers(x, y, c):
    out = []
    for flip in range(1, N_DEV):
        dx, dy, dc = (flip >> 2) & 1, (flip >> 1) & 1, flip & 1
        out.append((1 - x if dx else x, 1 - y if dy else y, 1 - c if dc else c))
    return out


def _side_copies(kind, ins, outs, send_sems, recv_sems, local_sems, sem_row=0):
    n = len(ins)
    x, y, c = _mesh_place()
    me = _slot((x, y, c))
    peers = _peers(x, y, c)

    def src(a, block):
        return ins[a] if kind == "gather" else ins[a].at[block]

    def send(a, k, peer):
        return pltpu.make_async_remote_copy(
            src_ref=src(a, _slot(peer)), dst_ref=outs[a].at[me],
            send_sem=send_sems.at[sem_row + a, k], recv_sem=recv_sems.at[sem_row + a, k],
            device_id=peer, device_id_type=MESH)

    def arrival(a, k, peer):
        return pltpu.make_async_remote_copy(
            src_ref=src(a, _slot(peer)), dst_ref=outs[a].at[_slot(peer)],
            send_sem=send_sems.at[sem_row + a, k], recv_sem=recv_sems.at[sem_row + a, k],
            device_id=peer, device_id_type=MESH)

    def own(a):
        return pltpu.make_async_copy(src(a, me), outs[a].at[me], local_sems.at[sem_row + a, 0])

    def start():
        for k, peer in enumerate(peers):
            for a in range(n):
                send(a, k, peer).start()
        for a in range(n):
            own(a).start()

    def wait():
        for k, peer in enumerate(peers):
            for a in range(n):
                arrival(a, k, peer).wait_recv()
        for k, peer in enumerate(peers):
            for a in range(n):
                send(a, k, peer).wait_send()
        for a in range(n):
            own(a).wait()

    return start, None, wait


N_CHIPS = N_DEV // 2


def _pair_swap(ins, received, send_sems, recv_sems):
    n = len(ins)
    x, y, c = _mesh_place()
    sibling = (x, y, 1 - c)

    def send(a, q):
        return pltpu.make_async_remote_copy(
            src_ref=ins[a].at[2 * q + (1 - c)], dst_ref=received[a].at[q],
            send_sem=send_sems.at[a, q], recv_sem=recv_sems.at[a, q], device_id=sibling, device_id_type=MESH)

    def start():
        for a in range(n):
            for q in range(N_CHIPS):
                send(a, q).start()

    def wait():
        for a in range(n):
            for q in range(N_CHIPS):
                send(a, q).wait_recv()
        for a in range(n):
            for q in range(N_CHIPS):
                send(a, q).wait_send()

    return start, None, wait


def _quad_exchange(ins, outs, send_sems, recv_sems, local_sems, sem_row=0):
    n = len(ins)
    x, y, c = _mesh_place()
    mine = 2 * x + y
    chips = [(1 - x, y), (x, 1 - y), (1 - x, 1 - y)]

    def send(a, k, chip):
        return pltpu.make_async_remote_copy(
            src_ref=ins[a].at[2 * chip[0] + chip[1]], dst_ref=outs[a].at[mine],
            send_sem=send_sems.at[sem_row + a, k], recv_sem=recv_sems.at[sem_row + a, k],
            device_id=(chip[0], chip[1], c), device_id_type=MESH)

    def arrival(a, k, chip):
        return pltpu.make_async_remote_copy(
            src_ref=ins[a].at[2 * chip[0] + chip[1]], dst_ref=outs[a].at[2 * chip[0] + chip[1]],
            send_sem=send_sems.at[sem_row + a, k], recv_sem=recv_sems.at[sem_row + a, k],
            device_id=(chip[0], chip[1], c), device_id_type=MESH)

    def own(a):
        return pltpu.make_async_copy(ins[a].at[mine], outs[a].at[mine], local_sems.at[sem_row + a, 0])

    def start():
        for k, chip in enumerate(chips):
            for a in range(n):
                send(a, k, chip).start()
        for a in range(n):
            own(a).start()

    def wait():
        for k, chip in enumerate(chips):
            for a in range(n):
                arrival(a, k, chip).wait_recv()
        for k, chip in enumerate(chips):
            for a in range(n):
                send(a, k, chip).wait_send()
        for a in range(n):
            own(a).wait()

    return start, None, wait


def _relay_gather(ins, outs, send_sems, recv_sems, local_sems):
    n = len(ins)
    x, y, c = _mesh_place()
    me, sibling = (x, y, c), (x, y, 1 - c)
    chips = [(1 - x, y), (x, 1 - y), (1 - x, 1 - y)]

    def copy(a, k, block, to, src=None):
        dst = outs[a].at[_slot(block)]
        return pltpu.make_async_remote_copy(
            src_ref=dst if src is None else src, dst_ref=dst,
            send_sem=send_sems.at[a, k], recv_sem=recv_sems.at[a, k], device_id=to, device_id_type=MESH)

    def own(a):
        return pltpu.make_async_copy(ins[a], outs[a].at[_slot(me)], local_sems.at[a, 0])

    def start():
        for j, chip in enumerate(chips):
            for a in range(n):
                copy(a, 1 + j, me, (*chip, c), src=ins[a]).start()
        for a in range(n):
            copy(a, 0, me, sibling, src=ins[a]).start()
            own(a).start()

    def relay():
        for j, chip in enumerate(chips):
            for a in range(n):
                copy(a, 1 + j, (*chip, c), me).wait_recv()
                copy(a, 4 + j, (*chip, c), sibling).start()

    def wait():
        for a in range(n):
            copy(a, 0, sibling, me).wait_recv()
        for j, chip in enumerate(chips):
            for a in range(n):
                copy(a, 4 + j, (*chip, 1 - c), me).wait_recv()
        for j, chip in enumerate(chips):
            for a in range(n):
                copy(a, 1 + j, me, (*chip, c), src=ins[a]).wait_send()
                copy(a, 4 + j, (*chip, c), sibling).wait_send()
        for a in range(n):
            copy(a, 0, me, sibling, src=ins[a]).wait_send()
            own(a).wait()

    return start, relay, wait


def _side_out_shapes(kind, arrays):
    if kind in ("gather", "relay_gather"):
        return [jax.ShapeDtypeStruct((N_DEV,) + a.shape, a.dtype) for a in arrays]
    return [jax.ShapeDtypeStruct(a.shape, a.dtype) for a in arrays]


def _hosted_call(body, name, grid, in_specs, out_specs, out_shape, scratch_shapes, args, side=None):
    if side is None:
        outs = pl.pallas_call(
            body, name=name, grid=grid, in_specs=in_specs, out_specs=out_specs, out_shape=out_shape,
            scratch_shapes=scratch_shapes, compiler_params=_params(len(grid)))(*args)
        return outs, []
    kind, arrays = side
    side_shapes = _side_out_shapes(kind, arrays)
    n_in, n_out, n_scr, n_side = len(in_specs), len(out_specs), len(scratch_shapes), len(arrays)

    def hosted(*refs):
        pos = 0
        groups = []
        for size in (n_in, n_side, n_out, len(side_shapes), n_scr):
            groups.append(refs[pos:pos + size])
            pos += size
        ins, side_in, outs, side_out, scr = groups
        send_sems, recv_sems, local_sems = refs[pos:]
        ids = [pl.program_id(d) for d in range(len(grid))]
        is_first = functools.reduce(jnp.logical_and, [i == 0 for i in ids])
        is_last = functools.reduce(jnp.logical_and, [i == g - 1 for i, g in zip(ids, grid)])
        if kind == "relay_gather":
            start, relay, wait = _relay_gather(side_in, side_out, send_sems, recv_sems, local_sems)
        else:
            start, relay, wait = _side_copies(kind, side_in, side_out, send_sems, recv_sems, local_sems)
        pl.when(is_first)(start)
        if relay is not None:
            pl.when(is_last)(relay)
        body(*ins, *outs, *scr)
        pl.when(is_last)(wait)

    any_spec = pl.BlockSpec(memory_space=pl.ANY)
    outs = pl.pallas_call(
        hosted, name=name, grid=grid,
        in_specs=list(in_specs) + [any_spec] * n_side,
        out_specs=list(out_specs) + [any_spec] * len(side_shapes),
        out_shape=list(out_shape) + side_shapes,
        scratch_shapes=list(scratch_shapes) + [pltpu.SemaphoreType.DMA((n_side, 7)), pltpu.SemaphoreType.DMA((n_side, 7)),
                                               pltpu.SemaphoreType.DMA((n_side, N_CHIPS))],
        compiler_params=_params(len(grid)))(*args, *arrays)
    return outs[:n_out], outs[n_out:]


def _lane_chunks(width, chunk=2 * LANES):
    return [slice(n0, min(n0 + chunk, width)) for n0 in range(0, width, chunk)]


def _pipelined(chunks, first, middle, last):
    n = len(chunks)
    a, b, total = {}, {}, None
    for step in range(n + 2):
        if step < n:
            a[step] = first(chunks[step])
        if 0 <= step - 1 < n:
            b[step - 1] = middle(chunks[step - 1], a.pop(step - 1))
        if 0 <= step - 2 < n:
            part = last(chunks[step - 2], b.pop(step - 2))
            total = part if total is None else total + part
    return total


def _ffn_fwd(h, g_pre, g_post, w_gu, w_down, name, side=None, attn=None):
    T, D = h.shape
    nj = w_gu.shape[0] // 2
    FB = w_gu.shape[1]
    tm = TOKEN_TILE
    n_attn = 0 if attn is None else 5

    def body(*refs):
        h_ref, gpre_ref, gpost_ref, wg_ref, wu_ref, wd_ref = refs[:6]
        hout_ref, f_ref, a_ref, gu_ref = refs[6 + n_attn:10 + n_attn]
        a_scr, acc = refs[-2:]
        j = pl.program_id(1)

        @pl.when(j == 0)
        def _():
            if attn is None:
                x = h_ref[...]
            else:
                ma_ref, mb_ref, wo_ref, bo_ref, ga_ref = refs[6:11]
                att_ref, hmid_ref, mix_ref = refs[15:18]
                mix = jnp.concatenate([ma_ref[...], mb_ref[...]], axis=1).astype(BF16)
                mix_ref[...] = mix
                att = _mm(mix, wo_ref[...]) + bo_ref[...]
                att_ref[...] = att
                x = h_ref[...] + att * _rstd(att) * ga_ref[...]
                hmid_ref[...] = x
            a = (x * _rstd(x) * gpre_ref[...]).astype(BF16)
            a_scr[...] = a
            a_ref[...] = a
            acc[...] = jnp.zeros_like(acc)

        a = a_scr[...]
        g = _mm_nt(a, wg_ref[...])
        u = _mm_nt(a, wu_ref[...])
        gu_ref[0] = g.astype(BF16)
        gu_ref[1] = u.astype(BF16)
        hh = (g * jax.nn.sigmoid(g) * u).astype(BF16)
        acc[...] += _mm(hh, wd_ref[...])

        @pl.when(j == nj - 1)
        def _():
            f = acc[...]
            f_ref[...] = f
            x = h_ref[...] if attn is None else refs[16][...]
            hout_ref[...] = x + 0.5 * (f * _rstd(f) * gpost_ref[...])

    tile = pl.BlockSpec((tm, D), lambda i, j: (i, 0))
    in_specs = [tile, _full((1, D)), _full((1, D)),
                pl.BlockSpec((None, FB, D), lambda i, j: (j, 0, 0)),
                pl.BlockSpec((None, FB, D), lambda i, j: (j + nj, 0, 0)),
                pl.BlockSpec((FB, D), lambda i, j: (j, 0))]
    out_specs = [tile, tile, tile, pl.BlockSpec((None, 2, tm, FB), lambda i, j: (j, 0, i, 0))]
    out_shape = [
        jax.ShapeDtypeStruct((T, D), F32),
        jax.ShapeDtypeStruct((T, D), F32),
        jax.ShapeDtypeStruct((T, D), BF16),
        jax.ShapeDtypeStruct((nj, 2, T, FB), BF16),
    ]
    args = [h, g_pre, g_post, w_gu, w_gu, w_down]
    if attn is not None:
        mix_a, mix_b, w_out, b_out, g_attn = attn
        d_mix = w_out.shape[0]
        in_specs += [pl.BlockSpec((tm, mix_a.shape[1]), lambda i, j: (i, 0)),
                     pl.BlockSpec((tm, mix_b.shape[1]), lambda i, j: (i, 0)),
                     _full((d_mix, D)), _full((1, D)), _full((1, D))]
        out_specs += [tile, tile, pl.BlockSpec((tm, d_mix), lambda i, j: (i, 0))]
        out_shape += [jax.ShapeDtypeStruct((T, D), F32),
                      jax.ShapeDtypeStruct((T, D), F32),
                      jax.ShapeDtypeStruct((T, d_mix), BF16)]
        args += [mix_a, mix_b, w_out, b_out, g_attn]
    return _hosted_call(
        body, name, (T // tm, nj), in_specs=in_specs, out_specs=out_specs, out_shape=out_shape,
        scratch_shapes=[pltpu.VMEM((tm, D), BF16), pltpu.VMEM((tm, D), F32)], args=args, side=side)


def _ffn_bwd(dh_out, f, g_post, h, g_pre, gu, w_gu, w_down, name, side=None):
    T, D = h.shape
    nj = w_gu.shape[0] // 2
    FB = w_gu.shape[1]
    tm = TOKEN_TILE

    def body(dho_ref, f_ref, gpost_ref, h_ref, gpre_ref, gu_ref, wg_ref, wu_ref, wd_ref,
             dhin_ref, df_ref, hh_ref, dgu_ref, dgpost_ref, dgpre_ref, df_scr, da):
        i, j = pl.program_id(0), pl.program_id(1)

        @pl.when(jnp.logical_and(i == 0, j == 0))
        def _():
            dgpost_ref[...] = jnp.zeros_like(dgpost_ref)
            dgpre_ref[...] = jnp.zeros_like(dgpre_ref)

        @pl.when(j == 0)
        def _():
            fv = f_ref[...]
            df, dgain = _rms_bwd(fv, _rstd(fv), gpost_ref[...], 0.5 * dho_ref[...])
            dgpost_ref[...] += _colsum(dgain)
            dfb = df.astype(BF16)
            df_scr[...] = dfb
            df_ref[...] = dfb
            da[...] = jnp.zeros_like(da)

        dfb = df_scr[...]

        halves = (slice(0, tm // 2), slice(tm // 2, tm))

        def hidden_grad(c):
            return [_mm_nt(dfb[rows], wd_ref[c, :]) for rows in halves]

        def through_swiglu(c, dhh):
            dhh = jnp.concatenate(dhh, axis=0)
            g = gu_ref[0, :, c].astype(F32)
            u = gu_ref[1, :, c].astype(F32)
            sg = jax.nn.sigmoid(g)
            silu = g * sg
            hh_ref[:, c] = (silu * u).astype(BF16)
            dg = (dhh * u * (sg * (1.0 + (g - silu)))).astype(BF16)
            du = (dhh * silu).astype(BF16)
            dgu_ref[0, :, c] = dg
            dgu_ref[1, :, c] = du
            return dg, du

        def input_grad(c, dgu):
            return jnp.concatenate(
                [_mm(dgu[0][rows], wg_ref[c, :]) + _mm(dgu[1][rows], wu_ref[c, :]) for rows in halves], axis=0)

        da[...] += _pipelined(_lane_chunks(FB), hidden_grad, through_swiglu, input_grad)

        @pl.when(j == nj - 1)
        def _():
            x = h_ref[...]
            dx, dgain = _rms_bwd(x, _rstd(x), gpre_ref[...], da[...])
            dgpre_ref[...] += _colsum(dgain)
            dhin_ref[...] = dho_ref[...] + dx

    tile = pl.BlockSpec((tm, D), lambda i, j: (i, 0))
    return _hosted_call(
        body, name, (T // tm, nj),
        in_specs=[
            tile, tile, _full((1, D)), tile, _full((1, D)),
            pl.BlockSpec((None, 2, tm, FB), lambda i, j: (j, 0, i, 0)),
            pl.BlockSpec((None, FB, D), lambda i, j: (j, 0, 0)),
            pl.BlockSpec((None, FB, D), lambda i, j: (j + nj, 0, 0)),
            pl.BlockSpec((FB, D), lambda i, j: (j, 0)),
        ],
        out_specs=[
            tile, tile,
            pl.BlockSpec((None, tm, FB), lambda i, j: (j, i, 0)),
            pl.BlockSpec((None, 2, tm, FB), lambda i, j: (j, 0, i, 0)),
            _full((1, D)), _full((1, D)),
        ],
        out_shape=[
            jax.ShapeDtypeStruct((T, D), F32),
            jax.ShapeDtypeStruct((T, D), BF16),
            jax.ShapeDtypeStruct((nj, T, FB), BF16),
            jax.ShapeDtypeStruct((nj, 2, T, FB), BF16),
            jax.ShapeDtypeStruct((1, D), F32),
            jax.ShapeDtypeStruct((1, D), F32),
        ],
        scratch_shapes=[pltpu.VMEM((tm, D), BF16), pltpu.VMEM((tm, D), F32)],
        args=(dh_out, f, g_post, h, g_pre, gu, w_gu, w_gu, w_down), side=side)


def _tn_matmul(x, y, x_spec, y_spec, out_shape, out_spec, n_blocks, n_steps, acc_shape, name, side=None):
    def body(x_ref, y_ref, o_ref, acc):
        t = pl.program_id(1)

        @pl.when(t == 0)
        def _():
            acc[...] = jnp.zeros_like(acc)

        acc[...] += _mm_tn(x_ref[...].astype(BF16), y_ref[...].astype(BF16))

        @pl.when(t == n_steps - 1)
        def _():
            o_ref[...] = acc[...].astype(o_ref.dtype)

    outs, side_outs = _hosted_call(
        body, name, (n_blocks, n_steps), in_specs=[x_spec, y_spec], out_specs=[out_spec], out_shape=[out_shape],
        scratch_shapes=[pltpu.VMEM(acc_shape, F32)], args=(x, y), side=side)
    return (outs[0], side_outs) if side is not None else outs[0]


def _inproj_fwd(h, g_pre, w_in, b_in, side=None):
    T, D = h.shape
    tm = TOKEN_TILE

    def body(h_ref, g_ref, w_ref, b_ref, z_ref, a_ref):
        x = h_ref[...]
        a = (x * _rstd(x) * g_ref[...]).astype(BF16)
        a_ref[...] = a
        z_ref[...] = _mm_nt(a, w_ref[...]) + b_ref[...]

    return _hosted_call(
        body, "inproj_fwd", (T // tm,),
        in_specs=[pl.BlockSpec((tm, D), lambda i: (i, 0)), _full((1, D)), _full((D_IN, D)), _full((1, D_IN))],
        out_specs=[pl.BlockSpec((tm, D_IN), lambda i: (i, 0)), pl.BlockSpec((tm, D), lambda i: (i, 0))],
        out_shape=[jax.ShapeDtypeStruct((T, D_IN), F32), jax.ShapeDtypeStruct((T, D), BF16)],
        scratch_shapes=[], args=(h, g_pre, w_in, b_in), side=side)


def _inproj_bwd(dqa, dka, dva, dqb, dkb, dvb, w_in, h, g_pre, dres, side=None):
    T, D = h.shape
    tm = TOKEN_TILE

    def body(dqa_ref, dka_ref, dva_ref, dqb_ref, dkb_ref, dvb_ref, w_ref, h_ref, g_ref, dres_ref,
             dh_ref, dz_ref, dbin_ref, dg_ref):
        i = pl.program_id(0)

        @pl.when(i == 0)
        def _():
            dbin_ref[...] = jnp.zeros_like(dbin_ref)
            dg_ref[...] = jnp.zeros_like(dg_ref)

        dz = jnp.concatenate([dqa_ref[...], dka_ref[...], dva_ref[...], dqb_ref[...], dkb_ref[...], dvb_ref[...]],
                             axis=1)
        dbin_ref[...] += _colsum(dz)
        dzb = dz.astype(BF16)
        dz_ref[...] = dzb
        da = _mm(dzb, w_ref[...])
        x = h_ref[...]
        dx, dgain = _rms_bwd(x, _rstd(x), g_ref[...], da)
        dg_ref[...] += _colsum(dgain)
        dh_ref[...] = dres_ref[...] + dx

    def tile(w):
        return pl.BlockSpec((tm, w), lambda i: (i, 0))

    return _hosted_call(
        body, "inproj_bwd", (T // tm,),
        in_specs=[tile(A_Q), tile(A_KV), tile(A_KV), tile(B_W), tile(B_W), tile(B_W),
                  _full((D_IN, D)), tile(D), _full((1, D)), tile(D)],
        out_specs=[tile(D), tile(D_IN), _full((1, D_IN)), _full((1, D))],
        out_shape=[jax.ShapeDtypeStruct((T, D), F32), jax.ShapeDtypeStruct((T, D_IN), BF16),
                   jax.ShapeDtypeStruct((1, D_IN), F32), jax.ShapeDtypeStruct((1, D), F32)],
        scratch_shapes=[], args=(dqa, dka, dva, dqb, dkb, dvb, w_in, h, g_pre, dres), side=side)


def _bucket_tiles(patterns):
    i = np.arange(QBLK)[:, None]
    j = np.arange(2 * QBLK)[None, :]
    dist = QBLK + i - j
    max_exact = NUM_BUCKETS // 2
    tiles = []
    for dilation, max_dist in patterns:
        n = np.maximum(dist * dilation, 0)
        nf = np.maximum(n, 1).astype(np.float32)
        large = max_exact + (np.log(nf / np.float32(max_exact)) / np.float32(math.log(MAX_DISTANCE / max_exact))
                             * np.float32(NUM_BUCKETS - max_exact)).astype(np.int32)
        bucket = np.where(n < max_exact, n, np.minimum(large, NUM_BUCKETS - 1))
        tiles.append(np.where((dist >= 0) & (dist <= max_dist), bucket, -1))
    return jnp.asarray(np.stack(tiles).astype(np.int32))


def _bias_build(rel_bias, buckets, head0, name, side=None):
    n = buckets.shape[0]

    def body(bk_ref, rb_ref, o_ref):
        bk = bk_ref[...]
        base = jnp.where(bk < 0, NEG_INF, 0.0).astype(F32)
        for hd in range(N_HEAD_GROUP):
            o_ref[hd] = lax.fori_loop(
                0, NUM_BUCKETS, lambda b, acc, hd=hd: jnp.where(bk == b, rb_ref[b, head0 + hd], acc), base)

    outs, side_outs = _hosted_call(
        body, name, (n,),
        in_specs=[pl.BlockSpec((None, QBLK, 2 * QBLK), lambda p: (p, 0, 0)), pl.BlockSpec(memory_space=pltpu.SMEM)],
        out_specs=[pl.BlockSpec((None, N_HEAD_GROUP, QBLK, 2 * QBLK), lambda p: (p, 0, 0, 0))],
        out_shape=[jax.ShapeDtypeStruct((n, N_HEAD_GROUP, QBLK, 2 * QBLK), F32)],
        scratch_shapes=[], args=(buckets, rel_bias), side=side)
    return outs[0], side_outs


def _bias_grad(ds, buckets, name):
    n = buckets.shape[0]

    def body(ds_ref, bk_ref, o_ref):
        bk = bk_ref[...]
        row = lax.broadcasted_iota(jnp.int32, (NUM_BUCKETS, 2 * QBLK), 0)
        for hd in range(N_HEAD_GROUP):
            d = ds_ref[hd]
            per_key = jnp.zeros((NUM_BUCKETS, 2 * QBLK), F32)
            for b in range(NUM_BUCKETS):
                per_key = jnp.where(row == b, jnp.sum(jnp.where(bk == b, d, 0.0), axis=0, keepdims=True), per_key)
            o_ref[hd] = jnp.broadcast_to(jnp.sum(per_key, axis=1, keepdims=True), (NUM_BUCKETS, LANES))

    out = pl.pallas_call(
        body, name=name, grid=(n,),
        in_specs=[pl.BlockSpec((None, N_HEAD_GROUP, QBLK, 2 * QBLK), lambda p: (p, 0, 0, 0)),
                  pl.BlockSpec((None, QBLK, 2 * QBLK), lambda p: (p, 0, 0))],
        out_specs=pl.BlockSpec((None, N_HEAD_GROUP, NUM_BUCKETS, LANES), lambda p: (p, 0, 0, 0)),
        out_shape=jax.ShapeDtypeStruct((n, N_HEAD_GROUP, NUM_BUCKETS, LANES), F32),
        compiler_params=_params(1),
    )(ds, buckets)
    return out[:, :, :, 0].reshape(n * N_HEAD_GROUP, NUM_BUCKETS)


def _class_rows(start, dilation):
    if dilation == 1:
        return pl.ds(pl.multiple_of(start, QBLK), QBLK)
    return pl.ds(start, QBLK, stride=dilation)


def _starts_class(u, blocks_per_pass, n_blocks):
    return blocks_per_pass % n_blocks == 0 and u % n_blocks == 0


def _block_starts(idx, n_blocks, dilation):
    cls = idx // n_blocks
    n = idx % n_blocks
    cur = cls + dilation * QBLK * n
    prev = cls + dilation * QBLK * jnp.maximum(n - 1, 0)
    return n, cur, prev


class _HeadPair:
    def __init__(self, g, shared_kv):
        self.lane = lax.broadcasted_iota(jnp.int32, (1, LANES), 1)
        self.lower = self.lane < HEAD_DIM
        self.shared_kv = shared_kv
        self.key_lanes = (self.lane >= HEAD_DIM).astype(jnp.int32) == (g // 2)

    def stack(self, t):
        return jnp.concatenate([jnp.where(self.lower, t, 0.0), jnp.where(self.lower, 0.0, t)], axis=0).astype(BF16)

    def unstack(self, t2):
        return jnp.where(self.lower, t2[:QBLK], t2[QBLK:])

    def keys(self, t):
        if self.shared_kv:
            return jnp.where(self.key_lanes, t, pltpu.roll(t, HEAD_DIM, 1))
        return t

    def key_grads(self, t):
        if self.shared_kv:
            return jnp.where(self.key_lanes, t + pltpu.roll(t, HEAD_DIM, 1), 0.0)
        return t


def _attn_specs(T, qcol, kcol, vcol, shared_kv):
    kv = (lambda c: (lambda g: (0, c))) if shared_kv else (lambda c: (lambda g: (0, c + g)))
    return [pl.BlockSpec((T, LANES), lambda g: (0, qcol + g)),
            pl.BlockSpec((T, LANES), kv(kcol)),
            pl.BlockSpec((T, LANES), kv(vcol))]


def _attn_fwd(z, bias, sinks, patterns, qcol, kcol, vcol, shared_kv, name, side=None):
    T = z.shape[0]
    n_pat = len(patterns)
    has_sink = sinks is not None

    def body(*refs):
        if has_sink:
            sink_ref, refs = refs[0], refs[1:]
        q_ref, k_ref, v_ref, b_ref, o_ref, l_ref = refs[:6]
        po_scr = refs[6:6 + n_pat]
        pl_scr = refs[6 + n_pat:]
        g = pl.program_id(0)
        heads = _HeadPair(g, shared_kv)
        in_prev = lax.broadcasted_iota(jnp.int32, (2 * QBLK, 2 * QBLK), 1) < QBLK

        for pi, (dilation, _) in enumerate(patterns):
            n_blocks = T // (QBLK * dilation)

            def step(it, carry, pi=pi, dilation=dilation, n_blocks=n_blocks):
                blocks = []
                for u in range(FWD_BLOCKS):
                    n, cur, prev = _block_starts(it * FWD_BLOCKS + u, n_blocks, dilation)
                    rows_c, rows_p = _class_rows(cur, dilation), _class_rows(prev, dilation)
                    qm = heads.stack(q_ref[rows_c, :])
                    k_cur, v_cur = k_ref[rows_c, :], v_ref[rows_c, :]
                    no_past = _starts_class(u, FWD_BLOCKS, n_blocks)
                    if no_past:
                        k2, v2 = k_cur, v_cur
                    else:
                        if u % min(FWD_BLOCKS, n_blocks) == 0:
                            k_prev, v_prev = k_ref[rows_p, :], v_ref[rows_p, :]
                        k2 = jnp.concatenate([k_prev, k_cur], axis=0)
                        v2 = jnp.concatenate([v_prev, v_cur], axis=0)
                    k2, v2 = heads.keys(k2).astype(BF16), heads.keys(v2).astype(BF16)
                    k_prev, v_prev = k_cur, v_cur
                    blocks.append(dict(n=n, no_past=no_past, rows=rows_c, v2=v2, s=_mm_nt(qm, k2)))
                for b in blocks:
                    if b["no_past"]:
                        b["s"] = b["s"] * (HEAD_DIM ** -0.5) + b_ref[pi, :, QBLK:]
                    else:
                        s = b["s"] * (HEAD_DIM ** -0.5) + b_ref[pi]
                        b["s"] = jnp.where(jnp.logical_and(in_prev, b["n"] == 0), NEG_INF, s)
                    b["m"] = jnp.max(b["s"], axis=1, keepdims=True)
                for b in blocks:
                    b["pr"] = jnp.exp(b["s"] - b["m"])
                    b["den"] = jnp.sum(b["pr"], axis=1, keepdims=True)
                for b in blocks:
                    b["o2"] = _mm(b["pr"].astype(BF16), b["v2"])
                for b in blocks:
                    lse = b["m"] + jnp.log(b["den"])
                    po_scr[pi][b["rows"], :] = heads.unstack(b["o2"] / b["den"])
                    pl_scr[2 * pi][b["rows"], :] = jnp.broadcast_to(lse[:QBLK], (QBLK, LANES))
                    pl_scr[2 * pi + 1][b["rows"], :] = jnp.broadcast_to(lse[QBLK:], (QBLK, LANES))
                return carry

            lax.fori_loop(0, (dilation * n_blocks) // FWD_BLOCKS, step, 0)

        def merge(ci, carry):
            rows = pl.ds(pl.multiple_of(ci * QBLK, QBLK), QBLK)
            weights = []
            for hd in range(2):
                parts = [pl_scr[2 * pi + hd][rows, :] for pi in range(n_pat)]
                m = functools.reduce(jnp.maximum, parts)
                if has_sink:
                    sink = sink_ref[0, 2 * g + hd]
                    m = jnp.maximum(m, sink)
                terms = [jnp.exp(x - m) for x in parts]
                den = functools.reduce(jnp.add, terms)
                if has_sink:
                    den = den + jnp.exp(sink - m)
                l_ref[hd, rows, :] = m + jnp.log(den)
                inv = 1.0 / den
                weights.append([t * inv for t in terms])
            o_ref[rows, :] = functools.reduce(
                jnp.add, [jnp.where(heads.lower, weights[0][pi], weights[1][pi]) * po_scr[pi][rows, :]
                          for pi in range(n_pat)])
            return carry

        lax.fori_loop(0, T // QBLK, merge, 0)

    in_specs = _attn_specs(T, qcol, kcol, vcol, shared_kv)
    in_specs.append(pl.BlockSpec((n_pat, None, 2 * QBLK, 2 * QBLK), lambda g: (0, g, 0, 0)))
    args = [z, z, z, bias.reshape(n_pat, N_HEAD_GROUP // 2, 2 * QBLK, 2 * QBLK)]
    if has_sink:
        in_specs.insert(0, pl.BlockSpec(memory_space=pltpu.SMEM))
        args.insert(0, sinks)
    return _hosted_call(
        body, name, (N_HEAD_GROUP // 2,),
        in_specs=in_specs,
        out_specs=[pl.BlockSpec((T, LANES), lambda g: (0, g)), pl.BlockSpec((2, T, LANES), lambda g: (g, 0, 0))],
        out_shape=[jax.ShapeDtypeStruct((T, N_HEAD_GROUP * HEAD_DIM), F32),
                   jax.ShapeDtypeStruct((N_HEAD_GROUP, T, LANES), F32)],
        scratch_shapes=[pltpu.VMEM((T, LANES), F32)] * (3 * n_pat), args=args, side=side)


def _attn_bwd(z, bias, sinks, d_out, out, lse, patterns, qcol, kcol, vcol, shared_kv, name, side=None):
    T = z.shape[0]
    n_pat = len(patterns)
    has_sink = sinks is not None
    kv_width = LANES if shared_kv else N_HEAD_GROUP * HEAD_DIM

    def body(*refs):
        if has_sink:
            sink_ref, refs = refs[0], refs[1:]
        q_ref, k_ref, v_ref, b_ref, do_ref, o_ref, l0_ref, l1_ref = refs[:8]
        dq_ref, dk_ref, dv_ref, ds_ref = refs[8:12]
        dsink_ref = refs[12] if has_sink else None
        dk_acc, dv_acc = refs[-2:]
        g = pl.program_id(0)
        heads = _HeadPair(g, shared_kv)
        in_prev = lax.broadcasted_iota(jnp.int32, (2 * QBLK, 2 * QBLK), 1) < QBLK

        dq_ref[...] = jnp.zeros_like(dq_ref)
        ds_ref[...] = jnp.zeros_like(ds_ref)
        dk_acc[...] = jnp.zeros_like(dk_acc)
        dv_acc[...] = jnp.zeros_like(dv_acc)

        dsink = jnp.zeros((1, LANES), F32)
        for pi, (dilation, _) in enumerate(patterns):
            n_blocks = T // (QBLK * dilation)

            def step(idx, dsink, pi=pi, dilation=dilation, n_blocks=n_blocks):
                blocks = []
                for u in range(BWD_BLOCKS):
                    n, cur, prev = _block_starts(idx * BWD_BLOCKS + u, n_blocks, dilation)
                    rows_c, rows_p = _class_rows(cur, dilation), _class_rows(prev, dilation)
                    qm = heads.stack(q_ref[rows_c, :])
                    k_cur, v_cur = k_ref[rows_c, :], v_ref[rows_c, :]
                    first = u % min(BWD_BLOCKS, n_blocks) == 0
                    no_past = _starts_class(u, BWD_BLOCKS, n_blocks)
                    if no_past:
                        k2, v2 = k_cur, v_cur
                    else:
                        if first:
                            k_prev, v_prev = k_ref[rows_p, :], v_ref[rows_p, :]
                        k2 = jnp.concatenate([k_prev, k_cur], axis=0)
                        v2 = jnp.concatenate([v_prev, v_cur], axis=0)
                    k2, v2 = heads.keys(k2).astype(BF16), heads.keys(v2).astype(BF16)
                    k_prev, v_prev = k_cur, v_cur
                    d_o = do_ref[rows_c, :]
                    dom = heads.stack(d_o)
                    dd = d_o * o_ref[rows_c, :]
                    delta = jnp.concatenate([jnp.sum(jnp.where(heads.lower, dd, 0.0), axis=1, keepdims=True),
                                             jnp.sum(jnp.where(heads.lower, 0.0, dd), axis=1, keepdims=True)], axis=0)
                    lse = jnp.concatenate([l0_ref[rows_c, :], l1_ref[rows_c, :]], axis=0)
                    blocks.append(dict(n=n, first=first, no_past=no_past, rows_c=rows_c, rows_p=rows_p, qm=qm, k2=k2,
                                       dom=dom, delta=delta, lse=lse, s=_mm_nt(qm, k2), dp=_mm_nt(dom, v2)))
                for b in blocks:
                    if b["no_past"]:
                        s = b["s"] * (HEAD_DIM ** -0.5) + b_ref[pi, :, QBLK:]
                        b["pr"] = jnp.exp(s - b["lse"])
                    else:
                        s = b["s"] * (HEAD_DIM ** -0.5) + b_ref[pi]
                        s = jnp.where(jnp.logical_and(in_prev, b["n"] == 0), NEG_INF, s)
                        b["pr"] = jnp.exp(s - jnp.concatenate([b["lse"], b["lse"]], axis=1))
                    b["ds"] = b["pr"] * (b["dp"] - b["delta"])
                for b in blocks:
                    dsb = b["ds"].astype(BF16)
                    b["dq2"] = _mm(dsb, b["k2"])
                    b["dk2"] = _mm_tn(dsb, b["qm"])
                    b["dv2"] = _mm_tn(b["pr"].astype(BF16), b["dom"])
                for b in blocks:
                    b["dk2"] = heads.key_grads(b["dk2"]) * (HEAD_DIM ** -0.5)
                    b["dv2"] = heads.key_grads(b["dv2"])
                for u, b in enumerate(blocks):
                    dq_ref[b["rows_c"], :] += heads.unstack(b["dq2"]) * (HEAD_DIM ** -0.5)
                    if b["no_past"]:
                        ds_ref[pi, :, QBLK:] += b["ds"]
                        dk_own, dv_own = b["dk2"], b["dv2"]
                    else:
                        ds_ref[pi] += b["ds"]
                        dk_own, dv_own = b["dk2"][QBLK:], b["dv2"][QBLK:]
                    if u + 1 < len(blocks) and not blocks[u + 1]["first"]:
                        dk_own = dk_own + blocks[u + 1]["dk2"][:QBLK]
                        dv_own = dv_own + blocks[u + 1]["dv2"][:QBLK]
                    if b["first"] and not b["no_past"]:
                        dk_acc[b["rows_p"], :] += b["dk2"][:QBLK]
                        dv_acc[b["rows_p"], :] += b["dv2"][:QBLK]
                    dk_acc[b["rows_c"], :] += dk_own
                    dv_acc[b["rows_c"], :] += dv_own
                    if has_sink:
                        for hd in range(2):
                            rows_h = slice(QBLK * hd, QBLK * (hd + 1))
                            p_sink = jnp.exp(sink_ref[0, 2 * g + hd] - b["lse"][rows_h, 0:1])
                            dsink = dsink - jnp.where(heads.lane == 2 * g + hd,
                                                      jnp.sum(p_sink * b["delta"][rows_h]), 0.0)
                return dsink

            dsink = lax.fori_loop(0, (dilation * n_blocks) // BWD_BLOCKS, step, dsink)

        if shared_kv:
            @pl.when(g == 0)
            def _():
                dk_ref[...] = dk_acc[...]
                dv_ref[...] = dv_acc[...]

            @pl.when(g != 0)
            def _():
                dk_ref[...] += dk_acc[...]
                dv_ref[...] += dv_acc[...]
        else:
            dk_ref[...] = dk_acc[...]
            dv_ref[...] = dv_acc[...]

        if has_sink:
            @pl.when(g == 0)
            def _():
                dsink_ref[...] = dsink

            @pl.when(g != 0)
            def _():
                dsink_ref[...] += dsink

    pair = pl.BlockSpec((T, LANES), lambda g: (0, g))
    stacked = pl.BlockSpec((n_pat, None, 2 * QBLK, 2 * QBLK), lambda g: (0, g, 0, 0))
    stacked_shape = (n_pat, N_HEAD_GROUP // 2, 2 * QBLK, 2 * QBLK)
    in_specs = _attn_specs(T, qcol, kcol, vcol, shared_kv)
    in_specs += [stacked, pair, pair,
                 pl.BlockSpec((None, T, LANES), lambda g: (2 * g, 0, 0)),
                 pl.BlockSpec((None, T, LANES), lambda g: (2 * g + 1, 0, 0))]
    args = [z, z, z, bias.reshape(stacked_shape), d_out, out, lse, lse]
    kv_out = _full((T, LANES)) if shared_kv else pair
    out_specs = [pair, kv_out, kv_out, stacked]
    out_shape = [jax.ShapeDtypeStruct((T, N_HEAD_GROUP * HEAD_DIM), F32),
                 jax.ShapeDtypeStruct((T, kv_width), F32), jax.ShapeDtypeStruct((T, kv_width), F32),
                 jax.ShapeDtypeStruct(stacked_shape, F32)]
    if has_sink:
        in_specs.insert(0, pl.BlockSpec(memory_space=pltpu.SMEM))
        args.insert(0, sinks)
        out_specs.append(_full((1, LANES)))
        out_shape.append(jax.ShapeDtypeStruct((1, LANES), F32))
    outs, side_outs = _hosted_call(
        body, name, (N_HEAD_GROUP // 2,), in_specs=in_specs, out_specs=out_specs, out_shape=out_shape,
        scratch_shapes=[pltpu.VMEM((T, LANES), F32), pltpu.VMEM((T, LANES), F32)], args=args, side=side)
    outs = list(outs)
    outs[3] = outs[3].reshape(n_pat, N_HEAD_GROUP, QBLK, 2 * QBLK)
    return outs, side_outs


def _outproj_bwd(dh, att, g_post, w_out):
    T, D = dh.shape
    tm = TOKEN_TILE
    d_mix = w_out.shape[0]

    def body(dh_ref, att_ref, g_ref, w_ref, dma_ref, dmb_ref, datt_ref, dg_ref, db_ref):
        i = pl.program_id(0)

        @pl.when(i == 0)
        def _():
            dg_ref[...] = jnp.zeros_like(dg_ref)
            db_ref[...] = jnp.zeros_like(db_ref)

        att = att_ref[...]
        datt, dgain = _rms_bwd(att, _rstd(att), g_ref[...], dh_ref[...])
        dg_ref[...] += _colsum(dgain)
        db_ref[...] += _colsum(datt)
        dattb = datt.astype(BF16)
        datt_ref[...] = dattb
        dmix = _mm_nt(dattb, w_ref[...])
        dma_ref[...] = dmix[:, :A_Q]
        dmb_ref[...] = dmix[:, A_Q:]

    def tile(w):
        return pl.BlockSpec((tm, w), lambda i: (i, 0))

    return pl.pallas_call(
        body, name="outproj_bwd", grid=(T // tm,),
        in_specs=[tile(D), tile(D), _full((1, D)), _full((d_mix, D))],
        out_specs=[tile(A_Q), tile(B_W), tile(D), _full((1, D)), _full((1, D))],
        out_shape=[jax.ShapeDtypeStruct((T, A_Q), F32), jax.ShapeDtypeStruct((T, B_W), F32),
                   jax.ShapeDtypeStruct((T, D), BF16), jax.ShapeDtypeStruct((1, D), F32),
                   jax.ShapeDtypeStruct((1, D), F32)],
        compiler_params=_params(1),
    )(dh, att, g_post, w_out)


def _ple_fwd_bwd(h, g_pre, w_gate, p, w_proj, g_post, target):
    T, D = h.shape
    tm = TOKEN_TILE
    n_proj, ple, db = w_proj.shape

    def body(h_ref, gpre_ref, wg_ref, p_ref, wp_ref, gpost_ref, t_ref,
             a_ref, dpre_ref, de_ref, dh_ref, loss_ref, dgpost_ref, dgpre_ref):
        i = pl.program_id(0)

        @pl.when(i == 0)
        def _():
            loss_ref[...] = jnp.zeros_like(loss_ref)
            dgpost_ref[...] = jnp.zeros_like(dgpost_ref)
            dgpre_ref[...] = jnp.zeros_like(dgpre_ref)

        x = h_ref[...]
        rx = _rstd(x)
        a = (x * rx * gpre_ref[...]).astype(BF16)
        a_ref[...] = a
        gate = jax.nn.sigmoid(_mm(a, wg_ref[...]))
        pb = p_ref[...].astype(BF16)
        e = jnp.concatenate([_mm(pb, wp_ref[k]) for k in range(n_proj)], axis=1)
        ge = gate * e
        rg = _rstd(ge)
        diff = x + ge * rg * gpost_ref[...] - t_ref[...]
        loss_ref[...] += 0.5 * jnp.sum(jnp.mean(diff * diff, axis=1, keepdims=True))
        dy = diff * (1.0 / D)
        dge, dgain = _rms_bwd(ge, rg, gpost_ref[...], dy)
        dgpost_ref[...] += _colsum(dgain)
        de_ref[...] = (dge * gate).astype(BF16)
        dpre = (dge * e * gate * (1.0 - gate)).astype(BF16)
        dpre_ref[...] = dpre
        dx, dgain = _rms_bwd(x, rx, gpre_ref[...], _mm_nt(dpre, wg_ref[...]))
        dgpre_ref[...] += _colsum(dgain)
        dh_ref[...] = dy + dx

    def tile(w):
        return pl.BlockSpec((tm, w), lambda i: (i, 0))

    return pl.pallas_call(
        body, name="ple_fwd_bwd", grid=(T // tm,),
        in_specs=[tile(D), _full((1, D)), _full((D, D)), tile(ple), _full((n_proj, ple, db)), _full((1, D)), tile(D)],
        out_specs=[tile(D), tile(D), tile(D), tile(D), _full((1, LANES)), _full((1, D)), _full((1, D))],
        out_shape=[jax.ShapeDtypeStruct((T, D), BF16),
                   jax.ShapeDtypeStruct((T, D), BF16),
                   jax.ShapeDtypeStruct((T, D), BF16),
                   jax.ShapeDtypeStruct((T, D), F32),
                   jax.ShapeDtypeStruct((1, LANES), F32),
                   jax.ShapeDtypeStruct((1, D), F32),
                   jax.ShapeDtypeStruct((1, D), F32)],
        compiler_params=_params(1),
    )(h, g_pre, w_gate, p, w_proj, g_post, target)


def _ple_dw_proj(p, de, n_proj):
    T, ple = p.shape
    D = de.shape[1]
    db = D // n_proj
    tk = TOKEN_TILE
    nt = T // tk

    def body(p_ref, de_ref, o_ref, acc):
        t = pl.program_id(0)

        @pl.when(t == 0)
        def _():
            acc[...] = jnp.zeros_like(acc)

        acc[...] += _mm_tn(p_ref[...].astype(BF16), de_ref[...])

        @pl.when(t == nt - 1)
        def _():
            for k in range(n_proj):
                o_ref[k] = acc[:, k * db:(k + 1) * db].astype(BF16)

    return pl.pallas_call(
        body, name="ple_dw_proj", grid=(nt,),
        in_specs=[pl.BlockSpec((tk, ple), lambda t: (t, 0)), pl.BlockSpec((tk, D), lambda t: (t, 0))],
        out_specs=_full((n_proj, ple, db)), out_shape=jax.ShapeDtypeStruct((n_proj, ple, db), BF16),
        scratch_shapes=[pltpu.VMEM((ple, D), F32)], compiler_params=_params(1),
    )(p, de)


def _tok(width):
    return pl.BlockSpec((DW_TILE, width), lambda b, t: (t, 0))


def _dw_gu(a, dgu, name, side=None):
    T, D = a.shape
    nj, _, _, FB = dgu.shape
    return _tn_matmul(
        dgu, a, pl.BlockSpec((None, None, DW_TILE, FB), lambda b, t: (b % nj, b // nj, t, 0)), _tok(D),
        jax.ShapeDtypeStruct((2 * nj, FB, D), BF16), pl.BlockSpec((None, FB, D), lambda b, t: (b, 0, 0)),
        2 * nj, T // DW_TILE, (FB, D), name, side=side)


def _dw_down(hh, df, name, side=None):
    nj, T, FB = hh.shape
    D = df.shape[1]
    return _tn_matmul(
        hh, df, pl.BlockSpec((None, DW_TILE, FB), lambda b, t: (b, t, 0)), _tok(D),
        jax.ShapeDtypeStruct((nj, FB, D), BF16), pl.BlockSpec((None, FB, D), lambda b, t: (b, 0, 0)),
        nj, T // DW_TILE, (FB, D), name, side=side)


def _dw_rows(xm, y, name):
    T, k = xm.shape
    D = y.shape[1]
    out = _tn_matmul(
        xm, y, _tok(k), _tok(D), jax.ShapeDtypeStruct((k, D), BF16), _full((k, D)),
        1, T // DW_TILE, (k, D), name)
    return out.reshape(N_DEV, k // N_DEV, D)


def _cast_bf16(arrays):
    n = len(arrays)

    def body(*refs):
        for a in range(n):
            refs[n + a][...] = refs[a][...].astype(BF16)

    return pl.pallas_call(
        body, name="cast_shards",
        in_specs=[pl.BlockSpec(memory_space=pltpu.VMEM)] * n, out_specs=[pl.BlockSpec(memory_space=pltpu.VMEM)] * n,
        out_shape=[jax.ShapeDtypeStruct(a.shape, BF16) for a in arrays],
        compiler_params=pltpu.CompilerParams(vmem_limit_bytes=VMEM_LIMIT),
    )(*arrays)


def _pack_layout(D, n_rel_rows):
    n_bin = -(-D_IN // D)
    row_bin = len(GAINS)
    row_sink = row_bin + n_bin
    row_loss = row_sink + 1
    row_rb = -(-(row_loss + 1) // 8) * 8
    n_rows = row_rb + -(-n_rel_rows // 8) * 8
    bin_parts = [(r, min(D, D_IN - r * D)) for r in range(n_bin)]
    return row_bin, row_sink, row_loss, row_rb, n_rows, bin_parts


def _pair_swap_call(grad_blocks):
    def body(g_in, received, send_sems, recv_sems):
        start, _, wait = _pair_swap([g_in], [received], send_sems, recv_sems)
        start()
        wait()

    any_spec = pl.BlockSpec(memory_space=pl.ANY)
    return pl.pallas_call(
        body, name="pair_swap", in_specs=[any_spec], out_specs=any_spec,
        out_shape=jax.ShapeDtypeStruct((N_CHIPS,) + grad_blocks.shape[1:], grad_blocks.dtype),
        scratch_shapes=[pltpu.SemaphoreType.DMA((1, N_CHIPS)), pltpu.SemaphoreType.DMA((1, N_CHIPS))],
    )(grad_blocks)


def _pair_add(blocks, received, name):
    n, R, C = received.shape
    rows = _adamw_rows(R)
    core = lax.axis_index("c").astype(jnp.int32).reshape(1)

    def body(core_ref, a_ref, b_ref, o_ref):
        o_ref[...] = (a_ref[...].astype(F32) + b_ref[...].astype(F32)).astype(o_ref.dtype)

    tile = pl.BlockSpec((None, rows, C), lambda q, r, core_ref: (q, r, 0))
    return pl.pallas_call(
        body, name=name,
        grid_spec=pltpu.PrefetchScalarGridSpec(
            num_scalar_prefetch=1, grid=(n, R // rows),
            in_specs=[pl.BlockSpec((None, rows, C), lambda q, r, core_ref: (2 * q + core_ref[0], r, 0)), tile],
            out_specs=tile),
        out_shape=jax.ShapeDtypeStruct(received.shape, received.dtype), compiler_params=_params(2),
    )(core, blocks, received)


def _final_exchange(partials, loss):
    D = partials["ffn1_pre_g"].shape[1]
    rb_shape = partials["rel_bias"].shape
    row_bin, row_sink, row_loss, row_rb, n_rows, bin_parts = _pack_layout(D, rb_shape[0])
    n_small = len(SMALL)

    def body(*refs):
        part = dict(zip(SMALL, refs[:n_small]))
        loss_ref = refs[n_small]
        gath, pack, send_sems, recv_sems, local_sems = refs[1 + n_small:]

        pack[...] = jnp.zeros_like(pack)
        for i, name in enumerate(GAINS):
            pack[i:i + 1, :] = part[name][...]
        for r, width in bin_parts:
            pack[row_bin + r:row_bin + r + 1, 0:width] = part["b_in"][:, r * D:r * D + width]
        pack[row_sink:row_sink + 1, 0:LANES] = part["sinks"][...]
        pack[row_loss:row_loss + 1, 0:LANES] = loss_ref[...]
        pack[row_rb:row_rb + rb_shape[0], 0:rb_shape[1]] = part["rel_bias"][...]

        small_start, _, small_wait = _side_copies("gather", [pack], [gath], send_sems, recv_sems, local_sems, sem_row=0)
        small_start()
        small_wait()

    args = [partials[k] for k in SMALL] + [loss]
    vmem = pl.BlockSpec(memory_space=pltpu.VMEM)
    return pl.pallas_call(
        body, name="final_exchange",
        in_specs=[vmem] * (n_small + 1),
        out_specs=pl.BlockSpec(memory_space=pl.ANY),
        out_shape=jax.ShapeDtypeStruct((N_DEV, n_rows, D), F32),
        scratch_shapes=[pltpu.VMEM((n_rows, D), F32), pltpu.SemaphoreType.DMA((1, 7)),
                        pltpu.SemaphoreType.DMA((1, 7)), pltpu.SemaphoreType.DMA((1, N_CHIPS))],
    )(*args)


def _adamw(w, g, m, v):
    m = ADAM_B1 * m + (1.0 - ADAM_B1) * g
    v = ADAM_B2 * v + (1.0 - ADAM_B2) * (g * g)
    m_hat = m / (1.0 - ADAM_B1 ** ADAM_STEP)
    v_hat = v / (1.0 - ADAM_B2 ** ADAM_STEP)
    return -ADAM_LR * (m_hat / (jnp.sqrt(v_hat) + ADAM_EPS) + ADAM_WD * w), m, v


def _quad_start(pairs):
    def body(pairs_ref, land_ref, send_sems, recv_sems, pairs_thru, land_thru, token):
        x, y, c = _mesh_place()
        mine = 2 * x + y
        for k, chip in enumerate([(1 - x, y), (x, 1 - y), (1 - x, 1 - y)]):
            pltpu.make_async_remote_copy(
                src_ref=pairs_ref.at[2 * chip[0] + chip[1]], dst_ref=land_ref.at[mine],
                send_sem=send_sems.at[k], recv_sem=recv_sems.at[k],
                device_id=(chip[0], chip[1], c), device_id_type=MESH).start()
        token[...] = jnp.zeros_like(token)

    hbm = pl.BlockSpec(memory_space=pltpu.HBM)
    sem = pl.BlockSpec(memory_space=pltpu.SEMAPHORE)
    return pl.pallas_call(
        body, name="quad_start",
        out_shape=(pltpu.SemaphoreType.DMA((3,)), pltpu.SemaphoreType.DMA((3,)),
                   pltpu.HBM(pairs.shape, pairs.dtype), pltpu.HBM(pairs.shape, pairs.dtype),
                   jax.ShapeDtypeStruct((8, LANES), F32)),
        in_specs=(hbm, hbm), out_specs=(sem, sem, hbm, hbm, pl.BlockSpec(memory_space=pltpu.VMEM)),
        input_output_aliases={0: 2, 1: 3},
        compiler_params=pltpu.CompilerParams(has_side_effects=pltpu.SideEffectType.DATAFLOW_SIDE_EFFECTING),
    )(pltpu.with_memory_space_constraint(pairs, pltpu.HBM),
      pltpu.with_memory_space_constraint(jnp.copy(pairs), pltpu.HBM))


def _quad_wait(send_sems, recv_sems, pairs_thru, land_thru, after):
    def body(pairs_ref, land_ref, send_sems, recv_sems, after_ref, pairs_dead, land_out):
        x, y, c = _mesh_place()
        mine = 2 * x + y
        for k, chip in enumerate([(1 - x, y), (x, 1 - y), (1 - x, 1 - y)]):
            q = 2 * chip[0] + chip[1]
            pltpu.make_async_remote_copy(
                src_ref=pairs_ref.at[q], dst_ref=land_ref.at[mine], send_sem=send_sems.at[k],
                recv_sem=recv_sems.at[k], device_id=(chip[0], chip[1], c), device_id_type=MESH).wait_send()
            pltpu.make_async_remote_copy(
                src_ref=pairs_ref.at[q], dst_ref=land_ref.at[q], send_sem=send_sems.at[k],
                recv_sem=recv_sems.at[k], device_id=(chip[0], chip[1], c), device_id_type=MESH).wait_recv()

    hbm = pl.BlockSpec(memory_space=pltpu.HBM)
    sem = pl.BlockSpec(memory_space=pltpu.SEMAPHORE)
    return pl.pallas_call(
        body, name="quad_wait",
        out_shape=(pltpu.HBM(pairs_thru.shape, pairs_thru.dtype), pltpu.HBM(land_thru.shape, land_thru.dtype)),
        in_specs=(hbm, hbm, sem, sem, pl.BlockSpec(memory_space=pl.ANY)), out_specs=(hbm, hbm),
        input_output_aliases={0: 0, 1: 1},
        compiler_params=pltpu.CompilerParams(has_side_effects=pltpu.SideEffectType.DATAFLOW_SIDE_EFFECTING),
    )(pairs_thru, land_thru, send_sems, recv_sems, after)[1]


def _sum_adamw(partials, w, m, v, rows, name, after=None):
    R, C = w.shape
    n = partials.shape[0]

    def body(p_ref, w_ref, m_ref, v_ref, *rest):
        g_ref, d_ref, nm_ref, nv_ref = rest[-4:]
        g = p_ref[0].astype(F32)
        for k in range(1, n):
            g = g + p_ref[k].astype(F32)
        g_ref[...] = g
        d_ref[...], nm_ref[...], nv_ref[...] = _adamw(w_ref[...], g, m_ref[...], v_ref[...])

    tile = pl.BlockSpec((rows, C), lambda i: (i, 0))
    extra = [] if after is None else [after]
    return pl.pallas_call(
        body, name=name, grid=(R // rows,),
        in_specs=[pl.BlockSpec((n, rows, C), lambda i: (0, i, 0)), tile, tile, tile]
        + [pl.BlockSpec(memory_space=pl.ANY)] * len(extra),
        out_specs=[tile] * 4, out_shape=[jax.ShapeDtypeStruct((R, C), F32)] * 4,
        compiler_params=_params(1),
    )(partials, w, m, v, *extra)


def _small_adamw(gathered, ws, ms, vs):
    D = ws["ffn1_pre_g"].shape[1]
    n_sink = ws["sinks"].shape[1]
    rb_shape = ws["rel_bias"].shape
    row_bin, row_sink, row_loss, row_rb, n_rows, bin_parts = _pack_layout(D, rb_shape[0])
    n_small = len(SMALL)

    def body(*refs):
        gath = refs[0]
        pos = 1
        w_ref = dict(zip(SMALL, refs[pos:pos + n_small]))
        m_ref = dict(zip(SMALL, refs[pos + n_small:pos + 2 * n_small]))
        v_ref = dict(zip(SMALL, refs[pos + 2 * n_small:pos + 3 * n_small]))
        pos += 3 * n_small
        outs = {name: refs[pos + 4 * i:pos + 4 * i + 4] for i, name in enumerate(SMALL)}
        loss_out = refs[pos + 4 * n_small]
        pack = refs[pos + 4 * n_small + 1]

        total = gath[0]
        for k in range(1, N_DEV):
            total = total + gath[k]
        pack[...] = total

        def update(name, g):
            g_out, d_out, m_out, v_out = outs[name]
            g_out[...] = g
            d_out[...], m_out[...], v_out[...] = _adamw(w_ref[name][...], g, m_ref[name][...], v_ref[name][...])

        for i, name in enumerate(GAINS):
            update(name, pack[i:i + 1, :])
        update("b_in", jnp.concatenate([pack[row_bin + r:row_bin + r + 1, 0:width] for r, width in bin_parts], axis=1))
        update("sinks", pack[row_sink:row_sink + 1, 0:n_sink])
        update("rel_bias", pack[row_rb:row_rb + rb_shape[0], 0:rb_shape[1]])
        loss_out[...] = pack[row_loss:row_loss + 1, 0:LANES]

    args = [gathered]
    for group in (ws, ms, vs):
        args += [group[k] for k in SMALL]
    out_shape = []
    for name in SMALL:
        out_shape += [jax.ShapeDtypeStruct(ws[name].shape, F32)] * 4
    out_shape.append(jax.ShapeDtypeStruct((1, LANES), F32))
    res = pl.pallas_call(
        body, name="small_adamw",
        in_specs=[pl.BlockSpec(memory_space=pltpu.VMEM)] * len(args),
        out_specs=[pl.BlockSpec(memory_space=pltpu.VMEM)] * len(out_shape),
        out_shape=out_shape,
        scratch_shapes=[pltpu.VMEM((n_rows, D), F32)],
    )(*args)
    per_name = {name: res[4 * i:4 * i + 4] for i, name in enumerate(SMALL)}
    return per_name, res[-1]


COLUMN_SHARDED = ("ffn1_w_gu", "ffn2_w_gu", "w_in")


def _adamw_rows(rows_total):
    return max(r for r in range(16, min(rows_total, 256) + 1, 16) if rows_total % r == 0)


def kernel(x, p, rel_bias, ffn1_pre_g, ffn1_w_gu, ffn1_w_down, ffn1_post_g, attn_pre_g, w_in, b_in, sinks, w_out, b_out, attn_post_g, ffn2_pre_g, ffn2_w_gu, ffn2_w_down, ffn2_post_g, ple_pre_g, w_ple_gate, w_ple_proj, ple_post_g, loss_target, m_rel_bias, m_ffn1_pre_g, m_ffn1_w_gu, m_ffn1_w_down, m_ffn1_post_g, m_attn_pre_g, m_w_in, m_b_in, m_sinks, m_w_out, m_b_out, m_attn_post_g, m_ffn2_pre_g, m_ffn2_w_gu, m_ffn2_w_down, m_ffn2_post_g, m_ple_pre_g, m_w_ple_gate, m_w_ple_proj, m_ple_post_g, v_rel_bias, v_ffn1_pre_g, v_ffn1_w_gu, v_ffn1_w_down, v_ffn1_post_g, v_attn_pre_g, v_w_in, v_b_in, v_sinks, v_w_out, v_b_out, v_attn_post_g, v_ffn2_pre_g, v_ffn2_w_gu, v_ffn2_w_down, v_ffn2_post_g, v_ple_pre_g, v_w_ple_gate, v_w_ple_proj, v_ple_post_g):
    given = dict(locals())
    ws = {k: given[k] for k in WEIGHTS}
    ms = {k: given["m_" + k] for k in WEIGHTS}
    vs = {k: given["v_" + k] for k in WEIGHTS}

    def shard(t):
        return t.reshape(t.shape[1:])

    xs, ps, target = shard(x), shard(shard(p)), shard(loss_target)
    T, D = xs.shape
    small = {k: ws[k] for k in SMALL}

    def local(group, k):
        t = shard(group[k])
        return jnp.swapaxes(t, 0, 1) if k in COLUMN_SHARDED else t

    shards = {k: local(ws, k) for k in BIG}

    cast = dict(zip(BIG, _cast_bf16([shards[k] for k in BIG])))
    buckets_a = _bucket_tiles(PATTERNS_A)
    buckets_b = _bucket_tiles(PATTERNS_B)
    bias_a, _ = _bias_build(small["rel_bias"], buckets_a, 0, "bias_build_a")
    bias_b, (w_gu1, w_down1) = _bias_build(
        small["rel_bias"], buckets_b, N_HEAD_GROUP, "bias_build_b",
        side=("relay_gather", [cast["ffn1_w_gu"], cast["ffn1_w_down"]]))
    w_down1 = w_down1.reshape(-1, D)
    a_cfg = dict(patterns=PATTERNS_A, qcol=Q_A_COL, kcol=K_A_COL, vcol=V_A_COL, shared_kv=True)
    b_cfg = dict(patterns=PATTERNS_B, qcol=Q_B_COL, kcol=K_B_COL, vcol=V_B_COL, shared_kv=False)

    (h1, f1, a1, gu1), (w_in_g, w_down2) = _ffn_fwd(
        xs, small["ffn1_pre_g"], small["ffn1_post_g"], w_gu1, w_down1, "ffn1_fwd",
        side=("relay_gather", [cast["w_in"], cast["ffn2_w_down"]]))
    w_in_full = w_in_g.reshape(D_IN, D)
    w_down2 = w_down2.reshape(-1, D)
    (z, a2), (w_out_g,) = _inproj_fwd(h1, small["attn_pre_g"], w_in_full, small["b_in"],
                                      side=("relay_gather", [cast["w_out"]]))
    w_out_full = w_out_g.reshape(-1, D)
    (mix_a, lse_a), (w_gate, w_proj) = _attn_fwd(
        z, bias_a, small["sinks"], name="attn_a_fwd", **a_cfg,
        side=("relay_gather", [cast["w_ple_gate"], cast["w_ple_proj"]]))
    w_gate = w_gate.reshape(-1, D)
    (mix_b, lse_b), (w_gu2,) = _attn_fwd(
        z, bias_b, None, name="attn_b_fwd", **b_cfg, side=("relay_gather", [cast["ffn2_w_gu"]]))
    (h3, f2, a3, gu2, att, h2, mix), _ = _ffn_fwd(
        h1, small["ffn2_pre_g"], small["ffn2_post_g"], w_gu2, w_down2, "ffn2_fwd",
        attn=(mix_a, mix_b, w_out_full, small["b_out"], small["attn_post_g"]))
    a4, dpre, de, dh3, loss, dg_ple_post, dg_ple_pre = _ple_fwd_bwd(
        h3, small["ple_pre_g"], w_gate, ps, w_proj, small["ple_post_g"], target)

    d_gate = _dw_rows(a4, dpre, "ple_dw_gate")
    d_proj = _ple_dw_proj(ps, de, N_DEV)
    landed = {}
    (dh2, df2, hh2, dgu2, dg_f2_post, dg_f2_pre), (landed["w_ple_gate"], landed["w_ple_proj"]) = _ffn_bwd(
        dh3, f2, small["ffn2_post_g"], h2, small["ffn2_pre_g"], gu2, w_gu2, w_down2, "ffn2_bwd",
        side=("exchange", [d_gate, d_proj]))
    d_gu2 = _dw_gu(a3, dgu2, "ffn2_dw_gu")
    d_down2 = _dw_down(hh2, df2, "ffn2_dw_down").reshape(N_DEV, -1, D)
    dmix_a, dmix_b, datt, dg_attn_post, db_out = _outproj_bwd(dh2, att, small["attn_post_g"], w_out_full)
    d_out = _dw_rows(mix, datt, "attn_dw_out")
    (dqa, dka, dva, ds_a, dsinks), _ = _attn_bwd(
        z, bias_a, small["sinks"], dmix_a, mix_a, lse_a, name="attn_a_bwd", **a_cfg)
    (dqb, dkb, dvb, ds_b), (landed["ffn2_w_gu"],) = _attn_bwd(
        z, bias_b, None, dmix_b, mix_b, lse_b, name="attn_b_bwd", **b_cfg, side=("exchange", [d_gu2]))
    (dh1, dz, db_in, dg_attn_pre), (landed["w_out"],) = _inproj_bwd(
        dqa, dka, dva, dqb, dkb, dvb, w_in_full, h1, small["attn_pre_g"], dh2, side=("exchange", [d_out]))
    cols = D_IN // 3
    d_in = _tn_matmul(
        dz, a2, pl.BlockSpec((DW_TILE, cols), lambda b, t: (t, b)), _tok(D),
        jax.ShapeDtypeStruct((D_IN, D), BF16), pl.BlockSpec((cols, D), lambda b, t: (b, 0)),
        3, T // DW_TILE, (cols, D), "attn_dw_in").reshape(N_DEV, D_IN // N_DEV, D)
    (grad_x, df1, hh1, dgu1, dg_f1_post, dg_f1_pre), (landed["w_in"], landed["ffn2_w_down"]) = _ffn_bwd(
        dh1, f1, small["ffn1_post_g"], xs, small["ffn1_pre_g"], gu1, w_gu1, w_down1, "ffn1_bwd",
        side=("exchange", [d_in, d_down2]))
    d_down1 = _dw_down(hh1, df1, "ffn1_dw_down").reshape(N_DEV, -1, D)
    d_gu1, (landed["ffn1_w_down"],) = _dw_gu(a1, dgu1, "ffn1_dw_gu", side=("exchange", [d_down1]))

    rb_a = _bias_grad(ds_a, buckets_a, "bias_grad_a")
    rb_b = _bias_grad(ds_b, buckets_b, "bias_grad_b").reshape(len(PATTERNS_B), N_HEAD_GROUP, NUM_BUCKETS)
    d_rel_bias = jnp.concatenate([rb_a.T, jnp.sum(rb_b, axis=0).T], axis=1)
    small_grads = {"ffn1_pre_g": dg_f1_pre, "ffn1_post_g": dg_f1_post, "attn_pre_g": dg_attn_pre,
                   "attn_post_g": dg_attn_post, "ffn2_pre_g": dg_f2_pre, "ffn2_post_g": dg_f2_post,
                   "ple_pre_g": dg_ple_pre, "ple_post_g": dg_ple_post, "b_out": db_out, "b_in": db_in,
                   "sinks": dsinks, "rel_bias": d_rel_bias}
    d_gu1_pairs = _pair_add(d_gu1, _pair_swap_call(d_gu1), "ffn1_dw_gu_pair_add")
    send_sems, recv_sems, pairs_thru, land_thru, token = _quad_start(d_gu1_pairs)
    small_gathered = _final_exchange(small_grads, loss)
    updates = {}
    for k in BIG:
        if k != "ffn1_w_gu":
            updates[k] = _sum_adamw(landed[k], shards[k], local(ms, k), local(vs, k),
                                    _adamw_rows(shards[k].shape[0]), k + "_adamw", after=token)
    landed["ffn1_w_gu"] = _quad_wait(send_sems, recv_sems, pairs_thru, land_thru, updates["ffn2_w_gu"][0])
    k = "ffn1_w_gu"
    updates[k] = _sum_adamw(landed[k], shards[k], local(ms, k), local(vs, k), _adamw_rows(shards[k].shape[0]),
                            k + "_adamw")
    result = {}
    for k in BIG:
        outs = updates[k]
        if k in COLUMN_SHARDED:
            outs = [jnp.swapaxes(o, 0, 1) for o in outs]
        result[k] = [o.reshape(ws[k].shape) for o in outs]
    small_res, loss_all = _small_adamw(
        small_gathered, small, {k: ms[k] for k in SMALL}, {k: vs[k] for k in SMALL})
    result.update(small_res)

    out = [loss_all[0, 0], grad_x.reshape(x.shape)]
    for i in range(4):
        out += [result[k][i] for k in WEIGHTS]
    return tuple(out)
```

```python
import functools
import math

import numpy as np
import jax
import jax.numpy as jnp
from jax import lax
from jax.experimental import pallas as pl
from jax.experimental.pallas import tpu as pltpu

F32 = jnp.float32
BF16 = jnp.bfloat16
MESH = pl.DeviceIdType.MESH

N_DEV = 8
EPS = 1e-6
NEG_INF = -1e30
HEAD_DIM = 64
LANES = 128
QBLK = 128
D_IN = 2304
A_Q, A_KV, B_W = 512, 128, 512
N_HEAD_GROUP = 8
NUM_BUCKETS = 32
MAX_DISTANCE = 2048
PATTERNS_A = ((1, 127),)
PATTERNS_B = ((1, 128), (4, 128), (16, 128))
Q_A_COL, K_A_COL, V_A_COL = 0, 4, 5
Q_B_COL, K_B_COL, V_B_COL = 6, 10, 14

ADAM_LR, ADAM_B1, ADAM_B2, ADAM_EPS, ADAM_WD, ADAM_STEP = 0.001, 0.9, 0.999, 1e-08, 0.01, 10

TOKEN_TILE = 512
DW_TILE = 1024
FWD_BLOCKS = 4
BWD_BLOCKS = 4
VMEM_LIMIT = 56 * 1024 * 1024
ARB = "arbitrary"

BIG = ("ffn1_w_gu", "ffn1_w_down", "w_in", "w_out", "ffn2_w_gu", "ffn2_w_down", "w_ple_gate", "w_ple_proj")
GAINS = ("ffn1_pre_g", "ffn1_post_g", "attn_pre_g", "attn_post_g", "ffn2_pre_g", "ffn2_post_g",
         "ple_pre_g", "ple_post_g", "b_out")
SMALL = GAINS + ("b_in", "sinks", "rel_bias")
WEIGHTS = ("rel_bias", "ffn1_pre_g", "ffn1_w_gu", "ffn1_w_down", "ffn1_post_g", "attn_pre_g", "w_in", "b_in",
           "sinks", "w_out", "b_out", "attn_post_g", "ffn2_pre_g", "ffn2_w_gu", "ffn2_w_down", "ffn2_post_g",
           "ple_pre_g", "w_ple_gate", "w_ple_proj", "ple_post_g")


def _params(n_axes):
    return pltpu.CompilerParams(dimension_semantics=(ARB,) * n_axes, vmem_limit_bytes=VMEM_LIMIT)


def _mm(a, b):
    return jnp.dot(a, b, preferred_element_type=F32)


def _mm_nt(a, b):
    return lax.dot_general(a, b, (((1,), (1,)), ((), ())), preferred_element_type=F32)


def _mm_tn(a, b):
    return lax.dot_general(a, b, (((0,), (0,)), ((), ())), preferred_element_type=F32)


def _rstd(x):
    return lax.rsqrt(jnp.mean(x * x, axis=-1, keepdims=True) + EPS)


def _rms_bwd(x, r, gain, dy):
    n = x * r
    gdy = dy * gain
    return r * (gdy - n * jnp.mean(gdy * n, axis=-1, keepdims=True)), dy * n


def _colsum(v):
    return jnp.sum(v, axis=0, keepdims=True)


def _full(shape):
    return pl.BlockSpec(shape, lambda *_: (0,) * len(shape))


def _mesh_place():
    return lax.axis_index("x"), lax.axis_index("y"), lax.axis_index("c")


def _slot(dev):
    return 4 * dev[0] + 2 * dev[1] + dev[2]


def _peers(x, y, c):
    out = []
    for flip in range(1, N_DEV):
        dx, dy, dc = (flip >> 2) & 1, (flip >> 1) & 1, flip & 1
        out.append((1 - x if dx else x, 1 - y if dy else y, 1 - c if dc else c))
    return out


def _side_copies(kind, ins, outs, send_sems, recv_sems, local_sems, sem_row=0):
    n = len(ins)
    x, y, c = _mesh_place()
    me = _slot((x, y, c))
    peers = _peers(x, y, c)

    def src(a, block):
        return ins[a] if kind == "gather" else ins[a].at[block]

    def send(a, k, peer):
        return pltpu.make_async_remote_copy(
            src_ref=src(a, _slot(peer)), dst_ref=outs[a].at[me],
            send_sem=send_sems.at[sem_row + a, k], recv_sem=recv_sems.at[sem_row + a, k],
            device_id=peer, device_id_type=MESH)

    def arrival(a, k, peer):
        return pltpu.make_async_remote_copy(
            src_ref=src(a, _slot(peer)), dst_ref=outs[a].at[_slot(peer)],
            send_sem=send_sems.at[sem_row + a, k], recv_sem=recv_sems.at[sem_row + a, k],
            device_id=peer, device_id_type=MESH)

    def own(a):
        return pltpu.make_async_copy(src(a, me), outs[a].at[me], local_sems.at[sem_row + a, 0])

    def start():
        for k, peer in enumerate(peers):
            for a in range(n):
                send(a, k, peer).start()
        for a in range(n):
            own(a).start()

    def wait():
        for k, peer in enumerate(peers):
            for a in range(n):
                arrival(a, k, peer).wait_recv()
        for k, peer in enumerate(peers):
            for a in range(n):
                send(a, k, peer).wait_send()
        for a in range(n):
            own(a).wait()

    return start, None, wait


N_CHIPS = N_DEV // 2


def _pair_swap(ins, received, send_sems, recv_sems):
    n = len(ins)
    x, y, c = _mesh_place()
    sibling = (x, y, 1 - c)

    def send(a, q):
        return pltpu.make_async_remote_copy(
            src_ref=ins[a].at[2 * q + (1 - c)], dst_ref=received[a].at[q],
            send_sem=send_sems.at[a, q], recv_sem=recv_sems.at[a, q], device_id=sibling, device_id_type=MESH)

    def start():
        for a in range(n):
            for q in range(N_CHIPS):
                send(a, q).start()

    def wait():
        for a in range(n):
            for q in range(N_CHIPS):
                send(a, q).wait_recv()
        for a in range(n):
            for q in range(N_CHIPS):
                send(a, q).wait_send()

    return start, None, wait


def _quad_exchange(ins, outs, send_sems, recv_sems, local_sems, sem_row=0):
    n = len(ins)
    x, y, c = _mesh_place()
    mine = 2 * x + y
    chips = [(1 - x, y), (x, 1 - y), (1 - x, 1 - y)]

    def send(a, k, chip):
        return pltpu.make_async_remote_copy(
            src_ref=ins[a].at[2 * chip[0] + chip[1]], dst_ref=outs[a].at[mine],
            send_sem=send_sems.at[sem_row + a, k], recv_sem=recv_sems.at[sem_row + a, k],
            device_id=(chip[0], chip[1], c), device_id_type=MESH)

    def arrival(a, k, chip):
        return pltpu.make_async_remote_copy(
            src_ref=ins[a].at[2 * chip[0] + chip[1]], dst_ref=outs[a].at[2 * chip[0] + chip[1]],
            send_sem=send_sems.at[sem_row + a, k], recv_sem=recv_sems.at[sem_row + a, k],
            device_id=(chip[0], chip[1], c), device_id_type=MESH)

    def own(a):
        return pltpu.make_async_copy(ins[a].at[mine], outs[a].at[mine], local_sems.at[sem_row + a, 0])

    def start():
        for k, chip in enumerate(chips):
            for a in range(n):
                send(a, k, chip).start()
        for a in range(n):
            own(a).start()

    def wait():
        for k, chip in enumerate(chips):
            for a in range(n):
                arrival(a, k, chip).wait_recv()
        for k, chip in enumerate(chips):
            for a in range(n):
                send(a, k, chip).wait_send()
        for a in range(n):
            own(a).wait()

    return start, None, wait


def _relay_gather(ins, outs, send_sems, recv_sems, local_sems):
    n = len(ins)
    x, y, c = _mesh_place()
    me, sibling = (x, y, c), (x, y, 1 - c)
    chips = [(1 - x, y), (x, 1 - y), (1 - x, 1 - y)]

    def copy(a, k, block, to, src=None):
        dst = outs[a].at[_slot(block)]
        return pltpu.make_async_remote_copy(
            src_ref=dst if src is None else src, dst_ref=dst,
            send_sem=send_sems.at[a, k], recv_sem=recv_sems.at[a, k], device_id=to, device_id_type=MESH)

    def own(a):
        return pltpu.make_async_copy(ins[a], outs[a].at[_slot(me)], local_sems.at[a, 0])

    def start():
        for j, chip in enumerate(chips):
            for a in range(n):
                copy(a, 1 + j, me, (*chip, c), src=ins[a]).start()
        for a in range(n):
            copy(a, 0, me, sibling, src=ins[a]).start()
            own(a).start()

    def relay():
        for j, chip in enumerate(chips):
            for a in range(n):
                copy(a, 1 + j, (*chip, c), me).wait_recv()
                copy(a, 4 + j, (*chip, c), sibling).start()

    def wait():
        for a in range(n):
            copy(a, 0, sibling, me).wait_recv()
        for j, chip in enumerate(chips):
            for a in range(n):
                copy(a, 4 + j, (*chip, 1 - c), me).wait_recv()
        for j, chip in enumerate(chips):
            for a in range(n):
                copy(a, 1 + j, me, (*chip, c), src=ins[a]).wait_send()
                copy(a, 4 + j, (*chip, c), sibling).wait_send()
        for a in range(n):
            copy(a, 0, me, sibling, src=ins[a]).wait_send()
            own(a).wait()

    return start, relay, wait


def _side_out_shapes(kind, arrays):
    if kind in ("gather", "relay_gather"):
        return [jax.ShapeDtypeStruct((N_DEV,) + a.shape, a.dtype) for a in arrays]
    return [jax.ShapeDtypeStruct(a.shape, a.dtype) for a in arrays]


def _hosted_call(body, name, grid, in_specs, out_specs, out_shape, scratch_shapes, args, side=None):
    if side is None:
        outs = pl.pallas_call(
            body, name=name, grid=grid, in_specs=in_specs, out_specs=out_specs, out_shape=out_shape,
            scratch_shapes=scratch_shapes, compiler_params=_params(len(grid)))(*args)
        return outs, []
    kind, arrays = side
    side_shapes = _side_out_shapes(kind, arrays)
    n_in, n_out, n_scr, n_side = len(in_specs), len(out_specs), len(scratch_shapes), len(arrays)

    def hosted(*refs):
        pos = 0
        groups = []
        for size in (n_in, n_side, n_out, len(side_shapes), n_scr):
            groups.append(refs[pos:pos + size])
            pos += size
        ins, side_in, outs, side_out, scr = groups
        send_sems, recv_sems, local_sems = refs[pos:]
        ids = [pl.program_id(d) for d in range(len(grid))]
        is_first = functools.reduce(jnp.logical_and, [i == 0 for i in ids])
        is_last = functools.reduce(jnp.logical_and, [i == g - 1 for i, g in zip(ids, grid)])
        if kind == "relay_gather":
            start, relay, wait = _relay_gather(side_in, side_out, send_sems, recv_sems, local_sems)
        else:
            start, relay, wait = _side_copies(kind, side_in, side_out, send_sems, recv_sems, local_sems)
        pl.when(is_first)(start)
        if relay is not None:
            pl.when(is_last)(relay)
        body(*ins, *outs, *scr)
        pl.when(is_last)(wait)

    any_spec = pl.BlockSpec(memory_space=pl.ANY)
    outs = pl.pallas_call(
        hosted, name=name, grid=grid,
        in_specs=list(in_specs) + [any_spec] * n_side,
        out_specs=list(out_specs) + [any_spec] * len(side_shapes),
        out_shape=list(out_shape) + side_shapes,
        scratch_shapes=list(scratch_shapes) + [pltpu.SemaphoreType.DMA((n_side, 7)), pltpu.SemaphoreType.DMA((n_side, 7)),
                                               pltpu.SemaphoreType.DMA((n_side, N_CHIPS))],
        compiler_params=_params(len(grid)))(*args, *arrays)
    return outs[:n_out], outs[n_out:]


def _lane_chunks(width, chunk=2 * LANES):
    return [slice(n0, min(n0 + chunk, width)) for n0 in range(0, width, chunk)]


def _pipelined(chunks, first, middle, last):
    n = len(chunks)
    a, b, total = {}, {}, None
    for step in range(n + 2):
        if step < n:
            a[step] = first(chunks[step])
        if 0 <= step - 1 < n:
            b[step - 1] = middle(chunks[step - 1], a.pop(step - 1))
        if 0 <= step - 2 < n:
            part = last(chunks[step - 2], b.pop(step - 2))
            total = part if total is None else total + part
    return total


def _ffn_fwd(h, g_pre, g_post, w_gu, w_down, name, side=None, attn=None):
    T, D = h.shape
    nj = w_gu.shape[0] // 2
    FB = w_gu.shape[1]
    tm = TOKEN_TILE
    n_attn = 0 if attn is None else 5

    def body(*refs):
        h_ref, gpre_ref, gpost_ref, wg_ref, wu_ref, wd_ref = refs[:6]
        hout_ref, f_ref, a_ref, gu_ref = refs[6 + n_attn:10 + n_attn]
        a_scr, acc = refs[-2:]
        j = pl.program_id(1)

        @pl.when(j == 0)
        def _():
            if attn is None:
                x = h_ref[...]
            else:
                ma_ref, mb_ref, wo_ref, bo_ref, ga_ref = refs[6:11]
                att_ref, hmid_ref, mix_ref = refs[15:18]
                mix = jnp.concatenate([ma_ref[...], mb_ref[...]], axis=1).astype(BF16)
                mix_ref[...] = mix
                att = _mm(mix, wo_ref[...]) + bo_ref[...]
                att_ref[...] = att
                x = h_ref[...] + att * _rstd(att) * ga_ref[...]
                hmid_ref[...] = x
            a = (x * _rstd(x) * gpre_ref[...]).astype(BF16)
            a_scr[...] = a
            a_ref[...] = a
            acc[...] = jnp.zeros_like(acc)

        a = a_scr[...]
        g = _mm_nt(a, wg_ref[...])
        u = _mm_nt(a, wu_ref[...])
        gu_ref[0] = g.astype(BF16)
        gu_ref[1] = u.astype(BF16)
        hh = (g * jax.nn.sigmoid(g) * u).astype(BF16)
        acc[...] += _mm(hh, wd_ref[...])

        @pl.when(j == nj - 1)
        def _():
            f = acc[...]
            f_ref[...] = f
            x = h_ref[...] if attn is None else refs[16][...]
            hout_ref[...] = x + 0.5 * (f * _rstd(f) * gpost_ref[...])

    tile = pl.BlockSpec((tm, D), lambda i, j: (i, 0))
    in_specs = [tile, _full((1, D)), _full((1, D)),
                pl.BlockSpec((None, FB, D), lambda i, j: (j, 0, 0)),
                pl.BlockSpec((None, FB, D), lambda i, j: (j + nj, 0, 0)),
                pl.BlockSpec((FB, D), lambda i, j: (j, 0))]
    out_specs = [tile, tile, tile, pl.BlockSpec((None, 2, tm, FB), lambda i, j: (j, 0, i, 0))]
    out_shape = [
        jax.ShapeDtypeStruct((T, D), F32),
        jax.ShapeDtypeStruct((T, D), F32),
        jax.ShapeDtypeStruct((T, D), BF16),
        jax.ShapeDtypeStruct((nj, 2, T, FB), BF16),
    ]
    args = [h, g_pre, g_post, w_gu, w_gu, w_down]
    if attn is not None:
        mix_a, mix_b, w_out, b_out, g_attn = attn
        d_mix = w_out.shape[0]
        in_specs += [pl.BlockSpec((tm, mix_a.shape[1]), lambda i, j: (i, 0)),
                     pl.BlockSpec((tm, mix_b.shape[1]), lambda i, j: (i, 0)),
                     _full((d_mix, D)), _full((1, D)), _full((1, D))]
        out_specs += [tile, tile, pl.BlockSpec((tm, d_mix), lambda i, j: (i, 0))]
        out_shape += [jax.ShapeDtypeStruct((T, D), F32),
                      jax.ShapeDtypeStruct((T, D), F32),
                      jax.ShapeDtypeStruct((T, d_mix), BF16)]
        args += [mix_a, mix_b, w_out, b_out, g_attn]
    return _hosted_call(
        body, name, (T // tm, nj), in_specs=in_specs, out_specs=out_specs, out_shape=out_shape,
        scratch_shapes=[pltpu.VMEM((tm, D), BF16), pltpu.VMEM((tm, D), F32)], args=args, side=side)


def _ffn_bwd(dh_out, f, g_post, h, g_pre, gu, w_gu, w_down, name, side=None):
    T, D = h.shape
    nj = w_gu.shape[0] // 2
    FB = w_gu.shape[1]
    tm = TOKEN_TILE

    def body(dho_ref, f_ref, gpost_ref, h_ref, gpre_ref, gu_ref, wg_ref, wu_ref, wd_ref,
             dhin_ref, df_ref, hh_ref, dgu_ref, dgpost_ref, dgpre_ref, df_scr, da):
        i, j = pl.program_id(0), pl.program_id(1)

        @pl.when(jnp.logical_and(i == 0, j == 0))
        def _():
            dgpost_ref[...] = jnp.zeros_like(dgpost_ref)
            dgpre_ref[...] = jnp.zeros_like(dgpre_ref)

        @pl.when(j == 0)
        def _():
            fv = f_ref[...]
            df, dgain = _rms_bwd(fv, _rstd(fv), gpost_ref[...], 0.5 * dho_ref[...])
            dgpost_ref[...] += _colsum(dgain)
            dfb = df.astype(BF16)
            df_scr[...] = dfb
            df_ref[...] = dfb
            da[...] = jnp.zeros_like(da)

        dfb = df_scr[...]

        halves = (slice(0, tm // 2), slice(tm // 2, tm))

        def hidden_grad(c):
            return [_mm_nt(dfb[rows], wd_ref[c, :]) for rows in halves]

        def through_swiglu(c, dhh):
            dhh = jnp.concatenate(dhh, axis=0)
            g = gu_ref[0, :, c].astype(F32)
            u = gu_ref[1, :, c].astype(F32)
            sg = jax.nn.sigmoid(g)
            silu = g * sg
            hh_ref[:, c] = (silu * u).astype(BF16)
            dg = (dhh * u * (sg * (1.0 + (g - silu)))).astype(BF16)
            du = (dhh * silu).astype(BF16)
            dgu_ref[0, :, c] = dg
            dgu_ref[1, :, c] = du
            return dg, du

        def input_grad(c, dgu):
            return jnp.concatenate(
                [_mm(dgu[0][rows], wg_ref[c, :]) + _mm(dgu[1][rows], wu_ref[c, :]) for rows in halves], axis=0)

        da[...] += _pipelined(_lane_chunks(FB), hidden_grad, through_swiglu, input_grad)

        @pl.when(j == nj - 1)
        def _():
            x = h_ref[...]
            dx, dgain = _rms_bwd(x, _rstd(x), gpre_ref[...], da[...])
            dgpre_ref[...] += _colsum(dgain)
            dhin_ref[...] = dho_ref[...] + dx

    tile = pl.BlockSpec((tm, D), lambda i, j: (i, 0))
    return _hosted_call(
        body, name, (T // tm, nj),
        in_specs=[
            tile, tile, _full((1, D)), tile, _full((1, D)),
            pl.BlockSpec((None, 2, tm, FB), lambda i, j: (j, 0, i, 0)),
            pl.BlockSpec((None, FB, D), lambda i, j: (j, 0, 0)),
            pl.BlockSpec((None, FB, D), lambda i, j: (j + nj, 0, 0)),
            pl.BlockSpec((FB, D), lambda i, j: (j, 0)),
        ],
        out_specs=[
            tile, tile,
            pl.BlockSpec((None, tm, FB), lambda i, j: (j, i, 0)),
            pl.BlockSpec((None, 2, tm, FB), lambda i, j: (j, 0, i, 0)),
            _full((1, D)), _full((1, D)),
        ],
        out_shape=[
            jax.ShapeDtypeStruct((T, D), F32),
            jax.ShapeDtypeStruct((T, D), BF16),
            jax.ShapeDtypeStruct((nj, T, FB), BF16),
            jax.ShapeDtypeStruct((nj, 2, T, FB), BF16),
            jax.ShapeDtypeStruct((1, D), F32),
            jax.ShapeDtypeStruct((1, D), F32),
        ],
        scratch_shapes=[pltpu.VMEM((tm, D), BF16), pltpu.VMEM((tm, D), F32)],
        args=(dh_out, f, g_post, h, g_pre, gu, w_gu, w_gu, w_down), side=side)


def _tn_matmul(x, y, x_spec, y_spec, out_shape, out_spec, n_blocks, n_steps, acc_shape, name, side=None):
    def body(x_ref, y_ref, o_ref, acc):
        t = pl.program_id(1)

        @pl.when(t == 0)
        def _():
            acc[...] = jnp.zeros_like(acc)

        acc[...] += _mm_tn(x_ref[...].astype(BF16), y_ref[...].astype(BF16))

        @pl.when(t == n_steps - 1)
        def _():
            o_ref[...] = acc[...].astype(o_ref.dtype)

    outs, side_outs = _hosted_call(
        body, name, (n_blocks, n_steps), in_specs=[x_spec, y_spec], out_specs=[out_spec], out_shape=[out_shape],
        scratch_shapes=[pltpu.VMEM(acc_shape, F32)], args=(x, y), side=side)
    return (outs[0], side_outs) if side is not None else outs[0]


def _inproj_fwd(h, g_pre, w_in, b_in, side=None):
    T, D = h.shape
    tm = TOKEN_TILE

    def body(h_ref, g_ref, w_ref, b_ref, z_ref, a_ref):
        x = h_ref[...]
        a = (x * _rstd(x) * g_ref[...]).astype(BF16)
        a_ref[...] = a
        z_ref[...] = _mm_nt(a, w_ref[...]) + b_ref[...]

    return _hosted_call(
        body, "inproj_fwd", (T // tm,),
        in_specs=[pl.BlockSpec((tm, D), lambda i: (i, 0)), _full((1, D)), _full((D_IN, D)), _full((1, D_IN))],
        out_specs=[pl.BlockSpec((tm, D_IN), lambda i: (i, 0)), pl.BlockSpec((tm, D), lambda i: (i, 0))],
        out_shape=[jax.ShapeDtypeStruct((T, D_IN), F32), jax.ShapeDtypeStruct((T, D), BF16)],
        scratch_shapes=[], args=(h, g_pre, w_in, b_in), side=side)


def _inproj_bwd(dqa, dka, dva, dqb, dkb, dvb, w_in, h, g_pre, dres, side=None):
    T, D = h.shape
    tm = TOKEN_TILE

    def body(dqa_ref, dka_ref, dva_ref, dqb_ref, dkb_ref, dvb_ref, w_ref, h_ref, g_ref, dres_ref,
             dh_ref, dz_ref, dbin_ref, dg_ref):
        i = pl.program_id(0)

        @pl.when(i == 0)
        def _():
            dbin_ref[...] = jnp.zeros_like(dbin_ref)
            dg_ref[...] = jnp.zeros_like(dg_ref)

        dz = jnp.concatenate([dqa_ref[...], dka_ref[...], dva_ref[...], dqb_ref[...], dkb_ref[...], dvb_ref[...]],
                             axis=1)
        dbin_ref[...] += _colsum(dz)
        dzb = dz.astype(BF16)
        dz_ref[...] = dzb
        da = _mm(dzb, w_ref[...])
        x = h_ref[...]
        dx, dgain = _rms_bwd(x, _rstd(x), g_ref[...], da)
        dg_ref[...] += _colsum(dgain)
        dh_ref[...] = dres_ref[...] + dx

    def tile(w):
        return pl.BlockSpec((tm, w), lambda i: (i, 0))

    return _hosted_call(
        body, "inproj_bwd", (T // tm,),
        in_specs=[tile(A_Q), tile(A_KV), tile(A_KV), tile(B_W), tile(B_W), tile(B_W),
                  _full((D_IN, D)), tile(D), _full((1, D)), tile(D)],
        out_specs=[tile(D), tile(D_IN), _full((1, D_IN)), _full((1, D))],
        out_shape=[jax.ShapeDtypeStruct((T, D), F32), jax.ShapeDtypeStruct((T, D_IN), BF16),
                   jax.ShapeDtypeStruct((1, D_IN), F32), jax.ShapeDtypeStruct((1, D), F32)],
        scratch_shapes=[], args=(dqa, dka, dva, dqb, dkb, dvb, w_in, h, g_pre, dres), side=side)


def _bucket_tiles(patterns):
    i = np.arange(QBLK)[:, None]
    j = np.arange(2 * QBLK)[None, :]
    dist = QBLK + i - j
    max_exact = NUM_BUCKETS // 2
    tiles = []
    for dilation, max_dist in patterns:
        n = np.maximum(dist * dilation, 0)
        nf = np.maximum(n, 1).astype(np.float32)
        large = max_exact + (np.log(nf / np.float32(max_exact)) / np.float32(math.log(MAX_DISTANCE / max_exact))
                             * np.float32(NUM_BUCKETS - max_exact)).astype(np.int32)
        bucket = np.where(n < max_exact, n, np.minimum(large, NUM_BUCKETS - 1))
        tiles.append(np.where((dist >= 0) & (dist <= max_dist), bucket, -1))
    return jnp.asarray(np.stack(tiles).astype(np.int32))


def _bias_build(rel_bias, buckets, head0, name, side=None):
    n = buckets.shape[0]

    def body(bk_ref, rb_ref, o_ref):
        bk = bk_ref[...]
        base = jnp.where(bk < 0, NEG_INF, 0.0).astype(F32)
        for hd in range(N_HEAD_GROUP):
            o_ref[hd] = lax.fori_loop(
                0, NUM_BUCKETS, lambda b, acc, hd=hd: jnp.where(bk == b, rb_ref[b, head0 + hd], acc), base)

    outs, side_outs = _hosted_call(
        body, name, (n,),
        in_specs=[pl.BlockSpec((None, QBLK, 2 * QBLK), lambda p: (p, 0, 0)), pl.BlockSpec(memory_space=pltpu.SMEM)],
        out_specs=[pl.BlockSpec((None, N_HEAD_GROUP, QBLK, 2 * QBLK), lambda p: (p, 0, 0, 0))],
        out_shape=[jax.ShapeDtypeStruct((n, N_HEAD_GROUP, QBLK, 2 * QBLK), F32)],
        scratch_shapes=[], args=(buckets, rel_bias), side=side)
    return outs[0], side_outs


def _bias_grad(ds, buckets, name, after=None):
    n = buckets.shape[0]
    extra = [] if after is None else [after]

    def body(ds_ref, bk_ref, *rest):
        o_ref = rest[-1]
        bk = bk_ref[...]
        row = lax.broadcasted_iota(jnp.int32, (NUM_BUCKETS, 2 * QBLK), 0)
        for hd in range(N_HEAD_GROUP):
            d = ds_ref[hd]
            per_key = jnp.zeros((NUM_BUCKETS, 2 * QBLK), F32)
            for b in range(NUM_BUCKETS):
                per_key = jnp.where(row == b, jnp.sum(jnp.where(bk == b, d, 0.0), axis=0, keepdims=True), per_key)
            o_ref[hd] = jnp.broadcast_to(jnp.sum(per_key, axis=1, keepdims=True), (NUM_BUCKETS, LANES))

    out = pl.pallas_call(
        body, name=name, grid=(n,),
        in_specs=[pl.BlockSpec((None, N_HEAD_GROUP, QBLK, 2 * QBLK), lambda p: (p, 0, 0, 0)),
                  pl.BlockSpec((None, QBLK, 2 * QBLK), lambda p: (p, 0, 0))]
        + [pl.BlockSpec(memory_space=pl.ANY)] * len(extra),
        out_specs=pl.BlockSpec((None, N_HEAD_GROUP, NUM_BUCKETS, LANES), lambda p: (p, 0, 0, 0)),
        out_shape=jax.ShapeDtypeStruct((n, N_HEAD_GROUP, NUM_BUCKETS, LANES), F32),
        compiler_params=_params(1),
    )(ds, buckets, *extra)
    return out[:, :, :, 0].reshape(n * N_HEAD_GROUP, NUM_BUCKETS)


def _class_rows(start, dilation):
    if dilation == 1:
        return pl.ds(pl.multiple_of(start, QBLK), QBLK)
    return pl.ds(start, QBLK, stride=dilation)


def _starts_class(u, blocks_per_pass, n_blocks):
    return blocks_per_pass % n_blocks == 0 and u % n_blocks == 0


def _block_starts(idx, n_blocks, dilation):
    cls = idx // n_blocks
    n = idx % n_blocks
    cur = cls + dilation * QBLK * n
    prev = cls + dilation * QBLK * jnp.maximum(n - 1, 0)
    return n, cur, prev


class _HeadPair:
    def __init__(self, g, shared_kv):
        self.lane = lax.broadcasted_iota(jnp.int32, (1, LANES), 1)
        self.lower = self.lane < HEAD_DIM
        self.shared_kv = shared_kv
        self.key_lanes = (self.lane >= HEAD_DIM).astype(jnp.int32) == (g // 2)

    def stack(self, t):
        return jnp.concatenate([jnp.where(self.lower, t, 0.0), jnp.where(self.lower, 0.0, t)], axis=0).astype(BF16)

    def unstack(self, t2):
        return jnp.where(self.lower, t2[:QBLK], t2[QBLK:])

    def keys(self, t):
        if self.shared_kv:
            return jnp.where(self.key_lanes, t, pltpu.roll(t, HEAD_DIM, 1))
        return t

    def key_grads(self, t):
        if self.shared_kv:
            return jnp.where(self.key_lanes, t + pltpu.roll(t, HEAD_DIM, 1), 0.0)
        return t


def _attn_specs(T, qcol, kcol, vcol, shared_kv):
    kv = (lambda c: (lambda g: (0, c))) if shared_kv else (lambda c: (lambda g: (0, c + g)))
    return [pl.BlockSpec((T, LANES), lambda g: (0, qcol + g)),
            pl.BlockSpec((T, LANES), kv(kcol)),
            pl.BlockSpec((T, LANES), kv(vcol))]


def _attn_fwd(z, bias, sinks, patterns, qcol, kcol, vcol, shared_kv, name, side=None):
    T = z.shape[0]
    n_pat = len(patterns)
    has_sink = sinks is not None

    def body(*refs):
        if has_sink:
            sink_ref, refs = refs[0], refs[1:]
        q_ref, k_ref, v_ref, b_ref, o_ref, l_ref = refs[:6]
        po_scr = refs[6:6 + n_pat]
        pl_scr = refs[6 + n_pat:]
        g = pl.program_id(0)
        heads = _HeadPair(g, shared_kv)
        in_prev = lax.broadcasted_iota(jnp.int32, (2 * QBLK, 2 * QBLK), 1) < QBLK

        for pi, (dilation, _) in enumerate(patterns):
            n_blocks = T // (QBLK * dilation)

            def step(it, carry, pi=pi, dilation=dilation, n_blocks=n_blocks):
                blocks = []
                for u in range(FWD_BLOCKS):
                    n, cur, prev = _block_starts(it * FWD_BLOCKS + u, n_blocks, dilation)
                    rows_c, rows_p = _class_rows(cur, dilation), _class_rows(prev, dilation)
                    qm = heads.stack(q_ref[rows_c, :])
                    k_cur, v_cur = k_ref[rows_c, :], v_ref[rows_c, :]
                    no_past = _starts_class(u, FWD_BLOCKS, n_blocks)
                    if no_past:
                        k2, v2 = k_cur, v_cur
                    else:
                        if u % min(FWD_BLOCKS, n_blocks) == 0:
                            k_prev, v_prev = k_ref[rows_p, :], v_ref[rows_p, :]
                        k2 = jnp.concatenate([k_prev, k_cur], axis=0)
                        v2 = jnp.concatenate([v_prev, v_cur], axis=0)
                    k2, v2 = heads.keys(k2).astype(BF16), heads.keys(v2).astype(BF16)
                    k_prev, v_prev = k_cur, v_cur
                    blocks.append(dict(n=n, no_past=no_past, rows=rows_c, v2=v2, s=_mm_nt(qm, k2)))
                for b in blocks:
                    if b["no_past"]:
                        b["s"] = b["s"] * (HEAD_DIM ** -0.5) + b_ref[pi, :, QBLK:]
                    else:
                        s = b["s"] * (HEAD_DIM ** -0.5) + b_ref[pi]
                        b["s"] = jnp.where(jnp.logical_and(in_prev, b["n"] == 0), NEG_INF, s)
                    b["m"] = jnp.max(b["s"], axis=1, keepdims=True)
                for b in blocks:
                    b["pr"] = jnp.exp(b["s"] - b["m"])
                    b["den"] = jnp.sum(b["pr"], axis=1, keepdims=True)
                for b in blocks:
                    b["o2"] = _mm(b["pr"].astype(BF16), b["v2"])
                for b in blocks:
                    lse = b["m"] + jnp.log(b["den"])
                    po_scr[pi][b["rows"], :] = heads.unstack(b["o2"] / b["den"])
                    pl_scr[2 * pi][b["rows"], :] = jnp.broadcast_to(lse[:QBLK], (QBLK, LANES))
                    pl_scr[2 * pi + 1][b["rows"], :] = jnp.broadcast_to(lse[QBLK:], (QBLK, LANES))
                return carry

            lax.fori_loop(0, (dilation * n_blocks) // FWD_BLOCKS, step, 0)

        def merge(ci, carry):
            rows = pl.ds(pl.multiple_of(ci * QBLK, QBLK), QBLK)
            weights = []
            for hd in range(2):
                parts = [pl_scr[2 * pi + hd][rows, :] for pi in range(n_pat)]
                m = functools.reduce(jnp.maximum, parts)
                if has_sink:
                    sink = sink_ref[0, 2 * g + hd]
                    m = jnp.maximum(m, sink)
                terms = [jnp.exp(x - m) for x in parts]
                den = functools.reduce(jnp.add, terms)
                if has_sink:
                    den = den + jnp.exp(sink - m)
                l_ref[hd, rows, :] = m + jnp.log(den)
                inv = 1.0 / den
                weights.append([t * inv for t in terms])
            o_ref[rows, :] = functools.reduce(
                jnp.add, [jnp.where(heads.lower, weights[0][pi], weights[1][pi]) * po_scr[pi][rows, :]
                          for pi in range(n_pat)])
            return carry

        lax.fori_loop(0, T // QBLK, merge, 0)

    in_specs = _attn_specs(T, qcol, kcol, vcol, shared_kv)
    in_specs.append(pl.BlockSpec((n_pat, None, 2 * QBLK, 2 * QBLK), lambda g: (0, g, 0, 0)))
    args = [z, z, z, bias.reshape(n_pat, N_HEAD_GROUP // 2, 2 * QBLK, 2 * QBLK)]
    if has_sink:
        in_specs.insert(0, pl.BlockSpec(memory_space=pltpu.SMEM))
        args.insert(0, sinks)
    return _hosted_call(
        body, name, (N_HEAD_GROUP // 2,),
        in_specs=in_specs,
        out_specs=[pl.BlockSpec((T, LANES), lambda g: (0, g)), pl.BlockSpec((2, T, LANES), lambda g: (g, 0, 0))],
        out_shape=[jax.ShapeDtypeStruct((T, N_HEAD_GROUP * HEAD_DIM), F32),
                   jax.ShapeDtypeStruct((N_HEAD_GROUP, T, LANES), F32)],
        scratch_shapes=[pltpu.VMEM((T, LANES), F32)] * (3 * n_pat), args=args, side=side)


def _attn_bwd(z, bias, sinks, d_out, out, lse, patterns, qcol, kcol, vcol, shared_kv, name, side=None):
    T = z.shape[0]
    n_pat = len(patterns)
    has_sink = sinks is not None
    kv_width = LANES if shared_kv else N_HEAD_GROUP * HEAD_DIM

    def body(*refs):
        if has_sink:
            sink_ref, refs = refs[0], refs[1:]
        q_ref, k_ref, v_ref, b_ref, do_ref, o_ref, l0_ref, l1_ref = refs[:8]
        dq_ref, dk_ref, dv_ref, ds_ref = refs[8:12]
        dsink_ref = refs[12] if has_sink else None
        dk_acc, dv_acc = refs[-2:]
        g = pl.program_id(0)
        heads = _HeadPair(g, shared_kv)
        in_prev = lax.broadcasted_iota(jnp.int32, (2 * QBLK, 2 * QBLK), 1) < QBLK

        dq_ref[...] = jnp.zeros_like(dq_ref)
        ds_ref[...] = jnp.zeros_like(ds_ref)
        dk_acc[...] = jnp.zeros_like(dk_acc)
        dv_acc[...] = jnp.zeros_like(dv_acc)

        dsink = jnp.zeros((1, LANES), F32)
        for pi, (dilation, _) in enumerate(patterns):
            n_blocks = T // (QBLK * dilation)

            def step(idx, dsink, pi=pi, dilation=dilation, n_blocks=n_blocks):
                blocks = []
                for u in range(BWD_BLOCKS):
                    n, cur, prev = _block_starts(idx * BWD_BLOCKS + u, n_blocks, dilation)
                    rows_c, rows_p = _class_rows(cur, dilation), _class_rows(prev, dilation)
                    qm = heads.stack(q_ref[rows_c, :])
                    k_cur, v_cur = k_ref[rows_c, :], v_ref[rows_c, :]
                    first = u % min(BWD_BLOCKS, n_blocks) == 0
                    no_past = _starts_class(u, BWD_BLOCKS, n_blocks)
                    if no_past:
                        k2, v2 = k_cur, v_cur
                    else:
                        if first:
                            k_prev, v_prev = k_ref[rows_p, :], v_ref[rows_p, :]
                        k2 = jnp.concatenate([k_prev, k_cur], axis=0)
                        v2 = jnp.concatenate([v_prev, v_cur], axis=0)
                    k2, v2 = heads.keys(k2).astype(BF16), heads.keys(v2).astype(BF16)
                    k_prev, v_prev = k_cur, v_cur
                    d_o = do_ref[rows_c, :]
                    dom = heads.stack(d_o)
                    dd = d_o * o_ref[rows_c, :]
                    delta = jnp.concatenate([jnp.sum(jnp.where(heads.lower, dd, 0.0), axis=1, keepdims=True),
                                             jnp.sum(jnp.where(heads.lower, 0.0, dd), axis=1, keepdims=True)], axis=0)
                    lse = jnp.concatenate([l0_ref[rows_c, :], l1_ref[rows_c, :]], axis=0)
                    blocks.append(dict(n=n, first=first, no_past=no_past, rows_c=rows_c, rows_p=rows_p, qm=qm, k2=k2,
                                       dom=dom, delta=delta, lse=lse, s=_mm_nt(qm, k2), dp=_mm_nt(dom, v2)))
                for b in blocks:
                    if b["no_past"]:
                        s = b["s"] * (HEAD_DIM ** -0.5) + b_ref[pi, :, QBLK:]
                        b["pr"] = jnp.exp(s - b["lse"])
                    else:
                        s = b["s"] * (HEAD_DIM ** -0.5) + b_ref[pi]
                        s = jnp.where(jnp.logical_and(in_prev, b["n"] == 0), NEG_INF, s)
                        b["pr"] = jnp.exp(s - jnp.concatenate([b["lse"], b["lse"]], axis=1))
                    b["ds"] = b["pr"] * (b["dp"] - b["delta"])
                for b in blocks:
                    dsb = b["ds"].astype(BF16)
                    b["dq2"] = _mm(dsb, b["k2"])
                    b["dk2"] = _mm_tn(dsb, b["qm"])
                    b["dv2"] = _mm_tn(b["pr"].astype(BF16), b["dom"])
                for b in blocks:
                    b["dk2"] = heads.key_grads(b["dk2"]) * (HEAD_DIM ** -0.5)
                    b["dv2"] = heads.key_grads(b["dv2"])
                for u, b in enumerate(blocks):
                    dq_ref[b["rows_c"], :] += heads.unstack(b["dq2"]) * (HEAD_DIM ** -0.5)
                    if b["no_past"]:
                        ds_ref[pi, :, QBLK:] += b["ds"]
                        dk_own, dv_own = b["dk2"], b["dv2"]
                    else:
                        ds_ref[pi] += b["ds"]
                        dk_own, dv_own = b["dk2"][QBLK:], b["dv2"][QBLK:]
                    if u + 1 < len(blocks) and not blocks[u + 1]["first"]:
                        dk_own = dk_own + blocks[u + 1]["dk2"][:QBLK]
                        dv_own = dv_own + blocks[u + 1]["dv2"][:QBLK]
                    if b["first"] and not b["no_past"]:
                        dk_acc[b["rows_p"], :] += b["dk2"][:QBLK]
                        dv_acc[b["rows_p"], :] += b["dv2"][:QBLK]
                    dk_acc[b["rows_c"], :] += dk_own
                    dv_acc[b["rows_c"], :] += dv_own
                    if has_sink:
                        for hd in range(2):
                            rows_h = slice(QBLK * hd, QBLK * (hd + 1))
                            p_sink = jnp.exp(sink_ref[0, 2 * g + hd] - b["lse"][rows_h, 0:1])
                            dsink = dsink - jnp.where(heads.lane == 2 * g + hd,
                                                      jnp.sum(p_sink * b["delta"][rows_h]), 0.0)
                return dsink

            dsink = lax.fori_loop(0, (dilation * n_blocks) // BWD_BLOCKS, step, dsink)

        if shared_kv:
            @pl.when(g == 0)
            def _():
                dk_ref[...] = dk_acc[...]
                dv_ref[...] = dv_acc[...]

            @pl.when(g != 0)
            def _():
                dk_ref[...] += dk_acc[...]
                dv_ref[...] += dv_acc[...]
        else:
            dk_ref[...] = dk_acc[...]
            dv_ref[...] = dv_acc[...]

        if has_sink:
            @pl.when(g == 0)
            def _():
                dsink_ref[...] = dsink

            @pl.when(g != 0)
            def _():
                dsink_ref[...] += dsink

    pair = pl.BlockSpec((T, LANES), lambda g: (0, g))
    stacked = pl.BlockSpec((n_pat, None, 2 * QBLK, 2 * QBLK), lambda g: (0, g, 0, 0))
    stacked_shape = (n_pat, N_HEAD_GROUP // 2, 2 * QBLK, 2 * QBLK)
    in_specs = _attn_specs(T, qcol, kcol, vcol, shared_kv)
    in_specs += [stacked, pair, pair,
                 pl.BlockSpec((None, T, LANES), lambda g: (2 * g, 0, 0)),
                 pl.BlockSpec((None, T, LANES), lambda g: (2 * g + 1, 0, 0))]
    args = [z, z, z, bias.reshape(stacked_shape), d_out, out, lse, lse]
    kv_out = _full((T, LANES)) if shared_kv else pair
    out_specs = [pair, kv_out, kv_out, stacked]
    out_shape = [jax.ShapeDtypeStruct((T, N_HEAD_GROUP * HEAD_DIM), F32),
                 jax.ShapeDtypeStruct((T, kv_width), F32), jax.ShapeDtypeStruct((T, kv_width), F32),
                 jax.ShapeDtypeStruct(stacked_shape, F32)]
    if has_sink:
        in_specs.insert(0, pl.BlockSpec(memory_space=pltpu.SMEM))
        args.insert(0, sinks)
        out_specs.append(_full((1, LANES)))
        out_shape.append(jax.ShapeDtypeStruct((1, LANES), F32))
    outs, side_outs = _hosted_call(
        body, name, (N_HEAD_GROUP // 2,), in_specs=in_specs, out_specs=out_specs, out_shape=out_shape,
        scratch_shapes=[pltpu.VMEM((T, LANES), F32), pltpu.VMEM((T, LANES), F32)], args=args, side=side)
    outs = list(outs)
    outs[3] = outs[3].reshape(n_pat, N_HEAD_GROUP, QBLK, 2 * QBLK)
    return outs, side_outs


def _outproj_bwd(dh, att, g_post, w_out):
    T, D = dh.shape
    tm = TOKEN_TILE
    d_mix = w_out.shape[0]

    def body(dh_ref, att_ref, g_ref, w_ref, dma_ref, dmb_ref, datt_ref, dg_ref, db_ref):
        i = pl.program_id(0)

        @pl.when(i == 0)
        def _():
            dg_ref[...] = jnp.zeros_like(dg_ref)
            db_ref[...] = jnp.zeros_like(db_ref)

        att = att_ref[...]
        datt, dgain = _rms_bwd(att, _rstd(att), g_ref[...], dh_ref[...])
        dg_ref[...] += _colsum(dgain)
        db_ref[...] += _colsum(datt)
        dattb = datt.astype(BF16)
        datt_ref[...] = dattb
        dmix = _mm_nt(dattb, w_ref[...])
        dma_ref[...] = dmix[:, :A_Q]
        dmb_ref[...] = dmix[:, A_Q:]

    def tile(w):
        return pl.BlockSpec((tm, w), lambda i: (i, 0))

    return pl.pallas_call(
        body, name="outproj_bwd", grid=(T // tm,),
        in_specs=[tile(D), tile(D), _full((1, D)), _full((d_mix, D))],
        out_specs=[tile(A_Q), tile(B_W), tile(D), _full((1, D)), _full((1, D))],
        out_shape=[jax.ShapeDtypeStruct((T, A_Q), F32), jax.ShapeDtypeStruct((T, B_W), F32),
                   jax.ShapeDtypeStruct((T, D), BF16), jax.ShapeDtypeStruct((1, D), F32),
                   jax.ShapeDtypeStruct((1, D), F32)],
        compiler_params=_params(1),
    )(dh, att, g_post, w_out)


def _ple_fwd_bwd(h, g_pre, w_gate, p, w_proj, g_post, target):
    T, D = h.shape
    tm = TOKEN_TILE
    n_proj, ple, db = w_proj.shape

    def body(h_ref, gpre_ref, wg_ref, p_ref, wp_ref, gpost_ref, t_ref,
             a_ref, dpre_ref, de_ref, dh_ref, loss_ref, dgpost_ref, dgpre_ref):
        i = pl.program_id(0)

        @pl.when(i == 0)
        def _():
            loss_ref[...] = jnp.zeros_like(loss_ref)
            dgpost_ref[...] = jnp.zeros_like(dgpost_ref)
            dgpre_ref[...] = jnp.zeros_like(dgpre_ref)

        x = h_ref[...]
        rx = _rstd(x)
        a = (x * rx * gpre_ref[...]).astype(BF16)
        a_ref[...] = a
        gate = jax.nn.sigmoid(_mm(a, wg_ref[...]))
        pb = p_ref[...].astype(BF16)
        e = jnp.concatenate([_mm(pb, wp_ref[k]) for k in range(n_proj)], axis=1)
        ge = gate * e
        rg = _rstd(ge)
        diff = x + ge * rg * gpost_ref[...] - t_ref[...]
        loss_ref[...] += 0.5 * jnp.sum(jnp.mean(diff * diff, axis=1, keepdims=True))
        dy = diff * (1.0 / D)
        dge, dgain = _rms_bwd(ge, rg, gpost_ref[...], dy)
        dgpost_ref[...] += _colsum(dgain)
        de_ref[...] = (dge * gate).astype(BF16)
        dpre = (dge * e * gate * (1.0 - gate)).astype(BF16)
        dpre_ref[...] = dpre
        dx, dgain = _rms_bwd(x, rx, gpre_ref[...], _mm_nt(dpre, wg_ref[...]))
        dgpre_ref[...] += _colsum(dgain)
        dh_ref[...] = dy + dx

    def tile(w):
        return pl.BlockSpec((tm, w), lambda i: (i, 0))

    return pl.pallas_call(
        body, name="ple_fwd_bwd", grid=(T // tm,),
        in_specs=[tile(D), _full((1, D)), _full((D, D)), tile(ple), _full((n_proj, ple, db)), _full((1, D)), tile(D)],
        out_specs=[tile(D), tile(D), tile(D), tile(D), _full((1, LANES)), _full((1, D)), _full((1, D))],
        out_shape=[jax.ShapeDtypeStruct((T, D), BF16),
                   jax.ShapeDtypeStruct((T, D), BF16),
                   jax.ShapeDtypeStruct((T, D), BF16),
                   jax.ShapeDtypeStruct((T, D), F32),
                   jax.ShapeDtypeStruct((1, LANES), F32),
                   jax.ShapeDtypeStruct((1, D), F32),
                   jax.ShapeDtypeStruct((1, D), F32)],
        compiler_params=_params(1),
    )(h, g_pre, w_gate, p, w_proj, g_post, target)


def _ple_dw_proj(p, de, n_proj):
    T, ple = p.shape
    D = de.shape[1]
    db = D // n_proj
    tk = TOKEN_TILE
    nt = T // tk

    def body(p_ref, de_ref, o_ref, acc):
        t = pl.program_id(0)

        @pl.when(t == 0)
        def _():
            acc[...] = jnp.zeros_like(acc)

        acc[...] += _mm_tn(p_ref[...].astype(BF16), de_ref[...])

        @pl.when(t == nt - 1)
        def _():
            for k in range(n_proj):
                o_ref[k] = acc[:, k * db:(k + 1) * db].astype(BF16)

    return pl.pallas_call(
        body, name="ple_dw_proj", grid=(nt,),
        in_specs=[pl.BlockSpec((tk, ple), lambda t: (t, 0)), pl.BlockSpec((tk, D), lambda t: (t, 0))],
        out_specs=_full((n_proj, ple, db)), out_shape=jax.ShapeDtypeStruct((n_proj, ple, db), BF16),
        scratch_shapes=[pltpu.VMEM((ple, D), F32)], compiler_params=_params(1),
    )(p, de)


def _tok(width):
    return pl.BlockSpec((DW_TILE, width), lambda b, t: (t, 0))


def _dw_gu(a, dgu, name, side=None):
    T, D = a.shape
    nj, _, _, FB = dgu.shape
    return _tn_matmul(
        dgu, a, pl.BlockSpec((None, None, DW_TILE, FB), lambda b, t: (b % nj, b // nj, t, 0)), _tok(D),
        jax.ShapeDtypeStruct((2 * nj, FB, D), BF16), pl.BlockSpec((None, FB, D), lambda b, t: (b, 0, 0)),
        2 * nj, T // DW_TILE, (FB, D), name, side=side)


def _dw_down(hh, df, name, side=None):
    nj, T, FB = hh.shape
    D = df.shape[1]
    return _tn_matmul(
        hh, df, pl.BlockSpec((None, DW_TILE, FB), lambda b, t: (b, t, 0)), _tok(D),
        jax.ShapeDtypeStruct((nj, FB, D), BF16), pl.BlockSpec((None, FB, D), lambda b, t: (b, 0, 0)),
        nj, T // DW_TILE, (FB, D), name, side=side)


def _dw_rows(xm, y, name):
    T, k = xm.shape
    D = y.shape[1]
    out = _tn_matmul(
        xm, y, _tok(k), _tok(D), jax.ShapeDtypeStruct((k, D), BF16), _full((k, D)),
        1, T // DW_TILE, (k, D), name)
    return out.reshape(N_DEV, k // N_DEV, D)


def _cast_bf16(arrays):
    n = len(arrays)

    def body(*refs):
        for a in range(n):
            refs[n + a][...] = refs[a][...].astype(BF16)

    return pl.pallas_call(
        body, name="cast_shards",
        in_specs=[pl.BlockSpec(memory_space=pltpu.VMEM)] * n, out_specs=[pl.BlockSpec(memory_space=pltpu.VMEM)] * n,
        out_shape=[jax.ShapeDtypeStruct(a.shape, BF16) for a in arrays],
        compiler_params=pltpu.CompilerParams(vmem_limit_bytes=VMEM_LIMIT),
    )(*arrays)


def _pack_layout(D, n_rel_rows):
    n_bin = -(-D_IN // D)
    row_bin = len(GAINS)
    row_sink = row_bin + n_bin
    row_loss = row_sink + 1
    row_rb = -(-(row_loss + 1) // 8) * 8
    n_rows = row_rb + -(-n_rel_rows // 8) * 8
    bin_parts = [(r, min(D, D_IN - r * D)) for r in range(n_bin)]
    return row_bin, row_sink, row_loss, row_rb, n_rows, bin_parts


def _pair_swap_call(grad_blocks):
    def body(g_in, received, send_sems, recv_sems):
        start, _, wait = _pair_swap([g_in], [received], send_sems, recv_sems)
        start()
        wait()

    any_spec = pl.BlockSpec(memory_space=pl.ANY)
    return pl.pallas_call(
        body, name="pair_swap", in_specs=[any_spec], out_specs=any_spec,
        out_shape=jax.ShapeDtypeStruct((N_CHIPS,) + grad_blocks.shape[1:], grad_blocks.dtype),
        scratch_shapes=[pltpu.SemaphoreType.DMA((1, N_CHIPS)), pltpu.SemaphoreType.DMA((1, N_CHIPS))],
    )(grad_blocks)


def _pair_add(blocks, received, name):
    n, R, C = received.shape
    rows = _adamw_rows(R)
    core = lax.axis_index("c").astype(jnp.int32).reshape(1)

    def body(core_ref, a_ref, b_ref, o_ref):
        o_ref[...] = (a_ref[...].astype(F32) + b_ref[...].astype(F32)).astype(o_ref.dtype)

    tile = pl.BlockSpec((None, rows, C), lambda q, r, core_ref: (q, r, 0))
    return pl.pallas_call(
        body, name=name,
        grid_spec=pltpu.PrefetchScalarGridSpec(
            num_scalar_prefetch=1, grid=(n, R // rows),
            in_specs=[pl.BlockSpec((None, rows, C), lambda q, r, core_ref: (2 * q + core_ref[0], r, 0)), tile],
            out_specs=tile),
        out_shape=jax.ShapeDtypeStruct(received.shape, received.dtype), compiler_params=_params(2),
    )(core, blocks, received)


def _final_exchange(partials, loss):
    D = partials["ffn1_pre_g"].shape[1]
    rb_shape = partials["rel_bias"].shape
    row_bin, row_sink, row_loss, row_rb, n_rows, bin_parts = _pack_layout(D, rb_shape[0])
    n_small = len(SMALL)

    def body(*refs):
        part = dict(zip(SMALL, refs[:n_small]))
        loss_ref = refs[n_small]
        gath, pack, send_sems, recv_sems, local_sems = refs[1 + n_small:]

        pack[...] = jnp.zeros_like(pack)
        for i, name in enumerate(GAINS):
            pack[i:i + 1, :] = part[name][...]
        for r, width in bin_parts:
            pack[row_bin + r:row_bin + r + 1, 0:width] = part["b_in"][:, r * D:r * D + width]
        pack[row_sink:row_sink + 1, 0:LANES] = part["sinks"][...]
        pack[row_loss:row_loss + 1, 0:LANES] = loss_ref[...]
        pack[row_rb:row_rb + rb_shape[0], 0:rb_shape[1]] = part["rel_bias"][...]

        small_start, _, small_wait = _side_copies("gather", [pack], [gath], send_sems, recv_sems, local_sems, sem_row=0)
        small_start()
        small_wait()

    args = [partials[k] for k in SMALL] + [loss]
    vmem = pl.BlockSpec(memory_space=pltpu.VMEM)
    return pl.pallas_call(
        body, name="final_exchange",
        in_specs=[vmem] * (n_small + 1),
        out_specs=pl.BlockSpec(memory_space=pl.ANY),
        out_shape=jax.ShapeDtypeStruct((N_DEV, n_rows, D), F32),
        scratch_shapes=[pltpu.VMEM((n_rows, D), F32), pltpu.SemaphoreType.DMA((1, 7)),
                        pltpu.SemaphoreType.DMA((1, 7)), pltpu.SemaphoreType.DMA((1, N_CHIPS))],
    )(*args)


def _adamw(w, g, m, v):
    m = ADAM_B1 * m + (1.0 - ADAM_B1) * g
    v = ADAM_B2 * v + (1.0 - ADAM_B2) * (g * g)
    m_hat = m / (1.0 - ADAM_B1 ** ADAM_STEP)
    v_hat = v / (1.0 - ADAM_B2 ** ADAM_STEP)
    return -ADAM_LR * (m_hat / (jnp.sqrt(v_hat) + ADAM_EPS) + ADAM_WD * w), m, v


def _quad_start(pairs):
    def body(pairs_ref, land_ref, send_sems, recv_sems, pairs_thru, land_thru, token):
        x, y, c = _mesh_place()
        mine = 2 * x + y
        for k, chip in enumerate([(1 - x, y), (x, 1 - y), (1 - x, 1 - y)]):
            pltpu.make_async_remote_copy(
                src_ref=pairs_ref.at[2 * chip[0] + chip[1]], dst_ref=land_ref.at[mine],
                send_sem=send_sems.at[k], recv_sem=recv_sems.at[k],
                device_id=(chip[0], chip[1], c), device_id_type=MESH).start()
        token[...] = jnp.zeros_like(token)

    hbm = pl.BlockSpec(memory_space=pltpu.HBM)
    sem = pl.BlockSpec(memory_space=pltpu.SEMAPHORE)
    return pl.pallas_call(
        body, name="quad_start",
        out_shape=(pltpu.SemaphoreType.DMA((3,)), pltpu.SemaphoreType.DMA((3,)),
                   pltpu.HBM(pairs.shape, pairs.dtype), pltpu.HBM(pairs.shape, pairs.dtype),
                   jax.ShapeDtypeStruct((8, LANES), F32)),
        in_specs=(hbm, hbm), out_specs=(sem, sem, hbm, hbm, pl.BlockSpec(memory_space=pltpu.VMEM)),
        input_output_aliases={0: 2, 1: 3},
        compiler_params=pltpu.CompilerParams(has_side_effects=pltpu.SideEffectType.DATAFLOW_SIDE_EFFECTING),
    )(pltpu.with_memory_space_constraint(pairs, pltpu.HBM),
      pltpu.with_memory_space_constraint(jnp.copy(pairs), pltpu.HBM))


def _quad_wait(send_sems, recv_sems, pairs_thru, land_thru, after):
    def body(pairs_ref, land_ref, send_sems, recv_sems, after_ref, pairs_dead, land_out):
        x, y, c = _mesh_place()
        mine = 2 * x + y
        for k, chip in enumerate([(1 - x, y), (x, 1 - y), (1 - x, 1 - y)]):
            q = 2 * chip[0] + chip[1]
            pltpu.make_async_remote_copy(
                src_ref=pairs_ref.at[q], dst_ref=land_ref.at[mine], send_sem=send_sems.at[k],
                recv_sem=recv_sems.at[k], device_id=(chip[0], chip[1], c), device_id_type=MESH).wait_send()
            pltpu.make_async_remote_copy(
                src_ref=pairs_ref.at[q], dst_ref=land_ref.at[q], send_sem=send_sems.at[k],
                recv_sem=recv_sems.at[k], device_id=(chip[0], chip[1], c), device_id_type=MESH).wait_recv()

    hbm = pl.BlockSpec(memory_space=pltpu.HBM)
    sem = pl.BlockSpec(memory_space=pltpu.SEMAPHORE)
    return pl.pallas_call(
        body, name="quad_wait",
        out_shape=(pltpu.HBM(pairs_thru.shape, pairs_thru.dtype), pltpu.HBM(land_thru.shape, land_thru.dtype)),
        in_specs=(hbm, hbm, sem, sem, pl.BlockSpec(memory_space=pl.ANY)), out_specs=(hbm, hbm),
        input_output_aliases={0: 0, 1: 1},
        compiler_params=pltpu.CompilerParams(has_side_effects=pltpu.SideEffectType.DATAFLOW_SIDE_EFFECTING),
    )(pairs_thru, land_thru, send_sems, recv_sems, after)[1]


def _sum_adamw(partials, w, m, v, rows, name, after=None):
    R, C = w.shape
    n = partials.shape[0]

    def body(p_ref, w_ref, m_ref, v_ref, *rest):
        g_ref, d_ref, nm_ref, nv_ref = rest[-4:]
        g = p_ref[0].astype(F32)
        for k in range(1, n):
            g = g + p_ref[k].astype(F32)
        g_ref[...] = g
        d_ref[...], nm_ref[...], nv_ref[...] = _adamw(w_ref[...], g, m_ref[...], v_ref[...])

    tile = pl.BlockSpec((rows, C), lambda i: (i, 0))
    extra = [] if after is None else [after]
    return pl.pallas_call(
        body, name=name, grid=(R // rows,),
        in_specs=[pl.BlockSpec((n, rows, C), lambda i: (0, i, 0)), tile, tile, tile]
        + [pl.BlockSpec(memory_space=pl.ANY)] * len(extra),
        out_specs=[tile] * 4, out_shape=[jax.ShapeDtypeStruct((R, C), F32)] * 4,
        compiler_params=_params(1),
    )(partials, w, m, v, *extra)


def _small_adamw(gathered, ws, ms, vs):
    D = ws["ffn1_pre_g"].shape[1]
    n_sink = ws["sinks"].shape[1]
    rb_shape = ws["rel_bias"].shape
    row_bin, row_sink, row_loss, row_rb, n_rows, bin_parts = _pack_layout(D, rb_shape[0])
    n_small = len(SMALL)

    def body(*refs):
        gath = refs[0]
        pos = 1
        w_ref = dict(zip(SMALL, refs[pos:pos + n_small]))
        m_ref = dict(zip(SMALL, refs[pos + n_small:pos + 2 * n_small]))
        v_ref = dict(zip(SMALL, refs[pos + 2 * n_small:pos + 3 * n_small]))
        pos += 3 * n_small
        outs = {name: refs[pos + 4 * i:pos + 4 * i + 4] for i, name in enumerate(SMALL)}
        loss_out = refs[pos + 4 * n_small]
        pack = refs[pos + 4 * n_small + 1]

        total = gath[0]
        for k in range(1, N_DEV):
            total = total + gath[k]
        pack[...] = total

        def update(name, g):
            g_out, d_out, m_out, v_out = outs[name]
            g_out[...] = g
            d_out[...], m_out[...], v_out[...] = _adamw(w_ref[name][...], g, m_ref[name][...], v_ref[name][...])

        for i, name in enumerate(GAINS):
            update(name, pack[i:i + 1, :])
        update("b_in", jnp.concatenate([pack[row_bin + r:row_bin + r + 1, 0:width] for r, width in bin_parts], axis=1))
        update("sinks", pack[row_sink:row_sink + 1, 0:n_sink])
        update("rel_bias", pack[row_rb:row_rb + rb_shape[0], 0:rb_shape[1]])
        loss_out[...] = pack[row_loss:row_loss + 1, 0:LANES]

    args = [gathered]
    for group in (ws, ms, vs):
        args += [group[k] for k in SMALL]
    out_shape = []
    for name in SMALL:
        out_shape += [jax.ShapeDtypeStruct(ws[name].shape, F32)] * 4
    out_shape.append(jax.ShapeDtypeStruct((1, LANES), F32))
    res = pl.pallas_call(
        body, name="small_adamw",
        in_specs=[pl.BlockSpec(memory_space=pltpu.VMEM)] * len(args),
        out_specs=[pl.BlockSpec(memory_space=pltpu.VMEM)] * len(out_shape),
        out_shape=out_shape,
        scratch_shapes=[pltpu.VMEM((n_rows, D), F32)],
    )(*args)
    per_name = {name: res[4 * i:4 * i + 4] for i, name in enumerate(SMALL)}
    return per_name, res[-1]


COLUMN_SHARDED = ("ffn1_w_gu", "ffn2_w_gu", "w_in")


def _adamw_rows(rows_total):
    return max(r for r in range(16, min(rows_total, 256) + 1, 16) if rows_total % r == 0)


def kernel(x, p, rel_bias, ffn1_pre_g, ffn1_w_gu, ffn1_w_down, ffn1_post_g, attn_pre_g, w_in, b_in, sinks, w_out, b_out, attn_post_g, ffn2_pre_g, ffn2_w_gu, ffn2_w_down, ffn2_post_g, ple_pre_g, w_ple_gate, w_ple_proj, ple_post_g, loss_target, m_rel_bias, m_ffn1_pre_g, m_ffn1_w_gu, m_ffn1_w_down, m_ffn1_post_g, m_attn_pre_g, m_w_in, m_b_in, m_sinks, m_w_out, m_b_out, m_attn_post_g, m_ffn2_pre_g, m_ffn2_w_gu, m_ffn2_w_down, m_ffn2_post_g, m_ple_pre_g, m_w_ple_gate, m_w_ple_proj, m_ple_post_g, v_rel_bias, v_ffn1_pre_g, v_ffn1_w_gu, v_ffn1_w_down, v_ffn1_post_g, v_attn_pre_g, v_w_in, v_b_in, v_sinks, v_w_out, v_b_out, v_attn_post_g, v_ffn2_pre_g, v_ffn2_w_gu, v_ffn2_w_down, v_ffn2_post_g, v_ple_pre_g, v_w_ple_gate, v_w_ple_proj, v_ple_post_g):
    given = dict(locals())
    ws = {k: given[k] for k in WEIGHTS}
    ms = {k: given["m_" + k] for k in WEIGHTS}
    vs = {k: given["v_" + k] for k in WEIGHTS}

    def shard(t):
        return t.reshape(t.shape[1:])

    xs, ps, target = shard(x), shard(shard(p)), shard(loss_target)
    T, D = xs.shape
    small = {k: ws[k] for k in SMALL}

    def local(group, k):
        t = shard(group[k])
        return jnp.swapaxes(t, 0, 1) if k in COLUMN_SHARDED else t

    shards = {k: local(ws, k) for k in BIG}

    cast = dict(zip(BIG, _cast_bf16([shards[k] for k in BIG])))
    buckets_a = _bucket_tiles(PATTERNS_A)
    buckets_b = _bucket_tiles(PATTERNS_B)
    bias_a, _ = _bias_build(small["rel_bias"], buckets_a, 0, "bias_build_a")
    bias_b, (w_gu1, w_down1) = _bias_build(
        small["rel_bias"], buckets_b, N_HEAD_GROUP, "bias_build_b",
        side=("relay_gather", [cast["ffn1_w_gu"], cast["ffn1_w_down"]]))
    w_down1 = w_down1.reshape(-1, D)
    a_cfg = dict(patterns=PATTERNS_A, qcol=Q_A_COL, kcol=K_A_COL, vcol=V_A_COL, shared_kv=True)
    b_cfg = dict(patterns=PATTERNS_B, qcol=Q_B_COL, kcol=K_B_COL, vcol=V_B_COL, shared_kv=False)

    (h1, f1, a1, gu1), (w_in_g, w_down2) = _ffn_fwd(
        xs, small["ffn1_pre_g"], small["ffn1_post_g"], w_gu1, w_down1, "ffn1_fwd",
        side=("relay_gather", [cast["w_in"], cast["ffn2_w_down"]]))
    w_in_full = w_in_g.reshape(D_IN, D)
    w_down2 = w_down2.reshape(-1, D)
    (z, a2), (w_out_g,) = _inproj_fwd(h1, small["attn_pre_g"], w_in_full, small["b_in"],
                                      side=("relay_gather", [cast["w_out"]]))
    w_out_full = w_out_g.reshape(-1, D)
    (mix_a, lse_a), (w_gate, w_proj) = _attn_fwd(
        z, bias_a, small["sinks"], name="attn_a_fwd", **a_cfg,
        side=("relay_gather", [cast["w_ple_gate"], cast["w_ple_proj"]]))
    w_gate = w_gate.reshape(-1, D)
    (mix_b, lse_b), (w_gu2,) = _attn_fwd(
        z, bias_b, None, name="attn_b_fwd", **b_cfg, side=("relay_gather", [cast["ffn2_w_gu"]]))
    (h3, f2, a3, gu2, att, h2, mix), _ = _ffn_fwd(
        h1, small["ffn2_pre_g"], small["ffn2_post_g"], w_gu2, w_down2, "ffn2_fwd",
        attn=(mix_a, mix_b, w_out_full, small["b_out"], small["attn_post_g"]))
    a4, dpre, de, dh3, loss, dg_ple_post, dg_ple_pre = _ple_fwd_bwd(
        h3, small["ple_pre_g"], w_gate, ps, w_proj, small["ple_post_g"], target)

    d_gate = _dw_rows(a4, dpre, "ple_dw_gate")
    d_proj = _ple_dw_proj(ps, de, N_DEV)
    landed = {}
    (dh2, df2, hh2, dgu2, dg_f2_post, dg_f2_pre), (landed["w_ple_gate"], landed["w_ple_proj"]) = _ffn_bwd(
        dh3, f2, small["ffn2_post_g"], h2, small["ffn2_pre_g"], gu2, w_gu2, w_down2, "ffn2_bwd",
        side=("exchange", [d_gate, d_proj]))
    d_gu2 = _dw_gu(a3, dgu2, "ffn2_dw_gu")
    d_down2 = _dw_down(hh2, df2, "ffn2_dw_down").reshape(N_DEV, -1, D)
    dmix_a, dmix_b, datt, dg_attn_post, db_out = _outproj_bwd(dh2, att, small["attn_post_g"], w_out_full)
    d_out = _dw_rows(mix, datt, "attn_dw_out")
    (dqa, dka, dva, ds_a, dsinks), _ = _attn_bwd(
        z, bias_a, small["sinks"], dmix_a, mix_a, lse_a, name="attn_a_bwd", **a_cfg)
    (dqb, dkb, dvb, ds_b), (landed["ffn2_w_gu"],) = _attn_bwd(
        z, bias_b, None, dmix_b, mix_b, lse_b, name="attn_b_bwd", **b_cfg, side=("exchange", [d_gu2]))
    (dh1, dz, db_in, dg_attn_pre), (landed["w_out"],) = _inproj_bwd(
        dqa, dka, dva, dqb, dkb, dvb, w_in_full, h1, small["attn_pre_g"], dh2, side=("exchange", [d_out]))
    cols = D_IN // 3
    d_in = _tn_matmul(
        dz, a2, pl.BlockSpec((DW_TILE, cols), lambda b, t: (t, b)), _tok(D),
        jax.ShapeDtypeStruct((D_IN, D), BF16), pl.BlockSpec((cols, D), lambda b, t: (b, 0)),
        3, T // DW_TILE, (cols, D), "attn_dw_in").reshape(N_DEV, D_IN // N_DEV, D)
    (grad_x, df1, hh1, dgu1, dg_f1_post, dg_f1_pre), (landed["w_in"], landed["ffn2_w_down"]) = _ffn_bwd(
        dh1, f1, small["ffn1_post_g"], xs, small["ffn1_pre_g"], gu1, w_gu1, w_down1, "ffn1_bwd",
        side=("exchange", [d_in, d_down2]))
    d_down1 = _dw_down(hh1, df1, "ffn1_dw_down").reshape(N_DEV, -1, D)
    d_gu1, (landed["ffn1_w_down"],) = _dw_gu(a1, dgu1, "ffn1_dw_gu", side=("exchange", [d_down1]))

    d_gu1_pairs = _pair_add(d_gu1, _pair_swap_call(d_gu1), "ffn1_dw_gu_pair_add")
    send_sems, recv_sems, pairs_thru, land_thru, token = _quad_start(d_gu1_pairs)
    rb_a = _bias_grad(ds_a, buckets_a, "bias_grad_a", after=token)
    rb_b = _bias_grad(ds_b, buckets_b, "bias_grad_b", after=token).reshape(
        len(PATTERNS_B), N_HEAD_GROUP, NUM_BUCKETS)
    d_rel_bias = jnp.concatenate([rb_a.T, jnp.sum(rb_b, axis=0).T], axis=1)
    small_grads = {"ffn1_pre_g": dg_f1_pre, "ffn1_post_g": dg_f1_post, "attn_pre_g": dg_attn_pre,
                   "attn_post_g": dg_attn_post, "ffn2_pre_g": dg_f2_pre, "ffn2_post_g": dg_f2_post,
                   "ple_pre_g": dg_ple_pre, "ple_post_g": dg_ple_post, "b_out": db_out, "b_in": db_in,
                   "sinks": dsinks, "rel_bias": d_rel_bias}
    small_gathered = _final_exchange(small_grads, loss)
    updates = {}
    after = small_gathered
    for k in BIG:
        if k != "ffn1_w_gu":
            updates[k] = _sum_adamw(landed[k], shards[k], local(ms, k), local(vs, k),
                                    _adamw_rows(shards[k].shape[0]), k + "_adamw", after=after)
            after = updates[k][0]
    landed["ffn1_w_gu"] = _quad_wait(send_sems, recv_sems, pairs_thru, land_thru, after)
    k = "ffn1_w_gu"
    updates[k] = _sum_adamw(landed[k], shards[k], local(ms, k), local(vs, k), _adamw_rows(shards[k].shape[0]),
                            k + "_adamw")
    result = {}
    for k in BIG:
        outs = updates[k]
        if k in COLUMN_SHARDED:
            outs = [jnp.swapaxes(o, 0, 1) for o in outs]
        result[k] = [o.reshape(ws[k].shape) for o in outs]
    small_res, loss_all = _small_adamw(
        small_gathered, small, {k: ms[k] for k in SMALL}, {k: vs[k] for k in SMALL})
    result.update(small_res)

    out = [loss_all[0, 0], grad_x.reshape(x.shape)]
    for i in range(4):
        out += [result[k][i] for k in WEIGHTS]
    return tuple(out)
```

```python
import functools
import math

import numpy as np
import jax
import jax.numpy as jnp
from jax import lax
from jax.experimental import pallas as pl
from jax.experimental.pallas import tpu as pltpu

F32 = jnp.float32
BF16 = jnp.bfloat16
MESH = pl.DeviceIdType.MESH

N_DEV = 8
EPS = 1e-6
NEG_INF = -1e30
HEAD_DIM = 64
LANES = 128
QBLK = 128
D_IN = 2304
A_Q, A_KV, B_W = 512, 128, 512
N_HEAD_GROUP = 8
NUM_BUCKETS = 32
MAX_DISTANCE = 2048
PATTERNS_A = ((1, 127),)
PATTERNS_B = ((1, 128), (4, 128), (16, 128))
Q_A_COL, K_A_COL, V_A_COL = 0, 4, 5
Q_B_COL, K_B_COL, V_B_COL = 6, 10, 14

ADAM_LR, ADAM_B1, ADAM_B2, ADAM_EPS, ADAM_WD, ADAM_STEP = 0.001, 0.9, 0.999, 1e-08, 0.01, 10

TOKEN_TILE = 512
DW_TILE = 1024
FWD_BLOCKS = 4
BWD_BLOCKS = 4
VMEM_LIMIT = 56 * 1024 * 1024
ARB = "arbitrary"

BIG = ("ffn1_w_gu", "ffn1_w_down", "w_in", "w_out", "ffn2_w_gu", "ffn2_w_down", "w_ple_gate", "w_ple_proj")
GAINS = ("ffn1_pre_g", "ffn1_post_g", "attn_pre_g", "attn_post_g", "ffn2_pre_g", "ffn2_post_g",
         "ple_pre_g", "ple_post_g", "b_out")
SMALL = GAINS + ("b_in", "sinks", "rel_bias")
WEIGHTS = ("rel_bias", "ffn1_pre_g", "ffn1_w_gu", "ffn1_w_down", "ffn1_post_g", "attn_pre_g", "w_in", "b_in",
           "sinks", "w_out", "b_out", "attn_post_g", "ffn2_pre_g", "ffn2_w_gu", "ffn2_w_down", "ffn2_post_g",
           "ple_pre_g", "w_ple_gate", "w_ple_proj", "ple_post_g")


def _params(n_axes):
    return pltpu.CompilerParams(dimension_semantics=(ARB,) * n_axes, vmem_limit_bytes=VMEM_LIMIT)


def _mm(a, b):
    return jnp.dot(a, b, preferred_element_type=F32)


def _mm_nt(a, b):
    return lax.dot_general(a, b, (((1,), (1,)), ((), ())), preferred_element_type=F32)


def _mm_tn(a, b):
    return lax.dot_general(a, b, (((0,), (0,)), ((), ())), preferred_element_type=F32)


def _rstd(x):
    return lax.rsqrt(jnp.mean(x * x, axis=-1, keepdims=True) + EPS)


def _rms_bwd(x, r, gain, dy):
    n = x * r
    gdy = dy * gain
    return r * (gdy - n * jnp.mean(gdy * n, axis=-1, keepdims=True)), dy * n


def _colsum(v):
    return jnp.sum(v, axis=0, keepdims=True)


def _full(shape):
    return pl.BlockSpec(shape, lambda *_: (0,) * len(shape))


def _mesh_place():
    return lax.axis_index("x"), lax.axis_index("y"), lax.axis_index("c")


def _slot(dev):
    return 4 * dev[0] + 2 * dev[1] + dev[2]


def _peers(x, y, c):
    out = []
    for flip in range(1, N_DEV):
        dx, dy, dc = (flip >> 2) & 1, (flip >> 1) & 1, flip & 1
        out.append((1 - x if dx else x, 1 - y if dy else y, 1 - c if dc else c))
    return out


def _side_copies(kind, ins, outs, send_sems, recv_sems, local_sems, sem_row=0):
    n = len(ins)
    x, y, c = _mesh_place()
    me = _slot((x, y, c))
    peers = _peers(x, y, c)

    def src(a, block):
        return ins[a] if kind == "gather" else ins[a].at[block]

    def send(a, k, peer):
        return pltpu.make_async_remote_copy(
            src_ref=src(a, _slot(peer)), dst_ref=outs[a].at[me],
            send_sem=send_sems.at[sem_row + a, k], recv_sem=recv_sems.at[sem_row + a, k],
            device_id=peer, device_id_type=MESH)

    def arrival(a, k, peer):
        return pltpu.make_async_remote_copy(
            src_ref=src(a, _slot(peer)), dst_ref=outs[a].at[_slot(peer)],
            send_sem=send_sems.at[sem_row + a, k], recv_sem=recv_sems.at[sem_row + a, k],
            device_id=peer, device_id_type=MESH)

    def own(a):
        return pltpu.make_async_copy(src(a, me), outs[a].at[me], local_sems.at[sem_row + a, 0])

    def start():
        for k, peer in enumerate(peers):
            for a in range(n):
                send(a, k, peer).start()
        for a in range(n):
            own(a).start()

    def wait():
        for k, peer in enumerate(peers):
            for a in range(n):
                arrival(a, k, peer).wait_recv()
        for k, peer in enumerate(peers):
            for a in range(n):
                send(a, k, peer).wait_send()
        for a in range(n):
            own(a).wait()

    return start, None, wait


N_CHIPS = N_DEV // 2


def _pair_swap(ins, received, send_sems, recv_sems):
    n = len(ins)
    x, y, c = _mesh_place()
    sibling = (x, y, 1 - c)

    def send(a, q):
        return pltpu.make_async_remote_copy(
            src_ref=ins[a].at[2 * q + (1 - c)], dst_ref=received[a].at[q],
            send_sem=send_sems.at[a, q], recv_sem=recv_sems.at[a, q], device_id=sibling, device_id_type=MESH)

    def start():
        for a in range(n):
            for q in range(N_CHIPS):
                send(a, q).start()

    def wait():
        for a in range(n):
            for q in range(N_CHIPS):
                send(a, q).wait_recv()
        for a in range(n):
            for q in range(N_CHIPS):
                send(a, q).wait_send()

    return start, None, wait


def _quad_exchange(ins, outs, send_sems, recv_sems, local_sems, sem_row=0):
    n = len(ins)
    x, y, c = _mesh_place()
    mine = 2 * x + y
    chips = [(1 - x, y), (x, 1 - y), (1 - x, 1 - y)]

    def send(a, k, chip):
        return pltpu.make_async_remote_copy(
            src_ref=ins[a].at[2 * chip[0] + chip[1]], dst_ref=outs[a].at[mine],
            send_sem=send_sems.at[sem_row + a, k], recv_sem=recv_sems.at[sem_row + a, k],
            device_id=(chip[0], chip[1], c), device_id_type=MESH)

    def arrival(a, k, chip):
        return pltpu.make_async_remote_copy(
            src_ref=ins[a].at[2 * chip[0] + chip[1]], dst_ref=outs[a].at[2 * chip[0] + chip[1]],
            send_sem=send_sems.at[sem_row + a, k], recv_sem=recv_sems.at[sem_row + a, k],
            device_id=(chip[0], chip[1], c), device_id_type=MESH)

    def own(a):
        return pltpu.make_async_copy(ins[a].at[mine], outs[a].at[mine], local_sems.at[sem_row + a, 0])

    def start():
        for k, chip in enumerate(chips):
            for a in range(n):
                send(a, k, chip).start()
        for a in range(n):
            own(a).start()

    def wait():
        for k, chip in enumerate(chips):
            for a in range(n):
                arrival(a, k, chip).wait_recv()
        for k, chip in enumerate(chips):
            for a in range(n):
                send(a, k, chip).wait_send()
        for a in range(n):
            own(a).wait()

    return start, None, wait


def _relay_gather(ins, outs, send_sems, recv_sems, local_sems):
    n = len(ins)
    x, y, c = _mesh_place()
    me, sibling = (x, y, c), (x, y, 1 - c)
    chips = [(1 - x, y), (x, 1 - y), (1 - x, 1 - y)]

    def copy(a, k, block, to, src=None):
        dst = outs[a].at[_slot(block)]
        return pltpu.make_async_remote_copy(
            src_ref=dst if src is None else src, dst_ref=dst,
            send_sem=send_sems.at[a, k], recv_sem=recv_sems.at[a, k], device_id=to, device_id_type=MESH)

    def own(a):
        return pltpu.make_async_copy(ins[a], outs[a].at[_slot(me)], local_sems.at[a, 0])

    def start():
        for j, chip in enumerate(chips):
            for a in range(n):
                copy(a, 1 + j, me, (*chip, c), src=ins[a]).start()
        for a in range(n):
            copy(a, 0, me, sibling, src=ins[a]).start()
            own(a).start()

    def relay():
        for j, chip in enumerate(chips):
            for a in range(n):
                copy(a, 1 + j, (*chip, c), me).wait_recv()
                copy(a, 4 + j, (*chip, c), sibling).start()

    def wait():
        for a in range(n):
            copy(a, 0, sibling, me).wait_recv()
        for j, chip in enumerate(chips):
            for a in range(n):
                copy(a, 4 + j, (*chip, 1 - c), me).wait_recv()
        for j, chip in enumerate(chips):
            for a in range(n):
                copy(a, 1 + j, me, (*chip, c), src=ins[a]).wait_send()
                copy(a, 4 + j, (*chip, c), sibling).wait_send()
        for a in range(n):
            copy(a, 0, me, sibling, src=ins[a]).wait_send()
            own(a).wait()

    return start, relay, wait


def _side_out_shapes(kind, arrays):
    if kind in ("gather", "relay_gather"):
        return [jax.ShapeDtypeStruct((N_DEV,) + a.shape, a.dtype) for a in arrays]
    return [jax.ShapeDtypeStruct(a.shape, a.dtype) for a in arrays]


def _hosted_call(body, name, grid, in_specs, out_specs, out_shape, scratch_shapes, args, side=None):
    if side is None:
        outs = pl.pallas_call(
            body, name=name, grid=grid, in_specs=in_specs, out_specs=out_specs, out_shape=out_shape,
            scratch_shapes=scratch_shapes, compiler_params=_params(len(grid)))(*args)
        return outs, []
    kind, arrays = side
    side_shapes = _side_out_shapes(kind, arrays)
    n_in, n_out, n_scr, n_side = len(in_specs), len(out_specs), len(scratch_shapes), len(arrays)

    def hosted(*refs):
        pos = 0
        groups = []
        for size in (n_in, n_side, n_out, len(side_shapes), n_scr):
            groups.append(refs[pos:pos + size])
            pos += size
        ins, side_in, outs, side_out, scr = groups
        send_sems, recv_sems, local_sems = refs[pos:]
        ids = [pl.program_id(d) for d in range(len(grid))]
        is_first = functools.reduce(jnp.logical_and, [i == 0 for i in ids])
        is_last = functools.reduce(jnp.logical_and, [i == g - 1 for i, g in zip(ids, grid)])
        if kind == "relay_gather":
            start, relay, wait = _relay_gather(side_in, side_out, send_sems, recv_sems, local_sems)
        else:
            start, relay, wait = _side_copies(kind, side_in, side_out, send_sems, recv_sems, local_sems)
        pl.when(is_first)(start)
        if relay is not None:
            pl.when(is_last)(relay)
        body(*ins, *outs, *scr)
        pl.when(is_last)(wait)

    any_spec = pl.BlockSpec(memory_space=pl.ANY)
    outs = pl.pallas_call(
        hosted, name=name, grid=grid,
        in_specs=list(in_specs) + [any_spec] * n_side,
        out_specs=list(out_specs) + [any_spec] * len(side_shapes),
        out_shape=list(out_shape) + side_shapes,
        scratch_shapes=list(scratch_shapes) + [pltpu.SemaphoreType.DMA((n_side, 7)), pltpu.SemaphoreType.DMA((n_side, 7)),
                                               pltpu.SemaphoreType.DMA((n_side, N_CHIPS))],
        compiler_params=_params(len(grid)))(*args, *arrays)
    return outs[:n_out], outs[n_out:]


def _lane_chunks(width, chunk=2 * LANES):
    return [slice(n0, min(n0 + chunk, width)) for n0 in range(0, width, chunk)]


def _pipelined(chunks, first, middle, last):
    n = len(chunks)
    a, b, total = {}, {}, None
    for step in range(n + 2):
        if step < n:
            a[step] = first(chunks[step])
        if 0 <= step - 1 < n:
            b[step - 1] = middle(chunks[step - 1], a.pop(step - 1))
        if 0 <= step - 2 < n:
            part = last(chunks[step - 2], b.pop(step - 2))
            total = part if total is None else total + part
    return total


def _ffn_fwd(h, g_pre, g_post, w_gu, w_down, name, side=None, attn=None):
    T, D = h.shape
    nj = w_gu.shape[0] // 2
    FB = w_gu.shape[1]
    tm = TOKEN_TILE
    n_attn = 0 if attn is None else 5

    def body(*refs):
        h_ref, gpre_ref, gpost_ref, wg_ref, wu_ref, wd_ref = refs[:6]
        hout_ref, f_ref, a_ref, gu_ref = refs[6 + n_attn:10 + n_attn]
        a_scr, acc = refs[-2:]
        j = pl.program_id(1)

        @pl.when(j == 0)
        def _():
            if attn is None:
                x = h_ref[...]
            else:
                ma_ref, mb_ref, wo_ref, bo_ref, ga_ref = refs[6:11]
                att_ref, hmid_ref, mix_ref = refs[15:18]
                mix = jnp.concatenate([ma_ref[...], mb_ref[...]], axis=1).astype(BF16)
                mix_ref[...] = mix
                att = _mm(mix, wo_ref[...]) + bo_ref[...]
                att_ref[...] = att
                x = h_ref[...] + att * _rstd(att) * ga_ref[...]
                hmid_ref[...] = x
            a = (x * _rstd(x) * gpre_ref[...]).astype(BF16)
            a_scr[...] = a
            a_ref[...] = a
            acc[...] = jnp.zeros_like(acc)

        a = a_scr[...]
        g = _mm_nt(a, wg_ref[...])
        u = _mm_nt(a, wu_ref[...])
        gu_ref[0] = g.astype(BF16)
        gu_ref[1] = u.astype(BF16)
        hh = (g * jax.nn.sigmoid(g) * u).astype(BF16)
        acc[...] += _mm(hh, wd_ref[...])

        @pl.when(j == nj - 1)
        def _():
            f = acc[...]
            f_ref[...] = f
            x = h_ref[...] if attn is None else refs[16][...]
            hout_ref[...] = x + 0.5 * (f * _rstd(f) * gpost_ref[...])

    tile = pl.BlockSpec((tm, D), lambda i, j: (i, 0))
    in_specs = [tile, _full((1, D)), _full((1, D)),
                pl.BlockSpec((None, FB, D), lambda i, j: (j, 0, 0)),
                pl.BlockSpec((None, FB, D), lambda i, j: (j + nj, 0, 0)),
                pl.BlockSpec((FB, D), lambda i, j: (j, 0))]
    out_specs = [tile, tile, tile, pl.BlockSpec((None, 2, tm, FB), lambda i, j: (j, 0, i, 0))]
    out_shape = [
        jax.ShapeDtypeStruct((T, D), F32),
        jax.ShapeDtypeStruct((T, D), F32),
        jax.ShapeDtypeStruct((T, D), BF16),
        jax.ShapeDtypeStruct((nj, 2, T, FB), BF16),
    ]
    args = [h, g_pre, g_post, w_gu, w_gu, w_down]
    if attn is not None:
        mix_a, mix_b, w_out, b_out, g_attn = attn
        d_mix = w_out.shape[0]
        in_specs += [pl.BlockSpec((tm, mix_a.shape[1]), lambda i, j: (i, 0)),
                     pl.BlockSpec((tm, mix_b.shape[1]), lambda i, j: (i, 0)),
                     _full((d_mix, D)), _full((1, D)), _full((1, D))]
        out_specs += [tile, tile, pl.BlockSpec((tm, d_mix), lambda i, j: (i, 0))]
        out_shape += [jax.ShapeDtypeStruct((T, D), F32),
                      jax.ShapeDtypeStruct((T, D), F32),
                      jax.ShapeDtypeStruct((T, d_mix), BF16)]
        args += [mix_a, mix_b, w_out, b_out, g_attn]
    return _hosted_call(
        body, name, (T // tm, nj), in_specs=in_specs, out_specs=out_specs, out_shape=out_shape,
        scratch_shapes=[pltpu.VMEM((tm, D), BF16), pltpu.VMEM((tm, D), F32)], args=args, side=side)


def _ffn_bwd(dh_out, f, g_post, h, g_pre, gu, w_gu, w_down, name, side=None):
    T, D = h.shape
    nj = w_gu.shape[0] // 2
    FB = w_gu.shape[1]
    tm = TOKEN_TILE

    def body(dho_ref, f_ref, gpost_ref, h_ref, gpre_ref, gu_ref, wg_ref, wu_ref, wd_ref,
             dhin_ref, df_ref, hh_ref, dgu_ref, dgpost_ref, dgpre_ref, df_scr, da):
        i, j = pl.program_id(0), pl.program_id(1)

        @pl.when(jnp.logical_and(i == 0, j == 0))
        def _():
            dgpost_ref[...] = jnp.zeros_like(dgpost_ref)
            dgpre_ref[...] = jnp.zeros_like(dgpre_ref)

        @pl.when(j == 0)
        def _():
            fv = f_ref[...]
            df, dgain = _rms_bwd(fv, _rstd(fv), gpost_ref[...], 0.5 * dho_ref[...])
            dgpost_ref[...] += _colsum(dgain)
            dfb = df.astype(BF16)
            df_scr[...] = dfb
            df_ref[...] = dfb
            da[...] = jnp.zeros_like(da)

        dfb = df_scr[...]

        halves = (slice(0, tm // 2), slice(tm // 2, tm))

        def hidden_grad(c):
            return [_mm_nt(dfb[rows], wd_ref[c, :]) for rows in halves]

        def through_swiglu(c, dhh):
            dhh = jnp.concatenate(dhh, axis=0)
            g = gu_ref[0, :, c].astype(F32)
            u = gu_ref[1, :, c].astype(F32)
            sg = jax.nn.sigmoid(g)
            silu = g * sg
            hh_ref[:, c] = (silu * u).astype(BF16)
            dg = (dhh * u * (sg * (1.0 + (g - silu)))).astype(BF16)
            du = (dhh * silu).astype(BF16)
            dgu_ref[0, :, c] = dg
            dgu_ref[1, :, c] = du
            return dg, du

        def input_grad(c, dgu):
            return jnp.concatenate(
                [_mm(dgu[0][rows], wg_ref[c, :]) + _mm(dgu[1][rows], wu_ref[c, :]) for rows in halves], axis=0)

        da[...] += _pipelined(_lane_chunks(FB), hidden_grad, through_swiglu, input_grad)

        @pl.when(j == nj - 1)
        def _():
            x = h_ref[...]
            dx, dgain = _rms_bwd(x, _rstd(x), gpre_ref[...], da[...])
            dgpre_ref[...] += _colsum(dgain)
            dhin_ref[...] = dho_ref[...] + dx

    tile = pl.BlockSpec((tm, D), lambda i, j: (i, 0))
    return _hosted_call(
        body, name, (T // tm, nj),
        in_specs=[
            tile, tile, _full((1, D)), tile, _full((1, D)),
            pl.BlockSpec((None, 2, tm, FB), lambda i, j: (j, 0, i, 0)),
            pl.BlockSpec((None, FB, D), lambda i, j: (j, 0, 0)),
            pl.BlockSpec((None, FB, D), lambda i, j: (j + nj, 0, 0)),
            pl.BlockSpec((FB, D), lambda i, j: (j, 0)),
        ],
        out_specs=[
            tile, tile,
            pl.BlockSpec((None, tm, FB), lambda i, j: (j, i, 0)),
            pl.BlockSpec((None, 2, tm, FB), lambda i, j: (j, 0, i, 0)),
            _full((1, D)), _full((1, D)),
        ],
        out_shape=[
            jax.ShapeDtypeStruct((T, D), F32),
            jax.ShapeDtypeStruct((T, D), BF16),
            jax.ShapeDtypeStruct((nj, T, FB), BF16),
            jax.ShapeDtypeStruct((nj, 2, T, FB), BF16),
            jax.ShapeDtypeStruct((1, D), F32),
            jax.ShapeDtypeStruct((1, D), F32),
        ],
        scratch_shapes=[pltpu.VMEM((tm, D), BF16), pltpu.VMEM((tm, D), F32)],
        args=(dh_out, f, g_post, h, g_pre, gu, w_gu, w_gu, w_down), side=side)


def _tn_matmul(x, y, x_spec, y_spec, out_shape, out_spec, n_blocks, n_steps, acc_shape, name, side=None):
    def body(x_ref, y_ref, o_ref, acc):
        t = pl.program_id(1)

        @pl.when(t == 0)
        def _():
            acc[...] = jnp.zeros_like(acc)

        acc[...] += _mm_tn(x_ref[...].astype(BF16), y_ref[...].astype(BF16))

        @pl.when(t == n_steps - 1)
        def _():
            o_ref[...] = acc[...].astype(o_ref.dtype)

    outs, side_outs = _hosted_call(
        body, name, (n_blocks, n_steps), in_specs=[x_spec, y_spec], out_specs=[out_spec], out_shape=[out_shape],
        scratch_shapes=[pltpu.VMEM(acc_shape, F32)], args=(x, y), side=side)
    return (outs[0], side_outs) if side is not None else outs[0]


def _inproj_fwd(h, g_pre, w_in, b_in, side=None):
    T, D = h.shape
    tm = TOKEN_TILE

    def body(h_ref, g_ref, w_ref, b_ref, z_ref, a_ref):
        x = h_ref[...]
        a = (x * _rstd(x) * g_ref[...]).astype(BF16)
        a_ref[...] = a
        z_ref[...] = _mm_nt(a, w_ref[...]) + b_ref[...]

    return _hosted_call(
        body, "inproj_fwd", (T // tm,),
        in_specs=[pl.BlockSpec((tm, D), lambda i: (i, 0)), _full((1, D)), _full((D_IN, D)), _full((1, D_IN))],
        out_specs=[pl.BlockSpec((tm, D_IN), lambda i: (i, 0)), pl.BlockSpec((tm, D), lambda i: (i, 0))],
        out_shape=[jax.ShapeDtypeStruct((T, D_IN), F32), jax.ShapeDtypeStruct((T, D), BF16)],
        scratch_shapes=[], args=(h, g_pre, w_in, b_in), side=side)


def _inproj_bwd(dqa, dka, dva, dqb, dkb, dvb, w_in, h, g_pre, dres, side=None):
    T, D = h.shape
    tm = TOKEN_TILE

    def body(dqa_ref, dka_ref, dva_ref, dqb_ref, dkb_ref, dvb_ref, w_ref, h_ref, g_ref, dres_ref,
             dh_ref, dz_ref, dbin_ref, dg_ref):
        i = pl.program_id(0)

        @pl.when(i == 0)
        def _():
            dbin_ref[...] = jnp.zeros_like(dbin_ref)
            dg_ref[...] = jnp.zeros_like(dg_ref)

        dz = jnp.concatenate([dqa_ref[...], dka_ref[...], dva_ref[...], dqb_ref[...], dkb_ref[...], dvb_ref[...]],
                             axis=1)
        dbin_ref[...] += _colsum(dz)
        dzb = dz.astype(BF16)
        dz_ref[...] = dzb
        da = _mm(dzb, w_ref[...])
        x = h_ref[...]
        dx, dgain = _rms_bwd(x, _rstd(x), g_ref[...], da)
        dg_ref[...] += _colsum(dgain)
        dh_ref[...] = dres_ref[...] + dx

    def tile(w):
        return pl.BlockSpec((tm, w), lambda i: (i, 0))

    return _hosted_call(
        body, "inproj_bwd", (T // tm,),
        in_specs=[tile(A_Q), tile(A_KV), tile(A_KV), tile(B_W), tile(B_W), tile(B_W),
                  _full((D_IN, D)), tile(D), _full((1, D)), tile(D)],
        out_specs=[tile(D), tile(D_IN), _full((1, D_IN)), _full((1, D))],
        out_shape=[jax.ShapeDtypeStruct((T, D), F32), jax.ShapeDtypeStruct((T, D_IN), BF16),
                   jax.ShapeDtypeStruct((1, D_IN), F32), jax.ShapeDtypeStruct((1, D), F32)],
        scratch_shapes=[], args=(dqa, dka, dva, dqb, dkb, dvb, w_in, h, g_pre, dres), side=side)


def _bucket_tiles(patterns):
    i = np.arange(QBLK)[:, None]
    j = np.arange(2 * QBLK)[None, :]
    dist = QBLK + i - j
    max_exact = NUM_BUCKETS // 2
    tiles = []
    for dilation, max_dist in patterns:
        n = np.maximum(dist * dilation, 0)
        nf = np.maximum(n, 1).astype(np.float32)
        large = max_exact + (np.log(nf / np.float32(max_exact)) / np.float32(math.log(MAX_DISTANCE / max_exact))
                             * np.float32(NUM_BUCKETS - max_exact)).astype(np.int32)
        bucket = np.where(n < max_exact, n, np.minimum(large, NUM_BUCKETS - 1))
        tiles.append(np.where((dist >= 0) & (dist <= max_dist), bucket, -1))
    return jnp.asarray(np.stack(tiles).astype(np.int32))


def _bias_build(rel_bias, buckets, head0, name, side=None):
    n = buckets.shape[0]

    def body(bk_ref, rb_ref, o_ref):
        bk = bk_ref[...]
        base = jnp.where(bk < 0, NEG_INF, 0.0).astype(F32)
        for hd in range(N_HEAD_GROUP):
            o_ref[hd] = lax.fori_loop(
                0, NUM_BUCKETS, lambda b, acc, hd=hd: jnp.where(bk == b, rb_ref[b, head0 + hd], acc), base)

    outs, side_outs = _hosted_call(
        body, name, (n,),
        in_specs=[pl.BlockSpec((None, QBLK, 2 * QBLK), lambda p: (p, 0, 0)), pl.BlockSpec(memory_space=pltpu.SMEM)],
        out_specs=[pl.BlockSpec((None, N_HEAD_GROUP, QBLK, 2 * QBLK), lambda p: (p, 0, 0, 0))],
        out_shape=[jax.ShapeDtypeStruct((n, N_HEAD_GROUP, QBLK, 2 * QBLK), F32)],
        scratch_shapes=[], args=(buckets, rel_bias), side=side)
    return outs[0], side_outs


def _bias_grad(ds, buckets, name, after=None):
    n = buckets.shape[0]
    extra = [] if after is None else [after]

    def body(ds_ref, bk_ref, *rest):
        o_ref = rest[-1]
        bk = bk_ref[...]
        row = lax.broadcasted_iota(jnp.int32, (NUM_BUCKETS, 2 * QBLK), 0)
        for hd in range(N_HEAD_GROUP):
            d = ds_ref[hd]
            per_key = jnp.zeros((NUM_BUCKETS, 2 * QBLK), F32)
            for b in range(NUM_BUCKETS):
                per_key = jnp.where(row == b, jnp.sum(jnp.where(bk == b, d, 0.0), axis=0, keepdims=True), per_key)
            o_ref[hd] = jnp.broadcast_to(jnp.sum(per_key, axis=1, keepdims=True), (NUM_BUCKETS, LANES))

    out = pl.pallas_call(
        body, name=name, grid=(n,),
        in_specs=[pl.BlockSpec((None, N_HEAD_GROUP, QBLK, 2 * QBLK), lambda p: (p, 0, 0, 0)),
                  pl.BlockSpec((None, QBLK, 2 * QBLK), lambda p: (p, 0, 0))]
        + [pl.BlockSpec(memory_space=pl.ANY)] * len(extra),
        out_specs=pl.BlockSpec((None, N_HEAD_GROUP, NUM_BUCKETS, LANES), lambda p: (p, 0, 0, 0)),
        out_shape=jax.ShapeDtypeStruct((n, N_HEAD_GROUP, NUM_BUCKETS, LANES), F32),
        compiler_params=_params(1),
    )(ds, buckets, *extra)
    return out[:, :, :, 0].reshape(n * N_HEAD_GROUP, NUM_BUCKETS)


def _class_rows(start, dilation):
    if dilation == 1:
        return pl.ds(pl.multiple_of(start, QBLK), QBLK)
    return pl.ds(start, QBLK, stride=dilation)


def _starts_class(u, blocks_per_pass, n_blocks):
    return blocks_per_pass % n_blocks == 0 and u % n_blocks == 0


def _block_starts(idx, n_blocks, dilation):
    cls = idx // n_blocks
    n = idx % n_blocks
    cur = cls + dilation * QBLK * n
    prev = cls + dilation * QBLK * jnp.maximum(n - 1, 0)
    return n, cur, prev


class _HeadPair:
    def __init__(self, g, shared_kv):
        self.lane = lax.broadcasted_iota(jnp.int32, (1, LANES), 1)
        self.lower = self.lane < HEAD_DIM
        self.shared_kv = shared_kv
        self.key_lanes = (self.lane >= HEAD_DIM).astype(jnp.int32) == (g // 2)

    def stack(self, t):
        return jnp.concatenate([jnp.where(self.lower, t, 0.0), jnp.where(self.lower, 0.0, t)], axis=0).astype(BF16)

    def unstack(self, t2):
        return jnp.where(self.lower, t2[:QBLK], t2[QBLK:])

    def keys(self, t):
        if self.shared_kv:
            return jnp.where(self.key_lanes, t, pltpu.roll(t, HEAD_DIM, 1))
        return t

    def key_grads(self, t):
        if self.shared_kv:
            return jnp.where(self.key_lanes, t + pltpu.roll(t, HEAD_DIM, 1), 0.0)
        return t


def _attn_specs(T, qcol, kcol, vcol, shared_kv):
    kv = (lambda c: (lambda g: (0, c))) if shared_kv else (lambda c: (lambda g: (0, c + g)))
    return [pl.BlockSpec((T, LANES), lambda g: (0, qcol + g)),
            pl.BlockSpec((T, LANES), kv(kcol)),
            pl.BlockSpec((T, LANES), kv(vcol))]


def _attn_fwd(z, bias, sinks, patterns, qcol, kcol, vcol, shared_kv, name, side=None):
    T = z.shape[0]
    n_pat = len(patterns)
    has_sink = sinks is not None

    def body(*refs):
        if has_sink:
            sink_ref, refs = refs[0], refs[1:]
        q_ref, k_ref, v_ref, b_ref, o_ref, l_ref = refs[:6]
        po_scr = refs[6:6 + n_pat]
        pl_scr = refs[6 + n_pat:]
        g = pl.program_id(0)
        heads = _HeadPair(g, shared_kv)
        in_prev = lax.broadcasted_iota(jnp.int32, (2 * QBLK, 2 * QBLK), 1) < QBLK

        for pi, (dilation, _) in enumerate(patterns):
            n_blocks = T // (QBLK * dilation)

            def step(it, carry, pi=pi, dilation=dilation, n_blocks=n_blocks):
                blocks = []
                for u in range(FWD_BLOCKS):
                    n, cur, prev = _block_starts(it * FWD_BLOCKS + u, n_blocks, dilation)
                    rows_c, rows_p = _class_rows(cur, dilation), _class_rows(prev, dilation)
                    qm = heads.stack(q_ref[rows_c, :])
                    k_cur, v_cur = k_ref[rows_c, :], v_ref[rows_c, :]
                    no_past = _starts_class(u, FWD_BLOCKS, n_blocks)
                    if no_past:
                        k2, v2 = k_cur, v_cur
                    else:
                        if u % min(FWD_BLOCKS, n_blocks) == 0:
                            k_prev, v_prev = k_ref[rows_p, :], v_ref[rows_p, :]
                        k2 = jnp.concatenate([k_prev, k_cur], axis=0)
                        v2 = jnp.concatenate([v_prev, v_cur], axis=0)
                    k2, v2 = heads.keys(k2).astype(BF16), heads.keys(v2).astype(BF16)
                    k_prev, v_prev = k_cur, v_cur
                    blocks.append(dict(n=n, no_past=no_past, rows=rows_c, v2=v2, s=_mm_nt(qm, k2)))
                for b in blocks:
                    if b["no_past"]:
                        b["s"] = b["s"] * (HEAD_DIM ** -0.5) + b_ref[pi, :, QBLK:]
                    else:
                        s = b["s"] * (HEAD_DIM ** -0.5) + b_ref[pi]
                        b["s"] = jnp.where(jnp.logical_and(in_prev, b["n"] == 0), NEG_INF, s)
                    b["m"] = jnp.max(b["s"], axis=1, keepdims=True)
                for b in blocks:
                    b["pr"] = jnp.exp(b["s"] - b["m"])
                    b["den"] = jnp.sum(b["pr"], axis=1, keepdims=True)
                for b in blocks:
                    b["o2"] = _mm(b["pr"].astype(BF16), b["v2"])
                for b in blocks:
                    lse = b["m"] + jnp.log(b["den"])
                    po_scr[pi][b["rows"], :] = heads.unstack(b["o2"] / b["den"])
                    pl_scr[2 * pi][b["rows"], :] = jnp.broadcast_to(lse[:QBLK], (QBLK, LANES))
                    pl_scr[2 * pi + 1][b["rows"], :] = jnp.broadcast_to(lse[QBLK:], (QBLK, LANES))
                return carry

            lax.fori_loop(0, (dilation * n_blocks) // FWD_BLOCKS, step, 0)

        def merge(ci, carry):
            rows = pl.ds(pl.multiple_of(ci * QBLK, QBLK), QBLK)
            weights = []
            for hd in range(2):
                parts = [pl_scr[2 * pi + hd][rows, :] for pi in range(n_pat)]
                m = functools.reduce(jnp.maximum, parts)
                if has_sink:
                    sink = sink_ref[0, 2 * g + hd]
                    m = jnp.maximum(m, sink)
                terms = [jnp.exp(x - m) for x in parts]
                den = functools.reduce(jnp.add, terms)
                if has_sink:
                    den = den + jnp.exp(sink - m)
                l_ref[hd, rows, :] = m + jnp.log(den)
                inv = 1.0 / den
                weights.append([t * inv for t in terms])
            o_ref[rows, :] = functools.reduce(
                jnp.add, [jnp.where(heads.lower, weights[0][pi], weights[1][pi]) * po_scr[pi][rows, :]
                          for pi in range(n_pat)])
            return carry

        lax.fori_loop(0, T // QBLK, merge, 0)

    in_specs = _attn_specs(T, qcol, kcol, vcol, shared_kv)
    in_specs.append(pl.BlockSpec((n_pat, None, 2 * QBLK, 2 * QBLK), lambda g: (0, g, 0, 0)))
    args = [z, z, z, bias.reshape(n_pat, N_HEAD_GROUP // 2, 2 * QBLK, 2 * QBLK)]
    if has_sink:
        in_specs.insert(0, pl.BlockSpec(memory_space=pltpu.SMEM))
        args.insert(0, sinks)
    return _hosted_call(
        body, name, (N_HEAD_GROUP // 2,),
        in_specs=in_specs,
        out_specs=[pl.BlockSpec((T, LANES), lambda g: (0, g)), pl.BlockSpec((2, T, LANES), lambda g: (g, 0, 0))],
        out_shape=[jax.ShapeDtypeStruct((T, N_HEAD_GROUP * HEAD_DIM), F32),
                   jax.ShapeDtypeStruct((N_HEAD_GROUP, T, LANES), F32)],
        scratch_shapes=[pltpu.VMEM((T, LANES), F32)] * (3 * n_pat), args=args, side=side)


def _attn_bwd(z, bias, sinks, d_out, out, lse, patterns, qcol, kcol, vcol, shared_kv, name, side=None):
    T = z.shape[0]
    n_pat = len(patterns)
    has_sink = sinks is not None
    kv_width = LANES if shared_kv else N_HEAD_GROUP * HEAD_DIM

    def body(*refs):
        if has_sink:
            sink_ref, refs = refs[0], refs[1:]
        q_ref, k_ref, v_ref, b_ref, do_ref, o_ref, l0_ref, l1_ref = refs[:8]
        dq_ref, dk_ref, dv_ref, ds_ref = refs[8:12]
        dsink_ref = refs[12] if has_sink else None
        dk_acc, dv_acc = refs[-2:]
        g = pl.program_id(0)
        heads = _HeadPair(g, shared_kv)
        in_prev = lax.broadcasted_iota(jnp.int32, (2 * QBLK, 2 * QBLK), 1) < QBLK

        dq_ref[...] = jnp.zeros_like(dq_ref)
        ds_ref[...] = jnp.zeros_like(ds_ref)
        dk_acc[...] = jnp.zeros_like(dk_acc)
        dv_acc[...] = jnp.zeros_like(dv_acc)

        dsink = jnp.zeros((1, LANES), F32)
        for pi, (dilation, _) in enumerate(patterns):
            n_blocks = T // (QBLK * dilation)

            def step(idx, dsink, pi=pi, dilation=dilation, n_blocks=n_blocks):
                blocks = []
                for u in range(BWD_BLOCKS):
                    n, cur, prev = _block_starts(idx * BWD_BLOCKS + u, n_blocks, dilation)
                    rows_c, rows_p = _class_rows(cur, dilation), _class_rows(prev, dilation)
                    qm = heads.stack(q_ref[rows_c, :])
                    k_cur, v_cur = k_ref[rows_c, :], v_ref[rows_c, :]
                    first = u % min(BWD_BLOCKS, n_blocks) == 0
                    no_past = _starts_class(u, BWD_BLOCKS, n_blocks)
                    if no_past:
                        k2, v2 = k_cur, v_cur
                    else:
                        if first:
                            k_prev, v_prev = k_ref[rows_p, :], v_ref[rows_p, :]
                        k2 = jnp.concatenate([k_prev, k_cur], axis=0)
                        v2 = jnp.concatenate([v_prev, v_cur], axis=0)
                    k2, v2 = heads.keys(k2).astype(BF16), heads.keys(v2).astype(BF16)
                    k_prev, v_prev = k_cur, v_cur
                    d_o = do_ref[rows_c, :]
                    dom = heads.stack(d_o)
                    dd = d_o * o_ref[rows_c, :]
                    delta = jnp.concatenate([jnp.sum(jnp.where(heads.lower, dd, 0.0), axis=1, keepdims=True),
                                             jnp.sum(jnp.where(heads.lower, 0.0, dd), axis=1, keepdims=True)], axis=0)
                    lse = jnp.concatenate([l0_ref[rows_c, :], l1_ref[rows_c, :]], axis=0)
                    blocks.append(dict(n=n, first=first, no_past=no_past, rows_c=rows_c, rows_p=rows_p, qm=qm, k2=k2,
                                       dom=dom, delta=delta, lse=lse, s=_mm_nt(qm, k2), dp=_mm_nt(dom, v2)))
                for b in blocks:
                    if b["no_past"]:
                        s = b["s"] * (HEAD_DIM ** -0.5) + b_ref[pi, :, QBLK:]
                        b["pr"] = jnp.exp(s - b["lse"])
                    else:
                        s = b["s"] * (HEAD_DIM ** -0.5) + b_ref[pi]
                        s = jnp.where(jnp.logical_and(in_prev, b["n"] == 0), NEG_INF, s)
                        b["pr"] = jnp.exp(s - jnp.concatenate([b["lse"], b["lse"]], axis=1))
                    b["ds"] = b["pr"] * (b["dp"] - b["delta"])
                for b in blocks:
                    dsb = b["ds"].astype(BF16)
                    b["dq2"] = _mm(dsb, b["k2"])
                    b["dk2"] = _mm_tn(dsb, b["qm"])
                    b["dv2"] = _mm_tn(b["pr"].astype(BF16), b["dom"])
                for b in blocks:
                    b["dk2"] = heads.key_grads(b["dk2"]) * (HEAD_DIM ** -0.5)
                    b["dv2"] = heads.key_grads(b["dv2"])
                for u, b in enumerate(blocks):
                    dq_ref[b["rows_c"], :] += heads.unstack(b["dq2"]) * (HEAD_DIM ** -0.5)
                    if b["no_past"]:
                        ds_ref[pi, :, QBLK:] += b["ds"]
                        dk_own, dv_own = b["dk2"], b["dv2"]
                    else:
                        ds_ref[pi] += b["ds"]
                        dk_own, dv_own = b["dk2"][QBLK:], b["dv2"][QBLK:]
                    if u + 1 < len(blocks) and not blocks[u + 1]["first"]:
                        dk_own = dk_own + blocks[u + 1]["dk2"][:QBLK]
                        dv_own = dv_own + blocks[u + 1]["dv2"][:QBLK]
                    if b["first"] and not b["no_past"]:
                        dk_acc[b["rows_p"], :] += b["dk2"][:QBLK]
                        dv_acc[b["rows_p"], :] += b["dv2"][:QBLK]
                    dk_acc[b["rows_c"], :] += dk_own
                    dv_acc[b["rows_c"], :] += dv_own
                    if has_sink:
                        for hd in range(2):
                            rows_h = slice(QBLK * hd, QBLK * (hd + 1))
                            p_sink = jnp.exp(sink_ref[0, 2 * g + hd] - b["lse"][rows_h, 0:1])
                            dsink = dsink - jnp.where(heads.lane == 2 * g + hd,
                                                      jnp.sum(p_sink * b["delta"][rows_h]), 0.0)
                return dsink

            dsink = lax.fori_loop(0, (dilation * n_blocks) // BWD_BLOCKS, step, dsink)

        if shared_kv:
            @pl.when(g == 0)
            def _():
                dk_ref[...] = dk_acc[...]
                dv_ref[...] = dv_acc[...]

            @pl.when(g != 0)
            def _():
                dk_ref[...] += dk_acc[...]
                dv_ref[...] += dv_acc[...]
        else:
            dk_ref[...] = dk_acc[...]
            dv_ref[...] = dv_acc[...]

        if has_sink:
            @pl.when(g == 0)
            def _():
                dsink_ref[...] = dsink

            @pl.when(g != 0)
            def _():
                dsink_ref[...] += dsink

    pair = pl.BlockSpec((T, LANES), lambda g: (0, g))
    stacked = pl.BlockSpec((n_pat, None, 2 * QBLK, 2 * QBLK), lambda g: (0, g, 0, 0))
    stacked_shape = (n_pat, N_HEAD_GROUP // 2, 2 * QBLK, 2 * QBLK)
    in_specs = _attn_specs(T, qcol, kcol, vcol, shared_kv)
    in_specs += [stacked, pair, pair,
                 pl.BlockSpec((None, T, LANES), lambda g: (2 * g, 0, 0)),
                 pl.BlockSpec((None, T, LANES), lambda g: (2 * g + 1, 0, 0))]
    args = [z, z, z, bias.reshape(stacked_shape), d_out, out, lse, lse]
    kv_out = _full((T, LANES)) if shared_kv else pair
    out_specs = [pair, kv_out, kv_out, stacked]
    out_shape = [jax.ShapeDtypeStruct((T, N_HEAD_GROUP * HEAD_DIM), F32),
                 jax.ShapeDtypeStruct((T, kv_width), F32), jax.ShapeDtypeStruct((T, kv_width), F32),
                 jax.ShapeDtypeStruct(stacked_shape, F32)]
    if has_sink:
        in_specs.insert(0, pl.BlockSpec(memory_space=pltpu.SMEM))
        args.insert(0, sinks)
        out_specs.append(_full((1, LANES)))
        out_shape.append(jax.ShapeDtypeStruct((1, LANES), F32))
    outs, side_outs = _hosted_call(
        body, name, (N_HEAD_GROUP // 2,), in_specs=in_specs, out_specs=out_specs, out_shape=out_shape,
        scratch_shapes=[pltpu.VMEM((T, LANES), F32), pltpu.VMEM((T, LANES), F32)], args=args, side=side)
    outs = list(outs)
    outs[3] = outs[3].reshape(n_pat, N_HEAD_GROUP, QBLK, 2 * QBLK)
    return outs, side_outs


def _outproj_bwd(dh, att, g_post, w_out):
    T, D = dh.shape
    tm = TOKEN_TILE
    d_mix = w_out.shape[0]

    def body(dh_ref, att_ref, g_ref, w_ref, dma_ref, dmb_ref, datt_ref, dg_ref, db_ref):
        i = pl.program_id(0)

        @pl.when(i == 0)
        def _():
            dg_ref[...] = jnp.zeros_like(dg_ref)
            db_ref[...] = jnp.zeros_like(db_ref)

        att = att_ref[...]
        datt, dgain = _rms_bwd(att, _rstd(att), g_ref[...], dh_ref[...])
        dg_ref[...] += _colsum(dgain)
        db_ref[...] += _colsum(datt)
        dattb = datt.astype(BF16)
        datt_ref[...] = dattb
        dmix = _mm_nt(dattb, w_ref[...])
        dma_ref[...] = dmix[:, :A_Q]
        dmb_ref[...] = dmix[:, A_Q:]

    def tile(w):
        return pl.BlockSpec((tm, w), lambda i: (i, 0))

    return pl.pallas_call(
        body, name="outproj_bwd", grid=(T // tm,),
        in_specs=[tile(D), tile(D), _full((1, D)), _full((d_mix, D))],
        out_specs=[tile(A_Q), tile(B_W), tile(D), _full((1, D)), _full((1, D))],
        out_shape=[jax.ShapeDtypeStruct((T, A_Q), F32), jax.ShapeDtypeStruct((T, B_W), F32),
                   jax.ShapeDtypeStruct((T, D), BF16), jax.ShapeDtypeStruct((1, D), F32),
                   jax.ShapeDtypeStruct((1, D), F32)],
        compiler_params=_params(1),
    )(dh, att, g_post, w_out)


def _ple_fwd_bwd(h, g_pre, w_gate, p, w_proj, g_post, target):
    T, D = h.shape
    tm = TOKEN_TILE
    n_proj, ple, db = w_proj.shape

    def body(h_ref, gpre_ref, wg_ref, p_ref, wp_ref, gpost_ref, t_ref,
             a_ref, dpre_ref, de_ref, dh_ref, loss_ref, dgpost_ref, dgpre_ref):
        i = pl.program_id(0)

        @pl.when(i == 0)
        def _():
            loss_ref[...] = jnp.zeros_like(loss_ref)
            dgpost_ref[...] = jnp.zeros_like(dgpost_ref)
            dgpre_ref[...] = jnp.zeros_like(dgpre_ref)

        x = h_ref[...]
        rx = _rstd(x)
        a = (x * rx * gpre_ref[...]).astype(BF16)
        a_ref[...] = a
        gate = jax.nn.sigmoid(_mm(a, wg_ref[...]))
        pb = p_ref[...].astype(BF16)
        e = jnp.concatenate([_mm(pb, wp_ref[k]) for k in range(n_proj)], axis=1)
        ge = gate * e
        rg = _rstd(ge)
        diff = x + ge * rg * gpost_ref[...] - t_ref[...]
        loss_ref[...] += 0.5 * jnp.sum(jnp.mean(diff * diff, axis=1, keepdims=True))
        dy = diff * (1.0 / D)
        dge, dgain = _rms_bwd(ge, rg, gpost_ref[...], dy)
        dgpost_ref[...] += _colsum(dgain)
        de_ref[...] = (dge * gate).astype(BF16)
        dpre = (dge * e * gate * (1.0 - gate)).astype(BF16)
        dpre_ref[...] = dpre
        dx, dgain = _rms_bwd(x, rx, gpre_ref[...], _mm_nt(dpre, wg_ref[...]))
        dgpre_ref[...] += _colsum(dgain)
        dh_ref[...] = dy + dx

    def tile(w):
        return pl.BlockSpec((tm, w), lambda i: (i, 0))

    return pl.pallas_call(
        body, name="ple_fwd_bwd", grid=(T // tm,),
        in_specs=[tile(D), _full((1, D)), _full((D, D)), tile(ple), _full((n_proj, ple, db)), _full((1, D)), tile(D)],
        out_specs=[tile(D), tile(D), tile(D), tile(D), _full((1, LANES)), _full((1, D)), _full((1, D))],
        out_shape=[jax.ShapeDtypeStruct((T, D), BF16),
                   jax.ShapeDtypeStruct((T, D), BF16),
                   jax.ShapeDtypeStruct((T, D), BF16),
                   jax.ShapeDtypeStruct((T, D), F32),
                   jax.ShapeDtypeStruct((1, LANES), F32),
                   jax.ShapeDtypeStruct((1, D), F32),
                   jax.ShapeDtypeStruct((1, D), F32)],
        compiler_params=_params(1),
    )(h, g_pre, w_gate, p, w_proj, g_post, target)


def _ple_dw_proj(p, de, n_proj):
    T, ple = p.shape
    D = de.shape[1]
    db = D // n_proj
    tk = TOKEN_TILE
    nt = T // tk

    def body(p_ref, de_ref, o_ref, acc):
        t = pl.program_id(0)

        @pl.when(t == 0)
        def _():
            acc[...] = jnp.zeros_like(acc)

        acc[...] += _mm_tn(p_ref[...].astype(BF16), de_ref[...])

        @pl.when(t == nt - 1)
        def _():
            for k in range(n_proj):
                o_ref[k] = acc[:, k * db:(k + 1) * db].astype(BF16)

    return pl.pallas_call(
        body, name="ple_dw_proj", grid=(nt,),
        in_specs=[pl.BlockSpec((tk, ple), lambda t: (t, 0)), pl.BlockSpec((tk, D), lambda t: (t, 0))],
        out_specs=_full((n_proj, ple, db)), out_shape=jax.ShapeDtypeStruct((n_proj, ple, db), BF16),
        scratch_shapes=[pltpu.VMEM((ple, D), F32)], compiler_params=_params(1),
    )(p, de)


def _tok(width):
    return pl.BlockSpec((DW_TILE, width), lambda b, t: (t, 0))


def _dw_gu(a, dgu, name, side=None):
    T, D = a.shape
    nj, _, _, FB = dgu.shape
    return _tn_matmul(
        dgu, a, pl.BlockSpec((None, None, DW_TILE, FB), lambda b, t: (b % nj, b // nj, t, 0)), _tok(D),
        jax.ShapeDtypeStruct((2 * nj, FB, D), BF16), pl.BlockSpec((None, FB, D), lambda b, t: (b, 0, 0)),
        2 * nj, T // DW_TILE, (FB, D), name, side=side)


def _dw_down(hh, df, name, side=None):
    nj, T, FB = hh.shape
    D = df.shape[1]
    return _tn_matmul(
        hh, df, pl.BlockSpec((None, DW_TILE, FB), lambda b, t: (b, t, 0)), _tok(D),
        jax.ShapeDtypeStruct((nj, FB, D), BF16), pl.BlockSpec((None, FB, D), lambda b, t: (b, 0, 0)),
        nj, T // DW_TILE, (FB, D), name, side=side)


def _dw_rows(xm, y, name):
    T, k = xm.shape
    D = y.shape[1]
    out = _tn_matmul(
        xm, y, _tok(k), _tok(D), jax.ShapeDtypeStruct((k, D), BF16), _full((k, D)),
        1, T // DW_TILE, (k, D), name)
    return out.reshape(N_DEV, k // N_DEV, D)


def _cast_bf16(arrays):
    n = len(arrays)

    def body(*refs):
        for a in range(n):
            refs[n + a][...] = refs[a][...].astype(BF16)

    return pl.pallas_call(
        body, name="cast_shards",
        in_specs=[pl.BlockSpec(memory_space=pltpu.VMEM)] * n, out_specs=[pl.BlockSpec(memory_space=pltpu.VMEM)] * n,
        out_shape=[jax.ShapeDtypeStruct(a.shape, BF16) for a in arrays],
        compiler_params=pltpu.CompilerParams(vmem_limit_bytes=VMEM_LIMIT),
    )(*arrays)


def _pack_layout(D, n_rel_rows):
    n_bin = -(-D_IN // D)
    row_bin = len(GAINS)
    row_sink = row_bin + n_bin
    row_loss = row_sink + 1
    row_rb = -(-(row_loss + 1) // 8) * 8
    n_rows = row_rb + -(-n_rel_rows // 8) * 8
    bin_parts = [(r, min(D, D_IN - r * D)) for r in range(n_bin)]
    return row_bin, row_sink, row_loss, row_rb, n_rows, bin_parts


def _pair_swap_call(grad_blocks):
    def body(g_in, received, send_sems, recv_sems):
        start, _, wait = _pair_swap([g_in], [received], send_sems, recv_sems)
        start()
        wait()

    any_spec = pl.BlockSpec(memory_space=pl.ANY)
    return pl.pallas_call(
        body, name="pair_swap", in_specs=[any_spec], out_specs=any_spec,
        out_shape=jax.ShapeDtypeStruct((N_CHIPS,) + grad_blocks.shape[1:], grad_blocks.dtype),
        scratch_shapes=[pltpu.SemaphoreType.DMA((1, N_CHIPS)), pltpu.SemaphoreType.DMA((1, N_CHIPS))],
    )(grad_blocks)


def _pair_add(blocks, received, name):
    n, R, C = received.shape
    rows = _adamw_rows(R)
    core = lax.axis_index("c").astype(jnp.int32).reshape(1)

    def body(core_ref, a_ref, b_ref, o_ref):
        o_ref[...] = (a_ref[...].astype(F32) + b_ref[...].astype(F32)).astype(o_ref.dtype)

    tile = pl.BlockSpec((None, rows, C), lambda q, r, core_ref: (q, r, 0))
    return pl.pallas_call(
        body, name=name,
        grid_spec=pltpu.PrefetchScalarGridSpec(
            num_scalar_prefetch=1, grid=(n, R // rows),
            in_specs=[pl.BlockSpec((None, rows, C), lambda q, r, core_ref: (2 * q + core_ref[0], r, 0)), tile],
            out_specs=tile),
        out_shape=jax.ShapeDtypeStruct(received.shape, received.dtype), compiler_params=_params(2),
    )(core, blocks, received)


def _final_exchange(partials, loss):
    D = partials["ffn1_pre_g"].shape[1]
    rb_shape = partials["rel_bias"].shape
    row_bin, row_sink, row_loss, row_rb, n_rows, bin_parts = _pack_layout(D, rb_shape[0])
    n_small = len(SMALL)

    def body(*refs):
        part = dict(zip(SMALL, refs[:n_small]))
        loss_ref = refs[n_small]
        gath, pack, send_sems, recv_sems, local_sems = refs[1 + n_small:]

        pack[...] = jnp.zeros_like(pack)
        for i, name in enumerate(GAINS):
            pack[i:i + 1, :] = part[name][...]
        for r, width in bin_parts:
            pack[row_bin + r:row_bin + r + 1, 0:width] = part["b_in"][:, r * D:r * D + width]
        pack[row_sink:row_sink + 1, 0:LANES] = part["sinks"][...]
        pack[row_loss:row_loss + 1, 0:LANES] = loss_ref[...]
        pack[row_rb:row_rb + rb_shape[0], 0:rb_shape[1]] = part["rel_bias"][...]

        small_start, _, small_wait = _side_copies("gather", [pack], [gath], send_sems, recv_sems, local_sems, sem_row=0)
        small_start()
        small_wait()

    args = [partials[k] for k in SMALL] + [loss]
    vmem = pl.BlockSpec(memory_space=pltpu.VMEM)
    return pl.pallas_call(
        body, name="final_exchange",
        in_specs=[vmem] * (n_small + 1),
        out_specs=pl.BlockSpec(memory_space=pl.ANY),
        out_shape=jax.ShapeDtypeStruct((N_DEV, n_rows, D), F32),
        scratch_shapes=[pltpu.VMEM((n_rows, D), F32), pltpu.SemaphoreType.DMA((1, 7)),
                        pltpu.SemaphoreType.DMA((1, 7)), pltpu.SemaphoreType.DMA((1, N_CHIPS))],
    )(*args)


def _adamw(w, g, m, v):
    m = ADAM_B1 * m + (1.0 - ADAM_B1) * g
    v = ADAM_B2 * v + (1.0 - ADAM_B2) * (g * g)
    m_hat = m / (1.0 - ADAM_B1 ** ADAM_STEP)
    v_hat = v / (1.0 - ADAM_B2 ** ADAM_STEP)
    return -ADAM_LR * (m_hat / (jnp.sqrt(v_hat) + ADAM_EPS) + ADAM_WD * w), m, v


def _pair_start(blocks):
    def body(blocks_ref, recv_ref, send_sems, recv_sems, blocks_thru, recv_thru, token):
        x, y, c = _mesh_place()
        for q in range(N_CHIPS):
            pltpu.make_async_remote_copy(
                src_ref=blocks_ref.at[2 * q + (1 - c)], dst_ref=recv_ref.at[q], send_sem=send_sems.at[q],
                recv_sem=recv_sems.at[q], device_id=(x, y, 1 - c), device_id_type=MESH).start()
        token[...] = jnp.zeros_like(token)

    hbm = pl.BlockSpec(memory_space=pltpu.HBM)
    sem = pl.BlockSpec(memory_space=pltpu.SEMAPHORE)
    received = (N_CHIPS,) + blocks.shape[1:]
    return pl.pallas_call(
        body, name="pair_start",
        out_shape=(pltpu.SemaphoreType.DMA((N_CHIPS,)), pltpu.SemaphoreType.DMA((N_CHIPS,)),
                   pltpu.HBM(blocks.shape, blocks.dtype), pltpu.HBM(received, blocks.dtype),
                   jax.ShapeDtypeStruct((8, LANES), F32)),
        in_specs=(hbm, hbm), out_specs=(sem, sem, hbm, hbm, pl.BlockSpec(memory_space=pltpu.VMEM)),
        input_output_aliases={0: 2, 1: 3},
        compiler_params=pltpu.CompilerParams(has_side_effects=pltpu.SideEffectType.DATAFLOW_SIDE_EFFECTING),
    )(pltpu.with_memory_space_constraint(blocks, pltpu.HBM),
      pltpu.with_memory_space_constraint(lax.empty(received, blocks.dtype), pltpu.HBM))


def _pair_wait(send_sems, recv_sems, blocks_thru, recv_thru, after):
    def body(blocks_ref, recv_ref, send_sems, recv_sems, after_ref, blocks_out, recv_out):
        x, y, c = _mesh_place()
        for q in range(N_CHIPS):
            copy = pltpu.make_async_remote_copy(
                src_ref=blocks_ref.at[2 * q + (1 - c)], dst_ref=recv_ref.at[q], send_sem=send_sems.at[q],
                recv_sem=recv_sems.at[q], device_id=(x, y, 1 - c), device_id_type=MESH)
            copy.wait_send()
            copy.wait_recv()

    hbm = pl.BlockSpec(memory_space=pltpu.HBM)
    sem = pl.BlockSpec(memory_space=pltpu.SEMAPHORE)
    return pl.pallas_call(
        body, name="pair_wait",
        out_shape=(pltpu.HBM(blocks_thru.shape, blocks_thru.dtype), pltpu.HBM(recv_thru.shape, recv_thru.dtype)),
        in_specs=(hbm, hbm, sem, sem, pl.BlockSpec(memory_space=pl.ANY)), out_specs=(hbm, hbm),
        input_output_aliases={0: 0, 1: 1},
        compiler_params=pltpu.CompilerParams(has_side_effects=pltpu.SideEffectType.DATAFLOW_SIDE_EFFECTING),
    )(blocks_thru, recv_thru, send_sems, recv_sems, after)


def _quad_start(pairs):
    def body(pairs_ref, land_ref, send_sems, recv_sems, pairs_thru, land_thru, token):
        x, y, c = _mesh_place()
        mine = 2 * x + y
        for k, chip in enumerate([(1 - x, y), (x, 1 - y), (1 - x, 1 - y)]):
            pltpu.make_async_remote_copy(
                src_ref=pairs_ref.at[2 * chip[0] + chip[1]], dst_ref=land_ref.at[mine],
                send_sem=send_sems.at[k], recv_sem=recv_sems.at[k],
                device_id=(chip[0], chip[1], c), device_id_type=MESH).start()
        token[...] = jnp.zeros_like(token)

    hbm = pl.BlockSpec(memory_space=pltpu.HBM)
    sem = pl.BlockSpec(memory_space=pltpu.SEMAPHORE)
    return pl.pallas_call(
        body, name="quad_start",
        out_shape=(pltpu.SemaphoreType.DMA((3,)), pltpu.SemaphoreType.DMA((3,)),
                   pltpu.HBM(pairs.shape, pairs.dtype), pltpu.HBM(pairs.shape, pairs.dtype),
                   jax.ShapeDtypeStruct((8, LANES), F32)),
        in_specs=(hbm, hbm), out_specs=(sem, sem, hbm, hbm, pl.BlockSpec(memory_space=pltpu.VMEM)),
        input_output_aliases={0: 2, 1: 3},
        compiler_params=pltpu.CompilerParams(has_side_effects=pltpu.SideEffectType.DATAFLOW_SIDE_EFFECTING),
    )(pltpu.with_memory_space_constraint(pairs, pltpu.HBM),
      pltpu.with_memory_space_constraint(jnp.copy(pairs), pltpu.HBM))


def _quad_wait(send_sems, recv_sems, pairs_thru, land_thru, after):
    def body(pairs_ref, land_ref, send_sems, recv_sems, after_ref, pairs_dead, land_out):
        x, y, c = _mesh_place()
        mine = 2 * x + y
        for k, chip in enumerate([(1 - x, y), (x, 1 - y), (1 - x, 1 - y)]):
            q = 2 * chip[0] + chip[1]
            pltpu.make_async_remote_copy(
                src_ref=pairs_ref.at[q], dst_ref=land_ref.at[mine], send_sem=send_sems.at[k],
                recv_sem=recv_sems.at[k], device_id=(chip[0], chip[1], c), device_id_type=MESH).wait_send()
            pltpu.make_async_remote_copy(
                src_ref=pairs_ref.at[q], dst_ref=land_ref.at[q], send_sem=send_sems.at[k],
                recv_sem=recv_sems.at[k], device_id=(chip[0], chip[1], c), device_id_type=MESH).wait_recv()

    hbm = pl.BlockSpec(memory_space=pltpu.HBM)
    sem = pl.BlockSpec(memory_space=pltpu.SEMAPHORE)
    return pl.pallas_call(
        body, name="quad_wait",
        out_shape=(pltpu.HBM(pairs_thru.shape, pairs_thru.dtype), pltpu.HBM(land_thru.shape, land_thru.dtype)),
        in_specs=(hbm, hbm, sem, sem, pl.BlockSpec(memory_space=pl.ANY)), out_specs=(hbm, hbm),
        input_output_aliases={0: 0, 1: 1},
        compiler_params=pltpu.CompilerParams(has_side_effects=pltpu.SideEffectType.DATAFLOW_SIDE_EFFECTING),
    )(pairs_thru, land_thru, send_sems, recv_sems, after)[1]


def _sum_adamw(partials, w, m, v, rows, name, after=None):
    R, C = w.shape
    n = partials.shape[0]

    def body(p_ref, w_ref, m_ref, v_ref, *rest):
        g_ref, d_ref, nm_ref, nv_ref = rest[-4:]
        g = p_ref[0].astype(F32)
        for k in range(1, n):
            g = g + p_ref[k].astype(F32)
        g_ref[...] = g
        d_ref[...], nm_ref[...], nv_ref[...] = _adamw(w_ref[...], g, m_ref[...], v_ref[...])

    tile = pl.BlockSpec((rows, C), lambda i: (i, 0))
    extra = [] if after is None else [after]
    return pl.pallas_call(
        body, name=name, grid=(R // rows,),
        in_specs=[pl.BlockSpec((n, rows, C), lambda i: (0, i, 0)), tile, tile, tile]
        + [pl.BlockSpec(memory_space=pl.ANY)] * len(extra),
        out_specs=[tile] * 4, out_shape=[jax.ShapeDtypeStruct((R, C), F32)] * 4,
        compiler_params=_params(1),
    )(partials, w, m, v, *extra)


def _small_adamw(gathered, ws, ms, vs):
    D = ws["ffn1_pre_g"].shape[1]
    n_sink = ws["sinks"].shape[1]
    rb_shape = ws["rel_bias"].shape
    row_bin, row_sink, row_loss, row_rb, n_rows, bin_parts = _pack_layout(D, rb_shape[0])
    n_small = len(SMALL)

    def body(*refs):
        gath = refs[0]
        pos = 1
        w_ref = dict(zip(SMALL, refs[pos:pos + n_small]))
        m_ref = dict(zip(SMALL, refs[pos + n_small:pos + 2 * n_small]))
        v_ref = dict(zip(SMALL, refs[pos + 2 * n_small:pos + 3 * n_small]))
        pos += 3 * n_small
        outs = {name: refs[pos + 4 * i:pos + 4 * i + 4] for i, name in enumerate(SMALL)}
        loss_out = refs[pos + 4 * n_small]
        pack = refs[pos + 4 * n_small + 1]

        total = gath[0]
        for k in range(1, N_DEV):
            total = total + gath[k]
        pack[...] = total

        def update(name, g):
            g_out, d_out, m_out, v_out = outs[name]
            g_out[...] = g
            d_out[...], m_out[...], v_out[...] = _adamw(w_ref[name][...], g, m_ref[name][...], v_ref[name][...])

        for i, name in enumerate(GAINS):
            update(name, pack[i:i + 1, :])
        update("b_in", jnp.concatenate([pack[row_bin + r:row_bin + r + 1, 0:width] for r, width in bin_parts], axis=1))
        update("sinks", pack[row_sink:row_sink + 1, 0:n_sink])
        update("rel_bias", pack[row_rb:row_rb + rb_shape[0], 0:rb_shape[1]])
        loss_out[...] = pack[row_loss:row_loss + 1, 0:LANES]

    args = [gathered]
    for group in (ws, ms, vs):
        args += [group[k] for k in SMALL]
    out_shape = []
    for name in SMALL:
        out_shape += [jax.ShapeDtypeStruct(ws[name].shape, F32)] * 4
    out_shape.append(jax.ShapeDtypeStruct((1, LANES), F32))
    res = pl.pallas_call(
        body, name="small_adamw",
        in_specs=[pl.BlockSpec(memory_space=pltpu.VMEM)] * len(args),
        out_specs=[pl.BlockSpec(memory_space=pltpu.VMEM)] * len(out_shape),
        out_shape=out_shape,
        scratch_shapes=[pltpu.VMEM((n_rows, D), F32)],
    )(*args)
    per_name = {name: res[4 * i:4 * i + 4] for i, name in enumerate(SMALL)}
    return per_name, res[-1]


COLUMN_SHARDED = ("ffn1_w_gu", "ffn2_w_gu", "w_in")


def _adamw_rows(rows_total):
    return max(r for r in range(16, min(rows_total, 256) + 1, 16) if rows_total % r == 0)


def kernel(x, p, rel_bias, ffn1_pre_g, ffn1_w_gu, ffn1_w_down, ffn1_post_g, attn_pre_g, w_in, b_in, sinks, w_out, b_out, attn_post_g, ffn2_pre_g, ffn2_w_gu, ffn2_w_down, ffn2_post_g, ple_pre_g, w_ple_gate, w_ple_proj, ple_post_g, loss_target, m_rel_bias, m_ffn1_pre_g, m_ffn1_w_gu, m_ffn1_w_down, m_ffn1_post_g, m_attn_pre_g, m_w_in, m_b_in, m_sinks, m_w_out, m_b_out, m_attn_post_g, m_ffn2_pre_g, m_ffn2_w_gu, m_ffn2_w_down, m_ffn2_post_g, m_ple_pre_g, m_w_ple_gate, m_w_ple_proj, m_ple_post_g, v_rel_bias, v_ffn1_pre_g, v_ffn1_w_gu, v_ffn1_w_down, v_ffn1_post_g, v_attn_pre_g, v_w_in, v_b_in, v_sinks, v_w_out, v_b_out, v_attn_post_g, v_ffn2_pre_g, v_ffn2_w_gu, v_ffn2_w_down, v_ffn2_post_g, v_ple_pre_g, v_w_ple_gate, v_w_ple_proj, v_ple_post_g):
    given = dict(locals())
    ws = {k: given[k] for k in WEIGHTS}
    ms = {k: given["m_" + k] for k in WEIGHTS}
    vs = {k: given["v_" + k] for k in WEIGHTS}

    def shard(t):
        return t.reshape(t.shape[1:])

    xs, ps, target = shard(x), shard(shard(p)), shard(loss_target)
    T, D = xs.shape
    small = {k: ws[k] for k in SMALL}

    def local(group, k):
        t = shard(group[k])
        return jnp.swapaxes(t, 0, 1) if k in COLUMN_SHARDED else t

    shards = {k: local(ws, k) for k in BIG}

    cast = dict(zip(BIG, _cast_bf16([shards[k] for k in BIG])))
    buckets_a = _bucket_tiles(PATTERNS_A)
    buckets_b = _bucket_tiles(PATTERNS_B)
    bias_a, _ = _bias_build(small["rel_bias"], buckets_a, 0, "bias_build_a")
    bias_b, (w_gu1, w_down1) = _bias_build(
        small["rel_bias"], buckets_b, N_HEAD_GROUP, "bias_build_b",
        side=("relay_gather", [cast["ffn1_w_gu"], cast["ffn1_w_down"]]))
    w_down1 = w_down1.reshape(-1, D)
    a_cfg = dict(patterns=PATTERNS_A, qcol=Q_A_COL, kcol=K_A_COL, vcol=V_A_COL, shared_kv=True)
    b_cfg = dict(patterns=PATTERNS_B, qcol=Q_B_COL, kcol=K_B_COL, vcol=V_B_COL, shared_kv=False)

    (h1, f1, a1, gu1), (w_in_g, w_down2) = _ffn_fwd(
        xs, small["ffn1_pre_g"], small["ffn1_post_g"], w_gu1, w_down1, "ffn1_fwd",
        side=("relay_gather", [cast["w_in"], cast["ffn2_w_down"]]))
    w_in_full = w_in_g.reshape(D_IN, D)
    w_down2 = w_down2.reshape(-1, D)
    (z, a2), (w_out_g,) = _inproj_fwd(h1, small["attn_pre_g"], w_in_full, small["b_in"],
                                      side=("relay_gather", [cast["w_out"]]))
    w_out_full = w_out_g.reshape(-1, D)
    (mix_a, lse_a), (w_gate, w_proj) = _attn_fwd(
        z, bias_a, small["sinks"], name="attn_a_fwd", **a_cfg,
        side=("relay_gather", [cast["w_ple_gate"], cast["w_ple_proj"]]))
    w_gate = w_gate.reshape(-1, D)
    (mix_b, lse_b), (w_gu2,) = _attn_fwd(
        z, bias_b, None, name="attn_b_fwd", **b_cfg, side=("relay_gather", [cast["ffn2_w_gu"]]))
    (h3, f2, a3, gu2, att, h2, mix), _ = _ffn_fwd(
        h1, small["ffn2_pre_g"], small["ffn2_post_g"], w_gu2, w_down2, "ffn2_fwd",
        attn=(mix_a, mix_b, w_out_full, small["b_out"], small["attn_post_g"]))
    a4, dpre, de, dh3, loss, dg_ple_post, dg_ple_pre = _ple_fwd_bwd(
        h3, small["ple_pre_g"], w_gate, ps, w_proj, small["ple_post_g"], target)

    d_gate = _dw_rows(a4, dpre, "ple_dw_gate")
    d_proj = _ple_dw_proj(ps, de, N_DEV)
    landed = {}
    (dh2, df2, hh2, dgu2, dg_f2_post, dg_f2_pre), (landed["w_ple_gate"], landed["w_ple_proj"]) = _ffn_bwd(
        dh3, f2, small["ffn2_post_g"], h2, small["ffn2_pre_g"], gu2, w_gu2, w_down2, "ffn2_bwd",
        side=("exchange", [d_gate, d_proj]))
    d_gu2 = _dw_gu(a3, dgu2, "ffn2_dw_gu")
    d_down2 = _dw_down(hh2, df2, "ffn2_dw_down").reshape(N_DEV, -1, D)
    dmix_a, dmix_b, datt, dg_attn_post, db_out = _outproj_bwd(dh2, att, small["attn_post_g"], w_out_full)
    d_out = _dw_rows(mix, datt, "attn_dw_out")
    (dqa, dka, dva, ds_a, dsinks), _ = _attn_bwd(
        z, bias_a, small["sinks"], dmix_a, mix_a, lse_a, name="attn_a_bwd", **a_cfg)
    (dqb, dkb, dvb, ds_b), (landed["ffn2_w_gu"],) = _attn_bwd(
        z, bias_b, None, dmix_b, mix_b, lse_b, name="attn_b_bwd", **b_cfg, side=("exchange", [d_gu2]))
    (dh1, dz, db_in, dg_attn_pre), (landed["w_out"],) = _inproj_bwd(
        dqa, dka, dva, dqb, dkb, dvb, w_in_full, h1, small["attn_pre_g"], dh2, side=("exchange", [d_out]))
    cols = D_IN // 3
    d_in = _tn_matmul(
        dz, a2, pl.BlockSpec((DW_TILE, cols), lambda b, t: (t, b)), _tok(D),
        jax.ShapeDtypeStruct((D_IN, D), BF16), pl.BlockSpec((cols, D), lambda b, t: (b, 0)),
        3, T // DW_TILE, (cols, D), "attn_dw_in").reshape(N_DEV, D_IN // N_DEV, D)
    (grad_x, df1, hh1, dgu1, dg_f1_post, dg_f1_pre), (landed["w_in"], landed["ffn2_w_down"]) = _ffn_bwd(
        dh1, f1, small["ffn1_post_g"], xs, small["ffn1_pre_g"], gu1, w_gu1, w_down1, "ffn1_bwd",
        side=("exchange", [d_in, d_down2]))
    d_down1 = _dw_down(hh1, df1, "ffn1_dw_down").reshape(N_DEV, -1, D)
    d_gu1, (landed["ffn1_w_down"],) = _dw_gu(a1, dgu1, "ffn1_dw_gu", side=("exchange", [d_down1]))

    p_send, p_recv, blocks_thru, recv_thru, p_token = _pair_start(d_gu1)
    rb_a = _bias_grad(ds_a, buckets_a, "bias_grad_a", after=p_token)
    rb_b = _bias_grad(ds_b, buckets_b, "bias_grad_b", after=p_token).reshape(
        len(PATTERNS_B), N_HEAD_GROUP, NUM_BUCKETS)
    d_rel_bias = jnp.concatenate([rb_a.T, jnp.sum(rb_b, axis=0).T], axis=1)
    d_gu1_done, d_gu1_received = _pair_wait(p_send, p_recv, blocks_thru, recv_thru, d_rel_bias)
    small_grads = {"ffn1_pre_g": dg_f1_pre, "ffn1_post_g": dg_f1_post, "attn_pre_g": dg_attn_pre,
                   "attn_post_g": dg_attn_post, "ffn2_pre_g": dg_f2_pre, "ffn2_post_g": dg_f2_post,
                   "ple_pre_g": dg_ple_pre, "ple_post_g": dg_ple_post, "b_out": db_out, "b_in": db_in,
                   "sinks": dsinks, "rel_bias": d_rel_bias}
    d_gu1_pairs = _pair_add(d_gu1_done, d_gu1_received, "ffn1_dw_gu_pair_add")
    send_sems, recv_sems, pairs_thru, land_thru, token = _quad_start(d_gu1_pairs)
    small_gathered = _final_exchange(small_grads, loss)
    updates = {}
    for k in BIG:
        if k != "ffn1_w_gu":
            updates[k] = _sum_adamw(landed[k], shards[k], local(ms, k), local(vs, k),
                                    _adamw_rows(shards[k].shape[0]), k + "_adamw", after=token)
    landed["ffn1_w_gu"] = _quad_wait(send_sems, recv_sems, pairs_thru, land_thru, updates["ffn2_w_gu"][0])
    k = "ffn1_w_gu"
    updates[k] = _sum_adamw(landed[k], shards[k], local(ms, k), local(vs, k), _adamw_rows(shards[k].shape[0]),
                            k + "_adamw")
    result = {}
    for k in BIG:
        outs = updates[k]
        if k in COLUMN_SHARDED:
            outs = [jnp.swapaxes(o, 0, 1) for o in outs]
        result[k] = [o.reshape(ws[k].shape) for o in outs]
    small_res, loss_all = _small_adamw(
        small_gathered, small, {k: ms[k] for k in SMALL}, {k: vs[k] for k in SMALL})
    result.update(small_res)

    out = [loss_all[0, 0], grad_x.reshape(x.shape)]
    for i in range(4):
        out += [result[k][i] for k in WEIGHTS]
    return tuple(out)
```
